```python
import math
import jax, jax.numpy as jnp
from jax import lax
import numpy as np

D_MODEL = 1024
BATCH = 1
SEQ = 16384
DEPTH = 2
DEC_BATCH = 32
DEC_SEQ = 4
PAST_LEN = 16384
PAGE_SIZE = 128

N_EVEN = (DEPTH + 1) // 2
N_ODD = DEPTH // 2

NSA_HEADS = 8
NSA_KV_HEADS = 2
NSA_GROUP = NSA_HEADS // NSA_KV_HEADS
NSA_HD = 64
NSA_Q = NSA_HEADS * NSA_HD
NSA_KV = NSA_KV_HEADS * NSA_HD
CMP_BLK = 64
SEL_BLK = 64
TOPK_BLK = 16
WINDOW = 512
Q_BLK = 128
FORCE_BONUS = 2.0 * NSA_GROUP
NSA_COLS = NSA_Q + 6 * NSA_KV + 3 * NSA_HEADS

RWKV_HEADS = 8
RWKV_HD = 64
RWKV_W = RWKV_HEADS * RWKV_HD
LORA_W = 64
LORA_A = 64
LORA_G = 128
RWKV_COLS = 3 * RWKV_W + LORA_W + LORA_A + LORA_G
RWKV_GN_EPS = 64e-5
EVEN_IN = NSA_COLS + RWKV_COLS
EVEN_MIX = NSA_Q + RWKV_W

GDN_HEADS = 8
GDN_HD = 128
GDN_W = GDN_HEADS * GDN_HD
CONV_W = 4
GDN_CHUNK = 64
ODD_IN = 4 * GDN_W + 2 * GDN_HEADS

N_GROUPS = 4
EXP_PER_GROUP = 8
N_EXPERTS = N_GROUPS * EXP_PER_GROUP
D_EXPERT = 512
TOPK_INNER = 2

EPS = 1e-6
NEG = -1e30

kernel_name = 'hybrid_nsa_rwkv7_gdn_hmoe_step'


def rmsnorm(x, w):
    xf = x.astype(jnp.float32)
    y = xf * lax.rsqrt(jnp.mean(xf * xf, axis=-1, keepdims=True) + EPS)
    return (y * w.astype(jnp.float32)).astype(x.dtype)


def l2n(z):
    return z * lax.rsqrt(jnp.sum(z * z, axis=-1, keepdims=True) + EPS)


def adaln_params(c, w, b):
    m = (jax.nn.silu(c) @ w + b)[:, None, :]
    return jnp.split(m, 6, axis=-1)


def masked_softmax(s, mask):
    s = jnp.where(mask, s, NEG)
    m = jnp.max(s, axis=-1, keepdims=True)
    e = jnp.where(mask, jnp.exp(s - m), 0.0)
    return e / jnp.maximum(jnp.sum(e, axis=-1, keepdims=True), 1e-30)


def alibi_slopes():
    return jnp.asarray(2.0 ** (-8.0 * np.arange(1, NSA_HEADS + 1) / NSA_HEADS), dtype=jnp.float32)


def compress_blocks(kv, pos_wts, w_c):
    b, l = kv.shape[:2]
    blocks = kv.reshape(b, l // CMP_BLK, CMP_BLK, 2, NSA_KV_HEADS, NSA_HD)
    pooled = jnp.einsum('bnpckd,cp->bnckd', blocks, pos_wts)
    return jnp.einsum('bnckd,cde->bncke', pooled, w_c)


def nsa_core(q, pos_q, kvc, n_blk, gather_sel, kvw, pos_w, gates, slopes):
    b, tq = q.shape[:2]
    qg = q.reshape(b, tq, NSA_KV_HEADS, NSA_GROUP, NSA_HD) * (NSA_HD ** -0.5)
    sl = slopes.reshape(1, 1, NSA_KV_HEADS, NSA_GROUP, 1)
    nc = kvc.shape[1]
    c_end = jnp.arange(nc, dtype=jnp.int32) * CMP_BLK + (CMP_BLK - 1)
    dist_c = (pos_q[:, None] - c_end[None, :])[None, :, None, None, :]
    s_c = jnp.einsum('btkgd,bnkd->btkgn', qg, kvc[:, :, 0]).astype(jnp.float32) - sl * dist_c.astype(jnp.float32)
    p_c = masked_softmax(s_c, dist_c >= 0)
    o_c = jnp.einsum('btkgn,bnkd->btkgd', p_c.astype(q.dtype), kvc[:, :, 1])
    imp = jnp.pad(p_c.sum(axis=3), ((0, 0), (0, 0), (0, 0), (0, n_blk - nc)))
    blk = jnp.arange(n_blk, dtype=jnp.int32)[None, :]
    cur = (pos_q // SEL_BLK)[:, None]
    forced = (blk == cur) | (blk == cur - 1) | (blk == 0)
    valid = blk <= cur
    score = jnp.where(valid[None, :, None, :], imp + jnp.where(forced, FORCE_BONUS, 0.0)[None, :, None, :], -1.0)
    _, idx = lax.top_k(score, min(TOPK_BLK, n_blk))
    kvs = gather_sel(idx)
    n_sel = kvs.shape[3] * SEL_BLK
    kvs = kvs.reshape(b, tq, NSA_KV_HEADS, n_sel, 2, NSA_HD)
    pos_s = (idx[..., None] * SEL_BLK + jnp.arange(SEL_BLK, dtype=jnp.int32)).reshape(b, tq, NSA_KV_HEADS, n_sel)
    dist_s = (pos_q[None, :, None, None] - pos_s)[:, :, :, None, :]
    s_s = jnp.einsum('btkgd,btksd->btkgs', qg, kvs[..., 0, :]).astype(jnp.float32) - sl * dist_s.astype(jnp.float32)
    p_s = masked_softmax(s_s, dist_s >= 0)
    o_s = jnp.einsum('btkgs,btksd->btkgd', p_s.astype(q.dtype), kvs[..., 1, :])
    dist_w = pos_q[:, None] - pos_w[None, :]
    mask_w = ((dist_w >= 0) & (dist_w < WINDOW) & (pos_w[None, :] >= 0))[None, :, None, None, :]
    dist_w = dist_w[None, :, None, None, :].astype(jnp.float32)
    s_w = jnp.einsum('btkgd,bskd->btkgs', qg, kvw[:, :, 0]).astype(jnp.float32) - sl * dist_w
    p_w = masked_softmax(s_w, mask_w)
    o_w = jnp.einsum('btkgs,bskd->btkgd', p_w.astype(q.dtype), kvw[:, :, 1])
    g = gates.reshape(b, tq, NSA_KV_HEADS, NSA_GROUP, 3)
    o = g[..., 0:1] * o_c + g[..., 1:2] * o_s + g[..., 2:3] * o_w
    return o.reshape(b, tq, NSA_Q)


def nsa_prompt(q, kv_cmp, kv_sel, kv_win, gates, pos_wts, w_c, slopes):
    b, t = q.shape[:2]
    kvc = compress_blocks(kv_cmp, pos_wts, w_c)
    n_blk = t // SEL_BLK
    kvw_pad = jnp.pad(kv_win, ((0, 0), (WINDOW, 0), (0, 0), (0, 0), (0, 0)))
    bi = jnp.arange(b)[:, None, None, None, None]
    hi = jnp.arange(NSA_KV_HEADS)[None, None, :, None, None]
    offs = jnp.arange(SEL_BLK, dtype=jnp.int32)

    def gather_sel(idx):
        return kv_sel[bi, idx[..., None] * SEL_BLK + offs, :, hi]

    def block(i):
        s0 = i * Q_BLK
        pos_q = s0 + jnp.arange(Q_BLK, dtype=jnp.int32)
        pos_w = s0 - WINDOW + jnp.arange(WINDOW + Q_BLK, dtype=jnp.int32)
        qb = lax.dynamic_slice_in_dim(q, s0, Q_BLK, axis=1)
        gb = lax.dynamic_slice_in_dim(gates, s0, Q_BLK, axis=1)
        kwb = lax.dynamic_slice_in_dim(kvw_pad, s0, WINDOW + Q_BLK, axis=1)
        return nsa_core(qb, pos_q, kvc, n_blk, gather_sel, kwb, pos_w, gb, slopes)

    o = lax.map(block, jnp.arange(t // Q_BLK, dtype=jnp.int32))
    o = jnp.moveaxis(o, 0, 1).reshape(b, t, NSA_Q)
    return o, kv_win[:, -min(WINDOW, t):]


def nsa_sample(q, kv_cmp, kv_sel, kv_win, gates, pool_cmp, pool_sel, page_table, win_buf, pos_wts, w_c, slopes):
    b, t = q.shape[:2]
    past = page_table.shape[1] * PAGE_SIZE
    past_cmp = pool_cmp[page_table].reshape(b, past, 2, NSA_KV_HEADS, NSA_HD)
    kvc = compress_blocks(past_cmp, pos_wts, w_c)
    n_new_full = t // CMP_BLK
    if n_new_full > 0:
        kvc = jnp.concatenate([kvc, compress_blocks(kv_cmp[:, :n_new_full * CMP_BLK], pos_wts, w_c)], axis=1)
    n_past_blk = past // SEL_BLK
    n_tail_blk = -(-t // SEL_BLK)
    tail_sel = jnp.pad(kv_sel, ((0, 0), (0, n_tail_blk * SEL_BLK - t), (0, 0), (0, 0), (0, 0)))
    bi = jnp.arange(b)[:, None, None, None, None]
    hi = jnp.arange(NSA_KV_HEADS)[None, None, :, None, None]
    offs = jnp.arange(SEL_BLK, dtype=jnp.int32)

    def gather_sel(idx):
        start = jnp.minimum(idx, n_past_blk - 1) * SEL_BLK
        page = page_table[bi[..., 0], start // PAGE_SIZE]
        past_rows = pool_sel[page[..., None], (start % PAGE_SIZE)[..., None] + offs, :, hi]
        tail_rows = tail_sel[bi, jnp.clip(idx - n_past_blk, 0, n_tail_blk - 1)[..., None] * SEL_BLK + offs, :, hi]
        return jnp.where((idx < n_past_blk)[..., None, None, None], past_rows, tail_rows)

    kvw = jnp.concatenate([win_buf, kv_win], axis=1)
    wb = win_buf.shape[1]
    pos_q = past + jnp.arange(t, dtype=jnp.int32)
    pos_w = past - wb + jnp.arange(wb + t, dtype=jnp.int32)
    o = nsa_core(q, pos_q, kvc, n_past_blk + n_tail_blk, gather_sel, kvw, pos_w, gates, slopes)
    return o, kvw[:, -wb:]


def even_project(h, h_prev, w_in, mu):
    proj = jnp.concatenate([h_prev[:, None, :], h], axis=1) @ w_in
    nsa_cols = proj[:, 1:, :NSA_COLS]
    rw_cur = proj[:, 1:, NSA_COLS:]
    rw_prev = proj[:, :-1, NSA_COLS:]
    return nsa_cols, rw_cur + (rw_prev - rw_cur) * mu


def nsa_split(cols):
    b, t = cols.shape[:2]
    q = cols[..., :NSA_Q].reshape(b, t, NSA_HEADS, NSA_HD)
    kv = [cols[..., NSA_Q + j * 2 * NSA_KV: NSA_Q + (j + 1) * 2 * NSA_KV].reshape(b, t, 2, NSA_KV_HEADS, NSA_HD) for j in range(3)]
    gates = jax.nn.sigmoid(cols[..., NSA_Q + 6 * NSA_KV:]).reshape(b, t, NSA_HEADS, 3)
    return q, kv[0], kv[1], kv[2], gates


def rwkv7_scan(s0, r, w_log, k, v, kk, bb):
    def step(s, inp):
        r_t, wl_t, k_t, v_t, kk_t, b_t = inp
        sa = jnp.einsum('bhvk,bhk->bhv', s, kk_t)
        s = s * jnp.exp(wl_t)[:, :, None, :] - sa[..., None] * b_t[:, :, None, :] + v_t[..., None] * k_t[:, :, None, :]
        return s, jnp.einsum('bhvk,bhk->bhv', s, r_t)
    xs = tuple(jnp.moveaxis(z, 1, 0) for z in (r, w_log, k, v, kk, bb))
    s, ys = lax.scan(step, s0, xs)
    return s, jnp.moveaxis(ys, 0, 1)


def rwkv_mix(xr, s0, w0, w2, a0, a2, g2, kkw, kaw, rk, ln_w, ln_b):
    b, t = xr.shape[:2]
    f = xr.astype(jnp.float32)
    r, k, v, xw, xa, xg = jnp.split(f, [RWKV_W, 2 * RWKV_W, 3 * RWKV_W, 3 * RWKV_W + LORA_W, 3 * RWKV_W + LORA_W + LORA_A], axis=-1)
    w_log = -jnp.exp(-jax.nn.softplus(-(w0 + jnp.tanh(xw) @ w2)) - 0.5)
    a = jax.nn.sigmoid(a0 + xa @ a2)
    g = jax.nn.sigmoid(xg) @ g2
    hs = lambda z: z.reshape(b, t, RWKV_HEADS, RWKV_HD)
    kk = l2n(hs(k * kkw))
    k = k * (1.0 + (a - 1.0) * kaw)
    r, k, v, w_log, a = hs(r), hs(k), hs(v), hs(w_log), hs(a)
    s, y = rwkv7_scan(s0.astype(jnp.float32), r, w_log, k, v, kk, kk * a)
    mu = jnp.mean(y, axis=-1, keepdims=True)
    var = jnp.mean(jnp.square(y - mu), axis=-1, keepdims=True)
    y = ((y - mu) * lax.rsqrt(var + RWKV_GN_EPS)).reshape(b, t, RWKV_W) * ln_w + ln_b
    y = y + (jnp.sum(r * k * rk, axis=-1, keepdims=True) * v).reshape(b, t, RWKV_W)
    return (y * g).astype(xr.dtype), s


def gated_delta_chunked(s0, q, k, v, g, beta):
    b, t, h, dk = q.shape
    c = GDN_CHUNK
    n = -(-t // c)
    pad = n * c - t

    def prep(z):
        z = jnp.pad(z, [(0, 0), (0, pad)] + [(0, 0)] * (z.ndim - 2))
        z = z.reshape((b, n, c) + z.shape[2:])
        return jnp.moveaxis(jnp.moveaxis(z, 1, 0), 3, 2)

    q, k, v, g, beta = prep(q), prep(k), prep(v), prep(g), prep(beta)
    gc = jnp.cumsum(g, axis=-1)
    ar = jnp.arange(c)
    tril = ar[:, None] >= ar[None, :]
    strict = ar[:, None] > ar[None, :]
    decay = jnp.where(tril, jnp.exp(jnp.where(tril, gc[..., :, None] - gc[..., None, :], 0.0)), 0.0)
    kb = k * beta[..., None]
    vb = v * beta[..., None]
    m = jnp.where(strict, jnp.einsum('nbhid,nbhjd->nbhij', kb, k) * decay, 0.0)
    a_mat = m + jnp.eye(c, dtype=m.dtype)
    u = lax.linalg.triangular_solve(a_mat, vb, left_side=True, lower=True, unit_diagonal=True)
    w = lax.linalg.triangular_solve(a_mat, kb * jnp.exp(gc)[..., None], left_side=True, lower=True, unit_diagonal=True)
    qk = jnp.where(tril, jnp.einsum('nbhid,nbhjd->nbhij', q, k) * decay, 0.0)

    def step(s, inp):
        q_i, k_i, u_i, w_i, gc_i, qk_i = inp
        v_new = u_i - jnp.einsum('bhcd,bhde->bhce', w_i, s)
        o_i = jnp.einsum('bhcd,bhde->bhce', q_i * jnp.exp(gc_i)[..., None], s) + jnp.einsum('bhij,bhje->bhie', qk_i, v_new)
        g_last = gc_i[..., -1:]
        s = s * jnp.exp(g_last)[..., None] + jnp.einsum('bhcd,bhce->bhde', k_i * jnp.exp(g_last - gc_i)[..., None], v_new)
        return s, o_i

    s, o = lax.scan(step, s0, (q, k, u, w, gc, qk))
    o = o.transpose(1, 0, 3, 2, 4).reshape(b, n * c, h, v.shape[-1])[:, :t]
    return s, o


def gdn_mixer(h, conv_state, s0, w_in, w_out, conv_w, a_log, dt_bias, norm_w):
    b, t = h.shape[:2]
    proj = h @ w_in
    qkv = proj[..., :3 * GDN_W]
    z = proj[..., 3 * GDN_W:4 * GDN_W]
    bt = proj[..., 4 * GDN_W:4 * GDN_W + GDN_HEADS]
    al = proj[..., 4 * GDN_W + GDN_HEADS:]
    xpad = jnp.concatenate([conv_state.astype(qkv.dtype), qkv], axis=1)
    conv = xpad[:, 0:t] * conv_w[0]
    for j in range(1, CONV_W):
        conv = conv + xpad[:, j:j + t] * conv_w[j]
    conv = jax.nn.silu(conv).astype(jnp.float32)
    new_conv = xpad[:, -(CONV_W - 1):]
    q, k, v = jnp.split(conv.reshape(b, t, 3, GDN_HEADS, GDN_HD), 3, axis=2)
    q = l2n(q[:, :, 0]) * (GDN_HD ** -0.5)
    k = l2n(k[:, :, 0])
    v = v[:, :, 0]
    beta = jax.nn.sigmoid(bt.astype(jnp.float32))
    g = -jnp.exp(a_log.astype(jnp.float32)) * jax.nn.softplus(al.astype(jnp.float32) + dt_bias)
    s, o = gated_delta_chunked(s0.astype(jnp.float32), q, k, v, g, beta)
    o = o * lax.rsqrt(jnp.mean(o * o, axis=-1, keepdims=True) + EPS) * norm_w
    o = o * jax.nn.silu(z.astype(jnp.float32)).reshape(b, t, GDN_HEADS, GDN_HD)
    y = o.reshape(b, t, GDN_W).astype(h.dtype) @ w_out
    return y, s.astype(s0.dtype), new_conv


def hmoe(h, w_grp, b_grp, w_exp, b_exp, w1, w3, w2):
    b, t, d = h.shape
    x = h.reshape(b * t, d)
    gl = (x @ w_grp).astype(jnp.float32) + b_grp
    pg = jax.nn.softmax(gl, axis=-1)
    g_idx = jnp.argmax(gl, axis=-1)
    g_w = jnp.take_along_axis(pg, g_idx[:, None], axis=-1)
    el = ((x @ w_exp).astype(jnp.float32) + b_exp).reshape(b * t, N_GROUPS, EXP_PER_GROUP)
    el = jnp.take_along_axis(el, g_idx[:, None, None], axis=1)[:, 0]
    top_w, top_i = lax.top_k(jax.nn.softmax(el, axis=-1), TOPK_INNER)
    top_w = top_w / jnp.sum(top_w, axis=-1, keepdims=True) * g_w
    eid = g_idx[:, None] * EXP_PER_GROUP + top_i
    gate = jnp.sum(jax.nn.one_hot(eid, N_EXPERTS, dtype=jnp.float32) * top_w[..., None], axis=1).astype(x.dtype)
    y = jnp.zeros_like(x)
    for e in range(N_EXPERTS):
        he = jax.nn.silu(x @ w1[e]) * (x @ w3[e])
        y = y + gate[:, e:e + 1] * (he @ w2[e])
    return y.reshape(b, t, d)


def setup_inputs(seed: int = 0) -> dict:
    key = jax.random.key(seed)
    kit = iter(jax.random.split(key, 64))
    f32 = jnp.float32

    def nrm(shape, scale=1.0):
        return jax.random.normal(next(kit), shape, f32) * scale

    def unif(shape, lo, hi):
        return jax.random.uniform(next(kit), shape, f32, lo, hi)

    n_pages = PAST_LEN // PAGE_SIZE
    n_used = DEC_BATCH * n_pages
    n_phys = (5 * n_used + 3) // 4
    win_buf = min(WINDOW, PAST_LEN)
    page_table = jax.random.permutation(next(kit), n_phys)[:n_used].reshape(DEC_BATCH, n_pages).astype(jnp.int32)
    dt = jnp.exp(unif((N_ODD, GDN_HEADS), math.log(1e-3), math.log(1e-1)))
    return {
        'x_prompt': nrm((BATCH, SEQ, D_MODEL)),
        'x_sample': nrm((DEC_BATCH, DEC_SEQ, D_MODEL)),
        'c_prompt': nrm((BATCH, D_MODEL)),
        'c_sample': nrm((DEC_BATCH, D_MODEL)),
        'cache_nsa_cmp': nrm((N_EVEN, n_phys, PAGE_SIZE, 2, NSA_KV_HEADS, NSA_HD)),
        'cache_nsa_sel': nrm((N_EVEN, n_phys, PAGE_SIZE, 2, NSA_KV_HEADS, NSA_HD)),
        'page_table': page_table,
        'state_nsa_win': nrm((N_EVEN, DEC_BATCH, win_buf, 2, NSA_KV_HEADS, NSA_HD)),
        'state_rwkv': nrm((N_EVEN, DEC_BATCH, RWKV_HEADS, RWKV_HD, RWKV_HD), 0.3),
        'state_rwkv_shift': nrm((N_EVEN, DEC_BATCH, D_MODEL)),
        'state_gdn': nrm((N_ODD, DEC_BATCH, GDN_HEADS, GDN_HD, GDN_HD), 0.1),
        'state_gdn_conv': nrm((N_ODD, DEC_BATCH, CONV_W - 1, 3 * GDN_W)),
        'norm_mix': 1.0 + nrm((DEPTH, D_MODEL), 0.02),
        'norm_ffn': 1.0 + nrm((DEPTH, D_MODEL), 0.02),
        'norm_final': 1.0 + nrm((D_MODEL,), 0.02),
        'w_ada': nrm((DEPTH, D_MODEL, 6 * D_MODEL), 0.02),
        'b_ada': nrm((DEPTH, 6 * D_MODEL), 0.02),
        'even_w_in': nrm((N_EVEN, D_MODEL, EVEN_IN), D_MODEL ** -0.5),
        'even_w_out': nrm((N_EVEN, EVEN_MIX, D_MODEL), EVEN_MIX ** -0.5),
        'nsa_cmp_pos': (1.0 + nrm((N_EVEN, 2, CMP_BLK), 0.1)) / CMP_BLK,
        'nsa_cmp_w': nrm((N_EVEN, 2, NSA_HD, NSA_HD), NSA_HD ** -0.5),
        'rwkv_mu': unif((N_EVEN, RWKV_COLS), 0.0, 1.0),
        'rwkv_w0': unif((N_EVEN, RWKV_W), -6.0, 0.0),
        'rwkv_w2': nrm((N_EVEN, LORA_W, RWKV_W), 0.1),
        'rwkv_a0': nrm((N_EVEN, RWKV_W), 0.1),
        'rwkv_a2': nrm((N_EVEN, LORA_A, RWKV_W), 0.1),
        'rwkv_g2': nrm((N_EVEN, LORA_G, RWKV_W), LORA_G ** -0.5),
        'rwkv_kk': 0.85 + nrm((N_EVEN, RWKV_W), 0.1),
        'rwkv_ka': 1.0 + nrm((N_EVEN, RWKV_W), 0.1),
        'rwkv_rk': nrm((N_EVEN, RWKV_HEADS, RWKV_HD), 0.1),
        'rwkv_ln_w': 1.0 + nrm((N_EVEN, RWKV_W), 0.02),
        'rwkv_ln_b': nrm((N_EVEN, RWKV_W), 0.02),
        'odd_w_in': nrm((N_ODD, D_MODEL, ODD_IN), D_MODEL ** -0.5),
        'odd_w_out': nrm((N_ODD, GDN_W, D_MODEL), GDN_W ** -0.5),
        'gdn_conv_w': nrm((N_ODD, CONV_W, 3 * GDN_W), CONV_W ** -0.5),
        'gdn_a_log': jnp.log(unif((N_ODD, GDN_HEADS), 1.0, 16.0)),
        'gdn_dt_bias': jnp.log(jnp.expm1(dt)),
        'gdn_norm_w': 1.0 + nrm((N_ODD, GDN_HD), 0.02),
        'moe_w_grp': nrm((DEPTH, D_MODEL, N_GROUPS), D_MODEL ** -0.5),
        'moe_b_grp': nrm((DEPTH, N_GROUPS), 0.01),
        'moe_w_exp': nrm((DEPTH, D_MODEL, N_EXPERTS), D_MODEL ** -0.5),
        'moe_b_exp': nrm((DEPTH, N_EXPERTS), 0.01),
        'moe_w1': nrm((DEPTH, N_EXPERTS, D_MODEL, D_EXPERT), D_MODEL ** -0.5),
        'moe_w3': nrm((DEPTH, N_EXPERTS, D_MODEL, D_EXPERT), D_MODEL ** -0.5),
        'moe_w2': nrm((DEPTH, N_EXPERTS, D_EXPERT, D_MODEL), D_EXPERT ** -0.5),
    }


def reference(x_prompt, x_sample, c_prompt, c_sample, cache_nsa_cmp, cache_nsa_sel, page_table, state_nsa_win, state_rwkv, state_rwkv_shift, state_gdn, state_gdn_conv, norm_mix, norm_ffn, norm_final, w_ada, b_ada, even_w_in, even_w_out, nsa_cmp_pos, nsa_cmp_w, rwkv_mu, rwkv_w0, rwkv_w2, rwkv_a0, rwkv_a2, rwkv_g2, rwkv_kk, rwkv_ka, rwkv_rk, rwkv_ln_w, rwkv_ln_b, odd_w_in, odd_w_out, gdn_conv_w, gdn_a_log, gdn_dt_bias, gdn_norm_w, moe_w_grp, moe_b_grp, moe_w_exp, moe_b_exp, moe_w1, moe_w3, moe_w2):
    slopes = alibi_slopes()
    xp, xs = x_prompt, x_sample
    bp, bd = xp.shape[0], xs.shape[0]
    cmp_p, cmp_s, sel_p, sel_s, win_p, win_s = [], [], [], [], [], []
    rw_p, rw_s, sh_p, sh_s, gd_p, gd_s, cv_p, cv_s = [], [], [], [], [], [], [], []
    for i in range(DEPTH):
        sh1p, sc1p, gt1p, sh2p, sc2p, gt2p = adaln_params(c_prompt, w_ada[i], b_ada[i])
        sh1s, sc1s, gt1s, sh2s, sc2s, gt2s = adaln_params(c_sample, w_ada[i], b_ada[i])
        hp = rmsnorm(xp, norm_mix[i]) * (1.0 + sc1p) + sh1p
        hs = rmsnorm(xs, norm_mix[i]) * (1.0 + sc1s) + sh1s
        j = i // 2
        if i % 2 == 0:
            rw_par = (rwkv_w0[j], rwkv_w2[j], rwkv_a0[j], rwkv_a2[j], rwkv_g2[j], rwkv_kk[j], rwkv_ka[j], rwkv_rk[j], rwkv_ln_w[j], rwkv_ln_b[j])
            cols, rwx = even_project(hp, jnp.zeros((bp, D_MODEL), hp.dtype), even_w_in[j], rwkv_mu[j])
            q, kvc, kvsl, kvw, gts = nsa_split(cols)
            o_nsa, wnp = nsa_prompt(q, kvc, kvsl, kvw, gts, nsa_cmp_pos[j], nsa_cmp_w[j], slopes)
            o_rw, s_rw = rwkv_mix(rwx, jnp.zeros((bp, RWKV_HEADS, RWKV_HD, RWKV_HD), jnp.float32), *rw_par)
            yp = jnp.concatenate([o_nsa, o_rw], axis=-1) @ even_w_out[j]
            cmp_p.append(kvc); sel_p.append(kvsl); win_p.append(wnp); rw_p.append(s_rw); sh_p.append(hp[:, -1])
            cols, rwx = even_project(hs, state_rwkv_shift[j].astype(hs.dtype), even_w_in[j], rwkv_mu[j])
            q, kvc, kvsl, kvw, gts = nsa_split(cols)
            o_nsa, wns = nsa_sample(q, kvc, kvsl, kvw, gts, cache_nsa_cmp[j], cache_nsa_sel[j], page_table, state_nsa_win[j], nsa_cmp_pos[j], nsa_cmp_w[j], slopes)
            o_rw, s_rw = rwkv_mix(rwx, state_rwkv[j], *rw_par)
            ys = jnp.concatenate([o_nsa, o_rw], axis=-1) @ even_w_out[j]
            cmp_s.append(kvc); sel_s.append(kvsl); win_s.append(wns); rw_s.append(s_rw.astype(state_rwkv.dtype)); sh_s.append(hs[:, -1])
        else:
            gdn_par = (odd_w_in[j], odd_w_out[j], gdn_conv_w[j], gdn_a_log[j], gdn_dt_bias[j], gdn_norm_w[j])
            yp, s_g, cv = gdn_mixer(hp, jnp.zeros((bp, CONV_W - 1, 3 * GDN_W), hp.dtype), jnp.zeros((bp, GDN_HEADS, GDN_HD, GDN_HD), jnp.float32), *gdn_par)
            gd_p.append(s_g); cv_p.append(cv)
            ys, s_g, cv = gdn_mixer(hs, state_gdn_conv[j], state_gdn[j], *gdn_par)
            gd_s.append(s_g); cv_s.append(cv)
        xp = xp + gt1p * yp
        xs = xs + gt1s * ys
        moe_par = (moe_w_grp[i], moe_b_grp[i], moe_w_exp[i], moe_b_exp[i], moe_w1[i], moe_w3[i], moe_w2[i])
        xp = xp + gt2p * hmoe(rmsnorm(xp, norm_ffn[i]) * (1.0 + sc2p) + sh2p, *moe_par)
        xs = xs + gt2s * hmoe(rmsnorm(xs, norm_ffn[i]) * (1.0 + sc2s) + sh2s, *moe_par)
    y_prompt = rmsnorm(xp, norm_final)
    y_sample = rmsnorm(xs, norm_final)
    return (y_prompt, y_sample, jnp.stack(cmp_p), jnp.stack(cmp_s), jnp.stack(sel_p), jnp.stack(sel_s), jnp.stack(win_p), jnp.stack(win_s), jnp.stack(rw_p), jnp.stack(rw_s), jnp.stack(sh_p), jnp.stack(sh_s), jnp.stack(gd_p), jnp.stack(gd_s), jnp.stack(cv_p), jnp.stack(cv_s))
```

```python
import functools
import math

import jax
import jax.numpy as jnp
from jax import lax
from jax.experimental import pallas as pl
from jax.experimental.pallas import tpu as pltpu

F32 = jnp.float32
BF16 = jnp.bfloat16
HIGHEST = lax.Precision.HIGHEST

NSA_HEADS = 8
NSA_KV_HEADS = 2
NSA_GROUP = 4
NSA_HD = 64
CMP_BLK = 64
SEL_BLK = 64
TOPK_BLK = 16
WINDOW = 512
FORCE_BONUS = 2.0 * NSA_GROUP
RWKV_HEADS = 8
RWKV_HD = 64
RWKV_W = 512
RWKV_GN_EPS = 64e-5
GDN_HEADS = 8
GDN_HD = 128
GDN_W = 1024
CONV_W = 4
N_GROUPS = 4
EXP_PER_GROUP = 8
N_EXPERTS = 32
EPS = 1e-6
NEG = -1e30

LANE = 128
SPAD = 8
VMEM_LIMIT = 56 * 1024 * 1024

NN = (((1,), (0,)), ((), ()))
NT = (((1,), (1,)), ((), ()))
TN = (((0,), (0,)), ((), ()))

E_Q, E_KV, E_G, E_RW = 0, 512, 1280, 1408
E_COLS = 1408 + 1920
RW_COLS = 1920
O_COLS = 3072 + 1024 + 128


def _mm(a, b, dims=NN):
    return lax.dot_general(a.astype(BF16), b.astype(BF16), dims, preferred_element_type=F32)


def _mmh(a, b, dims=NN):
    return lax.dot_general(a.astype(F32), b.astype(F32), dims, precision=HIGHEST, preferred_element_type=F32)


def _sigmoid(x):
    return 1.0 / (1.0 + jnp.exp(-x))


def _silu(x):
    return x * _sigmoid(x)


def _softplus(x):
    return jnp.maximum(x, 0.0) + jnp.log(1.0 + jnp.exp(-jnp.abs(x)))


def _cparams(sem):
    return pltpu.CompilerParams(dimension_semantics=sem, vmem_limit_bytes=VMEM_LIMIT)


def _norm_mod(x, nw, sc, sh):
    y = x * lax.rsqrt(jnp.mean(x * x, axis=-1, keepdims=True) + EPS)
    return (y * nw) * (1.0 + sc) + sh


def _mod_spec(rows_mod, tm, d):
    if rows_mod == 1:
        return pl.BlockSpec((1, d), lambda i: (0, 0))
    return pl.BlockSpec((tm, d), lambda i: (i, 0))


def _adaln_body(c_ref, w_ref, b_ref, o_ref):
    o_ref[0] = _mmh(_silu(c_ref[...]), w_ref[0]) + b_ref[0]


def adaln(c_all, w_ada, b_ada):
    depth, d, n6 = w_ada.shape
    rows = c_all.shape[0]
    tn = 768
    return pl.pallas_call(
        _adaln_body,
        grid=(depth, n6 // tn),
        in_specs=[pl.BlockSpec((rows, d), lambda l, j: (0, 0)),
                  pl.BlockSpec((1, d, tn), lambda l, j: (l, 0, j)),
                  pl.BlockSpec((1, 1, tn), lambda l, j: (l, 0, j))],
        out_specs=pl.BlockSpec((1, rows, tn), lambda l, j: (l, 0, j)),
        out_shape=jax.ShapeDtypeStruct((depth, rows, n6), F32),
        compiler_params=_cparams(("arbitrary", "arbitrary")),
        name="adaln",
    )(c_all, w_ada, b_ada.reshape(depth, 1, n6))


def _even_proj_body(x_ref, nw_ref, sc_ref, sh_ref, w_ref,
                    kv_ref, qt_ref, gt_ref, ks_ref, vst_ref, kw_ref, vwt_ref, rw_ref, hl_ref):
    h = _norm_mod(x_ref[...], nw_ref[...], sc_ref[...], sh_ref[...])
    hl = hl_ref.shape[0]
    hl_ref[...] = h[h.shape[0] - hl:, :]
    hb = h.astype(BF16)
    q = _mm(hb, w_ref[:, E_Q:E_Q + 512]) * (NSA_HD ** -0.5)
    qt_ref[...] = q.T.astype(BF16)
    kv = _mm(hb, w_ref[:, E_KV:E_KV + 768])
    kv_ref[...] = kv
    ks_ref[...] = kv[:, 256:384].astype(BF16)
    vst_ref[...] = kv[:, 384:512].T.astype(BF16)
    kw_ref[...] = kv[:, 512:640].astype(BF16)
    vwt_ref[...] = kv[:, 640:768].T.astype(BF16)
    g = _sigmoid(_mm(hb, w_ref[:, E_G:E_G + 128]))
    gt_ref[...] = g.T
    rw_ref[...] = _mm(hb, w_ref[:, E_RW:E_RW + RW_COLS])


def even_proj(x, nw, sc, sh, w_packed, tm, hl_rows):
    n, d = x.shape
    rows_mod = sc.shape[0]
    row = lambda c: pl.BlockSpec((tm, c), lambda i: (i, 0))
    col = lambda r: pl.BlockSpec((r, tm), lambda i: (0, i))
    return pl.pallas_call(
        _even_proj_body,
        grid=(n // tm,),
        in_specs=[row(d), pl.BlockSpec((1, d), lambda i: (0, 0)),
                  _mod_spec(rows_mod, tm, d), _mod_spec(rows_mod, tm, d),
                  pl.BlockSpec((d, E_COLS), lambda i: (0, 0))],
        out_specs=[row(768), col(512), col(128), row(128), col(128), row(128), col(128), row(RW_COLS),
                   pl.BlockSpec((hl_rows, d), lambda i: (0, 0))],
        out_shape=[jax.ShapeDtypeStruct((n, 768), F32),
                   jax.ShapeDtypeStruct((512, n), BF16),
                   jax.ShapeDtypeStruct((128, n), F32),
                   jax.ShapeDtypeStruct((n, 128), BF16),
                   jax.ShapeDtypeStruct((128, n), BF16),
                   jax.ShapeDtypeStruct((n, 128), BF16),
                   jax.ShapeDtypeStruct((128, n), BF16),
                   jax.ShapeDtypeStruct((n, RW_COLS), F32),
                   jax.ShapeDtypeStruct((hl_rows, d), F32)],
        compiler_params=_cparams(("arbitrary",)),
        name="even_proj",
    )(x, nw, sc, sh, w_packed)


def _pack_even_w(w_in):
    d = w_in.shape[0]
    z = lambda c: jnp.zeros((d, c), w_in.dtype)
    nsa = 1304
    rw = w_in[:, nsa:]
    parts = [w_in[:, :1280], w_in[:, 1280:1304], z(104),
             rw[:, :1536], rw[:, 1536:1600], z(64), rw[:, 1600:1664], z(64), rw[:, 1664:1792]]
    return jnp.concatenate(parts, axis=1).astype(BF16)


def _pack_rw_vec(v):
    z = jnp.zeros((64,), v.dtype)
    return jnp.concatenate([v[:1536], v[1536:1600], z, v[1600:1664], z, v[1664:1792]])[None, :]


def _mm_body(x_ref, w_ref, o_ref):
    o_ref[...] = _mm(x_ref[...], w_ref[...])


def small_matmul(x, w):
    return pl.pallas_call(
        _mm_body,
        out_shape=jax.ShapeDtypeStruct((x.shape[0], w.shape[1]), F32),
        compiler_params=pltpu.CompilerParams(vmem_limit_bytes=VMEM_LIMIT),
        name="small_matmul",
    )(x, w)


def _compress_body(*refs, n_in, paged):
    if paged:
        refs = refs[1:]
    in_refs = refs[:n_in]
    wts_ref, wc_ref, o_ref = refs[n_in:]
    wts = wts_ref[...]
    pooled = []
    for r in in_refs:
        x = r[...]
        x = x.reshape(-1, x.shape[-1])
        nb = x.shape[0] // CMP_BLK
        pooled.append(jnp.sum(x.reshape(nb, CMP_BLK, x.shape[-1]) * wts[None], axis=1))
    p = pooled[0] if n_in == 1 else jnp.concatenate(pooled, axis=0)
    out = _mm(p, wc_ref[...])
    o_ref[...] = out.reshape(o_ref.shape)


def _cmp_weights(pos_wts, w_c):
    wts = jnp.repeat(pos_wts.T, 128, axis=1)
    eye2 = jnp.eye(2, dtype=w_c.dtype)
    blocks = [jnp.kron(eye2, w_c[c]) for c in range(2)]
    z = jnp.zeros((128, 128), w_c.dtype)
    wc = jnp.concatenate([jnp.concatenate([blocks[0], z], axis=1),
                          jnp.concatenate([z, blocks[1]], axis=1)], axis=0)
    return wts, wc


def compress_prompt(kv, wts, wc, tr):
    t = kv.shape[0]
    nb = tr // CMP_BLK
    return pl.pallas_call(
        functools.partial(_compress_body, n_in=1, paged=False),
        grid=(t // tr,),
        in_specs=[pl.BlockSpec((tr, 256), lambda i: (i, 0)),
                  pl.BlockSpec((CMP_BLK, 256), lambda i: (0, 0)),
                  pl.BlockSpec((256, 256), lambda i: (0, 0))],
        out_specs=pl.BlockSpec((nb, 256), lambda i: (i, 0)),
        out_shape=jax.ShapeDtypeStruct((t // CMP_BLK, 256), F32),
        compiler_params=_cparams(("arbitrary",)),
        name="compress_prompt",
    )(kv, wts, wc)


def compress_paged(pool, page_table, wts, wc, pages_per_step):
    b, n_pages = page_table.shape
    page = pool.shape[1]
    pps = pages_per_step
    nb = pps * page // CMP_BLK

    def page_spec(u):
        return pl.BlockSpec((1, page, 256), lambda bi, g, pt: (pt[bi, g * pps + u], 0, 0))

    grid_spec = pltpu.PrefetchScalarGridSpec(
        num_scalar_prefetch=1,
        grid=(b, n_pages // pps),
        in_specs=[page_spec(u) for u in range(pps)] + [
            pl.BlockSpec((CMP_BLK, 256), lambda bi, g, pt: (0, 0)),
            pl.BlockSpec((256, 256), lambda bi, g, pt: (0, 0))],
        out_specs=pl.BlockSpec((1, nb, 256), lambda bi, g, pt: (bi, g, 0)),
    )
    return pl.pallas_call(
        functools.partial(_compress_body, n_in=pps, paged=True),
        grid_spec=grid_spec,
        out_shape=jax.ShapeDtypeStruct((b, n_pages * page // CMP_BLK, 256), F32),
        compiler_params=_cparams(("arbitrary", "arbitrary")),
        name="compress_paged",
    )(page_table, *([pool] * pps), wts, wc)


def _gather_sel_body(pt_ref, *refs, pps, n_page_steps):
    page_refs = refs[:pps]
    tail_ref, ks_ref, vst_ref = refs[pps:]
    g = pl.program_id(1)

    @pl.when(g < n_page_steps)
    def _():
        x = jnp.concatenate([r[0] for r in page_refs], axis=0)
        ks_ref[0] = x[:, :128].astype(BF16)
        vst_ref[0] = x[:, 128:].T.astype(BF16)

    @pl.when(g >= n_page_steps)
    def _():
        x = tail_ref[0]
        ks_ref[0] = x[:, :128].astype(BF16)
        vst_ref[0] = x[:, 128:].T.astype(BF16)


def gather_sel(pool, page_table, tail, tk):
    b, n_pages = page_table.shape
    page = pool.shape[1]
    pps = tk // page
    n_page_steps = n_pages // pps
    nk = n_pages * page + tk

    def page_spec(u):
        return pl.BlockSpec((1, page, 256),
                            lambda bi, g, pt: (pt[bi, jnp.minimum(g * pps + u, n_pages - 1)], 0, 0))

    grid_spec = pltpu.PrefetchScalarGridSpec(
        num_scalar_prefetch=1,
        grid=(b, n_page_steps + 1),
        in_specs=[page_spec(u) for u in range(pps)] + [pl.BlockSpec((1, tk, 256), lambda bi, g, pt: (bi, 0, 0))],
        out_specs=[pl.BlockSpec((1, tk, 128), lambda bi, g, pt: (bi, g, 0)),
                   pl.BlockSpec((1, 128, tk), lambda bi, g, pt: (bi, 0, g))],
    )
    return pl.pallas_call(
        functools.partial(_gather_sel_body, pps=pps, n_page_steps=n_page_steps),
        grid_spec=grid_spec,
        out_shape=[jax.ShapeDtypeStruct((b, nk, 128), BF16), jax.ShapeDtypeStruct((b, 128, nk), BF16)],
        compiler_params=_cparams(("arbitrary", "arbitrary")),
        name="gather_sel",
    )(page_table, *([pool] * pps), tail)


def _nsa_body(qt_ref, g_ref, kvc_ref, kvct_ref, ks_ref, vst_ref, kw_ref, vwt_ref, o_ref, sel_ref, *,
              tq, tk, wk, nbc, nb, pos0_fn, nkv_fn, wstart_fn, wpos0_fn):
    i = pl.program_id(1)
    k = pl.program_id(2)
    w4 = NSA_GROUP * tq
    pos0 = pos0_fn(i)

    qb = qt_ref[0].astype(F32)
    qcat = jnp.concatenate([qb[g * 64:(g + 1) * 64] for g in range(NSA_GROUP)], axis=1)
    q2 = jnp.concatenate([qcat, qcat], axis=0)
    rsel = lax.broadcasted_iota(jnp.int32, (128, w4), 0) // 64 == k
    qe = jnp.where(rsel, q2, 0.0).astype(BF16)

    lane = lax.broadcasted_iota(jnp.int32, (1, w4), 1)
    pos_q = pos0 + lane % tq
    gidx = lane // tq
    base = jnp.where(k == 0, 0.5, 0.5 / 16.0)
    slope = base * jnp.where(gidx == 0, 1.0, jnp.where(gidx == 1, 0.5, jnp.where(gidx == 2, 0.25, 0.125)))

    def softmax_cols(s, mask):
        s = jnp.where(mask, s, NEG)
        m = jnp.max(s, axis=0, keepdims=True)
        e = jnp.where(mask, jnp.exp(s - m), 0.0)
        return e / jnp.maximum(jnp.sum(e, axis=0, keepdims=True), 1e-30)

    kc = kvc_ref[0][:, :128]
    c_end = lax.broadcasted_iota(jnp.int32, (nbc, 1), 0) * CMP_BLK + (CMP_BLK - 1)
    dist_c = pos_q - c_end
    s_c = _mm(kc, qe) - slope * dist_c.astype(F32)
    p_c = softmax_cols(s_c, dist_c >= 0)
    vct = kvct_ref[0, pl.ds(pl.multiple_of(128 + k * 64, 64), 64), :]
    o_c = _mm(vct, p_c)

    imp = p_c[:, 0:tq]
    for g in range(1, NSA_GROUP):
        imp = imp + p_c[:, g * tq:(g + 1) * tq]
    if nb > nbc:
        imp = jnp.concatenate([imp, jnp.zeros((nb - nbc, tq), F32)], axis=0)
    blk = lax.broadcasted_iota(jnp.int32, (nb, tq), 0)
    cur = (pos0 + lax.broadcasted_iota(jnp.int32, (1, tq), 1)) // SEL_BLK
    forced = (blk == cur) | (blk == cur - 1) | (blk == 0)
    score = jnp.where(blk <= cur, imp + jnp.where(forced, FORCE_BONUS, 0.0), -1.0)
    sel = jnp.zeros((nb, tq), F32)
    for _ in range(min(TOPK_BLK, nb)):
        m = jnp.max(score, axis=0, keepdims=True)
        first = jnp.min(jnp.where(score == m, blk, nb), axis=0, keepdims=True)
        hit = blk == first
        sel = jnp.where(hit, 1.0, sel)
        score = jnp.where(hit, -2.0, score)
    sel_ref[...] = sel

    blocks_per_tile = tk // SEL_BLK
    row_k = lax.broadcasted_iota(jnp.int32, (tk, 1), 0)

    def kv_step(j, carry):
        m_i, l_i, acc = carry
        off = pl.multiple_of(j * tk, tk)
        kj = ks_ref[0, pl.ds(off, tk), :]
        dist = pos_q - (off + row_k)
        s = _mm(kj, qe) - slope * dist.astype(F32)
        sel8 = sel_ref[pl.ds(pl.multiple_of(j * blocks_per_tile, blocks_per_tile), blocks_per_tile), :]
        sel8 = jnp.concatenate([sel8] * NSA_GROUP, axis=1)
        selm = jnp.broadcast_to(sel8[:, None, :], (blocks_per_tile, SEL_BLK, w4)).reshape(tk, w4)
        mask = (dist >= 0) & (selm > 0.5)
        s = jnp.where(mask, s, NEG)
        m_new = jnp.maximum(m_i, jnp.max(s, axis=0, keepdims=True))
        p = jnp.where(mask, jnp.exp(s - m_new), 0.0)
        alpha = jnp.exp(m_i - m_new)
        l_new = l_i * alpha + jnp.sum(p, axis=0, keepdims=True)
        vj = vst_ref[0, pl.ds(pl.multiple_of(k * 64, 64), 64), pl.ds(off, tk)]
        acc = acc * alpha + _mm(vj, p)
        return m_new, l_new, acc

    init = (jnp.full((1, w4), NEG, F32), jnp.zeros((1, w4), F32), jnp.zeros((64, w4), F32))
    _, l_s, acc_s = lax.fori_loop(0, nkv_fn(i), kv_step, init)
    o_s = acc_s / jnp.maximum(l_s, 1e-30)

    wstart = wstart_fn(i)
    if not isinstance(wstart, int):
        wstart = pl.multiple_of(wstart, 128)
    kwin = kw_ref[0, pl.ds(wstart, wk), :]
    dist_w = pos_q - (wpos0_fn(i) + lax.broadcasted_iota(jnp.int32, (wk, 1), 0))
    s_w = _mm(kwin, qe) - slope * dist_w.astype(F32)
    p_w = softmax_cols(s_w, (dist_w >= 0) & (dist_w < WINDOW))
    vwin = vwt_ref[0, pl.ds(pl.multiple_of(k * 64, 64), 64), pl.ds(wstart, wk)]
    o_w = _mm(vwin, p_w)

    gb = g_ref[0, 0]
    def gate(j):
        return jnp.concatenate([gb[g * 3 + j:g * 3 + j + 1, :] for g in range(NSA_GROUP)], axis=1)
    o_t = gate(0) * o_c + gate(1) * o_s + gate(2) * o_w
    o_ref[0] = jnp.concatenate([o_t[:, g * tq:(g + 1) * tq].T for g in range(NSA_GROUP)], axis=1)


def nsa_attention(qt, gates, kvc, kvct, ks, vst, kw, vwt, *, tq, tk, wk, pos0_fn, nkv_fn, wstart_fn, wpos0_fn):
    b, _, nq = qt.shape
    nbc = kvc.shape[1]
    nk = ks.shape[1]
    nw = kw.shape[1]
    nb = nk // SEL_BLK
    body = functools.partial(_nsa_body, tq=tq, tk=tk, wk=wk, nbc=nbc, nb=nb, pos0_fn=pos0_fn, nkv_fn=nkv_fn,
                             wstart_fn=wstart_fn, wpos0_fn=wpos0_fn)
    full = lambda s1, s2: pl.BlockSpec((1, s1, s2), lambda bi, i, k: (bi, 0, 0))
    return pl.pallas_call(
        body,
        grid=(b, nq // tq, NSA_KV_HEADS),
        in_specs=[pl.BlockSpec((1, 256, tq), lambda bi, i, k: (bi, k, i)),
                  pl.BlockSpec((1, 1, 16, tq), lambda bi, i, k: (bi, k, 0, i)),
                  full(nbc, 256), full(256, nbc), full(nk, 128), full(128, nk), full(nw, 128), full(128, nw)],
        out_specs=pl.BlockSpec((1, tq, 256), lambda bi, i, k: (bi, i, k)),
        out_shape=jax.ShapeDtypeStruct((b, nq, 512), F32),
        scratch_shapes=[pltpu.VMEM((nb, tq), F32)],
        compiler_params=_cparams(("arbitrary", "arbitrary", "arbitrary")),
        name="nsa_attention",
    )(qt, gates, kvc, kvct, ks, vst, kw, vwt)


def _tri_inverse(m_strict, c):
    eye = (lax.broadcasted_iota(jnp.int32, (c, c), 0) == lax.broadcasted_iota(jnp.int32, (c, c), 1)).astype(F32)
    n = -m_strict
    t = eye + n
    p = n
    steps = max(int(math.ceil(math.log2(c))) - 1, 0)
    for _ in range(steps):
        p = _mmh(p, p)
        t = t + _mmh(t, p)
    return t


def _rwkv_body(rw_ref, rw0_ref, s0_ref, mu_ref, vec_ref, w2_ref, a2_ref, g2_ref, seg_ref, rk_ref,
               o_ref, sfin_ref, buf_ref, s_ref, y_ref, *, c, valid, n_chunks):
    ci = pl.program_id(1)
    halo = 8

    @pl.when(ci == 0)
    def _():
        buf_ref[0:halo, :] = rw0_ref[0]
        s_ref[...] = s0_ref[0]

    cur = rw_ref[...]
    buf_ref[halo:halo + c, :] = cur
    prev = buf_ref[halo - 1:halo - 1 + c, :]
    xr = cur + (prev - cur) * mu_ref[...]
    buf_ref[0:halo, :] = cur[c - halo:, :]

    vec = vec_ref[...]
    w0, a0, kkw, kaw, ln_w, ln_b = (vec[r:r + 1, :] for r in range(6))
    r = xr[:, 0:512]
    kx = xr[:, 512:1024]
    v = xr[:, 1024:1536]
    xw = xr[:, 1536:1664]
    xa = xr[:, 1664:1792]
    xg = xr[:, 1792:1920]
    wl = -jnp.exp(-_softplus(-(w0 + _mm(jnp.tanh(xw), w2_ref[...]))) - 0.5)
    a = _sigmoid(a0 + _mm(xa, a2_ref[...]))
    gate = _mm(_sigmoid(xg), g2_ref[...])
    seg = seg_ref[...]
    zk = kx * kkw
    kk = zk * lax.rsqrt(_mmh(zk * zk, seg) + EPS)
    k2 = kx * (1.0 + (a - 1.0) * kaw)
    bonus = _mmh(r * k2 * rk_ref[...], seg) * v
    if valid < c:
        live = lax.broadcasted_iota(jnp.int32, (c, 1), 0) < valid
        wl = jnp.where(live, wl, 0.0)
        kk = jnp.where(live, kk, 0.0)
        k2 = jnp.where(live, k2, 0.0)
        v = jnp.where(live, v, 0.0)
        r = jnp.where(live, r, 0.0)
    bb = kk * a

    ri = lax.broadcasted_iota(jnp.int32, (c, c), 0)
    cj = lax.broadcasted_iota(jnp.int32, (c, c), 1)
    tril = ri >= cj
    strict = ri > cj
    cw = _mmh(tril.astype(F32), wl)
    ecw = jnp.exp(cw)
    einv = jnp.exp(-cw)
    p_c = ecw[c - 1:c, :]
    kt = kk * jnp.exp(cw - wl)
    bt = bb * einv
    ki = k2 * einv
    rt = r * ecw
    bd = bt * p_c
    kd = ki * p_c

    for h in range(RWKV_HEADS):
        sl = slice(h * RWKV_HD, (h + 1) * RWKV_HD)
        kt_h, bt_h, ki_h, rt_h, v_h = kt[:, sl], bt[:, sl], ki[:, sl], rt[:, sl], v[:, sl]
        l_m = jnp.where(strict, _mmh(kt_h, bt_h, NT), 0.0)
        m_kk = jnp.where(strict, _mmh(kt_h, ki_h, NT), 0.0)
        a_rb = jnp.where(tril, _mmh(rt_h, bt_h, NT), 0.0)
        a_rk = jnp.where(tril, _mmh(rt_h, ki_h, NT), 0.0)
        t_inv = _tri_inverse(l_m, c)
        w_h = _mmh(t_inv, kt_h)
        u_h = -_mmh(t_inv, _mmh(m_kk, v_h))
        s_h = s_ref[h]
        e_h = u_h - _mmh(w_h, s_h, NT)
        y_h = _mmh(rt_h, s_h, NT) + _mmh(a_rb, e_h) + _mmh(a_rk, v_h)
        s_ref[h] = s_h * p_c[:, sl] + _mmh(e_h, bd[:, sl], TN) + _mmh(v_h, kd[:, sl], TN)
        mu_h = jnp.mean(y_h, axis=-1, keepdims=True)
        d_h = y_h - mu_h
        var_h = jnp.mean(d_h * d_h, axis=-1, keepdims=True)
        y_ref[:, sl] = d_h * lax.rsqrt(var_h + RWKV_GN_EPS)

    o_ref[...] = (y_ref[...] * ln_w + ln_b + bonus) * gate

    @pl.when(ci == n_chunks - 1)
    def _():
        sfin_ref[0] = s_ref[...]


def rwkv_mix(rw, rw0, s0, mu, vec, w2, a2, g2, seg, rk, *, c, valid):
    b = s0.shape[0]
    rows = rw.shape[0]
    n_chunks = rows // (b * c)
    const = lambda s: pl.BlockSpec(s, lambda bi, ci: tuple(0 for _ in s))
    return pl.pallas_call(
        functools.partial(_rwkv_body, c=c, valid=valid, n_chunks=n_chunks),
        grid=(b, n_chunks),
        in_specs=[pl.BlockSpec((c, RW_COLS), lambda bi, ci: (bi * n_chunks + ci, 0)),
                  pl.BlockSpec((1, 8, RW_COLS), lambda bi, ci: (bi, 0, 0)),
                  pl.BlockSpec((1, RWKV_HEADS, 64, 64), lambda bi, ci: (bi, 0, 0, 0)),
                  const((1, RW_COLS)), const((8, 512)), const((128, 512)), const((128, 512)), const((128, 512)),
                  const((512, 512)), const((1, 512))],
        out_specs=[pl.BlockSpec((c, 512), lambda bi, ci: (bi * n_chunks + ci, 0)),
                   pl.BlockSpec((1, RWKV_HEADS, 64, 64), lambda bi, ci: (bi, 0, 0, 0))],
        out_shape=[jax.ShapeDtypeStruct((rows, 512), F32),
                   jax.ShapeDtypeStruct((b, RWKV_HEADS, 64, 64), F32)],
        scratch_shapes=[pltpu.VMEM((8 + c, RW_COLS), F32), pltpu.VMEM((RWKV_HEADS, 64, 64), F32),
                        pltpu.VMEM((c, 512), F32)],
        compiler_params=_cparams(("arbitrary", "arbitrary")),
        name="rwkv_mix",
    )(rw, rw0, s0, mu, vec, w2, a2, g2, seg, rk)


def _out_proj_body(*refs, n_in):
    a_refs = refs[:n_in]
    w_refs = refs[n_in:2 * n_in]
    x_ref, g_ref, o_ref = refs[2 * n_in:]
    y = _mm(a_refs[0][...], w_refs[0][...])
    for a_ref, w_ref in zip(a_refs[1:], w_refs[1:]):
        y = y + _mm(a_ref[...], w_ref[...])
    o_ref[...] = x_ref[...] + g_ref[...] * y


def out_proj(acts, weights, x, gate, tm):
    n, d = x.shape
    n_in = len(acts)
    return pl.pallas_call(
        functools.partial(_out_proj_body, n_in=n_in),
        grid=(n // tm,),
        in_specs=[pl.BlockSpec((tm, a.shape[1]), lambda i: (i, 0)) for a in acts]
        + [pl.BlockSpec(w.shape, lambda i: (0, 0)) for w in weights]
        + [pl.BlockSpec((tm, d), lambda i: (i, 0)), _mod_spec(gate.shape[0], tm, d)],
        out_specs=pl.BlockSpec((tm, d), lambda i: (i, 0)),
        out_shape=jax.ShapeDtypeStruct((n, d), F32),
        compiler_params=_cparams(("arbitrary",)),
        name="out_proj",
    )(*acts, *weights, x, gate)


def _odd_proj_body(x_ref, nw_ref, sc_ref, sh_ref, w_ref, qkv_ref, z_ref, ba_ref):
    hb = _norm_mod(x_ref[...], nw_ref[...], sc_ref[...], sh_ref[...]).astype(BF16)
    qkv_ref[...] = _mm(hb, w_ref[:, 0:3072])
    z_ref[...] = _mm(hb, w_ref[:, 3072:4096])
    ba_ref[...] = _mm(hb, w_ref[:, 4096:O_COLS])


def odd_proj(x, nw, sc, sh, w_packed, tm):
    n, d = x.shape
    rows_mod = sc.shape[0]
    row = lambda c: pl.BlockSpec((tm, c), lambda i: (i, 0))
    return pl.pallas_call(
        _odd_proj_body,
        grid=(n // tm,),
        in_specs=[row(d), pl.BlockSpec((1, d), lambda i: (0, 0)),
                  _mod_spec(rows_mod, tm, d), _mod_spec(rows_mod, tm, d),
                  pl.BlockSpec((d, O_COLS), lambda i: (0, 0))],
        out_specs=[row(3072), row(1024), row(128)],
        out_shape=[jax.ShapeDtypeStruct((n, 3072), F32), jax.ShapeDtypeStruct((n, 1024), F32),
                   jax.ShapeDtypeStruct((n, 128), F32)],
        compiler_params=_cparams(("arbitrary",)),
        name="odd_proj",
    )(x, nw, sc, sh, w_packed)


def _gdn_body(qkv_ref, z_ref, ba_ref, cs_ref, s0_ref, cw_ref, hp_ref, nw_ref,
              o_ref, sfin_ref, buf_ref, s_ref, *, c, valid, n_chunks):
    ci = pl.program_id(1)
    halo = 8

    @pl.when(ci == 0)
    def _():
        buf_ref[0:halo, :] = cs_ref[0]
        s_ref[...] = s0_ref[0]

    x = qkv_ref[...]
    buf_ref[halo:halo + c, :] = x
    cw = cw_ref[...]
    conv = buf_ref[halo - 3:halo - 3 + c, :] * cw[0:1, :]
    for j in range(1, CONV_W):
        conv = conv + buf_ref[halo - 3 + j:halo - 3 + j + c, :] * cw[j:j + 1, :]
    buf_ref[0:halo, :] = x[c - halo:, :]
    conv = _silu(conv)

    hp = hp_ref[...]
    ba = ba_ref[...]
    beta_f = _sigmoid(ba)
    g_f = hp[0:1, :] * _softplus(ba + hp[1:2, :])
    if valid < c:
        live = lax.broadcasted_iota(jnp.int32, (c, 1), 0) < valid
        beta_f = jnp.where(live, beta_f, 0.0)
        g_f = jnp.where(live, g_f, 0.0)
        conv = jnp.where(live, conv, 0.0)

    ri = lax.broadcasted_iota(jnp.int32, (c, c), 0)
    cj = lax.broadcasted_iota(jnp.int32, (c, c), 1)
    tril = ri >= cj
    strict = ri > cj
    gc = _mmh(tril.astype(F32), g_f)
    gct = gc.T
    z = z_ref[...]
    nw = nw_ref[...]

    for h in range(GDN_HEADS):
        sl = slice(h * GDN_HD, (h + 1) * GDN_HD)
        q_h = conv[:, sl]
        k_h = conv[:, GDN_W + h * GDN_HD:GDN_W + (h + 1) * GDN_HD]
        v_h = conv[:, 2 * GDN_W + h * GDN_HD:2 * GDN_W + (h + 1) * GDN_HD]
        q_h = q_h * lax.rsqrt(jnp.sum(q_h * q_h, axis=-1, keepdims=True) + EPS) * (GDN_HD ** -0.5)
        k_h = k_h * lax.rsqrt(jnp.sum(k_h * k_h, axis=-1, keepdims=True) + EPS)
        g_col = gc[:, 8 + h:9 + h]
        g_row = gct[8 + h:9 + h, :]
        b_col = beta_f[:, h:h + 1]
        decay = jnp.where(tril, jnp.exp(jnp.where(tril, g_col - g_row, 0.0)), 0.0)
        kb = k_h * b_col
        vb = v_h * b_col
        m_h = jnp.where(strict, _mmh(kb, k_h, NT) * decay, 0.0)
        t_inv = _tri_inverse(m_h, c)
        u_h = _mmh(t_inv, vb)
        w_h = _mmh(t_inv, kb * jnp.exp(g_col))
        qk = jnp.where(tril, _mm(q_h, k_h, NT) * decay, 0.0)
        s_h = s_ref[h]
        v_new = u_h - _mm(w_h, s_h)
        o_h = _mm(q_h * jnp.exp(g_col), s_h) + _mm(qk, v_new)
        g_last = g_col[c - 1:c, :]
        s_ref[h] = s_h * jnp.exp(g_last) + _mm(k_h * jnp.exp(g_last - g_col), v_new, TN)
        o_h = o_h * lax.rsqrt(jnp.mean(o_h * o_h, axis=-1, keepdims=True) + EPS) * nw
        o_ref[:, sl] = o_h * _silu(z[:, sl])

    @pl.when(ci == n_chunks - 1)
    def _():
        sfin_ref[0] = s_ref[...]


def gdn_mix(qkv, z, ba, cs, s0, conv_w8, hp, nw, *, c, valid):
    b = s0.shape[0]
    rows = qkv.shape[0]
    n_chunks = rows // (b * c)
    const = lambda s: pl.BlockSpec(s, lambda bi, ci: tuple(0 for _ in s))
    row = lambda w: pl.BlockSpec((c, w), lambda bi, ci: (bi * n_chunks + ci, 0))
    return pl.pallas_call(
        functools.partial(_gdn_body, c=c, valid=valid, n_chunks=n_chunks),
        grid=(b, n_chunks),
        in_specs=[row(3072), row(1024), row(128),
                  pl.BlockSpec((1, 8, 3072), lambda bi, ci: (bi, 0, 0)),
                  pl.BlockSpec((1, GDN_HEADS, 128, 128), lambda bi, ci: (bi, 0, 0, 0)),
                  const((8, 3072)), const((8, 128)), const((1, 128))],
        out_specs=[row(1024), pl.BlockSpec((1, GDN_HEADS, 128, 128), lambda bi, ci: (bi, 0, 0, 0))],
        out_shape=[jax.ShapeDtypeStruct((rows, 1024), F32),
                   jax.ShapeDtypeStruct((b, GDN_HEADS, 128, 128), F32)],
        scratch_shapes=[pltpu.VMEM((8 + c, 3072), F32), pltpu.VMEM((GDN_HEADS, 128, 128), F32)],
        compiler_params=_cparams(("arbitrary", "arbitrary")),
        name="gdn_mix",
    )(qkv, z, ba, cs, s0, conv_w8, hp, nw)


def _router_body(x_ref, nw_ref, sc_ref, sh_ref, wr_ref, br_ref, h_ref, gate_ref):
    h = _norm_mod(x_ref[...], nw_ref[...], sc_ref[...], sh_ref[...])
    h_ref[...] = h.astype(BF16)
    logits = _mmh(h, wr_ref[...]) + br_ref[...]
    tm = logits.shape[0]
    lane = lax.broadcasted_iota(jnp.int32, (tm, LANE), 1)
    is_grp = (lane >= N_EXPERTS) & (lane < N_EXPERTS + N_GROUPS)
    gl = jnp.where(is_grp, logits, NEG)
    gmax = jnp.max(gl, axis=-1, keepdims=True)
    g_idx = jnp.min(jnp.where(gl == gmax, lane, 4 * LANE), axis=-1, keepdims=True) - N_EXPERTS
    g_w = 1.0 / jnp.sum(jnp.where(is_grp, jnp.exp(gl - gmax), 0.0), axis=-1, keepdims=True)
    in_grp = (lane < N_EXPERTS) & (lane // EXP_PER_GROUP == g_idx)
    el = jnp.where(in_grp, logits, NEG)
    emax = jnp.max(el, axis=-1, keepdims=True)
    e = jnp.where(in_grp, jnp.exp(el - emax), 0.0)
    p = e / jnp.sum(e, axis=-1, keepdims=True)
    p1 = jnp.where(in_grp, p, -1.0)
    m1 = jnp.max(p1, axis=-1, keepdims=True)
    i1 = jnp.min(jnp.where(p1 == m1, lane, 4 * LANE), axis=-1, keepdims=True)
    p2 = jnp.where(lane == i1, -1.0, p1)
    m2 = jnp.max(p2, axis=-1, keepdims=True)
    i2 = jnp.min(jnp.where(p2 == m2, lane, 4 * LANE), axis=-1, keepdims=True)
    tot = m1 + m2
    gate_ref[...] = jnp.where(lane == i1, m1 / tot * g_w, jnp.where(lane == i2, m2 / tot * g_w, 0.0))


def moe_router(x, nw, sc, sh, w_r, b_r, tm):
    n, d = x.shape
    rows_mod = sc.shape[0]
    return pl.pallas_call(
        _router_body,
        grid=(n // tm,),
        in_specs=[pl.BlockSpec((tm, d), lambda i: (i, 0)), pl.BlockSpec((1, d), lambda i: (0, 0)),
                  _mod_spec(rows_mod, tm, d), _mod_spec(rows_mod, tm, d),
                  pl.BlockSpec((d, LANE), lambda i: (0, 0)), pl.BlockSpec((1, LANE), lambda i: (0, 0))],
        out_specs=[pl.BlockSpec((tm, d), lambda i: (i, 0)), pl.BlockSpec((tm, LANE), lambda i: (i, 0))],
        out_shape=[jax.ShapeDtypeStruct((n, d), BF16), jax.ShapeDtypeStruct((n, LANE), F32)],
        compiler_params=_cparams(("arbitrary",)),
        name="moe_router",
    )(x, nw, sc, sh, w_r, b_r)


def _moe_body(h_ref, gate_ref, w1_ref, w3_ref, w2_ref, x_ref, g2_ref, o_ref, acc_ref):
    e = pl.program_id(1)

    @pl.when(e == 0)
    def _():
        acc_ref[...] = jnp.zeros_like(acc_ref)

    hb = h_ref[...]
    he = _silu(_mm(hb, w1_ref[0])) * _mm(hb, w3_ref[0])
    y = _mm(he, w2_ref[0])
    gate = gate_ref[...]
    lane = lax.broadcasted_iota(jnp.int32, gate.shape, 1)
    ge = jnp.sum(jnp.where(lane == e, gate, 0.0), axis=-1, keepdims=True)
    acc_ref[...] += ge * y

    @pl.when(e == pl.num_programs(1) - 1)
    def _():
        o_ref[...] = x_ref[...] + g2_ref[...] * acc_ref[...]


def moe_ffn(h, gate, w1, w3, w2, x, g2, tm):
    n, d = x.shape
    ne, _, de = w1.shape
    return pl.pallas_call(
        _moe_body,
        grid=(n // tm, ne),
        in_specs=[pl.BlockSpec((tm, d), lambda i, e: (i, 0)), pl.BlockSpec((tm, LANE), lambda i, e: (i, 0)),
                  pl.BlockSpec((1, d, de), lambda i, e: (e, 0, 0)), pl.BlockSpec((1, d, de), lambda i, e: (e, 0, 0)),
                  pl.BlockSpec((1, de, d), lambda i, e: (e, 0, 0)),
                  pl.BlockSpec((tm, d), lambda i, e: (i, 0)),
                  pl.BlockSpec((1, d), lambda i, e: (0, 0)) if g2.shape[0] == 1
                  else pl.BlockSpec((tm, d), lambda i, e: (i, 0))],
        out_specs=pl.BlockSpec((tm, d), lambda i, e: (i, 0)),
        out_shape=jax.ShapeDtypeStruct((n, d), F32),
        scratch_shapes=[pltpu.VMEM((tm, d), F32)],
        compiler_params=_cparams(("arbitrary", "arbitrary")),
        name="moe_ffn",
    )(h, gate, w1, w3, w2, x, g2)


def _final_norm_body(x_ref, w_ref, o_ref):
    x = x_ref[...]
    o_ref[...] = x * lax.rsqrt(jnp.mean(x * x, axis=-1, keepdims=True) + EPS) * w_ref[...]


def final_norm(x, w, tm):
    n, d = x.shape
    return pl.pallas_call(
        _final_norm_body,
        grid=(n // tm,),
        in_specs=[pl.BlockSpec((tm, d), lambda i: (i, 0)), pl.BlockSpec((1, d), lambda i: (0, 0))],
        out_specs=pl.BlockSpec((tm, d), lambda i: (i, 0)),
        out_shape=jax.ShapeDtypeStruct((n, d), F32),
        compiler_params=_cparams(("arbitrary",)),
        name="final_norm",
    )(x, w)


def _row_tile(n, pref):
    t = min(pref, n)
    while n % t:
        t //= 2
    return t


def kernel(x_prompt, x_sample, c_prompt, c_sample, cache_nsa_cmp, cache_nsa_sel, page_table, state_nsa_win, state_rwkv, state_rwkv_shift, state_gdn, state_gdn_conv, norm_mix, norm_ffn, norm_final, w_ada, b_ada, even_w_in, even_w_out, nsa_cmp_pos, nsa_cmp_w, rwkv_mu, rwkv_w0, rwkv_w2, rwkv_a0, rwkv_a2, rwkv_g2, rwkv_kk, rwkv_ka, rwkv_rk, rwkv_ln_w, rwkv_ln_b, odd_w_in, odd_w_out, gdn_conv_w, gdn_a_log, gdn_dt_bias, gdn_norm_w, moe_w_grp, moe_b_grp, moe_w_exp, moe_b_exp, moe_w1, moe_w3, moe_w2):
    bp, t, d = x_prompt.shape
    bs, ts, _ = x_sample.shape
    assert bp == 1 and ts <= SPAD and ts < CMP_BLK
    depth = norm_mix.shape[0]
    n_pages, page = page_table.shape[1], cache_nsa_cmp.shape[2]
    past = n_pages * page
    wb = state_nsa_win.shape[2]
    ns = bs * SPAD
    tq, tk = 128, 512
    tm_p = _row_tile(t, 512)
    tm_s = ns

    rows_c = -(-(1 + bs) // 8) * 8
    c_all = jnp.concatenate([c_prompt, c_sample, jnp.zeros((rows_c - 1 - bs, d), F32)], axis=0)
    ada = adaln(c_all, w_ada, b_ada)

    def mods(i):
        mp = [ada[i, 0:1, j * d:(j + 1) * d] for j in range(6)]
        ms = [jnp.repeat(ada[i, 1:1 + bs, j * d:(j + 1) * d], SPAD, axis=0) for j in range(6)]
        return mp, ms

    xp = x_prompt[0]
    xs = jnp.pad(x_sample, ((0, 0), (0, SPAD - ts), (0, 0))).reshape(ns, d)

    def unpad(a):
        return a.reshape(bs, SPAD, -1)[:, :ts]

    outs = {k: [] for k in ("cmp_p", "cmp_s", "sel_p", "sel_s", "win_p", "win_s", "rw_p", "rw_s", "sh_p", "sh_s",
                            "gd_p", "gd_s", "cv_p", "cv_s")}

    for i in range(depth):
        (sh1p, sc1p, gt1p, sh2p, sc2p, gt2p), (sh1s, sc1s, gt1s, sh2s, sc2s, gt2s) = mods(i)
        j = i // 2
        nw = norm_mix[i][None, :]
        if i % 2 == 0:
            w_packed = _pack_even_w(even_w_in[j])
            mu = _pack_rw_vec(rwkv_mu[j])
            wts, wc = _cmp_weights(nsa_cmp_pos[j], nsa_cmp_w[j])
            vec = jnp.stack([rwkv_w0[j], rwkv_a0[j], rwkv_kk[j], rwkv_ka[j], rwkv_ln_w[j], rwkv_ln_b[j],
                             jnp.zeros_like(rwkv_w0[j]), jnp.zeros_like(rwkv_w0[j])])
            pad_lora = lambda w: jnp.concatenate([w, jnp.zeros((128 - w.shape[0], w.shape[1]), w.dtype)], axis=0)
            w2p, a2p, g2p = pad_lora(rwkv_w2[j]), pad_lora(rwkv_a2[j]), rwkv_g2[j]
            hid = jnp.arange(RWKV_W) // RWKV_HD
            seg = (hid[:, None] == hid[None, :]).astype(F32)
            rk = rwkv_rk[j].reshape(1, RWKV_W)
            wo_nsa, wo_rw = even_w_out[j][:512].astype(BF16), even_w_out[j][512:].astype(BF16)

            kv, qt, gt, ks, vst, kw, vwt, rw, hl = even_proj(xp, nw, sc1p, sh1p, w_packed, tm_p, 8)
            kvc = compress_prompt(kv, wts, wc, _row_tile(t, 512))
            gates = gt[:24].reshape(NSA_KV_HEADS, 12, t)
            gates = jnp.pad(gates, ((0, 0), (0, 4), (0, 0)))[None]
            o_nsa = nsa_attention(
                qt[None], gates, kvc[None], kvc.T[None], ks[None], vst[None], kw[None], vwt[None],
                tq=tq, tk=tk, wk=WINDOW + tq,
                pos0_fn=lambda qi: qi * tq,
                nkv_fn=lambda qi: (qi * tq + tq - 1) // tk + 1,
                wstart_fn=lambda qi: jnp.maximum(qi * tq - WINDOW, 0),
                wpos0_fn=lambda qi: jnp.maximum(qi * tq - WINDOW, 0))[0]
            o_rw, s_rw = rwkv_mix(rw, jnp.zeros((1, 8, RW_COLS), F32), jnp.zeros((1, RWKV_HEADS, 64, 64), F32),
                                  mu, vec, w2p, a2p, g2p, seg, rk, c=64, valid=64)
            xp = out_proj([o_nsa, o_rw], [wo_nsa, wo_rw], xp, gt1p, tm_p)
            outs["cmp_p"].append(kv[:, 0:256].reshape(1, t, 2, 2, 64))
            outs["sel_p"].append(kv[:, 256:512].reshape(1, t, 2, 2, 64))
            kvw_rows = kv[:, 512:768].reshape(1, t, 2, 2, 64)
            outs["win_p"].append(kvw_rows[:, -min(WINDOW, t):])
            outs["rw_p"].append(s_rw)
            outs["sh_p"].append(hl[-1:])

            kv, qt, gt, _, _, _, _, rw, hl = even_proj(xs, nw, sc1s, sh1s, w_packed, tm_s, ns)
            kv_new = unpad(kv)
            rw0 = small_matmul(jnp.pad(state_rwkv_shift[j], ((0, -bs % 8), (0, 0))), w_packed[:, E_RW:])[:bs]
            rw0 = jnp.pad(rw0[:, None, :], ((0, 0), (7, 0), (0, 0)))
            pool_cmp = cache_nsa_cmp[j].reshape(-1, page, 256)
            pool_sel = cache_nsa_sel[j].reshape(-1, page, 256)
            kvc_s = compress_paged(pool_cmp, page_table, wts, wc, 8 if n_pages % 8 == 0 else 1)
            tail = jnp.pad(kv_new[:, :, 256:512], ((0, 0), (0, tk - ts), (0, 0)))
            ks_s, vst_s = gather_sel(pool_sel, page_table, tail, tk)
            wbuf = state_nsa_win[j].reshape(bs, wb, 256)
            kvw_all = jnp.concatenate([wbuf, kv_new[:, :, 512:768]], axis=1)
            wk_s = -(-(wb + ts) // 128) * 128
            kvw_pad = jnp.pad(kvw_all, ((0, 0), (0, wk_s - wb - ts), (0, 0)))
            kw_s = kvw_pad[:, :, :128].astype(BF16)
            vwt_s = jnp.swapaxes(kvw_pad[:, :, 128:], 1, 2).astype(BF16)
            qt_s = jnp.pad(qt.reshape(512, bs, SPAD).transpose(1, 0, 2), ((0, 0), (0, 0), (0, tq - SPAD)))
            g_s = gt[:24].reshape(NSA_KV_HEADS, 12, bs, SPAD).transpose(2, 0, 1, 3)
            g_s = jnp.pad(g_s, ((0, 0), (0, 0), (0, 4), (0, tq - SPAD)))
            o_nsa = nsa_attention(
                qt_s, g_s, kvc_s, jnp.swapaxes(kvc_s, 1, 2), ks_s, vst_s, kw_s, vwt_s,
                tq=tq, tk=tk, wk=wk_s,
                pos0_fn=lambda qi: past,
                nkv_fn=lambda qi: (past + tk) // tk,
                wstart_fn=lambda qi: 0,
                wpos0_fn=lambda qi: past - wb)
            o_nsa = o_nsa[:, :SPAD].reshape(ns, 512)
            o_rw, s_rw = rwkv_mix(rw, rw0, state_rwkv[j], mu, vec, w2p, a2p, g2p, seg, rk, c=SPAD, valid=ts)
            xs = out_proj([o_nsa, o_rw], [wo_nsa, wo_rw], xs, gt1s, tm_s)
            outs["cmp_s"].append(kv_new[:, :, 0:256].reshape(bs, ts, 2, 2, 64))
            outs["sel_s"].append(kv_new[:, :, 256:512].reshape(bs, ts, 2, 2, 64))
            outs["win_s"].append(kvw_all[:, -wb:].reshape(bs, wb, 2, 2, 64))
            outs["rw_s"].append(s_rw)
            outs["sh_s"].append(hl.reshape(bs, SPAD, d)[:, ts - 1])
        else:
            w_in = odd_w_in[j]
            w_packed = jnp.concatenate([w_in, jnp.zeros((d, O_COLS - w_in.shape[1]), F32)], axis=1).astype(BF16)
            conv_w8 = jnp.pad(gdn_conv_w[j], ((0, 8 - CONV_W), (0, 0)))
            hp = jnp.zeros((8, 128), F32)
            hp = hp.at[0, 8:16].set(-jnp.exp(gdn_a_log[j])).at[1, 8:16].set(gdn_dt_bias[j])
            gnw = gdn_norm_w[j][None, :]
            wo = odd_w_out[j].astype(BF16)

            qkv, z, ba = odd_proj(xp, nw, sc1p, sh1p, w_packed, tm_p)
            o_g, s_g = gdn_mix(qkv, z, ba, jnp.zeros((1, 8, 3 * GDN_W), F32),
                               jnp.zeros((1, GDN_HEADS, GDN_HD, GDN_HD), F32), conv_w8, hp, gnw, c=64, valid=64)
            xp = out_proj([o_g], [wo], xp, gt1p, tm_p)
            outs["gd_p"].append(s_g)
            outs["cv_p"].append(qkv[None, -(CONV_W - 1):])

            qkv, z, ba = odd_proj(xs, nw, sc1s, sh1s, w_packed, tm_s)
            cs = jnp.pad(state_gdn_conv[j], ((0, 0), (8 - (CONV_W - 1), 0), (0, 0)))
            o_g, s_g = gdn_mix(qkv, z, ba, cs, state_gdn[j], conv_w8, hp, gnw, c=SPAD, valid=ts)
            xs = out_proj([o_g], [wo], xs, gt1s, tm_s)
            xpad = jnp.concatenate([state_gdn_conv[j], unpad(qkv)], axis=1)
            outs["gd_s"].append(s_g)
            outs["cv_s"].append(xpad[:, -(CONV_W - 1):])

        nwf = norm_ffn[i][None, :]
        w_r = jnp.concatenate([moe_w_exp[i], moe_w_grp[i], jnp.zeros((d, LANE - N_EXPERTS - N_GROUPS), F32)], axis=1)
        b_r = jnp.concatenate([moe_b_exp[i], moe_b_grp[i], jnp.zeros((LANE - N_EXPERTS - N_GROUPS,), F32)])[None, :]
        h2, gate = moe_router(xp, nwf, sc2p, sh2p, w_r, b_r, tm_p)
        xp = moe_ffn(h2, gate, moe_w1[i], moe_w3[i], moe_w2[i], xp, gt2p, _row_tile(t, 1024))
        h2, gate = moe_router(xs, nwf, sc2s, sh2s, w_r, b_r, tm_s)
        xs = moe_ffn(h2, gate, moe_w1[i], moe_w3[i], moe_w2[i], xs, gt2s, tm_s)

    nf = norm_final[None, :]
    y_prompt = final_norm(xp, nf, tm_p)[None]
    y_sample = unpad(final_norm(xs, nf, tm_s))
    st = lambda key: jnp.stack(outs[key])
    return (y_prompt, y_sample, st("cmp_p"), st("cmp_s"), st("sel_p"), st("sel_s"), st("win_p"), st("win_s"),
            st("rw_p"), st("rw_s"), st("sh_p"), st("sh_s"), st("gd_p"), st("gd_s"), st("cv_p"), st("cv_s"))
```

```python
import functools
import math

import jax
import jax.numpy as jnp
from jax import lax
from jax.experimental import pallas as pl
from jax.experimental.pallas import tpu as pltpu

F32 = jnp.float32
BF16 = jnp.bfloat16
HIGHEST = lax.Precision.HIGHEST

NSA_HEADS = 8
NSA_KV_HEADS = 2
NSA_GROUP = 4
NSA_HD = 64
CMP_BLK = 64
SEL_BLK = 64
TOPK_BLK = 16
WINDOW = 512
FORCE_BONUS = 2.0 * NSA_GROUP
RWKV_HEADS = 8
RWKV_HD = 64
RWKV_W = 512
RWKV_GN_EPS = 64e-5
GDN_HEADS = 8
GDN_HD = 128
GDN_W = 1024
CONV_W = 4
N_GROUPS = 4
EXP_PER_GROUP = 8
N_EXPERTS = 32
EPS = 1e-6
NEG = -1e30

LANE = 128
SPAD = 8
VMEM_LIMIT = 56 * 1024 * 1024

NN = (((1,), (0,)), ((), ()))
NT = (((1,), (1,)), ((), ()))
TN = (((0,), (0,)), ((), ()))

E_Q, E_KV, E_G, E_RW = 0, 512, 1280, 1408
E_COLS = 1408 + 1920
RW_COLS = 1920
O_COLS = 3072 + 1024 + 128


def _mm(a, b, dims=NN):
    return lax.dot_general(a.astype(BF16), b.astype(BF16), dims, preferred_element_type=F32)


def _mmh(a, b, dims=NN):
    return lax.dot_general(a.astype(F32), b.astype(F32), dims, precision=HIGHEST, preferred_element_type=F32)


def _split(a):
    hi = a.astype(BF16)
    return hi, (a - hi.astype(F32)).astype(BF16)


def _mm3(a, b, dims=NN):
    ah, al = _split(a)
    bh, bl = _split(b)
    d = lambda x, y: lax.dot_general(x, y, dims, preferred_element_type=F32)
    return d(ah, bh) + (d(ah, bl) + d(al, bh))


def _mm01(m01, x, dims=NN, left=True):
    h1 = x.astype(BF16)
    r1 = x - h1.astype(F32)
    h2 = r1.astype(BF16)
    h3 = (r1 - h2.astype(F32)).astype(BF16)
    m = m01.astype(BF16)
    if left:
        d = lambda y: lax.dot_general(m, y, dims, preferred_element_type=F32)
    else:
        d = lambda y: lax.dot_general(y, m, dims, preferred_element_type=F32)
    return d(h1) + (d(h2) + d(h3))


def _sigmoid(x):
    return 1.0 / (1.0 + jnp.exp(-x))


def _silu(x):
    return x * _sigmoid(x)


def _softplus(x):
    return jnp.maximum(x, 0.0) + jnp.log(1.0 + jnp.exp(-jnp.abs(x)))


def _cparams(sem):
    return pltpu.CompilerParams(dimension_semantics=sem, vmem_limit_bytes=VMEM_LIMIT)


def _norm_mod(x, nw, sc, sh):
    y = x * lax.rsqrt(jnp.mean(x * x, axis=-1, keepdims=True) + EPS)
    return (y * nw) * (1.0 + sc) + sh


def _mod_spec(rows_mod, tm, d):
    if rows_mod == 1:
        return pl.BlockSpec((1, d), lambda i: (0, 0))
    return pl.BlockSpec((tm, d), lambda i: (i, 0))


def _adaln_body(c_ref, w_ref, b_ref, o_ref):
    o_ref[0] = _mmh(_silu(c_ref[...]), w_ref[0]) + b_ref[0]


def adaln(c_all, w_ada, b_ada):
    depth, d, n6 = w_ada.shape
    rows = c_all.shape[0]
    tn = 768
    return pl.pallas_call(
        _adaln_body,
        grid=(depth, n6 // tn),
        in_specs=[pl.BlockSpec((rows, d), lambda l, j: (0, 0)),
                  pl.BlockSpec((1, d, tn), lambda l, j: (l, 0, j)),
                  pl.BlockSpec((1, 1, tn), lambda l, j: (l, 0, j))],
        out_specs=pl.BlockSpec((1, rows, tn), lambda l, j: (l, 0, j)),
        out_shape=jax.ShapeDtypeStruct((depth, rows, n6), F32),
        compiler_params=_cparams(("arbitrary", "arbitrary")),
        name="adaln",
    )(c_all, w_ada, b_ada.reshape(depth, 1, n6))


def _even_proj_body(x_ref, nw_ref, sc_ref, sh_ref, w_ref,
                    kv_ref, qt_ref, gt_ref, ks_ref, vst_ref, kw_ref, vwt_ref, rw_ref, hl_ref):
    h = _norm_mod(x_ref[...], nw_ref[...], sc_ref[...], sh_ref[...])
    hl = hl_ref.shape[0]
    hl_ref[...] = h[h.shape[0] - hl:, :]
    hb = h.astype(BF16)
    q = _mm(hb, w_ref[:, E_Q:E_Q + 512]) * (NSA_HD ** -0.5)
    qt_ref[...] = q.T.astype(BF16)
    kv = _mm(hb, w_ref[:, E_KV:E_KV + 768])
    kv_ref[...] = kv
    ks_ref[...] = kv[:, 256:384].astype(BF16)
    vst_ref[...] = kv[:, 384:512].T.astype(BF16)
    kw_ref[...] = kv[:, 512:640].astype(BF16)
    vwt_ref[...] = kv[:, 640:768].T.astype(BF16)
    g = _sigmoid(_mm(hb, w_ref[:, E_G:E_G + 128]))
    gt_ref[...] = g.T
    rw_ref[...] = _mm(hb, w_ref[:, E_RW:E_RW + RW_COLS])


def even_proj(x, nw, sc, sh, w_packed, tm, hl_rows):
    n, d = x.shape
    rows_mod = sc.shape[0]
    row = lambda c: pl.BlockSpec((tm, c), lambda i: (i, 0))
    col = lambda r: pl.BlockSpec((r, tm), lambda i: (0, i))
    return pl.pallas_call(
        _even_proj_body,
        grid=(n // tm,),
        in_specs=[row(d), pl.BlockSpec((1, d), lambda i: (0, 0)),
                  _mod_spec(rows_mod, tm, d), _mod_spec(rows_mod, tm, d),
                  pl.BlockSpec((d, E_COLS), lambda i: (0, 0))],
        out_specs=[row(768), col(512), col(128), row(128), col(128), row(128), col(128), row(RW_COLS),
                   pl.BlockSpec((hl_rows, d), lambda i: (0, 0))],
        out_shape=[jax.ShapeDtypeStruct((n, 768), F32),
                   jax.ShapeDtypeStruct((512, n), BF16),
                   jax.ShapeDtypeStruct((128, n), F32),
                   jax.ShapeDtypeStruct((n, 128), BF16),
                   jax.ShapeDtypeStruct((128, n), BF16),
                   jax.ShapeDtypeStruct((n, 128), BF16),
                   jax.ShapeDtypeStruct((128, n), BF16),
                   jax.ShapeDtypeStruct((n, RW_COLS), F32),
                   jax.ShapeDtypeStruct((hl_rows, d), F32)],
        compiler_params=_cparams(("arbitrary",)),
        name="even_proj",
    )(x, nw, sc, sh, w_packed)


def _pack_even_w(w_in):
    d = w_in.shape[0]
    z = lambda c: jnp.zeros((d, c), w_in.dtype)
    nsa = 1304
    rw = w_in[:, nsa:]
    parts = [w_in[:, :1280], w_in[:, 1280:1304], z(104),
             rw[:, :1536], rw[:, 1536:1600], z(64), rw[:, 1600:1664], z(64), rw[:, 1664:1792]]
    return jnp.concatenate(parts, axis=1).astype(BF16)


def _pack_rw_vec(v):
    z = jnp.zeros((64,), v.dtype)
    return jnp.concatenate([v[:1536], v[1536:1600], z, v[1600:1664], z, v[1664:1792]])[None, :]


def _mm_body(x_ref, w_ref, o_ref):
    o_ref[...] = _mm(x_ref[...], w_ref[...])


def small_matmul(x, w):
    return pl.pallas_call(
        _mm_body,
        out_shape=jax.ShapeDtypeStruct((x.shape[0], w.shape[1]), F32),
        compiler_params=pltpu.CompilerParams(vmem_limit_bytes=VMEM_LIMIT),
        name="small_matmul",
    )(x, w)


def _compress_body(*refs, n_in, paged):
    if paged:
        refs = refs[1:]
    in_refs = refs[:n_in]
    wts_ref, wc_ref, o_ref = refs[n_in:]
    wts = wts_ref[...]
    pooled = []
    for r in in_refs:
        x = r[...]
        x = x.reshape(-1, x.shape[-1])
        nb = x.shape[0] // CMP_BLK
        pooled.append(jnp.sum(x.reshape(nb, CMP_BLK, x.shape[-1]) * wts[None], axis=1))
    p = pooled[0] if n_in == 1 else jnp.concatenate(pooled, axis=0)
    out = _mm(p, wc_ref[...])
    o_ref[...] = out.reshape(o_ref.shape)


def _cmp_weights(pos_wts, w_c):
    wts = jnp.repeat(pos_wts.T, 128, axis=1)
    eye2 = jnp.eye(2, dtype=w_c.dtype)
    blocks = [jnp.kron(eye2, w_c[c]) for c in range(2)]
    z = jnp.zeros((128, 128), w_c.dtype)
    wc = jnp.concatenate([jnp.concatenate([blocks[0], z], axis=1),
                          jnp.concatenate([z, blocks[1]], axis=1)], axis=0)
    return wts, wc


def compress_prompt(kv, wts, wc, tr):
    t = kv.shape[0]
    nb = tr // CMP_BLK
    return pl.pallas_call(
        functools.partial(_compress_body, n_in=1, paged=False),
        grid=(t // tr,),
        in_specs=[pl.BlockSpec((tr, 256), lambda i: (i, 0)),
                  pl.BlockSpec((CMP_BLK, 256), lambda i: (0, 0)),
                  pl.BlockSpec((256, 256), lambda i: (0, 0))],
        out_specs=pl.BlockSpec((nb, 256), lambda i: (i, 0)),
        out_shape=jax.ShapeDtypeStruct((t // CMP_BLK, 256), F32),
        compiler_params=_cparams(("arbitrary",)),
        name="compress_prompt",
    )(kv, wts, wc)


def compress_paged(pool, page_table, wts, wc, pages_per_step):
    b, n_pages = page_table.shape
    page = pool.shape[1]
    pps = pages_per_step
    nb = pps * page // CMP_BLK

    def page_spec(u):
        return pl.BlockSpec((1, page, 256), lambda bi, g, pt: (pt[bi, g * pps + u], 0, 0))

    grid_spec = pltpu.PrefetchScalarGridSpec(
        num_scalar_prefetch=1,
        grid=(b, n_pages // pps),
        in_specs=[page_spec(u) for u in range(pps)] + [
            pl.BlockSpec((CMP_BLK, 256), lambda bi, g, pt: (0, 0)),
            pl.BlockSpec((256, 256), lambda bi, g, pt: (0, 0))],
        out_specs=pl.BlockSpec((1, nb, 256), lambda bi, g, pt: (bi, g, 0)),
    )
    return pl.pallas_call(
        functools.partial(_compress_body, n_in=pps, paged=True),
        grid_spec=grid_spec,
        out_shape=jax.ShapeDtypeStruct((b, n_pages * page // CMP_BLK, 256), F32),
        compiler_params=_cparams(("arbitrary", "arbitrary")),
        name="compress_paged",
    )(page_table, *([pool] * pps), wts, wc)


def _gather_sel_body(pt_ref, *refs, pps, n_page_steps):
    page_refs = refs[:pps]
    tail_ref, ks_ref, vst_ref = refs[pps:]
    g = pl.program_id(1)

    @pl.when(g < n_page_steps)
    def _():
        x = jnp.concatenate([r[0] for r in page_refs], axis=0)
        ks_ref[0] = x[:, :128].astype(BF16)
        vst_ref[0] = x[:, 128:].T.astype(BF16)

    @pl.when(g >= n_page_steps)
    def _():
        x = tail_ref[0]
        ks_ref[0] = x[:, :128].astype(BF16)
        vst_ref[0] = x[:, 128:].T.astype(BF16)


def gather_sel(pool, page_table, tail, tk):
    b, n_pages = page_table.shape
    page = pool.shape[1]
    pps = tk // page
    n_page_steps = n_pages // pps
    nk = n_pages * page + tk

    def page_spec(u):
        return pl.BlockSpec((1, page, 256),
                            lambda bi, g, pt: (pt[bi, jnp.minimum(g * pps + u, n_pages - 1)], 0, 0))

    grid_spec = pltpu.PrefetchScalarGridSpec(
        num_scalar_prefetch=1,
        grid=(b, n_page_steps + 1),
        in_specs=[page_spec(u) for u in range(pps)] + [pl.BlockSpec((1, tk, 256), lambda bi, g, pt: (bi, 0, 0))],
        out_specs=[pl.BlockSpec((1, tk, 128), lambda bi, g, pt: (bi, g, 0)),
                   pl.BlockSpec((1, 128, tk), lambda bi, g, pt: (bi, 0, g))],
    )
    return pl.pallas_call(
        functools.partial(_gather_sel_body, pps=pps, n_page_steps=n_page_steps),
        grid_spec=grid_spec,
        out_shape=[jax.ShapeDtypeStruct((b, nk, 128), BF16), jax.ShapeDtypeStruct((b, 128, nk), BF16)],
        compiler_params=_cparams(("arbitrary", "arbitrary")),
        name="gather_sel",
    )(page_table, *([pool] * pps), tail)


MASKED = -1e30
M_INIT = -1e29


def _nsa_query(qt_ref, k, tq):
    w4 = NSA_GROUP * tq
    qb = qt_ref[0].astype(F32)
    qcat = jnp.concatenate([qb[g * 64:(g + 1) * 64] for g in range(NSA_GROUP)], axis=1)
    q2 = jnp.concatenate([qcat, qcat], axis=0)
    row = lax.broadcasted_iota(jnp.int32, (128, w4), 0)
    qe = jnp.where(row // 64 == k, q2, 0.0)
    gidx = lax.broadcasted_iota(jnp.int32, (128, w4), 1) // tq
    base = jnp.where(k == 0, 0.5, 0.5 / 16.0)
    slope = base * jnp.where(gidx == 0, 1.0, jnp.where(gidx == 1, 0.5, jnp.where(gidx == 2, 0.25, 0.125)))
    mult = jnp.where(row == 0, 16.0, jnp.where(row == 1, 1.0, jnp.where(row == 2, 128.0,
                                                                         jnp.where(row == 3, 64.0, 0.0))))
    return jnp.concatenate([qe, slope * mult], axis=0).astype(BF16)


def _pos_features(rows, tile_rel):
    r = lax.broadcasted_iota(jnp.int32, (rows, LANE), 0)
    lane = lax.broadcasted_iota(jnp.int32, (rows, LANE), 1)
    ab = jnp.where(lane == 0, r // 16, jnp.where(lane == 1, r % 16, 0)).astype(F32)
    return jnp.where(lane == 2, tile_rel, ab).astype(BF16)


def _gate_rows(gb, j, tq):
    return jnp.concatenate([gb[g * 3 + j:g * 3 + j + 1, :] for g in range(NSA_GROUP)], axis=1)


def _nsa_select_body(qt_ref, g_ref, kvc_ref, kvct_ref, kw_ref, vwt_ref, part_ref, sel_ref, flag_ref, *,
                     tq, tk, wk, nbc, nb, pos0_fn, wstart_fn, wpos0_fn):
    i = pl.program_id(1)
    k = pl.program_id(2)
    w4 = NSA_GROUP * tq
    pos0 = pos0_fn(i)
    qa = _nsa_query(qt_ref, k, tq)
    pos_q = pos0 + lax.broadcasted_iota(jnp.int32, (1, w4), 1) % tq

    def softmax_cols(s, bad):
        s = jnp.where(bad, MASKED, s)
        m = jnp.maximum(jnp.max(s, axis=0, keepdims=True), M_INIT)
        e = jnp.exp(s - m)
        return e / jnp.maximum(jnp.sum(e, axis=0, keepdims=True), 1e-30)

    n_i = lax.broadcasted_iota(jnp.int32, (nbc, LANE), 0)
    lane_c = lax.broadcasted_iota(jnp.int32, (nbc, LANE), 1)
    feat_c = jnp.where(lane_c == 3, n_i - pos0 // CMP_BLK, 0).astype(F32).astype(BF16)
    kc = jnp.concatenate([kvc_ref[0][:, :128].astype(BF16), feat_c], axis=1)
    c_end = lax.broadcasted_iota(jnp.int32, (nbc, 1), 0) * CMP_BLK + (CMP_BLK - 1)
    p_c = softmax_cols(lax.dot_general(kc, qa, NN, preferred_element_type=F32), c_end > pos_q)
    vct = kvct_ref[0, pl.ds(pl.multiple_of(128 + k * 64, 64), 64), :]
    o_c = _mm(vct, p_c)

    imp = p_c[:, 0:tq]
    for g in range(1, NSA_GROUP):
        imp = imp + p_c[:, g * tq:(g + 1) * tq]
    if nb > nbc:
        imp = jnp.concatenate([imp, jnp.zeros((nb - nbc, tq), F32)], axis=0)
    blk = lax.broadcasted_iota(jnp.int32, (nb, tq), 0)
    cur = (pos0 + lax.broadcasted_iota(jnp.int32, (1, tq), 1)) // SEL_BLK
    forced = (blk == cur) | (blk == cur - 1) | (blk == 0)
    score = jnp.where(blk <= cur, imp + jnp.where(forced, FORCE_BONUS, 0.0), -1.0)
    sel = jnp.zeros((nb, tq), F32)
    for _ in range(min(TOPK_BLK, nb)):
        m = jnp.max(score, axis=0, keepdims=True)
        first = jnp.min(jnp.where(score == m, blk, nb), axis=0, keepdims=True)
        hit = blk == first
        sel = jnp.where(hit, 1.0, sel)
        score = jnp.where(hit, -2.0, score)
    sel_ref[0, 0] = sel
    bpt = tk // SEL_BLK
    any_row = jnp.max(sel, axis=1, keepdims=True)
    flag_ref[0, 0] = jnp.max(any_row.reshape(nb // bpt, bpt, 1), axis=1)

    wstart = wstart_fn(i)
    if not isinstance(wstart, int):
        wstart = pl.multiple_of(wstart, 128)
    wpos0 = wpos0_fn(i)
    tile_rel = jnp.asarray((wpos0 - pos0) // 128, F32)
    kw = jnp.concatenate([kw_ref[0, pl.ds(wstart, wk), :], _pos_features(wk, tile_rel)], axis=1)
    dist_w = pos_q - (wpos0 + lax.broadcasted_iota(jnp.int32, (wk, 1), 0))
    p_w = softmax_cols(lax.dot_general(kw, qa, NN, preferred_element_type=F32), (dist_w < 0) | (dist_w >= WINDOW))
    vwin = vwt_ref[0, pl.ds(pl.multiple_of(k * 64, 64), 64), pl.ds(wstart, wk)]
    o_w = _mm(vwin, p_w)

    gb = g_ref[0, 0]
    part_ref[0, 0] = _gate_rows(gb, 0, tq) * o_c + _gate_rows(gb, 2, tq) * o_w


def nsa_select(qt, gates, kvc, kvct, kw, vwt, *, nb, tq, tk, wk, pos0_fn, wstart_fn, wpos0_fn):
    b, _, nq = qt.shape
    nbc = kvc.shape[1]
    nw = kw.shape[1]
    nqt = nq // tq
    nt = nb * SEL_BLK // tk
    w4 = NSA_GROUP * tq
    assert nbc <= 256 and tk <= 512 and wk <= 1024
    body = functools.partial(_nsa_select_body, tq=tq, tk=tk, wk=wk, nbc=nbc, nb=nb, pos0_fn=pos0_fn,
                             wstart_fn=wstart_fn, wpos0_fn=wpos0_fn)
    full = lambda s1, s2: pl.BlockSpec((1, s1, s2), lambda bi, i, k: (bi, 0, 0))
    step = lambda s1, s2: pl.BlockSpec((1, 1, s1, s2), lambda bi, i, k: (bi, i * NSA_KV_HEADS + k, 0, 0))
    return pl.pallas_call(
        body,
        grid=(b, nqt, NSA_KV_HEADS),
        in_specs=[pl.BlockSpec((1, 256, tq), lambda bi, i, k: (bi, k, i)),
                  pl.BlockSpec((1, 1, 16, tq), lambda bi, i, k: (bi, k, 0, i)),
                  full(nbc, 256), full(256, nbc), full(nw, 128), full(128, nw)],
        out_specs=[step(64, w4), step(nb, tq), step(nt, 1)],
        out_shape=[jax.ShapeDtypeStruct((b, nqt * 2, 64, w4), F32),
                   jax.ShapeDtypeStruct((b, nqt * 2, nb, tq), F32),
                   jax.ShapeDtypeStruct((b, nqt * 2, nt, 1), F32)],
        compiler_params=_cparams(("arbitrary", "arbitrary", "arbitrary")),
        name="nsa_select",
    )(qt, gates, kvc, kvct, kw, vwt)


def _nsa_selected_body(list_ref, cnt_ref, qt_ref, g_ref, sel_ref, ks_ref, vst_ref, part_ref, o_ref, *,
                       tq, tk, nt, pos0_fn):
    bi = pl.program_id(0)
    i = pl.program_id(1)
    k = pl.program_id(2)
    step = (bi * pl.num_programs(1) + i) * NSA_KV_HEADS + k
    w4 = NSA_GROUP * tq
    pos0 = pos0_fn(i)
    qa = _nsa_query(qt_ref, k, tq)
    pos_q = pos0 + lax.broadcasted_iota(jnp.int32, (1, w4), 1) % tq
    bpt = tk // SEL_BLK
    row_k = lax.broadcasted_iota(jnp.int32, (tk, 1), 0)
    r = lax.broadcasted_iota(jnp.int32, (tk, LANE), 0)
    lane = lax.broadcasted_iota(jnp.int32, (tk, LANE), 1)
    feat_ab = jnp.where(lane == 0, r // 16, jnp.where(lane == 1, r % 16, 0)).astype(F32)

    def kv_step(jj, carry):
        m_i, l_i, acc = carry
        j = list_ref[step * nt + jj]
        off = pl.multiple_of(j * tk, tk)
        tile_rel = ((off - pos0) // 128).astype(F32)
        feat = jnp.where(lane == 2, tile_rel, feat_ab).astype(BF16)
        kj = jnp.concatenate([ks_ref[0, pl.ds(off, tk), :], feat], axis=1)
        s = lax.dot_general(kj, qa, NN, preferred_element_type=F32)
        selb = (sel_ref[0, 0, pl.ds(pl.multiple_of(j * bpt, bpt), bpt), :] - 1.0) * (-MASKED)
        selb = jnp.concatenate([selb] * NSA_GROUP, axis=1)
        s = s + jnp.broadcast_to(selb[:, None, :], (bpt, SEL_BLK, w4)).reshape(tk, w4)
        s = jnp.where(row_k > pos_q - off, MASKED, s)
        m_new = jnp.maximum(m_i, jnp.max(s, axis=0, keepdims=True))
        p = jnp.exp(s - m_new)
        alpha = jnp.exp(m_i - m_new)
        l_new = l_i * alpha + jnp.sum(p, axis=0, keepdims=True)
        vj = vst_ref[0, pl.ds(pl.multiple_of(k * 64, 64), 64), pl.ds(off, tk)]
        return m_new, l_new, acc * alpha + _mm(vj, p)

    init = (jnp.full((1, w4), M_INIT, F32), jnp.zeros((1, w4), F32), jnp.zeros((64, w4), F32))
    _, l_s, acc_s = lax.fori_loop(0, cnt_ref[step], kv_step, init)
    o_s = acc_s / jnp.maximum(l_s, 1e-30)
    o_t = part_ref[0, 0] + _gate_rows(g_ref[0, 0], 1, tq) * o_s
    o_ref[0] = jnp.concatenate([o_t[:, g * tq:(g + 1) * tq].T for g in range(NSA_GROUP)], axis=1)


def nsa_selected(tile_list, tile_cnt, qt, gates, sel, ks, vst, part, *, tq, tk, pos0_fn):
    b, _, nq = qt.shape
    nk = ks.shape[1]
    nb = sel.shape[2]
    nt = nk // tk
    w4 = NSA_GROUP * tq
    full = lambda s1, s2: pl.BlockSpec((1, s1, s2), lambda bi, i, k, *_: (bi, 0, 0))
    step = lambda s1, s2: pl.BlockSpec((1, 1, s1, s2), lambda bi, i, k, *_: (bi, i * NSA_KV_HEADS + k, 0, 0))
    grid_spec = pltpu.PrefetchScalarGridSpec(
        num_scalar_prefetch=2,
        grid=(b, nq // tq, NSA_KV_HEADS),
        in_specs=[pl.BlockSpec((1, 256, tq), lambda bi, i, k, *_: (bi, k, i)),
                  pl.BlockSpec((1, 1, 16, tq), lambda bi, i, k, *_: (bi, k, 0, i)),
                  step(nb, tq), full(nk, 128), full(128, nk), step(64, w4)],
        out_specs=pl.BlockSpec((1, tq, 256), lambda bi, i, k, *_: (bi, i, k)),
    )
    return pl.pallas_call(
        functools.partial(_nsa_selected_body, tq=tq, tk=tk, nt=nt, pos0_fn=pos0_fn),
        grid_spec=grid_spec,
        out_shape=jax.ShapeDtypeStruct((b, nq, 512), F32),
        compiler_params=_cparams(("arbitrary", "arbitrary", "arbitrary")),
        name="nsa_selected",
    )(tile_list, tile_cnt, qt, gates, sel, ks, vst, part)


def nsa_attention(qt, gates, kvc, kvct, ks, vst, kw, vwt, *, tq, tk, wk, pos0_fn, wstart_fn, wpos0_fn):
    nb = ks.shape[1] // SEL_BLK
    part, sel, flags = nsa_select(qt, gates, kvc, kvct, kw, vwt, nb=nb, tq=tq, tk=tk, wk=wk, pos0_fn=pos0_fn,
                                  wstart_fn=wstart_fn, wpos0_fn=wpos0_fn)
    active = flags[..., 0] > 0.5
    order = jnp.argsort(jnp.where(active, 0, 1), axis=-1, stable=True).astype(jnp.int32)
    cnt = jnp.sum(active, axis=-1).astype(jnp.int32)
    return nsa_selected(order.reshape(-1), cnt.reshape(-1), qt, gates, sel, ks, vst, part,
                        tq=tq, tk=tk, pos0_fn=pos0_fn)


def _tri_inverse(m_strict, c):
    eye = (lax.broadcasted_iota(jnp.int32, (c, c), 0) == lax.broadcasted_iota(jnp.int32, (c, c), 1)).astype(F32)
    n = -m_strict
    t = eye + n
    p = n
    steps = max(int(math.ceil(math.log2(c))) - 1, 0)
    d = lambda x, y: lax.dot_general(x, y, NN, preferred_element_type=F32)
    for _ in range(steps):
        ph, pl_ = _split(p)
        p = d(ph, ph) + (d(ph, pl_) + d(pl_, ph))
        ph, pl_ = _split(p)
        th, tl = _split(t)
        t = t + (d(th, ph) + (d(th, pl_) + d(tl, ph)))
    return t


def _rwkv_body(rw_ref, rw0_ref, s0_ref, mu_ref, vec_ref, w2_ref, a2_ref, g2_ref, seg_ref, rk_ref,
               o_ref, sfin_ref, buf_ref, s_ref, y_ref, *, c, valid, n_chunks):
    ci = pl.program_id(1)
    halo = 8

    @pl.when(ci == 0)
    def _():
        buf_ref[0:halo, :] = rw0_ref[0]
        s_ref[...] = s0_ref[0]

    cur = rw_ref[...]
    buf_ref[halo:halo + c, :] = cur
    prev = buf_ref[halo - 1:halo - 1 + c, :]
    xr = cur + (prev - cur) * mu_ref[...]
    buf_ref[0:halo, :] = cur[c - halo:, :]

    vec = vec_ref[...]
    w0, a0, kkw, kaw, ln_w, ln_b = (vec[r:r + 1, :] for r in range(6))
    r = xr[:, 0:512]
    kx = xr[:, 512:1024]
    v = xr[:, 1024:1536]
    xw = xr[:, 1536:1664]
    xa = xr[:, 1664:1792]
    xg = xr[:, 1792:1920]
    wl = -jnp.exp(-_softplus(-(w0 + _mm(jnp.tanh(xw), w2_ref[...]))) - 0.5)
    a = _sigmoid(a0 + _mm(xa, a2_ref[...]))
    gate = _mm(_sigmoid(xg), g2_ref[...])
    seg = seg_ref[...]
    zk = kx * kkw
    kk = zk * lax.rsqrt(_mm01(seg, zk * zk, left=False) + EPS)
    k2 = kx * (1.0 + (a - 1.0) * kaw)
    bonus = _mm01(seg, r * k2 * rk_ref[...], left=False) * v
    if valid < c:
        live = lax.broadcasted_iota(jnp.int32, (c, 1), 0) < valid
        wl = jnp.where(live, wl, 0.0)
        kk = jnp.where(live, kk, 0.0)
        k2 = jnp.where(live, k2, 0.0)
        v = jnp.where(live, v, 0.0)
        r = jnp.where(live, r, 0.0)
    bb = kk * a

    ri = lax.broadcasted_iota(jnp.int32, (c, c), 0)
    cj = lax.broadcasted_iota(jnp.int32, (c, c), 1)
    tril = ri >= cj
    strict = ri > cj
    cw = _mm01(tril, wl)
    ecw = jnp.exp(cw)
    einv = jnp.exp(-cw)
    p_c = ecw[c - 1:c, :]
    kt = kk * jnp.exp(cw - wl)
    bt = bb * einv
    ki = k2 * einv
    rt = r * ecw
    bd = bt * p_c
    kd = ki * p_c

    for h in range(RWKV_HEADS):
        sl = slice(h * RWKV_HD, (h + 1) * RWKV_HD)
        kt_h, bt_h, ki_h, rt_h, v_h = kt[:, sl], bt[:, sl], ki[:, sl], rt[:, sl], v[:, sl]
        l_m = jnp.where(strict, _mm3(kt_h, bt_h, NT), 0.0)
        m_kk = jnp.where(strict, _mm(kt_h, ki_h, NT), 0.0)
        a_rb = jnp.where(tril, _mm(rt_h, bt_h, NT), 0.0)
        a_rk = jnp.where(tril, _mm(rt_h, ki_h, NT), 0.0)
        t_inv = _tri_inverse(l_m, c)
        w_h = _mm3(t_inv, kt_h)
        u_h = -_mm3(t_inv, _mm(m_kk, v_h))
        s_h = s_ref[h]
        e_h = u_h - _mm(w_h, s_h, NT)
        y_h = _mm(rt_h, s_h, NT) + _mm(a_rb, e_h) + _mm(a_rk, v_h)
        s_ref[h] = s_h * p_c[:, sl] + _mm(e_h, bd[:, sl], TN) + _mm(v_h, kd[:, sl], TN)
        mu_h = jnp.mean(y_h, axis=-1, keepdims=True)
        d_h = y_h - mu_h
        var_h = jnp.mean(d_h * d_h, axis=-1, keepdims=True)
        y_ref[:, sl] = d_h * lax.rsqrt(var_h + RWKV_GN_EPS)

    o_ref[...] = (y_ref[...] * ln_w + ln_b + bonus) * gate

    @pl.when(ci == n_chunks - 1)
    def _():
        sfin_ref[0] = s_ref[...]


def rwkv_mix(rw, rw0, s0, mu, vec, w2, a2, g2, seg, rk, *, c, valid):
    b = s0.shape[0]
    rows = rw.shape[0]
    n_chunks = rows // (b * c)
    const = lambda s: pl.BlockSpec(s, lambda bi, ci: tuple(0 for _ in s))
    return pl.pallas_call(
        functools.partial(_rwkv_body, c=c, valid=valid, n_chunks=n_chunks),
        grid=(b, n_chunks),
        in_specs=[pl.BlockSpec((c, RW_COLS), lambda bi, ci: (bi * n_chunks + ci, 0)),
                  pl.BlockSpec((1, 8, RW_COLS), lambda bi, ci: (bi, 0, 0)),
                  pl.BlockSpec((1, RWKV_HEADS, 64, 64), lambda bi, ci: (bi, 0, 0, 0)),
                  const((1, RW_COLS)), const((8, 512)), const((128, 512)), const((128, 512)), const((128, 512)),
                  const((512, 512)), const((1, 512))],
        out_specs=[pl.BlockSpec((c, 512), lambda bi, ci: (bi * n_chunks + ci, 0)),
                   pl.BlockSpec((1, RWKV_HEADS, 64, 64), lambda bi, ci: (bi, 0, 0, 0))],
        out_shape=[jax.ShapeDtypeStruct((rows, 512), F32),
                   jax.ShapeDtypeStruct((b, RWKV_HEADS, 64, 64), F32)],
        scratch_shapes=[pltpu.VMEM((8 + c, RW_COLS), F32), pltpu.VMEM((RWKV_HEADS, 64, 64), F32),
                        pltpu.VMEM((c, 512), F32)],
        compiler_params=_cparams(("arbitrary", "arbitrary")),
        name="rwkv_mix",
    )(rw, rw0, s0, mu, vec, w2, a2, g2, seg, rk)


def _out_proj_body(*refs, n_in):
    a_refs = refs[:n_in]
    w_refs = refs[n_in:2 * n_in]
    x_ref, g_ref, o_ref = refs[2 * n_in:]
    y = _mm(a_refs[0][...], w_refs[0][...])
    for a_ref, w_ref in zip(a_refs[1:], w_refs[1:]):
        y = y + _mm(a_ref[...], w_ref[...])
    o_ref[...] = x_ref[...] + g_ref[...] * y


def out_proj(acts, weights, x, gate, tm):
    n, d = x.shape
    n_in = len(acts)
    return pl.pallas_call(
        functools.partial(_out_proj_body, n_in=n_in),
        grid=(n // tm,),
        in_specs=[pl.BlockSpec((tm, a.shape[1]), lambda i: (i, 0)) for a in acts]
        + [pl.BlockSpec(w.shape, lambda i: (0, 0)) for w in weights]
        + [pl.BlockSpec((tm, d), lambda i: (i, 0)), _mod_spec(gate.shape[0], tm, d)],
        out_specs=pl.BlockSpec((tm, d), lambda i: (i, 0)),
        out_shape=jax.ShapeDtypeStruct((n, d), F32),
        compiler_params=_cparams(("arbitrary",)),
        name="out_proj",
    )(*acts, *weights, x, gate)


def _odd_proj_body(x_ref, nw_ref, sc_ref, sh_ref, w_ref, qkv_ref, z_ref, ba_ref):
    hb = _norm_mod(x_ref[...], nw_ref[...], sc_ref[...], sh_ref[...]).astype(BF16)
    qkv_ref[...] = _mm(hb, w_ref[:, 0:3072])
    z_ref[...] = _mm(hb, w_ref[:, 3072:4096])
    ba_ref[...] = _mm(hb, w_ref[:, 4096:O_COLS])


def odd_proj(x, nw, sc, sh, w_packed, tm):
    n, d = x.shape
    rows_mod = sc.shape[0]
    row = lambda c: pl.BlockSpec((tm, c), lambda i: (i, 0))
    return pl.pallas_call(
        _odd_proj_body,
        grid=(n // tm,),
        in_specs=[row(d), pl.BlockSpec((1, d), lambda i: (0, 0)),
                  _mod_spec(rows_mod, tm, d), _mod_spec(rows_mod, tm, d),
                  pl.BlockSpec((d, O_COLS), lambda i: (0, 0))],
        out_specs=[row(3072), row(1024), row(128)],
        out_shape=[jax.ShapeDtypeStruct((n, 3072), F32), jax.ShapeDtypeStruct((n, 1024), F32),
                   jax.ShapeDtypeStruct((n, 128), F32)],
        compiler_params=_cparams(("arbitrary",)),
        name="odd_proj",
    )(x, nw, sc, sh, w_packed)


def _gdn_body(qkv_ref, z_ref, ba_ref, cs_ref, s0_ref, cw_ref, hp_ref, nw_ref,
              o_ref, sfin_ref, buf_ref, s_ref, *, c, valid, n_chunks):
    ci = pl.program_id(1)
    halo = 8

    @pl.when(ci == 0)
    def _():
        buf_ref[0:halo, :] = cs_ref[0]
        s_ref[...] = s0_ref[0]

    x = qkv_ref[...]
    buf_ref[halo:halo + c, :] = x
    cw = cw_ref[...]
    conv = buf_ref[halo - 3:halo - 3 + c, :] * cw[0:1, :]
    for j in range(1, CONV_W):
        conv = conv + buf_ref[halo - 3 + j:halo - 3 + j + c, :] * cw[j:j + 1, :]
    buf_ref[0:halo, :] = x[c - halo:, :]
    conv = _silu(conv)

    hp = hp_ref[...]
    ba = ba_ref[...]
    beta_f = _sigmoid(ba)
    g_f = hp[0:1, :] * _softplus(ba + hp[1:2, :])
    if valid < c:
        live = lax.broadcasted_iota(jnp.int32, (c, 1), 0) < valid
        beta_f = jnp.where(live, beta_f, 0.0)
        g_f = jnp.where(live, g_f, 0.0)
        conv = jnp.where(live, conv, 0.0)

    ri = lax.broadcasted_iota(jnp.int32, (c, c), 0)
    cj = lax.broadcasted_iota(jnp.int32, (c, c), 1)
    tril = ri >= cj
    strict = ri > cj
    gc = _mm01(tril, g_f)
    gct = gc.T
    z = z_ref[...]
    nw = nw_ref[...]

    for h in range(GDN_HEADS):
        sl = slice(h * GDN_HD, (h + 1) * GDN_HD)
        q_h = conv[:, sl]
        k_h = conv[:, GDN_W + h * GDN_HD:GDN_W + (h + 1) * GDN_HD]
        v_h = conv[:, 2 * GDN_W + h * GDN_HD:2 * GDN_W + (h + 1) * GDN_HD]
        q_h = q_h * lax.rsqrt(jnp.sum(q_h * q_h, axis=-1, keepdims=True) + EPS) * (GDN_HD ** -0.5)
        k_h = k_h * lax.rsqrt(jnp.sum(k_h * k_h, axis=-1, keepdims=True) + EPS)
        g_col = gc[:, 8 + h:9 + h]
        g_row = gct[8 + h:9 + h, :]
        b_col = beta_f[:, h:h + 1]
        decay = jnp.where(tril, jnp.exp(jnp.where(tril, g_col - g_row, 0.0)), 0.0)
        kb = k_h * b_col
        vb = v_h * b_col
        m_h = jnp.where(strict, _mm3(kb, k_h, NT) * decay, 0.0)
        t_inv = _tri_inverse(m_h, c)
        u_h = _mm(t_inv, vb)
        w_h = _mm(t_inv, kb * jnp.exp(g_col))
        qk = jnp.where(tril, _mm(q_h, k_h, NT) * decay, 0.0)
        s_h = s_ref[h]
        v_new = u_h - _mm(w_h, s_h)
        o_h = _mm(q_h * jnp.exp(g_col), s_h) + _mm(qk, v_new)
        g_last = g_col[c - 1:c, :]
        s_ref[h] = s_h * jnp.exp(g_last) + _mm(k_h * jnp.exp(g_last - g_col), v_new, TN)
        o_h = o_h * lax.rsqrt(jnp.mean(o_h * o_h, axis=-1, keepdims=True) + EPS) * nw
        o_ref[:, sl] = o_h * _silu(z[:, sl])

    @pl.when(ci == n_chunks - 1)
    def _():
        sfin_ref[0] = s_ref[...]


def gdn_mix(qkv, z, ba, cs, s0, conv_w8, hp, nw, *, c, valid):
    b = s0.shape[0]
    rows = qkv.shape[0]
    n_chunks = rows // (b * c)
    const = lambda s: pl.BlockSpec(s, lambda bi, ci: tuple(0 for _ in s))
    row = lambda w: pl.BlockSpec((c, w), lambda bi, ci: (bi * n_chunks + ci, 0))
    return pl.pallas_call(
        functools.partial(_gdn_body, c=c, valid=valid, n_chunks=n_chunks),
        grid=(b, n_chunks),
        in_specs=[row(3072), row(1024), row(128),
                  pl.BlockSpec((1, 8, 3072), lambda bi, ci: (bi, 0, 0)),
                  pl.BlockSpec((1, GDN_HEADS, 128, 128), lambda bi, ci: (bi, 0, 0, 0)),
                  const((8, 3072)), const((8, 128)), const((1, 128))],
        out_specs=[row(1024), pl.BlockSpec((1, GDN_HEADS, 128, 128), lambda bi, ci: (bi, 0, 0, 0))],
        out_shape=[jax.ShapeDtypeStruct((rows, 1024), F32),
                   jax.ShapeDtypeStruct((b, GDN_HEADS, 128, 128), F32)],
        scratch_shapes=[pltpu.VMEM((8 + c, 3072), F32), pltpu.VMEM((GDN_HEADS, 128, 128), F32)],
        compiler_params=_cparams(("arbitrary", "arbitrary")),
        name="gdn_mix",
    )(qkv, z, ba, cs, s0, conv_w8, hp, nw)


def _router_body(x_ref, nw_ref, sc_ref, sh_ref, wr_ref, br_ref, h_ref, gate_ref):
    h = _norm_mod(x_ref[...], nw_ref[...], sc_ref[...], sh_ref[...])
    h_ref[...] = h.astype(BF16)
    logits = _mmh(h, wr_ref[...]) + br_ref[...]
    tm = logits.shape[0]
    lane = lax.broadcasted_iota(jnp.int32, (tm, LANE), 1)
    is_grp = (lane >= N_EXPERTS) & (lane < N_EXPERTS + N_GROUPS)
    gl = jnp.where(is_grp, logits, NEG)
    gmax = jnp.max(gl, axis=-1, keepdims=True)
    g_idx = jnp.min(jnp.where(gl == gmax, lane, 4 * LANE), axis=-1, keepdims=True) - N_EXPERTS
    g_w = 1.0 / jnp.sum(jnp.where(is_grp, jnp.exp(gl - gmax), 0.0), axis=-1, keepdims=True)
    in_grp = (lane < N_EXPERTS) & (lane // EXP_PER_GROUP == g_idx)
    el = jnp.where(in_grp, logits, NEG)
    emax = jnp.max(el, axis=-1, keepdims=True)
    e = jnp.where(in_grp, jnp.exp(el - emax), 0.0)
    p = e / jnp.sum(e, axis=-1, keepdims=True)
    p1 = jnp.where(in_grp, p, -1.0)
    m1 = jnp.max(p1, axis=-1, keepdims=True)
    i1 = jnp.min(jnp.where(p1 == m1, lane, 4 * LANE), axis=-1, keepdims=True)
    p2 = jnp.where(lane == i1, -1.0, p1)
    m2 = jnp.max(p2, axis=-1, keepdims=True)
    i2 = jnp.min(jnp.where(p2 == m2, lane, 4 * LANE), axis=-1, keepdims=True)
    tot = m1 + m2
    gate_ref[...] = jnp.where(lane == i1, m1 / tot * g_w, jnp.where(lane == i2, m2 / tot * g_w, 0.0))


def moe_router(x, nw, sc, sh, w_r, b_r, tm):
    n, d = x.shape
    rows_mod = sc.shape[0]
    return pl.pallas_call(
        _router_body,
        grid=(n // tm,),
        in_specs=[pl.BlockSpec((tm, d), lambda i: (i, 0)), pl.BlockSpec((1, d), lambda i: (0, 0)),
                  _mod_spec(rows_mod, tm, d), _mod_spec(rows_mod, tm, d),
                  pl.BlockSpec((d, LANE), lambda i: (0, 0)), pl.BlockSpec((1, LANE), lambda i: (0, 0))],
        out_specs=[pl.BlockSpec((tm, d), lambda i: (i, 0)), pl.BlockSpec((tm, LANE), lambda i: (i, 0))],
        out_shape=[jax.ShapeDtypeStruct((n, d), BF16), jax.ShapeDtypeStruct((n, LANE), F32)],
        compiler_params=_cparams(("arbitrary",)),
        name="moe_router",
    )(x, nw, sc, sh, w_r, b_r)


def _moe_body(h_ref, gate_ref, w1_ref, w3_ref, w2_ref, x_ref, g2_ref, o_ref, acc_ref):
    e = pl.program_id(1)

    @pl.when(e == 0)
    def _():
        acc_ref[...] = jnp.zeros_like(acc_ref)

    hb = h_ref[...]
    he = _silu(_mm(hb, w1_ref[0])) * _mm(hb, w3_ref[0])
    y = _mm(he, w2_ref[0])
    gate = gate_ref[...]
    lane = lax.broadcasted_iota(jnp.int32, gate.shape, 1)
    ge = jnp.sum(jnp.where(lane == e, gate, 0.0), axis=-1, keepdims=True)
    acc_ref[...] += ge * y

    @pl.when(e == pl.num_programs(1) - 1)
    def _():
        o_ref[...] = x_ref[...] + g2_ref[...] * acc_ref[...]


def moe_ffn(h, gate, w1, w3, w2, x, g2, tm):
    n, d = x.shape
    ne, _, de = w1.shape
    return pl.pallas_call(
        _moe_body,
        grid=(n // tm, ne),
        in_specs=[pl.BlockSpec((tm, d), lambda i, e: (i, 0)), pl.BlockSpec((tm, LANE), lambda i, e: (i, 0)),
                  pl.BlockSpec((1, d, de), lambda i, e: (e, 0, 0)), pl.BlockSpec((1, d, de), lambda i, e: (e, 0, 0)),
                  pl.BlockSpec((1, de, d), lambda i, e: (e, 0, 0)),
                  pl.BlockSpec((tm, d), lambda i, e: (i, 0)),
                  pl.BlockSpec((1, d), lambda i, e: (0, 0)) if g2.shape[0] == 1
                  else pl.BlockSpec((tm, d), lambda i, e: (i, 0))],
        out_specs=pl.BlockSpec((tm, d), lambda i, e: (i, 0)),
        out_shape=jax.ShapeDtypeStruct((n, d), F32),
        scratch_shapes=[pltpu.VMEM((tm, d), F32)],
        compiler_params=_cparams(("arbitrary", "arbitrary")),
        name="moe_ffn",
    )(h, gate, w1, w3, w2, x, g2)


def _final_norm_body(x_ref, w_ref, o_ref):
    x = x_ref[...]
    o_ref[...] = x * lax.rsqrt(jnp.mean(x * x, axis=-1, keepdims=True) + EPS) * w_ref[...]


def final_norm(x, w, tm):
    n, d = x.shape
    return pl.pallas_call(
        _final_norm_body,
        grid=(n // tm,),
        in_specs=[pl.BlockSpec((tm, d), lambda i: (i, 0)), pl.BlockSpec((1, d), lambda i: (0, 0))],
        out_specs=pl.BlockSpec((tm, d), lambda i: (i, 0)),
        out_shape=jax.ShapeDtypeStruct((n, d), F32),
        compiler_params=_cparams(("arbitrary",)),
        name="final_norm",
    )(x, w)


def _row_tile(n, pref):
    t = min(pref, n)
    while n % t:
        t //= 2
    return t


def kernel(x_prompt, x_sample, c_prompt, c_sample, cache_nsa_cmp, cache_nsa_sel, page_table, state_nsa_win, state_rwkv, state_rwkv_shift, state_gdn, state_gdn_conv, norm_mix, norm_ffn, norm_final, w_ada, b_ada, even_w_in, even_w_out, nsa_cmp_pos, nsa_cmp_w, rwkv_mu, rwkv_w0, rwkv_w2, rwkv_a0, rwkv_a2, rwkv_g2, rwkv_kk, rwkv_ka, rwkv_rk, rwkv_ln_w, rwkv_ln_b, odd_w_in, odd_w_out, gdn_conv_w, gdn_a_log, gdn_dt_bias, gdn_norm_w, moe_w_grp, moe_b_grp, moe_w_exp, moe_b_exp, moe_w1, moe_w3, moe_w2):
    bp, t, d = x_prompt.shape
    bs, ts, _ = x_sample.shape
    assert bp == 1 and ts <= SPAD and ts < CMP_BLK
    depth = norm_mix.shape[0]
    n_pages, page = page_table.shape[1], cache_nsa_cmp.shape[2]
    past = n_pages * page
    wb = state_nsa_win.shape[2]
    ns = bs * SPAD
    tq, tq_s, tk = 128, 32, 512
    tm_p = _row_tile(t, 512)
    tm_s = ns

    rows_c = -(-(1 + bs) // 8) * 8
    c_all = jnp.concatenate([c_prompt, c_sample, jnp.zeros((rows_c - 1 - bs, d), F32)], axis=0)
    ada = adaln(c_all, w_ada, b_ada)

    def mods(i):
        mp = [ada[i, 0:1, j * d:(j + 1) * d] for j in range(6)]
        ms = [jnp.repeat(ada[i, 1:1 + bs, j * d:(j + 1) * d], SPAD, axis=0) for j in range(6)]
        return mp, ms

    xp = x_prompt[0]
    xs = jnp.pad(x_sample, ((0, 0), (0, SPAD - ts), (0, 0))).reshape(ns, d)

    def unpad(a):
        return a.reshape(bs, SPAD, -1)[:, :ts]

    outs = {k: [] for k in ("cmp_p", "cmp_s", "sel_p", "sel_s", "win_p", "win_s", "rw_p", "rw_s", "sh_p", "sh_s",
                            "gd_p", "gd_s", "cv_p", "cv_s")}

    for i in range(depth):
        (sh1p, sc1p, gt1p, sh2p, sc2p, gt2p), (sh1s, sc1s, gt1s, sh2s, sc2s, gt2s) = mods(i)
        j = i // 2
        nw = norm_mix[i][None, :]
        if i % 2 == 0:
            w_packed = _pack_even_w(even_w_in[j])
            mu = _pack_rw_vec(rwkv_mu[j])
            wts, wc = _cmp_weights(nsa_cmp_pos[j], nsa_cmp_w[j])
            vec = jnp.stack([rwkv_w0[j], rwkv_a0[j], rwkv_kk[j], rwkv_ka[j], rwkv_ln_w[j], rwkv_ln_b[j],
                             jnp.zeros_like(rwkv_w0[j]), jnp.zeros_like(rwkv_w0[j])])
            pad_lora = lambda w: jnp.concatenate([w, jnp.zeros((128 - w.shape[0], w.shape[1]), w.dtype)], axis=0)
            w2p, a2p, g2p = pad_lora(rwkv_w2[j]), pad_lora(rwkv_a2[j]), rwkv_g2[j]
            hid = jnp.arange(RWKV_W) // RWKV_HD
            seg = (hid[:, None] == hid[None, :]).astype(F32)
            rk = rwkv_rk[j].reshape(1, RWKV_W)
            wo_nsa, wo_rw = even_w_out[j][:512].astype(BF16), even_w_out[j][512:].astype(BF16)

            kv, qt, gt, ks, vst, kw, vwt, rw, hl = even_proj(xp, nw, sc1p, sh1p, w_packed, tm_p, 8)
            kvc = compress_prompt(kv, wts, wc, _row_tile(t, 512))
            gates = gt[:24].reshape(NSA_KV_HEADS, 12, t)
            gates = jnp.pad(gates, ((0, 0), (0, 4), (0, 0)))[None]
            o_nsa = nsa_attention(
                qt[None], gates, kvc[None], kvc.T[None], ks[None], vst[None], kw[None], vwt[None],
                tq=tq, tk=tk, wk=WINDOW + tq,
                pos0_fn=lambda qi: qi * tq,
                wstart_fn=lambda qi: jnp.maximum(qi * tq - WINDOW, 0),
                wpos0_fn=lambda qi: jnp.maximum(qi * tq - WINDOW, 0))[0]
            o_rw, s_rw = rwkv_mix(rw, jnp.zeros((1, 8, RW_COLS), F32), jnp.zeros((1, RWKV_HEADS, 64, 64), F32),
                                  mu, vec, w2p, a2p, g2p, seg, rk, c=64, valid=64)
            xp = out_proj([o_nsa, o_rw], [wo_nsa, wo_rw], xp, gt1p, tm_p)
            outs["cmp_p"].append(kv[:, 0:256].reshape(1, t, 2, 2, 64))
            outs["sel_p"].append(kv[:, 256:512].reshape(1, t, 2, 2, 64))
            kvw_rows = kv[:, 512:768].reshape(1, t, 2, 2, 64)
            outs["win_p"].append(kvw_rows[:, -min(WINDOW, t):])
            outs["rw_p"].append(s_rw)
            outs["sh_p"].append(hl[-1:])

            kv, qt, gt, _, _, _, _, rw, hl = even_proj(xs, nw, sc1s, sh1s, w_packed, tm_s, ns)
            kv_new = unpad(kv)
            rw0 = small_matmul(jnp.pad(state_rwkv_shift[j], ((0, -bs % 8), (0, 0))), w_packed[:, E_RW:])[:bs]
            rw0 = jnp.pad(rw0[:, None, :], ((0, 0), (7, 0), (0, 0)))
            pool_cmp = cache_nsa_cmp[j].reshape(-1, page, 256)
            pool_sel = cache_nsa_sel[j].reshape(-1, page, 256)
            kvc_s = compress_paged(pool_cmp, page_table, wts, wc, 8 if n_pages % 8 == 0 else 1)
            tail = jnp.pad(kv_new[:, :, 256:512], ((0, 0), (0, tk - ts), (0, 0)))
            ks_s, vst_s = gather_sel(pool_sel, page_table, tail, tk)
            wbuf = state_nsa_win[j].reshape(bs, wb, 256)
            kvw_all = jnp.concatenate([wbuf, kv_new[:, :, 512:768]], axis=1)
            wk_s = -(-(wb + ts) // 128) * 128
            kvw_pad = jnp.pad(kvw_all, ((0, 0), (0, wk_s - wb - ts), (0, 0)))
            kw_s = kvw_pad[:, :, :128].astype(BF16)
            vwt_s = jnp.swapaxes(kvw_pad[:, :, 128:], 1, 2).astype(BF16)
            qt_s = jnp.pad(qt.reshape(512, bs, SPAD).transpose(1, 0, 2), ((0, 0), (0, 0), (0, tq_s - SPAD)))
            g_s = gt[:24].reshape(NSA_KV_HEADS, 12, bs, SPAD).transpose(2, 0, 1, 3)
            g_s = jnp.pad(g_s, ((0, 0), (0, 0), (0, 4), (0, tq_s - SPAD)))
            o_nsa = nsa_attention(
                qt_s, g_s, kvc_s, jnp.swapaxes(kvc_s, 1, 2), ks_s, vst_s, kw_s, vwt_s,
                tq=tq_s, tk=tk, wk=wk_s,
                pos0_fn=lambda qi: past,
                wstart_fn=lambda qi: 0,
                wpos0_fn=lambda qi: past - wb)
            o_nsa = o_nsa[:, :SPAD].reshape(ns, 512)
            o_rw, s_rw = rwkv_mix(rw, rw0, state_rwkv[j], mu, vec, w2p, a2p, g2p, seg, rk, c=SPAD, valid=ts)
            xs = out_proj([o_nsa, o_rw], [wo_nsa, wo_rw], xs, gt1s, tm_s)
            outs["cmp_s"].append(kv_new[:, :, 0:256].reshape(bs, ts, 2, 2, 64))
            outs["sel_s"].append(kv_new[:, :, 256:512].reshape(bs, ts, 2, 2, 64))
            outs["win_s"].append(kvw_all[:, -wb:].reshape(bs, wb, 2, 2, 64))
            outs["rw_s"].append(s_rw)
            outs["sh_s"].append(hl.reshape(bs, SPAD, d)[:, ts - 1])
        else:
            w_in = odd_w_in[j]
            w_packed = jnp.concatenate([w_in, jnp.zeros((d, O_COLS - w_in.shape[1]), F32)], axis=1).astype(BF16)
            conv_w8 = jnp.pad(gdn_conv_w[j], ((0, 8 - CONV_W), (0, 0)))
            hp = jnp.zeros((8, 128), F32)
            hp = hp.at[0, 8:16].set(-jnp.exp(gdn_a_log[j])).at[1, 8:16].set(gdn_dt_bias[j])
            gnw = gdn_norm_w[j][None, :]
            wo = odd_w_out[j].astype(BF16)

            qkv, z, ba = odd_proj(xp, nw, sc1p, sh1p, w_packed, tm_p)
            o_g, s_g = gdn_mix(qkv, z, ba, jnp.zeros((1, 8, 3 * GDN_W), F32),
                               jnp.zeros((1, GDN_HEADS, GDN_HD, GDN_HD), F32), conv_w8, hp, gnw, c=64, valid=64)
            xp = out_proj([o_g], [wo], xp, gt1p, tm_p)
            outs["gd_p"].append(s_g)
            outs["cv_p"].append(qkv[None, -(CONV_W - 1):])

            qkv, z, ba = odd_proj(xs, nw, sc1s, sh1s, w_packed, tm_s)
            cs = jnp.pad(state_gdn_conv[j], ((0, 0), (8 - (CONV_W - 1), 0), (0, 0)))
            o_g, s_g = gdn_mix(qkv, z, ba, cs, state_gdn[j], conv_w8, hp, gnw, c=SPAD, valid=ts)
            xs = out_proj([o_g], [wo], xs, gt1s, tm_s)
            xpad = jnp.concatenate([state_gdn_conv[j], unpad(qkv)], axis=1)
            outs["gd_s"].append(s_g)
            outs["cv_s"].append(xpad[:, -(CONV_W - 1):])

        nwf = norm_ffn[i][None, :]
        w_r = jnp.concatenate([moe_w_exp[i], moe_w_grp[i], jnp.zeros((d, LANE - N_EXPERTS - N_GROUPS), F32)], axis=1)
        b_r = jnp.concatenate([moe_b_exp[i], moe_b_grp[i], jnp.zeros((LANE - N_EXPERTS - N_GROUPS,), F32)])[None, :]
        h2, gate = moe_router(xp, nwf, sc2p, sh2p, w_r, b_r, tm_p)
        xp = moe_ffn(h2, gate, moe_w1[i], moe_w3[i], moe_w2[i], xp, gt2p, _row_tile(t, 1024))
        h2, gate = moe_router(xs, nwf, sc2s, sh2s, w_r, b_r, tm_s)
        xs = moe_ffn(h2, gate, moe_w1[i], moe_w3[i], moe_w2[i], xs, gt2s, tm_s)

    nf = norm_final[None, :]
    y_prompt = final_norm(xp, nf, tm_p)[None]
    y_sample = unpad(final_norm(xs, nf, tm_s))
    st = lambda key: jnp.stack(outs[key])
    return (y_prompt, y_sample, st("cmp_p"), st("cmp_s"), st("sel_p"), st("sel_s"), st("win_p"), st("win_s"),
            st("rw_p"), st("rw_s"), st("sh_p"), st("sh_s"), st("gd_p"), st("gd_s"), st("cv_p"), st("cv_s"))
```

```python
import functools
import math

import jax
import jax.numpy as jnp
from jax import lax
from jax.experimental import pallas as pl
from jax.experimental.pallas import tpu as pltpu

F32 = jnp.float32
BF16 = jnp.bfloat16
HIGHEST = lax.Precision.HIGHEST

NSA_HEADS = 8
NSA_KV_HEADS = 2
NSA_GROUP = 4
NSA_HD = 64
CMP_BLK = 64
SEL_BLK = 64
TOPK_BLK = 16
WINDOW = 512
FORCE_BONUS = 2.0 * NSA_GROUP
RWKV_HEADS = 8
RWKV_HD = 64
RWKV_W = 512
RWKV_GN_EPS = 64e-5
GDN_HEADS = 8
GDN_HD = 128
GDN_W = 1024
CONV_W = 4
N_GROUPS = 4
EXP_PER_GROUP = 8
N_EXPERTS = 32
EPS = 1e-6
NEG = -1e30

LANE = 128
SPAD = 8
VMEM_LIMIT = 56 * 1024 * 1024

NN = (((1,), (0,)), ((), ()))
NT = (((1,), (1,)), ((), ()))
TN = (((0,), (0,)), ((), ()))

E_Q, E_KV, E_G, E_RW = 0, 512, 1280, 1408
E_COLS = 1408 + 1920
RW_COLS = 1920
O_COLS = 3072 + 1024 + 128


def _mm(a, b, dims=NN):
    return lax.dot_general(a.astype(BF16), b.astype(BF16), dims, preferred_element_type=F32)


def _mmh(a, b, dims=NN):
    return lax.dot_general(a.astype(F32), b.astype(F32), dims, precision=HIGHEST, preferred_element_type=F32)


def _split(a):
    hi = a.astype(BF16)
    return hi, (a - hi.astype(F32)).astype(BF16)


def _mm3(a, b, dims=NN):
    ah, al = _split(a)
    bh, bl = _split(b)
    d = lambda x, y: lax.dot_general(x, y, dims, preferred_element_type=F32)
    return d(ah, bh) + (d(ah, bl) + d(al, bh))


def _mm01(m01, x, dims=NN, left=True):
    h1 = x.astype(BF16)
    r1 = x - h1.astype(F32)
    h2 = r1.astype(BF16)
    h3 = (r1 - h2.astype(F32)).astype(BF16)
    m = m01.astype(BF16)
    if left:
        d = lambda y: lax.dot_general(m, y, dims, preferred_element_type=F32)
    else:
        d = lambda y: lax.dot_general(y, m, dims, preferred_element_type=F32)
    return d(h1) + (d(h2) + d(h3))


def _sigmoid(x):
    return 1.0 / (1.0 + jnp.exp(-x))


def _silu(x):
    return x * _sigmoid(x)


def _softplus(x):
    return jnp.maximum(x, 0.0) + jnp.log(1.0 + jnp.exp(-jnp.abs(x)))


def _cparams(sem):
    return pltpu.CompilerParams(dimension_semantics=sem, vmem_limit_bytes=VMEM_LIMIT)


def _norm_mod(x, nw, sc, sh):
    y = x * lax.rsqrt(jnp.mean(x * x, axis=-1, keepdims=True) + EPS)
    return (y * nw) * (1.0 + sc) + sh


def _mod_spec(rows_mod, tm, d):
    if rows_mod == 1:
        return pl.BlockSpec((1, d), lambda i: (0, 0))
    return pl.BlockSpec((tm, d), lambda i: (i, 0))


def _adaln_body(c_ref, w_ref, b_ref, o_ref):
    o_ref[0] = _mmh(_silu(c_ref[...]), w_ref[0]) + b_ref[0]


def adaln(c_all, w_ada, b_ada):
    depth, d, n6 = w_ada.shape
    rows = c_all.shape[0]
    tn = 768
    return pl.pallas_call(
        _adaln_body,
        grid=(depth, n6 // tn),
        in_specs=[pl.BlockSpec((rows, d), lambda l, j: (0, 0)),
                  pl.BlockSpec((1, d, tn), lambda l, j: (l, 0, j)),
                  pl.BlockSpec((1, 1, tn), lambda l, j: (l, 0, j))],
        out_specs=pl.BlockSpec((1, rows, tn), lambda l, j: (l, 0, j)),
        out_shape=jax.ShapeDtypeStruct((depth, rows, n6), F32),
        compiler_params=_cparams(("arbitrary", "arbitrary")),
        name="adaln",
    )(c_all, w_ada, b_ada.reshape(depth, 1, n6))


def _even_proj_body(x_ref, nw_ref, sc_ref, sh_ref, w_ref,
                    kv_ref, qt_ref, gt_ref, ks_ref, vst_ref, kw_ref, vwt_ref, rw_ref, hl_ref):
    h = _norm_mod(x_ref[...], nw_ref[...], sc_ref[...], sh_ref[...])
    hl = hl_ref.shape[0]
    hl_ref[...] = h[h.shape[0] - hl:, :]
    hb = h.astype(BF16)
    q = _mm(hb, w_ref[:, E_Q:E_Q + 512]) * (NSA_HD ** -0.5)
    qt_ref[...] = q.T.astype(BF16)
    kv = _mm(hb, w_ref[:, E_KV:E_KV + 768])
    kv_ref[...] = kv
    ks_ref[...] = kv[:, 256:384].astype(BF16)
    vst_ref[...] = kv[:, 384:512].T.astype(BF16)
    kw_ref[...] = kv[:, 512:640].astype(BF16)
    vwt_ref[...] = kv[:, 640:768].T.astype(BF16)
    g = _sigmoid(_mm(hb, w_ref[:, E_G:E_G + 128]))
    gt_ref[...] = g.T
    rw_ref[...] = _mm(hb, w_ref[:, E_RW:E_RW + RW_COLS])


def even_proj(x, nw, sc, sh, w_packed, tm, hl_rows):
    n, d = x.shape
    rows_mod = sc.shape[0]
    row = lambda c: pl.BlockSpec((tm, c), lambda i: (i, 0))
    col = lambda r: pl.BlockSpec((r, tm), lambda i: (0, i))
    return pl.pallas_call(
        _even_proj_body,
        grid=(n // tm,),
        in_specs=[row(d), pl.BlockSpec((1, d), lambda i: (0, 0)),
                  _mod_spec(rows_mod, tm, d), _mod_spec(rows_mod, tm, d),
                  pl.BlockSpec((d, E_COLS), lambda i: (0, 0))],
        out_specs=[row(768), col(512), col(128), row(128), col(128), row(128), col(128), row(RW_COLS),
                   pl.BlockSpec((hl_rows, d), lambda i: (0, 0))],
        out_shape=[jax.ShapeDtypeStruct((n, 768), F32),
                   jax.ShapeDtypeStruct((512, n), BF16),
                   jax.ShapeDtypeStruct((128, n), F32),
                   jax.ShapeDtypeStruct((n, 128), BF16),
                   jax.ShapeDtypeStruct((128, n), BF16),
                   jax.ShapeDtypeStruct((n, 128), BF16),
                   jax.ShapeDtypeStruct((128, n), BF16),
                   jax.ShapeDtypeStruct((n, RW_COLS), F32),
                   jax.ShapeDtypeStruct((hl_rows, d), F32)],
        compiler_params=_cparams(("arbitrary",)),
        name="even_proj",
    )(x, nw, sc, sh, w_packed)


def _pack_even_w(w_in):
    d = w_in.shape[0]
    z = lambda c: jnp.zeros((d, c), w_in.dtype)
    nsa = 1304
    rw = w_in[:, nsa:]
    parts = [w_in[:, :1280], w_in[:, 1280:1304], z(104),
             rw[:, :1536], rw[:, 1536:1600], z(64), rw[:, 1600:1664], z(64), rw[:, 1664:1792]]
    return jnp.concatenate(parts, axis=1).astype(BF16)


def _pack_rw_vec(v):
    z = jnp.zeros((64,), v.dtype)
    return jnp.concatenate([v[:1536], v[1536:1600], z, v[1600:1664], z, v[1664:1792]])[None, :]


def _mm_body(x_ref, w_ref, o_ref):
    o_ref[...] = _mm(x_ref[...], w_ref[...])


def small_matmul(x, w):
    return pl.pallas_call(
        _mm_body,
        out_shape=jax.ShapeDtypeStruct((x.shape[0], w.shape[1]), F32),
        compiler_params=pltpu.CompilerParams(vmem_limit_bytes=VMEM_LIMIT),
        name="small_matmul",
    )(x, w)


def _compress_body(x_ref, wts_ref, wc_ref, o_ref):
    x = x_ref[...]
    nb = x.shape[0] // CMP_BLK
    pooled = jnp.sum(x.reshape(nb, CMP_BLK, x.shape[-1]) * wts_ref[...][None], axis=1)
    o_ref[...] = _mm(pooled, wc_ref[...])


def _compress_paged_body(pt_ref, *refs, pps):
    page_refs = refs[:pps]
    wp_ref, wc_ref, o_ref = refs[pps:]
    x = jnp.concatenate([r[0] for r in page_refs], axis=1)
    pooled_t = jnp.concatenate([_mm(x[0:128], wp_ref[0]), _mm(x[128:256], wp_ref[1])], axis=0)
    nb = o_ref.shape[1]
    o_ref[0] = _mm(pooled_t.T[:nb], wc_ref[...])


def _cmp_weights(pos_wts, w_c):
    wts = jnp.repeat(pos_wts.T, 128, axis=1)
    eye2 = jnp.eye(2, dtype=w_c.dtype)
    blocks = [jnp.kron(eye2, w_c[c]) for c in range(2)]
    z = jnp.zeros((128, 128), w_c.dtype)
    wc = jnp.concatenate([jnp.concatenate([blocks[0], z], axis=1),
                          jnp.concatenate([z, blocks[1]], axis=1)], axis=0)
    return wts, wc


def compress_prompt(kv, wts, wc, tr):
    t = kv.shape[0]
    nb = tr // CMP_BLK
    return pl.pallas_call(
        _compress_body,
        grid=(t // tr,),
        in_specs=[pl.BlockSpec((tr, 256), lambda i: (i, 0)),
                  pl.BlockSpec((CMP_BLK, 256), lambda i: (0, 0)),
                  pl.BlockSpec((256, 256), lambda i: (0, 0))],
        out_specs=pl.BlockSpec((nb, 256), lambda i: (i, 0)),
        out_shape=jax.ShapeDtypeStruct((t // CMP_BLK, 256), F32),
        compiler_params=_cparams(("arbitrary",)),
        name="compress_prompt",
    )(kv, wts, wc)


def compress_paged(pool_t, page_table, pos_wts, wc, pages_per_step):
    b, n_pages = page_table.shape
    page = pool_t.shape[2]
    pps = pages_per_step
    nb = pps * page // CMP_BLK
    p_idx = jnp.arange(pps * page)
    wp = jax.nn.one_hot(p_idx // CMP_BLK, LANE, dtype=F32)[None] * pos_wts[:, p_idx % CMP_BLK][:, :, None]

    def page_spec(u):
        return pl.BlockSpec((1, 256, page), lambda bi, g, pt: (pt[bi, g * pps + u], 0, 0))

    grid_spec = pltpu.PrefetchScalarGridSpec(
        num_scalar_prefetch=1,
        grid=(b, n_pages // pps),
        in_specs=[page_spec(u) for u in range(pps)] + [
            pl.BlockSpec((2, pps * page, LANE), lambda bi, g, pt: (0, 0, 0)),
            pl.BlockSpec((256, 256), lambda bi, g, pt: (0, 0))],
        out_specs=pl.BlockSpec((1, nb, 256), lambda bi, g, pt: (bi, g, 0)),
    )
    return pl.pallas_call(
        functools.partial(_compress_paged_body, pps=pps),
        grid_spec=grid_spec,
        out_shape=jax.ShapeDtypeStruct((b, n_pages * page // CMP_BLK, 256), F32),
        compiler_params=_cparams(("arbitrary", "arbitrary")),
        name="compress_paged",
    )(page_table, *([pool_t] * pps), wp, wc)


def _gather_sel_body(pt_ref, *refs, pps, n_page_steps):
    page_refs = refs[:pps]
    tail_ref, ks_ref, vst_ref = refs[pps:]
    g = pl.program_id(1)

    @pl.when(g < n_page_steps)
    def _():
        ks_ref[0] = jnp.concatenate([r[0][0:128].T for r in page_refs], axis=0).astype(BF16)
        vst_ref[0] = jnp.concatenate([r[0][128:256] for r in page_refs], axis=1).astype(BF16)

    @pl.when(g >= n_page_steps)
    def _():
        x = tail_ref[0]
        ks_ref[0] = x[:, :128].astype(BF16)
        vst_ref[0] = x[:, 128:].T.astype(BF16)


def gather_sel(pool_t, page_table, tail, tk):
    b, n_pages = page_table.shape
    page = pool_t.shape[2]
    pps = tk // page
    n_page_steps = n_pages // pps
    nk = n_pages * page + tk

    def page_spec(u):
        return pl.BlockSpec((1, 256, page),
                            lambda bi, g, pt: (pt[bi, jnp.minimum(g * pps + u, n_pages - 1)], 0, 0))

    grid_spec = pltpu.PrefetchScalarGridSpec(
        num_scalar_prefetch=1,
        grid=(b, n_page_steps + 1),
        in_specs=[page_spec(u) for u in range(pps)] + [pl.BlockSpec((1, tk, 256), lambda bi, g, pt: (bi, 0, 0))],
        out_specs=[pl.BlockSpec((1, tk, 128), lambda bi, g, pt: (bi, g, 0)),
                   pl.BlockSpec((1, 128, tk), lambda bi, g, pt: (bi, 0, g))],
    )
    return pl.pallas_call(
        functools.partial(_gather_sel_body, pps=pps, n_page_steps=n_page_steps),
        grid_spec=grid_spec,
        out_shape=[jax.ShapeDtypeStruct((b, nk, 128), BF16), jax.ShapeDtypeStruct((b, 128, nk), BF16)],
        compiler_params=_cparams(("arbitrary", "arbitrary")),
        name="gather_sel",
    )(page_table, *([pool_t] * pps), tail)


MASKED = -1e30
M_INIT = -1e29


def _nsa_query(qt_ref, k, tq):
    w4 = NSA_GROUP * tq
    qb = qt_ref[0].astype(F32)
    qcat = jnp.concatenate([qb[g * 64:(g + 1) * 64] for g in range(NSA_GROUP)], axis=1)
    q2 = jnp.concatenate([qcat, qcat], axis=0)
    row = lax.broadcasted_iota(jnp.int32, (128, w4), 0)
    qe = jnp.where(row // 64 == k, q2, 0.0)
    gidx = lax.broadcasted_iota(jnp.int32, (128, w4), 1) // tq
    base = jnp.where(k == 0, 0.5, 0.5 / 16.0)
    slope = base * jnp.where(gidx == 0, 1.0, jnp.where(gidx == 1, 0.5, jnp.where(gidx == 2, 0.25, 0.125)))
    mult = jnp.where(row == 0, 16.0, jnp.where(row == 1, 1.0, jnp.where(row == 2, 128.0,
                                                                         jnp.where(row == 3, 64.0, 0.0))))
    return jnp.concatenate([qe, slope * mult], axis=0).astype(BF16)


def _pos_features(rows, tile_rel):
    r = lax.broadcasted_iota(jnp.int32, (rows, LANE), 0)
    lane = lax.broadcasted_iota(jnp.int32, (rows, LANE), 1)
    ab = jnp.where(lane == 0, r // 16, jnp.where(lane == 1, r % 16, 0)).astype(F32)
    return jnp.where(lane == 2, tile_rel, ab).astype(BF16)


def _gate_rows(gb, j, tq):
    return jnp.concatenate([gb[g * 3 + j:g * 3 + j + 1, :] for g in range(NSA_GROUP)], axis=1)


def _nsa_select_body(qt_ref, g_ref, kvc_ref, kvct_ref, kw_ref, vwt_ref, part_ref, sel_ref, flag_ref, *,
                     tq, tk, wk, nbc, nb, pos0_fn, wstart_fn, wpos0_fn):
    i = pl.program_id(1)
    k = pl.program_id(2)
    w4 = NSA_GROUP * tq
    pos0 = pos0_fn(i)
    qa = _nsa_query(qt_ref, k, tq)
    pos_q = pos0 + lax.broadcasted_iota(jnp.int32, (1, w4), 1) % tq

    def softmax_cols(s, bad):
        s = jnp.where(bad, MASKED, s)
        m = jnp.maximum(jnp.max(s, axis=0, keepdims=True), M_INIT)
        e = jnp.exp(s - m)
        return e / jnp.maximum(jnp.sum(e, axis=0, keepdims=True), 1e-30)

    n_i = lax.broadcasted_iota(jnp.int32, (nbc, LANE), 0)
    lane_c = lax.broadcasted_iota(jnp.int32, (nbc, LANE), 1)
    feat_c = jnp.where(lane_c == 3, n_i - pos0 // CMP_BLK, 0).astype(F32).astype(BF16)
    kc = jnp.concatenate([kvc_ref[0][:, :128].astype(BF16), feat_c], axis=1)
    c_end = lax.broadcasted_iota(jnp.int32, (nbc, 1), 0) * CMP_BLK + (CMP_BLK - 1)
    p_c = softmax_cols(lax.dot_general(kc, qa, NN, preferred_element_type=F32), c_end > pos_q)
    vct = kvct_ref[0, pl.ds(pl.multiple_of(128 + k * 64, 64), 64), :]
    o_c = _mm(vct, p_c)

    imp = p_c[:, 0:tq]
    for g in range(1, NSA_GROUP):
        imp = imp + p_c[:, g * tq:(g + 1) * tq]
    if nb > nbc:
        imp = jnp.concatenate([imp, jnp.zeros((nb - nbc, tq), F32)], axis=0)
    blk = lax.broadcasted_iota(jnp.int32, (nb, tq), 0)
    cur = (pos0 + lax.broadcasted_iota(jnp.int32, (1, tq), 1)) // SEL_BLK
    forced = (blk == cur) | (blk == cur - 1) | (blk == 0)
    score = jnp.where(blk <= cur, imp + jnp.where(forced, FORCE_BONUS, 0.0), -1.0)
    sel = jnp.zeros((nb, tq), F32)
    for _ in range(min(TOPK_BLK, nb)):
        m = jnp.max(score, axis=0, keepdims=True)
        first = jnp.min(jnp.where(score == m, blk, nb), axis=0, keepdims=True)
        hit = blk == first
        sel = jnp.where(hit, 1.0, sel)
        score = jnp.where(hit, -2.0, score)
    sel_ref[0, 0] = sel
    bpt = tk // SEL_BLK
    any_row = jnp.max(sel, axis=1, keepdims=True)
    flag_ref[0, 0] = jnp.max(any_row.reshape(nb // bpt, bpt, 1), axis=1)

    wstart = wstart_fn(i)
    if not isinstance(wstart, int):
        wstart = pl.multiple_of(wstart, 128)
    wpos0 = wpos0_fn(i)
    tile_rel = jnp.asarray((wpos0 - pos0) // 128, F32)
    kw = jnp.concatenate([kw_ref[0, pl.ds(wstart, wk), :], _pos_features(wk, tile_rel)], axis=1)
    dist_w = pos_q - (wpos0 + lax.broadcasted_iota(jnp.int32, (wk, 1), 0))
    p_w = softmax_cols(lax.dot_general(kw, qa, NN, preferred_element_type=F32), (dist_w < 0) | (dist_w >= WINDOW))
    vwin = vwt_ref[0, pl.ds(pl.multiple_of(k * 64, 64), 64), pl.ds(wstart, wk)]
    o_w = _mm(vwin, p_w)

    gb = g_ref[0, 0]
    part_ref[0, 0] = _gate_rows(gb, 0, tq) * o_c + _gate_rows(gb, 2, tq) * o_w


def nsa_select(qt, gates, kvc, kvct, kw, vwt, *, nb, tq, tk, wk, pos0_fn, wstart_fn, wpos0_fn):
    b, _, nq = qt.shape
    nbc = kvc.shape[1]
    nw = kw.shape[1]
    nqt = nq // tq
    nt = nb * SEL_BLK // tk
    w4 = NSA_GROUP * tq
    assert nbc <= 256 and tk <= 512 and wk <= 1024
    body = functools.partial(_nsa_select_body, tq=tq, tk=tk, wk=wk, nbc=nbc, nb=nb, pos0_fn=pos0_fn,
                             wstart_fn=wstart_fn, wpos0_fn=wpos0_fn)
    full = lambda s1, s2: pl.BlockSpec((1, s1, s2), lambda bi, i, k: (bi, 0, 0))
    step = lambda s1, s2: pl.BlockSpec((1, 1, s1, s2), lambda bi, i, k: (bi, i * NSA_KV_HEADS + k, 0, 0))
    return pl.pallas_call(
        body,
        grid=(b, nqt, NSA_KV_HEADS),
        in_specs=[pl.BlockSpec((1, 256, tq), lambda bi, i, k: (bi, k, i)),
                  pl.BlockSpec((1, 1, 16, tq), lambda bi, i, k: (bi, k, 0, i)),
                  full(nbc, 256), full(256, nbc), full(nw, 128), full(128, nw)],
        out_specs=[step(64, w4), step(nb, tq), step(nt, 1)],
        out_shape=[jax.ShapeDtypeStruct((b, nqt * 2, 64, w4), F32),
                   jax.ShapeDtypeStruct((b, nqt * 2, nb, tq), F32),
                   jax.ShapeDtypeStruct((b, nqt * 2, nt, 1), F32)],
        compiler_params=_cparams(("arbitrary", "arbitrary", "arbitrary")),
        name="nsa_select",
    )(qt, gates, kvc, kvct, kw, vwt)


def _nsa_selected_body(list_ref, cnt_ref, qt_ref, g_ref, sel_ref, ks_ref, vst_ref, part_ref, o_ref, *,
                       tq, tk, nt, pos0_fn):
    bi = pl.program_id(0)
    i = pl.program_id(1)
    k = pl.program_id(2)
    step = (bi * pl.num_programs(1) + i) * NSA_KV_HEADS + k
    w4 = NSA_GROUP * tq
    pos0 = pos0_fn(i)
    qa = _nsa_query(qt_ref, k, tq)
    pos_q = pos0 + lax.broadcasted_iota(jnp.int32, (1, w4), 1) % tq
    bpt = tk // SEL_BLK
    row_k = lax.broadcasted_iota(jnp.int32, (tk, 1), 0)
    r = lax.broadcasted_iota(jnp.int32, (tk, LANE), 0)
    lane = lax.broadcasted_iota(jnp.int32, (tk, LANE), 1)
    feat_ab = jnp.where(lane == 0, r // 16, jnp.where(lane == 1, r % 16, 0)).astype(F32)

    def kv_step(jj, carry):
        m_i, l_i, acc = carry
        j = list_ref[step * nt + jj]
        off = pl.multiple_of(j * tk, tk)
        tile_rel = ((off - pos0) // 128).astype(F32)
        feat = jnp.where(lane == 2, tile_rel, feat_ab).astype(BF16)
        kj = jnp.concatenate([ks_ref[0, pl.ds(off, tk), :], feat], axis=1)
        s = lax.dot_general(kj, qa, NN, preferred_element_type=F32)
        selb = (sel_ref[0, 0, pl.ds(pl.multiple_of(j * bpt, bpt), bpt), :] - 1.0) * (-MASKED)
        selb = jnp.concatenate([selb] * NSA_GROUP, axis=1)
        s = s + jnp.broadcast_to(selb[:, None, :], (bpt, SEL_BLK, w4)).reshape(tk, w4)
        s = jnp.where(row_k > pos_q - off, MASKED, s)
        m_new = jnp.maximum(m_i, jnp.max(s, axis=0, keepdims=True))
        p = jnp.exp(s - m_new)
        alpha = jnp.exp(m_i - m_new)
        l_new = l_i * alpha + jnp.sum(p, axis=0, keepdims=True)
        vj = vst_ref[0, pl.ds(pl.multiple_of(k * 64, 64), 64), pl.ds(off, tk)]
        return m_new, l_new, acc * alpha + _mm(vj, p)

    init = (jnp.full((1, w4), M_INIT, F32), jnp.zeros((1, w4), F32), jnp.zeros((64, w4), F32))
    _, l_s, acc_s = lax.fori_loop(0, cnt_ref[step], kv_step, init)
    o_s = acc_s / jnp.maximum(l_s, 1e-30)
    o_t = part_ref[0, 0] + _gate_rows(g_ref[0, 0], 1, tq) * o_s
    o_ref[0] = jnp.concatenate([o_t[:, g * tq:(g + 1) * tq].T for g in range(NSA_GROUP)], axis=1)


def nsa_selected(tile_list, tile_cnt, qt, gates, sel, ks, vst, part, *, tq, tk, pos0_fn):
    b, _, nq = qt.shape
    nk = ks.shape[1]
    nb = sel.shape[2]
    nt = nk // tk
    w4 = NSA_GROUP * tq
    full = lambda s1, s2: pl.BlockSpec((1, s1, s2), lambda bi, i, k, *_: (bi, 0, 0))
    step = lambda s1, s2: pl.BlockSpec((1, 1, s1, s2), lambda bi, i, k, *_: (bi, i * NSA_KV_HEADS + k, 0, 0))
    grid_spec = pltpu.PrefetchScalarGridSpec(
        num_scalar_prefetch=2,
        grid=(b, nq // tq, NSA_KV_HEADS),
        in_specs=[pl.BlockSpec((1, 256, tq), lambda bi, i, k, *_: (bi, k, i)),
                  pl.BlockSpec((1, 1, 16, tq), lambda bi, i, k, *_: (bi, k, 0, i)),
                  step(nb, tq), full(nk, 128), full(128, nk), step(64, w4)],
        out_specs=pl.BlockSpec((1, tq, 256), lambda bi, i, k, *_: (bi, i, k)),
    )
    return pl.pallas_call(
        functools.partial(_nsa_selected_body, tq=tq, tk=tk, nt=nt, pos0_fn=pos0_fn),
        grid_spec=grid_spec,
        out_shape=jax.ShapeDtypeStruct((b, nq, 512), F32),
        compiler_params=_cparams(("arbitrary", "arbitrary", "arbitrary")),
        name="nsa_selected",
    )(tile_list, tile_cnt, qt, gates, sel, ks, vst, part)


def nsa_attention(qt, gates, kvc, kvct, ks, vst, kw, vwt, *, tq, tk, wk, pos0_fn, wstart_fn, wpos0_fn):
    nb = ks.shape[1] // SEL_BLK
    part, sel, flags = nsa_select(qt, gates, kvc, kvct, kw, vwt, nb=nb, tq=tq, tk=tk, wk=wk, pos0_fn=pos0_fn,
                                  wstart_fn=wstart_fn, wpos0_fn=wpos0_fn)
    active = flags[..., 0] > 0.5
    order = jnp.argsort(jnp.where(active, 0, 1), axis=-1, stable=True).astype(jnp.int32)
    cnt = jnp.sum(active, axis=-1).astype(jnp.int32)
    return nsa_selected(order.reshape(-1), cnt.reshape(-1), qt, gates, sel, ks, vst, part,
                        tq=tq, tk=tk, pos0_fn=pos0_fn)


def _tri_inverse(ms, c):
    eye = (lax.broadcasted_iota(jnp.int32, (c, c), 0) == lax.broadcasted_iota(jnp.int32, (c, c), 1)).astype(F32)
    ps = [-m for m in ms]
    ts = [eye + p for p in ps]
    steps = max(int(math.ceil(math.log2(c))) - 1, 0)
    d = lambda x, y: lax.dot_general(x, y, NN, preferred_element_type=F32)
    for _ in range(steps):
        sp = [_split(p) for p in ps]
        ps = [d(ph, ph) + (d(ph, pl_) + d(pl_, ph)) for ph, pl_ in sp]
        sp = [_split(p) for p in ps]
        st = [_split(t) for t in ts]
        ts = [t + (d(th, ph) + (d(th, pl_) + d(tl, ph))) for t, (th, tl), (ph, pl_) in zip(ts, st, sp)]
    return ts


def _rwkv_body(rw_ref, rw0_ref, s0_ref, mu_ref, vec_ref, w2_ref, a2_ref, g2_ref, seg_ref, rk_ref,
               o_ref, sfin_ref, buf_ref, s_ref, y_ref, *, c, valid, n_chunks):
    ci = pl.program_id(1)
    halo = 8

    @pl.when(ci == 0)
    def _():
        buf_ref[0:halo, :] = rw0_ref[0]
        s_ref[...] = s0_ref[0]

    cur = rw_ref[...]
    buf_ref[halo:halo + c, :] = cur
    prev = buf_ref[halo - 1:halo - 1 + c, :]
    xr = cur + (prev - cur) * mu_ref[...]
    buf_ref[0:halo, :] = cur[c - halo:, :]

    vec = vec_ref[...]
    w0, a0, kkw, kaw, ln_w, ln_b = (vec[r:r + 1, :] for r in range(6))
    r = xr[:, 0:512]
    kx = xr[:, 512:1024]
    v = xr[:, 1024:1536]
    xw = xr[:, 1536:1664]
    xa = xr[:, 1664:1792]
    xg = xr[:, 1792:1920]
    wl = -jnp.exp(-_softplus(-(w0 + _mm(jnp.tanh(xw), w2_ref[...]))) - 0.5)
    a = _sigmoid(a0 + _mm(xa, a2_ref[...]))
    gate = _mm(_sigmoid(xg), g2_ref[...])
    seg = seg_ref[...]
    zk = kx * kkw
    kk = zk * lax.rsqrt(_mm01(seg, zk * zk, left=False) + EPS)
    k2 = kx * (1.0 + (a - 1.0) * kaw)
    bonus = _mm01(seg, r * k2 * rk_ref[...], left=False) * v
    if valid < c:
        live = lax.broadcasted_iota(jnp.int32, (c, 1), 0) < valid
        wl = jnp.where(live, wl, 0.0)
        kk = jnp.where(live, kk, 0.0)
        k2 = jnp.where(live, k2, 0.0)
        v = jnp.where(live, v, 0.0)
        r = jnp.where(live, r, 0.0)
    bb = kk * a

    ri = lax.broadcasted_iota(jnp.int32, (c, c), 0)
    cj = lax.broadcasted_iota(jnp.int32, (c, c), 1)
    tril = ri >= cj
    strict = ri > cj
    cw = _mm01(tril, wl)
    ecw = jnp.exp(cw)
    einv = jnp.exp(-cw)
    p_c = ecw[c - 1:c, :]
    kt = kk * jnp.exp(cw - wl)
    bt = bb * einv
    ki = k2 * einv
    rt = r * ecw
    bd = bt * p_c
    kd = ki * p_c

    heads = range(RWKV_HEADS)
    sls = [slice(h * RWKV_HD, (h + 1) * RWKV_HD) for h in heads]
    kt_h = [kt[:, sl] for sl in sls]
    bt_h = [bt[:, sl] for sl in sls]
    ki_h = [ki[:, sl] for sl in sls]
    rt_h = [rt[:, sl] for sl in sls]
    v_h = [v[:, sl] for sl in sls]
    l_m = [jnp.where(strict, _mm3(kt_h[h], bt_h[h], NT), 0.0) for h in heads]
    m_kk = [jnp.where(strict, _mm(kt_h[h], ki_h[h], NT), 0.0) for h in heads]
    a_rb = [jnp.where(tril, _mm(rt_h[h], bt_h[h], NT), 0.0) for h in heads]
    a_rk = [jnp.where(tril, _mm(rt_h[h], ki_h[h], NT), 0.0) for h in heads]
    mv = [_mm(m_kk[h], v_h[h]) for h in heads]
    y0 = [_mm(a_rk[h], v_h[h]) for h in heads]
    t_inv = _tri_inverse(l_m, c)
    w_h = [_mm3(t_inv[h], kt_h[h]) for h in heads]
    u_h = [-_mm3(t_inv[h], mv[h]) for h in heads]
    s_h = [s_ref[h] for h in heads]
    e_h = [u_h[h] - _mm(w_h[h], s_h[h], NT) for h in heads]
    y1 = [_mm(rt_h[h], s_h[h], NT) + y0[h] for h in heads]
    y_h = [y1[h] + _mm(a_rb[h], e_h[h]) for h in heads]
    ds = [_mm(e_h[h], bd[:, sls[h]], TN) + _mm(v_h[h], kd[:, sls[h]], TN) for h in heads]
    for h in heads:
        s_ref[h] = s_h[h] * p_c[:, sls[h]] + ds[h]
        mu_h = jnp.mean(y_h[h], axis=-1, keepdims=True)
        d_h = y_h[h] - mu_h
        var_h = jnp.mean(d_h * d_h, axis=-1, keepdims=True)
        y_ref[:, sls[h]] = d_h * lax.rsqrt(var_h + RWKV_GN_EPS)

    o_ref[...] = (y_ref[...] * ln_w + ln_b + bonus) * gate

    @pl.when(ci == n_chunks - 1)
    def _():
        sfin_ref[0] = s_ref[...]


def rwkv_mix(rw, rw0, s0, mu, vec, w2, a2, g2, seg, rk, *, c, valid):
    b = s0.shape[0]
    rows = rw.shape[0]
    n_chunks = rows // (b * c)
    const = lambda s: pl.BlockSpec(s, lambda bi, ci: tuple(0 for _ in s))
    return pl.pallas_call(
        functools.partial(_rwkv_body, c=c, valid=valid, n_chunks=n_chunks),
        grid=(b, n_chunks),
        in_specs=[pl.BlockSpec((c, RW_COLS), lambda bi, ci: (bi * n_chunks + ci, 0)),
                  pl.BlockSpec((1, 8, RW_COLS), lambda bi, ci: (bi, 0, 0)),
                  pl.BlockSpec((1, RWKV_HEADS, 64, 64), lambda bi, ci: (bi, 0, 0, 0)),
                  const((1, RW_COLS)), const((8, 512)), const((128, 512)), const((128, 512)), const((128, 512)),
                  const((512, 512)), const((1, 512))],
        out_specs=[pl.BlockSpec((c, 512), lambda bi, ci: (bi * n_chunks + ci, 0)),
                   pl.BlockSpec((1, RWKV_HEADS, 64, 64), lambda bi, ci: (bi, 0, 0, 0))],
        out_shape=[jax.ShapeDtypeStruct((rows, 512), F32),
                   jax.ShapeDtypeStruct((b, RWKV_HEADS, 64, 64), F32)],
        scratch_shapes=[pltpu.VMEM((8 + c, RW_COLS), F32), pltpu.VMEM((RWKV_HEADS, 64, 64), F32),
                        pltpu.VMEM((c, 512), F32)],
        compiler_params=_cparams(("arbitrary", "arbitrary")),
        name="rwkv_mix",
    )(rw, rw0, s0, mu, vec, w2, a2, g2, seg, rk)


def _out_proj_body(*refs, n_in):
    a_refs = refs[:n_in]
    w_refs = refs[n_in:2 * n_in]
    x_ref, g_ref, o_ref = refs[2 * n_in:]
    y = _mm(a_refs[0][...], w_refs[0][...])
    for a_ref, w_ref in zip(a_refs[1:], w_refs[1:]):
        y = y + _mm(a_ref[...], w_ref[...])
    o_ref[...] = x_ref[...] + g_ref[...] * y


def out_proj(acts, weights, x, gate, tm):
    n, d = x.shape
    n_in = len(acts)
    return pl.pallas_call(
        functools.partial(_out_proj_body, n_in=n_in),
        grid=(n // tm,),
        in_specs=[pl.BlockSpec((tm, a.shape[1]), lambda i: (i, 0)) for a in acts]
        + [pl.BlockSpec(w.shape, lambda i: (0, 0)) for w in weights]
        + [pl.BlockSpec((tm, d), lambda i: (i, 0)), _mod_spec(gate.shape[0], tm, d)],
        out_specs=pl.BlockSpec((tm, d), lambda i: (i, 0)),
        out_shape=jax.ShapeDtypeStruct((n, d), F32),
        compiler_params=_cparams(("arbitrary",)),
        name="out_proj",
    )(*acts, *weights, x, gate)


def _odd_proj_body(x_ref, nw_ref, sc_ref, sh_ref, w_ref, qkv_ref, z_ref, ba_ref):
    hb = _norm_mod(x_ref[...], nw_ref[...], sc_ref[...], sh_ref[...]).astype(BF16)
    qkv_ref[...] = _mm(hb, w_ref[:, 0:3072])
    z_ref[...] = _mm(hb, w_ref[:, 3072:4096])
    ba_ref[...] = _mm(hb, w_ref[:, 4096:O_COLS])


def odd_proj(x, nw, sc, sh, w_packed, tm):
    n, d = x.shape
    rows_mod = sc.shape[0]
    row = lambda c: pl.BlockSpec((tm, c), lambda i: (i, 0))
    return pl.pallas_call(
        _odd_proj_body,
        grid=(n // tm,),
        in_specs=[row(d), pl.BlockSpec((1, d), lambda i: (0, 0)),
                  _mod_spec(rows_mod, tm, d), _mod_spec(rows_mod, tm, d),
                  pl.BlockSpec((d, O_COLS), lambda i: (0, 0))],
        out_specs=[row(3072), row(1024), row(128)],
        out_shape=[jax.ShapeDtypeStruct((n, 3072), F32), jax.ShapeDtypeStruct((n, 1024), F32),
                   jax.ShapeDtypeStruct((n, 128), F32)],
        compiler_params=_cparams(("arbitrary",)),
        name="odd_proj",
    )(x, nw, sc, sh, w_packed)


def _gdn_body(qkv_ref, z_ref, ba_ref, cs_ref, s0_ref, cw_ref, hp_ref, nw_ref,
              o_ref, sfin_ref, buf_ref, s_ref, *, c, valid, n_chunks):
    ci = pl.program_id(1)
    halo = 8

    @pl.when(ci == 0)
    def _():
        buf_ref[0:halo, :] = cs_ref[0]
        s_ref[...] = s0_ref[0]

    x = qkv_ref[...]
    buf_ref[halo:halo + c, :] = x
    cw = cw_ref[...]
    conv = buf_ref[halo - 3:halo - 3 + c, :] * cw[0:1, :]
    for j in range(1, CONV_W):
        conv = conv + buf_ref[halo - 3 + j:halo - 3 + j + c, :] * cw[j:j + 1, :]
    buf_ref[0:halo, :] = x[c - halo:, :]
    conv = _silu(conv)

    hp = hp_ref[...]
    ba = ba_ref[...]
    beta_f = _sigmoid(ba)
    g_f = hp[0:1, :] * _softplus(ba + hp[1:2, :])
    if valid < c:
        live = lax.broadcasted_iota(jnp.int32, (c, 1), 0) < valid
        beta_f = jnp.where(live, beta_f, 0.0)
        g_f = jnp.where(live, g_f, 0.0)
        conv = jnp.where(live, conv, 0.0)

    ri = lax.broadcasted_iota(jnp.int32, (c, c), 0)
    cj = lax.broadcasted_iota(jnp.int32, (c, c), 1)
    tril = ri >= cj
    strict = ri > cj
    gc = _mm01(tril, g_f)
    gct = gc.T
    z = z_ref[...]
    nw = nw_ref[...]

    heads = range(GDN_HEADS)
    sls = [slice(h * GDN_HD, (h + 1) * GDN_HD) for h in heads]
    q_h = [conv[:, sl] for sl in sls]
    k_h = [conv[:, GDN_W + h * GDN_HD:GDN_W + (h + 1) * GDN_HD] for h in heads]
    v_h = [conv[:, 2 * GDN_W + h * GDN_HD:2 * GDN_W + (h + 1) * GDN_HD] for h in heads]
    q_h = [q * lax.rsqrt(jnp.sum(q * q, axis=-1, keepdims=True) + EPS) * (GDN_HD ** -0.5) for q in q_h]
    k_h = [k * lax.rsqrt(jnp.sum(k * k, axis=-1, keepdims=True) + EPS) for k in k_h]
    g_col = [gc[:, 8 + h:9 + h] for h in heads]
    eg = [jnp.exp(g) for g in g_col]
    b_col = [beta_f[:, h:h + 1] for h in heads]
    decay = [jnp.where(tril, jnp.exp(jnp.where(tril, g_col[h] - gct[8 + h:9 + h, :], 0.0)), 0.0) for h in heads]
    kb = [k_h[h] * b_col[h] for h in heads]
    vb = [v_h[h] * b_col[h] for h in heads]
    m_h = [jnp.where(strict, _mm3(kb[h], k_h[h], NT) * decay[h], 0.0) for h in heads]
    qk = [jnp.where(tril, _mm(q_h[h], k_h[h], NT) * decay[h], 0.0) for h in heads]
    t_inv = _tri_inverse(m_h, c)
    u_h = [_mm(t_inv[h], vb[h]) for h in heads]
    w_h = [_mm(t_inv[h], kb[h] * eg[h]) for h in heads]
    s_h = [s_ref[h] for h in heads]
    v_new = [u_h[h] - _mm(w_h[h], s_h[h]) for h in heads]
    o1 = [_mm(q_h[h] * eg[h], s_h[h]) for h in heads]
    o_h = [o1[h] + _mm(qk[h], v_new[h]) for h in heads]
    g_last = [g[c - 1:c, :] for g in g_col]
    ds = [_mm(k_h[h] * jnp.exp(g_last[h] - g_col[h]), v_new[h], TN) for h in heads]
    for h in heads:
        s_ref[h] = s_h[h] * jnp.exp(g_last[h]) + ds[h]
        o_n = o_h[h] * lax.rsqrt(jnp.mean(o_h[h] * o_h[h], axis=-1, keepdims=True) + EPS) * nw
        o_ref[:, sls[h]] = o_n * _silu(z[:, sls[h]])

    @pl.when(ci == n_chunks - 1)
    def _():
        sfin_ref[0] = s_ref[...]


def gdn_mix(qkv, z, ba, cs, s0, conv_w8, hp, nw, *, c, valid):
    b = s0.shape[0]
    rows = qkv.shape[0]
    n_chunks = rows // (b * c)
    const = lambda s: pl.BlockSpec(s, lambda bi, ci: tuple(0 for _ in s))
    row = lambda w: pl.BlockSpec((c, w), lambda bi, ci: (bi * n_chunks + ci, 0))
    return pl.pallas_call(
        functools.partial(_gdn_body, c=c, valid=valid, n_chunks=n_chunks),
        grid=(b, n_chunks),
        in_specs=[row(3072), row(1024), row(128),
                  pl.BlockSpec((1, 8, 3072), lambda bi, ci: (bi, 0, 0)),
                  pl.BlockSpec((1, GDN_HEADS, 128, 128), lambda bi, ci: (bi, 0, 0, 0)),
                  const((8, 3072)), const((8, 128)), const((1, 128))],
        out_specs=[row(1024), pl.BlockSpec((1, GDN_HEADS, 128, 128), lambda bi, ci: (bi, 0, 0, 0))],
        out_shape=[jax.ShapeDtypeStruct((rows, 1024), F32),
                   jax.ShapeDtypeStruct((b, GDN_HEADS, 128, 128), F32)],
        scratch_shapes=[pltpu.VMEM((8 + c, 3072), F32), pltpu.VMEM((GDN_HEADS, 128, 128), F32)],
        compiler_params=_cparams(("arbitrary", "arbitrary")),
        name="gdn_mix",
    )(qkv, z, ba, cs, s0, conv_w8, hp, nw)


def _router_body(x_ref, nw_ref, sc_ref, sh_ref, wr_ref, br_ref, h_ref, gate_ref):
    h = _norm_mod(x_ref[...], nw_ref[...], sc_ref[...], sh_ref[...])
    h_ref[...] = h.astype(BF16)
    logits = _mmh(h, wr_ref[...]) + br_ref[...]
    tm = logits.shape[0]
    lane = lax.broadcasted_iota(jnp.int32, (tm, LANE), 1)
    is_grp = (lane >= N_EXPERTS) & (lane < N_EXPERTS + N_GROUPS)
    gl = jnp.where(is_grp, logits, NEG)
    gmax = jnp.max(gl, axis=-1, keepdims=True)
    g_idx = jnp.min(jnp.where(gl == gmax, lane, 4 * LANE), axis=-1, keepdims=True) - N_EXPERTS
    g_w = 1.0 / jnp.sum(jnp.where(is_grp, jnp.exp(gl - gmax), 0.0), axis=-1, keepdims=True)
    in_grp = (lane < N_EXPERTS) & (lane // EXP_PER_GROUP == g_idx)
    el = jnp.where(in_grp, logits, NEG)
    emax = jnp.max(el, axis=-1, keepdims=True)
    e = jnp.where(in_grp, jnp.exp(el - emax), 0.0)
    p = e / jnp.sum(e, axis=-1, keepdims=True)
    p1 = jnp.where(in_grp, p, -1.0)
    m1 = jnp.max(p1, axis=-1, keepdims=True)
    i1 = jnp.min(jnp.where(p1 == m1, lane, 4 * LANE), axis=-1, keepdims=True)
    p2 = jnp.where(lane == i1, -1.0, p1)
    m2 = jnp.max(p2, axis=-1, keepdims=True)
    i2 = jnp.min(jnp.where(p2 == m2, lane, 4 * LANE), axis=-1, keepdims=True)
    tot = m1 + m2
    gate_ref[...] = jnp.where(lane == i1, m1 / tot * g_w, jnp.where(lane == i2, m2 / tot * g_w, 0.0))


def moe_router(x, nw, sc, sh, w_r, b_r, tm):
    n, d = x.shape
    rows_mod = sc.shape[0]
    return pl.pallas_call(
        _router_body,
        grid=(n // tm,),
        in_specs=[pl.BlockSpec((tm, d), lambda i: (i, 0)), pl.BlockSpec((1, d), lambda i: (0, 0)),
                  _mod_spec(rows_mod, tm, d), _mod_spec(rows_mod, tm, d),
                  pl.BlockSpec((d, LANE), lambda i: (0, 0)), pl.BlockSpec((1, LANE), lambda i: (0, 0))],
        out_specs=[pl.BlockSpec((tm, d), lambda i: (i, 0)), pl.BlockSpec((tm, LANE), lambda i: (i, 0))],
        out_shape=[jax.ShapeDtypeStruct((n, d), BF16), jax.ShapeDtypeStruct((n, LANE), F32)],
        compiler_params=_cparams(("arbitrary",)),
        name="moe_router",
    )(x, nw, sc, sh, w_r, b_r)


def _moe_body(h_ref, gate_ref, w1_ref, w3_ref, w2_ref, x_ref, g2_ref, o_ref, acc_ref):
    e = pl.program_id(1)

    @pl.when(e == 0)
    def _():
        acc_ref[...] = jnp.zeros_like(acc_ref)

    hb = h_ref[...]
    he = _silu(_mm(hb, w1_ref[0])) * _mm(hb, w3_ref[0])
    y = _mm(he, w2_ref[0])
    gate = gate_ref[...]
    lane = lax.broadcasted_iota(jnp.int32, gate.shape, 1)
    ge = jnp.sum(jnp.where(lane == e, gate, 0.0), axis=-1, keepdims=True)
    acc_ref[...] += ge * y

    @pl.when(e == pl.num_programs(1) - 1)
    def _():
        o_ref[...] = x_ref[...] + g2_ref[...] * acc_ref[...]


def moe_ffn(h, gate, w1, w3, w2, x, g2, tm):
    n, d = x.shape
    ne, _, de = w1.shape
    return pl.pallas_call(
        _moe_body,
        grid=(n // tm, ne),
        in_specs=[pl.BlockSpec((tm, d), lambda i, e: (i, 0)), pl.BlockSpec((tm, LANE), lambda i, e: (i, 0)),
                  pl.BlockSpec((1, d, de), lambda i, e: (e, 0, 0)), pl.BlockSpec((1, d, de), lambda i, e: (e, 0, 0)),
                  pl.BlockSpec((1, de, d), lambda i, e: (e, 0, 0)),
                  pl.BlockSpec((tm, d), lambda i, e: (i, 0)),
                  pl.BlockSpec((1, d), lambda i, e: (0, 0)) if g2.shape[0] == 1
                  else pl.BlockSpec((tm, d), lambda i, e: (i, 0))],
        out_specs=pl.BlockSpec((tm, d), lambda i, e: (i, 0)),
        out_shape=jax.ShapeDtypeStruct((n, d), F32),
        scratch_shapes=[pltpu.VMEM((tm, d), F32)],
        compiler_params=_cparams(("arbitrary", "arbitrary")),
        name="moe_ffn",
    )(h, gate, w1, w3, w2, x, g2)


def _final_norm_body(x_ref, w_ref, o_ref):
    x = x_ref[...]
    o_ref[...] = x * lax.rsqrt(jnp.mean(x * x, axis=-1, keepdims=True) + EPS) * w_ref[...]


def final_norm(x, w, tm):
    n, d = x.shape
    return pl.pallas_call(
        _final_norm_body,
        grid=(n // tm,),
        in_specs=[pl.BlockSpec((tm, d), lambda i: (i, 0)), pl.BlockSpec((1, d), lambda i: (0, 0))],
        out_specs=pl.BlockSpec((tm, d), lambda i: (i, 0)),
        out_shape=jax.ShapeDtypeStruct((n, d), F32),
        compiler_params=_cparams(("arbitrary",)),
        name="final_norm",
    )(x, w)


def _row_tile(n, pref):
    t = min(pref, n)
    while n % t:
        t //= 2
    return t


def kernel(x_prompt, x_sample, c_prompt, c_sample, cache_nsa_cmp, cache_nsa_sel, page_table, state_nsa_win, state_rwkv, state_rwkv_shift, state_gdn, state_gdn_conv, norm_mix, norm_ffn, norm_final, w_ada, b_ada, even_w_in, even_w_out, nsa_cmp_pos, nsa_cmp_w, rwkv_mu, rwkv_w0, rwkv_w2, rwkv_a0, rwkv_a2, rwkv_g2, rwkv_kk, rwkv_ka, rwkv_rk, rwkv_ln_w, rwkv_ln_b, odd_w_in, odd_w_out, gdn_conv_w, gdn_a_log, gdn_dt_bias, gdn_norm_w, moe_w_grp, moe_b_grp, moe_w_exp, moe_b_exp, moe_w1, moe_w3, moe_w2):
    bp, t, d = x_prompt.shape
    bs, ts, _ = x_sample.shape
    assert bp == 1 and ts <= SPAD and ts < CMP_BLK
    depth = norm_mix.shape[0]
    n_pages, page = page_table.shape[1], cache_nsa_cmp.shape[2]
    past = n_pages * page
    wb = state_nsa_win.shape[2]
    ns = bs * SPAD
    tq, tq_s, tk = 128, 32, 512
    tm_p = _row_tile(t, 512)
    tm_s = ns

    rows_c = -(-(1 + bs) // 8) * 8
    c_all = jnp.concatenate([c_prompt, c_sample, jnp.zeros((rows_c - 1 - bs, d), F32)], axis=0)
    ada = adaln(c_all, w_ada, b_ada)

    def mods(i):
        mp = [ada[i, 0:1, j * d:(j + 1) * d] for j in range(6)]
        ms = [jnp.repeat(ada[i, 1:1 + bs, j * d:(j + 1) * d], SPAD, axis=0) for j in range(6)]
        return mp, ms

    xp = x_prompt[0]
    xs = jnp.pad(x_sample, ((0, 0), (0, SPAD - ts), (0, 0))).reshape(ns, d)

    def unpad(a):
        return a.reshape(bs, SPAD, -1)[:, :ts]

    outs = {k: [] for k in ("cmp_p", "cmp_s", "sel_p", "sel_s", "win_p", "win_s", "rw_p", "rw_s", "sh_p", "sh_s",
                            "gd_p", "gd_s", "cv_p", "cv_s")}

    for i in range(depth):
        (sh1p, sc1p, gt1p, sh2p, sc2p, gt2p), (sh1s, sc1s, gt1s, sh2s, sc2s, gt2s) = mods(i)
        j = i // 2
        nw = norm_mix[i][None, :]
        if i % 2 == 0:
            w_packed = _pack_even_w(even_w_in[j])
            mu = _pack_rw_vec(rwkv_mu[j])
            wts, wc = _cmp_weights(nsa_cmp_pos[j], nsa_cmp_w[j])
            vec = jnp.stack([rwkv_w0[j], rwkv_a0[j], rwkv_kk[j], rwkv_ka[j], rwkv_ln_w[j], rwkv_ln_b[j],
                             jnp.zeros_like(rwkv_w0[j]), jnp.zeros_like(rwkv_w0[j])])
            pad_lora = lambda w: jnp.concatenate([w, jnp.zeros((128 - w.shape[0], w.shape[1]), w.dtype)], axis=0)
            w2p, a2p, g2p = pad_lora(rwkv_w2[j]), pad_lora(rwkv_a2[j]), rwkv_g2[j]
            hid = jnp.arange(RWKV_W) // RWKV_HD
            seg = (hid[:, None] == hid[None, :]).astype(F32)
            rk = rwkv_rk[j].reshape(1, RWKV_W)
            wo_nsa, wo_rw = even_w_out[j][:512].astype(BF16), even_w_out[j][512:].astype(BF16)

            kv, qt, gt, ks, vst, kw, vwt, rw, hl = even_proj(xp, nw, sc1p, sh1p, w_packed, tm_p, 8)
            kvc = compress_prompt(kv, wts, wc, _row_tile(t, 512))
            gates = gt[:24].reshape(NSA_KV_HEADS, 12, t)
            gates = jnp.pad(gates, ((0, 0), (0, 4), (0, 0)))[None]
            o_nsa = nsa_attention(
                qt[None], gates, kvc[None], kvc.T[None], ks[None], vst[None], kw[None], vwt[None],
                tq=tq, tk=tk, wk=WINDOW + tq,
                pos0_fn=lambda qi: qi * tq,
                wstart_fn=lambda qi: jnp.maximum(qi * tq - WINDOW, 0),
                wpos0_fn=lambda qi: jnp.maximum(qi * tq - WINDOW, 0))[0]
            o_rw, s_rw = rwkv_mix(rw, jnp.zeros((1, 8, RW_COLS), F32), jnp.zeros((1, RWKV_HEADS, 64, 64), F32),
                                  mu, vec, w2p, a2p, g2p, seg, rk, c=64, valid=64)
            xp = out_proj([o_nsa, o_rw], [wo_nsa, wo_rw], xp, gt1p, tm_p)
            outs["cmp_p"].append(kv[:, 0:256].reshape(1, t, 2, 2, 64))
            outs["sel_p"].append(kv[:, 256:512].reshape(1, t, 2, 2, 64))
            kvw_rows = kv[:, 512:768].reshape(1, t, 2, 2, 64)
            outs["win_p"].append(kvw_rows[:, -min(WINDOW, t):])
            outs["rw_p"].append(s_rw)
            outs["sh_p"].append(hl[-1:])

            kv, qt, gt, _, _, _, _, rw, hl = even_proj(xs, nw, sc1s, sh1s, w_packed, tm_s, ns)
            kv_new = unpad(kv)
            rw0 = small_matmul(jnp.pad(state_rwkv_shift[j], ((0, -bs % 8), (0, 0))), w_packed[:, E_RW:])[:bs]
            rw0 = jnp.pad(rw0[:, None, :], ((0, 0), (7, 0), (0, 0)))
            pool_cmp = cache_nsa_cmp[j].transpose(0, 2, 3, 4, 1).reshape(-1, 256, page)
            pool_sel = cache_nsa_sel[j].transpose(0, 2, 3, 4, 1).reshape(-1, 256, page)
            kvc_s = compress_paged(pool_cmp, page_table, nsa_cmp_pos[j], wc, 8 if n_pages % 8 == 0 else 1)
            tail = jnp.pad(kv_new[:, :, 256:512], ((0, 0), (0, tk - ts), (0, 0)))
            ks_s, vst_s = gather_sel(pool_sel, page_table, tail, tk)
            wbuf = state_nsa_win[j].reshape(bs, wb, 256)
            kvw_all = jnp.concatenate([wbuf, kv_new[:, :, 512:768]], axis=1)
            wk_s = -(-(wb + ts) // 128) * 128
            kvw_pad = jnp.pad(kvw_all, ((0, 0), (0, wk_s - wb - ts), (0, 0)))
            kw_s = kvw_pad[:, :, :128].astype(BF16)
            vwt_s = jnp.swapaxes(kvw_pad[:, :, 128:], 1, 2).astype(BF16)
            qt_s = jnp.pad(qt.reshape(512, bs, SPAD).transpose(1, 0, 2), ((0, 0), (0, 0), (0, tq_s - SPAD)))
            g_s = gt[:24].reshape(NSA_KV_HEADS, 12, bs, SPAD).transpose(2, 0, 1, 3)
            g_s = jnp.pad(g_s, ((0, 0), (0, 0), (0, 4), (0, tq_s - SPAD)))
            o_nsa = nsa_attention(
                qt_s, g_s, kvc_s, jnp.swapaxes(kvc_s, 1, 2), ks_s, vst_s, kw_s, vwt_s,
                tq=tq_s, tk=tk, wk=wk_s,
                pos0_fn=lambda qi: past,
                wstart_fn=lambda qi: 0,
                wpos0_fn=lambda qi: past - wb)
            o_nsa = o_nsa[:, :SPAD].reshape(ns, 512)
            o_rw, s_rw = rwkv_mix(rw, rw0, state_rwkv[j], mu, vec, w2p, a2p, g2p, seg, rk, c=SPAD, valid=ts)
            xs = out_proj([o_nsa, o_rw], [wo_nsa, wo_rw], xs, gt1s, tm_s)
            outs["cmp_s"].append(kv_new[:, :, 0:256].reshape(bs, ts, 2, 2, 64))
            outs["sel_s"].append(kv_new[:, :, 256:512].reshape(bs, ts, 2, 2, 64))
            outs["win_s"].append(kvw_all[:, -wb:].reshape(bs, wb, 2, 2, 64))
            outs["rw_s"].append(s_rw)
            outs["sh_s"].append(hl.reshape(bs, SPAD, d)[:, ts - 1])
        else:
            w_in = odd_w_in[j]
            w_packed = jnp.concatenate([w_in, jnp.zeros((d, O_COLS - w_in.shape[1]), F32)], axis=1).astype(BF16)
            conv_w8 = jnp.pad(gdn_conv_w[j], ((0, 8 - CONV_W), (0, 0)))
            hp = jnp.zeros((8, 128), F32)
            hp = hp.at[0, 8:16].set(-jnp.exp(gdn_a_log[j])).at[1, 8:16].set(gdn_dt_bias[j])
            gnw = gdn_norm_w[j][None, :]
            wo = odd_w_out[j].astype(BF16)

            qkv, z, ba = odd_proj(xp, nw, sc1p, sh1p, w_packed, tm_p)
            o_g, s_g = gdn_mix(qkv, z, ba, jnp.zeros((1, 8, 3 * GDN_W), F32),
                               jnp.zeros((1, GDN_HEADS, GDN_HD, GDN_HD), F32), conv_w8, hp, gnw, c=64, valid=64)
            xp = out_proj([o_g], [wo], xp, gt1p, tm_p)
            outs["gd_p"].append(s_g)
            outs["cv_p"].append(qkv[None, -(CONV_W - 1):])

            qkv, z, ba = odd_proj(xs, nw, sc1s, sh1s, w_packed, tm_s)
            cs = jnp.pad(state_gdn_conv[j], ((0, 0), (8 - (CONV_W - 1), 0), (0, 0)))
            o_g, s_g = gdn_mix(qkv, z, ba, cs, state_gdn[j], conv_w8, hp, gnw, c=SPAD, valid=ts)
            xs = out_proj([o_g], [wo], xs, gt1s, tm_s)
            xpad = jnp.concatenate([state_gdn_conv[j], unpad(qkv)], axis=1)
            outs["gd_s"].append(s_g)
            outs["cv_s"].append(xpad[:, -(CONV_W - 1):])

        nwf = norm_ffn[i][None, :]
        w_r = jnp.concatenate([moe_w_exp[i], moe_w_grp[i], jnp.zeros((d, LANE - N_EXPERTS - N_GROUPS), F32)], axis=1)
        b_r = jnp.concatenate([moe_b_exp[i], moe_b_grp[i], jnp.zeros((LANE - N_EXPERTS - N_GROUPS,), F32)])[None, :]
        h2, gate = moe_router(xp, nwf, sc2p, sh2p, w_r, b_r, tm_p)
        xp = moe_ffn(h2, gate, moe_w1[i], moe_w3[i], moe_w2[i], xp, gt2p, _row_tile(t, 1024))
        h2, gate = moe_router(xs, nwf, sc2s, sh2s, w_r, b_r, tm_s)
        xs = moe_ffn(h2, gate, moe_w1[i], moe_w3[i], moe_w2[i], xs, gt2s, tm_s)

    nf = norm_final[None, :]
    y_prompt = final_norm(xp, nf, tm_p)[None]
    y_sample = unpad(final_norm(xs, nf, tm_s))
    st = lambda key: jnp.stack(outs[key])
    return (y_prompt, y_sample, st("cmp_p"), st("cmp_s"), st("sel_p"), st("sel_s"), st("win_p"), st("win_s"),
            st("rw_p"), st("rw_s"), st("sh_p"), st("sh_s"), st("gd_p"), st("gd_s"), st("cv_p"), st("cv_s"))
```

```python
import functools
import math

import jax
import jax.numpy as jnp
from jax import lax
from jax.experimental import pallas as pl
from jax.experimental.pallas import tpu as pltpu

F32 = jnp.float32
BF16 = jnp.bfloat16
HIGHEST = lax.Precision.HIGHEST

NSA_HEADS = 8
NSA_KV_HEADS = 2
NSA_GROUP = 4
NSA_HD = 64
CMP_BLK = 64
SEL_BLK = 64
TOPK_BLK = 16
WINDOW = 512
FORCE_BONUS = 2.0 * NSA_GROUP
RWKV_HEADS = 8
RWKV_HD = 64
RWKV_W = 512
RWKV_GN_EPS = 64e-5
GDN_HEADS = 8
GDN_HD = 128
GDN_W = 1024
CONV_W = 4
N_GROUPS = 4
EXP_PER_GROUP = 8
N_EXPERTS = 32
EPS = 1e-6
NEG = -1e30

LANE = 128
GRP_LANE = 64
ROW_ALIGN = 16
SPAD = 8
VMEM_LIMIT = 56 * 1024 * 1024

NN = (((1,), (0,)), ((), ()))
NT = (((1,), (1,)), ((), ()))
TN = (((0,), (0,)), ((), ()))

E_Q, E_KV, E_G, E_RW = 0, 512, 1280, 1408
E_COLS = 1408 + 1920
RW_COLS = 1920
O_COLS = 3072 + 1024 + 128


def _mm(a, b, dims=NN):
    return lax.dot_general(a.astype(BF16), b.astype(BF16), dims, preferred_element_type=F32)


def _mmh(a, b, dims=NN):
    return lax.dot_general(a.astype(F32), b.astype(F32), dims, precision=HIGHEST, preferred_element_type=F32)


def _split(a):
    hi = a.astype(BF16)
    return hi, (a - hi.astype(F32)).astype(BF16)


def _mm3(a, b, dims=NN):
    ah, al = _split(a)
    bh, bl = _split(b)
    d = lambda x, y: lax.dot_general(x, y, dims, preferred_element_type=F32)
    return d(ah, bh) + (d(ah, bl) + d(al, bh))


def _mm01(m01, x, dims=NN, left=True):
    h1 = x.astype(BF16)
    r1 = x - h1.astype(F32)
    h2 = r1.astype(BF16)
    h3 = (r1 - h2.astype(F32)).astype(BF16)
    m = m01.astype(BF16)
    if left:
        d = lambda y: lax.dot_general(m, y, dims, preferred_element_type=F32)
    else:
        d = lambda y: lax.dot_general(y, m, dims, preferred_element_type=F32)
    return d(h1) + (d(h2) + d(h3))


def _sigmoid(x):
    return 1.0 / (1.0 + jnp.exp(-x))


def _silu(x):
    return x * _sigmoid(x)


def _softplus(x):
    return jnp.maximum(x, 0.0) + jnp.log(1.0 + jnp.exp(-jnp.abs(x)))


def _cparams(sem):
    return pltpu.CompilerParams(dimension_semantics=sem, vmem_limit_bytes=VMEM_LIMIT)


def _norm_mod(x, nw, sc, sh):
    y = x * lax.rsqrt(jnp.mean(x * x, axis=-1, keepdims=True) + EPS)
    return (y * nw) * (1.0 + sc) + sh


def _mod_spec(rows_mod, tm, d):
    if rows_mod == 1:
        return pl.BlockSpec((1, d), lambda i: (0, 0))
    return pl.BlockSpec((tm, d), lambda i: (i, 0))


def _adaln_body(c_ref, w_ref, b_ref, o_ref):
    o_ref[0] = _mmh(_silu(c_ref[...]), w_ref[0]) + b_ref[0]


def adaln(c_all, w_ada, b_ada):
    depth, d, n6 = w_ada.shape
    rows = c_all.shape[0]
    tn = 768
    return pl.pallas_call(
        _adaln_body,
        grid=(depth, n6 // tn),
        in_specs=[pl.BlockSpec((rows, d), lambda l, j: (0, 0)),
                  pl.BlockSpec((1, d, tn), lambda l, j: (l, 0, j)),
                  pl.BlockSpec((1, 1, tn), lambda l, j: (l, 0, j))],
        out_specs=pl.BlockSpec((1, rows, tn), lambda l, j: (l, 0, j)),
        out_shape=jax.ShapeDtypeStruct((depth, rows, n6), F32),
        compiler_params=_cparams(("arbitrary", "arbitrary")),
        name="adaln",
    )(c_all, w_ada, b_ada.reshape(depth, 1, n6))


def _even_proj_body(x_ref, nw_ref, sc_ref, sh_ref, w_ref,
                    kv_ref, qt_ref, gt_ref, ks_ref, vst_ref, kw_ref, vwt_ref, rw_ref, hl_ref):
    h = _norm_mod(x_ref[...], nw_ref[...], sc_ref[...], sh_ref[...])
    hl = hl_ref.shape[0]
    hl_ref[...] = h[h.shape[0] - hl:, :]
    hb = h.astype(BF16)
    q = _mm(hb, w_ref[:, E_Q:E_Q + 512]) * (NSA_HD ** -0.5)
    qt_ref[...] = q.T.astype(BF16)
    kv = _mm(hb, w_ref[:, E_KV:E_KV + 768])
    kv_ref[...] = kv
    ks_ref[...] = kv[:, 256:384].astype(BF16)
    vst_ref[...] = kv[:, 384:512].T.astype(BF16)
    kw_ref[...] = kv[:, 512:640].astype(BF16)
    vwt_ref[...] = kv[:, 640:768].T.astype(BF16)
    g = _sigmoid(_mm(hb, w_ref[:, E_G:E_G + 128]))
    gt_ref[...] = g.T
    rw_ref[...] = _mm(hb, w_ref[:, E_RW:E_RW + RW_COLS])


def even_proj(x, nw, sc, sh, w_packed, tm, hl_rows):
    n, d = x.shape
    rows_mod = sc.shape[0]
    row = lambda c: pl.BlockSpec((tm, c), lambda i: (i, 0))
    col = lambda r: pl.BlockSpec((r, tm), lambda i: (0, i))
    return pl.pallas_call(
        _even_proj_body,
        grid=(n // tm,),
        in_specs=[row(d), pl.BlockSpec((1, d), lambda i: (0, 0)),
                  _mod_spec(rows_mod, tm, d), _mod_spec(rows_mod, tm, d),
                  pl.BlockSpec((d, E_COLS), lambda i: (0, 0))],
        out_specs=[row(768), col(512), col(128), row(128), col(128), row(128), col(128), row(RW_COLS),
                   pl.BlockSpec((hl_rows, d), lambda i: (0, 0))],
        out_shape=[jax.ShapeDtypeStruct((n, 768), F32),
                   jax.ShapeDtypeStruct((512, n), BF16),
                   jax.ShapeDtypeStruct((128, n), F32),
                   jax.ShapeDtypeStruct((n, 128), BF16),
                   jax.ShapeDtypeStruct((128, n), BF16),
                   jax.ShapeDtypeStruct((n, 128), BF16),
                   jax.ShapeDtypeStruct((128, n), BF16),
                   jax.ShapeDtypeStruct((n, RW_COLS), F32),
                   jax.ShapeDtypeStruct((hl_rows, d), F32)],
        compiler_params=_cparams(("arbitrary",)),
        name="even_proj",
    )(x, nw, sc, sh, w_packed)


def _pack_even_w(w_in):
    d = w_in.shape[0]
    z = lambda c: jnp.zeros((d, c), w_in.dtype)
    nsa = 1304
    rw = w_in[:, nsa:]
    parts = [w_in[:, :1280], w_in[:, 1280:1304], z(104),
             rw[:, :1536], rw[:, 1536:1600], z(64), rw[:, 1600:1664], z(64), rw[:, 1664:1792]]
    return jnp.concatenate(parts, axis=1).astype(BF16)


def _pack_rw_vec(v):
    z = jnp.zeros((64,), v.dtype)
    return jnp.concatenate([v[:1536], v[1536:1600], z, v[1600:1664], z, v[1664:1792]])[None, :]


def _mm_body(x_ref, w_ref, o_ref):
    o_ref[...] = _mm(x_ref[...], w_ref[...])


def small_matmul(x, w):
    return pl.pallas_call(
        _mm_body,
        out_shape=jax.ShapeDtypeStruct((x.shape[0], w.shape[1]), F32),
        compiler_params=pltpu.CompilerParams(vmem_limit_bytes=VMEM_LIMIT),
        name="small_matmul",
    )(x, w)


def _compress_body(x_ref, wts_ref, wc_ref, o_ref):
    x = x_ref[...]
    nb = x.shape[0] // CMP_BLK
    pooled = jnp.sum(x.reshape(nb, CMP_BLK, x.shape[-1]) * wts_ref[...][None], axis=1)
    o_ref[...] = _mm(pooled, wc_ref[...])


def _compress_paged_body(pt_ref, *refs, pps):
    page_refs = refs[:pps]
    wp_ref, wc_ref, o_ref = refs[pps:]
    x = jnp.concatenate([r[0] for r in page_refs], axis=1)
    pooled_t = jnp.concatenate([_mm(x[0:128], wp_ref[0]), _mm(x[128:256], wp_ref[1])], axis=0)
    nb = o_ref.shape[1]
    o_ref[0] = _mm(pooled_t.T[:nb], wc_ref[...])


def _cmp_weights(pos_wts, w_c):
    wts = jnp.repeat(pos_wts.T, 128, axis=1)
    eye2 = jnp.eye(2, dtype=w_c.dtype)
    blocks = [jnp.kron(eye2, w_c[c]) for c in range(2)]
    z = jnp.zeros((128, 128), w_c.dtype)
    wc = jnp.concatenate([jnp.concatenate([blocks[0], z], axis=1),
                          jnp.concatenate([z, blocks[1]], axis=1)], axis=0)
    return wts, wc


def compress_prompt(kv, wts, wc, tr):
    t = kv.shape[0]
    nb = tr // CMP_BLK
    return pl.pallas_call(
        _compress_body,
        grid=(t // tr,),
        in_specs=[pl.BlockSpec((tr, 256), lambda i: (i, 0)),
                  pl.BlockSpec((CMP_BLK, 256), lambda i: (0, 0)),
                  pl.BlockSpec((256, 256), lambda i: (0, 0))],
        out_specs=pl.BlockSpec((nb, 256), lambda i: (i, 0)),
        out_shape=jax.ShapeDtypeStruct((t // CMP_BLK, 256), F32),
        compiler_params=_cparams(("arbitrary",)),
        name="compress_prompt",
    )(kv, wts, wc)


def compress_paged(pool_t, page_table, pos_wts, wc, pages_per_step):
    b, n_pages = page_table.shape
    page = pool_t.shape[2]
    pps = pages_per_step
    nb = pps * page // CMP_BLK
    p_idx = jnp.arange(pps * page)
    wp = jax.nn.one_hot(p_idx // CMP_BLK, LANE, dtype=F32)[None] * pos_wts[:, p_idx % CMP_BLK][:, :, None]

    def page_spec(u):
        return pl.BlockSpec((1, 256, page), lambda bi, g, pt: (pt[bi, g * pps + u], 0, 0))

    grid_spec = pltpu.PrefetchScalarGridSpec(
        num_scalar_prefetch=1,
        grid=(b, n_pages // pps),
        in_specs=[page_spec(u) for u in range(pps)] + [
            pl.BlockSpec((2, pps * page, LANE), lambda bi, g, pt: (0, 0, 0)),
            pl.BlockSpec((256, 256), lambda bi, g, pt: (0, 0))],
        out_specs=pl.BlockSpec((1, nb, 256), lambda bi, g, pt: (bi, g, 0)),
    )
    return pl.pallas_call(
        functools.partial(_compress_paged_body, pps=pps),
        grid_spec=grid_spec,
        out_shape=jax.ShapeDtypeStruct((b, n_pages * page // CMP_BLK, 256), F32),
        compiler_params=_cparams(("arbitrary", "arbitrary")),
        name="compress_paged",
    )(page_table, *([pool_t] * pps), wp, wc)


def _gather_sel_body(pt_ref, *refs, pps, n_page_steps):
    page_refs = refs[:pps]
    tail_ref, ks_ref, vst_ref = refs[pps:]
    g = pl.program_id(1)

    @pl.when(g < n_page_steps)
    def _():
        ks_ref[0] = jnp.concatenate([r[0][0:128].T for r in page_refs], axis=0).astype(BF16)
        vst_ref[0] = jnp.concatenate([r[0][128:256] for r in page_refs], axis=1).astype(BF16)

    @pl.when(g >= n_page_steps)
    def _():
        x = tail_ref[0]
        ks_ref[0] = x[:, :128].astype(BF16)
        vst_ref[0] = x[:, 128:].T.astype(BF16)


def gather_sel(pool_t, page_table, tail, tk):
    b, n_pages = page_table.shape
    page = pool_t.shape[2]
    pps = tk // page
    n_page_steps = n_pages // pps
    nk = n_pages * page + tk

    def page_spec(u):
        return pl.BlockSpec((1, 256, page),
                            lambda bi, g, pt: (pt[bi, jnp.minimum(g * pps + u, n_pages - 1)], 0, 0))

    grid_spec = pltpu.PrefetchScalarGridSpec(
        num_scalar_prefetch=1,
        grid=(b, n_page_steps + 1),
        in_specs=[page_spec(u) for u in range(pps)] + [pl.BlockSpec((1, tk, 256), lambda bi, g, pt: (bi, 0, 0))],
        out_specs=[pl.BlockSpec((1, tk, 128), lambda bi, g, pt: (bi, g, 0)),
                   pl.BlockSpec((1, 128, tk), lambda bi, g, pt: (bi, 0, g))],
    )
    return pl.pallas_call(
        functools.partial(_gather_sel_body, pps=pps, n_page_steps=n_page_steps),
        grid_spec=grid_spec,
        out_shape=[jax.ShapeDtypeStruct((b, nk, 128), BF16), jax.ShapeDtypeStruct((b, 128, nk), BF16)],
        compiler_params=_cparams(("arbitrary", "arbitrary")),
        name="gather_sel",
    )(page_table, *([pool_t] * pps), tail)


MASKED = -1e30
M_INIT = -1e29


def _nsa_query(qt_ref, k, tq):
    w4 = NSA_GROUP * tq
    qb = qt_ref[0].astype(F32)
    qcat = jnp.concatenate([qb[g * 64:(g + 1) * 64] for g in range(NSA_GROUP)], axis=1)
    q2 = jnp.concatenate([qcat, qcat], axis=0)
    row = lax.broadcasted_iota(jnp.int32, (128, w4), 0)
    qe = jnp.where(row // 64 == k, q2, 0.0)
    gidx = lax.broadcasted_iota(jnp.int32, (128, w4), 1) // tq
    base = jnp.where(k == 0, 0.5, 0.5 / 16.0)
    slope = base * jnp.where(gidx == 0, 1.0, jnp.where(gidx == 1, 0.5, jnp.where(gidx == 2, 0.25, 0.125)))
    mult = jnp.where(row == 0, 16.0, jnp.where(row == 1, 1.0, jnp.where(row == 2, 128.0,
                                                                         jnp.where(row == 3, 64.0, 0.0))))
    return jnp.concatenate([qe, slope * mult], axis=0).astype(BF16)


def _pos_features(rows, tile_rel):
    r = lax.broadcasted_iota(jnp.int32, (rows, LANE), 0)
    lane = lax.broadcasted_iota(jnp.int32, (rows, LANE), 1)
    ab = jnp.where(lane == 0, r // 16, jnp.where(lane == 1, r % 16, 0)).astype(F32)
    return jnp.where(lane == 2, tile_rel, ab).astype(BF16)


def _gate_rows(gb, j, tq):
    return jnp.concatenate([gb[g * 3 + j:g * 3 + j + 1, :] for g in range(NSA_GROUP)], axis=1)


def _nsa_select_body(qt_ref, g_ref, kvc_ref, kvct_ref, kw_ref, vwt_ref, part_ref, sel_ref, flag_ref, *,
                     tq, tk, wk, nbc, nb, pos0_fn, wstart_fn, wpos0_fn):
    i = pl.program_id(1)
    k = pl.program_id(2)
    w4 = NSA_GROUP * tq
    pos0 = pos0_fn(i)
    qa = _nsa_query(qt_ref, k, tq)
    pos_q = pos0 + lax.broadcasted_iota(jnp.int32, (1, w4), 1) % tq

    def softmax_cols(s, bad):
        s = jnp.where(bad, MASKED, s)
        m = jnp.maximum(jnp.max(s, axis=0, keepdims=True), M_INIT)
        e = jnp.exp(s - m)
        return e / jnp.maximum(jnp.sum(e, axis=0, keepdims=True), 1e-30)

    n_i = lax.broadcasted_iota(jnp.int32, (nbc, LANE), 0)
    lane_c = lax.broadcasted_iota(jnp.int32, (nbc, LANE), 1)
    feat_c = jnp.where(lane_c == 3, n_i - pos0 // CMP_BLK, 0).astype(F32).astype(BF16)
    kc = jnp.concatenate([kvc_ref[0][:, :128].astype(BF16), feat_c], axis=1)
    c_end = lax.broadcasted_iota(jnp.int32, (nbc, 1), 0) * CMP_BLK + (CMP_BLK - 1)
    p_c = softmax_cols(lax.dot_general(kc, qa, NN, preferred_element_type=F32), c_end > pos_q)
    vct = kvct_ref[0, pl.ds(pl.multiple_of(128 + k * 64, 64), 64), :]
    o_c = _mm(vct, p_c)

    imp = p_c[:, 0:tq]
    for g in range(1, NSA_GROUP):
        imp = imp + p_c[:, g * tq:(g + 1) * tq]
    if nb > nbc:
        imp = jnp.concatenate([imp, jnp.zeros((nb - nbc, tq), F32)], axis=0)
    blk = lax.broadcasted_iota(jnp.int32, (nb, tq), 0)
    cur = (pos0 + lax.broadcasted_iota(jnp.int32, (1, tq), 1)) // SEL_BLK
    forced = (blk == cur) | (blk == cur - 1) | (blk == 0)
    score = jnp.where(blk <= cur, imp + jnp.where(forced, FORCE_BONUS, 0.0), -1.0)
    sel = jnp.zeros((nb, tq), F32)
    for _ in range(min(TOPK_BLK, nb)):
        m = jnp.max(score, axis=0, keepdims=True)
        first = jnp.min(jnp.where(score == m, blk, nb), axis=0, keepdims=True)
        hit = blk == first
        sel = jnp.where(hit, 1.0, sel)
        score = jnp.where(hit, -2.0, score)
    sel_ref[0, 0] = sel
    bpt = tk // SEL_BLK
    any_row = jnp.max(sel, axis=1, keepdims=True)
    flag_ref[0, 0] = jnp.max(any_row.reshape(nb // bpt, bpt, 1), axis=1)

    wstart = wstart_fn(i)
    if not isinstance(wstart, int):
        wstart = pl.multiple_of(wstart, 128)
    wpos0 = wpos0_fn(i)
    tile_rel = jnp.asarray((wpos0 - pos0) // 128, F32)
    kw = jnp.concatenate([kw_ref[0, pl.ds(wstart, wk), :], _pos_features(wk, tile_rel)], axis=1)
    dist_w = pos_q - (wpos0 + lax.broadcasted_iota(jnp.int32, (wk, 1), 0))
    p_w = softmax_cols(lax.dot_general(kw, qa, NN, preferred_element_type=F32), (dist_w < 0) | (dist_w >= WINDOW))
    vwin = vwt_ref[0, pl.ds(pl.multiple_of(k * 64, 64), 64), pl.ds(wstart, wk)]
    o_w = _mm(vwin, p_w)

    gb = g_ref[0, 0]
    part_ref[0, 0] = _gate_rows(gb, 0, tq) * o_c + _gate_rows(gb, 2, tq) * o_w


def nsa_select(qt, gates, kvc, kvct, kw, vwt, *, nb, tq, tk, wk, pos0_fn, wstart_fn, wpos0_fn):
    b, _, nq = qt.shape
    nbc = kvc.shape[1]
    nw = kw.shape[1]
    nqt = nq // tq
    nt = nb * SEL_BLK // tk
    w4 = NSA_GROUP * tq
    assert nbc <= 256 and tk <= 512 and wk <= 1024
    body = functools.partial(_nsa_select_body, tq=tq, tk=tk, wk=wk, nbc=nbc, nb=nb, pos0_fn=pos0_fn,
                             wstart_fn=wstart_fn, wpos0_fn=wpos0_fn)
    full = lambda s1, s2: pl.BlockSpec((1, s1, s2), lambda bi, i, k: (bi, 0, 0))
    step = lambda s1, s2: pl.BlockSpec((1, 1, s1, s2), lambda bi, i, k: (bi, i * NSA_KV_HEADS + k, 0, 0))
    return pl.pallas_call(
        body,
        grid=(b, nqt, NSA_KV_HEADS),
        in_specs=[pl.BlockSpec((1, 256, tq), lambda bi, i, k: (bi, k, i)),
                  pl.BlockSpec((1, 1, 16, tq), lambda bi, i, k: (bi, k, 0, i)),
                  full(nbc, 256), full(256, nbc), full(nw, 128), full(128, nw)],
        out_specs=[step(64, w4), step(nb, tq), step(nt, 1)],
        out_shape=[jax.ShapeDtypeStruct((b, nqt * 2, 64, w4), F32),
                   jax.ShapeDtypeStruct((b, nqt * 2, nb, tq), F32),
                   jax.ShapeDtypeStruct((b, nqt * 2, nt, 1), F32)],
        compiler_params=_cparams(("arbitrary", "arbitrary", "arbitrary")),
        name="nsa_select",
    )(qt, gates, kvc, kvct, kw, vwt)


def _nsa_selected_body(list_ref, cnt_ref, qt_ref, g_ref, sel_ref, ks_ref, vst_ref, part_ref, o_ref, *,
                       tq, tk, nt, pos0_fn):
    bi = pl.program_id(0)
    i = pl.program_id(1)
    k = pl.program_id(2)
    step = (bi * pl.num_programs(1) + i) * NSA_KV_HEADS + k
    w4 = NSA_GROUP * tq
    pos0 = pos0_fn(i)
    qa = _nsa_query(qt_ref, k, tq)
    pos_q = pos0 + lax.broadcasted_iota(jnp.int32, (1, w4), 1) % tq
    bpt = tk // SEL_BLK
    row_k = lax.broadcasted_iota(jnp.int32, (tk, 1), 0)
    r = lax.broadcasted_iota(jnp.int32, (tk, LANE), 0)
    lane = lax.broadcasted_iota(jnp.int32, (tk, LANE), 1)
    feat_ab = jnp.where(lane == 0, r // 16, jnp.where(lane == 1, r % 16, 0)).astype(F32)

    def kv_step(jj, carry):
        m_i, l_i, acc = carry
        j = list_ref[step * nt + jj]
        off = pl.multiple_of(j * tk, tk)
        tile_rel = ((off - pos0) // 128).astype(F32)
        feat = jnp.where(lane == 2, tile_rel, feat_ab).astype(BF16)
        kj = jnp.concatenate([ks_ref[0, pl.ds(off, tk), :], feat], axis=1)
        s = lax.dot_general(kj, qa, NN, preferred_element_type=F32)
        selb = (sel_ref[0, 0, pl.ds(pl.multiple_of(j * bpt, bpt), bpt), :] - 1.0) * (-MASKED)
        selb = jnp.concatenate([selb] * NSA_GROUP, axis=1)
        s = s + jnp.broadcast_to(selb[:, None, :], (bpt, SEL_BLK, w4)).reshape(tk, w4)
        s = jnp.where(row_k > pos_q - off, MASKED, s)
        m_new = jnp.maximum(m_i, jnp.max(s, axis=0, keepdims=True))
        p = jnp.exp(s - m_new)
        alpha = jnp.exp(m_i - m_new)
        l_new = l_i * alpha + jnp.sum(p, axis=0, keepdims=True)
        vj = vst_ref[0, pl.ds(pl.multiple_of(k * 64, 64), 64), pl.ds(off, tk)]
        return m_new, l_new, acc * alpha + _mm(vj, p)

    init = (jnp.full((1, w4), M_INIT, F32), jnp.zeros((1, w4), F32), jnp.zeros((64, w4), F32))
    _, l_s, acc_s = lax.fori_loop(0, cnt_ref[step], kv_step, init)
    o_s = acc_s / jnp.maximum(l_s, 1e-30)
    o_t = part_ref[0, 0] + _gate_rows(g_ref[0, 0], 1, tq) * o_s
    o_ref[0] = jnp.concatenate([o_t[:, g * tq:(g + 1) * tq].T for g in range(NSA_GROUP)], axis=1)


def nsa_selected(tile_list, tile_cnt, qt, gates, sel, ks, vst, part, *, tq, tk, pos0_fn):
    b, _, nq = qt.shape
    nk = ks.shape[1]
    nb = sel.shape[2]
    nt = nk // tk
    w4 = NSA_GROUP * tq
    full = lambda s1, s2: pl.BlockSpec((1, s1, s2), lambda bi, i, k, *_: (bi, 0, 0))
    step = lambda s1, s2: pl.BlockSpec((1, 1, s1, s2), lambda bi, i, k, *_: (bi, i * NSA_KV_HEADS + k, 0, 0))
    grid_spec = pltpu.PrefetchScalarGridSpec(
        num_scalar_prefetch=2,
        grid=(b, nq // tq, NSA_KV_HEADS),
        in_specs=[pl.BlockSpec((1, 256, tq), lambda bi, i, k, *_: (bi, k, i)),
                  pl.BlockSpec((1, 1, 16, tq), lambda bi, i, k, *_: (bi, k, 0, i)),
                  step(nb, tq), full(nk, 128), full(128, nk), step(64, w4)],
        out_specs=pl.BlockSpec((1, tq, 256), lambda bi, i, k, *_: (bi, i, k)),
    )
    return pl.pallas_call(
        functools.partial(_nsa_selected_body, tq=tq, tk=tk, nt=nt, pos0_fn=pos0_fn),
        grid_spec=grid_spec,
        out_shape=jax.ShapeDtypeStruct((b, nq, 512), F32),
        compiler_params=_cparams(("arbitrary", "arbitrary", "arbitrary")),
        name="nsa_selected",
    )(tile_list, tile_cnt, qt, gates, sel, ks, vst, part)


def nsa_attention(qt, gates, kvc, kvct, ks, vst, kw, vwt, *, tq, tk, wk, pos0_fn, wstart_fn, wpos0_fn):
    nb = ks.shape[1] // SEL_BLK
    part, sel, flags = nsa_select(qt, gates, kvc, kvct, kw, vwt, nb=nb, tq=tq, tk=tk, wk=wk, pos0_fn=pos0_fn,
                                  wstart_fn=wstart_fn, wpos0_fn=wpos0_fn)
    active = flags[..., 0] > 0.5
    order = jnp.argsort(jnp.where(active, 0, 1), axis=-1, stable=True).astype(jnp.int32)
    cnt = jnp.sum(active, axis=-1).astype(jnp.int32)
    return nsa_selected(order.reshape(-1), cnt.reshape(-1), qt, gates, sel, ks, vst, part,
                        tq=tq, tk=tk, pos0_fn=pos0_fn)


def _tri_inverse(ms, c):
    eye = (lax.broadcasted_iota(jnp.int32, (c, c), 0) == lax.broadcasted_iota(jnp.int32, (c, c), 1)).astype(F32)
    ps = [-m for m in ms]
    ts = [eye + p for p in ps]
    steps = max(int(math.ceil(math.log2(c))) - 1, 0)
    d = lambda x, y: lax.dot_general(x, y, NN, preferred_element_type=F32)
    for _ in range(steps):
        sp = [_split(p) for p in ps]
        ps = [d(ph, ph) + (d(ph, pl_) + d(pl_, ph)) for ph, pl_ in sp]
        sp = [_split(p) for p in ps]
        st = [_split(t) for t in ts]
        ts = [t + (d(th, ph) + (d(th, pl_) + d(tl, ph))) for t, (th, tl), (ph, pl_) in zip(ts, st, sp)]
    return ts


def _rwkv_body(rw_ref, rw0_ref, s0_ref, mu_ref, vec_ref, w2_ref, a2_ref, g2_ref, seg_ref, rk_ref,
               o_ref, sfin_ref, buf_ref, s_ref, y_ref, *, c, valid, n_chunks):
    ci = pl.program_id(1)
    halo = 8

    @pl.when(ci == 0)
    def _():
        buf_ref[0:halo, :] = rw0_ref[0]
        s_ref[...] = s0_ref[0]

    cur = rw_ref[...]
    buf_ref[halo:halo + c, :] = cur
    prev = buf_ref[halo - 1:halo - 1 + c, :]
    xr = cur + (prev - cur) * mu_ref[...]
    buf_ref[0:halo, :] = cur[c - halo:, :]

    vec = vec_ref[...]
    w0, a0, kkw, kaw, ln_w, ln_b = (vec[r:r + 1, :] for r in range(6))
    r = xr[:, 0:512]
    kx = xr[:, 512:1024]
    v = xr[:, 1024:1536]
    xw = xr[:, 1536:1664]
    xa = xr[:, 1664:1792]
    xg = xr[:, 1792:1920]
    wl = -jnp.exp(-_softplus(-(w0 + _mm(jnp.tanh(xw), w2_ref[...]))) - 0.5)
    a = _sigmoid(a0 + _mm(xa, a2_ref[...]))
    gate = _mm(_sigmoid(xg), g2_ref[...])
    seg = seg_ref[...]
    zk = kx * kkw
    kk = zk * lax.rsqrt(_mm01(seg, zk * zk, left=False) + EPS)
    k2 = kx * (1.0 + (a - 1.0) * kaw)
    bonus = _mm01(seg, r * k2 * rk_ref[...], left=False) * v
    if valid < c:
        live = lax.broadcasted_iota(jnp.int32, (c, 1), 0) < valid
        wl = jnp.where(live, wl, 0.0)
        kk = jnp.where(live, kk, 0.0)
        k2 = jnp.where(live, k2, 0.0)
        v = jnp.where(live, v, 0.0)
        r = jnp.where(live, r, 0.0)
    bb = kk * a

    ri = lax.broadcasted_iota(jnp.int32, (c, c), 0)
    cj = lax.broadcasted_iota(jnp.int32, (c, c), 1)
    tril = ri >= cj
    strict = ri > cj
    cw = _mm01(tril, wl)
    ecw = jnp.exp(cw)
    einv = jnp.exp(-cw)
    p_c = ecw[c - 1:c, :]
    kt = kk * jnp.exp(cw - wl)
    bt = bb * einv
    ki = k2 * einv
    rt = r * ecw
    bd = bt * p_c
    kd = ki * p_c

    heads = range(RWKV_HEADS)
    sls = [slice(h * RWKV_HD, (h + 1) * RWKV_HD) for h in heads]
    kt_h = [kt[:, sl] for sl in sls]
    bt_h = [bt[:, sl] for sl in sls]
    ki_h = [ki[:, sl] for sl in sls]
    rt_h = [rt[:, sl] for sl in sls]
    v_h = [v[:, sl] for sl in sls]
    l_m = [jnp.where(strict, _mm3(kt_h[h], bt_h[h], NT), 0.0) for h in heads]
    m_kk = [jnp.where(strict, _mm(kt_h[h], ki_h[h], NT), 0.0) for h in heads]
    a_rb = [jnp.where(tril, _mm(rt_h[h], bt_h[h], NT), 0.0) for h in heads]
    a_rk = [jnp.where(tril, _mm(rt_h[h], ki_h[h], NT), 0.0) for h in heads]
    mv = [_mm(m_kk[h], v_h[h]) for h in heads]
    y0 = [_mm(a_rk[h], v_h[h]) for h in heads]
    t_inv = _tri_inverse(l_m, c)
    w_h = [_mm3(t_inv[h], kt_h[h]) for h in heads]
    u_h = [-_mm3(t_inv[h], mv[h]) for h in heads]
    s_h = [s_ref[h] for h in heads]
    e_h = [u_h[h] - _mm(w_h[h], s_h[h], NT) for h in heads]
    y1 = [_mm(rt_h[h], s_h[h], NT) + y0[h] for h in heads]
    y_h = [y1[h] + _mm(a_rb[h], e_h[h]) for h in heads]
    ds = [_mm(e_h[h], bd[:, sls[h]], TN) + _mm(v_h[h], kd[:, sls[h]], TN) for h in heads]
    for h in heads:
        s_ref[h] = s_h[h] * p_c[:, sls[h]] + ds[h]
        mu_h = jnp.mean(y_h[h], axis=-1, keepdims=True)
        d_h = y_h[h] - mu_h
        var_h = jnp.mean(d_h * d_h, axis=-1, keepdims=True)
        y_ref[:, sls[h]] = d_h * lax.rsqrt(var_h + RWKV_GN_EPS)

    o_ref[...] = (y_ref[...] * ln_w + ln_b + bonus) * gate

    @pl.when(ci == n_chunks - 1)
    def _():
        sfin_ref[0] = s_ref[...]


def rwkv_mix(rw, rw0, s0, mu, vec, w2, a2, g2, seg, rk, *, c, valid):
    b = s0.shape[0]
    rows = rw.shape[0]
    n_chunks = rows // (b * c)
    const = lambda s: pl.BlockSpec(s, lambda bi, ci: tuple(0 for _ in s))
    return pl.pallas_call(
        functools.partial(_rwkv_body, c=c, valid=valid, n_chunks=n_chunks),
        grid=(b, n_chunks),
        in_specs=[pl.BlockSpec((c, RW_COLS), lambda bi, ci: (bi * n_chunks + ci, 0)),
                  pl.BlockSpec((1, 8, RW_COLS), lambda bi, ci: (bi, 0, 0)),
                  pl.BlockSpec((1, RWKV_HEADS, 64, 64), lambda bi, ci: (bi, 0, 0, 0)),
                  const((1, RW_COLS)), const((8, 512)), const((128, 512)), const((128, 512)), const((128, 512)),
                  const((512, 512)), const((1, 512))],
        out_specs=[pl.BlockSpec((c, 512), lambda bi, ci: (bi * n_chunks + ci, 0)),
                   pl.BlockSpec((1, RWKV_HEADS, 64, 64), lambda bi, ci: (bi, 0, 0, 0))],
        out_shape=[jax.ShapeDtypeStruct((rows, 512), F32),
                   jax.ShapeDtypeStruct((b, RWKV_HEADS, 64, 64), F32)],
        scratch_shapes=[pltpu.VMEM((8 + c, RW_COLS), F32), pltpu.VMEM((RWKV_HEADS, 64, 64), F32),
                        pltpu.VMEM((c, 512), F32)],
        compiler_params=_cparams(("arbitrary", "arbitrary")),
        name="rwkv_mix",
    )(rw, rw0, s0, mu, vec, w2, a2, g2, seg, rk)


def _out_proj_body(*refs, n_in):
    a_refs = refs[:n_in]
    w_refs = refs[n_in:2 * n_in]
    x_ref, g_ref, o_ref = refs[2 * n_in:]
    y = _mm(a_refs[0][...], w_refs[0][...])
    for a_ref, w_ref in zip(a_refs[1:], w_refs[1:]):
        y = y + _mm(a_ref[...], w_ref[...])
    o_ref[...] = x_ref[...] + g_ref[...] * y


def out_proj(acts, weights, x, gate, tm):
    n, d = x.shape
    n_in = len(acts)
    return pl.pallas_call(
        functools.partial(_out_proj_body, n_in=n_in),
        grid=(n // tm,),
        in_specs=[pl.BlockSpec((tm, a.shape[1]), lambda i: (i, 0)) for a in acts]
        + [pl.BlockSpec(w.shape, lambda i: (0, 0)) for w in weights]
        + [pl.BlockSpec((tm, d), lambda i: (i, 0)), _mod_spec(gate.shape[0], tm, d)],
        out_specs=pl.BlockSpec((tm, d), lambda i: (i, 0)),
        out_shape=jax.ShapeDtypeStruct((n, d), F32),
        compiler_params=_cparams(("arbitrary",)),
        name="out_proj",
    )(*acts, *weights, x, gate)


def _odd_proj_body(x_ref, nw_ref, sc_ref, sh_ref, w_ref, qkv_ref, z_ref, ba_ref):
    hb = _norm_mod(x_ref[...], nw_ref[...], sc_ref[...], sh_ref[...]).astype(BF16)
    qkv_ref[...] = _mm(hb, w_ref[:, 0:3072])
    z_ref[...] = _mm(hb, w_ref[:, 3072:4096])
    ba_ref[...] = _mm(hb, w_ref[:, 4096:O_COLS])


def odd_proj(x, nw, sc, sh, w_packed, tm):
    n, d = x.shape
    rows_mod = sc.shape[0]
    row = lambda c: pl.BlockSpec((tm, c), lambda i: (i, 0))
    return pl.pallas_call(
        _odd_proj_body,
        grid=(n // tm,),
        in_specs=[row(d), pl.BlockSpec((1, d), lambda i: (0, 0)),
                  _mod_spec(rows_mod, tm, d), _mod_spec(rows_mod, tm, d),
                  pl.BlockSpec((d, O_COLS), lambda i: (0, 0))],
        out_specs=[row(3072), row(1024), row(128)],
        out_shape=[jax.ShapeDtypeStruct((n, 3072), F32), jax.ShapeDtypeStruct((n, 1024), F32),
                   jax.ShapeDtypeStruct((n, 128), F32)],
        compiler_params=_cparams(("arbitrary",)),
        name="odd_proj",
    )(x, nw, sc, sh, w_packed)


def _gdn_body(qkv_ref, z_ref, ba_ref, cs_ref, s0_ref, cw_ref, hp_ref, nw_ref,
              o_ref, sfin_ref, buf_ref, s_ref, *, c, valid, n_chunks):
    ci = pl.program_id(1)
    halo = 8

    @pl.when(ci == 0)
    def _():
        buf_ref[0:halo, :] = cs_ref[0]
        s_ref[...] = s0_ref[0]

    x = qkv_ref[...]
    buf_ref[halo:halo + c, :] = x
    cw = cw_ref[...]
    conv = buf_ref[halo - 3:halo - 3 + c, :] * cw[0:1, :]
    for j in range(1, CONV_W):
        conv = conv + buf_ref[halo - 3 + j:halo - 3 + j + c, :] * cw[j:j + 1, :]
    buf_ref[0:halo, :] = x[c - halo:, :]
    conv = _silu(conv)

    hp = hp_ref[...]
    ba = ba_ref[...]
    beta_f = _sigmoid(ba)
    g_f = hp[0:1, :] * _softplus(ba + hp[1:2, :])
    if valid < c:
        live = lax.broadcasted_iota(jnp.int32, (c, 1), 0) < valid
        beta_f = jnp.where(live, beta_f, 0.0)
        g_f = jnp.where(live, g_f, 0.0)
        conv = jnp.where(live, conv, 0.0)

    ri = lax.broadcasted_iota(jnp.int32, (c, c), 0)
    cj = lax.broadcasted_iota(jnp.int32, (c, c), 1)
    tril = ri >= cj
    strict = ri > cj
    gc = _mm01(tril, g_f)
    gct = gc.T
    z = z_ref[...]
    nw = nw_ref[...]

    heads = range(GDN_HEADS)
    sls = [slice(h * GDN_HD, (h + 1) * GDN_HD) for h in heads]
    q_h = [conv[:, sl] for sl in sls]
    k_h = [conv[:, GDN_W + h * GDN_HD:GDN_W + (h + 1) * GDN_HD] for h in heads]
    v_h = [conv[:, 2 * GDN_W + h * GDN_HD:2 * GDN_W + (h + 1) * GDN_HD] for h in heads]
    q_h = [q * lax.rsqrt(jnp.sum(q * q, axis=-1, keepdims=True) + EPS) * (GDN_HD ** -0.5) for q in q_h]
    k_h = [k * lax.rsqrt(jnp.sum(k * k, axis=-1, keepdims=True) + EPS) for k in k_h]
    g_col = [gc[:, 8 + h:9 + h] for h in heads]
    eg = [jnp.exp(g) for g in g_col]
    b_col = [beta_f[:, h:h + 1] for h in heads]
    decay = [jnp.where(tril, jnp.exp(jnp.where(tril, g_col[h] - gct[8 + h:9 + h, :], 0.0)), 0.0) for h in heads]
    kb = [k_h[h] * b_col[h] for h in heads]
    vb = [v_h[h] * b_col[h] for h in heads]
    m_h = [jnp.where(strict, _mm3(kb[h], k_h[h], NT) * decay[h], 0.0) for h in heads]
    qk = [jnp.where(tril, _mm(q_h[h], k_h[h], NT) * decay[h], 0.0) for h in heads]
    t_inv = _tri_inverse(m_h, c)
    u_h = [_mm(t_inv[h], vb[h]) for h in heads]
    w_h = [_mm(t_inv[h], kb[h] * eg[h]) for h in heads]
    s_h = [s_ref[h] for h in heads]
    v_new = [u_h[h] - _mm(w_h[h], s_h[h]) for h in heads]
    o1 = [_mm(q_h[h] * eg[h], s_h[h]) for h in heads]
    o_h = [o1[h] + _mm(qk[h], v_new[h]) for h in heads]
    g_last = [g[c - 1:c, :] for g in g_col]
    ds = [_mm(k_h[h] * jnp.exp(g_last[h] - g_col[h]), v_new[h], TN) for h in heads]
    for h in heads:
        s_ref[h] = s_h[h] * jnp.exp(g_last[h]) + ds[h]
        o_n = o_h[h] * lax.rsqrt(jnp.mean(o_h[h] * o_h[h], axis=-1, keepdims=True) + EPS) * nw
        o_ref[:, sls[h]] = o_n * _silu(z[:, sls[h]])

    @pl.when(ci == n_chunks - 1)
    def _():
        sfin_ref[0] = s_ref[...]


def gdn_mix(qkv, z, ba, cs, s0, conv_w8, hp, nw, *, c, valid):
    b = s0.shape[0]
    rows = qkv.shape[0]
    n_chunks = rows // (b * c)
    const = lambda s: pl.BlockSpec(s, lambda bi, ci: tuple(0 for _ in s))
    row = lambda w: pl.BlockSpec((c, w), lambda bi, ci: (bi * n_chunks + ci, 0))
    return pl.pallas_call(
        functools.partial(_gdn_body, c=c, valid=valid, n_chunks=n_chunks),
        grid=(b, n_chunks),
        in_specs=[row(3072), row(1024), row(128),
                  pl.BlockSpec((1, 8, 3072), lambda bi, ci: (bi, 0, 0)),
                  pl.BlockSpec((1, GDN_HEADS, 128, 128), lambda bi, ci: (bi, 0, 0, 0)),
                  const((8, 3072)), const((8, 128)), const((1, 128))],
        out_specs=[row(1024), pl.BlockSpec((1, GDN_HEADS, 128, 128), lambda bi, ci: (bi, 0, 0, 0))],
        out_shape=[jax.ShapeDtypeStruct((rows, 1024), F32),
                   jax.ShapeDtypeStruct((b, GDN_HEADS, 128, 128), F32)],
        scratch_shapes=[pltpu.VMEM((8 + c, 3072), F32), pltpu.VMEM((GDN_HEADS, 128, 128), F32)],
        compiler_params=_cparams(("arbitrary", "arbitrary")),
        name="gdn_mix",
    )(qkv, z, ba, cs, s0, conv_w8, hp, nw)


def _router_body(x_ref, nw_ref, sc_ref, sh_ref, wr_ref, br_ref, h_ref, gate_ref):
    h = _norm_mod(x_ref[...], nw_ref[...], sc_ref[...], sh_ref[...])
    h_ref[...] = h.astype(BF16)
    logits = _mmh(h, wr_ref[...]) + br_ref[...]
    tm = logits.shape[0]
    lane = lax.broadcasted_iota(jnp.int32, (tm, LANE), 1)
    is_grp = (lane >= N_EXPERTS) & (lane < N_EXPERTS + N_GROUPS)
    gl = jnp.where(is_grp, logits, NEG)
    gmax = jnp.max(gl, axis=-1, keepdims=True)
    g_idx = jnp.min(jnp.where(gl == gmax, lane, 4 * LANE), axis=-1, keepdims=True) - N_EXPERTS
    g_w = 1.0 / jnp.sum(jnp.where(is_grp, jnp.exp(gl - gmax), 0.0), axis=-1, keepdims=True)
    in_grp = (lane < N_EXPERTS) & (lane // EXP_PER_GROUP == g_idx)
    el = jnp.where(in_grp, logits, NEG)
    emax = jnp.max(el, axis=-1, keepdims=True)
    e = jnp.where(in_grp, jnp.exp(el - emax), 0.0)
    p = e / jnp.sum(e, axis=-1, keepdims=True)
    p1 = jnp.where(in_grp, p, -1.0)
    m1 = jnp.max(p1, axis=-1, keepdims=True)
    i1 = jnp.min(jnp.where(p1 == m1, lane, 4 * LANE), axis=-1, keepdims=True)
    p2 = jnp.where(lane == i1, -1.0, p1)
    m2 = jnp.max(p2, axis=-1, keepdims=True)
    i2 = jnp.min(jnp.where(p2 == m2, lane, 4 * LANE), axis=-1, keepdims=True)
    tot = m1 + m2
    gate = jnp.where(lane == i1, m1 / tot * g_w, jnp.where(lane == i2, m2 / tot * g_w, 0.0))
    gate_ref[...] = jnp.where(lane == GRP_LANE, g_idx.astype(F32), gate)


def moe_router(x, nw, sc, sh, w_r, b_r, tm):
    n, d = x.shape
    rows_mod = sc.shape[0]
    return pl.pallas_call(
        _router_body,
        grid=(n // tm,),
        in_specs=[pl.BlockSpec((tm, d), lambda i: (i, 0)), pl.BlockSpec((1, d), lambda i: (0, 0)),
                  _mod_spec(rows_mod, tm, d), _mod_spec(rows_mod, tm, d),
                  pl.BlockSpec((d, LANE), lambda i: (0, 0)), pl.BlockSpec((1, LANE), lambda i: (0, 0))],
        out_specs=[pl.BlockSpec((tm, d), lambda i: (i, 0)), pl.BlockSpec((tm, LANE), lambda i: (i, 0))],
        out_shape=[jax.ShapeDtypeStruct((n, d), BF16), jax.ShapeDtypeStruct((n, LANE), F32)],
        compiler_params=_cparams(("arbitrary",)),
        name="moe_router",
    )(x, nw, sc, sh, w_r, b_r)


def _moe_body(h_ref, gate_ref, w1_ref, w3_ref, w2_ref, x_ref, g2_ref, o_ref, acc_ref):
    e = pl.program_id(1)

    @pl.when(e == 0)
    def _():
        acc_ref[...] = jnp.zeros_like(acc_ref)

    hb = h_ref[...]
    he = _silu(_mm(hb, w1_ref[0])) * _mm(hb, w3_ref[0])
    y = _mm(he, w2_ref[0])
    gate = gate_ref[...]
    lane = lax.broadcasted_iota(jnp.int32, gate.shape, 1)
    ge = jnp.sum(jnp.where(lane == e, gate, 0.0), axis=-1, keepdims=True)
    acc_ref[...] += ge * y

    @pl.when(e == pl.num_programs(1) - 1)
    def _():
        o_ref[...] = x_ref[...] + g2_ref[...] * acc_ref[...]


def moe_ffn(h, gate, w1, w3, w2, x, g2, tm):
    n, d = x.shape
    ne, _, de = w1.shape
    return pl.pallas_call(
        _moe_body,
        grid=(n // tm, ne),
        in_specs=[pl.BlockSpec((tm, d), lambda i, e: (i, 0)), pl.BlockSpec((tm, LANE), lambda i, e: (i, 0)),
                  pl.BlockSpec((1, d, de), lambda i, e: (e, 0, 0)), pl.BlockSpec((1, d, de), lambda i, e: (e, 0, 0)),
                  pl.BlockSpec((1, de, d), lambda i, e: (e, 0, 0)),
                  pl.BlockSpec((tm, d), lambda i, e: (i, 0)),
                  pl.BlockSpec((1, d), lambda i, e: (0, 0)) if g2.shape[0] == 1
                  else pl.BlockSpec((tm, d), lambda i, e: (i, 0))],
        out_specs=pl.BlockSpec((tm, d), lambda i, e: (i, 0)),
        out_shape=jax.ShapeDtypeStruct((n, d), F32),
        scratch_shapes=[pltpu.VMEM((tm, d), F32)],
        compiler_params=_cparams(("arbitrary", "arbitrary")),
        name="moe_ffn",
    )(h, gate, w1, w3, w2, x, g2)


def _moe_plan(grp, tm, tb, cap, max_entries):
    nt = grp.shape[0] // tm
    cnt = jax.nn.one_hot(grp, N_GROUPS, dtype=jnp.int32).reshape(nt, tm, N_GROUPS).sum(axis=1)
    pc = (cnt + ROW_ALIGN - 1) // ROW_ALIGN * ROW_ALIGN
    segb = jnp.cumsum(pc, axis=1) - pc
    off = jnp.cumsum(pc, axis=0) - pc
    tot = pc.sum(axis=0)
    n_real = (tot + tb - 1) // tb
    n_all = jnp.minimum((tot + tm + tb - 1) // tb, cap // tb)
    ends = jnp.cumsum(n_all)
    s = jnp.arange(max_entries)
    g_of = jnp.sum(s[:, None] >= ends[None, :], axis=1)
    active = g_of < N_GROUPS
    g_c = jnp.minimum(g_of, N_GROUPS - 1)
    rt = s - (ends - n_all)[g_c]
    kind = jnp.where(active, jnp.where(rt < n_real[g_c], 1, 2), 0)
    last = ends[-1] - 1
    e_grp = jnp.where(active, g_c, g_c[last])
    e_rt = jnp.where(active, rt, rt[last])
    i32 = lambda a: a.reshape(-1).astype(jnp.int32)
    return i32(segb), i32(off // ROW_ALIGN), i32(e_grp), i32(e_rt), i32(kind)


def _group_perm(gate, segb_ref, base, tm, rows):
    gt = gate.T
    grp = gt[GRP_LANE:GRP_LANE + 1, :]
    gi = lax.broadcasted_iota(jnp.int32, (8, tm), 0).astype(F32)
    oh = jnp.where(gi == grp, 1.0, 0.0)
    r_i = lax.broadcasted_iota(jnp.int32, (tm, tm), 0)
    c_i = lax.broadcasted_iota(jnp.int32, (tm, tm), 1)
    before = jnp.where(r_i < c_i, 1.0, 0.0).astype(BF16)
    rank = lax.dot_general(oh.astype(BF16), before, NN, preferred_element_type=F32)
    dest = jnp.zeros((1, tm), F32)
    for g in range(N_GROUPS):
        dest = dest + oh[g:g + 1] * (segb_ref[base + g].astype(F32) + rank[g:g + 1])
    rows_i = lax.broadcasted_iota(jnp.int32, (rows, tm), 0).astype(F32)
    return jnp.where(rows_i == dest, 1.0, 0.0).astype(BF16)


def _moe_dispatch_body(segb_ref, off_ref, h_ref, gate_ref, xg_in, gg_in, xg_ref, gg_ref, xs_ref, gs_ref, *, tm, rows):
    del off_ref, xg_in, gg_in
    i = pl.program_id(0)
    g = pl.program_id(1)

    @pl.when((i == 0) & (g == 0))
    def _():
        xs_ref[...] = jnp.zeros_like(xs_ref)
        gs_ref[...] = jnp.zeros_like(gs_ref)

    @pl.when(g == 0)
    def _():
        gate = gate_ref[...]
        p = _group_perm(gate, segb_ref, i * N_GROUPS, tm, rows)
        xs_ref[0:rows, :] = lax.dot_general(p, h_ref[...], NN, preferred_element_type=F32).astype(BF16)
        gs_ref[0:rows, :] = _mm01(p, gate)

    start = pl.multiple_of(segb_ref[i * N_GROUPS + g], ROW_ALIGN)
    xg_ref[...] = xs_ref[pl.ds(start, tm), :]
    gg_ref[...] = gs_ref[pl.ds(start, tm), :]


def moe_dispatch(h, gate, segb, off, tm, cap):
    n, d = h.shape
    rows = tm + N_GROUPS * ROW_ALIGN
    win = lambda w: pl.BlockSpec((pl.Element(tm), pl.Element(w)),
                                 lambda i, g, segb, off: ((g * (cap // ROW_ALIGN) + off[i * N_GROUPS + g]) * ROW_ALIGN, 0))
    grid_spec = pltpu.PrefetchScalarGridSpec(
        num_scalar_prefetch=2,
        grid=(n // tm, N_GROUPS),
        in_specs=[pl.BlockSpec((tm, d), lambda i, g, *_: (i, 0)), pl.BlockSpec((tm, LANE), lambda i, g, *_: (i, 0)),
                  pl.BlockSpec(memory_space=pl.ANY), pl.BlockSpec(memory_space=pl.ANY)],
        out_specs=[win(d), win(LANE)],
        scratch_shapes=[pltpu.VMEM((2 * tm + N_GROUPS * ROW_ALIGN, d), BF16),
                        pltpu.VMEM((2 * tm + N_GROUPS * ROW_ALIGN, LANE), F32)],
    )
    return pl.pallas_call(
        functools.partial(_moe_dispatch_body, tm=tm, rows=rows),
        grid_spec=grid_spec,
        out_shape=[jax.ShapeDtypeStruct((N_GROUPS * cap, d), BF16), jax.ShapeDtypeStruct((N_GROUPS * cap, LANE), F32)],
        input_output_aliases={4: 0, 5: 1},
        compiler_params=_cparams(("arbitrary", "arbitrary")),
        name="moe_dispatch",
    )(segb, off, h, gate, jnp.zeros((N_GROUPS * cap, d), BF16), jnp.zeros((N_GROUPS * cap, LANE), F32))


def _moe_group_body(grp_ref, rt_ref, kind_ref, xg_ref, gg_ref, w1_ref, w3_ref, w2_ref, yg_ref, acc_ref):
    del rt_ref
    s = pl.program_id(0)
    e = pl.program_id(1)
    kind = kind_ref[s]
    last = e == pl.num_programs(1) - 1

    @pl.when(kind == 1)
    def _():
        @pl.when(e == 0)
        def _():
            acc_ref[...] = jnp.zeros_like(acc_ref)

        xb = xg_ref[...]
        he = _silu(_mm(xb, w1_ref[0])) * _mm(xb, w3_ref[0])
        y = _mm(he, w2_ref[0])
        gate = gg_ref[...]
        lane = lax.broadcasted_iota(jnp.int32, gate.shape, 1)
        ge = jnp.sum(jnp.where(lane == grp_ref[s] * EXP_PER_GROUP + e, gate, 0.0), axis=-1, keepdims=True)
        acc_ref[...] += ge * y

        @pl.when(last)
        def _():
            yg_ref[...] = acc_ref[...]

    @pl.when((kind == 2) & last)
    def _():
        yg_ref[...] = jnp.zeros_like(yg_ref)


def moe_group_ffn(e_grp, e_rt, e_kind, xg, gg, w1, w3, w2, tb, cap):
    d = xg.shape[1]
    de = w1.shape[2]
    row = lambda s, e, grp, rt, kind: (grp[s] * (cap // tb) + rt[s], 0)
    wsel = lambda s, e, grp, rt, kind: (grp[s] * EXP_PER_GROUP + jnp.where(kind[s] == 1, e, EXP_PER_GROUP - 1), 0, 0)
    grid_spec = pltpu.PrefetchScalarGridSpec(
        num_scalar_prefetch=3,
        grid=(e_grp.shape[0], EXP_PER_GROUP),
        in_specs=[pl.BlockSpec((tb, d), row), pl.BlockSpec((tb, LANE), row),
                  pl.BlockSpec((1, d, de), wsel), pl.BlockSpec((1, d, de), wsel), pl.BlockSpec((1, de, d), wsel)],
        out_specs=pl.BlockSpec((tb, d), row),
        scratch_shapes=[pltpu.VMEM((tb, d), F32)],
    )
    return pl.pallas_call(
        _moe_group_body,
        grid_spec=grid_spec,
        out_shape=jax.ShapeDtypeStruct((N_GROUPS * cap, d), F32),
        compiler_params=_cparams(("arbitrary", "arbitrary")),
        name="moe_group_ffn",
    )(e_grp, e_rt, e_kind, xg, gg, w1, w3, w2)


def _moe_combine_body(segb_ref, off_ref, yg_ref, gate_ref, x_ref, g2_ref, o_ref, ys_ref, *, tm, rows):
    del off_ref
    i = pl.program_id(0)
    g = pl.program_id(1)

    @pl.when((i == 0) & (g == 0))
    def _():
        ys_ref[...] = jnp.zeros_like(ys_ref)

    start = pl.multiple_of(segb_ref[i * N_GROUPS + g], ROW_ALIGN)
    ys_ref[pl.ds(start, tm), :] = yg_ref[...]

    @pl.when(g == N_GROUPS - 1)
    def _():
        p = _group_perm(gate_ref[...], segb_ref, i * N_GROUPS, tm, rows)
        yh, yl = _split(ys_ref[0:rows, :])
        y = (lax.dot_general(p, yh, TN, preferred_element_type=F32)
             + lax.dot_general(p, yl, TN, preferred_element_type=F32))
        o_ref[...] = x_ref[...] + g2_ref[...] * y


def moe_combine(yg, gate, x, g2, segb, off, tm, cap):
    n, d = x.shape
    rows = tm + N_GROUPS * ROW_ALIGN
    grid_spec = pltpu.PrefetchScalarGridSpec(
        num_scalar_prefetch=2,
        grid=(n // tm, N_GROUPS),
        in_specs=[pl.BlockSpec((pl.Element(tm), pl.Element(d)),
                               lambda i, g, segb, off: ((g * (cap // ROW_ALIGN) + off[i * N_GROUPS + g]) * ROW_ALIGN, 0)),
                  pl.BlockSpec((tm, LANE), lambda i, g, *_: (i, 0)),
                  pl.BlockSpec((tm, d), lambda i, g, *_: (i, 0)),
                  pl.BlockSpec((1, d), lambda i, g, *_: (0, 0))],
        out_specs=pl.BlockSpec((tm, d), lambda i, g, *_: (i, 0)),
        scratch_shapes=[pltpu.VMEM((2 * tm + N_GROUPS * ROW_ALIGN, d), F32)],
    )
    return pl.pallas_call(
        functools.partial(_moe_combine_body, tm=tm, rows=rows),
        grid_spec=grid_spec,
        out_shape=jax.ShapeDtypeStruct((n, d), F32),
        compiler_params=_cparams(("arbitrary", "arbitrary")),
        name="moe_combine",
    )(segb, off, yg, gate, x, g2)


def moe_grouped(h, gate, w1, w3, w2, x, g2, tm):
    n = h.shape[0]
    tb = tm
    cap = n + 2 * tm
    max_entries = (n + (n // tm) * N_GROUPS * (ROW_ALIGN - 1) + N_GROUPS * tm) // tb + N_GROUPS + 1
    grp = gate[:, GRP_LANE].astype(jnp.int32)
    segb, off, e_grp, e_rt, e_kind = _moe_plan(grp, tm, tb, cap, max_entries)
    xg, gg = moe_dispatch(h, gate, segb, off, tm, cap)
    yg = moe_group_ffn(e_grp, e_rt, e_kind, xg, gg, w1, w3, w2, tb, cap)
    return moe_combine(yg, gate, x, g2, segb, off, tm, cap)


def _final_norm_body(x_ref, w_ref, o_ref):
    x = x_ref[...]
    o_ref[...] = x * lax.rsqrt(jnp.mean(x * x, axis=-1, keepdims=True) + EPS) * w_ref[...]


def final_norm(x, w, tm):
    n, d = x.shape
    return pl.pallas_call(
        _final_norm_body,
        grid=(n // tm,),
        in_specs=[pl.BlockSpec((tm, d), lambda i: (i, 0)), pl.BlockSpec((1, d), lambda i: (0, 0))],
        out_specs=pl.BlockSpec((tm, d), lambda i: (i, 0)),
        out_shape=jax.ShapeDtypeStruct((n, d), F32),
        compiler_params=_cparams(("arbitrary",)),
        name="final_norm",
    )(x, w)


def _row_tile(n, pref):
    t = min(pref, n)
    while n % t:
        t //= 2
    return t


def kernel(x_prompt, x_sample, c_prompt, c_sample, cache_nsa_cmp, cache_nsa_sel, page_table, state_nsa_win, state_rwkv, state_rwkv_shift, state_gdn, state_gdn_conv, norm_mix, norm_ffn, norm_final, w_ada, b_ada, even_w_in, even_w_out, nsa_cmp_pos, nsa_cmp_w, rwkv_mu, rwkv_w0, rwkv_w2, rwkv_a0, rwkv_a2, rwkv_g2, rwkv_kk, rwkv_ka, rwkv_rk, rwkv_ln_w, rwkv_ln_b, odd_w_in, odd_w_out, gdn_conv_w, gdn_a_log, gdn_dt_bias, gdn_norm_w, moe_w_grp, moe_b_grp, moe_w_exp, moe_b_exp, moe_w1, moe_w3, moe_w2):
    bp, t, d = x_prompt.shape
    bs, ts, _ = x_sample.shape
    assert bp == 1 and ts <= SPAD and ts < CMP_BLK
    depth = norm_mix.shape[0]
    n_pages, page = page_table.shape[1], cache_nsa_cmp.shape[2]
    past = n_pages * page
    wb = state_nsa_win.shape[2]
    ns = bs * SPAD
    tq, tq_s, tk = 128, 32, 512
    tm_p = _row_tile(t, 512)
    tm_s = ns

    rows_c = -(-(1 + bs) // 8) * 8
    c_all = jnp.concatenate([c_prompt, c_sample, jnp.zeros((rows_c - 1 - bs, d), F32)], axis=0)
    ada = adaln(c_all, w_ada, b_ada)

    def mods(i):
        mp = [ada[i, 0:1, j * d:(j + 1) * d] for j in range(6)]
        ms = [jnp.repeat(ada[i, 1:1 + bs, j * d:(j + 1) * d], SPAD, axis=0) for j in range(6)]
        return mp, ms

    xp = x_prompt[0]
    xs = jnp.pad(x_sample, ((0, 0), (0, SPAD - ts), (0, 0))).reshape(ns, d)

    def unpad(a):
        return a.reshape(bs, SPAD, -1)[:, :ts]

    outs = {k: [] for k in ("cmp_p", "cmp_s", "sel_p", "sel_s", "win_p", "win_s", "rw_p", "rw_s", "sh_p", "sh_s",
                            "gd_p", "gd_s", "cv_p", "cv_s")}

    for i in range(depth):
        (sh1p, sc1p, gt1p, sh2p, sc2p, gt2p), (sh1s, sc1s, gt1s, sh2s, sc2s, gt2s) = mods(i)
        j = i // 2
        nw = norm_mix[i][None, :]
        if i % 2 == 0:
            w_packed = _pack_even_w(even_w_in[j])
            mu = _pack_rw_vec(rwkv_mu[j])
            wts, wc = _cmp_weights(nsa_cmp_pos[j], nsa_cmp_w[j])
            vec = jnp.stack([rwkv_w0[j], rwkv_a0[j], rwkv_kk[j], rwkv_ka[j], rwkv_ln_w[j], rwkv_ln_b[j],
                             jnp.zeros_like(rwkv_w0[j]), jnp.zeros_like(rwkv_w0[j])])
            pad_lora = lambda w: jnp.concatenate([w, jnp.zeros((128 - w.shape[0], w.shape[1]), w.dtype)], axis=0)
            w2p, a2p, g2p = pad_lora(rwkv_w2[j]), pad_lora(rwkv_a2[j]), rwkv_g2[j]
            hid = jnp.arange(RWKV_W) // RWKV_HD
            seg = (hid[:, None] == hid[None, :]).astype(F32)
            rk = rwkv_rk[j].reshape(1, RWKV_W)
            wo_nsa, wo_rw = even_w_out[j][:512].astype(BF16), even_w_out[j][512:].astype(BF16)

            kv, qt, gt, ks, vst, kw, vwt, rw, hl = even_proj(xp, nw, sc1p, sh1p, w_packed, tm_p, 8)
            kvc = compress_prompt(kv, wts, wc, _row_tile(t, 512))
            gates = gt[:24].reshape(NSA_KV_HEADS, 12, t)
            gates = jnp.pad(gates, ((0, 0), (0, 4), (0, 0)))[None]
            o_nsa = nsa_attention(
                qt[None], gates, kvc[None], kvc.T[None], ks[None], vst[None], kw[None], vwt[None],
                tq=tq, tk=tk, wk=WINDOW + tq,
                pos0_fn=lambda qi: qi * tq,
                wstart_fn=lambda qi: jnp.maximum(qi * tq - WINDOW, 0),
                wpos0_fn=lambda qi: jnp.maximum(qi * tq - WINDOW, 0))[0]
            o_rw, s_rw = rwkv_mix(rw, jnp.zeros((1, 8, RW_COLS), F32), jnp.zeros((1, RWKV_HEADS, 64, 64), F32),
                                  mu, vec, w2p, a2p, g2p, seg, rk, c=64, valid=64)
            xp = out_proj([o_nsa, o_rw], [wo_nsa, wo_rw], xp, gt1p, tm_p)
            outs["cmp_p"].append(kv[:, 0:256].reshape(1, t, 2, 2, 64))
            outs["sel_p"].append(kv[:, 256:512].reshape(1, t, 2, 2, 64))
            kvw_rows = kv[:, 512:768].reshape(1, t, 2, 2, 64)
            outs["win_p"].append(kvw_rows[:, -min(WINDOW, t):])
            outs["rw_p"].append(s_rw)
            outs["sh_p"].append(hl[-1:])

            kv, qt, gt, _, _, _, _, rw, hl = even_proj(xs, nw, sc1s, sh1s, w_packed, tm_s, ns)
            kv_new = unpad(kv)
            rw0 = small_matmul(jnp.pad(state_rwkv_shift[j], ((0, -bs % 8), (0, 0))), w_packed[:, E_RW:])[:bs]
            rw0 = jnp.pad(rw0[:, None, :], ((0, 0), (7, 0), (0, 0)))
            pool_cmp = cache_nsa_cmp[j].transpose(0, 2, 3, 4, 1).reshape(-1, 256, page)
            pool_sel = cache_nsa_sel[j].transpose(0, 2, 3, 4, 1).reshape(-1, 256, page)
            kvc_s = compress_paged(pool_cmp, page_table, nsa_cmp_pos[j], wc, 8 if n_pages % 8 == 0 else 1)
            tail = jnp.pad(kv_new[:, :, 256:512], ((0, 0), (0, tk - ts), (0, 0)))
            ks_s, vst_s = gather_sel(pool_sel, page_table, tail, tk)
            wbuf = state_nsa_win[j].reshape(bs, wb, 256)
            kvw_all = jnp.concatenate([wbuf, kv_new[:, :, 512:768]], axis=1)
            wk_s = -(-(wb + ts) // 128) * 128
            kvw_pad = jnp.pad(kvw_all, ((0, 0), (0, wk_s - wb - ts), (0, 0)))
            kw_s = kvw_pad[:, :, :128].astype(BF16)
            vwt_s = jnp.swapaxes(kvw_pad[:, :, 128:], 1, 2).astype(BF16)
            qt_s = jnp.pad(qt.reshape(512, bs, SPAD).transpose(1, 0, 2), ((0, 0), (0, 0), (0, tq_s - SPAD)))
            g_s = gt[:24].reshape(NSA_KV_HEADS, 12, bs, SPAD).transpose(2, 0, 1, 3)
            g_s = jnp.pad(g_s, ((0, 0), (0, 0), (0, 4), (0, tq_s - SPAD)))
            o_nsa = nsa_attention(
                qt_s, g_s, kvc_s, jnp.swapaxes(kvc_s, 1, 2), ks_s, vst_s, kw_s, vwt_s,
                tq=tq_s, tk=tk, wk=wk_s,
                pos0_fn=lambda qi: past,
                wstart_fn=lambda qi: 0,
                wpos0_fn=lambda qi: past - wb)
            o_nsa = o_nsa[:, :SPAD].reshape(ns, 512)
            o_rw, s_rw = rwkv_mix(rw, rw0, state_rwkv[j], mu, vec, w2p, a2p, g2p, seg, rk, c=SPAD, valid=ts)
            xs = out_proj([o_nsa, o_rw], [wo_nsa, wo_rw], xs, gt1s, tm_s)
            outs["cmp_s"].append(kv_new[:, :, 0:256].reshape(bs, ts, 2, 2, 64))
            outs["sel_s"].append(kv_new[:, :, 256:512].reshape(bs, ts, 2, 2, 64))
            outs["win_s"].append(kvw_all[:, -wb:].reshape(bs, wb, 2, 2, 64))
            outs["rw_s"].append(s_rw)
            outs["sh_s"].append(hl.reshape(bs, SPAD, d)[:, ts - 1])
        else:
            w_in = odd_w_in[j]
            w_packed = jnp.concatenate([w_in, jnp.zeros((d, O_COLS - w_in.shape[1]), F32)], axis=1).astype(BF16)
            conv_w8 = jnp.pad(gdn_conv_w[j], ((0, 8 - CONV_W), (0, 0)))
            hp = jnp.zeros((8, 128), F32)
            hp = hp.at[0, 8:16].set(-jnp.exp(gdn_a_log[j])).at[1, 8:16].set(gdn_dt_bias[j])
            gnw = gdn_norm_w[j][None, :]
            wo = odd_w_out[j].astype(BF16)

            qkv, z, ba = odd_proj(xp, nw, sc1p, sh1p, w_packed, tm_p)
            o_g, s_g = gdn_mix(qkv, z, ba, jnp.zeros((1, 8, 3 * GDN_W), F32),
                               jnp.zeros((1, GDN_HEADS, GDN_HD, GDN_HD), F32), conv_w8, hp, gnw, c=64, valid=64)
            xp = out_proj([o_g], [wo], xp, gt1p, tm_p)
            outs["gd_p"].append(s_g)
            outs["cv_p"].append(qkv[None, -(CONV_W - 1):])

            qkv, z, ba = odd_proj(xs, nw, sc1s, sh1s, w_packed, tm_s)
            cs = jnp.pad(state_gdn_conv[j], ((0, 0), (8 - (CONV_W - 1), 0), (0, 0)))
            o_g, s_g = gdn_mix(qkv, z, ba, cs, state_gdn[j], conv_w8, hp, gnw, c=SPAD, valid=ts)
            xs = out_proj([o_g], [wo], xs, gt1s, tm_s)
            xpad = jnp.concatenate([state_gdn_conv[j], unpad(qkv)], axis=1)
            outs["gd_s"].append(s_g)
            outs["cv_s"].append(xpad[:, -(CONV_W - 1):])

        nwf = norm_ffn[i][None, :]
        w_r = jnp.concatenate([moe_w_exp[i], moe_w_grp[i], jnp.zeros((d, LANE - N_EXPERTS - N_GROUPS), F32)], axis=1)
        b_r = jnp.concatenate([moe_b_exp[i], moe_b_grp[i], jnp.zeros((LANE - N_EXPERTS - N_GROUPS,), F32)])[None, :]
        h2, gate = moe_router(xp, nwf, sc2p, sh2p, w_r, b_r, tm_p)
        xp = moe_grouped(h2, gate, moe_w1[i], moe_w3[i], moe_w2[i], xp, gt2p, _row_tile(t, 1024))
        h2, gate = moe_router(xs, nwf, sc2s, sh2s, w_r, b_r, tm_s)
        xs = moe_ffn(h2, gate, moe_w1[i], moe_w3[i], moe_w2[i], xs, gt2s, tm_s)

    nf = norm_final[None, :]
    y_prompt = final_norm(xp, nf, tm_p)[None]
    y_sample = unpad(final_norm(xs, nf, tm_s))
    st = lambda key: jnp.stack(outs[key])
    return (y_prompt, y_sample, st("cmp_p"), st("cmp_s"), st("sel_p"), st("sel_s"), st("win_p"), st("win_s"),
            st("rw_p"), st("rw_s"), st("sh_p"), st("sh_s"), st("gd_p"), st("gd_s"), st("cv_p"), st("cv_s"))
```

```python
import functools
import math

import jax
import jax.numpy as jnp
from jax import lax
from jax.experimental import pallas as pl
from jax.experimental.pallas import tpu as pltpu

F32 = jnp.float32
BF16 = jnp.bfloat16
HIGHEST = lax.Precision.HIGHEST

NSA_HEADS = 8
NSA_KV_HEADS = 2
NSA_GROUP = 4
NSA_HD = 64
CMP_BLK = 64
SEL_BLK = 64
TOPK_BLK = 16
WINDOW = 512
FORCE_BONUS = 2.0 * NSA_GROUP
RWKV_HEADS = 8
RWKV_HD = 64
RWKV_W = 512
RWKV_GN_EPS = 64e-5
GDN_HEADS = 8
GDN_HD = 128
GDN_W = 1024
CONV_W = 4
N_GROUPS = 4
EXP_PER_GROUP = 8
N_EXPERTS = 32
EPS = 1e-6
NEG = -1e30

LANE = 128
GRP_LANE = 64
ROW_ALIGN = 16
SPAD = 8
VMEM_LIMIT = 56 * 1024 * 1024

NN = (((1,), (0,)), ((), ()))
NT = (((1,), (1,)), ((), ()))
TN = (((0,), (0,)), ((), ()))

E_Q, E_KV, E_G, E_RW = 0, 512, 1280, 1408
E_COLS = 1408 + 1920
RW_COLS = 1920
O_COLS = 3072 + 1024 + 128


def _mm(a, b, dims=NN):
    return lax.dot_general(a.astype(BF16), b.astype(BF16), dims, preferred_element_type=F32)


def _mmh(a, b, dims=NN):
    return lax.dot_general(a.astype(F32), b.astype(F32), dims, precision=HIGHEST, preferred_element_type=F32)


def _split(a):
    hi = a.astype(BF16)
    return hi, (a - hi.astype(F32)).astype(BF16)


def _mm3(a, b, dims=NN):
    def parts(x):
        hi = x.astype(BF16).astype(F32)
        return hi, x - hi
    ah, al = parts(a)
    bh, bl = parts(b)
    ax_a = 0 if dims == TN else 1
    ax_b = 1 if dims == NT else 0
    lhs = jnp.concatenate([ah, ah, al], axis=ax_a).astype(BF16)
    rhs = jnp.concatenate([bh, bl, bh], axis=ax_b).astype(BF16)
    return lax.dot_general(lhs, rhs, dims, preferred_element_type=F32)


def _split3(x):
    h1 = x.astype(BF16)
    r1 = x - h1.astype(F32)
    h2 = r1.astype(BF16)
    return h1, h2, (r1 - h2.astype(F32)).astype(BF16)


def _mm01(m01, x):
    m = m01.astype(BF16)
    parts = _split3(x)
    if 3 * m.shape[1] <= 256:
        return lax.dot_general(jnp.concatenate([m, m, m], axis=1), jnp.concatenate(parts, axis=0), NN,
                               preferred_element_type=F32)
    d = lambda y: lax.dot_general(m, y, NN, preferred_element_type=F32)
    return d(parts[0]) + (d(parts[1]) + d(parts[2]))


def _rows_times01(xs, m01):
    r = xs[0].shape[0]
    stacked = jnp.concatenate([p for x in xs for p in _split3(x)], axis=0)
    out = lax.dot_general(stacked, m01.astype(BF16), NN, preferred_element_type=F32)
    return [out[3 * i * r:(3 * i + 1) * r] + (out[(3 * i + 1) * r:(3 * i + 2) * r] + out[(3 * i + 2) * r:(3 * i + 3) * r])
            for i in range(len(xs))]


def _sigmoid(x):
    return 1.0 / (1.0 + jnp.exp(-x))


def _silu(x):
    return x * _sigmoid(x)


def _softplus(x):
    return jnp.maximum(x, 0.0) + jnp.log(1.0 + jnp.exp(-jnp.abs(x)))


def _cparams(sem):
    return pltpu.CompilerParams(dimension_semantics=sem, vmem_limit_bytes=VMEM_LIMIT)


def _norm_mod(x, nw, sc, sh):
    y = x * lax.rsqrt(jnp.mean(x * x, axis=-1, keepdims=True) + EPS)
    return (y * nw) * (1.0 + sc) + sh


def _mod_spec(rows_mod, tm, d):
    if rows_mod == 1:
        return pl.BlockSpec((1, d), lambda i: (0, 0))
    return pl.BlockSpec((tm, d), lambda i: (i, 0))


def _adaln_body(c_ref, w_ref, b_ref, o_ref):
    o_ref[0] = _mmh(_silu(c_ref[...]), w_ref[0]) + b_ref[0]


def adaln(c_all, w_ada, b_ada):
    depth, d, n6 = w_ada.shape
    rows = c_all.shape[0]
    tn = 768
    return pl.pallas_call(
        _adaln_body,
        grid=(depth, n6 // tn),
        in_specs=[pl.BlockSpec((rows, d), lambda l, j: (0, 0)),
                  pl.BlockSpec((1, d, tn), lambda l, j: (l, 0, j)),
                  pl.BlockSpec((1, 1, tn), lambda l, j: (l, 0, j))],
        out_specs=pl.BlockSpec((1, rows, tn), lambda l, j: (l, 0, j)),
        out_shape=jax.ShapeDtypeStruct((depth, rows, n6), F32),
        compiler_params=_cparams(("arbitrary", "arbitrary")),
        name="adaln",
    )(c_all, w_ada, b_ada.reshape(depth, 1, n6))


def _even_proj_body(x_ref, nw_ref, sc_ref, sh_ref, w_ref,
                    kv_ref, qt_ref, gt_ref, ks_ref, vst_ref, kw_ref, vwt_ref, rw_ref, hl_ref):
    h = _norm_mod(x_ref[...], nw_ref[...], sc_ref[...], sh_ref[...])
    hl = hl_ref.shape[0]
    hl_ref[...] = h[h.shape[0] - hl:, :]
    hb = h.astype(BF16)
    q = _mm(hb, w_ref[:, E_Q:E_Q + 512]) * (NSA_HD ** -0.5)
    qt_ref[...] = q.T.astype(BF16)
    kv = _mm(hb, w_ref[:, E_KV:E_KV + 768])
    kv_ref[...] = kv
    ks_ref[...] = kv[:, 256:384].astype(BF16)
    vst_ref[...] = kv[:, 384:512].T.astype(BF16)
    kw_ref[...] = kv[:, 512:640].astype(BF16)
    vwt_ref[...] = kv[:, 640:768].T.astype(BF16)
    g = _sigmoid(_mm(hb, w_ref[:, E_G:E_G + 128]))
    gt_ref[...] = g.T
    rw_ref[...] = _mm(hb, w_ref[:, E_RW:E_RW + RW_COLS])


def even_proj(x, nw, sc, sh, w_packed, tm, hl_rows):
    n, d = x.shape
    rows_mod = sc.shape[0]
    row = lambda c: pl.BlockSpec((tm, c), lambda i: (i, 0))
    col = lambda r: pl.BlockSpec((r, tm), lambda i: (0, i))
    return pl.pallas_call(
        _even_proj_body,
        grid=(n // tm,),
        in_specs=[row(d), pl.BlockSpec((1, d), lambda i: (0, 0)),
                  _mod_spec(rows_mod, tm, d), _mod_spec(rows_mod, tm, d),
                  pl.BlockSpec((d, E_COLS), lambda i: (0, 0))],
        out_specs=[row(768), col(512), col(128), row(128), col(128), row(128), col(128), row(RW_COLS),
                   pl.BlockSpec((hl_rows, d), lambda i: (0, 0))],
        out_shape=[jax.ShapeDtypeStruct((n, 768), F32),
                   jax.ShapeDtypeStruct((512, n), BF16),
                   jax.ShapeDtypeStruct((128, n), F32),
                   jax.ShapeDtypeStruct((n, 128), BF16),
                   jax.ShapeDtypeStruct((128, n), BF16),
                   jax.ShapeDtypeStruct((n, 128), BF16),
                   jax.ShapeDtypeStruct((128, n), BF16),
                   jax.ShapeDtypeStruct((n, RW_COLS), F32),
                   jax.ShapeDtypeStruct((hl_rows, d), F32)],
        compiler_params=_cparams(("arbitrary",)),
        name="even_proj",
    )(x, nw, sc, sh, w_packed)


def _pack_even_w(w_in):
    d = w_in.shape[0]
    z = lambda c: jnp.zeros((d, c), w_in.dtype)
    nsa = 1304
    rw = w_in[:, nsa:]
    parts = [w_in[:, :1280], w_in[:, 1280:1304], z(104),
             rw[:, :1536], rw[:, 1536:1600], z(64), rw[:, 1600:1664], z(64), rw[:, 1664:1792]]
    return jnp.concatenate(parts, axis=1).astype(BF16)


def _pack_rw_vec(v):
    z = jnp.zeros((64,), v.dtype)
    return jnp.concatenate([v[:1536], v[1536:1600], z, v[1600:1664], z, v[1664:1792]])[None, :]


def _mm_body(x_ref, w_ref, o_ref):
    o_ref[...] = _mm(x_ref[...], w_ref[...])


def small_matmul(x, w):
    return pl.pallas_call(
        _mm_body,
        out_shape=jax.ShapeDtypeStruct((x.shape[0], w.shape[1]), F32),
        compiler_params=pltpu.CompilerParams(vmem_limit_bytes=VMEM_LIMIT),
        name="small_matmul",
    )(x, w)


def _compress_body(x_ref, wts_ref, wc_ref, o_ref):
    x = x_ref[...]
    nb = x.shape[0] // CMP_BLK
    pooled = jnp.sum(x.reshape(nb, CMP_BLK, x.shape[-1]) * wts_ref[...][None], axis=1)
    o_ref[...] = _mm(pooled, wc_ref[...])


def _compress_paged_body(pt_ref, *refs, pps):
    page_refs = refs[:pps]
    wp_ref, wc_ref, o_ref = refs[pps:]
    x = jnp.concatenate([r[0] for r in page_refs], axis=1)
    pooled_t = jnp.concatenate([_mm(x[0:128], wp_ref[0]), _mm(x[128:256], wp_ref[1])], axis=0)
    nb = o_ref.shape[1]
    o_ref[0] = _mm(pooled_t.T[:nb], wc_ref[...])


def _cmp_weights(pos_wts, w_c):
    wts = jnp.repeat(pos_wts.T, 128, axis=1)
    eye2 = jnp.eye(2, dtype=w_c.dtype)
    blocks = [jnp.kron(eye2, w_c[c]) for c in range(2)]
    z = jnp.zeros((128, 128), w_c.dtype)
    wc = jnp.concatenate([jnp.concatenate([blocks[0], z], axis=1),
                          jnp.concatenate([z, blocks[1]], axis=1)], axis=0)
    return wts, wc


def compress_prompt(kv, wts, wc, tr):
    t = kv.shape[0]
    nb = tr // CMP_BLK
    return pl.pallas_call(
        _compress_body,
        grid=(t // tr,),
        in_specs=[pl.BlockSpec((tr, 256), lambda i: (i, 0)),
                  pl.BlockSpec((CMP_BLK, 256), lambda i: (0, 0)),
                  pl.BlockSpec((256, 256), lambda i: (0, 0))],
        out_specs=pl.BlockSpec((nb, 256), lambda i: (i, 0)),
        out_shape=jax.ShapeDtypeStruct((t // CMP_BLK, 256), F32),
        compiler_params=_cparams(("arbitrary",)),
        name="compress_prompt",
    )(kv, wts, wc)


def compress_paged(pool_t, page_table, pos_wts, wc, pages_per_step):
    b, n_pages = page_table.shape
    page = pool_t.shape[2]
    pps = pages_per_step
    nb = pps * page // CMP_BLK
    p_idx = jnp.arange(pps * page)
    wp = jax.nn.one_hot(p_idx // CMP_BLK, LANE, dtype=F32)[None] * pos_wts[:, p_idx % CMP_BLK][:, :, None]

    def page_spec(u):
        return pl.BlockSpec((1, 256, page), lambda bi, g, pt: (pt[bi, g * pps + u], 0, 0))

    grid_spec = pltpu.PrefetchScalarGridSpec(
        num_scalar_prefetch=1,
        grid=(b, n_pages // pps),
        in_specs=[page_spec(u) for u in range(pps)] + [
            pl.BlockSpec((2, pps * page, LANE), lambda bi, g, pt: (0, 0, 0)),
            pl.BlockSpec((256, 256), lambda bi, g, pt: (0, 0))],
        out_specs=pl.BlockSpec((1, nb, 256), lambda bi, g, pt: (bi, g, 0)),
    )
    return pl.pallas_call(
        functools.partial(_compress_paged_body, pps=pps),
        grid_spec=grid_spec,
        out_shape=jax.ShapeDtypeStruct((b, n_pages * page // CMP_BLK, 256), F32),
        compiler_params=_cparams(("arbitrary", "arbitrary")),
        name="compress_paged",
    )(page_table, *([pool_t] * pps), wp, wc)


def _gather_sel_body(pt_ref, tiles_ref, cnt_ref, *refs, pps, n_page_steps, nt):
    del pt_ref
    page_refs = refs[:pps]
    tail_ref, ks_ref, vst_ref = refs[pps:]
    bi = pl.program_id(0)
    a = pl.program_id(1)
    j = tiles_ref[bi * nt + jnp.minimum(a, cnt_ref[bi] - 1)]
    live = a < cnt_ref[bi]

    @pl.when(live & (j < n_page_steps))
    def _():
        ks_ref[0] = jnp.concatenate([r[0][0:128].T for r in page_refs], axis=0).astype(BF16)
        vst_ref[0] = jnp.concatenate([r[0][128:256] for r in page_refs], axis=1).astype(BF16)

    @pl.when(live & (j >= n_page_steps))
    def _():
        x = tail_ref[0]
        ks_ref[0] = x[:, :128].astype(BF16)
        vst_ref[0] = x[:, 128:].T.astype(BF16)


def gather_sel(pool_t, page_table, tail, tk, tiles, cnt):
    b, n_pages = page_table.shape
    page = pool_t.shape[2]
    pps = tk // page
    n_page_steps = n_pages // pps
    nt = n_page_steps + 1
    nk = n_pages * page + tk

    def slot(bi, a, pt, tiles, cnt):
        return jnp.minimum(a, cnt[bi] - 1)

    def page_spec(u):
        def index(bi, a, pt, tiles, cnt):
            j = tiles[bi * nt + slot(bi, a, pt, tiles, cnt)]
            return (pt[bi, jnp.minimum(j * pps + u, n_pages - 1)], 0, 0)
        return pl.BlockSpec((1, 256, page), index)

    grid_spec = pltpu.PrefetchScalarGridSpec(
        num_scalar_prefetch=3,
        grid=(b, jnp.max(cnt)),
        in_specs=[page_spec(u) for u in range(pps)] + [pl.BlockSpec((1, tk, 256), lambda bi, a, *_: (bi, 0, 0))],
        out_specs=[pl.BlockSpec((1, tk, 128), lambda bi, a, *s: (bi, slot(bi, a, *s), 0)),
                   pl.BlockSpec((1, 128, tk), lambda bi, a, *s: (bi, 0, slot(bi, a, *s)))],
    )
    return pl.pallas_call(
        functools.partial(_gather_sel_body, pps=pps, n_page_steps=n_page_steps, nt=nt),
        grid_spec=grid_spec,
        out_shape=[jax.ShapeDtypeStruct((b, nk, 128), BF16), jax.ShapeDtypeStruct((b, 128, nk), BF16)],
        compiler_params=_cparams(("arbitrary", "arbitrary")),
        name="gather_sel",
    )(page_table, tiles, cnt, *([pool_t] * pps), tail)


MASKED = -1e30
M_INIT = -1e29


def _nsa_query(qt_ref, k, tq):
    w4 = NSA_GROUP * tq
    qb = qt_ref[0].astype(F32)
    qcat = jnp.concatenate([qb[g * 64:(g + 1) * 64] for g in range(NSA_GROUP)], axis=1)
    q2 = jnp.concatenate([qcat, qcat], axis=0)
    row = lax.broadcasted_iota(jnp.int32, (128, w4), 0)
    qe = jnp.where(row // 64 == k, q2, 0.0)
    gidx = lax.broadcasted_iota(jnp.int32, (128, w4), 1) // tq
    base = jnp.where(k == 0, 0.5, 0.5 / 16.0)
    slope = base * jnp.where(gidx == 0, 1.0, jnp.where(gidx == 1, 0.5, jnp.where(gidx == 2, 0.25, 0.125)))
    mult = jnp.where(row == 0, 16.0, jnp.where(row == 1, 1.0, jnp.where(row == 2, 128.0,
                                                                         jnp.where(row == 3, 64.0, 0.0))))
    return jnp.concatenate([qe, slope * mult], axis=0).astype(BF16)


def _pos_features(rows, tile_rel):
    r = lax.broadcasted_iota(jnp.int32, (rows, LANE), 0)
    lane = lax.broadcasted_iota(jnp.int32, (rows, LANE), 1)
    ab = jnp.where(lane == 0, r // 16, jnp.where(lane == 1, r % 16, 0)).astype(F32)
    return jnp.where(lane == 2, tile_rel, ab).astype(BF16)


def _gate_rows(gb, j, tq):
    return jnp.concatenate([gb[g * 3 + j:g * 3 + j + 1, :] for g in range(NSA_GROUP)], axis=1)


def _nsa_select_body(qt_ref, g_ref, kvc_ref, kvct_ref, kw_ref, vwt_ref, part_ref, sel_ref, flag_ref, *,
                     tq, tk, wk, nbc, nb, pos0_fn, wstart_fn, wpos0_fn):
    i = pl.program_id(1)
    k = pl.program_id(2)
    w4 = NSA_GROUP * tq
    pos0 = pos0_fn(i)
    qa = _nsa_query(qt_ref, k, tq)
    pos_q = pos0 + lax.broadcasted_iota(jnp.int32, (1, w4), 1) % tq

    def softmax_cols(s, bad):
        s = jnp.where(bad, MASKED, s)
        m = jnp.maximum(jnp.max(s, axis=0, keepdims=True), M_INIT)
        e = jnp.exp(s - m)
        return e / jnp.maximum(jnp.sum(e, axis=0, keepdims=True), 1e-30)

    n_i = lax.broadcasted_iota(jnp.int32, (nbc, LANE), 0)
    lane_c = lax.broadcasted_iota(jnp.int32, (nbc, LANE), 1)
    feat_c = jnp.where(lane_c == 3, n_i - pos0 // CMP_BLK, 0).astype(F32).astype(BF16)
    kc = jnp.concatenate([kvc_ref[0][:, :128].astype(BF16), feat_c], axis=1)
    c_end = lax.broadcasted_iota(jnp.int32, (nbc, 1), 0) * CMP_BLK + (CMP_BLK - 1)
    p_c = softmax_cols(lax.dot_general(kc, qa, NN, preferred_element_type=F32), c_end > pos_q)
    vct = kvct_ref[0, pl.ds(pl.multiple_of(128 + k * 64, 64), 64), :]
    o_c = _mm(vct, p_c)

    imp = p_c[:, 0:tq]
    for g in range(1, NSA_GROUP):
        imp = imp + p_c[:, g * tq:(g + 1) * tq]
    if nb > nbc:
        imp = jnp.concatenate([imp, jnp.zeros((nb - nbc, tq), F32)], axis=0)
    blk = lax.broadcasted_iota(jnp.int32, (nb, tq), 0)
    cur = (pos0 + lax.broadcasted_iota(jnp.int32, (1, tq), 1)) // SEL_BLK
    forced = (blk == cur) | (blk == cur - 1) | (blk == 0)
    score = jnp.where(blk <= cur, imp + jnp.where(forced, FORCE_BONUS, 0.0), -1.0)
    for _ in range(min(TOPK_BLK, nb)):
        m = jnp.max(score, axis=0, keepdims=True)
        first = jnp.min(jnp.where(score == m, blk, nb), axis=0, keepdims=True)
        score = jnp.where(blk == first, -2.0, score)
    sel = jnp.where(score == -2.0, 1.0, 0.0)
    sel_ref[0, 0] = sel
    bpt = tk // SEL_BLK
    any_row = jnp.max(sel, axis=1, keepdims=True)
    flag_ref[0, 0] = jnp.max(any_row.reshape(nb // bpt, bpt, 1), axis=1)

    wstart = wstart_fn(i)
    if not isinstance(wstart, int):
        wstart = pl.multiple_of(wstart, 128)
    wpos0 = wpos0_fn(i)
    tile_rel = jnp.asarray((wpos0 - pos0) // 128, F32)
    kw = jnp.concatenate([kw_ref[0, pl.ds(wstart, wk), :], _pos_features(wk, tile_rel)], axis=1)
    dist_w = pos_q - (wpos0 + lax.broadcasted_iota(jnp.int32, (wk, 1), 0))
    p_w = softmax_cols(lax.dot_general(kw, qa, NN, preferred_element_type=F32), (dist_w < 0) | (dist_w >= WINDOW))
    vwin = vwt_ref[0, pl.ds(pl.multiple_of(k * 64, 64), 64), pl.ds(wstart, wk)]
    o_w = _mm(vwin, p_w)

    gb = g_ref[0, 0]
    part_ref[0, 0] = _gate_rows(gb, 0, tq) * o_c + _gate_rows(gb, 2, tq) * o_w


def nsa_select(qt, gates, kvc, kvct, kw, vwt, *, nb, tq, tk, wk, pos0_fn, wstart_fn, wpos0_fn):
    b, _, nq = qt.shape
    nbc = kvc.shape[1]
    nw = kw.shape[1]
    nqt = nq // tq
    nt = nb * SEL_BLK // tk
    w4 = NSA_GROUP * tq
    assert nbc <= 256 and tk <= 512 and wk <= 1024
    body = functools.partial(_nsa_select_body, tq=tq, tk=tk, wk=wk, nbc=nbc, nb=nb, pos0_fn=pos0_fn,
                             wstart_fn=wstart_fn, wpos0_fn=wpos0_fn)
    full = lambda s1, s2: pl.BlockSpec((1, s1, s2), lambda bi, i, k: (bi, 0, 0))
    step = lambda s1, s2: pl.BlockSpec((1, 1, s1, s2), lambda bi, i, k: (bi, i * NSA_KV_HEADS + k, 0, 0))
    return pl.pallas_call(
        body,
        grid=(b, nqt, NSA_KV_HEADS),
        in_specs=[pl.BlockSpec((1, 256, tq), lambda bi, i, k: (bi, k, i)),
                  pl.BlockSpec((1, 1, 16, tq), lambda bi, i, k: (bi, k, 0, i)),
                  full(nbc, 256), full(256, nbc), full(nw, 128), full(128, nw)],
        out_specs=[step(64, w4), step(nb, tq), step(nt, 1)],
        out_shape=[jax.ShapeDtypeStruct((b, nqt * 2, 64, w4), F32),
                   jax.ShapeDtypeStruct((b, nqt * 2, nb, tq), F32),
                   jax.ShapeDtypeStruct((b, nqt * 2, nt, 1), F32)],
        compiler_params=_cparams(("arbitrary", "arbitrary", "arbitrary")),
        name="nsa_select",
    )(qt, gates, kvc, kvct, kw, vwt)


def _nsa_selected_body(list_ref, slot_ref, cnt_ref, qt_ref, g_ref, sel_ref, ks_ref, vst_ref, part_ref, o_ref, *,
                       tq, tk, nt, pos0_fn):
    bi = pl.program_id(0)
    i = pl.program_id(1)
    k = pl.program_id(2)
    step = (bi * pl.num_programs(1) + i) * NSA_KV_HEADS + k
    w4 = NSA_GROUP * tq
    pos0 = pos0_fn(i)
    qa = _nsa_query(qt_ref, k, tq)
    pos_q = pos0 + lax.broadcasted_iota(jnp.int32, (1, w4), 1) % tq
    bpt = tk // SEL_BLK
    row_k = lax.broadcasted_iota(jnp.int32, (tk, 1), 0)
    r = lax.broadcasted_iota(jnp.int32, (tk, LANE), 0)
    lane = lax.broadcasted_iota(jnp.int32, (tk, LANE), 1)
    feat_ab = jnp.where(lane == 0, r // 16, jnp.where(lane == 1, r % 16, 0)).astype(F32)

    def kv_step(jj, carry):
        m_i, l_i, acc = carry
        j = list_ref[step * nt + jj]
        off = pl.multiple_of(j * tk, tk)
        buf = pl.multiple_of(slot_ref[step * nt + jj] * tk, tk)
        tile_rel = ((off - pos0) // 128).astype(F32)
        feat = jnp.where(lane == 2, tile_rel, feat_ab).astype(BF16)
        kj = jnp.concatenate([ks_ref[0, pl.ds(buf, tk), :], feat], axis=1)
        s = lax.dot_general(kj, qa, NN, preferred_element_type=F32)
        selb = (sel_ref[0, 0, pl.ds(pl.multiple_of(j * bpt, bpt), bpt), :] - 1.0) * (-MASKED)
        selb = jnp.concatenate([selb] * NSA_GROUP, axis=1)
        s = s + jnp.broadcast_to(selb[:, None, :], (bpt, SEL_BLK, w4)).reshape(tk, w4)
        s = jnp.where(row_k > pos_q - off, MASKED, s)
        m_new = jnp.maximum(m_i, jnp.max(s, axis=0, keepdims=True))
        p = jnp.exp(s - m_new)
        alpha = jnp.exp(m_i - m_new)
        l_new = l_i * alpha + jnp.sum(p, axis=0, keepdims=True)
        vj = vst_ref[0, pl.ds(pl.multiple_of(k * 64, 64), 64), pl.ds(buf, tk)]
        return m_new, l_new, acc * alpha + _mm(vj, p)

    init = (jnp.full((1, w4), M_INIT, F32), jnp.zeros((1, w4), F32), jnp.zeros((64, w4), F32))
    _, l_s, acc_s = lax.fori_loop(0, cnt_ref[step], kv_step, init)
    o_s = acc_s / jnp.maximum(l_s, 1e-30)
    o_t = part_ref[0, 0] + _gate_rows(g_ref[0, 0], 1, tq) * o_s
    o_ref[0] = jnp.concatenate([o_t[:, g * tq:(g + 1) * tq].T for g in range(NSA_GROUP)], axis=1)


def nsa_selected(tile_list, slot_list, tile_cnt, qt, gates, sel, ks, vst, part, *, tq, tk, pos0_fn):
    b, _, nq = qt.shape
    nk = ks.shape[1]
    nb = sel.shape[2]
    nt = nk // tk
    w4 = NSA_GROUP * tq
    full = lambda s1, s2: pl.BlockSpec((1, s1, s2), lambda bi, i, k, *_: (bi, 0, 0))
    step = lambda s1, s2: pl.BlockSpec((1, 1, s1, s2), lambda bi, i, k, *_: (bi, i * NSA_KV_HEADS + k, 0, 0))
    grid_spec = pltpu.PrefetchScalarGridSpec(
        num_scalar_prefetch=3,
        grid=(b, nq // tq, NSA_KV_HEADS),
        in_specs=[pl.BlockSpec((1, 256, tq), lambda bi, i, k, *_: (bi, k, i)),
                  pl.BlockSpec((1, 1, 16, tq), lambda bi, i, k, *_: (bi, k, 0, i)),
                  step(nb, tq), full(nk, 128), full(128, nk), step(64, w4)],
        out_specs=pl.BlockSpec((1, tq, 256), lambda bi, i, k, *_: (bi, i, k)),
    )
    return pl.pallas_call(
        functools.partial(_nsa_selected_body, tq=tq, tk=tk, nt=nt, pos0_fn=pos0_fn),
        grid_spec=grid_spec,
        out_shape=jax.ShapeDtypeStruct((b, nq, 512), F32),
        compiler_params=_cparams(("arbitrary", "arbitrary", "arbitrary")),
        name="nsa_selected",
    )(tile_list, slot_list, tile_cnt, qt, gates, sel, ks, vst, part)


def _active_first(active):
    order = jnp.argsort(jnp.where(active, 0, 1), axis=-1, stable=True).astype(jnp.int32)
    return order, jnp.sum(active, axis=-1).astype(jnp.int32)


def nsa_attention(qt, gates, kvc, kvct, ks, vst, kw, vwt, *, tq, tk, wk, pos0_fn, wstart_fn, wpos0_fn):
    nb = ks.shape[1] // SEL_BLK
    part, sel, flags = nsa_select(qt, gates, kvc, kvct, kw, vwt, nb=nb, tq=tq, tk=tk, wk=wk, pos0_fn=pos0_fn,
                                  wstart_fn=wstart_fn, wpos0_fn=wpos0_fn)
    order, cnt = _active_first(flags[..., 0] > 0.5)
    return nsa_selected(order.reshape(-1), order.reshape(-1), cnt.reshape(-1), qt, gates, sel, ks, vst, part,
                        tq=tq, tk=tk, pos0_fn=pos0_fn)


def nsa_attention_paged(qt, gates, kvc, kvct, pool_t, page_table, tail, kw, vwt, *, tq, tk, wk, pos0_fn, wstart_fn,
                        wpos0_fn):
    nb = (page_table.shape[1] * pool_t.shape[2] + tk) // SEL_BLK
    part, sel, flags = nsa_select(qt, gates, kvc, kvct, kw, vwt, nb=nb, tq=tq, tk=tk, wk=wk, pos0_fn=pos0_fn,
                                  wstart_fn=wstart_fn, wpos0_fn=wpos0_fn)
    active = flags[..., 0] > 0.5
    tiles_b, cnt_b = _active_first(jnp.any(active, axis=1))
    ks, vst = gather_sel(pool_t, page_table, tail, tk, tiles_b.reshape(-1), cnt_b)
    slot_of_tile = jnp.argsort(tiles_b, axis=-1).astype(jnp.int32)
    order, cnt = _active_first(active)
    slots = jnp.take_along_axis(jnp.broadcast_to(slot_of_tile[:, None, :], order.shape), order, axis=-1)
    return nsa_selected(order.reshape(-1), slots.reshape(-1), cnt.reshape(-1), qt, gates, sel, ks, vst, part,
                        tq=tq, tk=tk, pos0_fn=pos0_fn)


def _tri_inverse(ms, c):
    eye = (lax.broadcasted_iota(jnp.int32, (c, c), 0) == lax.broadcasted_iota(jnp.int32, (c, c), 1)).astype(F32)
    ps = [-m for m in ms]
    ts = [eye + p for p in ps]
    steps = max(int(math.ceil(math.log2(c))) - 1, 0)
    for _ in range(steps):
        ps = [_mm3(p, p) for p in ps]
        ts = [t + _mm3(t, p) for t, p in zip(ts, ps)]
    return ts


def _rwkv_body(rw_ref, rw0_ref, s0_ref, mu_ref, vec_ref, w2_ref, a2_ref, g2_ref, seg_ref, rk_ref,
               o_ref, sfin_ref, buf_ref, s_ref, y_ref, *, c, valid, n_chunks):
    ci = pl.program_id(1)
    halo = 8

    @pl.when(ci == 0)
    def _():
        buf_ref[0:halo, :] = rw0_ref[0]
        s_ref[...] = s0_ref[0]

    cur = rw_ref[...]
    buf_ref[halo:halo + c, :] = cur
    prev = buf_ref[halo - 1:halo - 1 + c, :]
    xr = cur + (prev - cur) * mu_ref[...]
    buf_ref[0:halo, :] = cur[c - halo:, :]

    vec = vec_ref[...]
    w0, a0, kkw, kaw, ln_w, ln_b = (vec[r:r + 1, :] for r in range(6))
    r = xr[:, 0:512]
    kx = xr[:, 512:1024]
    v = xr[:, 1024:1536]
    xw = xr[:, 1536:1664]
    xa = xr[:, 1664:1792]
    xg = xr[:, 1792:1920]
    wl = -jnp.exp(-_softplus(-(w0 + _mm(jnp.tanh(xw), w2_ref[...]))) - 0.5)
    a = _sigmoid(a0 + _mm(xa, a2_ref[...]))
    gate = _mm(_sigmoid(xg), g2_ref[...])
    seg = seg_ref[...]
    zk = kx * kkw
    k2 = kx * (1.0 + (a - 1.0) * kaw)
    zz_sum, rk_sum = _rows_times01([zk * zk, r * k2 * rk_ref[...]], seg)
    kk = zk * lax.rsqrt(zz_sum + EPS)
    bonus = rk_sum * v
    if valid < c:
        live = lax.broadcasted_iota(jnp.int32, (c, 1), 0) < valid
        wl = jnp.where(live, wl, 0.0)
        kk = jnp.where(live, kk, 0.0)
        k2 = jnp.where(live, k2, 0.0)
        v = jnp.where(live, v, 0.0)
        r = jnp.where(live, r, 0.0)
    bb = kk * a

    ri = lax.broadcasted_iota(jnp.int32, (c, c), 0)
    cj = lax.broadcasted_iota(jnp.int32, (c, c), 1)
    tril = ri >= cj
    strict = ri > cj
    cw = _mm01(tril, wl)
    ecw = jnp.exp(cw)
    einv = jnp.exp(-cw)
    p_c = ecw[c - 1:c, :]
    kt = kk * jnp.exp(cw - wl)
    bt = bb * einv
    ki = k2 * einv
    rt = r * ecw
    bd = bt * p_c
    kd = ki * p_c

    heads = range(RWKV_HEADS)
    sls = [slice(h * RWKV_HD, (h + 1) * RWKV_HD) for h in heads]
    kt_h = [kt[:, sl] for sl in sls]
    bt_h = [bt[:, sl] for sl in sls]
    ki_h = [ki[:, sl] for sl in sls]
    rt_h = [rt[:, sl] for sl in sls]
    v_h = [v[:, sl] for sl in sls]
    cat0 = lambda a, b: jnp.concatenate([a, b], axis=0)
    cat1 = lambda a, b: jnp.concatenate([a, b], axis=1)
    l_m = [jnp.where(strict, _mm3(kt_h[h], bt_h[h], NT), 0.0) for h in heads]
    quad = [_mm(cat0(kt_h[h], rt_h[h]), cat0(bt_h[h], ki_h[h]), NT) for h in heads]
    m_kk = [jnp.where(strict, q[0:c, c:2 * c], 0.0) for q in quad]
    a_rb = [jnp.where(tril, q[c:2 * c, 0:c], 0.0) for q in quad]
    a_rk = [jnp.where(tril, q[c:2 * c, c:2 * c], 0.0) for q in quad]
    mvy = [_mm(cat0(m_kk[h], a_rk[h]), v_h[h]) for h in heads]
    t_inv = _tri_inverse(l_m, c)
    wu = [_mm3(t_inv[h], cat1(kt_h[h], mvy[h][0:c])) for h in heads]
    s_h = [s_ref[h] for h in heads]
    ws = [_mm(cat0(wu[h][:, 0:RWKV_HD], rt_h[h]), s_h[h], NT) for h in heads]
    e_h = [-wu[h][:, RWKV_HD:2 * RWKV_HD] - ws[h][0:c] for h in heads]
    y_h = [ws[h][c:2 * c] + mvy[h][c:2 * c] + _mm(a_rb[h], e_h[h]) for h in heads]
    ds = [_mm(cat0(e_h[h], v_h[h]), cat0(bd[:, sls[h]], kd[:, sls[h]]), TN) for h in heads]
    for h in heads:
        s_ref[h] = s_h[h] * p_c[:, sls[h]] + ds[h]
        mu_h = jnp.mean(y_h[h], axis=-1, keepdims=True)
        d_h = y_h[h] - mu_h
        var_h = jnp.mean(d_h * d_h, axis=-1, keepdims=True)
        y_ref[:, sls[h]] = d_h * lax.rsqrt(var_h + RWKV_GN_EPS)

    o_ref[...] = (y_ref[...] * ln_w + ln_b + bonus) * gate

    @pl.when(ci == n_chunks - 1)
    def _():
        sfin_ref[0] = s_ref[...]


def rwkv_mix(rw, rw0, s0, mu, vec, w2, a2, g2, seg, rk, *, c, valid):
    b = s0.shape[0]
    rows = rw.shape[0]
    n_chunks = rows // (b * c)
    const = lambda s: pl.BlockSpec(s, lambda bi, ci: tuple(0 for _ in s))
    return pl.pallas_call(
        functools.partial(_rwkv_body, c=c, valid=valid, n_chunks=n_chunks),
        grid=(b, n_chunks),
        in_specs=[pl.BlockSpec((c, RW_COLS), lambda bi, ci: (bi * n_chunks + ci, 0)),
                  pl.BlockSpec((1, 8, RW_COLS), lambda bi, ci: (bi, 0, 0)),
                  pl.BlockSpec((1, RWKV_HEADS, 64, 64), lambda bi, ci: (bi, 0, 0, 0)),
                  const((1, RW_COLS)), const((8, 512)), const((128, 512)), const((128, 512)), const((128, 512)),
                  const((512, 512)), const((1, 512))],
        out_specs=[pl.BlockSpec((c, 512), lambda bi, ci: (bi * n_chunks + ci, 0)),
                   pl.BlockSpec((1, RWKV_HEADS, 64, 64), lambda bi, ci: (bi, 0, 0, 0))],
        out_shape=[jax.ShapeDtypeStruct((rows, 512), F32),
                   jax.ShapeDtypeStruct((b, RWKV_HEADS, 64, 64), F32)],
        scratch_shapes=[pltpu.VMEM((8 + c, RW_COLS), F32), pltpu.VMEM((RWKV_HEADS, 64, 64), F32),
                        pltpu.VMEM((c, 512), F32)],
        compiler_params=_cparams(("arbitrary", "arbitrary")),
        name="rwkv_mix",
    )(rw, rw0, s0, mu, vec, w2, a2, g2, seg, rk)


def _out_proj_body(*refs, n_in):
    a_refs = refs[:n_in]
    w_refs = refs[n_in:2 * n_in]
    x_ref, g_ref, o_ref = refs[2 * n_in:]
    y = _mm(a_refs[0][...], w_refs[0][...])
    for a_ref, w_ref in zip(a_refs[1:], w_refs[1:]):
        y = y + _mm(a_ref[...], w_ref[...])
    o_ref[...] = x_ref[...] + g_ref[...] * y


def out_proj(acts, weights, x, gate, tm):
    n, d = x.shape
    n_in = len(acts)
    return pl.pallas_call(
        functools.partial(_out_proj_body, n_in=n_in),
        grid=(n // tm,),
        in_specs=[pl.BlockSpec((tm, a.shape[1]), lambda i: (i, 0)) for a in acts]
        + [pl.BlockSpec(w.shape, lambda i: (0, 0)) for w in weights]
        + [pl.BlockSpec((tm, d), lambda i: (i, 0)), _mod_spec(gate.shape[0], tm, d)],
        out_specs=pl.BlockSpec((tm, d), lambda i: (i, 0)),
        out_shape=jax.ShapeDtypeStruct((n, d), F32),
        compiler_params=_cparams(("arbitrary",)),
        name="out_proj",
    )(*acts, *weights, x, gate)


def _odd_proj_body(x_ref, nw_ref, sc_ref, sh_ref, w_ref, qkv_ref, z_ref, ba_ref):
    hb = _norm_mod(x_ref[...], nw_ref[...], sc_ref[...], sh_ref[...]).astype(BF16)
    qkv_ref[...] = _mm(hb, w_ref[:, 0:3072])
    z_ref[...] = _mm(hb, w_ref[:, 3072:4096])
    ba_ref[...] = _mm(hb, w_ref[:, 4096:O_COLS])


def odd_proj(x, nw, sc, sh, w_packed, tm):
    n, d = x.shape
    rows_mod = sc.shape[0]
    row = lambda c: pl.BlockSpec((tm, c), lambda i: (i, 0))
    return pl.pallas_call(
        _odd_proj_body,
        grid=(n // tm,),
        in_specs=[row(d), pl.BlockSpec((1, d), lambda i: (0, 0)),
                  _mod_spec(rows_mod, tm, d), _mod_spec(rows_mod, tm, d),
                  pl.BlockSpec((d, O_COLS), lambda i: (0, 0))],
        out_specs=[row(3072), row(1024), row(128)],
        out_shape=[jax.ShapeDtypeStruct((n, 3072), F32), jax.ShapeDtypeStruct((n, 1024), F32),
                   jax.ShapeDtypeStruct((n, 128), F32)],
        compiler_params=_cparams(("arbitrary",)),
        name="odd_proj",
    )(x, nw, sc, sh, w_packed)


def _gdn_body(qkv_ref, z_ref, ba_ref, cs_ref, s0_ref, cw_ref, hp_ref, nw_ref,
              o_ref, sfin_ref, buf_ref, s_ref, *, c, valid, n_chunks):
    ci = pl.program_id(1)
    halo = 8

    @pl.when(ci == 0)
    def _():
        buf_ref[0:halo, :] = cs_ref[0]
        s_ref[...] = s0_ref[0]

    x = qkv_ref[...]
    buf_ref[halo:halo + c, :] = x
    cw = cw_ref[...]
    conv = buf_ref[halo - 3:halo - 3 + c, :] * cw[0:1, :]
    for j in range(1, CONV_W):
        conv = conv + buf_ref[halo - 3 + j:halo - 3 + j + c, :] * cw[j:j + 1, :]
    buf_ref[0:halo, :] = x[c - halo:, :]
    conv = _silu(conv)

    hp = hp_ref[...]
    ba = ba_ref[...]
    beta_f = _sigmoid(ba)
    g_f = hp[0:1, :] * _softplus(ba + hp[1:2, :])
    if valid < c:
        live = lax.broadcasted_iota(jnp.int32, (c, 1), 0) < valid
        beta_f = jnp.where(live, beta_f, 0.0)
        g_f = jnp.where(live, g_f, 0.0)
        conv = jnp.where(live, conv, 0.0)

    ri = lax.broadcasted_iota(jnp.int32, (c, c), 0)
    cj = lax.broadcasted_iota(jnp.int32, (c, c), 1)
    tril = ri >= cj
    strict = ri > cj
    gc = _mm01(tril, g_f)
    gct = gc.T
    z = z_ref[...]
    nw = nw_ref[...]

    heads = range(GDN_HEADS)
    sls = [slice(h * GDN_HD, (h + 1) * GDN_HD) for h in heads]
    q_h = [conv[:, sl] for sl in sls]
    k_h = [conv[:, GDN_W + h * GDN_HD:GDN_W + (h + 1) * GDN_HD] for h in heads]
    v_h = [conv[:, 2 * GDN_W + h * GDN_HD:2 * GDN_W + (h + 1) * GDN_HD] for h in heads]
    q_h = [q * lax.rsqrt(jnp.sum(q * q, axis=-1, keepdims=True) + EPS) * (GDN_HD ** -0.5) for q in q_h]
    k_h = [k * lax.rsqrt(jnp.sum(k * k, axis=-1, keepdims=True) + EPS) for k in k_h]
    g_col = [gc[:, 8 + h:9 + h] for h in heads]
    eg = [jnp.exp(g) for g in g_col]
    b_col = [beta_f[:, h:h + 1] for h in heads]
    decay = [jnp.where(tril, jnp.exp(jnp.where(tril, g_col[h] - gct[8 + h:9 + h, :], 0.0)), 0.0) for h in heads]
    kb = [k_h[h] * b_col[h] for h in heads]
    vb = [v_h[h] * b_col[h] for h in heads]
    cat0 = lambda a, b: jnp.concatenate([a, b], axis=0)
    mq = [_mm3(cat0(kb[h], q_h[h]), k_h[h], NT) for h in heads]
    m_h = [jnp.where(strict, mq[h][0:c] * decay[h], 0.0) for h in heads]
    qk = [jnp.where(tril, mq[h][c:2 * c] * decay[h], 0.0) for h in heads]
    t_inv = _tri_inverse(m_h, c)
    uw = [_mm(t_inv[h], jnp.concatenate([vb[h], kb[h] * eg[h]], axis=1)) for h in heads]
    s_h = [s_ref[h] for h in heads]
    wq = [_mm(cat0(uw[h][:, GDN_HD:2 * GDN_HD], q_h[h] * eg[h]), s_h[h]) for h in heads]
    v_new = [uw[h][:, 0:GDN_HD] - wq[h][0:c] for h in heads]
    o_h = [wq[h][c:2 * c] + _mm(qk[h], v_new[h]) for h in heads]
    g_last = [g[c - 1:c, :] for g in g_col]
    ds = [_mm(k_h[h] * jnp.exp(g_last[h] - g_col[h]), v_new[h], TN) for h in heads]
    for h in heads:
        s_ref[h] = s_h[h] * jnp.exp(g_last[h]) + ds[h]
        o_n = o_h[h] * lax.rsqrt(jnp.mean(o_h[h] * o_h[h], axis=-1, keepdims=True) + EPS) * nw
        o_ref[:, sls[h]] = o_n * _silu(z[:, sls[h]])

    @pl.when(ci == n_chunks - 1)
    def _():
        sfin_ref[0] = s_ref[...]


def gdn_mix(qkv, z, ba, cs, s0, conv_w8, hp, nw, *, c, valid):
    b = s0.shape[0]
    rows = qkv.shape[0]
    n_chunks = rows // (b * c)
    const = lambda s: pl.BlockSpec(s, lambda bi, ci: tuple(0 for _ in s))
    row = lambda w: pl.BlockSpec((c, w), lambda bi, ci: (bi * n_chunks + ci, 0))
    return pl.pallas_call(
        functools.partial(_gdn_body, c=c, valid=valid, n_chunks=n_chunks),
        grid=(b, n_chunks),
        in_specs=[row(3072), row(1024), row(128),
                  pl.BlockSpec((1, 8, 3072), lambda bi, ci: (bi, 0, 0)),
                  pl.BlockSpec((1, GDN_HEADS, 128, 128), lambda bi, ci: (bi, 0, 0, 0)),
                  const((8, 3072)), const((8, 128)), const((1, 128))],
        out_specs=[row(1024), pl.BlockSpec((1, GDN_HEADS, 128, 128), lambda bi, ci: (bi, 0, 0, 0))],
        out_shape=[jax.ShapeDtypeStruct((rows, 1024), F32),
                   jax.ShapeDtypeStruct((b, GDN_HEADS, 128, 128), F32)],
        scratch_shapes=[pltpu.VMEM((8 + c, 3072), F32), pltpu.VMEM((GDN_HEADS, 128, 128), F32)],
        compiler_params=_cparams(("arbitrary", "arbitrary")),
        name="gdn_mix",
    )(qkv, z, ba, cs, s0, conv_w8, hp, nw)


def _router_body(x_ref, nw_ref, sc_ref, sh_ref, wr_ref, br_ref, h_ref, gate_ref):
    h = _norm_mod(x_ref[...], nw_ref[...], sc_ref[...], sh_ref[...])
    h_ref[...] = h.astype(BF16)
    logits = _mmh(h, wr_ref[...]) + br_ref[...]
    tm = logits.shape[0]
    lane = lax.broadcasted_iota(jnp.int32, (tm, LANE), 1)
    is_grp = (lane >= N_EXPERTS) & (lane < N_EXPERTS + N_GROUPS)
    gl = jnp.where(is_grp, logits, NEG)
    gmax = jnp.max(gl, axis=-1, keepdims=True)
    g_idx = jnp.min(jnp.where(gl == gmax, lane, 4 * LANE), axis=-1, keepdims=True) - N_EXPERTS
    g_w = 1.0 / jnp.sum(jnp.where(is_grp, jnp.exp(gl - gmax), 0.0), axis=-1, keepdims=True)
    in_grp = (lane < N_EXPERTS) & (lane // EXP_PER_GROUP == g_idx)
    el = jnp.where(in_grp, logits, NEG)
    emax = jnp.max(el, axis=-1, keepdims=True)
    e = jnp.where(in_grp, jnp.exp(el - emax), 0.0)
    p = e / jnp.sum(e, axis=-1, keepdims=True)
    p1 = jnp.where(in_grp, p, -1.0)
    m1 = jnp.max(p1, axis=-1, keepdims=True)
    i1 = jnp.min(jnp.where(p1 == m1, lane, 4 * LANE), axis=-1, keepdims=True)
    p2 = jnp.where(lane == i1, -1.0, p1)
    m2 = jnp.max(p2, axis=-1, keepdims=True)
    i2 = jnp.min(jnp.where(p2 == m2, lane, 4 * LANE), axis=-1, keepdims=True)
    tot = m1 + m2
    gate = jnp.where(lane == i1, m1 / tot * g_w, jnp.where(lane == i2, m2 / tot * g_w, 0.0))
    gate_ref[...] = jnp.where(lane == GRP_LANE, g_idx.astype(F32), gate)


def moe_router(x, nw, sc, sh, w_r, b_r, tm):
    n, d = x.shape
    rows_mod = sc.shape[0]
    return pl.pallas_call(
        _router_body,
        grid=(n // tm,),
        in_specs=[pl.BlockSpec((tm, d), lambda i: (i, 0)), pl.BlockSpec((1, d), lambda i: (0, 0)),
                  _mod_spec(rows_mod, tm, d), _mod_spec(rows_mod, tm, d),
                  pl.BlockSpec((d, LANE), lambda i: (0, 0)), pl.BlockSpec((1, LANE), lambda i: (0, 0))],
        out_specs=[pl.BlockSpec((tm, d), lambda i: (i, 0)), pl.BlockSpec((tm, LANE), lambda i: (i, 0))],
        out_shape=[jax.ShapeDtypeStruct((n, d), BF16), jax.ShapeDtypeStruct((n, LANE), F32)],
        compiler_params=_cparams(("arbitrary",)),
        name="moe_router",
    )(x, nw, sc, sh, w_r, b_r)


def _moe_body(h_ref, gate_ref, w1_ref, w3_ref, w2_ref, x_ref, g2_ref, o_ref, acc_ref):
    e = pl.program_id(1)

    @pl.when(e == 0)
    def _():
        acc_ref[...] = jnp.zeros_like(acc_ref)

    hb = h_ref[...]
    he = _silu(_mm(hb, w1_ref[0])) * _mm(hb, w3_ref[0])
    y = _mm(he, w2_ref[0])
    gate = gate_ref[...]
    lane = lax.broadcasted_iota(jnp.int32, gate.shape, 1)
    ge = jnp.sum(jnp.where(lane == e, gate, 0.0), axis=-1, keepdims=True)
    acc_ref[...] += ge * y

    @pl.when(e == pl.num_programs(1) - 1)
    def _():
        o_ref[...] = x_ref[...] + g2_ref[...] * acc_ref[...]


def moe_ffn(h, gate, w1, w3, w2, e0, x, g2, tm):
    n, d = x.shape
    de = w1.shape[2]
    return pl.pallas_call(
        _moe_body,
        grid=(n // tm, N_EXPERTS),
        in_specs=[pl.BlockSpec((tm, d), lambda i, e: (i, 0)), pl.BlockSpec((tm, LANE), lambda i, e: (i, 0)),
                  pl.BlockSpec((1, d, de), lambda i, e: (e0 + e, 0, 0)),
                  pl.BlockSpec((1, d, de), lambda i, e: (e0 + e, 0, 0)),
                  pl.BlockSpec((1, de, d), lambda i, e: (e0 + e, 0, 0)),
                  pl.BlockSpec((tm, d), lambda i, e: (i, 0)),
                  pl.BlockSpec((1, d), lambda i, e: (0, 0)) if g2.shape[0] == 1
                  else pl.BlockSpec((tm, d), lambda i, e: (i, 0))],
        out_specs=pl.BlockSpec((tm, d), lambda i, e: (i, 0)),
        out_shape=jax.ShapeDtypeStruct((n, d), F32),
        scratch_shapes=[pltpu.VMEM((tm, d), F32)],
        compiler_params=_cparams(("arbitrary", "arbitrary")),
        name="moe_ffn",
    )(h, gate, w1, w3, w2, x, g2)


def _moe_plan(grp, tm, tb, cap, max_entries):
    nt = grp.shape[0] // tm
    cnt = jax.nn.one_hot(grp, N_GROUPS, dtype=jnp.int32).reshape(nt, tm, N_GROUPS).sum(axis=1)
    pc = (cnt + ROW_ALIGN - 1) // ROW_ALIGN * ROW_ALIGN
    segb = jnp.cumsum(pc, axis=1) - pc
    off = jnp.cumsum(pc, axis=0) - pc
    tot = pc.sum(axis=0)
    n_real = (tot + tb - 1) // tb
    n_all = jnp.minimum((tot + tm + tb - 1) // tb, cap // tb)
    ends = jnp.cumsum(n_all)
    s = jnp.arange(max_entries)
    g_of = jnp.sum(s[:, None] >= ends[None, :], axis=1)
    active = g_of < N_GROUPS
    g_c = jnp.minimum(g_of, N_GROUPS - 1)
    rt = s - (ends - n_all)[g_c]
    kind = jnp.where(active, jnp.where(rt < n_real[g_c], 1, 2), 0)
    last = ends[-1] - 1
    e_grp = jnp.where(active, g_c, g_c[last])
    e_rt = jnp.where(active, rt, rt[last])
    i32 = lambda a: a.reshape(-1).astype(jnp.int32)
    return i32(segb), i32(off // ROW_ALIGN), i32(e_grp), i32(e_rt), i32(kind)


def _group_perm(gate, segb_ref, base, tm, rows):
    gt = gate.T
    grp = gt[GRP_LANE:GRP_LANE + 1, :]
    gi = lax.broadcasted_iota(jnp.int32, (8, tm), 0).astype(F32)
    oh = jnp.where(gi == grp, 1.0, 0.0)
    r_i = lax.broadcasted_iota(jnp.int32, (tm, tm), 0)
    c_i = lax.broadcasted_iota(jnp.int32, (tm, tm), 1)
    before = jnp.where(r_i < c_i, 1.0, 0.0).astype(BF16)
    rank = lax.dot_general(oh.astype(BF16), before, NN, preferred_element_type=F32)
    dest = jnp.zeros((1, tm), F32)
    for g in range(N_GROUPS):
        dest = dest + oh[g:g + 1] * (segb_ref[base + g].astype(F32) + rank[g:g + 1])
    rows_i = lax.broadcasted_iota(jnp.int32, (rows, tm), 0).astype(F32)
    return jnp.where(rows_i == dest, 1.0, 0.0).astype(BF16)


def _moe_dispatch_body(segb_ref, off_ref, h_ref, gate_ref, xg_in, gg_in, xg_ref, gg_ref, xs_ref, gs_ref, *, tm, rows):
    del off_ref, xg_in, gg_in
    i = pl.program_id(0)
    g = pl.program_id(1)

    @pl.when((i == 0) & (g == 0))
    def _():
        xs_ref[...] = jnp.zeros_like(xs_ref)
        gs_ref[...] = jnp.zeros_like(gs_ref)

    @pl.when(g == 0)
    def _():
        gate = gate_ref[...]
        p = _group_perm(gate, segb_ref, i * N_GROUPS, tm, rows)
        xs_ref[0:rows, :] = lax.dot_general(p, h_ref[...], NN, preferred_element_type=F32).astype(BF16)
        gs_ref[0:rows, :] = _mm01(p, gate)

    start = pl.multiple_of(segb_ref[i * N_GROUPS + g], ROW_ALIGN)
    xg_ref[...] = xs_ref[pl.ds(start, tm), :]
    gg_ref[...] = gs_ref[pl.ds(start, tm), :]


def moe_dispatch(h, gate, segb, off, tm, cap):
    n, d = h.shape
    rows = tm + N_GROUPS * ROW_ALIGN
    win = lambda w: pl.BlockSpec((pl.Element(tm), pl.Element(w)),
                                 lambda i, g, segb, off: ((g * (cap // ROW_ALIGN) + off[i * N_GROUPS + g]) * ROW_ALIGN, 0))
    grid_spec = pltpu.PrefetchScalarGridSpec(
        num_scalar_prefetch=2,
        grid=(n // tm, N_GROUPS),
        in_specs=[pl.BlockSpec((tm, d), lambda i, g, *_: (i, 0)), pl.BlockSpec((tm, LANE), lambda i, g, *_: (i, 0)),
                  pl.BlockSpec(memory_space=pl.ANY), pl.BlockSpec(memory_space=pl.ANY)],
        out_specs=[win(d), win(LANE)],
        scratch_shapes=[pltpu.VMEM((2 * tm + N_GROUPS * ROW_ALIGN, d), BF16),
                        pltpu.VMEM((2 * tm + N_GROUPS * ROW_ALIGN, LANE), F32)],
    )
    return pl.pallas_call(
        functools.partial(_moe_dispatch_body, tm=tm, rows=rows),
        grid_spec=grid_spec,
        out_shape=[jax.ShapeDtypeStruct((N_GROUPS * cap, d), BF16), jax.ShapeDtypeStruct((N_GROUPS * cap, LANE), F32)],
        input_output_aliases={4: 0, 5: 1},
        compiler_params=_cparams(("arbitrary", "arbitrary")),
        name="moe_dispatch",
    )(segb, off, h, gate, jnp.zeros((N_GROUPS * cap, d), BF16), jnp.zeros((N_GROUPS * cap, LANE), F32))


def _moe_group_body(grp_ref, rt_ref, kind_ref, xg_ref, gg_ref, w1_ref, w3_ref, w2_ref, yg_ref, acc_ref):
    del rt_ref
    s = pl.program_id(0)
    e = pl.program_id(1)
    kind = kind_ref[s]
    last = e == pl.num_programs(1) - 1

    @pl.when(kind == 1)
    def _():
        @pl.when(e == 0)
        def _():
            acc_ref[...] = jnp.zeros_like(acc_ref)

        xb = xg_ref[...]
        he = _silu(_mm(xb, w1_ref[0])) * _mm(xb, w3_ref[0])
        y = _mm(he, w2_ref[0])
        gate = gg_ref[...]
        lane = lax.broadcasted_iota(jnp.int32, gate.shape, 1)
        ge = jnp.sum(jnp.where(lane == grp_ref[s] * EXP_PER_GROUP + e, gate, 0.0), axis=-1, keepdims=True)
        acc_ref[...] += ge * y

        @pl.when(last)
        def _():
            yg_ref[...] = acc_ref[...]

    @pl.when((kind == 2) & last)
    def _():
        yg_ref[...] = jnp.zeros_like(yg_ref)


def moe_group_ffn(e_grp, e_rt, e_kind, xg, gg, w1, w3, w2, e0, tb, cap):
    d = xg.shape[1]
    de = w1.shape[2]
    row = lambda s, e, grp, rt, kind: (grp[s] * (cap // tb) + rt[s], 0)
    wsel = lambda s, e, grp, rt, kind: (e0 + grp[s] * EXP_PER_GROUP + jnp.where(kind[s] == 1, e, EXP_PER_GROUP - 1),
                                        0, 0)
    grid_spec = pltpu.PrefetchScalarGridSpec(
        num_scalar_prefetch=3,
        grid=(e_grp.shape[0], EXP_PER_GROUP),
        in_specs=[pl.BlockSpec((tb, d), row), pl.BlockSpec((tb, LANE), row),
                  pl.BlockSpec((1, d, de), wsel), pl.BlockSpec((1, d, de), wsel), pl.BlockSpec((1, de, d), wsel)],
        out_specs=pl.BlockSpec((tb, d), row),
        scratch_shapes=[pltpu.VMEM((tb, d), F32)],
    )
    return pl.pallas_call(
        _moe_group_body,
        grid_spec=grid_spec,
        out_shape=jax.ShapeDtypeStruct((N_GROUPS * cap, d), F32),
        compiler_params=_cparams(("arbitrary", "arbitrary")),
        name="moe_group_ffn",
    )(e_grp, e_rt, e_kind, xg, gg, w1, w3, w2)


def _moe_combine_body(segb_ref, off_ref, yg_ref, gate_ref, x_ref, g2_ref, o_ref, ys_ref, *, tm, rows):
    del off_ref
    i = pl.program_id(0)
    g = pl.program_id(1)

    @pl.when((i == 0) & (g == 0))
    def _():
        ys_ref[...] = jnp.zeros_like(ys_ref)

    start = pl.multiple_of(segb_ref[i * N_GROUPS + g], ROW_ALIGN)
    ys_ref[pl.ds(start, tm), :] = yg_ref[...]

    @pl.when(g == N_GROUPS - 1)
    def _():
        p = _group_perm(gate_ref[...], segb_ref, i * N_GROUPS, tm, rows)
        yh, yl = _split(ys_ref[0:rows, :])
        y = (lax.dot_general(p, yh, TN, preferred_element_type=F32)
             + lax.dot_general(p, yl, TN, preferred_element_type=F32))
        o_ref[...] = x_ref[...] + g2_ref[...] * y


def moe_combine(yg, gate, x, g2, segb, off, tm, cap):
    n, d = x.shape
    rows = tm + N_GROUPS * ROW_ALIGN
    grid_spec = pltpu.PrefetchScalarGridSpec(
        num_scalar_prefetch=2,
        grid=(n // tm, N_GROUPS),
        in_specs=[pl.BlockSpec((pl.Element(tm), pl.Element(d)),
                               lambda i, g, segb, off: ((g * (cap // ROW_ALIGN) + off[i * N_GROUPS + g]) * ROW_ALIGN, 0)),
                  pl.BlockSpec((tm, LANE), lambda i, g, *_: (i, 0)),
                  pl.BlockSpec((tm, d), lambda i, g, *_: (i, 0)),
                  pl.BlockSpec((1, d), lambda i, g, *_: (0, 0))],
        out_specs=pl.BlockSpec((tm, d), lambda i, g, *_: (i, 0)),
        scratch_shapes=[pltpu.VMEM((2 * tm + N_GROUPS * ROW_ALIGN, d), F32)],
    )
    return pl.pallas_call(
        functools.partial(_moe_combine_body, tm=tm, rows=rows),
        grid_spec=grid_spec,
        out_shape=jax.ShapeDtypeStruct((n, d), F32),
        compiler_params=_cparams(("arbitrary", "arbitrary")),
        name="moe_combine",
    )(segb, off, yg, gate, x, g2)


def moe_grouped(h, gate, w1, w3, w2, e0, x, g2, tm):
    n = h.shape[0]
    tb = tm
    cap = n + 2 * tm
    max_entries = (n + (n // tm) * N_GROUPS * (ROW_ALIGN - 1) + N_GROUPS * tm) // tb + N_GROUPS + 1
    grp = gate[:, GRP_LANE].astype(jnp.int32)
    segb, off, e_grp, e_rt, e_kind = _moe_plan(grp, tm, tb, cap, max_entries)
    xg, gg = moe_dispatch(h, gate, segb, off, tm, cap)
    yg = moe_group_ffn(e_grp, e_rt, e_kind, xg, gg, w1, w3, w2, e0, tb, cap)
    return moe_combine(yg, gate, x, g2, segb, off, tm, cap)


def _final_norm_body(x_ref, w_ref, o_ref):
    x = x_ref[...]
    o_ref[...] = x * lax.rsqrt(jnp.mean(x * x, axis=-1, keepdims=True) + EPS) * w_ref[...]


def final_norm(x, w, tm):
    n, d = x.shape
    return pl.pallas_call(
        _final_norm_body,
        grid=(n // tm,),
        in_specs=[pl.BlockSpec((tm, d), lambda i: (i, 0)), pl.BlockSpec((1, d), lambda i: (0, 0))],
        out_specs=pl.BlockSpec((tm, d), lambda i: (i, 0)),
        out_shape=jax.ShapeDtypeStruct((n, d), F32),
        compiler_params=_cparams(("arbitrary",)),
        name="final_norm",
    )(x, w)


def _row_tile(n, pref):
    t = min(pref, n)
    while n % t:
        t //= 2
    return t


def kernel(x_prompt, x_sample, c_prompt, c_sample, cache_nsa_cmp, cache_nsa_sel, page_table, state_nsa_win, state_rwkv, state_rwkv_shift, state_gdn, state_gdn_conv, norm_mix, norm_ffn, norm_final, w_ada, b_ada, even_w_in, even_w_out, nsa_cmp_pos, nsa_cmp_w, rwkv_mu, rwkv_w0, rwkv_w2, rwkv_a0, rwkv_a2, rwkv_g2, rwkv_kk, rwkv_ka, rwkv_rk, rwkv_ln_w, rwkv_ln_b, odd_w_in, odd_w_out, gdn_conv_w, gdn_a_log, gdn_dt_bias, gdn_norm_w, moe_w_grp, moe_b_grp, moe_w_exp, moe_b_exp, moe_w1, moe_w3, moe_w2):
    bp, t, d = x_prompt.shape
    bs, ts, _ = x_sample.shape
    assert bp == 1 and ts <= SPAD and ts < CMP_BLK
    depth = norm_mix.shape[0]
    n_pages, page = page_table.shape[1], cache_nsa_cmp.shape[2]
    past = n_pages * page
    wb = state_nsa_win.shape[2]
    ns = bs * SPAD
    tq, tq_s, tk = 128, 32, 512
    tm_p = _row_tile(t, 512)
    tm_s = ns

    rows_c = -(-(1 + bs) // 8) * 8
    c_all = jnp.concatenate([c_prompt, c_sample, jnp.zeros((rows_c - 1 - bs, d), F32)], axis=0)
    ada = adaln(c_all, w_ada, b_ada)

    def mods(i):
        mp = [ada[i, 0:1, j * d:(j + 1) * d] for j in range(6)]
        ms = [jnp.repeat(ada[i, 1:1 + bs, j * d:(j + 1) * d], SPAD, axis=0) for j in range(6)]
        return mp, ms

    xp = x_prompt[0]
    xs = jnp.pad(x_sample, ((0, 0), (0, SPAD - ts), (0, 0))).reshape(ns, d)

    def unpad(a):
        return a.reshape(bs, SPAD, -1)[:, :ts]

    w1_all, w3_all, w2_all = (w.reshape((-1,) + w.shape[2:]) for w in (moe_w1, moe_w3, moe_w2))
    outs = {k: [] for k in ("cmp_p", "cmp_s", "sel_p", "sel_s", "win_p", "win_s", "rw_p", "rw_s", "sh_p", "sh_s",
                            "gd_p", "gd_s", "cv_p", "cv_s")}

    for i in range(depth):
        (sh1p, sc1p, gt1p, sh2p, sc2p, gt2p), (sh1s, sc1s, gt1s, sh2s, sc2s, gt2s) = mods(i)
        j = i // 2
        nw = norm_mix[i][None, :]
        if i % 2 == 0:
            w_packed = _pack_even_w(even_w_in[j])
            mu = _pack_rw_vec(rwkv_mu[j])
            wts, wc = _cmp_weights(nsa_cmp_pos[j], nsa_cmp_w[j])
            vec = jnp.stack([rwkv_w0[j], rwkv_a0[j], rwkv_kk[j], rwkv_ka[j], rwkv_ln_w[j], rwkv_ln_b[j],
                             jnp.zeros_like(rwkv_w0[j]), jnp.zeros_like(rwkv_w0[j])])
            pad_lora = lambda w: jnp.concatenate([w, jnp.zeros((128 - w.shape[0], w.shape[1]), w.dtype)], axis=0)
            w2p, a2p, g2p = pad_lora(rwkv_w2[j]), pad_lora(rwkv_a2[j]), rwkv_g2[j]
            hid = jnp.arange(RWKV_W) // RWKV_HD
            seg = (hid[:, None] == hid[None, :]).astype(F32)
            rk = rwkv_rk[j].reshape(1, RWKV_W)
            wo_nsa, wo_rw = even_w_out[j][:512].astype(BF16), even_w_out[j][512:].astype(BF16)

            kv, qt, gt, ks, vst, kw, vwt, rw, hl = even_proj(xp, nw, sc1p, sh1p, w_packed, tm_p, 8)
            kvc = compress_prompt(kv, wts, wc, _row_tile(t, 512))
            gates = gt[:24].reshape(NSA_KV_HEADS, 12, t)
            gates = jnp.pad(gates, ((0, 0), (0, 4), (0, 0)))[None]
            o_nsa = nsa_attention(
                qt[None], gates, kvc[None], kvc.T[None], ks[None], vst[None], kw[None], vwt[None],
                tq=tq, tk=tk, wk=WINDOW + tq,
                pos0_fn=lambda qi: qi * tq,
                wstart_fn=lambda qi: jnp.maximum(qi * tq - WINDOW, 0),
                wpos0_fn=lambda qi: jnp.maximum(qi * tq - WINDOW, 0))[0]
            o_rw, s_rw = rwkv_mix(rw, jnp.zeros((1, 8, RW_COLS), F32), jnp.zeros((1, RWKV_HEADS, 64, 64), F32),
                                  mu, vec, w2p, a2p, g2p, seg, rk, c=64, valid=64)
            xp = out_proj([o_nsa, o_rw], [wo_nsa, wo_rw], xp, gt1p, tm_p)
            outs["cmp_p"].append(kv[:, 0:256].reshape(1, t, 2, 2, 64))
            outs["sel_p"].append(kv[:, 256:512].reshape(1, t, 2, 2, 64))
            kvw_rows = kv[:, 512:768].reshape(1, t, 2, 2, 64)
            outs["win_p"].append(kvw_rows[:, -min(WINDOW, t):])
            outs["rw_p"].append(s_rw)
            outs["sh_p"].append(hl[-1:])

            kv, qt, gt, _, _, _, _, rw, hl = even_proj(xs, nw, sc1s, sh1s, w_packed, tm_s, ns)
            kv_new = unpad(kv)
            rw0 = small_matmul(jnp.pad(state_rwkv_shift[j], ((0, -bs % 8), (0, 0))), w_packed[:, E_RW:])[:bs]
            rw0 = jnp.pad(rw0[:, None, :], ((0, 0), (7, 0), (0, 0)))
            pool_cmp = cache_nsa_cmp[j].transpose(0, 2, 3, 4, 1).reshape(-1, 256, page)
            pool_sel = cache_nsa_sel[j].transpose(0, 2, 3, 4, 1).reshape(-1, 256, page)
            kvc_s = compress_paged(pool_cmp, page_table, nsa_cmp_pos[j], wc, math.gcd(n_pages, 16))
            tail = jnp.pad(kv_new[:, :, 256:512], ((0, 0), (0, tk - ts), (0, 0)))
            wbuf = state_nsa_win[j].reshape(bs, wb, 256)
            kvw_all = jnp.concatenate([wbuf, kv_new[:, :, 512:768]], axis=1)
            wk_s = -(-(wb + ts) // 128) * 128
            kvw_pad = jnp.pad(kvw_all, ((0, 0), (0, wk_s - wb - ts), (0, 0)))
            kw_s = kvw_pad[:, :, :128].astype(BF16)
            vwt_s = jnp.swapaxes(kvw_pad[:, :, 128:], 1, 2).astype(BF16)
            qt_s = jnp.pad(qt.reshape(512, bs, SPAD).transpose(1, 0, 2), ((0, 0), (0, 0), (0, tq_s - SPAD)))
            g_s = gt[:24].reshape(NSA_KV_HEADS, 12, bs, SPAD).transpose(2, 0, 1, 3)
            g_s = jnp.pad(g_s, ((0, 0), (0, 0), (0, 4), (0, tq_s - SPAD)))
            o_nsa = nsa_attention_paged(
                qt_s, g_s, kvc_s, jnp.swapaxes(kvc_s, 1, 2), pool_sel, page_table, tail, kw_s, vwt_s,
                tq=tq_s, tk=tk, wk=wk_s,
                pos0_fn=lambda qi: past,
                wstart_fn=lambda qi: 0,
                wpos0_fn=lambda qi: past - wb)
            o_nsa = o_nsa[:, :SPAD].reshape(ns, 512)
            o_rw, s_rw = rwkv_mix(rw, rw0, state_rwkv[j], mu, vec, w2p, a2p, g2p, seg, rk, c=SPAD, valid=ts)
            xs = out_proj([o_nsa, o_rw], [wo_nsa, wo_rw], xs, gt1s, tm_s)
            outs["cmp_s"].append(kv_new[:, :, 0:256].reshape(bs, ts, 2, 2, 64))
            outs["sel_s"].append(kv_new[:, :, 256:512].reshape(bs, ts, 2, 2, 64))
            outs["win_s"].append(kvw_all[:, -wb:].reshape(bs, wb, 2, 2, 64))
            outs["rw_s"].append(s_rw)
            outs["sh_s"].append(hl.reshape(bs, SPAD, d)[:, ts - 1])
        else:
            w_in = odd_w_in[j]
            w_packed = jnp.concatenate([w_in, jnp.zeros((d, O_COLS - w_in.shape[1]), F32)], axis=1).astype(BF16)
            conv_w8 = jnp.pad(gdn_conv_w[j], ((0, 8 - CONV_W), (0, 0)))
            hp = jnp.zeros((8, 128), F32)
            hp = hp.at[0, 8:16].set(-jnp.exp(gdn_a_log[j])).at[1, 8:16].set(gdn_dt_bias[j])
            gnw = gdn_norm_w[j][None, :]
            wo = odd_w_out[j].astype(BF16)

            qkv, z, ba = odd_proj(xp, nw, sc1p, sh1p, w_packed, tm_p)
            o_g, s_g = gdn_mix(qkv, z, ba, jnp.zeros((1, 8, 3 * GDN_W), F32),
                               jnp.zeros((1, GDN_HEADS, GDN_HD, GDN_HD), F32), conv_w8, hp, gnw, c=64, valid=64)
            xp = out_proj([o_g], [wo], xp, gt1p, tm_p)
            outs["gd_p"].append(s_g)
            outs["cv_p"].append(qkv[None, -(CONV_W - 1):])

            qkv, z, ba = odd_proj(xs, nw, sc1s, sh1s, w_packed, tm_s)
            cs = jnp.pad(state_gdn_conv[j], ((0, 0), (8 - (CONV_W - 1), 0), (0, 0)))
            o_g, s_g = gdn_mix(qkv, z, ba, cs, state_gdn[j], conv_w8, hp, gnw, c=SPAD, valid=ts)
            xs = out_proj([o_g], [wo], xs, gt1s, tm_s)
            xpad = jnp.concatenate([state_gdn_conv[j], unpad(qkv)], axis=1)
            outs["gd_s"].append(s_g)
            outs["cv_s"].append(xpad[:, -(CONV_W - 1):])

        nwf = norm_ffn[i][None, :]
        w_r = jnp.concatenate([moe_w_exp[i], moe_w_grp[i], jnp.zeros((d, LANE - N_EXPERTS - N_GROUPS), F32)], axis=1)
        b_r = jnp.concatenate([moe_b_exp[i], moe_b_grp[i], jnp.zeros((LANE - N_EXPERTS - N_GROUPS,), F32)])[None, :]
        h2, gate = moe_router(xp, nwf, sc2p, sh2p, w_r, b_r, tm_p)
        xp = moe_grouped(h2, gate, w1_all, w3_all, w2_all, i * N_EXPERTS, xp, gt2p, _row_tile(t, 1024))
        h2, gate = moe_router(xs, nwf, sc2s, sh2s, w_r, b_r, tm_s)
        xs = moe_ffn(h2, gate, w1_all, w3_all, w2_all, i * N_EXPERTS, xs, gt2s, tm_s)

    nf = norm_final[None, :]
    y_prompt = final_norm(xp, nf, tm_p)[None]
    y_sample = unpad(final_norm(xs, nf, tm_s))
    st = lambda key: jnp.stack(outs[key])
    return (y_prompt, y_sample, st("cmp_p"), st("cmp_s"), st("sel_p"), st("sel_s"), st("win_p"), st("win_s"),
            st("rw_p"), st("rw_s"), st("sh_p"), st("sh_s"), st("gd_p"), st("gd_s"), st("cv_p"), st("cv_s"))
```

```python
import functools
import math

import jax
import jax.numpy as jnp
from jax import lax
from jax.experimental import pallas as pl
from jax.experimental.pallas import tpu as pltpu

F32 = jnp.float32
BF16 = jnp.bfloat16
HIGHEST = lax.Precision.HIGHEST

NSA_HEADS = 8
NSA_KV_HEADS = 2
NSA_GROUP = 4
NSA_HD = 64
CMP_BLK = 64
SEL_BLK = 64
TOPK_BLK = 16
WINDOW = 512
FORCE_BONUS = 2.0 * NSA_GROUP
RWKV_HEADS = 8
RWKV_HD = 64
RWKV_W = 512
RWKV_GN_EPS = 64e-5
GDN_HEADS = 8
GDN_HD = 128
GDN_W = 1024
CONV_W = 4
N_GROUPS = 4
EXP_PER_GROUP = 8
N_EXPERTS = 32
EPS = 1e-6
NEG = -1e30

LANE = 128
GRP_LANE = 64
ROW_ALIGN = 16
SAMPLE_TILE_SLOTS = 8
SPAD = 8
VMEM_LIMIT = 56 * 1024 * 1024

NN = (((1,), (0,)), ((), ()))
NT = (((1,), (1,)), ((), ()))
TN = (((0,), (0,)), ((), ()))

E_Q, E_KV, E_G, E_RW = 0, 512, 1280, 1408
E_COLS = 1408 + 1920
RW_COLS = 1920
O_COLS = 3072 + 1024 + 128


def _mm(a, b, dims=NN):
    return lax.dot_general(a.astype(BF16), b.astype(BF16), dims, preferred_element_type=F32)


def _mmh(a, b, dims=NN):
    return lax.dot_general(a.astype(F32), b.astype(F32), dims, precision=HIGHEST, preferred_element_type=F32)


def _split(a):
    hi = a.astype(BF16)
    return hi, (a - hi.astype(F32)).astype(BF16)


def _mm3(a, b, dims=NN):
    ah, al = _split(a)
    bh, bl = _split(b)
    d = lambda x, y: lax.dot_general(x, y, dims, preferred_element_type=F32)
    return d(ah, bh) + (d(ah, bl) + d(al, bh))


def _split3(x):
    h1 = x.astype(BF16)
    r1 = x - h1.astype(F32)
    h2 = r1.astype(BF16)
    return h1, h2, (r1 - h2.astype(F32)).astype(BF16)


def _mm01(m01, x):
    m = m01.astype(BF16)
    parts = _split3(x)
    d = lambda y: lax.dot_general(m, y, NN, preferred_element_type=F32)
    return d(parts[0]) + (d(parts[1]) + d(parts[2]))


def _head_sums(xs, seg_half):
    r = xs[0].shape[0]
    half = seg_half.shape[0]
    pieces = [p[:, h0:h0 + half] for x in xs for p in _split3(x) for h0 in (0, half)]
    out = lax.dot_general(jnp.concatenate(pieces, axis=0), seg_half.astype(BF16), NN, preferred_element_type=F32)
    res = []
    for i in range(len(xs)):
        o = [out[(6 * i + u) * r:(6 * i + u + 1) * r] for u in range(6)]
        res.append(jnp.concatenate([o[0] + (o[2] + o[4]), o[1] + (o[3] + o[5])], axis=1))
    return res


def _sigmoid(x):
    return 1.0 / (1.0 + jnp.exp(-x))


def _silu(x):
    return x * _sigmoid(x)


def _softplus(x):
    return jnp.maximum(x, 0.0) + jnp.log(1.0 + jnp.exp(-jnp.abs(x)))


def _cparams(sem):
    return pltpu.CompilerParams(dimension_semantics=sem, vmem_limit_bytes=VMEM_LIMIT)


def _norm_mod(x, nw, sc, sh):
    y = x * lax.rsqrt(jnp.mean(x * x, axis=-1, keepdims=True) + EPS)
    return (y * nw) * (1.0 + sc) + sh


def _mod_spec(rows_mod, tm, d):
    if rows_mod == 1:
        return pl.BlockSpec((1, d), lambda i: (0, 0))
    return pl.BlockSpec((tm, d), lambda i: (i, 0))


def _adaln_body(c_ref, w_ref, b_ref, o_ref):
    o_ref[0] = _mmh(_silu(c_ref[...]), w_ref[0]) + b_ref[0]


def adaln(c_all, w_ada, b_ada):
    depth, d, n6 = w_ada.shape
    rows = c_all.shape[0]
    tn = 768
    return pl.pallas_call(
        _adaln_body,
        grid=(depth, n6 // tn),
        in_specs=[pl.BlockSpec((rows, d), lambda l, j: (0, 0)),
                  pl.BlockSpec((1, d, tn), lambda l, j: (l, 0, j)),
                  pl.BlockSpec((1, 1, tn), lambda l, j: (l, 0, j))],
        out_specs=pl.BlockSpec((1, rows, tn), lambda l, j: (l, 0, j)),
        out_shape=jax.ShapeDtypeStruct((depth, rows, n6), F32),
        compiler_params=_cparams(("arbitrary", "arbitrary")),
        name="adaln",
    )(c_all, w_ada, b_ada.reshape(depth, 1, n6))


def _even_proj_body(x_ref, nw_ref, sc_ref, sh_ref, w_ref,
                    kv_ref, qt_ref, gt_ref, ks_ref, vst_ref, kw_ref, vwt_ref, rw_ref, hl_ref):
    h = _norm_mod(x_ref[...], nw_ref[...], sc_ref[...], sh_ref[...])
    hl = hl_ref.shape[0]
    hl_ref[...] = h[h.shape[0] - hl:, :]
    hb = h.astype(BF16)
    q = _mm(hb, w_ref[:, E_Q:E_Q + 512]) * (NSA_HD ** -0.5)
    qt_ref[...] = q.T.astype(BF16)
    kv = _mm(hb, w_ref[:, E_KV:E_KV + 768])
    kv_ref[...] = kv
    ks_ref[...] = kv[:, 256:384].astype(BF16)
    vst_ref[...] = kv[:, 384:512].T.astype(BF16)
    kw_ref[...] = kv[:, 512:640].astype(BF16)
    vwt_ref[...] = kv[:, 640:768].T.astype(BF16)
    g = _sigmoid(_mm(hb, w_ref[:, E_G:E_G + 128]))
    gt_ref[...] = g.T
    rw_ref[...] = _mm(hb, w_ref[:, E_RW:E_RW + RW_COLS])


def even_proj(x, nw, sc, sh, w_packed, tm, hl_rows):
    n, d = x.shape
    rows_mod = sc.shape[0]
    row = lambda c: pl.BlockSpec((tm, c), lambda i: (i, 0))
    col = lambda r: pl.BlockSpec((r, tm), lambda i: (0, i))
    return pl.pallas_call(
        _even_proj_body,
        grid=(n // tm,),
        in_specs=[row(d), pl.BlockSpec((1, d), lambda i: (0, 0)),
                  _mod_spec(rows_mod, tm, d), _mod_spec(rows_mod, tm, d),
                  pl.BlockSpec((d, E_COLS), lambda i: (0, 0))],
        out_specs=[row(768), col(512), col(128), row(128), col(128), row(128), col(128), row(RW_COLS),
                   pl.BlockSpec((hl_rows, d), lambda i: (0, 0))],
        out_shape=[jax.ShapeDtypeStruct((n, 768), F32),
                   jax.ShapeDtypeStruct((512, n), BF16),
                   jax.ShapeDtypeStruct((128, n), F32),
                   jax.ShapeDtypeStruct((n, 128), BF16),
                   jax.ShapeDtypeStruct((128, n), BF16),
                   jax.ShapeDtypeStruct((n, 128), BF16),
                   jax.ShapeDtypeStruct((128, n), BF16),
                   jax.ShapeDtypeStruct((n, RW_COLS), F32),
                   jax.ShapeDtypeStruct((hl_rows, d), F32)],
        compiler_params=_cparams(("arbitrary",)),
        name="even_proj",
    )(x, nw, sc, sh, w_packed)


def _pack_even_w(w_in):
    d = w_in.shape[0]
    z = lambda c: jnp.zeros((d, c), w_in.dtype)
    nsa = 1304
    rw = w_in[:, nsa:]
    parts = [w_in[:, :1280], w_in[:, 1280:1304], z(104),
             rw[:, :1536], rw[:, 1536:1600], z(64), rw[:, 1600:1664], z(64), rw[:, 1664:1792]]
    return jnp.concatenate(parts, axis=1).astype(BF16)


def _pack_rw_vec(v):
    z = jnp.zeros((64,), v.dtype)
    return jnp.concatenate([v[:1536], v[1536:1600], z, v[1600:1664], z, v[1664:1792]])[None, :]


def _mm_body(x_ref, w_ref, o_ref):
    o_ref[...] = _mm(x_ref[...], w_ref[...])


def small_matmul(x, w):
    return pl.pallas_call(
        _mm_body,
        out_shape=jax.ShapeDtypeStruct((x.shape[0], w.shape[1]), F32),
        compiler_params=pltpu.CompilerParams(vmem_limit_bytes=VMEM_LIMIT),
        name="small_matmul",
    )(x, w)


def _compress_body(x_ref, wts_ref, wc_ref, o_ref):
    x = x_ref[...]
    nb = x.shape[0] // CMP_BLK
    pooled = jnp.sum(x.reshape(nb, CMP_BLK, x.shape[-1]) * wts_ref[...][None], axis=1)
    o_ref[...] = _mm(pooled, wc_ref[...])


def _compress_paged_body(pt_ref, *refs, pps):
    page_refs = refs[:pps]
    wp_ref, wc_ref, o_ref = refs[pps:]
    x = jnp.concatenate([r[0] for r in page_refs], axis=1)
    pooled_t = jnp.concatenate([_mm(x[0:128], wp_ref[0]), _mm(x[128:256], wp_ref[1])], axis=0)
    nb = o_ref.shape[1]
    o_ref[0] = _mm(pooled_t.T[:nb], wc_ref[...])


def _cmp_weights(pos_wts, w_c):
    wts = jnp.repeat(pos_wts.T, 128, axis=1)
    eye2 = jnp.eye(2, dtype=w_c.dtype)
    blocks = [jnp.kron(eye2, w_c[c]) for c in range(2)]
    z = jnp.zeros((128, 128), w_c.dtype)
    wc = jnp.concatenate([jnp.concatenate([blocks[0], z], axis=1),
                          jnp.concatenate([z, blocks[1]], axis=1)], axis=0)
    return wts, wc


def compress_prompt(kv, wts, wc, tr):
    t = kv.shape[0]
    nb = tr // CMP_BLK
    return pl.pallas_call(
        _compress_body,
        grid=(t // tr,),
        in_specs=[pl.BlockSpec((tr, 256), lambda i: (i, 0)),
                  pl.BlockSpec((CMP_BLK, 256), lambda i: (0, 0)),
                  pl.BlockSpec((256, 256), lambda i: (0, 0))],
        out_specs=pl.BlockSpec((nb, 256), lambda i: (i, 0)),
        out_shape=jax.ShapeDtypeStruct((t // CMP_BLK, 256), F32),
        compiler_params=_cparams(("arbitrary",)),
        name="compress_prompt",
    )(kv, wts, wc)


def compress_paged(pool_t, page_table, pos_wts, wc, pages_per_step):
    b, n_pages = page_table.shape
    page = pool_t.shape[2]
    pps = pages_per_step
    nb = pps * page // CMP_BLK
    p_idx = jnp.arange(pps * page)
    wp = jax.nn.one_hot(p_idx // CMP_BLK, LANE, dtype=F32)[None] * pos_wts[:, p_idx % CMP_BLK][:, :, None]

    def page_spec(u):
        return pl.BlockSpec((1, 256, page), lambda bi, g, pt: (pt[bi, g * pps + u], 0, 0))

    grid_spec = pltpu.PrefetchScalarGridSpec(
        num_scalar_prefetch=1,
        grid=(b, n_pages // pps),
        in_specs=[page_spec(u) for u in range(pps)] + [
            pl.BlockSpec((2, pps * page, LANE), lambda bi, g, pt: (0, 0, 0)),
            pl.BlockSpec((256, 256), lambda bi, g, pt: (0, 0))],
        out_specs=pl.BlockSpec((1, nb, 256), lambda bi, g, pt: (bi, g, 0)),
    )
    return pl.pallas_call(
        functools.partial(_compress_paged_body, pps=pps),
        grid_spec=grid_spec,
        out_shape=jax.ShapeDtypeStruct((b, n_pages * page // CMP_BLK, 256), F32),
        compiler_params=_cparams(("arbitrary", "arbitrary")),
        name="compress_paged",
    )(page_table, *([pool_t] * pps), wp, wc)


def _gather_sel_body(pt_ref, tiles_ref, cnt_ref, *refs, pps, n_page_steps, nt):
    del pt_ref
    page_refs = refs[:pps]
    tail_ref, ks_ref, vst_ref = refs[pps:]
    bi = pl.program_id(0)
    a = pl.program_id(1)
    j = tiles_ref[bi * nt + jnp.minimum(a, cnt_ref[bi] - 1)]
    live = a < cnt_ref[bi]

    @pl.when(live & (j < n_page_steps))
    def _():
        ks_ref[0] = jnp.concatenate([r[0][0:128].T for r in page_refs], axis=0).astype(BF16)
        vst_ref[0] = jnp.concatenate([r[0][128:256] for r in page_refs], axis=1).astype(BF16)

    @pl.when(live & (j >= n_page_steps))
    def _():
        x = tail_ref[0]
        ks_ref[0] = x[:, :128].astype(BF16)
        vst_ref[0] = x[:, 128:].T.astype(BF16)


def gather_sel(pool_t, page_table, tail, tk, tiles, cnt, n_slots):
    b, n_pages = page_table.shape
    page = pool_t.shape[2]
    pps = tk // page
    n_page_steps = n_pages // pps
    nt = n_page_steps + 1
    nk = n_slots * tk

    def slot(bi, a, pt, tiles, cnt):
        return jnp.minimum(a, cnt[bi] - 1)

    def page_spec(u):
        def index(bi, a, pt, tiles, cnt):
            j = tiles[bi * nt + slot(bi, a, pt, tiles, cnt)]
            return (pt[bi, jnp.minimum(j * pps + u, n_pages - 1)], 0, 0)
        return pl.BlockSpec((1, 256, page), index)

    grid_spec = pltpu.PrefetchScalarGridSpec(
        num_scalar_prefetch=3,
        grid=(b, jnp.max(cnt)),
        in_specs=[page_spec(u) for u in range(pps)] + [pl.BlockSpec((1, tk, 256), lambda bi, a, *_: (bi, 0, 0))],
        out_specs=[pl.BlockSpec((1, tk, 128), lambda bi, a, *s: (bi, slot(bi, a, *s), 0)),
                   pl.BlockSpec((1, 128, tk), lambda bi, a, *s: (bi, 0, slot(bi, a, *s)))],
    )
    return pl.pallas_call(
        functools.partial(_gather_sel_body, pps=pps, n_page_steps=n_page_steps, nt=nt),
        grid_spec=grid_spec,
        out_shape=[jax.ShapeDtypeStruct((b, nk, 128), BF16), jax.ShapeDtypeStruct((b, 128, nk), BF16)],
        compiler_params=_cparams(("arbitrary", "arbitrary")),
        name="gather_sel",
    )(page_table, tiles, cnt, *([pool_t] * pps), tail)


MASKED = -1e30
M_INIT = -1e29


def _nsa_query(qt_ref, k, tq):
    w4 = NSA_GROUP * tq
    qb = qt_ref[0].astype(F32)
    qcat = jnp.concatenate([qb[g * 64:(g + 1) * 64] for g in range(NSA_GROUP)], axis=1)
    q2 = jnp.concatenate([qcat, qcat], axis=0)
    row = lax.broadcasted_iota(jnp.int32, (128, w4), 0)
    qe = jnp.where(row // 64 == k, q2, 0.0)
    gidx = lax.broadcasted_iota(jnp.int32, (128, w4), 1) // tq
    base = jnp.where(k == 0, 0.5, 0.5 / 16.0)
    slope = base * jnp.where(gidx == 0, 1.0, jnp.where(gidx == 1, 0.5, jnp.where(gidx == 2, 0.25, 0.125)))
    mult = jnp.where(row == 0, 16.0, jnp.where(row == 1, 1.0, jnp.where(row == 2, 128.0,
                                                                         jnp.where(row == 3, 64.0, 0.0))))
    return jnp.concatenate([qe, slope * mult], axis=0).astype(BF16)


def _pos_features(rows, tile_rel):
    r = lax.broadcasted_iota(jnp.int32, (rows, LANE), 0)
    lane = lax.broadcasted_iota(jnp.int32, (rows, LANE), 1)
    ab = jnp.where(lane == 0, r // 16, jnp.where(lane == 1, r % 16, 0)).astype(F32)
    return jnp.where(lane == 2, tile_rel, ab).astype(BF16)


def _gate_rows(gb, j, tq):
    return jnp.concatenate([gb[g * 3 + j:g * 3 + j + 1, :] for g in range(NSA_GROUP)], axis=1)


def _nsa_select_body(qt_ref, g_ref, kvc_ref, kvct_ref, kw_ref, vwt_ref, part_ref, sel_ref, flag_ref, *,
                     tq, tk, wk, nbc, nb, pos0_fn, wstart_fn, wpos0_fn):
    i = pl.program_id(1)
    k = pl.program_id(2)
    w4 = NSA_GROUP * tq
    pos0 = pos0_fn(i)
    qa = _nsa_query(qt_ref, k, tq)
    pos_q = pos0 + lax.broadcasted_iota(jnp.int32, (1, w4), 1) % tq

    def softmax_cols(s, bad):
        s = jnp.where(bad, MASKED, s)
        m = jnp.maximum(jnp.max(s, axis=0, keepdims=True), M_INIT)
        e = jnp.exp(s - m)
        return e / jnp.maximum(jnp.sum(e, axis=0, keepdims=True), 1e-30)

    n_i = lax.broadcasted_iota(jnp.int32, (nbc, LANE), 0)
    lane_c = lax.broadcasted_iota(jnp.int32, (nbc, LANE), 1)
    feat_c = jnp.where(lane_c == 3, n_i - pos0 // CMP_BLK, 0).astype(F32).astype(BF16)
    kc = jnp.concatenate([kvc_ref[0][:, :128].astype(BF16), feat_c], axis=1)
    c_end = lax.broadcasted_iota(jnp.int32, (nbc, 1), 0) * CMP_BLK + (CMP_BLK - 1)
    p_c = softmax_cols(lax.dot_general(kc, qa, NN, preferred_element_type=F32), c_end > pos_q)
    vct = kvct_ref[0, pl.ds(pl.multiple_of(128 + k * 64, 64), 64), :]
    o_c = _mm(vct, p_c)

    imp = p_c[:, 0:tq]
    for g in range(1, NSA_GROUP):
        imp = imp + p_c[:, g * tq:(g + 1) * tq]
    if nb > nbc:
        imp = jnp.concatenate([imp, jnp.zeros((nb - nbc, tq), F32)], axis=0)
    blk = lax.broadcasted_iota(jnp.int32, (nb, tq), 0)
    cur = (pos0 + lax.broadcasted_iota(jnp.int32, (1, tq), 1)) // SEL_BLK
    forced = (blk == cur) | (blk == cur - 1) | (blk == 0)
    score = jnp.where(blk <= cur, imp + jnp.where(forced, FORCE_BONUS, 0.0), -1.0)
    for _ in range(min(TOPK_BLK, nb)):
        m = jnp.max(score, axis=0, keepdims=True)
        first = jnp.min(jnp.where(score == m, blk, nb), axis=0, keepdims=True)
        score = jnp.where(blk == first, -2.0, score)
    sel = jnp.where(score == -2.0, 1.0, 0.0)
    sel_ref[0, 0] = sel
    bpt = tk // SEL_BLK
    any_row = jnp.max(sel, axis=1, keepdims=True)
    flag_ref[0, 0] = jnp.max(any_row.reshape(nb // bpt, bpt, 1), axis=1)

    wstart = wstart_fn(i)
    if not isinstance(wstart, int):
        wstart = pl.multiple_of(wstart, 128)
    wpos0 = wpos0_fn(i)
    tile_rel = jnp.asarray((wpos0 - pos0) // 128, F32)
    kw = jnp.concatenate([kw_ref[0, pl.ds(wstart, wk), :], _pos_features(wk, tile_rel)], axis=1)
    dist_w = pos_q - (wpos0 + lax.broadcasted_iota(jnp.int32, (wk, 1), 0))
    p_w = softmax_cols(lax.dot_general(kw, qa, NN, preferred_element_type=F32), (dist_w < 0) | (dist_w >= WINDOW))
    vwin = vwt_ref[0, pl.ds(pl.multiple_of(k * 64, 64), 64), pl.ds(wstart, wk)]
    o_w = _mm(vwin, p_w)

    gb = g_ref[0, 0]
    part_ref[0, 0] = _gate_rows(gb, 0, tq) * o_c + _gate_rows(gb, 2, tq) * o_w


def nsa_select(qt, gates, kvc, kvct, kw, vwt, *, nb, tq, tk, wk, pos0_fn, wstart_fn, wpos0_fn):
    b, _, nq = qt.shape
    nbc = kvc.shape[1]
    nw = kw.shape[1]
    nqt = nq // tq
    nt = nb * SEL_BLK // tk
    w4 = NSA_GROUP * tq
    assert nbc <= 256 and tk <= 512 and wk <= 1024
    body = functools.partial(_nsa_select_body, tq=tq, tk=tk, wk=wk, nbc=nbc, nb=nb, pos0_fn=pos0_fn,
                             wstart_fn=wstart_fn, wpos0_fn=wpos0_fn)
    full = lambda s1, s2: pl.BlockSpec((1, s1, s2), lambda bi, i, k: (bi, 0, 0))
    step = lambda s1, s2: pl.BlockSpec((1, 1, s1, s2), lambda bi, i, k: (bi, i * NSA_KV_HEADS + k, 0, 0))
    return pl.pallas_call(
        body,
        grid=(b, nqt, NSA_KV_HEADS),
        in_specs=[pl.BlockSpec((1, 256, tq), lambda bi, i, k: (bi, k, i)),
                  pl.BlockSpec((1, 1, 16, tq), lambda bi, i, k: (bi, k, 0, i)),
                  full(nbc, 256), full(256, nbc), full(nw, 128), full(128, nw)],
        out_specs=[step(64, w4), step(nb, tq), step(nt, 1)],
        out_shape=[jax.ShapeDtypeStruct((b, nqt * 2, 64, w4), F32),
                   jax.ShapeDtypeStruct((b, nqt * 2, nb, tq), F32),
                   jax.ShapeDtypeStruct((b, nqt * 2, nt, 1), F32)],
        compiler_params=_cparams(("arbitrary", "arbitrary", "arbitrary")),
        name="nsa_select",
    )(qt, gates, kvc, kvct, kw, vwt)


def _nsa_selected_body(list_ref, slot_ref, cnt_ref, qt_ref, g_ref, sel_ref, ks_ref, vst_ref, part_ref, o_ref, *,
                       tq, tk, nt, pos0_fn):
    bi = pl.program_id(0)
    i = pl.program_id(1)
    k = pl.program_id(2)
    step = (bi * pl.num_programs(1) + i) * NSA_KV_HEADS + k
    w4 = NSA_GROUP * tq
    pos0 = pos0_fn(i)
    qa = _nsa_query(qt_ref, k, tq)
    pos_q = pos0 + lax.broadcasted_iota(jnp.int32, (1, w4), 1) % tq
    bpt = tk // SEL_BLK
    row_k = lax.broadcasted_iota(jnp.int32, (tk, 1), 0)
    r = lax.broadcasted_iota(jnp.int32, (tk, LANE), 0)
    lane = lax.broadcasted_iota(jnp.int32, (tk, LANE), 1)
    feat_ab = jnp.where(lane == 0, r // 16, jnp.where(lane == 1, r % 16, 0)).astype(F32)

    def kv_step(jj, carry):
        m_i, l_i, acc = carry
        j = list_ref[step * nt + jj]
        off = pl.multiple_of(j * tk, tk)
        buf = pl.multiple_of(slot_ref[step * nt + jj] * tk, tk)
        tile_rel = ((off - pos0) // 128).astype(F32)
        feat = jnp.where(lane == 2, tile_rel, feat_ab).astype(BF16)
        kj = jnp.concatenate([ks_ref[0, pl.ds(buf, tk), :], feat], axis=1)
        s = lax.dot_general(kj, qa, NN, preferred_element_type=F32)
        selb = (sel_ref[0, 0, pl.ds(pl.multiple_of(j * bpt, bpt), bpt), :] - 1.0) * (-MASKED)
        selb = jnp.concatenate([selb] * NSA_GROUP, axis=1)
        s = s + jnp.broadcast_to(selb[:, None, :], (bpt, SEL_BLK, w4)).reshape(tk, w4)
        s = jnp.where(row_k > pos_q - off, MASKED, s)
        m_new = jnp.maximum(m_i, jnp.max(s, axis=0, keepdims=True))
        p = jnp.exp(s - m_new)
        alpha = jnp.exp(m_i - m_new)
        l_new = l_i * alpha + jnp.sum(p, axis=0, keepdims=True)
        vj = vst_ref[0, pl.ds(pl.multiple_of(k * 64, 64), 64), pl.ds(buf, tk)]
        return m_new, l_new, acc * alpha + _mm(vj, p)

    init = (jnp.full((1, w4), M_INIT, F32), jnp.zeros((1, w4), F32), jnp.zeros((64, w4), F32))
    _, l_s, acc_s = lax.fori_loop(0, cnt_ref[step], kv_step, init)
    o_s = acc_s / jnp.maximum(l_s, 1e-30)
    o_t = part_ref[0, 0] + _gate_rows(g_ref[0, 0], 1, tq) * o_s
    o_ref[0] = jnp.concatenate([o_t[:, g * tq:(g + 1) * tq].T for g in range(NSA_GROUP)], axis=1)


def nsa_selected(tile_list, slot_list, tile_cnt, qt, gates, sel, ks, vst, part, *, tq, tk, pos0_fn):
    b, _, nq = qt.shape
    nk = ks.shape[1]
    nb = sel.shape[2]
    nt = nb * SEL_BLK // tk
    w4 = NSA_GROUP * tq
    full = lambda s1, s2: pl.BlockSpec((1, s1, s2), lambda bi, i, k, *_: (bi, 0, 0))
    step = lambda s1, s2: pl.BlockSpec((1, 1, s1, s2), lambda bi, i, k, *_: (bi, i * NSA_KV_HEADS + k, 0, 0))
    grid_spec = pltpu.PrefetchScalarGridSpec(
        num_scalar_prefetch=3,
        grid=(b, nq // tq, NSA_KV_HEADS),
        in_specs=[pl.BlockSpec((1, 256, tq), lambda bi, i, k, *_: (bi, k, i)),
                  pl.BlockSpec((1, 1, 16, tq), lambda bi, i, k, *_: (bi, k, 0, i)),
                  step(nb, tq), full(nk, 128), full(128, nk), step(64, w4)],
        out_specs=pl.BlockSpec((1, tq, 256), lambda bi, i, k, *_: (bi, i, k)),
    )
    return pl.pallas_call(
        functools.partial(_nsa_selected_body, tq=tq, tk=tk, nt=nt, pos0_fn=pos0_fn),
        grid_spec=grid_spec,
        out_shape=jax.ShapeDtypeStruct((b, nq, 512), F32),
        compiler_params=_cparams(("arbitrary", "arbitrary", "arbitrary")),
        name="nsa_selected",
    )(tile_list, slot_list, tile_cnt, qt, gates, sel, ks, vst, part)


def _active_first(active):
    order = jnp.argsort(jnp.where(active, 0, 1), axis=-1, stable=True).astype(jnp.int32)
    return order, jnp.sum(active, axis=-1).astype(jnp.int32)


def nsa_attention(qt, gates, kvc, kvct, ks, vst, kw, vwt, *, tq, tk, wk, pos0_fn, wstart_fn, wpos0_fn):
    nb = ks.shape[1] // SEL_BLK
    part, sel, flags = nsa_select(qt, gates, kvc, kvct, kw, vwt, nb=nb, tq=tq, tk=tk, wk=wk, pos0_fn=pos0_fn,
                                  wstart_fn=wstart_fn, wpos0_fn=wpos0_fn)
    order, cnt = _active_first(flags[..., 0] > 0.5)
    return nsa_selected(order.reshape(-1), order.reshape(-1), cnt.reshape(-1), qt, gates, sel, ks, vst, part,
                        tq=tq, tk=tk, pos0_fn=pos0_fn)


def nsa_attention_paged(qt, gates, kvc, kvct, pool_t, page_table, tail, kw, vwt, *, tq, tk, wk, pos0_fn, wstart_fn,
                        wpos0_fn):
    nb = (page_table.shape[1] * pool_t.shape[2] + tk) // SEL_BLK
    part, sel, flags = nsa_select(qt, gates, kvc, kvct, kw, vwt, nb=nb, tq=tq, tk=tk, wk=wk, pos0_fn=pos0_fn,
                                  wstart_fn=wstart_fn, wpos0_fn=wpos0_fn)
    active = flags[..., 0] > 0.5
    tiles_b, cnt_b = _active_first(jnp.any(active, axis=1))
    slot_of_tile = jnp.argsort(tiles_b, axis=-1).astype(jnp.int32)
    order, cnt = _active_first(active)
    slots = jnp.take_along_axis(jnp.broadcast_to(slot_of_tile[:, None, :], order.shape), order, axis=-1)

    def run(n_slots):
        ks, vst = gather_sel(pool_t, page_table, tail, tk, tiles_b.reshape(-1), cnt_b, n_slots)
        return nsa_selected(order.reshape(-1), slots.reshape(-1), cnt.reshape(-1), qt, gates, sel, ks, vst, part,
                            tq=tq, tk=tk, pos0_fn=pos0_fn)

    nt = tiles_b.shape[-1]
    few = min(SAMPLE_TILE_SLOTS, nt)
    return lax.cond(jnp.max(cnt_b) <= few, lambda: run(few), lambda: run(nt))


def _tri_inverse(ms, c):
    eye = (lax.broadcasted_iota(jnp.int32, (c, c), 0) == lax.broadcasted_iota(jnp.int32, (c, c), 1)).astype(F32)
    ps = [-m for m in ms]
    ts = [eye + p for p in ps]
    steps = max(int(math.ceil(math.log2(c))) - 1, 0)
    d = lambda x, y: lax.dot_general(x, y, NN, preferred_element_type=F32)
    for _ in range(steps):
        sp = [_split(p) for p in ps]
        ps = [d(ph, ph) + (d(ph, pl_) + d(pl_, ph)) for ph, pl_ in sp]
        sp = [_split(p) for p in ps]
        st = [_split(t) for t in ts]
        ts = [t + (d(th, ph) + (d(th, pl_) + d(tl, ph))) for t, (th, tl), (ph, pl_) in zip(ts, st, sp)]
    return ts


def _rwkv_body(rw_ref, rw0_ref, s0_ref, mu_ref, vec_ref, w2_ref, a2_ref, g2_ref, seg_ref, rk_ref,
               o_ref, sfin_ref, buf_ref, s_ref, y_ref, *, c, valid, n_chunks):
    ci = pl.program_id(1)
    halo = 8

    @pl.when(ci == 0)
    def _():
        buf_ref[0:halo, :] = rw0_ref[0]
        s_ref[...] = s0_ref[0]

    cur = rw_ref[...]
    buf_ref[halo:halo + c, :] = cur
    prev = buf_ref[halo - 1:halo - 1 + c, :]
    xr = cur + (prev - cur) * mu_ref[...]
    buf_ref[0:halo, :] = cur[c - halo:, :]

    vec = vec_ref[...]
    w0, a0, kkw, kaw, ln_w, ln_b = (vec[r:r + 1, :] for r in range(6))
    r = xr[:, 0:512]
    kx = xr[:, 512:1024]
    v = xr[:, 1024:1536]
    xw = xr[:, 1536:1664]
    xa = xr[:, 1664:1792]
    xg = xr[:, 1792:1920]
    wl = -jnp.exp(-_softplus(-(w0 + _mm(jnp.tanh(xw), w2_ref[...]))) - 0.5)
    a = _sigmoid(a0 + _mm(xa, a2_ref[...]))
    gate = _mm(_sigmoid(xg), g2_ref[...])
    seg = seg_ref[...]
    zk = kx * kkw
    k2 = kx * (1.0 + (a - 1.0) * kaw)
    zz_sum, rk_sum = _head_sums([zk * zk, r * k2 * rk_ref[...]], seg)
    kk = zk * lax.rsqrt(zz_sum + EPS)
    bonus = rk_sum * v
    if valid < c:
        live = lax.broadcasted_iota(jnp.int32, (c, 1), 0) < valid
        wl = jnp.where(live, wl, 0.0)
        kk = jnp.where(live, kk, 0.0)
        k2 = jnp.where(live, k2, 0.0)
        v = jnp.where(live, v, 0.0)
        r = jnp.where(live, r, 0.0)
    bb = kk * a

    ri = lax.broadcasted_iota(jnp.int32, (c, c), 0)
    cj = lax.broadcasted_iota(jnp.int32, (c, c), 1)
    tril = ri >= cj
    strict = ri > cj
    cw = _mm01(tril, wl)
    ecw = jnp.exp(cw)
    einv = jnp.exp(-cw)
    p_c = ecw[c - 1:c, :]
    kt = kk * jnp.exp(cw - wl)
    bt = bb * einv
    ki = k2 * einv
    rt = r * ecw
    bd = bt * p_c
    kd = ki * p_c

    heads = range(RWKV_HEADS)
    sls = [slice(h * RWKV_HD, (h + 1) * RWKV_HD) for h in heads]
    kt_h = [kt[:, sl] for sl in sls]
    bt_h = [bt[:, sl] for sl in sls]
    ki_h = [ki[:, sl] for sl in sls]
    rt_h = [rt[:, sl] for sl in sls]
    v_h = [v[:, sl] for sl in sls]
    l_m = [jnp.where(strict, _mm3(kt_h[h], bt_h[h], NT), 0.0) for h in heads]
    m_kk = [jnp.where(strict, _mm(kt_h[h], ki_h[h], NT), 0.0) for h in heads]
    a_rb = [jnp.where(tril, _mm(rt_h[h], bt_h[h], NT), 0.0) for h in heads]
    a_rk = [jnp.where(tril, _mm(rt_h[h], ki_h[h], NT), 0.0) for h in heads]
    mv = [_mm(m_kk[h], v_h[h]) for h in heads]
    y0 = [_mm(a_rk[h], v_h[h]) for h in heads]
    t_inv = _tri_inverse(l_m, c)
    w_h = [_mm3(t_inv[h], kt_h[h]) for h in heads]
    u_h = [-_mm3(t_inv[h], mv[h]) for h in heads]
    s_h = [s_ref[h] for h in heads]
    e_h = [u_h[h] - _mm(w_h[h], s_h[h], NT) for h in heads]
    y1 = [_mm(rt_h[h], s_h[h], NT) + y0[h] for h in heads]
    y_h = [y1[h] + _mm(a_rb[h], e_h[h]) for h in heads]
    ds = [_mm(e_h[h], bd[:, sls[h]], TN) + _mm(v_h[h], kd[:, sls[h]], TN) for h in heads]
    for h in heads:
        s_ref[h] = s_h[h] * p_c[:, sls[h]] + ds[h]
        mu_h = jnp.mean(y_h[h], axis=-1, keepdims=True)
        d_h = y_h[h] - mu_h
        var_h = jnp.mean(d_h * d_h, axis=-1, keepdims=True)
        y_ref[:, sls[h]] = d_h * lax.rsqrt(var_h + RWKV_GN_EPS)

    o_ref[...] = (y_ref[...] * ln_w + ln_b + bonus) * gate

    @pl.when(ci == n_chunks - 1)
    def _():
        sfin_ref[0] = s_ref[...]


def rwkv_mix(rw, rw0, s0, mu, vec, w2, a2, g2, seg, rk, *, c, valid):
    b = s0.shape[0]
    rows = rw.shape[0]
    n_chunks = rows // (b * c)
    const = lambda s: pl.BlockSpec(s, lambda bi, ci: tuple(0 for _ in s))
    return pl.pallas_call(
        functools.partial(_rwkv_body, c=c, valid=valid, n_chunks=n_chunks),
        grid=(b, n_chunks),
        in_specs=[pl.BlockSpec((c, RW_COLS), lambda bi, ci: (bi * n_chunks + ci, 0)),
                  pl.BlockSpec((1, 8, RW_COLS), lambda bi, ci: (bi, 0, 0)),
                  pl.BlockSpec((1, RWKV_HEADS, 64, 64), lambda bi, ci: (bi, 0, 0, 0)),
                  const((1, RW_COLS)), const((8, 512)), const((128, 512)), const((128, 512)), const((128, 512)),
                  const((RWKV_W // 2, RWKV_W // 2)), const((1, 512))],
        out_specs=[pl.BlockSpec((c, 512), lambda bi, ci: (bi * n_chunks + ci, 0)),
                   pl.BlockSpec((1, RWKV_HEADS, 64, 64), lambda bi, ci: (bi, 0, 0, 0))],
        out_shape=[jax.ShapeDtypeStruct((rows, 512), F32),
                   jax.ShapeDtypeStruct((b, RWKV_HEADS, 64, 64), F32)],
        scratch_shapes=[pltpu.VMEM((8 + c, RW_COLS), F32), pltpu.VMEM((RWKV_HEADS, 64, 64), F32),
                        pltpu.VMEM((c, 512), F32)],
        compiler_params=_cparams(("arbitrary", "arbitrary")),
        name="rwkv_mix",
    )(rw, rw0, s0, mu, vec, w2, a2, g2, seg, rk)


def _out_proj_body(*refs, n_in):
    a_refs = refs[:n_in]
    w_refs = refs[n_in:2 * n_in]
    x_ref, g_ref, o_ref = refs[2 * n_in:]
    y = _mm(a_refs[0][...], w_refs[0][...])
    for a_ref, w_ref in zip(a_refs[1:], w_refs[1:]):
        y = y + _mm(a_ref[...], w_ref[...])
    o_ref[...] = x_ref[...] + g_ref[...] * y


def out_proj(acts, weights, x, gate, tm):
    n, d = x.shape
    n_in = len(acts)
    return pl.pallas_call(
        functools.partial(_out_proj_body, n_in=n_in),
        grid=(n // tm,),
        in_specs=[pl.BlockSpec((tm, a.shape[1]), lambda i: (i, 0)) for a in acts]
        + [pl.BlockSpec(w.shape, lambda i: (0, 0)) for w in weights]
        + [pl.BlockSpec((tm, d), lambda i: (i, 0)), _mod_spec(gate.shape[0], tm, d)],
        out_specs=pl.BlockSpec((tm, d), lambda i: (i, 0)),
        out_shape=jax.ShapeDtypeStruct((n, d), F32),
        compiler_params=_cparams(("arbitrary",)),
        name="out_proj",
    )(*acts, *weights, x, gate)


def _odd_proj_body(x_ref, nw_ref, sc_ref, sh_ref, w_ref, qkv_ref, z_ref, ba_ref):
    hb = _norm_mod(x_ref[...], nw_ref[...], sc_ref[...], sh_ref[...]).astype(BF16)
    qkv_ref[...] = _mm(hb, w_ref[:, 0:3072])
    z_ref[...] = _mm(hb, w_ref[:, 3072:4096])
    ba_ref[...] = _mm(hb, w_ref[:, 4096:O_COLS])


def odd_proj(x, nw, sc, sh, w_packed, tm):
    n, d = x.shape
    rows_mod = sc.shape[0]
    row = lambda c: pl.BlockSpec((tm, c), lambda i: (i, 0))
    return pl.pallas_call(
        _odd_proj_body,
        grid=(n // tm,),
        in_specs=[row(d), pl.BlockSpec((1, d), lambda i: (0, 0)),
                  _mod_spec(rows_mod, tm, d), _mod_spec(rows_mod, tm, d),
                  pl.BlockSpec((d, O_COLS), lambda i: (0, 0))],
        out_specs=[row(3072), row(1024), row(128)],
        out_shape=[jax.ShapeDtypeStruct((n, 3072), F32), jax.ShapeDtypeStruct((n, 1024), F32),
                   jax.ShapeDtypeStruct((n, 128), F32)],
        compiler_params=_cparams(("arbitrary",)),
        name="odd_proj",
    )(x, nw, sc, sh, w_packed)


def _gdn_body(qkv_ref, z_ref, ba_ref, cs_ref, s0_ref, cw_ref, hp_ref, nw_ref,
              o_ref, sfin_ref, buf_ref, s_ref, *, c, valid, n_chunks):
    ci = pl.program_id(1)
    halo = 8

    @pl.when(ci == 0)
    def _():
        buf_ref[0:halo, :] = cs_ref[0]
        s_ref[...] = s0_ref[0]

    x = qkv_ref[...]
    buf_ref[halo:halo + c, :] = x
    cw = cw_ref[...]
    conv = buf_ref[halo - 3:halo - 3 + c, :] * cw[0:1, :]
    for j in range(1, CONV_W):
        conv = conv + buf_ref[halo - 3 + j:halo - 3 + j + c, :] * cw[j:j + 1, :]
    buf_ref[0:halo, :] = x[c - halo:, :]
    conv = _silu(conv)

    hp = hp_ref[...]
    ba = ba_ref[...]
    beta_f = _sigmoid(ba)
    g_f = hp[0:1, :] * _softplus(ba + hp[1:2, :])
    if valid < c:
        live = lax.broadcasted_iota(jnp.int32, (c, 1), 0) < valid
        beta_f = jnp.where(live, beta_f, 0.0)
        g_f = jnp.where(live, g_f, 0.0)
        conv = jnp.where(live, conv, 0.0)

    ri = lax.broadcasted_iota(jnp.int32, (c, c), 0)
    cj = lax.broadcasted_iota(jnp.int32, (c, c), 1)
    tril = ri >= cj
    strict = ri > cj
    gc = _mm01(tril, g_f)
    gct = gc.T
    z = z_ref[...]
    nw = nw_ref[...]

    heads = range(GDN_HEADS)
    sls = [slice(h * GDN_HD, (h + 1) * GDN_HD) for h in heads]
    q_h = [conv[:, sl] for sl in sls]
    k_h = [conv[:, GDN_W + h * GDN_HD:GDN_W + (h + 1) * GDN_HD] for h in heads]
    v_h = [conv[:, 2 * GDN_W + h * GDN_HD:2 * GDN_W + (h + 1) * GDN_HD] for h in heads]
    q_h = [q * lax.rsqrt(jnp.sum(q * q, axis=-1, keepdims=True) + EPS) * (GDN_HD ** -0.5) for q in q_h]
    k_h = [k * lax.rsqrt(jnp.sum(k * k, axis=-1, keepdims=True) + EPS) for k in k_h]
    g_col = [gc[:, 8 + h:9 + h] for h in heads]
    eg = [jnp.exp(g) for g in g_col]
    b_col = [beta_f[:, h:h + 1] for h in heads]
    decay = [jnp.where(tril, jnp.exp(jnp.where(tril, g_col[h] - gct[8 + h:9 + h, :], 0.0)), 0.0) for h in heads]
    kb = [k_h[h] * b_col[h] for h in heads]
    vb = [v_h[h] * b_col[h] for h in heads]
    m_h = [jnp.where(strict, _mm3(kb[h], k_h[h], NT) * decay[h], 0.0) for h in heads]
    qk = [jnp.where(tril, _mm(q_h[h], k_h[h], NT) * decay[h], 0.0) for h in heads]
    t_inv = _tri_inverse(m_h, c)
    u_h = [_mm(t_inv[h], vb[h]) for h in heads]
    w_h = [_mm(t_inv[h], kb[h] * eg[h]) for h in heads]
    s_h = [s_ref[h] for h in heads]
    v_new = [u_h[h] - _mm(w_h[h], s_h[h]) for h in heads]
    o1 = [_mm(q_h[h] * eg[h], s_h[h]) for h in heads]
    o_h = [o1[h] + _mm(qk[h], v_new[h]) for h in heads]
    g_last = [g[c - 1:c, :] for g in g_col]
    ds = [_mm(k_h[h] * jnp.exp(g_last[h] - g_col[h]), v_new[h], TN) for h in heads]
    for h in heads:
        s_ref[h] = s_h[h] * jnp.exp(g_last[h]) + ds[h]
        o_n = o_h[h] * lax.rsqrt(jnp.mean(o_h[h] * o_h[h], axis=-1, keepdims=True) + EPS) * nw
        o_ref[:, sls[h]] = o_n * _silu(z[:, sls[h]])

    @pl.when(ci == n_chunks - 1)
    def _():
        sfin_ref[0] = s_ref[...]


def gdn_mix(qkv, z, ba, cs, s0, conv_w8, hp, nw, *, c, valid):
    b = s0.shape[0]
    rows = qkv.shape[0]
    n_chunks = rows // (b * c)
    const = lambda s: pl.BlockSpec(s, lambda bi, ci: tuple(0 for _ in s))
    row = lambda w: pl.BlockSpec((c, w), lambda bi, ci: (bi * n_chunks + ci, 0))
    return pl.pallas_call(
        functools.partial(_gdn_body, c=c, valid=valid, n_chunks=n_chunks),
        grid=(b, n_chunks),
        in_specs=[row(3072), row(1024), row(128),
                  pl.BlockSpec((1, 8, 3072), lambda bi, ci: (bi, 0, 0)),
                  pl.BlockSpec((1, GDN_HEADS, 128, 128), lambda bi, ci: (bi, 0, 0, 0)),
                  const((8, 3072)), const((8, 128)), const((1, 128))],
        out_specs=[row(1024), pl.BlockSpec((1, GDN_HEADS, 128, 128), lambda bi, ci: (bi, 0, 0, 0))],
        out_shape=[jax.ShapeDtypeStruct((rows, 1024), F32),
                   jax.ShapeDtypeStruct((b, GDN_HEADS, 128, 128), F32)],
        scratch_shapes=[pltpu.VMEM((8 + c, 3072), F32), pltpu.VMEM((GDN_HEADS, 128, 128), F32)],
        compiler_params=_cparams(("arbitrary", "arbitrary")),
        name="gdn_mix",
    )(qkv, z, ba, cs, s0, conv_w8, hp, nw)


def _router_body(x_ref, nw_ref, sc_ref, sh_ref, wr_ref, br_ref, h_ref, gate_ref):
    h = _norm_mod(x_ref[...], nw_ref[...], sc_ref[...], sh_ref[...])
    h_ref[...] = h.astype(BF16)
    logits = _mmh(h, wr_ref[...]) + br_ref[...]
    tm = logits.shape[0]
    lane = lax.broadcasted_iota(jnp.int32, (tm, LANE), 1)
    is_grp = (lane >= N_EXPERTS) & (lane < N_EXPERTS + N_GROUPS)
    gl = jnp.where(is_grp, logits, NEG)
    gmax = jnp.max(gl, axis=-1, keepdims=True)
    g_idx = jnp.min(jnp.where(gl == gmax, lane, 4 * LANE), axis=-1, keepdims=True) - N_EXPERTS
    g_w = 1.0 / jnp.sum(jnp.where(is_grp, jnp.exp(gl - gmax), 0.0), axis=-1, keepdims=True)
    in_grp = (lane < N_EXPERTS) & (lane // EXP_PER_GROUP == g_idx)
    el = jnp.where(in_grp, logits, NEG)
    emax = jnp.max(el, axis=-1, keepdims=True)
    e = jnp.where(in_grp, jnp.exp(el - emax), 0.0)
    p = e / jnp.sum(e, axis=-1, keepdims=True)
    p1 = jnp.where(in_grp, p, -1.0)
    m1 = jnp.max(p1, axis=-1, keepdims=True)
    i1 = jnp.min(jnp.where(p1 == m1, lane, 4 * LANE), axis=-1, keepdims=True)
    p2 = jnp.where(lane == i1, -1.0, p1)
    m2 = jnp.max(p2, axis=-1, keepdims=True)
    i2 = jnp.min(jnp.where(p2 == m2, lane, 4 * LANE), axis=-1, keepdims=True)
    tot = m1 + m2
    gate = jnp.where(lane == i1, m1 / tot * g_w, jnp.where(lane == i2, m2 / tot * g_w, 0.0))
    gate_ref[...] = jnp.where(lane == GRP_LANE, g_idx.astype(F32), gate)


def moe_router(x, nw, sc, sh, w_r, b_r, tm):
    n, d = x.shape
    rows_mod = sc.shape[0]
    return pl.pallas_call(
        _router_body,
        grid=(n // tm,),
        in_specs=[pl.BlockSpec((tm, d), lambda i: (i, 0)), pl.BlockSpec((1, d), lambda i: (0, 0)),
                  _mod_spec(rows_mod, tm, d), _mod_spec(rows_mod, tm, d),
                  pl.BlockSpec((d, LANE), lambda i: (0, 0)), pl.BlockSpec((1, LANE), lambda i: (0, 0))],
        out_specs=[pl.BlockSpec((tm, d), lambda i: (i, 0)), pl.BlockSpec((tm, LANE), lambda i: (i, 0))],
        out_shape=[jax.ShapeDtypeStruct((n, d), BF16), jax.ShapeDtypeStruct((n, LANE), F32)],
        compiler_params=_cparams(("arbitrary",)),
        name="moe_router",
    )(x, nw, sc, sh, w_r, b_r)


def _moe_body(h_ref, gate_ref, w1_ref, w3_ref, w2_ref, x_ref, g2_ref, o_ref, acc_ref):
    e = pl.program_id(1)

    @pl.when(e == 0)
    def _():
        acc_ref[...] = jnp.zeros_like(acc_ref)

    hb = h_ref[...]
    he = _silu(_mm(hb, w1_ref[0])) * _mm(hb, w3_ref[0])
    y = _mm(he, w2_ref[0])
    gate = gate_ref[...]
    lane = lax.broadcasted_iota(jnp.int32, gate.shape, 1)
    ge = jnp.sum(jnp.where(lane == e, gate, 0.0), axis=-1, keepdims=True)
    acc_ref[...] += ge * y

    @pl.when(e == pl.num_programs(1) - 1)
    def _():
        o_ref[...] = x_ref[...] + g2_ref[...] * acc_ref[...]


def moe_ffn(h, gate, w1, w3, w2, e0, x, g2, tm):
    n, d = x.shape
    de = w1.shape[2]
    return pl.pallas_call(
        _moe_body,
        grid=(n // tm, N_EXPERTS),
        in_specs=[pl.BlockSpec((tm, d), lambda i, e: (i, 0)), pl.BlockSpec((tm, LANE), lambda i, e: (i, 0)),
                  pl.BlockSpec((1, d, de), lambda i, e: (e0 + e, 0, 0)),
                  pl.BlockSpec((1, d, de), lambda i, e: (e0 + e, 0, 0)),
                  pl.BlockSpec((1, de, d), lambda i, e: (e0 + e, 0, 0)),
                  pl.BlockSpec((tm, d), lambda i, e: (i, 0)),
                  pl.BlockSpec((1, d), lambda i, e: (0, 0)) if g2.shape[0] == 1
                  else pl.BlockSpec((tm, d), lambda i, e: (i, 0))],
        out_specs=pl.BlockSpec((tm, d), lambda i, e: (i, 0)),
        out_shape=jax.ShapeDtypeStruct((n, d), F32),
        scratch_shapes=[pltpu.VMEM((tm, d), F32)],
        compiler_params=_cparams(("arbitrary", "arbitrary")),
        name="moe_ffn",
    )(h, gate, w1, w3, w2, x, g2)


def _moe_plan(grp, tm, tb, cap, max_entries):
    nt = grp.shape[0] // tm
    cnt = jax.nn.one_hot(grp, N_GROUPS, dtype=jnp.int32).reshape(nt, tm, N_GROUPS).sum(axis=1)
    pc = (cnt + ROW_ALIGN - 1) // ROW_ALIGN * ROW_ALIGN
    segb = jnp.cumsum(pc, axis=1) - pc
    off = jnp.cumsum(pc, axis=0) - pc
    tot = pc.sum(axis=0)
    n_real = (tot + tb - 1) // tb
    n_all = jnp.minimum((tot + tm + tb - 1) // tb, cap // tb)
    ends = jnp.cumsum(n_all)
    s = jnp.arange(max_entries)
    g_of = jnp.sum(s[:, None] >= ends[None, :], axis=1)
    active = g_of < N_GROUPS
    g_c = jnp.minimum(g_of, N_GROUPS - 1)
    rt = s - (ends - n_all)[g_c]
    kind = jnp.where(active, jnp.where(rt < n_real[g_c], 1, 2), 0)
    last = ends[-1] - 1
    e_grp = jnp.where(active, g_c, g_c[last])
    e_rt = jnp.where(active, rt, rt[last])
    i32 = lambda a: a.reshape(-1).astype(jnp.int32)
    return i32(segb), i32(off // ROW_ALIGN), i32(e_grp), i32(e_rt), i32(kind)


def _group_perm(gate, segb_ref, base, tm, rows):
    gt = gate.T
    grp = gt[GRP_LANE:GRP_LANE + 1, :]
    gi = lax.broadcasted_iota(jnp.int32, (8, tm), 0).astype(F32)
    oh = jnp.where(gi == grp, 1.0, 0.0)
    r_i = lax.broadcasted_iota(jnp.int32, (tm, tm), 0)
    c_i = lax.broadcasted_iota(jnp.int32, (tm, tm), 1)
    before = jnp.where(r_i < c_i, 1.0, 0.0).astype(BF16)
    rank = lax.dot_general(oh.astype(BF16), before, NN, preferred_element_type=F32)
    dest = jnp.zeros((1, tm), F32)
    for g in range(N_GROUPS):
        dest = dest + oh[g:g + 1] * (segb_ref[base + g].astype(F32) + rank[g:g + 1])
    rows_i = lax.broadcasted_iota(jnp.int32, (rows, tm), 0).astype(F32)
    return jnp.where(rows_i == dest, 1.0, 0.0).astype(BF16)


def _moe_dispatch_body(segb_ref, off_ref, h_ref, gate_ref, xg_in, gg_in, xg_ref, gg_ref, xs_ref, gs_ref, *, tm, rows):
    del off_ref, xg_in, gg_in
    i = pl.program_id(0)
    g = pl.program_id(1)

    @pl.when((i == 0) & (g == 0))
    def _():
        xs_ref[...] = jnp.zeros_like(xs_ref)
        gs_ref[...] = jnp.zeros_like(gs_ref)

    @pl.when(g == 0)
    def _():
        gate = gate_ref[...]
        p = _group_perm(gate, segb_ref, i * N_GROUPS, tm, rows)
        xs_ref[0:rows, :] = lax.dot_general(p, h_ref[...], NN, preferred_element_type=F32).astype(BF16)
        gs_ref[0:rows, :] = _mm01(p, gate)

    start = pl.multiple_of(segb_ref[i * N_GROUPS + g], ROW_ALIGN)
    xg_ref[...] = xs_ref[pl.ds(start, tm), :]
    gg_ref[...] = gs_ref[pl.ds(start, tm), :]


def moe_dispatch(h, gate, segb, off, tm, cap):
    n, d = h.shape
    rows = tm + N_GROUPS * ROW_ALIGN
    win = lambda w: pl.BlockSpec((pl.Element(tm), pl.Element(w)),
                                 lambda i, g, segb, off: ((g * (cap // ROW_ALIGN) + off[i * N_GROUPS + g]) * ROW_ALIGN, 0))
    grid_spec = pltpu.PrefetchScalarGridSpec(
        num_scalar_prefetch=2,
        grid=(n // tm, N_GROUPS),
        in_specs=[pl.BlockSpec((tm, d), lambda i, g, *_: (i, 0)), pl.BlockSpec((tm, LANE), lambda i, g, *_: (i, 0)),
                  pl.BlockSpec(memory_space=pl.ANY), pl.BlockSpec(memory_space=pl.ANY)],
        out_specs=[win(d), win(LANE)],
        scratch_shapes=[pltpu.VMEM((2 * tm + N_GROUPS * ROW_ALIGN, d), BF16),
                        pltpu.VMEM((2 * tm + N_GROUPS * ROW_ALIGN, LANE), F32)],
    )
    return pl.pallas_call(
        functools.partial(_moe_dispatch_body, tm=tm, rows=rows),
        grid_spec=grid_spec,
        out_shape=[jax.ShapeDtypeStruct((N_GROUPS * cap, d), BF16), jax.ShapeDtypeStruct((N_GROUPS * cap, LANE), F32)],
        input_output_aliases={4: 0, 5: 1},
        compiler_params=_cparams(("arbitrary", "arbitrary")),
        name="moe_dispatch",
    )(segb, off, h, gate, jnp.zeros((N_GROUPS * cap, d), BF16), jnp.zeros((N_GROUPS * cap, LANE), F32))


def _moe_group_body(grp_ref, rt_ref, kind_ref, xg_ref, gg_ref, w1_ref, w3_ref, w2_ref, yg_ref, acc_ref):
    del rt_ref
    s = pl.program_id(0)
    e = pl.program_id(1)
    kind = kind_ref[s]
    last = e == pl.num_programs(1) - 1

    @pl.when(kind == 1)
    def _():
        @pl.when(e == 0)
        def _():
            acc_ref[...] = jnp.zeros_like(acc_ref)

        xb = xg_ref[...]
        he = _silu(_mm(xb, w1_ref[0])) * _mm(xb, w3_ref[0])
        y = _mm(he, w2_ref[0])
        gate = gg_ref[...]
        lane = lax.broadcasted_iota(jnp.int32, gate.shape, 1)
        ge = jnp.sum(jnp.where(lane == grp_ref[s] * EXP_PER_GROUP + e, gate, 0.0), axis=-1, keepdims=True)
        acc_ref[...] += ge * y

        @pl.when(last)
        def _():
            yg_ref[...] = acc_ref[...]

    @pl.when((kind == 2) & last)
    def _():
        yg_ref[...] = jnp.zeros_like(yg_ref)


def moe_group_ffn(e_grp, e_rt, e_kind, xg, gg, w1, w3, w2, e0, tb, cap):
    d = xg.shape[1]
    de = w1.shape[2]
    row = lambda s, e, grp, rt, kind: (grp[s] * (cap // tb) + rt[s], 0)
    wsel = lambda s, e, grp, rt, kind: (e0 + grp[s] * EXP_PER_GROUP + jnp.where(kind[s] == 1, e, EXP_PER_GROUP - 1),
                                        0, 0)
    grid_spec = pltpu.PrefetchScalarGridSpec(
        num_scalar_prefetch=3,
        grid=(e_grp.shape[0], EXP_PER_GROUP),
        in_specs=[pl.BlockSpec((tb, d), row), pl.BlockSpec((tb, LANE), row),
                  pl.BlockSpec((1, d, de), wsel), pl.BlockSpec((1, d, de), wsel), pl.BlockSpec((1, de, d), wsel)],
        out_specs=pl.BlockSpec((tb, d), row),
        scratch_shapes=[pltpu.VMEM((tb, d), F32)],
    )
    return pl.pallas_call(
        _moe_group_body,
        grid_spec=grid_spec,
        out_shape=jax.ShapeDtypeStruct((N_GROUPS * cap, d), F32),
        compiler_params=_cparams(("arbitrary", "arbitrary")),
        name="moe_group_ffn",
    )(e_grp, e_rt, e_kind, xg, gg, w1, w3, w2)


def _moe_combine_body(segb_ref, off_ref, yg_ref, gate_ref, x_ref, g2_ref, o_ref, ys_ref, *, tm, rows):
    del off_ref
    i = pl.program_id(0)
    g = pl.program_id(1)

    @pl.when((i == 0) & (g == 0))
    def _():
        ys_ref[...] = jnp.zeros_like(ys_ref)

    start = pl.multiple_of(segb_ref[i * N_GROUPS + g], ROW_ALIGN)
    ys_ref[pl.ds(start, tm), :] = yg_ref[...]

    @pl.when(g == N_GROUPS - 1)
    def _():
        p = _group_perm(gate_ref[...], segb_ref, i * N_GROUPS, tm, rows)
        yh, yl = _split(ys_ref[0:rows, :])
        y = (lax.dot_general(p, yh, TN, preferred_element_type=F32)
             + lax.dot_general(p, yl, TN, preferred_element_type=F32))
        o_ref[...] = x_ref[...] + g2_ref[...] * y


def moe_combine(yg, gate, x, g2, segb, off, tm, cap):
    n, d = x.shape
    rows = tm + N_GROUPS * ROW_ALIGN
    grid_spec = pltpu.PrefetchScalarGridSpec(
        num_scalar_prefetch=2,
        grid=(n // tm, N_GROUPS),
        in_specs=[pl.BlockSpec((pl.Element(tm), pl.Element(d)),
                               lambda i, g, segb, off: ((g * (cap // ROW_ALIGN) + off[i * N_GROUPS + g]) * ROW_ALIGN, 0)),
                  pl.BlockSpec((tm, LANE), lambda i, g, *_: (i, 0)),
                  pl.BlockSpec((tm, d), lambda i, g, *_: (i, 0)),
                  pl.BlockSpec((1, d), lambda i, g, *_: (0, 0))],
        out_specs=pl.BlockSpec((tm, d), lambda i, g, *_: (i, 0)),
        scratch_shapes=[pltpu.VMEM((2 * tm + N_GROUPS * ROW_ALIGN, d), F32)],
    )
    return pl.pallas_call(
        functools.partial(_moe_combine_body, tm=tm, rows=rows),
        grid_spec=grid_spec,
        out_shape=jax.ShapeDtypeStruct((n, d), F32),
        compiler_params=_cparams(("arbitrary", "arbitrary")),
        name="moe_combine",
    )(segb, off, yg, gate, x, g2)


def moe_grouped(h, gate, w1, w3, w2, e0, x, g2, tm):
    n = h.shape[0]
    tb = tm
    cap = n + 2 * tm
    max_entries = (n + (n // tm) * N_GROUPS * (ROW_ALIGN - 1) + N_GROUPS * tm) // tb + N_GROUPS + 1
    grp = gate[:, GRP_LANE].astype(jnp.int32)
    segb, off, e_grp, e_rt, e_kind = _moe_plan(grp, tm, tb, cap, max_entries)
    xg, gg = moe_dispatch(h, gate, segb, off, tm, cap)
    yg = moe_group_ffn(e_grp, e_rt, e_kind, xg, gg, w1, w3, w2, e0, tb, cap)
    return moe_combine(yg, gate, x, g2, segb, off, tm, cap)


def _final_norm_body(x_ref, w_ref, o_ref):
    x = x_ref[...]
    o_ref[...] = x * lax.rsqrt(jnp.mean(x * x, axis=-1, keepdims=True) + EPS) * w_ref[...]


def final_norm(x, w, tm):
    n, d = x.shape
    return pl.pallas_call(
        _final_norm_body,
        grid=(n // tm,),
        in_specs=[pl.BlockSpec((tm, d), lambda i: (i, 0)), pl.BlockSpec((1, d), lambda i: (0, 0))],
        out_specs=pl.BlockSpec((tm, d), lambda i: (i, 0)),
        out_shape=jax.ShapeDtypeStruct((n, d), F32),
        compiler_params=_cparams(("arbitrary",)),
        name="final_norm",
    )(x, w)


def _row_tile(n, pref):
    t = min(pref, n)
    while n % t:
        t //= 2
    return t


def kernel(x_prompt, x_sample, c_prompt, c_sample, cache_nsa_cmp, cache_nsa_sel, page_table, state_nsa_win, state_rwkv, state_rwkv_shift, state_gdn, state_gdn_conv, norm_mix, norm_ffn, norm_final, w_ada, b_ada, even_w_in, even_w_out, nsa_cmp_pos, nsa_cmp_w, rwkv_mu, rwkv_w0, rwkv_w2, rwkv_a0, rwkv_a2, rwkv_g2, rwkv_kk, rwkv_ka, rwkv_rk, rwkv_ln_w, rwkv_ln_b, odd_w_in, odd_w_out, gdn_conv_w, gdn_a_log, gdn_dt_bias, gdn_norm_w, moe_w_grp, moe_b_grp, moe_w_exp, moe_b_exp, moe_w1, moe_w3, moe_w2):
    bp, t, d = x_prompt.shape
    bs, ts, _ = x_sample.shape
    assert bp == 1 and ts <= SPAD and ts < CMP_BLK
    depth = norm_mix.shape[0]
    n_pages, page = page_table.shape[1], cache_nsa_cmp.shape[2]
    past = n_pages * page
    wb = state_nsa_win.shape[2]
    ns = bs * SPAD
    tq, tq_s, tk = 128, 32, 512
    tm_p = _row_tile(t, 512)
    tm_s = ns

    rows_c = -(-(1 + bs) // 8) * 8
    c_all = jnp.concatenate([c_prompt, c_sample, jnp.zeros((rows_c - 1 - bs, d), F32)], axis=0)
    ada = adaln(c_all, w_ada, b_ada)

    def mods(i):
        mp = [ada[i, 0:1, j * d:(j + 1) * d] for j in range(6)]
        ms = [jnp.repeat(ada[i, 1:1 + bs, j * d:(j + 1) * d], SPAD, axis=0) for j in range(6)]
        return mp, ms

    xp = x_prompt[0]
    xs = jnp.pad(x_sample, ((0, 0), (0, SPAD - ts), (0, 0))).reshape(ns, d)

    def unpad(a):
        return a.reshape(bs, SPAD, -1)[:, :ts]

    w1_all, w3_all, w2_all = (w.reshape((-1,) + w.shape[2:]) for w in (moe_w1, moe_w3, moe_w2))
    outs = {k: [] for k in ("cmp_p", "cmp_s", "sel_p", "sel_s", "win_p", "win_s", "rw_p", "rw_s", "sh_p", "sh_s",
                            "gd_p", "gd_s", "cv_p", "cv_s")}

    for i in range(depth):
        (sh1p, sc1p, gt1p, sh2p, sc2p, gt2p), (sh1s, sc1s, gt1s, sh2s, sc2s, gt2s) = mods(i)
        j = i // 2
        nw = norm_mix[i][None, :]
        if i % 2 == 0:
            w_packed = _pack_even_w(even_w_in[j])
            mu = _pack_rw_vec(rwkv_mu[j])
            wts, wc = _cmp_weights(nsa_cmp_pos[j], nsa_cmp_w[j])
            vec = jnp.stack([rwkv_w0[j], rwkv_a0[j], rwkv_kk[j], rwkv_ka[j], rwkv_ln_w[j], rwkv_ln_b[j],
                             jnp.zeros_like(rwkv_w0[j]), jnp.zeros_like(rwkv_w0[j])])
            pad_lora = lambda w: jnp.concatenate([w, jnp.zeros((128 - w.shape[0], w.shape[1]), w.dtype)], axis=0)
            w2p, a2p, g2p = pad_lora(rwkv_w2[j]), pad_lora(rwkv_a2[j]), rwkv_g2[j]
            hid = jnp.arange(RWKV_W) // RWKV_HD
            seg = (hid[:, None] == hid[None, :]).astype(F32)[:RWKV_W // 2, :RWKV_W // 2]
            rk = rwkv_rk[j].reshape(1, RWKV_W)
            wo_nsa, wo_rw = even_w_out[j][:512].astype(BF16), even_w_out[j][512:].astype(BF16)

            kv, qt, gt, ks, vst, kw, vwt, rw, hl = even_proj(xp, nw, sc1p, sh1p, w_packed, tm_p, 8)
            kvc = compress_prompt(kv, wts, wc, _row_tile(t, 512))
            gates = gt[:24].reshape(NSA_KV_HEADS, 12, t)
            gates = jnp.pad(gates, ((0, 0), (0, 4), (0, 0)))[None]
            o_nsa = nsa_attention(
                qt[None], gates, kvc[None], kvc.T[None], ks[None], vst[None], kw[None], vwt[None],
                tq=tq, tk=tk, wk=WINDOW + tq,
                pos0_fn=lambda qi: qi * tq,
                wstart_fn=lambda qi: jnp.maximum(qi * tq - WINDOW, 0),
                wpos0_fn=lambda qi: jnp.maximum(qi * tq - WINDOW, 0))[0]
            o_rw, s_rw = rwkv_mix(rw, jnp.zeros((1, 8, RW_COLS), F32), jnp.zeros((1, RWKV_HEADS, 64, 64), F32),
                                  mu, vec, w2p, a2p, g2p, seg, rk, c=64, valid=64)
            xp = out_proj([o_nsa, o_rw], [wo_nsa, wo_rw], xp, gt1p, tm_p)
            outs["cmp_p"].append(kv[:, 0:256].reshape(1, t, 2, 2, 64))
            outs["sel_p"].append(kv[:, 256:512].reshape(1, t, 2, 2, 64))
            kvw_rows = kv[:, 512:768].reshape(1, t, 2, 2, 64)
            outs["win_p"].append(kvw_rows[:, -min(WINDOW, t):])
            outs["rw_p"].append(s_rw)
            outs["sh_p"].append(hl[-1:])

            kv, qt, gt, _, _, _, _, rw, hl = even_proj(xs, nw, sc1s, sh1s, w_packed, tm_s, ns)
            kv_new = unpad(kv)
            rw0 = small_matmul(jnp.pad(state_rwkv_shift[j], ((0, -bs % 8), (0, 0))), w_packed[:, E_RW:])[:bs]
            rw0 = jnp.pad(rw0[:, None, :], ((0, 0), (7, 0), (0, 0)))
            pool_cmp = cache_nsa_cmp[j].transpose(0, 2, 3, 4, 1).reshape(-1, 256, page)
            pool_sel = cache_nsa_sel[j].transpose(0, 2, 3, 4, 1).reshape(-1, 256, page)
            kvc_s = compress_paged(pool_cmp, page_table, nsa_cmp_pos[j], wc, math.gcd(n_pages, 16))
            tail = jnp.pad(kv_new[:, :, 256:512], ((0, 0), (0, tk - ts), (0, 0)))
            wbuf = state_nsa_win[j].reshape(bs, wb, 256)
            kvw_all = jnp.concatenate([wbuf, kv_new[:, :, 512:768]], axis=1)
            wk_s = -(-(wb + ts) // 128) * 128
            kvw_pad = jnp.pad(kvw_all, ((0, 0), (0, wk_s - wb - ts), (0, 0)))
            kw_s = kvw_pad[:, :, :128].astype(BF16)
            vwt_s = jnp.swapaxes(kvw_pad[:, :, 128:], 1, 2).astype(BF16)
            qt_s = jnp.pad(qt.reshape(512, bs, SPAD).transpose(1, 0, 2), ((0, 0), (0, 0), (0, tq_s - SPAD)))
            g_s = gt[:24].reshape(NSA_KV_HEADS, 12, bs, SPAD).transpose(2, 0, 1, 3)
            g_s = jnp.pad(g_s, ((0, 0), (0, 0), (0, 4), (0, tq_s - SPAD)))
            o_nsa = nsa_attention_paged(
                qt_s, g_s, kvc_s, jnp.swapaxes(kvc_s, 1, 2), pool_sel, page_table, tail, kw_s, vwt_s,
                tq=tq_s, tk=tk, wk=wk_s,
                pos0_fn=lambda qi: past,
                wstart_fn=lambda qi: 0,
                wpos0_fn=lambda qi: past - wb)
            o_nsa = o_nsa[:, :SPAD].reshape(ns, 512)
            o_rw, s_rw = rwkv_mix(rw, rw0, state_rwkv[j], mu, vec, w2p, a2p, g2p, seg, rk, c=SPAD, valid=ts)
            xs = out_proj([o_nsa, o_rw], [wo_nsa, wo_rw], xs, gt1s, tm_s)
            outs["cmp_s"].append(kv_new[:, :, 0:256].reshape(bs, ts, 2, 2, 64))
            outs["sel_s"].append(kv_new[:, :, 256:512].reshape(bs, ts, 2, 2, 64))
            outs["win_s"].append(kvw_all[:, -wb:].reshape(bs, wb, 2, 2, 64))
            outs["rw_s"].append(s_rw)
            outs["sh_s"].append(hl.reshape(bs, SPAD, d)[:, ts - 1])
        else:
            w_in = odd_w_in[j]
            w_packed = jnp.concatenate([w_in, jnp.zeros((d, O_COLS - w_in.shape[1]), F32)], axis=1).astype(BF16)
            conv_w8 = jnp.pad(gdn_conv_w[j], ((0, 8 - CONV_W), (0, 0)))
            hp = jnp.zeros((8, 128), F32)
            hp = hp.at[0, 8:16].set(-jnp.exp(gdn_a_log[j])).at[1, 8:16].set(gdn_dt_bias[j])
            gnw = gdn_norm_w[j][None, :]
            wo = odd_w_out[j].astype(BF16)

            qkv, z, ba = odd_proj(xp, nw, sc1p, sh1p, w_packed, tm_p)
            o_g, s_g = gdn_mix(qkv, z, ba, jnp.zeros((1, 8, 3 * GDN_W), F32),
                               jnp.zeros((1, GDN_HEADS, GDN_HD, GDN_HD), F32), conv_w8, hp, gnw, c=64, valid=64)
            xp = out_proj([o_g], [wo], xp, gt1p, tm_p)
            outs["gd_p"].append(s_g)
            outs["cv_p"].append(qkv[None, -(CONV_W - 1):])

            qkv, z, ba = odd_proj(xs, nw, sc1s, sh1s, w_packed, tm_s)
            cs = jnp.pad(state_gdn_conv[j], ((0, 0), (8 - (CONV_W - 1), 0), (0, 0)))
            o_g, s_g = gdn_mix(qkv, z, ba, cs, state_gdn[j], conv_w8, hp, gnw, c=SPAD, valid=ts)
            xs = out_proj([o_g], [wo], xs, gt1s, tm_s)
            xpad = jnp.concatenate([state_gdn_conv[j], unpad(qkv)], axis=1)
            outs["gd_s"].append(s_g)
            outs["cv_s"].append(xpad[:, -(CONV_W - 1):])

        nwf = norm_ffn[i][None, :]
        w_r = jnp.concatenate([moe_w_exp[i], moe_w_grp[i], jnp.zeros((d, LANE - N_EXPERTS - N_GROUPS), F32)], axis=1)
        b_r = jnp.concatenate([moe_b_exp[i], moe_b_grp[i], jnp.zeros((LANE - N_EXPERTS - N_GROUPS,), F32)])[None, :]
        h2, gate = moe_router(xp, nwf, sc2p, sh2p, w_r, b_r, tm_p)
        xp = moe_grouped(h2, gate, w1_all, w3_all, w2_all, i * N_EXPERTS, xp, gt2p, _row_tile(t, 1024))
        h2, gate = moe_router(xs, nwf, sc2s, sh2s, w_r, b_r, tm_s)
        xs = moe_ffn(h2, gate, w1_all, w3_all, w2_all, i * N_EXPERTS, xs, gt2s, tm_s)

    nf = norm_final[None, :]
    y_prompt = final_norm(xp, nf, tm_p)[None]
    y_sample = unpad(final_norm(xs, nf, tm_s))
    st = lambda key: jnp.stack(outs[key])
    return (y_prompt, y_sample, st("cmp_p"), st("cmp_s"), st("sel_p"), st("sel_s"), st("win_p"), st("win_s"),
            st("rw_p"), st("rw_s"), st("sh_p"), st("sh_s"), st("gd_p"), st("gd_s"), st("cv_p"), st("cv_s"))
```

```python
import functools
import math

import jax
import jax.numpy as jnp
from jax import lax
from jax.experimental import pallas as pl
from jax.experimental.pallas import tpu as pltpu

F32 = jnp.float32
BF16 = jnp.bfloat16
HIGHEST = lax.Precision.HIGHEST

NSA_HEADS = 8
NSA_KV_HEADS = 2
NSA_GROUP = 4
NSA_HD = 64
CMP_BLK = 64
SEL_BLK = 64
TOPK_BLK = 16
WINDOW = 512
FORCE_BONUS = 2.0 * NSA_GROUP
RWKV_HEADS = 8
RWKV_HD = 64
RWKV_W = 512
RWKV_GN_EPS = 64e-5
GDN_HEADS = 8
GDN_HD = 128
GDN_W = 1024
CONV_W = 4
N_GROUPS = 4
EXP_PER_GROUP = 8
N_EXPERTS = 32
EPS = 1e-6
NEG = -1e30

LANE = 128
GRP_LANE = 64
ROW_ALIGN = 16
SAMPLE_TILE_SLOTS = 8
SPAD = 8
VMEM_LIMIT = 56 * 1024 * 1024

NN = (((1,), (0,)), ((), ()))
NT = (((1,), (1,)), ((), ()))
TN = (((0,), (0,)), ((), ()))

E_Q, E_KV, E_G, E_RW = 0, 512, 1280, 1408
E_COLS = 1408 + 1920
RW_COLS = 1920
O_COLS = 3072 + 1024 + 128


def _mm(a, b, dims=NN):
    return lax.dot_general(a.astype(BF16), b.astype(BF16), dims, preferred_element_type=F32)


def _mmh(a, b, dims=NN):
    return lax.dot_general(a.astype(F32), b.astype(F32), dims, precision=HIGHEST, preferred_element_type=F32)


def _split(a):
    hi = a.astype(BF16)
    return hi, (a - hi.astype(F32)).astype(BF16)


def _mm3(a, b, dims=NN):
    ah, al = _split(a)
    bh, bl = _split(b)
    d = lambda x, y: lax.dot_general(x, y, dims, preferred_element_type=F32)
    return d(ah, bh) + (d(ah, bl) + d(al, bh))


def _split3(x):
    h1 = x.astype(BF16)
    r1 = x - h1.astype(F32)
    h2 = r1.astype(BF16)
    return h1, h2, (r1 - h2.astype(F32)).astype(BF16)


def _mm01(m01, x):
    m = m01.astype(BF16)
    parts = _split3(x)
    d = lambda y: lax.dot_general(m, y, NN, preferred_element_type=F32)
    return d(parts[0]) + (d(parts[1]) + d(parts[2]))


def _head_sums(xs, seg_half):
    r = xs[0].shape[0]
    half = seg_half.shape[0]
    pieces = [p[:, h0:h0 + half] for x in xs for p in _split3(x) for h0 in (0, half)]
    out = lax.dot_general(jnp.concatenate(pieces, axis=0), seg_half.astype(BF16), NN, preferred_element_type=F32)
    res = []
    for i in range(len(xs)):
        o = [out[(6 * i + u) * r:(6 * i + u + 1) * r] for u in range(6)]
        res.append(jnp.concatenate([o[0] + (o[2] + o[4]), o[1] + (o[3] + o[5])], axis=1))
    return res


def _sigmoid(x):
    return 1.0 / (1.0 + jnp.exp(-x))


def _silu(x):
    return x * _sigmoid(x)


def _softplus(x):
    return jnp.maximum(x, 0.0) + jnp.log(1.0 + jnp.exp(-jnp.abs(x)))


def _cparams(sem):
    return pltpu.CompilerParams(dimension_semantics=sem, vmem_limit_bytes=VMEM_LIMIT)


def _norm_mod(x, nw, sc, sh):
    y = x * lax.rsqrt(jnp.mean(x * x, axis=-1, keepdims=True) + EPS)
    return (y * nw) * (1.0 + sc) + sh


def _mod_spec(rows_mod, tm, d):
    if rows_mod == 1:
        return pl.BlockSpec((1, d), lambda i: (0, 0))
    return pl.BlockSpec((tm, d), lambda i: (i, 0))


def _adaln_body(c_ref, w_ref, b_ref, o_ref):
    o_ref[0] = _mmh(_silu(c_ref[...]), w_ref[0]) + b_ref[0]


def adaln(c_all, w_ada, b_ada):
    depth, d, n6 = w_ada.shape
    rows = c_all.shape[0]
    tn = 768
    return pl.pallas_call(
        _adaln_body,
        grid=(depth, n6 // tn),
        in_specs=[pl.BlockSpec((rows, d), lambda l, j: (0, 0)),
                  pl.BlockSpec((1, d, tn), lambda l, j: (l, 0, j)),
                  pl.BlockSpec((1, 1, tn), lambda l, j: (l, 0, j))],
        out_specs=pl.BlockSpec((1, rows, tn), lambda l, j: (l, 0, j)),
        out_shape=jax.ShapeDtypeStruct((depth, rows, n6), F32),
        compiler_params=_cparams(("arbitrary", "arbitrary")),
        name="adaln",
    )(c_all, w_ada, b_ada.reshape(depth, 1, n6))


def _even_proj_body(x_ref, nw_ref, sc_ref, sh_ref, w_ref,
                    kv_ref, qt_ref, gt_ref, ks_ref, vst_ref, kw_ref, vwt_ref, rw_ref, hl_ref):
    h = _norm_mod(x_ref[...], nw_ref[...], sc_ref[...], sh_ref[...])
    hl = hl_ref.shape[0]
    hl_ref[...] = h[h.shape[0] - hl:, :]
    hb = h.astype(BF16)
    q = _mm(hb, w_ref[:, E_Q:E_Q + 512]) * (NSA_HD ** -0.5)
    qt_ref[...] = q.T.astype(BF16)
    kv = _mm(hb, w_ref[:, E_KV:E_KV + 768])
    kv_ref[...] = kv
    ks_ref[...] = kv[:, 256:384].astype(BF16)
    vst_ref[...] = kv[:, 384:512].T.astype(BF16)
    kw_ref[...] = kv[:, 512:640].astype(BF16)
    vwt_ref[...] = kv[:, 640:768].T.astype(BF16)
    g = _sigmoid(_mm(hb, w_ref[:, E_G:E_G + 128]))
    gt_ref[...] = g.T
    rw_ref[...] = _mm(hb, w_ref[:, E_RW:E_RW + RW_COLS])


def even_proj(x, nw, sc, sh, w_packed, tm, hl_rows):
    n, d = x.shape
    rows_mod = sc.shape[0]
    row = lambda c: pl.BlockSpec((tm, c), lambda i: (i, 0))
    col = lambda r: pl.BlockSpec((r, tm), lambda i: (0, i))
    return pl.pallas_call(
        _even_proj_body,
        grid=(n // tm,),
        in_specs=[row(d), pl.BlockSpec((1, d), lambda i: (0, 0)),
                  _mod_spec(rows_mod, tm, d), _mod_spec(rows_mod, tm, d),
                  pl.BlockSpec((d, E_COLS), lambda i: (0, 0))],
        out_specs=[row(768), col(512), col(128), row(128), col(128), row(128), col(128), row(RW_COLS),
                   pl.BlockSpec((hl_rows, d), lambda i: (0, 0))],
        out_shape=[jax.ShapeDtypeStruct((n, 768), F32),
                   jax.ShapeDtypeStruct((512, n), BF16),
                   jax.ShapeDtypeStruct((128, n), F32),
                   jax.ShapeDtypeStruct((n, 128), BF16),
                   jax.ShapeDtypeStruct((128, n), BF16),
                   jax.ShapeDtypeStruct((n, 128), BF16),
                   jax.ShapeDtypeStruct((128, n), BF16),
                   jax.ShapeDtypeStruct((n, RW_COLS), F32),
                   jax.ShapeDtypeStruct((hl_rows, d), F32)],
        compiler_params=_cparams(("arbitrary",)),
        name="even_proj",
    )(x, nw, sc, sh, w_packed)


def _pack_even_w(w_in):
    d = w_in.shape[0]
    z = lambda c: jnp.zeros((d, c), w_in.dtype)
    nsa = 1304
    rw = w_in[:, nsa:]
    parts = [w_in[:, :1280], w_in[:, 1280:1304], z(104),
             rw[:, :1536], rw[:, 1536:1600], z(64), rw[:, 1600:1664], z(64), rw[:, 1664:1792]]
    return jnp.concatenate(parts, axis=1).astype(BF16)


def _pack_rw_vec(v):
    z = jnp.zeros((64,), v.dtype)
    return jnp.concatenate([v[:1536], v[1536:1600], z, v[1600:1664], z, v[1664:1792]])[None, :]


def _mm_body(x_ref, w_ref, o_ref):
    o_ref[...] = _mm(x_ref[...], w_ref[...])


def small_matmul(x, w):
    return pl.pallas_call(
        _mm_body,
        out_shape=jax.ShapeDtypeStruct((x.shape[0], w.shape[1]), F32),
        compiler_params=pltpu.CompilerParams(vmem_limit_bytes=VMEM_LIMIT),
        name="small_matmul",
    )(x, w)


def _compress_body(x_ref, wts_ref, wc_ref, o_ref):
    x = x_ref[...]
    nb = x.shape[0] // CMP_BLK
    pooled = jnp.sum(x.reshape(nb, CMP_BLK, x.shape[-1]) * wts_ref[...][None], axis=1)
    o_ref[...] = _mm(pooled, wc_ref[...])


def _compress_paged_body(pt_ref, *refs, pps):
    page_refs = refs[:pps]
    wp_ref, wc_ref, o_ref = refs[pps:]
    x = jnp.concatenate([r[0] for r in page_refs], axis=1)
    pooled_t = jnp.concatenate([_mm(x[0:128], wp_ref[0]), _mm(x[128:256], wp_ref[1])], axis=0)
    nb = o_ref.shape[1]
    o_ref[0] = _mm(pooled_t.T[:nb], wc_ref[...])


def _cmp_weights(pos_wts, w_c):
    wts = jnp.repeat(pos_wts.T, 128, axis=1)
    eye2 = jnp.eye(2, dtype=w_c.dtype)
    blocks = [jnp.kron(eye2, w_c[c]) for c in range(2)]
    z = jnp.zeros((128, 128), w_c.dtype)
    wc = jnp.concatenate([jnp.concatenate([blocks[0], z], axis=1),
                          jnp.concatenate([z, blocks[1]], axis=1)], axis=0)
    return wts, wc


def compress_prompt(kv, wts, wc, tr):
    t = kv.shape[0]
    nb = tr // CMP_BLK
    return pl.pallas_call(
        _compress_body,
        grid=(t // tr,),
        in_specs=[pl.BlockSpec((tr, 256), lambda i: (i, 0)),
                  pl.BlockSpec((CMP_BLK, 256), lambda i: (0, 0)),
                  pl.BlockSpec((256, 256), lambda i: (0, 0))],
        out_specs=pl.BlockSpec((nb, 256), lambda i: (i, 0)),
        out_shape=jax.ShapeDtypeStruct((t // CMP_BLK, 256), F32),
        compiler_params=_cparams(("arbitrary",)),
        name="compress_prompt",
    )(kv, wts, wc)


def compress_paged(pool_t, page_table, pos_wts, wc, pages_per_step):
    b, n_pages = page_table.shape
    page = pool_t.shape[2]
    pps = pages_per_step
    nb = pps * page // CMP_BLK
    p_idx = jnp.arange(pps * page)
    wp = jax.nn.one_hot(p_idx // CMP_BLK, LANE, dtype=F32)[None] * pos_wts[:, p_idx % CMP_BLK][:, :, None]

    def page_spec(u):
        return pl.BlockSpec((1, 256, page), lambda bi, g, pt: (pt[bi, g * pps + u], 0, 0))

    grid_spec = pltpu.PrefetchScalarGridSpec(
        num_scalar_prefetch=1,
        grid=(b, n_pages // pps),
        in_specs=[page_spec(u) for u in range(pps)] + [
            pl.BlockSpec((2, pps * page, LANE), lambda bi, g, pt: (0, 0, 0)),
            pl.BlockSpec((256, 256), lambda bi, g, pt: (0, 0))],
        out_specs=pl.BlockSpec((1, nb, 256), lambda bi, g, pt: (bi, g, 0)),
    )
    return pl.pallas_call(
        functools.partial(_compress_paged_body, pps=pps),
        grid_spec=grid_spec,
        out_shape=jax.ShapeDtypeStruct((b, n_pages * page // CMP_BLK, 256), F32),
        compiler_params=_cparams(("arbitrary", "arbitrary")),
        name="compress_paged",
    )(page_table, *([pool_t] * pps), wp, wc)


def _gather_sel_body(pt_ref, tiles_ref, cnt_ref, *refs, pps, n_page_steps, nt):
    del pt_ref
    page_refs = refs[:pps]
    tail_ref, ks_ref, vst_ref = refs[pps:]
    bi = pl.program_id(0)
    a = pl.program_id(1)
    j = tiles_ref[bi * nt + jnp.minimum(a, cnt_ref[bi] - 1)]
    live = a < cnt_ref[bi]

    @pl.when(live & (j < n_page_steps))
    def _():
        ks_ref[0] = jnp.concatenate([r[0][0:128].T for r in page_refs], axis=0).astype(BF16)
        vst_ref[0] = jnp.concatenate([r[0][128:256] for r in page_refs], axis=1).astype(BF16)

    @pl.when(live & (j >= n_page_steps))
    def _():
        x = tail_ref[0]
        ks_ref[0] = x[:, :128].astype(BF16)
        vst_ref[0] = x[:, 128:].T.astype(BF16)


def gather_sel(pool_t, page_table, tail, tk, tiles, cnt, n_slots):
    b, n_pages = page_table.shape
    page = pool_t.shape[2]
    pps = tk // page
    n_page_steps = n_pages // pps
    nt = n_page_steps + 1
    nk = n_slots * tk

    def slot(bi, a, pt, tiles, cnt):
        return jnp.minimum(a, cnt[bi] - 1)

    def page_spec(u):
        def index(bi, a, pt, tiles, cnt):
            j = tiles[bi * nt + slot(bi, a, pt, tiles, cnt)]
            return (pt[bi, jnp.minimum(j * pps + u, n_pages - 1)], 0, 0)
        return pl.BlockSpec((1, 256, page), index)

    grid_spec = pltpu.PrefetchScalarGridSpec(
        num_scalar_prefetch=3,
        grid=(b, jnp.max(cnt)),
        in_specs=[page_spec(u) for u in range(pps)] + [pl.BlockSpec((1, tk, 256), lambda bi, a, *_: (bi, 0, 0))],
        out_specs=[pl.BlockSpec((1, tk, 128), lambda bi, a, *s: (bi, slot(bi, a, *s), 0)),
                   pl.BlockSpec((1, 128, tk), lambda bi, a, *s: (bi, 0, slot(bi, a, *s)))],
    )
    return pl.pallas_call(
        functools.partial(_gather_sel_body, pps=pps, n_page_steps=n_page_steps, nt=nt),
        grid_spec=grid_spec,
        out_shape=[jax.ShapeDtypeStruct((b, nk, 128), BF16), jax.ShapeDtypeStruct((b, 128, nk), BF16)],
        compiler_params=_cparams(("arbitrary", "arbitrary")),
        name="gather_sel",
    )(page_table, tiles, cnt, *([pool_t] * pps), tail)


MASKED = -1e30
M_INIT = -1e29


def _nsa_query(qt_ref, k, tq):
    w4 = NSA_GROUP * tq
    qb = qt_ref[0].astype(F32)
    qcat = jnp.concatenate([qb[g * 64:(g + 1) * 64] for g in range(NSA_GROUP)], axis=1)
    q2 = jnp.concatenate([qcat, qcat], axis=0)
    row = lax.broadcasted_iota(jnp.int32, (128, w4), 0)
    qe = jnp.where(row // 64 == k, q2, 0.0)
    gidx = lax.broadcasted_iota(jnp.int32, (128, w4), 1) // tq
    base = jnp.where(k == 0, 0.5, 0.5 / 16.0)
    slope = base * jnp.where(gidx == 0, 1.0, jnp.where(gidx == 1, 0.5, jnp.where(gidx == 2, 0.25, 0.125)))
    mult = jnp.where(row == 0, 16.0, jnp.where(row == 1, 1.0, jnp.where(row == 2, 128.0,
                                                                         jnp.where(row == 3, 64.0, 0.0))))
    return jnp.concatenate([qe, slope * mult], axis=0).astype(BF16)


def _pos_features(rows, tile_rel):
    r = lax.broadcasted_iota(jnp.int32, (rows, LANE), 0)
    lane = lax.broadcasted_iota(jnp.int32, (rows, LANE), 1)
    ab = jnp.where(lane == 0, r // 16, jnp.where(lane == 1, r % 16, 0)).astype(F32)
    return jnp.where(lane == 2, tile_rel, ab).astype(BF16)


def _gate_rows(gb, j, tq):
    return jnp.concatenate([gb[g * 3 + j:g * 3 + j + 1, :] for g in range(NSA_GROUP)], axis=1)


def _nsa_select_body(qt_ref, g_ref, kvc_ref, kvct_ref, kw_ref, vwt_ref, part_ref, sel_ref, flag_ref, *,
                     tq, tk, wk, nbc, nb, pos0_fn, wstart_fn, wpos0_fn):
    i = pl.program_id(1)
    k = pl.program_id(2)
    w4 = NSA_GROUP * tq
    pos0 = pos0_fn(i)
    qa = _nsa_query(qt_ref, k, tq)
    pos_q = pos0 + lax.broadcasted_iota(jnp.int32, (1, w4), 1) % tq

    def softmax_cols(s, bad):
        s = jnp.where(bad, MASKED, s)
        m = jnp.maximum(jnp.max(s, axis=0, keepdims=True), M_INIT)
        e = jnp.exp(s - m)
        return e / jnp.maximum(jnp.sum(e, axis=0, keepdims=True), 1e-30)

    n_i = lax.broadcasted_iota(jnp.int32, (nbc, LANE), 0)
    lane_c = lax.broadcasted_iota(jnp.int32, (nbc, LANE), 1)
    feat_c = jnp.where(lane_c == 3, n_i - pos0 // CMP_BLK, 0).astype(F32).astype(BF16)
    kc = jnp.concatenate([kvc_ref[0][:, :128].astype(BF16), feat_c], axis=1)
    c_end = lax.broadcasted_iota(jnp.int32, (nbc, 1), 0) * CMP_BLK + (CMP_BLK - 1)
    p_c = softmax_cols(lax.dot_general(kc, qa, NN, preferred_element_type=F32), c_end > pos_q)
    vct = kvct_ref[0, pl.ds(pl.multiple_of(128 + k * 64, 64), 64), :]
    o_c = _mm(vct, p_c)

    imp = p_c[:, 0:tq]
    for g in range(1, NSA_GROUP):
        imp = imp + p_c[:, g * tq:(g + 1) * tq]
    if nb > nbc:
        imp = jnp.concatenate([imp, jnp.zeros((nb - nbc, tq), F32)], axis=0)
    blk = lax.broadcasted_iota(jnp.int32, (nb, tq), 0)
    cur = (pos0 + lax.broadcasted_iota(jnp.int32, (1, tq), 1)) // SEL_BLK
    forced = (blk == cur) | (blk == cur - 1) | (blk == 0)
    score = jnp.where(blk <= cur, imp + jnp.where(forced, FORCE_BONUS, 0.0), -1.0)
    for _ in range(min(TOPK_BLK, nb)):
        m = jnp.max(score, axis=0, keepdims=True)
        first = jnp.min(jnp.where(score == m, blk, nb), axis=0, keepdims=True)
        score = jnp.where(blk == first, -2.0, score)
    sel = jnp.where(score == -2.0, 1.0, 0.0)
    sel_ref[0, 0] = sel
    bpt = tk // SEL_BLK
    any_row = jnp.max(sel, axis=1, keepdims=True)
    flag_ref[0, 0] = jnp.max(any_row.reshape(nb // bpt, bpt, 1), axis=1)

    wstart = wstart_fn(i)
    if not isinstance(wstart, int):
        wstart = pl.multiple_of(wstart, 128)
    wpos0 = wpos0_fn(i)
    tile_rel = jnp.asarray((wpos0 - pos0) // 128, F32)
    kw = jnp.concatenate([kw_ref[0, pl.ds(wstart, wk), :], _pos_features(wk, tile_rel)], axis=1)
    dist_w = pos_q - (wpos0 + lax.broadcasted_iota(jnp.int32, (wk, 1), 0))
    p_w = softmax_cols(lax.dot_general(kw, qa, NN, preferred_element_type=F32), (dist_w < 0) | (dist_w >= WINDOW))
    vwin = vwt_ref[0, pl.ds(pl.multiple_of(k * 64, 64), 64), pl.ds(wstart, wk)]
    o_w = _mm(vwin, p_w)

    gb = g_ref[0, 0]
    part_ref[0, 0] = _gate_rows(gb, 0, tq) * o_c + _gate_rows(gb, 2, tq) * o_w


def nsa_select(qt, gates, kvc, kvct, kw, vwt, *, nb, tq, tk, wk, pos0_fn, wstart_fn, wpos0_fn):
    b, _, nq = qt.shape
    nbc = kvc.shape[1]
    nw = kw.shape[1]
    nqt = nq // tq
    nt = nb * SEL_BLK // tk
    w4 = NSA_GROUP * tq
    assert nbc <= 256 and tk <= 512 and wk <= 1024
    body = functools.partial(_nsa_select_body, tq=tq, tk=tk, wk=wk, nbc=nbc, nb=nb, pos0_fn=pos0_fn,
                             wstart_fn=wstart_fn, wpos0_fn=wpos0_fn)
    full = lambda s1, s2: pl.BlockSpec((1, s1, s2), lambda bi, i, k: (bi, 0, 0))
    step = lambda s1, s2: pl.BlockSpec((1, 1, s1, s2), lambda bi, i, k: (bi, i * NSA_KV_HEADS + k, 0, 0))
    return pl.pallas_call(
        body,
        grid=(b, nqt, NSA_KV_HEADS),
        in_specs=[pl.BlockSpec((1, 256, tq), lambda bi, i, k: (bi, k, i)),
                  pl.BlockSpec((1, 1, 16, tq), lambda bi, i, k: (bi, k, 0, i)),
                  full(nbc, 256), full(256, nbc), full(nw, 128), full(128, nw)],
        out_specs=[step(64, w4), step(nb, tq), step(nt, 1)],
        out_shape=[jax.ShapeDtypeStruct((b, nqt * 2, 64, w4), F32),
                   jax.ShapeDtypeStruct((b, nqt * 2, nb, tq), F32),
                   jax.ShapeDtypeStruct((b, nqt * 2, nt, 1), F32)],
        compiler_params=_cparams(("arbitrary", "arbitrary", "arbitrary")),
        name="nsa_select",
    )(qt, gates, kvc, kvct, kw, vwt)


def _nsa_selected_body(list_ref, slot_ref, cnt_ref, qt_ref, g_ref, sel_ref, ks_ref, vst_ref, part_ref, o_ref, *,
                       tq, tk, nt, pos0_fn):
    bi = pl.program_id(0)
    i = pl.program_id(1)
    k = pl.program_id(2)
    step = (bi * pl.num_programs(1) + i) * NSA_KV_HEADS + k
    w4 = NSA_GROUP * tq
    pos0 = pos0_fn(i)
    qa = _nsa_query(qt_ref, k, tq)
    pos_q = pos0 + lax.broadcasted_iota(jnp.int32, (1, w4), 1) % tq
    bpt = tk // SEL_BLK
    row_k = lax.broadcasted_iota(jnp.int32, (tk, 1), 0)
    r = lax.broadcasted_iota(jnp.int32, (tk, LANE), 0)
    lane = lax.broadcasted_iota(jnp.int32, (tk, LANE), 1)
    feat_ab = jnp.where(lane == 0, r // 16, jnp.where(lane == 1, r % 16, 0)).astype(F32)

    n_act = cnt_ref[step]

    def tile_scores(jj, live):
        j = list_ref[step * nt + jj]
        off = pl.multiple_of(j * tk, tk)
        buf = pl.multiple_of(slot_ref[step * nt + jj] * tk, tk)
        tile_rel = ((off - pos0) // 128).astype(F32)
        feat = jnp.where(lane == 2, tile_rel, feat_ab).astype(BF16)
        kj = jnp.concatenate([ks_ref[0, pl.ds(buf, tk), :], feat], axis=1)
        s = lax.dot_general(kj, qa, NN, preferred_element_type=F32)
        selb = (sel_ref[0, 0, pl.ds(pl.multiple_of(j * bpt, bpt), bpt), :] - 1.0) * (-MASKED)
        selb = jnp.concatenate([selb] * NSA_GROUP, axis=1) + jnp.where(live, 0.0, MASKED)
        s = s + jnp.broadcast_to(selb[:, None, :], (bpt, SEL_BLK, w4)).reshape(tk, w4)
        s = jnp.where(row_k > pos_q - off, MASKED, s)
        return s, vst_ref[0, pl.ds(pl.multiple_of(k * 64, 64), 64), pl.ds(buf, tk)]

    def kv_pair(pp, carry):
        m_i, l_i, acc = carry
        second = 2 * pp + 1
        s_a, v_a = tile_scores(2 * pp, True)
        s_b, v_b = tile_scores(jnp.minimum(second, n_act - 1), second < n_act)
        m_new = jnp.maximum(m_i, jnp.maximum(jnp.max(s_a, axis=0, keepdims=True), jnp.max(s_b, axis=0, keepdims=True)))
        p_a = jnp.exp(s_a - m_new)
        p_b = jnp.exp(s_b - m_new)
        alpha = jnp.exp(m_i - m_new)
        l_new = l_i * alpha + (jnp.sum(p_a, axis=0, keepdims=True) + jnp.sum(p_b, axis=0, keepdims=True))
        return m_new, l_new, acc * alpha + (_mm(v_a, p_a) + _mm(v_b, p_b))

    init = (jnp.full((1, w4), M_INIT, F32), jnp.zeros((1, w4), F32), jnp.zeros((64, w4), F32))
    _, l_s, acc_s = lax.fori_loop(0, (n_act + 1) // 2, kv_pair, init)
    o_s = acc_s / jnp.maximum(l_s, 1e-30)
    o_t = part_ref[0, 0] + _gate_rows(g_ref[0, 0], 1, tq) * o_s
    o_ref[0] = jnp.concatenate([o_t[:, g * tq:(g + 1) * tq].T for g in range(NSA_GROUP)], axis=1)


def nsa_selected(tile_list, slot_list, tile_cnt, qt, gates, sel, ks, vst, part, *, tq, tk, pos0_fn):
    b, _, nq = qt.shape
    nk = ks.shape[1]
    nb = sel.shape[2]
    nt = nb * SEL_BLK // tk
    w4 = NSA_GROUP * tq
    full = lambda s1, s2: pl.BlockSpec((1, s1, s2), lambda bi, i, k, *_: (bi, 0, 0))
    step = lambda s1, s2: pl.BlockSpec((1, 1, s1, s2), lambda bi, i, k, *_: (bi, i * NSA_KV_HEADS + k, 0, 0))
    grid_spec = pltpu.PrefetchScalarGridSpec(
        num_scalar_prefetch=3,
        grid=(b, nq // tq, NSA_KV_HEADS),
        in_specs=[pl.BlockSpec((1, 256, tq), lambda bi, i, k, *_: (bi, k, i)),
                  pl.BlockSpec((1, 1, 16, tq), lambda bi, i, k, *_: (bi, k, 0, i)),
                  step(nb, tq), full(nk, 128), full(128, nk), step(64, w4)],
        out_specs=pl.BlockSpec((1, tq, 256), lambda bi, i, k, *_: (bi, i, k)),
    )
    return pl.pallas_call(
        functools.partial(_nsa_selected_body, tq=tq, tk=tk, nt=nt, pos0_fn=pos0_fn),
        grid_spec=grid_spec,
        out_shape=jax.ShapeDtypeStruct((b, nq, 512), F32),
        compiler_params=_cparams(("arbitrary", "arbitrary", "arbitrary")),
        name="nsa_selected",
    )(tile_list, slot_list, tile_cnt, qt, gates, sel, ks, vst, part)


def _active_first(active):
    order = jnp.argsort(jnp.where(active, 0, 1), axis=-1, stable=True).astype(jnp.int32)
    return order, jnp.sum(active, axis=-1).astype(jnp.int32)


def nsa_attention(qt, gates, kvc, kvct, ks, vst, kw, vwt, *, tq, tk, wk, pos0_fn, wstart_fn, wpos0_fn):
    nb = ks.shape[1] // SEL_BLK
    part, sel, flags = nsa_select(qt, gates, kvc, kvct, kw, vwt, nb=nb, tq=tq, tk=tk, wk=wk, pos0_fn=pos0_fn,
                                  wstart_fn=wstart_fn, wpos0_fn=wpos0_fn)
    order, cnt = _active_first(flags[..., 0] > 0.5)
    return nsa_selected(order.reshape(-1), order.reshape(-1), cnt.reshape(-1), qt, gates, sel, ks, vst, part,
                        tq=tq, tk=tk, pos0_fn=pos0_fn)


def nsa_attention_paged(qt, gates, kvc, kvct, pool_t, page_table, tail, kw, vwt, *, tq, tk, wk, pos0_fn, wstart_fn,
                        wpos0_fn):
    nb = (page_table.shape[1] * pool_t.shape[2] + tk) // SEL_BLK
    part, sel, flags = nsa_select(qt, gates, kvc, kvct, kw, vwt, nb=nb, tq=tq, tk=tk, wk=wk, pos0_fn=pos0_fn,
                                  wstart_fn=wstart_fn, wpos0_fn=wpos0_fn)
    active = flags[..., 0] > 0.5
    tiles_b, cnt_b = _active_first(jnp.any(active, axis=1))
    slot_of_tile = jnp.argsort(tiles_b, axis=-1).astype(jnp.int32)
    order, cnt = _active_first(active)
    slots = jnp.take_along_axis(jnp.broadcast_to(slot_of_tile[:, None, :], order.shape), order, axis=-1)

    def run(n_slots):
        ks, vst = gather_sel(pool_t, page_table, tail, tk, tiles_b.reshape(-1), cnt_b, n_slots)
        return nsa_selected(order.reshape(-1), slots.reshape(-1), cnt.reshape(-1), qt, gates, sel, ks, vst, part,
                            tq=tq, tk=tk, pos0_fn=pos0_fn)

    nt = tiles_b.shape[-1]
    few = min(SAMPLE_TILE_SLOTS, nt)
    return lax.cond(jnp.max(cnt_b) <= few, lambda: run(few), lambda: run(nt))


def _tri_inverse(ms, c):
    eye = (lax.broadcasted_iota(jnp.int32, (c, c), 0) == lax.broadcasted_iota(jnp.int32, (c, c), 1)).astype(F32)
    ps = [-m for m in ms]
    ts = [eye + p for p in ps]
    steps = max(int(math.ceil(math.log2(c))) - 1, 0)
    d = lambda x, y: lax.dot_general(x, y, NN, preferred_element_type=F32)
    for _ in range(steps):
        sp = [_split(p) for p in ps]
        ps = [d(ph, ph) + (d(ph, pl_) + d(pl_, ph)) for ph, pl_ in sp]
        sp = [_split(p) for p in ps]
        st = [_split(t) for t in ts]
        ts = [t + (d(th, ph) + (d(th, pl_) + d(tl, ph))) for t, (th, tl), (ph, pl_) in zip(ts, st, sp)]
    return ts


def _rwkv_body(rw_ref, rw0_ref, s0_ref, mu_ref, vec_ref, w2_ref, a2_ref, g2_ref, seg_ref, rk_ref,
               o_ref, sfin_ref, buf_ref, s_ref, y_ref, *, c, valid, n_chunks):
    ci = pl.program_id(1)
    halo = 8

    @pl.when(ci == 0)
    def _():
        buf_ref[0:halo, :] = rw0_ref[0]
        s_ref[...] = s0_ref[0]

    cur = rw_ref[...]
    buf_ref[halo:halo + c, :] = cur
    prev = buf_ref[halo - 1:halo - 1 + c, :]
    xr = cur + (prev - cur) * mu_ref[...]
    buf_ref[0:halo, :] = cur[c - halo:, :]

    vec = vec_ref[...]
    w0, a0, kkw, kaw, ln_w, ln_b = (vec[r:r + 1, :] for r in range(6))
    r = xr[:, 0:512]
    kx = xr[:, 512:1024]
    v = xr[:, 1024:1536]
    xw = xr[:, 1536:1664]
    xa = xr[:, 1664:1792]
    xg = xr[:, 1792:1920]
    wl = -jnp.exp(-_softplus(-(w0 + _mm(jnp.tanh(xw), w2_ref[...]))) - 0.5)
    a = _sigmoid(a0 + _mm(xa, a2_ref[...]))
    gate = _mm(_sigmoid(xg), g2_ref[...])
    seg = seg_ref[...]
    zk = kx * kkw
    k2 = kx * (1.0 + (a - 1.0) * kaw)
    zz_sum, rk_sum = _head_sums([zk * zk, r * k2 * rk_ref[...]], seg)
    kk = zk * lax.rsqrt(zz_sum + EPS)
    bonus = rk_sum * v
    if valid < c:
        live = lax.broadcasted_iota(jnp.int32, (c, 1), 0) < valid
        wl = jnp.where(live, wl, 0.0)
        kk = jnp.where(live, kk, 0.0)
        k2 = jnp.where(live, k2, 0.0)
        v = jnp.where(live, v, 0.0)
        r = jnp.where(live, r, 0.0)
    bb = kk * a

    ri = lax.broadcasted_iota(jnp.int32, (c, c), 0)
    cj = lax.broadcasted_iota(jnp.int32, (c, c), 1)
    tril = ri >= cj
    strict = ri > cj
    cw = _mm01(tril, wl)
    ecw = jnp.exp(cw)
    einv = jnp.exp(-cw)
    p_c = ecw[c - 1:c, :]
    kt = kk * jnp.exp(cw - wl)
    bt = bb * einv
    ki = k2 * einv
    rt = r * ecw
    bd = bt * p_c
    kd = ki * p_c

    heads = range(RWKV_HEADS)
    sls = [slice(h * RWKV_HD, (h + 1) * RWKV_HD) for h in heads]
    kt_h = [kt[:, sl] for sl in sls]
    bt_h = [bt[:, sl] for sl in sls]
    ki_h = [ki[:, sl] for sl in sls]
    rt_h = [rt[:, sl] for sl in sls]
    v_h = [v[:, sl] for sl in sls]
    l_m = [jnp.where(strict, _mm3(kt_h[h], bt_h[h], NT), 0.0) for h in heads]
    m_kk = [jnp.where(strict, _mm(kt_h[h], ki_h[h], NT), 0.0) for h in heads]
    a_rb = [jnp.where(tril, _mm(rt_h[h], bt_h[h], NT), 0.0) for h in heads]
    a_rk = [jnp.where(tril, _mm(rt_h[h], ki_h[h], NT), 0.0) for h in heads]
    mv = [_mm(m_kk[h], v_h[h]) for h in heads]
    y0 = [_mm(a_rk[h], v_h[h]) for h in heads]
    t_inv = _tri_inverse(l_m, c)
    w_h = [_mm3(t_inv[h], kt_h[h]) for h in heads]
    u_h = [-_mm3(t_inv[h], mv[h]) for h in heads]
    s_h = [s_ref[h] for h in heads]
    e_h = [u_h[h] - _mm(w_h[h], s_h[h], NT) for h in heads]
    y1 = [_mm(rt_h[h], s_h[h], NT) + y0[h] for h in heads]
    y_h = [y1[h] + _mm(a_rb[h], e_h[h]) for h in heads]
    ds = [_mm(e_h[h], bd[:, sls[h]], TN) + _mm(v_h[h], kd[:, sls[h]], TN) for h in heads]
    for h in heads:
        s_ref[h] = s_h[h] * p_c[:, sls[h]] + ds[h]
        mu_h = jnp.mean(y_h[h], axis=-1, keepdims=True)
        d_h = y_h[h] - mu_h
        var_h = jnp.mean(d_h * d_h, axis=-1, keepdims=True)
        y_ref[:, sls[h]] = d_h * lax.rsqrt(var_h + RWKV_GN_EPS)

    o_ref[...] = (y_ref[...] * ln_w + ln_b + bonus) * gate

    @pl.when(ci == n_chunks - 1)
    def _():
        sfin_ref[0] = s_ref[...]


def rwkv_mix(rw, rw0, s0, mu, vec, w2, a2, g2, seg, rk, *, c, valid):
    b = s0.shape[0]
    rows = rw.shape[0]
    n_chunks = rows // (b * c)
    const = lambda s: pl.BlockSpec(s, lambda bi, ci: tuple(0 for _ in s))
    return pl.pallas_call(
        functools.partial(_rwkv_body, c=c, valid=valid, n_chunks=n_chunks),
        grid=(b, n_chunks),
        in_specs=[pl.BlockSpec((c, RW_COLS), lambda bi, ci: (bi * n_chunks + ci, 0)),
                  pl.BlockSpec((1, 8, RW_COLS), lambda bi, ci: (bi, 0, 0)),
                  pl.BlockSpec((1, RWKV_HEADS, 64, 64), lambda bi, ci: (bi, 0, 0, 0)),
                  const((1, RW_COLS)), const((8, 512)), const((128, 512)), const((128, 512)), const((128, 512)),
                  const((RWKV_W // 2, RWKV_W // 2)), const((1, 512))],
        out_specs=[pl.BlockSpec((c, 512), lambda bi, ci: (bi * n_chunks + ci, 0)),
                   pl.BlockSpec((1, RWKV_HEADS, 64, 64), lambda bi, ci: (bi, 0, 0, 0))],
        out_shape=[jax.ShapeDtypeStruct((rows, 512), F32),
                   jax.ShapeDtypeStruct((b, RWKV_HEADS, 64, 64), F32)],
        scratch_shapes=[pltpu.VMEM((8 + c, RW_COLS), F32), pltpu.VMEM((RWKV_HEADS, 64, 64), F32),
                        pltpu.VMEM((c, 512), F32)],
        compiler_params=_cparams(("arbitrary", "arbitrary")),
        name="rwkv_mix",
    )(rw, rw0, s0, mu, vec, w2, a2, g2, seg, rk)


def _out_proj_body(*refs, n_in):
    a_refs = refs[:n_in]
    w_refs = refs[n_in:2 * n_in]
    x_ref, g_ref, o_ref = refs[2 * n_in:]
    y = _mm(a_refs[0][...], w_refs[0][...])
    for a_ref, w_ref in zip(a_refs[1:], w_refs[1:]):
        y = y + _mm(a_ref[...], w_ref[...])
    o_ref[...] = x_ref[...] + g_ref[...] * y


def out_proj(acts, weights, x, gate, tm):
    n, d = x.shape
    n_in = len(acts)
    return pl.pallas_call(
        functools.partial(_out_proj_body, n_in=n_in),
        grid=(n // tm,),
        in_specs=[pl.BlockSpec((tm, a.shape[1]), lambda i: (i, 0)) for a in acts]
        + [pl.BlockSpec(w.shape, lambda i: (0, 0)) for w in weights]
        + [pl.BlockSpec((tm, d), lambda i: (i, 0)), _mod_spec(gate.shape[0], tm, d)],
        out_specs=pl.BlockSpec((tm, d), lambda i: (i, 0)),
        out_shape=jax.ShapeDtypeStruct((n, d), F32),
        compiler_params=_cparams(("arbitrary",)),
        name="out_proj",
    )(*acts, *weights, x, gate)


def _odd_proj_body(x_ref, nw_ref, sc_ref, sh_ref, w_ref, qkv_ref, z_ref, ba_ref):
    hb = _norm_mod(x_ref[...], nw_ref[...], sc_ref[...], sh_ref[...]).astype(BF16)
    qkv_ref[...] = _mm(hb, w_ref[:, 0:3072])
    z_ref[...] = _mm(hb, w_ref[:, 3072:4096])
    ba_ref[...] = _mm(hb, w_ref[:, 4096:O_COLS])


def odd_proj(x, nw, sc, sh, w_packed, tm):
    n, d = x.shape
    rows_mod = sc.shape[0]
    row = lambda c: pl.BlockSpec((tm, c), lambda i: (i, 0))
    return pl.pallas_call(
        _odd_proj_body,
        grid=(n // tm,),
        in_specs=[row(d), pl.BlockSpec((1, d), lambda i: (0, 0)),
                  _mod_spec(rows_mod, tm, d), _mod_spec(rows_mod, tm, d),
                  pl.BlockSpec((d, O_COLS), lambda i: (0, 0))],
        out_specs=[row(3072), row(1024), row(128)],
        out_shape=[jax.ShapeDtypeStruct((n, 3072), F32), jax.ShapeDtypeStruct((n, 1024), F32),
                   jax.ShapeDtypeStruct((n, 128), F32)],
        compiler_params=_cparams(("arbitrary",)),
        name="odd_proj",
    )(x, nw, sc, sh, w_packed)


def _gdn_body(qkv_ref, z_ref, ba_ref, cs_ref, s0_ref, cw_ref, hp_ref, nw_ref,
              o_ref, sfin_ref, buf_ref, s_ref, *, c, valid, n_chunks):
    ci = pl.program_id(1)
    halo = 8

    @pl.when(ci == 0)
    def _():
        buf_ref[0:halo, :] = cs_ref[0]
        s_ref[...] = s0_ref[0]

    x = qkv_ref[...]
    buf_ref[halo:halo + c, :] = x
    cw = cw_ref[...]
    conv = buf_ref[halo - 3:halo - 3 + c, :] * cw[0:1, :]
    for j in range(1, CONV_W):
        conv = conv + buf_ref[halo - 3 + j:halo - 3 + j + c, :] * cw[j:j + 1, :]
    buf_ref[0:halo, :] = x[c - halo:, :]
    conv = _silu(conv)

    hp = hp_ref[...]
    ba = ba_ref[...]
    beta_f = _sigmoid(ba)
    g_f = hp[0:1, :] * _softplus(ba + hp[1:2, :])
    if valid < c:
        live = lax.broadcasted_iota(jnp.int32, (c, 1), 0) < valid
        beta_f = jnp.where(live, beta_f, 0.0)
        g_f = jnp.where(live, g_f, 0.0)
        conv = jnp.where(live, conv, 0.0)

    ri = lax.broadcasted_iota(jnp.int32, (c, c), 0)
    cj = lax.broadcasted_iota(jnp.int32, (c, c), 1)
    tril = ri >= cj
    strict = ri > cj
    gc = _mm01(tril, g_f)
    gct = gc.T
    z = z_ref[...]
    nw = nw_ref[...]

    heads = range(GDN_HEADS)
    sls = [slice(h * GDN_HD, (h + 1) * GDN_HD) for h in heads]
    q_h = [conv[:, sl] for sl in sls]
    k_h = [conv[:, GDN_W + h * GDN_HD:GDN_W + (h + 1) * GDN_HD] for h in heads]
    v_h = [conv[:, 2 * GDN_W + h * GDN_HD:2 * GDN_W + (h + 1) * GDN_HD] for h in heads]
    q_h = [q * lax.rsqrt(jnp.sum(q * q, axis=-1, keepdims=True) + EPS) * (GDN_HD ** -0.5) for q in q_h]
    k_h = [k * lax.rsqrt(jnp.sum(k * k, axis=-1, keepdims=True) + EPS) for k in k_h]
    g_col = [gc[:, 8 + h:9 + h] for h in heads]
    eg = [jnp.exp(g) for g in g_col]
    b_col = [beta_f[:, h:h + 1] for h in heads]
    decay = [jnp.where(tril, jnp.exp(jnp.where(tril, g_col[h] - gct[8 + h:9 + h, :], 0.0)), 0.0) for h in heads]
    kb = [k_h[h] * b_col[h] for h in heads]
    vb = [v_h[h] * b_col[h] for h in heads]
    m_h = [jnp.where(strict, _mm3(kb[h], k_h[h], NT) * decay[h], 0.0) for h in heads]
    qk = [jnp.where(tril, _mm(q_h[h], k_h[h], NT) * decay[h], 0.0) for h in heads]
    t_inv = _tri_inverse(m_h, c)
    u_h = [_mm3(t_inv[h], vb[h]) for h in heads]
    w_h = [_mm3(t_inv[h], kb[h] * eg[h]) for h in heads]
    s_h = [s_ref[h] for h in heads]
    v_new = [u_h[h] - _mm(w_h[h], s_h[h]) for h in heads]
    o1 = [_mm(q_h[h] * eg[h], s_h[h]) for h in heads]
    o_h = [o1[h] + _mm(qk[h], v_new[h]) for h in heads]
    g_last = [g[c - 1:c, :] for g in g_col]
    ds = [_mm(k_h[h] * jnp.exp(g_last[h] - g_col[h]), v_new[h], TN) for h in heads]
    for h in heads:
        s_ref[h] = s_h[h] * jnp.exp(g_last[h]) + ds[h]
        o_n = o_h[h] * lax.rsqrt(jnp.mean(o_h[h] * o_h[h], axis=-1, keepdims=True) + EPS) * nw
        o_ref[:, sls[h]] = o_n * _silu(z[:, sls[h]])

    @pl.when(ci == n_chunks - 1)
    def _():
        sfin_ref[0] = s_ref[...]


def gdn_mix(qkv, z, ba, cs, s0, conv_w8, hp, nw, *, c, valid):
    b = s0.shape[0]
    rows = qkv.shape[0]
    n_chunks = rows // (b * c)
    const = lambda s: pl.BlockSpec(s, lambda bi, ci: tuple(0 for _ in s))
    row = lambda w: pl.BlockSpec((c, w), lambda bi, ci: (bi * n_chunks + ci, 0))
    return pl.pallas_call(
        functools.partial(_gdn_body, c=c, valid=valid, n_chunks=n_chunks),
        grid=(b, n_chunks),
        in_specs=[row(3072), row(1024), row(128),
                  pl.BlockSpec((1, 8, 3072), lambda bi, ci: (bi, 0, 0)),
                  pl.BlockSpec((1, GDN_HEADS, 128, 128), lambda bi, ci: (bi, 0, 0, 0)),
                  const((8, 3072)), const((8, 128)), const((1, 128))],
        out_specs=[row(1024), pl.BlockSpec((1, GDN_HEADS, 128, 128), lambda bi, ci: (bi, 0, 0, 0))],
        out_shape=[jax.ShapeDtypeStruct((rows, 1024), F32),
                   jax.ShapeDtypeStruct((b, GDN_HEADS, 128, 128), F32)],
        scratch_shapes=[pltpu.VMEM((8 + c, 3072), F32), pltpu.VMEM((GDN_HEADS, 128, 128), F32)],
        compiler_params=_cparams(("arbitrary", "arbitrary")),
        name="gdn_mix",
    )(qkv, z, ba, cs, s0, conv_w8, hp, nw)


def _router_body(x_ref, nw_ref, sc_ref, sh_ref, wr_ref, br_ref, h_ref, gate_ref):
    h = _norm_mod(x_ref[...], nw_ref[...], sc_ref[...], sh_ref[...])
    h_ref[...] = h.astype(BF16)
    logits = _mmh(h, wr_ref[...]) + br_ref[...]
    tm = logits.shape[0]
    lane = lax.broadcasted_iota(jnp.int32, (tm, LANE), 1)
    is_grp = (lane >= N_EXPERTS) & (lane < N_EXPERTS + N_GROUPS)
    gl = jnp.where(is_grp, logits, NEG)
    gmax = jnp.max(gl, axis=-1, keepdims=True)
    g_idx = jnp.min(jnp.where(gl == gmax, lane, 4 * LANE), axis=-1, keepdims=True) - N_EXPERTS
    g_w = 1.0 / jnp.sum(jnp.where(is_grp, jnp.exp(gl - gmax), 0.0), axis=-1, keepdims=True)
    in_grp = (lane < N_EXPERTS) & (lane // EXP_PER_GROUP == g_idx)
    el = jnp.where(in_grp, logits, NEG)
    emax = jnp.max(el, axis=-1, keepdims=True)
    e = jnp.where(in_grp, jnp.exp(el - emax), 0.0)
    p = e / jnp.sum(e, axis=-1, keepdims=True)
    p1 = jnp.where(in_grp, p, -1.0)
    m1 = jnp.max(p1, axis=-1, keepdims=True)
    i1 = jnp.min(jnp.where(p1 == m1, lane, 4 * LANE), axis=-1, keepdims=True)
    p2 = jnp.where(lane == i1, -1.0, p1)
    m2 = jnp.max(p2, axis=-1, keepdims=True)
    i2 = jnp.min(jnp.where(p2 == m2, lane, 4 * LANE), axis=-1, keepdims=True)
    tot = m1 + m2
    gate = jnp.where(lane == i1, m1 / tot * g_w, jnp.where(lane == i2, m2 / tot * g_w, 0.0))
    gate_ref[...] = jnp.where(lane == GRP_LANE, g_idx.astype(F32), gate)


def moe_router(x, nw, sc, sh, w_r, b_r, tm):
    n, d = x.shape
    rows_mod = sc.shape[0]
    return pl.pallas_call(
        _router_body,
        grid=(n // tm,),
        in_specs=[pl.BlockSpec((tm, d), lambda i: (i, 0)), pl.BlockSpec((1, d), lambda i: (0, 0)),
                  _mod_spec(rows_mod, tm, d), _mod_spec(rows_mod, tm, d),
                  pl.BlockSpec((d, LANE), lambda i: (0, 0)), pl.BlockSpec((1, LANE), lambda i: (0, 0))],
        out_specs=[pl.BlockSpec((tm, d), lambda i: (i, 0)), pl.BlockSpec((tm, LANE), lambda i: (i, 0))],
        out_shape=[jax.ShapeDtypeStruct((n, d), BF16), jax.ShapeDtypeStruct((n, LANE), F32)],
        compiler_params=_cparams(("arbitrary",)),
        name="moe_router",
    )(x, nw, sc, sh, w_r, b_r)


def _moe_body(h_ref, gate_ref, w1_ref, w3_ref, w2_ref, x_ref, g2_ref, o_ref, acc_ref):
    e = pl.program_id(1)

    @pl.when(e == 0)
    def _():
        acc_ref[...] = jnp.zeros_like(acc_ref)

    hb = h_ref[...]
    he = _silu(_mm(hb, w1_ref[0])) * _mm(hb, w3_ref[0])
    y = _mm(he, w2_ref[0])
    gate = gate_ref[...]
    lane = lax.broadcasted_iota(jnp.int32, gate.shape, 1)
    ge = jnp.sum(jnp.where(lane == e, gate, 0.0), axis=-1, keepdims=True)
    acc_ref[...] += ge * y

    @pl.when(e == pl.num_programs(1) - 1)
    def _():
        o_ref[...] = x_ref[...] + g2_ref[...] * acc_ref[...]


def moe_ffn(h, gate, w1, w3, w2, e0, x, g2, tm):
    n, d = x.shape
    de = w1.shape[2]
    return pl.pallas_call(
        _moe_body,
        grid=(n // tm, N_EXPERTS),
        in_specs=[pl.BlockSpec((tm, d), lambda i, e: (i, 0)), pl.BlockSpec((tm, LANE), lambda i, e: (i, 0)),
                  pl.BlockSpec((1, d, de), lambda i, e: (e0 + e, 0, 0)),
                  pl.BlockSpec((1, d, de), lambda i, e: (e0 + e, 0, 0)),
                  pl.BlockSpec((1, de, d), lambda i, e: (e0 + e, 0, 0)),
                  pl.BlockSpec((tm, d), lambda i, e: (i, 0)),
                  pl.BlockSpec((1, d), lambda i, e: (0, 0)) if g2.shape[0] == 1
                  else pl.BlockSpec((tm, d), lambda i, e: (i, 0))],
        out_specs=pl.BlockSpec((tm, d), lambda i, e: (i, 0)),
        out_shape=jax.ShapeDtypeStruct((n, d), F32),
        scratch_shapes=[pltpu.VMEM((tm, d), F32)],
        compiler_params=_cparams(("arbitrary", "arbitrary")),
        name="moe_ffn",
    )(h, gate, w1, w3, w2, x, g2)


def _moe_plan(grp, tm, tb, cap, max_entries):
    nt = grp.shape[0] // tm
    cnt = jax.nn.one_hot(grp, N_GROUPS, dtype=jnp.int32).reshape(nt, tm, N_GROUPS).sum(axis=1)
    pc = (cnt + ROW_ALIGN - 1) // ROW_ALIGN * ROW_ALIGN
    segb = jnp.cumsum(pc, axis=1) - pc
    off = jnp.cumsum(pc, axis=0) - pc
    tot = pc.sum(axis=0)
    n_real = (tot + tb - 1) // tb
    n_all = jnp.minimum((tot + tm + tb - 1) // tb, cap // tb)
    ends = jnp.cumsum(n_all)
    s = jnp.arange(max_entries)
    g_of = jnp.sum(s[:, None] >= ends[None, :], axis=1)
    active = g_of < N_GROUPS
    g_c = jnp.minimum(g_of, N_GROUPS - 1)
    rt = s - (ends - n_all)[g_c]
    kind = jnp.where(active, jnp.where(rt < n_real[g_c], 1, 2), 0)
    last = ends[-1] - 1
    e_grp = jnp.where(active, g_c, g_c[last])
    e_rt = jnp.where(active, rt, rt[last])
    i32 = lambda a: a.reshape(-1).astype(jnp.int32)
    return i32(segb), i32(off // ROW_ALIGN), i32(e_grp), i32(e_rt), i32(kind)


def _group_perm(gate, segb_ref, base, tm, rows):
    gt = gate.T
    grp = gt[GRP_LANE:GRP_LANE + 1, :]
    gi = lax.broadcasted_iota(jnp.int32, (8, tm), 0).astype(F32)
    oh = jnp.where(gi == grp, 1.0, 0.0)
    r_i = lax.broadcasted_iota(jnp.int32, (tm, tm), 0)
    c_i = lax.broadcasted_iota(jnp.int32, (tm, tm), 1)
    before = jnp.where(r_i < c_i, 1.0, 0.0).astype(BF16)
    rank = lax.dot_general(oh.astype(BF16), before, NN, preferred_element_type=F32)
    dest = jnp.zeros((1, tm), F32)
    for g in range(N_GROUPS):
        dest = dest + oh[g:g + 1] * (segb_ref[base + g].astype(F32) + rank[g:g + 1])
    rows_i = lax.broadcasted_iota(jnp.int32, (rows, tm), 0).astype(F32)
    return jnp.where(rows_i == dest, 1.0, 0.0).astype(BF16)


def _moe_dispatch_body(segb_ref, off_ref, h_ref, gate_ref, xg_in, gg_in, xg_ref, gg_ref, xs_ref, gs_ref, *, tm, rows):
    del off_ref, xg_in, gg_in
    i = pl.program_id(0)
    g = pl.program_id(1)

    @pl.when((i == 0) & (g == 0))
    def _():
        xs_ref[...] = jnp.zeros_like(xs_ref)
        gs_ref[...] = jnp.zeros_like(gs_ref)

    @pl.when(g == 0)
    def _():
        gate = gate_ref[...]
        p = _group_perm(gate, segb_ref, i * N_GROUPS, tm, rows)
        xs_ref[0:rows, :] = lax.dot_general(p, h_ref[...], NN, preferred_element_type=F32).astype(BF16)
        gs_ref[0:rows, :] = _mm01(p, gate)

    start = pl.multiple_of(segb_ref[i * N_GROUPS + g], ROW_ALIGN)
    xg_ref[...] = xs_ref[pl.ds(start, tm), :]
    gg_ref[...] = gs_ref[pl.ds(start, tm), :]


def moe_dispatch(h, gate, segb, off, tm, cap):
    n, d = h.shape
    rows = tm + N_GROUPS * ROW_ALIGN
    win = lambda w: pl.BlockSpec((pl.Element(tm), pl.Element(w)),
                                 lambda i, g, segb, off: ((g * (cap // ROW_ALIGN) + off[i * N_GROUPS + g]) * ROW_ALIGN, 0))
    grid_spec = pltpu.PrefetchScalarGridSpec(
        num_scalar_prefetch=2,
        grid=(n // tm, N_GROUPS),
        in_specs=[pl.BlockSpec((tm, d), lambda i, g, *_: (i, 0)), pl.BlockSpec((tm, LANE), lambda i, g, *_: (i, 0)),
                  pl.BlockSpec(memory_space=pl.ANY), pl.BlockSpec(memory_space=pl.ANY)],
        out_specs=[win(d), win(LANE)],
        scratch_shapes=[pltpu.VMEM((2 * tm + N_GROUPS * ROW_ALIGN, d), BF16),
                        pltpu.VMEM((2 * tm + N_GROUPS * ROW_ALIGN, LANE), F32)],
    )
    return pl.pallas_call(
        functools.partial(_moe_dispatch_body, tm=tm, rows=rows),
        grid_spec=grid_spec,
        out_shape=[jax.ShapeDtypeStruct((N_GROUPS * cap, d), BF16), jax.ShapeDtypeStruct((N_GROUPS * cap, LANE), F32)],
        input_output_aliases={4: 0, 5: 1},
        compiler_params=_cparams(("arbitrary", "arbitrary")),
        name="moe_dispatch",
    )(segb, off, h, gate, jnp.zeros((N_GROUPS * cap, d), BF16), jnp.zeros((N_GROUPS * cap, LANE), F32))


def _moe_group_body(grp_ref, rt_ref, kind_ref, xg_ref, gg_ref, w1_ref, w3_ref, w2_ref, yg_ref, acc_ref):
    del rt_ref
    s = pl.program_id(0)
    e = pl.program_id(1)
    kind = kind_ref[s]
    last = e == pl.num_programs(1) - 1

    @pl.when(kind == 1)
    def _():
        @pl.when(e == 0)
        def _():
            acc_ref[...] = jnp.zeros_like(acc_ref)

        xb = xg_ref[...]
        he = _silu(_mm(xb, w1_ref[0])) * _mm(xb, w3_ref[0])
        y = _mm(he, w2_ref[0])
        gate = gg_ref[...]
        lane = lax.broadcasted_iota(jnp.int32, gate.shape, 1)
        ge = jnp.sum(jnp.where(lane == grp_ref[s] * EXP_PER_GROUP + e, gate, 0.0), axis=-1, keepdims=True)
        acc_ref[...] += ge * y

        @pl.when(last)
        def _():
            yg_ref[...] = acc_ref[...]

    @pl.when((kind == 2) & last)
    def _():
        yg_ref[...] = jnp.zeros_like(yg_ref)


def moe_group_ffn(e_grp, e_rt, e_kind, xg, gg, w1, w3, w2, e0, tb, cap):
    d = xg.shape[1]
    de = w1.shape[2]
    row = lambda s, e, grp, rt, kind: (grp[s] * (cap // tb) + rt[s], 0)
    wsel = lambda s, e, grp, rt, kind: (e0 + grp[s] * EXP_PER_GROUP + jnp.where(kind[s] == 1, e, EXP_PER_GROUP - 1),
                                        0, 0)
    grid_spec = pltpu.PrefetchScalarGridSpec(
        num_scalar_prefetch=3,
        grid=(e_grp.shape[0], EXP_PER_GROUP),
        in_specs=[pl.BlockSpec((tb, d), row), pl.BlockSpec((tb, LANE), row),
                  pl.BlockSpec((1, d, de), wsel), pl.BlockSpec((1, d, de), wsel), pl.BlockSpec((1, de, d), wsel)],
        out_specs=pl.BlockSpec((tb, d), row),
        scratch_shapes=[pltpu.VMEM((tb, d), F32)],
    )
    return pl.pallas_call(
        _moe_group_body,
        grid_spec=grid_spec,
        out_shape=jax.ShapeDtypeStruct((N_GROUPS * cap, d), F32),
        compiler_params=_cparams(("arbitrary", "arbitrary")),
        name="moe_group_ffn",
    )(e_grp, e_rt, e_kind, xg, gg, w1, w3, w2)


def _moe_combine_body(segb_ref, off_ref, yg_ref, gate_ref, x_ref, g2_ref, o_ref, ys_ref, *, tm, rows):
    del off_ref
    i = pl.program_id(0)
    g = pl.program_id(1)

    @pl.when((i == 0) & (g == 0))
    def _():
        ys_ref[...] = jnp.zeros_like(ys_ref)

    start = pl.multiple_of(segb_ref[i * N_GROUPS + g], ROW_ALIGN)
    ys_ref[pl.ds(start, tm), :] = yg_ref[...]

    @pl.when(g == N_GROUPS - 1)
    def _():
        p = _group_perm(gate_ref[...], segb_ref, i * N_GROUPS, tm, rows)
        yh, yl = _split(ys_ref[0:rows, :])
        y = (lax.dot_general(p, yh, TN, preferred_element_type=F32)
             + lax.dot_general(p, yl, TN, preferred_element_type=F32))
        o_ref[...] = x_ref[...] + g2_ref[...] * y


def moe_combine(yg, gate, x, g2, segb, off, tm, cap):
    n, d = x.shape
    rows = tm + N_GROUPS * ROW_ALIGN
    grid_spec = pltpu.PrefetchScalarGridSpec(
        num_scalar_prefetch=2,
        grid=(n // tm, N_GROUPS),
        in_specs=[pl.BlockSpec((pl.Element(tm), pl.Element(d)),
                               lambda i, g, segb, off: ((g * (cap // ROW_ALIGN) + off[i * N_GROUPS + g]) * ROW_ALIGN, 0)),
                  pl.BlockSpec((tm, LANE), lambda i, g, *_: (i, 0)),
                  pl.BlockSpec((tm, d), lambda i, g, *_: (i, 0)),
                  pl.BlockSpec((1, d), lambda i, g, *_: (0, 0))],
        out_specs=pl.BlockSpec((tm, d), lambda i, g, *_: (i, 0)),
        scratch_shapes=[pltpu.VMEM((2 * tm + N_GROUPS * ROW_ALIGN, d), F32)],
    )
    return pl.pallas_call(
        functools.partial(_moe_combine_body, tm=tm, rows=rows),
        grid_spec=grid_spec,
        out_shape=jax.ShapeDtypeStruct((n, d), F32),
        compiler_params=_cparams(("arbitrary", "arbitrary")),
        name="moe_combine",
    )(segb, off, yg, gate, x, g2)


def moe_grouped(h, gate, w1, w3, w2, e0, x, g2, tm):
    n = h.shape[0]
    tb = tm
    cap = n + 2 * tm
    max_entries = (n + (n // tm) * N_GROUPS * (ROW_ALIGN - 1) + N_GROUPS * tm) // tb + N_GROUPS + 1
    grp = gate[:, GRP_LANE].astype(jnp.int32)
    segb, off, e_grp, e_rt, e_kind = _moe_plan(grp, tm, tb, cap, max_entries)
    xg, gg = moe_dispatch(h, gate, segb, off, tm, cap)
    yg = moe_group_ffn(e_grp, e_rt, e_kind, xg, gg, w1, w3, w2, e0, tb, cap)
    return moe_combine(yg, gate, x, g2, segb, off, tm, cap)


def _final_norm_body(x_ref, w_ref, o_ref):
    x = x_ref[...]
    o_ref[...] = x * lax.rsqrt(jnp.mean(x * x, axis=-1, keepdims=True) + EPS) * w_ref[...]


def final_norm(x, w, tm):
    n, d = x.shape
    return pl.pallas_call(
        _final_norm_body,
        grid=(n // tm,),
        in_specs=[pl.BlockSpec((tm, d), lambda i: (i, 0)), pl.BlockSpec((1, d), lambda i: (0, 0))],
        out_specs=pl.BlockSpec((tm, d), lambda i: (i, 0)),
        out_shape=jax.ShapeDtypeStruct((n, d), F32),
        compiler_params=_cparams(("arbitrary",)),
        name="final_norm",
    )(x, w)


def _row_tile(n, pref):
    t = min(pref, n)
    while n % t:
        t //= 2
    return t


def kernel(x_prompt, x_sample, c_prompt, c_sample, cache_nsa_cmp, cache_nsa_sel, page_table, state_nsa_win, state_rwkv, state_rwkv_shift, state_gdn, state_gdn_conv, norm_mix, norm_ffn, norm_final, w_ada, b_ada, even_w_in, even_w_out, nsa_cmp_pos, nsa_cmp_w, rwkv_mu, rwkv_w0, rwkv_w2, rwkv_a0, rwkv_a2, rwkv_g2, rwkv_kk, rwkv_ka, rwkv_rk, rwkv_ln_w, rwkv_ln_b, odd_w_in, odd_w_out, gdn_conv_w, gdn_a_log, gdn_dt_bias, gdn_norm_w, moe_w_grp, moe_b_grp, moe_w_exp, moe_b_exp, moe_w1, moe_w3, moe_w2):
    bp, t, d = x_prompt.shape
    bs, ts, _ = x_sample.shape
    assert bp == 1 and ts <= SPAD and ts < CMP_BLK
    depth = norm_mix.shape[0]
    n_pages, page = page_table.shape[1], cache_nsa_cmp.shape[2]
    past = n_pages * page
    wb = state_nsa_win.shape[2]
    ns = bs * SPAD
    tq, tq_s, tk = 128, 32, 512
    tm_p = _row_tile(t, 512)
    tm_s = ns

    rows_c = -(-(1 + bs) // 8) * 8
    c_all = jnp.concatenate([c_prompt, c_sample, jnp.zeros((rows_c - 1 - bs, d), F32)], axis=0)
    ada = adaln(c_all, w_ada, b_ada)

    def mods(i):
        mp = [ada[i, 0:1, j * d:(j + 1) * d] for j in range(6)]
        ms = [jnp.repeat(ada[i, 1:1 + bs, j * d:(j + 1) * d], SPAD, axis=0) for j in range(6)]
        return mp, ms

    xp = x_prompt[0]
    xs = jnp.pad(x_sample, ((0, 0), (0, SPAD - ts), (0, 0))).reshape(ns, d)

    def unpad(a):
        return a.reshape(bs, SPAD, -1)[:, :ts]

    w1_all, w3_all, w2_all = (w.reshape((-1,) + w.shape[2:]) for w in (moe_w1, moe_w3, moe_w2))
    outs = {k: [] for k in ("cmp_p", "cmp_s", "sel_p", "sel_s", "win_p", "win_s", "rw_p", "rw_s", "sh_p", "sh_s",
                            "gd_p", "gd_s", "cv_p", "cv_s")}

    for i in range(depth):
        (sh1p, sc1p, gt1p, sh2p, sc2p, gt2p), (sh1s, sc1s, gt1s, sh2s, sc2s, gt2s) = mods(i)
        j = i // 2
        nw = norm_mix[i][None, :]
        if i % 2 == 0:
            w_packed = _pack_even_w(even_w_in[j])
            mu = _pack_rw_vec(rwkv_mu[j])
            wts, wc = _cmp_weights(nsa_cmp_pos[j], nsa_cmp_w[j])
            vec = jnp.stack([rwkv_w0[j], rwkv_a0[j], rwkv_kk[j], rwkv_ka[j], rwkv_ln_w[j], rwkv_ln_b[j],
                             jnp.zeros_like(rwkv_w0[j]), jnp.zeros_like(rwkv_w0[j])])
            pad_lora = lambda w: jnp.concatenate([w, jnp.zeros((128 - w.shape[0], w.shape[1]), w.dtype)], axis=0)
            w2p, a2p, g2p = pad_lora(rwkv_w2[j]), pad_lora(rwkv_a2[j]), rwkv_g2[j]
            hid = jnp.arange(RWKV_W) // RWKV_HD
            seg = (hid[:, None] == hid[None, :]).astype(F32)[:RWKV_W // 2, :RWKV_W // 2]
            rk = rwkv_rk[j].reshape(1, RWKV_W)
            wo_nsa, wo_rw = even_w_out[j][:512].astype(BF16), even_w_out[j][512:].astype(BF16)

            kv, qt, gt, ks, vst, kw, vwt, rw, hl = even_proj(xp, nw, sc1p, sh1p, w_packed, tm_p, 8)
            kvc = compress_prompt(kv, wts, wc, _row_tile(t, 512))
            gates = gt[:24].reshape(NSA_KV_HEADS, 12, t)
            gates = jnp.pad(gates, ((0, 0), (0, 4), (0, 0)))[None]
            o_nsa = nsa_attention(
                qt[None], gates, kvc[None], kvc.T[None], ks[None], vst[None], kw[None], vwt[None],
                tq=tq, tk=tk, wk=WINDOW + tq,
                pos0_fn=lambda qi: qi * tq,
                wstart_fn=lambda qi: jnp.maximum(qi * tq - WINDOW, 0),
                wpos0_fn=lambda qi: jnp.maximum(qi * tq - WINDOW, 0))[0]
            o_rw, s_rw = rwkv_mix(rw, jnp.zeros((1, 8, RW_COLS), F32), jnp.zeros((1, RWKV_HEADS, 64, 64), F32),
                                  mu, vec, w2p, a2p, g2p, seg, rk, c=64, valid=64)
            xp = out_proj([o_nsa, o_rw], [wo_nsa, wo_rw], xp, gt1p, tm_p)
            outs["cmp_p"].append(kv[:, 0:256].reshape(1, t, 2, 2, 64))
            outs["sel_p"].append(kv[:, 256:512].reshape(1, t, 2, 2, 64))
            kvw_rows = kv[:, 512:768].reshape(1, t, 2, 2, 64)
            outs["win_p"].append(kvw_rows[:, -min(WINDOW, t):])
            outs["rw_p"].append(s_rw)
            outs["sh_p"].append(hl[-1:])

            kv, qt, gt, _, _, _, _, rw, hl = even_proj(xs, nw, sc1s, sh1s, w_packed, tm_s, ns)
            kv_new = unpad(kv)
            rw0 = small_matmul(jnp.pad(state_rwkv_shift[j], ((0, -bs % 8), (0, 0))), w_packed[:, E_RW:])[:bs]
            rw0 = jnp.pad(rw0[:, None, :], ((0, 0), (7, 0), (0, 0)))
            pool_cmp = cache_nsa_cmp[j].transpose(0, 2, 3, 4, 1).reshape(-1, 256, page)
            pool_sel = cache_nsa_sel[j].transpose(0, 2, 3, 4, 1).reshape(-1, 256, page)
            kvc_s = compress_paged(pool_cmp, page_table, nsa_cmp_pos[j], wc, math.gcd(n_pages, 16))
            tail = jnp.pad(kv_new[:, :, 256:512], ((0, 0), (0, tk - ts), (0, 0)))
            wbuf = state_nsa_win[j].reshape(bs, wb, 256)
            kvw_all = jnp.concatenate([wbuf, kv_new[:, :, 512:768]], axis=1)
            wk_s = -(-(wb + ts) // 128) * 128
            kvw_pad = jnp.pad(kvw_all, ((0, 0), (0, wk_s - wb - ts), (0, 0)))
            kw_s = kvw_pad[:, :, :128].astype(BF16)
            vwt_s = jnp.swapaxes(kvw_pad[:, :, 128:], 1, 2).astype(BF16)
            qt_s = jnp.pad(qt.reshape(512, bs, SPAD).transpose(1, 0, 2), ((0, 0), (0, 0), (0, tq_s - SPAD)))
            g_s = gt[:24].reshape(NSA_KV_HEADS, 12, bs, SPAD).transpose(2, 0, 1, 3)
            g_s = jnp.pad(g_s, ((0, 0), (0, 0), (0, 4), (0, tq_s - SPAD)))
            o_nsa = nsa_attention_paged(
                qt_s, g_s, kvc_s, jnp.swapaxes(kvc_s, 1, 2), pool_sel, page_table, tail, kw_s, vwt_s,
                tq=tq_s, tk=tk, wk=wk_s,
                pos0_fn=lambda qi: past,
                wstart_fn=lambda qi: 0,
                wpos0_fn=lambda qi: past - wb)
            o_nsa = o_nsa[:, :SPAD].reshape(ns, 512)
            o_rw, s_rw = rwkv_mix(rw, rw0, state_rwkv[j], mu, vec, w2p, a2p, g2p, seg, rk, c=SPAD, valid=ts)
            xs = out_proj([o_nsa, o_rw], [wo_nsa, wo_rw], xs, gt1s, tm_s)
            outs["cmp_s"].append(kv_new[:, :, 0:256].reshape(bs, ts, 2, 2, 64))
            outs["sel_s"].append(kv_new[:, :, 256:512].reshape(bs, ts, 2, 2, 64))
            outs["win_s"].append(kvw_all[:, -wb:].reshape(bs, wb, 2, 2, 64))
            outs["rw_s"].append(s_rw)
            outs["sh_s"].append(hl.reshape(bs, SPAD, d)[:, ts - 1])
        else:
            w_in = odd_w_in[j]
            w_packed = jnp.concatenate([w_in, jnp.zeros((d, O_COLS - w_in.shape[1]), F32)], axis=1).astype(BF16)
            conv_w8 = jnp.pad(gdn_conv_w[j], ((0, 8 - CONV_W), (0, 0)))
            hp = jnp.zeros((8, 128), F32)
            hp = hp.at[0, 8:16].set(-jnp.exp(gdn_a_log[j])).at[1, 8:16].set(gdn_dt_bias[j])
            gnw = gdn_norm_w[j][None, :]
            wo = odd_w_out[j].astype(BF16)

            qkv, z, ba = odd_proj(xp, nw, sc1p, sh1p, w_packed, tm_p)
            o_g, s_g = gdn_mix(qkv, z, ba, jnp.zeros((1, 8, 3 * GDN_W), F32),
                               jnp.zeros((1, GDN_HEADS, GDN_HD, GDN_HD), F32), conv_w8, hp, gnw, c=64, valid=64)
            xp = out_proj([o_g], [wo], xp, gt1p, tm_p)
            outs["gd_p"].append(s_g)
            outs["cv_p"].append(qkv[None, -(CONV_W - 1):])

            qkv, z, ba = odd_proj(xs, nw, sc1s, sh1s, w_packed, tm_s)
            cs = jnp.pad(state_gdn_conv[j], ((0, 0), (8 - (CONV_W - 1), 0), (0, 0)))
            o_g, s_g = gdn_mix(qkv, z, ba, cs, state_gdn[j], conv_w8, hp, gnw, c=SPAD, valid=ts)
            xs = out_proj([o_g], [wo], xs, gt1s, tm_s)
            xpad = jnp.concatenate([state_gdn_conv[j], unpad(qkv)], axis=1)
            outs["gd_s"].append(s_g)
            outs["cv_s"].append(xpad[:, -(CONV_W - 1):])

        nwf = norm_ffn[i][None, :]
        w_r = jnp.concatenate([moe_w_exp[i], moe_w_grp[i], jnp.zeros((d, LANE - N_EXPERTS - N_GROUPS), F32)], axis=1)
        b_r = jnp.concatenate([moe_b_exp[i], moe_b_grp[i], jnp.zeros((LANE - N_EXPERTS - N_GROUPS,), F32)])[None, :]
        h2, gate = moe_router(xp, nwf, sc2p, sh2p, w_r, b_r, tm_p)
        xp = moe_grouped(h2, gate, w1_all, w3_all, w2_all, i * N_EXPERTS, xp, gt2p, _row_tile(t, 1024))
        h2, gate = moe_router(xs, nwf, sc2s, sh2s, w_r, b_r, tm_s)
        xs = moe_ffn(h2, gate, w1_all, w3_all, w2_all, i * N_EXPERTS, xs, gt2s, tm_s)

    nf = norm_final[None, :]
    y_prompt = final_norm(xp, nf, tm_p)[None]
    y_sample = unpad(final_norm(xs, nf, tm_s))
    st = lambda key: jnp.stack(outs[key])
    return (y_prompt, y_sample, st("cmp_p"), st("cmp_s"), st("sel_p"), st("sel_s"), st("win_p"), st("win_s"),
            st("rw_p"), st("rw_s"), st("sh_p"), st("sh_s"), st("gd_p"), st("gd_s"), st("cv_p"), st("cv_s"))
```

```python
import functools
import math

import jax
import jax.numpy as jnp
from jax import lax
from jax.experimental import pallas as pl
from jax.experimental.pallas import tpu as pltpu

F32 = jnp.float32
BF16 = jnp.bfloat16
HIGHEST = lax.Precision.HIGHEST

NSA_HEADS = 8
NSA_KV_HEADS = 2
NSA_GROUP = 4
NSA_HD = 64
CMP_BLK = 64
SEL_BLK = 64
TOPK_BLK = 16
WINDOW = 512
FORCE_BONUS = 2.0 * NSA_GROUP
RWKV_HEADS = 8
RWKV_HD = 64
RWKV_W = 512
RWKV_GN_EPS = 64e-5
GDN_HEADS = 8
GDN_HD = 128
GDN_W = 1024
CONV_W = 4
N_GROUPS = 4
EXP_PER_GROUP = 8
N_EXPERTS = 32
EPS = 1e-6
NEG = -1e30

LANE = 128
GRP_LANE = 64
ROW_ALIGN = 16
SAMPLE_TILE_SLOTS = 8
SPAD = 8
VMEM_LIMIT = 56 * 1024 * 1024

NN = (((1,), (0,)), ((), ()))
NT = (((1,), (1,)), ((), ()))
TN = (((0,), (0,)), ((), ()))

E_Q, E_KV, E_G, E_RW = 0, 512, 1280, 1408
E_COLS = 1408 + 1920
RW_COLS = 1920
O_COLS = 3072 + 1024 + 128


def _mm(a, b, dims=NN):
    return lax.dot_general(a.astype(BF16), b.astype(BF16), dims, preferred_element_type=F32)


def _mmh(a, b, dims=NN):
    return lax.dot_general(a.astype(F32), b.astype(F32), dims, precision=HIGHEST, preferred_element_type=F32)


def _split(a):
    hi = a.astype(BF16)
    return hi, (a - hi.astype(F32)).astype(BF16)


def _mm3(a, b, dims=NN):
    ah, al = _split(a)
    bh, bl = _split(b)
    d = lambda x, y: lax.dot_general(x, y, dims, preferred_element_type=F32)
    return d(ah, bh) + (d(ah, bl) + d(al, bh))


def _split3(x):
    h1 = x.astype(BF16)
    r1 = x - h1.astype(F32)
    h2 = r1.astype(BF16)
    return h1, h2, (r1 - h2.astype(F32)).astype(BF16)


def _mm01(m01, x):
    m = m01.astype(BF16)
    parts = _split3(x)
    d = lambda y: lax.dot_general(m, y, NN, preferred_element_type=F32)
    return d(parts[0]) + (d(parts[1]) + d(parts[2]))


def _head_sums(xs, seg_half):
    r = xs[0].shape[0]
    half = seg_half.shape[0]
    pieces = [p[:, h0:h0 + half] for x in xs for p in _split3(x) for h0 in (0, half)]
    out = lax.dot_general(jnp.concatenate(pieces, axis=0), seg_half.astype(BF16), NN, preferred_element_type=F32)
    res = []
    for i in range(len(xs)):
        o = [out[(6 * i + u) * r:(6 * i + u + 1) * r] for u in range(6)]
        res.append(jnp.concatenate([o[0] + (o[2] + o[4]), o[1] + (o[3] + o[5])], axis=1))
    return res


def _sigmoid(x):
    return 1.0 / (1.0 + jnp.exp(-x))


def _silu(x):
    return x * _sigmoid(x)


def _softplus(x):
    return jnp.maximum(x, 0.0) + jnp.log(1.0 + jnp.exp(-jnp.abs(x)))


def _cparams(sem):
    return pltpu.CompilerParams(dimension_semantics=sem, vmem_limit_bytes=VMEM_LIMIT)


def _norm_mod(x, nw, sc, sh):
    y = x * lax.rsqrt(jnp.mean(x * x, axis=-1, keepdims=True) + EPS)
    return (y * nw) * (1.0 + sc) + sh


def _mod_spec(rows_mod, tm, d):
    if rows_mod == 1:
        return pl.BlockSpec((1, d), lambda i: (0, 0))
    return pl.BlockSpec((tm, d), lambda i: (i, 0))


def _adaln_body(c_ref, w_ref, b_ref, o_ref):
    o_ref[0] = _mmh(_silu(c_ref[...]), w_ref[0]) + b_ref[0]


def adaln(c_all, w_ada, b_ada):
    depth, d, n6 = w_ada.shape
    rows = c_all.shape[0]
    tn = 768
    return pl.pallas_call(
        _adaln_body,
        grid=(depth, n6 // tn),
        in_specs=[pl.BlockSpec((rows, d), lambda l, j: (0, 0)),
                  pl.BlockSpec((1, d, tn), lambda l, j: (l, 0, j)),
                  pl.BlockSpec((1, 1, tn), lambda l, j: (l, 0, j))],
        out_specs=pl.BlockSpec((1, rows, tn), lambda l, j: (l, 0, j)),
        out_shape=jax.ShapeDtypeStruct((depth, rows, n6), F32),
        compiler_params=_cparams(("arbitrary", "arbitrary")),
        name="adaln",
    )(c_all, w_ada, b_ada.reshape(depth, 1, n6))


def _even_proj_body(x_ref, nw_ref, sc_ref, sh_ref, w_ref,
                    kv_ref, qt_ref, gt_ref, ks_ref, vst_ref, kw_ref, vwt_ref, rw_ref, hl_ref):
    h = _norm_mod(x_ref[...], nw_ref[...], sc_ref[...], sh_ref[...])
    hl = hl_ref.shape[0]
    hl_ref[...] = h[h.shape[0] - hl:, :]
    hb = h.astype(BF16)
    q = _mm(hb, w_ref[:, E_Q:E_Q + 512]) * (NSA_HD ** -0.5)
    qt_ref[...] = q.T.astype(BF16)
    kv = _mm(hb, w_ref[:, E_KV:E_KV + 768])
    kv_ref[...] = kv
    ks_ref[...] = kv[:, 256:384].astype(BF16)
    vst_ref[...] = kv[:, 384:512].T.astype(BF16)
    kw_ref[...] = kv[:, 512:640].astype(BF16)
    vwt_ref[...] = kv[:, 640:768].T.astype(BF16)
    g = _sigmoid(_mm(hb, w_ref[:, E_G:E_G + 128]))
    gt_ref[...] = g.T
    rw_ref[...] = _mm(hb, w_ref[:, E_RW:E_RW + RW_COLS])


def even_proj(x, nw, sc, sh, w_packed, tm, hl_rows):
    n, d = x.shape
    rows_mod = sc.shape[0]
    row = lambda c: pl.BlockSpec((tm, c), lambda i: (i, 0))
    col = lambda r: pl.BlockSpec((r, tm), lambda i: (0, i))
    return pl.pallas_call(
        _even_proj_body,
        grid=(n // tm,),
        in_specs=[row(d), pl.BlockSpec((1, d), lambda i: (0, 0)),
                  _mod_spec(rows_mod, tm, d), _mod_spec(rows_mod, tm, d),
                  pl.BlockSpec((d, E_COLS), lambda i: (0, 0))],
        out_specs=[row(768), col(512), col(128), row(128), col(128), row(128), col(128), row(RW_COLS),
                   pl.BlockSpec((hl_rows, d), lambda i: (0, 0))],
        out_shape=[jax.ShapeDtypeStruct((n, 768), F32),
                   jax.ShapeDtypeStruct((512, n), BF16),
                   jax.ShapeDtypeStruct((128, n), F32),
                   jax.ShapeDtypeStruct((n, 128), BF16),
                   jax.ShapeDtypeStruct((128, n), BF16),
                   jax.ShapeDtypeStruct((n, 128), BF16),
                   jax.ShapeDtypeStruct((128, n), BF16),
                   jax.ShapeDtypeStruct((n, RW_COLS), F32),
                   jax.ShapeDtypeStruct((hl_rows, d), F32)],
        compiler_params=_cparams(("arbitrary",)),
        name="even_proj",
    )(x, nw, sc, sh, w_packed)


def _pack_even_w(w_in):
    d = w_in.shape[0]
    z = lambda c: jnp.zeros((d, c), w_in.dtype)
    nsa = 1304
    rw = w_in[:, nsa:]
    parts = [w_in[:, :1280], w_in[:, 1280:1304], z(104),
             rw[:, :1536], rw[:, 1536:1600], z(64), rw[:, 1600:1664], z(64), rw[:, 1664:1792]]
    return jnp.concatenate(parts, axis=1).astype(BF16)


def _pack_rw_vec(v):
    z = jnp.zeros((64,), v.dtype)
    return jnp.concatenate([v[:1536], v[1536:1600], z, v[1600:1664], z, v[1664:1792]])[None, :]


def _mm_body(x_ref, w_ref, o_ref):
    o_ref[...] = _mm(x_ref[...], w_ref[...])


def small_matmul(x, w):
    return pl.pallas_call(
        _mm_body,
        out_shape=jax.ShapeDtypeStruct((x.shape[0], w.shape[1]), F32),
        compiler_params=pltpu.CompilerParams(vmem_limit_bytes=VMEM_LIMIT),
        name="small_matmul",
    )(x, w)


def _compress_body(x_ref, wts_ref, wc_ref, o_ref):
    x = x_ref[...]
    nb = x.shape[0] // CMP_BLK
    pooled = jnp.sum(x.reshape(nb, CMP_BLK, x.shape[-1]) * wts_ref[...][None], axis=1)
    o_ref[...] = _mm(pooled, wc_ref[...])


def _compress_paged_body(pt_ref, *refs, pps):
    page_refs = refs[:pps]
    wp_ref, wc_ref, o_ref = refs[pps:]
    x = jnp.concatenate([r[0] for r in page_refs], axis=1)
    pooled_t = jnp.concatenate([_mm(x[0:128], wp_ref[0]), _mm(x[128:256], wp_ref[1])], axis=0)
    nb = o_ref.shape[1]
    o_ref[0] = _mm(pooled_t.T[:nb], wc_ref[...])


def _cmp_weights(pos_wts, w_c):
    wts = jnp.repeat(pos_wts.T, 128, axis=1)
    eye2 = jnp.eye(2, dtype=w_c.dtype)
    blocks = [jnp.kron(eye2, w_c[c]) for c in range(2)]
    z = jnp.zeros((128, 128), w_c.dtype)
    wc = jnp.concatenate([jnp.concatenate([blocks[0], z], axis=1),
                          jnp.concatenate([z, blocks[1]], axis=1)], axis=0)
    return wts, wc


def compress_prompt(kv, wts, wc, tr):
    t = kv.shape[0]
    nb = tr // CMP_BLK
    return pl.pallas_call(
        _compress_body,
        grid=(t // tr,),
        in_specs=[pl.BlockSpec((tr, 256), lambda i: (i, 0)),
                  pl.BlockSpec((CMP_BLK, 256), lambda i: (0, 0)),
                  pl.BlockSpec((256, 256), lambda i: (0, 0))],
        out_specs=pl.BlockSpec((nb, 256), lambda i: (i, 0)),
        out_shape=jax.ShapeDtypeStruct((t // CMP_BLK, 256), F32),
        compiler_params=_cparams(("arbitrary",)),
        name="compress_prompt",
    )(kv, wts, wc)


def compress_paged(pool_t, page_table, pos_wts, wc, pages_per_step):
    b, n_pages = page_table.shape
    page = pool_t.shape[2]
    pps = pages_per_step
    nb = pps * page // CMP_BLK
    p_idx = jnp.arange(pps * page)
    wp = jax.nn.one_hot(p_idx // CMP_BLK, LANE, dtype=F32)[None] * pos_wts[:, p_idx % CMP_BLK][:, :, None]

    def page_spec(u):
        return pl.BlockSpec((1, 256, page), lambda bi, g, pt: (pt[bi, g * pps + u], 0, 0))

    grid_spec = pltpu.PrefetchScalarGridSpec(
        num_scalar_prefetch=1,
        grid=(b, n_pages // pps),
        in_specs=[page_spec(u) for u in range(pps)] + [
            pl.BlockSpec((2, pps * page, LANE), lambda bi, g, pt: (0, 0, 0)),
            pl.BlockSpec((256, 256), lambda bi, g, pt: (0, 0))],
        out_specs=pl.BlockSpec((1, nb, 256), lambda bi, g, pt: (bi, g, 0)),
    )
    return pl.pallas_call(
        functools.partial(_compress_paged_body, pps=pps),
        grid_spec=grid_spec,
        out_shape=jax.ShapeDtypeStruct((b, n_pages * page // CMP_BLK, 256), F32),
        compiler_params=_cparams(("arbitrary", "arbitrary")),
        name="compress_paged",
    )(page_table, *([pool_t] * pps), wp, wc)


def _gather_sel_body(pt_ref, tiles_ref, cnt_ref, *refs, pps, n_page_steps, nt):
    del pt_ref
    page_refs = refs[:pps]
    tail_ref, ks_ref, vst_ref = refs[pps:]
    bi = pl.program_id(0)
    a = pl.program_id(1)
    j = tiles_ref[bi * nt + jnp.minimum(a, cnt_ref[bi] - 1)]
    live = a < cnt_ref[bi]

    @pl.when(live & (j < n_page_steps))
    def _():
        ks_ref[0] = jnp.concatenate([r[0][0:128].T for r in page_refs], axis=0).astype(BF16)
        vst_ref[0] = jnp.concatenate([r[0][128:256] for r in page_refs], axis=1).astype(BF16)

    @pl.when(live & (j >= n_page_steps))
    def _():
        x = tail_ref[0]
        ks_ref[0] = x[:, :128].astype(BF16)
        vst_ref[0] = x[:, 128:].T.astype(BF16)


def gather_sel(pool_t, page_table, tail, tk, tiles, cnt, n_slots):
    b, n_pages = page_table.shape
    page = pool_t.shape[2]
    pps = tk // page
    n_page_steps = n_pages // pps
    nt = n_page_steps + 1
    nk = n_slots * tk

    def slot(bi, a, pt, tiles, cnt):
        return jnp.minimum(a, cnt[bi] - 1)

    def page_spec(u):
        def index(bi, a, pt, tiles, cnt):
            j = tiles[bi * nt + slot(bi, a, pt, tiles, cnt)]
            return (pt[bi, jnp.minimum(j * pps + u, n_pages - 1)], 0, 0)
        return pl.BlockSpec((1, 256, page), index)

    grid_spec = pltpu.PrefetchScalarGridSpec(
        num_scalar_prefetch=3,
        grid=(b, jnp.max(cnt)),
        in_specs=[page_spec(u) for u in range(pps)] + [pl.BlockSpec((1, tk, 256), lambda bi, a, *_: (bi, 0, 0))],
        out_specs=[pl.BlockSpec((1, tk, 128), lambda bi, a, *s: (bi, slot(bi, a, *s), 0)),
                   pl.BlockSpec((1, 128, tk), lambda bi, a, *s: (bi, 0, slot(bi, a, *s)))],
    )
    return pl.pallas_call(
        functools.partial(_gather_sel_body, pps=pps, n_page_steps=n_page_steps, nt=nt),
        grid_spec=grid_spec,
        out_shape=[jax.ShapeDtypeStruct((b, nk, 128), BF16), jax.ShapeDtypeStruct((b, 128, nk), BF16)],
        compiler_params=_cparams(("arbitrary", "arbitrary")),
        name="gather_sel",
    )(page_table, tiles, cnt, *([pool_t] * pps), tail)


MASKED = -1e30
M_INIT = -1e29


def _nsa_query(qt_ref, k, tq):
    w4 = NSA_GROUP * tq
    qb = qt_ref[0].astype(F32)
    qcat = jnp.concatenate([qb[g * 64:(g + 1) * 64] for g in range(NSA_GROUP)], axis=1)
    q2 = jnp.concatenate([qcat, qcat], axis=0)
    row = lax.broadcasted_iota(jnp.int32, (128, w4), 0)
    qe = jnp.where(row // 64 == k, q2, 0.0)
    gidx = lax.broadcasted_iota(jnp.int32, (128, w4), 1) // tq
    base = jnp.where(k == 0, 0.5, 0.5 / 16.0)
    slope = base * jnp.where(gidx == 0, 1.0, jnp.where(gidx == 1, 0.5, jnp.where(gidx == 2, 0.25, 0.125)))
    mult = jnp.where(row == 0, 16.0, jnp.where(row == 1, 1.0, jnp.where(row == 2, 128.0,
                                                                         jnp.where(row == 3, 64.0, 0.0))))
    return jnp.concatenate([qe, slope * mult], axis=0).astype(BF16)


def _pos_features(rows, tile_rel):
    r = lax.broadcasted_iota(jnp.int32, (rows, LANE), 0)
    lane = lax.broadcasted_iota(jnp.int32, (rows, LANE), 1)
    ab = jnp.where(lane == 0, r // 16, jnp.where(lane == 1, r % 16, 0)).astype(F32)
    return jnp.where(lane == 2, tile_rel, ab).astype(BF16)


def _gate_rows(gb, j, tq):
    return jnp.concatenate([gb[g * 3 + j:g * 3 + j + 1, :] for g in range(NSA_GROUP)], axis=1)


def _nsa_select_body(qt_ref, g_ref, kvc_ref, kvct_ref, kw_ref, vwt_ref, part_ref, sel_ref, flag_ref, *,
                     tq, tk, wk, nbc, nb, pos0_fn, wstart_fn, wpos0_fn):
    i = pl.program_id(1)
    k = pl.program_id(2)
    w4 = NSA_GROUP * tq
    pos0 = pos0_fn(i)
    qa = _nsa_query(qt_ref, k, tq)
    pos_q = pos0 + lax.broadcasted_iota(jnp.int32, (1, w4), 1) % tq

    def softmax_cols(s, bad):
        s = jnp.where(bad, MASKED, s)
        m = jnp.maximum(jnp.max(s, axis=0, keepdims=True), M_INIT)
        e = jnp.exp(s - m)
        return e / jnp.maximum(jnp.sum(e, axis=0, keepdims=True), 1e-30)

    n_i = lax.broadcasted_iota(jnp.int32, (nbc, LANE), 0)
    lane_c = lax.broadcasted_iota(jnp.int32, (nbc, LANE), 1)
    feat_c = jnp.where(lane_c == 3, n_i - pos0 // CMP_BLK, 0).astype(F32).astype(BF16)
    kc = jnp.concatenate([kvc_ref[0][:, :128].astype(BF16), feat_c], axis=1)
    c_end = lax.broadcasted_iota(jnp.int32, (nbc, 1), 0) * CMP_BLK + (CMP_BLK - 1)
    p_c = softmax_cols(lax.dot_general(kc, qa, NN, preferred_element_type=F32), c_end > pos_q)
    vct = kvct_ref[0, pl.ds(pl.multiple_of(128 + k * 64, 64), 64), :]
    o_c = _mm(vct, p_c)

    imp = p_c[:, 0:tq]
    for g in range(1, NSA_GROUP):
        imp = imp + p_c[:, g * tq:(g + 1) * tq]
    if nb > nbc:
        imp = jnp.concatenate([imp, jnp.zeros((nb - nbc, tq), F32)], axis=0)
    blk = lax.broadcasted_iota(jnp.int32, (nb, tq), 0)
    cur = (pos0 + lax.broadcasted_iota(jnp.int32, (1, tq), 1)) // SEL_BLK
    forced = (blk == cur) | (blk == cur - 1) | (blk == 0)
    score = jnp.where(blk <= cur, imp + jnp.where(forced, FORCE_BONUS, 0.0), -1.0)
    for _ in range(min(TOPK_BLK, nb)):
        m = jnp.max(score, axis=0, keepdims=True)
        first = jnp.min(jnp.where(score == m, blk, nb), axis=0, keepdims=True)
        score = jnp.where(blk == first, -2.0, score)
    sel = jnp.where(score == -2.0, 1.0, 0.0)
    sel_ref[0, 0] = sel
    bpt = tk // SEL_BLK
    any_row = jnp.max(sel, axis=1, keepdims=True)
    flag_ref[0, 0] = jnp.max(any_row.reshape(nb // bpt, bpt, 1), axis=1)

    wstart = wstart_fn(i)
    if not isinstance(wstart, int):
        wstart = pl.multiple_of(wstart, 128)
    wpos0 = wpos0_fn(i)
    tile_rel = jnp.asarray((wpos0 - pos0) // 128, F32)
    kw = jnp.concatenate([kw_ref[0, pl.ds(wstart, wk), :], _pos_features(wk, tile_rel)], axis=1)
    dist_w = pos_q - (wpos0 + lax.broadcasted_iota(jnp.int32, (wk, 1), 0))
    p_w = softmax_cols(lax.dot_general(kw, qa, NN, preferred_element_type=F32), (dist_w < 0) | (dist_w >= WINDOW))
    vwin = vwt_ref[0, pl.ds(pl.multiple_of(k * 64, 64), 64), pl.ds(wstart, wk)]
    o_w = _mm(vwin, p_w)

    gb = g_ref[0, 0]
    part_ref[0, 0] = _gate_rows(gb, 0, tq) * o_c + _gate_rows(gb, 2, tq) * o_w


def nsa_select(qt, gates, kvc, kvct, kw, vwt, *, nb, tq, tk, wk, pos0_fn, wstart_fn, wpos0_fn):
    b, _, nq = qt.shape
    nbc = kvc.shape[1]
    nw = kw.shape[1]
    nqt = nq // tq
    nt = nb * SEL_BLK // tk
    w4 = NSA_GROUP * tq
    assert nbc <= 256 and tk <= 512 and wk <= 1024
    body = functools.partial(_nsa_select_body, tq=tq, tk=tk, wk=wk, nbc=nbc, nb=nb, pos0_fn=pos0_fn,
                             wstart_fn=wstart_fn, wpos0_fn=wpos0_fn)
    full = lambda s1, s2: pl.BlockSpec((1, s1, s2), lambda bi, i, k: (bi, 0, 0))
    step = lambda s1, s2: pl.BlockSpec((1, 1, s1, s2), lambda bi, i, k: (bi, i * NSA_KV_HEADS + k, 0, 0))
    return pl.pallas_call(
        body,
        grid=(b, nqt, NSA_KV_HEADS),
        in_specs=[pl.BlockSpec((1, 256, tq), lambda bi, i, k: (bi, k, i)),
                  pl.BlockSpec((1, 1, 16, tq), lambda bi, i, k: (bi, k, 0, i)),
                  full(nbc, 256), full(256, nbc), full(nw, 128), full(128, nw)],
        out_specs=[step(64, w4), step(nb, tq), step(nt, 1)],
        out_shape=[jax.ShapeDtypeStruct((b, nqt * 2, 64, w4), F32),
                   jax.ShapeDtypeStruct((b, nqt * 2, nb, tq), F32),
                   jax.ShapeDtypeStruct((b, nqt * 2, nt, 1), F32)],
        compiler_params=_cparams(("arbitrary", "arbitrary", "arbitrary")),
        name="nsa_select",
    )(qt, gates, kvc, kvct, kw, vwt)


def _nsa_selected_body(list_ref, slot_ref, cnt_ref, qt_ref, g_ref, sel_ref, ks_ref, vst_ref, part_ref, o_ref, *,
                       tq, tk, nt, pos0_fn):
    bi = pl.program_id(0)
    i = pl.program_id(1)
    k = pl.program_id(2)
    step = (bi * pl.num_programs(1) + i) * NSA_KV_HEADS + k
    w4 = NSA_GROUP * tq
    pos0 = pos0_fn(i)
    qa = _nsa_query(qt_ref, k, tq)
    pos_q = pos0 + lax.broadcasted_iota(jnp.int32, (1, w4), 1) % tq
    bpt = tk // SEL_BLK
    row_k = lax.broadcasted_iota(jnp.int32, (tk, 1), 0)
    r = lax.broadcasted_iota(jnp.int32, (tk, LANE), 0)
    lane = lax.broadcasted_iota(jnp.int32, (tk, LANE), 1)
    feat_ab = jnp.where(lane == 0, r // 16, jnp.where(lane == 1, r % 16, 0)).astype(F32)

    n_act = cnt_ref[step]

    def tile_scores(jj, live):
        j = list_ref[step * nt + jj]
        off = pl.multiple_of(j * tk, tk)
        buf = pl.multiple_of(slot_ref[step * nt + jj] * tk, tk)
        tile_rel = ((off - pos0) // 128).astype(F32)
        feat = jnp.where(lane == 2, tile_rel, feat_ab).astype(BF16)
        kj = jnp.concatenate([ks_ref[0, pl.ds(buf, tk), :], feat], axis=1)
        s = lax.dot_general(kj, qa, NN, preferred_element_type=F32)
        selb = (sel_ref[0, 0, pl.ds(pl.multiple_of(j * bpt, bpt), bpt), :] - 1.0) * (-MASKED)
        selb = jnp.concatenate([selb] * NSA_GROUP, axis=1) + jnp.where(live, 0.0, MASKED)
        s = s + jnp.broadcast_to(selb[:, None, :], (bpt, SEL_BLK, w4)).reshape(tk, w4)
        s = jnp.where(row_k > pos_q - off, MASKED, s)
        return s, vst_ref[0, pl.ds(pl.multiple_of(k * 64, 64), 64), pl.ds(buf, tk)]

    def kv_pair(pp, carry):
        m_i, l_i, acc = carry
        second = 2 * pp + 1
        s_a, v_a = tile_scores(2 * pp, True)
        s_b, v_b = tile_scores(jnp.minimum(second, n_act - 1), second < n_act)
        m_new = jnp.maximum(m_i, jnp.maximum(jnp.max(s_a, axis=0, keepdims=True), jnp.max(s_b, axis=0, keepdims=True)))
        p_a = jnp.exp(s_a - m_new)
        p_b = jnp.exp(s_b - m_new)
        alpha = jnp.exp(m_i - m_new)
        l_new = l_i * alpha + (jnp.sum(p_a, axis=0, keepdims=True) + jnp.sum(p_b, axis=0, keepdims=True))
        return m_new, l_new, acc * alpha + (_mm(v_a, p_a) + _mm(v_b, p_b))

    init = (jnp.full((1, w4), M_INIT, F32), jnp.zeros((1, w4), F32), jnp.zeros((64, w4), F32))
    _, l_s, acc_s = lax.fori_loop(0, (n_act + 1) // 2, kv_pair, init)
    o_s = acc_s / jnp.maximum(l_s, 1e-30)
    o_t = part_ref[0, 0] + _gate_rows(g_ref[0, 0], 1, tq) * o_s
    o_ref[0] = jnp.concatenate([o_t[:, g * tq:(g + 1) * tq].T for g in range(NSA_GROUP)], axis=1)


def nsa_selected(tile_list, slot_list, tile_cnt, qt, gates, sel, ks, vst, part, *, tq, tk, pos0_fn):
    b, _, nq = qt.shape
    nk = ks.shape[1]
    nb = sel.shape[2]
    nt = nb * SEL_BLK // tk
    w4 = NSA_GROUP * tq
    full = lambda s1, s2: pl.BlockSpec((1, s1, s2), lambda bi, i, k, *_: (bi, 0, 0))
    step = lambda s1, s2: pl.BlockSpec((1, 1, s1, s2), lambda bi, i, k, *_: (bi, i * NSA_KV_HEADS + k, 0, 0))
    grid_spec = pltpu.PrefetchScalarGridSpec(
        num_scalar_prefetch=3,
        grid=(b, nq // tq, NSA_KV_HEADS),
        in_specs=[pl.BlockSpec((1, 256, tq), lambda bi, i, k, *_: (bi, k, i)),
                  pl.BlockSpec((1, 1, 16, tq), lambda bi, i, k, *_: (bi, k, 0, i)),
                  step(nb, tq), full(nk, 128), full(128, nk), step(64, w4)],
        out_specs=pl.BlockSpec((1, tq, 256), lambda bi, i, k, *_: (bi, i, k)),
    )
    return pl.pallas_call(
        functools.partial(_nsa_selected_body, tq=tq, tk=tk, nt=nt, pos0_fn=pos0_fn),
        grid_spec=grid_spec,
        out_shape=jax.ShapeDtypeStruct((b, nq, 512), F32),
        compiler_params=_cparams(("arbitrary", "arbitrary", "arbitrary")),
        name="nsa_selected",
    )(tile_list, slot_list, tile_cnt, qt, gates, sel, ks, vst, part)


def _active_first(active):
    order = jnp.argsort(jnp.where(active, 0, 1), axis=-1, stable=True).astype(jnp.int32)
    return order, jnp.sum(active, axis=-1).astype(jnp.int32)


def nsa_attention(qt, gates, kvc, kvct, ks, vst, kw, vwt, *, tq, tk, wk, pos0_fn, wstart_fn, wpos0_fn):
    nb = ks.shape[1] // SEL_BLK
    part, sel, flags = nsa_select(qt, gates, kvc, kvct, kw, vwt, nb=nb, tq=tq, tk=tk, wk=wk, pos0_fn=pos0_fn,
                                  wstart_fn=wstart_fn, wpos0_fn=wpos0_fn)
    order, cnt = _active_first(flags[..., 0] > 0.5)
    return nsa_selected(order.reshape(-1), order.reshape(-1), cnt.reshape(-1), qt, gates, sel, ks, vst, part,
                        tq=tq, tk=tk, pos0_fn=pos0_fn)


def nsa_attention_paged(qt, gates, kvc, kvct, pool_t, page_table, tail, kw, vwt, *, tq, tk, wk, pos0_fn, wstart_fn,
                        wpos0_fn):
    nb = (page_table.shape[1] * pool_t.shape[2] + tk) // SEL_BLK
    part, sel, flags = nsa_select(qt, gates, kvc, kvct, kw, vwt, nb=nb, tq=tq, tk=tk, wk=wk, pos0_fn=pos0_fn,
                                  wstart_fn=wstart_fn, wpos0_fn=wpos0_fn)
    active = flags[..., 0] > 0.5
    tiles_b, cnt_b = _active_first(jnp.any(active, axis=1))
    slot_of_tile = jnp.argsort(tiles_b, axis=-1).astype(jnp.int32)
    order, cnt = _active_first(active)
    slots = jnp.take_along_axis(jnp.broadcast_to(slot_of_tile[:, None, :], order.shape), order, axis=-1)

    def run(n_slots):
        ks, vst = gather_sel(pool_t, page_table, tail, tk, tiles_b.reshape(-1), cnt_b, n_slots)
        return nsa_selected(order.reshape(-1), slots.reshape(-1), cnt.reshape(-1), qt, gates, sel, ks, vst, part,
                            tq=tq, tk=tk, pos0_fn=pos0_fn)

    nt = tiles_b.shape[-1]
    few = min(SAMPLE_TILE_SLOTS, nt)
    return lax.cond(jnp.max(cnt_b) <= few, lambda: run(few), lambda: run(nt))


def _tri_inverse(ms, c):
    eye = (lax.broadcasted_iota(jnp.int32, (c, c), 0) == lax.broadcasted_iota(jnp.int32, (c, c), 1)).astype(F32)
    ps = [-m for m in ms]
    ts = [eye + p for p in ps]
    steps = max(int(math.ceil(math.log2(c))) - 1, 0)
    d = lambda x, y: lax.dot_general(x, y, NN, preferred_element_type=F32)
    for _ in range(steps):
        sp = [_split(p) for p in ps]
        ps = [d(ph, ph) + (d(ph, pl_) + d(pl_, ph)) for ph, pl_ in sp]
        sp = [_split(p) for p in ps]
        st = [_split(t) for t in ts]
        ts = [t + (d(th, ph) + (d(th, pl_) + d(tl, ph))) for t, (th, tl), (ph, pl_) in zip(ts, st, sp)]
    return ts


def _rwkv_body(rw_ref, rw0_ref, s0_ref, mu_ref, vec_ref, w2_ref, a2_ref, g2_ref, seg_ref, rk_ref,
               o_ref, sfin_ref, buf_ref, s_ref, y_ref, *, c, valid, n_chunks):
    ci = pl.program_id(1)
    halo = 8

    @pl.when(ci == 0)
    def _():
        buf_ref[0:halo, :] = rw0_ref[0]
        s_ref[...] = s0_ref[0]

    cur = rw_ref[...]
    buf_ref[halo:halo + c, :] = cur
    prev = buf_ref[halo - 1:halo - 1 + c, :]
    xr = cur + (prev - cur) * mu_ref[...]
    buf_ref[0:halo, :] = cur[c - halo:, :]

    vec = vec_ref[...]
    w0, a0, kkw, kaw, ln_w, ln_b = (vec[r:r + 1, :] for r in range(6))
    r = xr[:, 0:512]
    kx = xr[:, 512:1024]
    v = xr[:, 1024:1536]
    xw = xr[:, 1536:1664]
    xa = xr[:, 1664:1792]
    xg = xr[:, 1792:1920]
    wl = -jnp.exp(-_softplus(-(w0 + _mm(jnp.tanh(xw), w2_ref[...]))) - 0.5)
    a = _sigmoid(a0 + _mm(xa, a2_ref[...]))
    gate = _mm(_sigmoid(xg), g2_ref[...])
    seg = seg_ref[...]
    zk = kx * kkw
    k2 = kx * (1.0 + (a - 1.0) * kaw)
    zz_sum, rk_sum = _head_sums([zk * zk, r * k2 * rk_ref[...]], seg)
    kk = zk * lax.rsqrt(zz_sum + EPS)
    bonus = rk_sum * v
    if valid < c:
        live = lax.broadcasted_iota(jnp.int32, (c, 1), 0) < valid
        wl = jnp.where(live, wl, 0.0)
        kk = jnp.where(live, kk, 0.0)
        k2 = jnp.where(live, k2, 0.0)
        v = jnp.where(live, v, 0.0)
        r = jnp.where(live, r, 0.0)
    bb = kk * a

    ri = lax.broadcasted_iota(jnp.int32, (c, c), 0)
    cj = lax.broadcasted_iota(jnp.int32, (c, c), 1)
    tril = ri >= cj
    strict = ri > cj
    cw = _mm01(tril, wl)
    ecw = jnp.exp(cw)
    einv = jnp.exp(-cw)
    p_c = ecw[c - 1:c, :]
    kt = kk * jnp.exp(cw - wl)
    bt = bb * einv
    ki = k2 * einv
    rt = r * ecw
    bd = bt * p_c
    kd = ki * p_c

    heads = range(RWKV_HEADS)
    sls = [slice(h * RWKV_HD, (h + 1) * RWKV_HD) for h in heads]
    kt_h = [kt[:, sl] for sl in sls]
    bt_h = [bt[:, sl] for sl in sls]
    ki_h = [ki[:, sl] for sl in sls]
    rt_h = [rt[:, sl] for sl in sls]
    v_h = [v[:, sl] for sl in sls]
    l_m = [jnp.where(strict, _mm3(kt_h[h], bt_h[h], NT), 0.0) for h in heads]
    m_kk = [jnp.where(strict, _mm(kt_h[h], ki_h[h], NT), 0.0) for h in heads]
    a_rb = [jnp.where(tril, _mm(rt_h[h], bt_h[h], NT), 0.0) for h in heads]
    a_rk = [jnp.where(tril, _mm(rt_h[h], ki_h[h], NT), 0.0) for h in heads]
    mv = [_mm(m_kk[h], v_h[h]) for h in heads]
    y0 = [_mm(a_rk[h], v_h[h]) for h in heads]
    t_inv = _tri_inverse(l_m, c)
    w_h = [_mm3(t_inv[h], kt_h[h]) for h in heads]
    u_h = [-_mm3(t_inv[h], mv[h]) for h in heads]
    s_h = [s_ref[h] for h in heads]
    e_h = [u_h[h] - _mm(w_h[h], s_h[h], NT) for h in heads]
    y1 = [_mm(rt_h[h], s_h[h], NT) + y0[h] for h in heads]
    y_h = [y1[h] + _mm(a_rb[h], e_h[h]) for h in heads]
    ds = [_mm(e_h[h], bd[:, sls[h]], TN) + _mm(v_h[h], kd[:, sls[h]], TN) for h in heads]
    for h in heads:
        s_ref[h] = s_h[h] * p_c[:, sls[h]] + ds[h]
        mu_h = jnp.mean(y_h[h], axis=-1, keepdims=True)
        d_h = y_h[h] - mu_h
        var_h = jnp.mean(d_h * d_h, axis=-1, keepdims=True)
        y_ref[:, sls[h]] = d_h * lax.rsqrt(var_h + RWKV_GN_EPS)

    o_ref[...] = (y_ref[...] * ln_w + ln_b + bonus) * gate

    @pl.when(ci == n_chunks - 1)
    def _():
        sfin_ref[0] = s_ref[...]


def rwkv_mix(rw, rw0, s0, mu, vec, w2, a2, g2, seg, rk, *, c, valid):
    b = s0.shape[0]
    rows = rw.shape[0]
    n_chunks = rows // (b * c)
    const = lambda s: pl.BlockSpec(s, lambda bi, ci: tuple(0 for _ in s))
    return pl.pallas_call(
        functools.partial(_rwkv_body, c=c, valid=valid, n_chunks=n_chunks),
        grid=(b, n_chunks),
        in_specs=[pl.BlockSpec((c, RW_COLS), lambda bi, ci: (bi * n_chunks + ci, 0)),
                  pl.BlockSpec((1, 8, RW_COLS), lambda bi, ci: (bi, 0, 0)),
                  pl.BlockSpec((1, RWKV_HEADS, 64, 64), lambda bi, ci: (bi, 0, 0, 0)),
                  const((1, RW_COLS)), const((8, 512)), const((128, 512)), const((128, 512)), const((128, 512)),
                  const((RWKV_W // 2, RWKV_W // 2)), const((1, 512))],
        out_specs=[pl.BlockSpec((c, 512), lambda bi, ci: (bi * n_chunks + ci, 0)),
                   pl.BlockSpec((1, RWKV_HEADS, 64, 64), lambda bi, ci: (bi, 0, 0, 0))],
        out_shape=[jax.ShapeDtypeStruct((rows, 512), F32),
                   jax.ShapeDtypeStruct((b, RWKV_HEADS, 64, 64), F32)],
        scratch_shapes=[pltpu.VMEM((8 + c, RW_COLS), F32), pltpu.VMEM((RWKV_HEADS, 64, 64), F32),
                        pltpu.VMEM((c, 512), F32)],
        compiler_params=_cparams(("arbitrary", "arbitrary")),
        name="rwkv_mix",
    )(rw, rw0, s0, mu, vec, w2, a2, g2, seg, rk)


def _out_proj_body(*refs, n_in):
    a_refs = refs[:n_in]
    w_refs = refs[n_in:2 * n_in]
    x_ref, g_ref, o_ref = refs[2 * n_in:]
    y = _mm(a_refs[0][...], w_refs[0][...])
    for a_ref, w_ref in zip(a_refs[1:], w_refs[1:]):
        y = y + _mm(a_ref[...], w_ref[...])
    o_ref[...] = x_ref[...] + g_ref[...] * y


def out_proj(acts, weights, x, gate, tm):
    n, d = x.shape
    n_in = len(acts)
    return pl.pallas_call(
        functools.partial(_out_proj_body, n_in=n_in),
        grid=(n // tm,),
        in_specs=[pl.BlockSpec((tm, a.shape[1]), lambda i: (i, 0)) for a in acts]
        + [pl.BlockSpec(w.shape, lambda i: (0, 0)) for w in weights]
        + [pl.BlockSpec((tm, d), lambda i: (i, 0)), _mod_spec(gate.shape[0], tm, d)],
        out_specs=pl.BlockSpec((tm, d), lambda i: (i, 0)),
        out_shape=jax.ShapeDtypeStruct((n, d), F32),
        compiler_params=_cparams(("arbitrary",)),
        name="out_proj",
    )(*acts, *weights, x, gate)


def _odd_proj_body(x_ref, nw_ref, sc_ref, sh_ref, w_ref, qkv_ref, z_ref, ba_ref):
    hb = _norm_mod(x_ref[...], nw_ref[...], sc_ref[...], sh_ref[...]).astype(BF16)
    qkv_ref[...] = _mm(hb, w_ref[:, 0:3072])
    z_ref[...] = _mm(hb, w_ref[:, 3072:4096])
    ba_ref[...] = _mm(hb, w_ref[:, 4096:O_COLS])


def odd_proj(x, nw, sc, sh, w_packed, tm):
    n, d = x.shape
    rows_mod = sc.shape[0]
    row = lambda c: pl.BlockSpec((tm, c), lambda i: (i, 0))
    return pl.pallas_call(
        _odd_proj_body,
        grid=(n // tm,),
        in_specs=[row(d), pl.BlockSpec((1, d), lambda i: (0, 0)),
                  _mod_spec(rows_mod, tm, d), _mod_spec(rows_mod, tm, d),
                  pl.BlockSpec((d, O_COLS), lambda i: (0, 0))],
        out_specs=[row(3072), row(1024), row(128)],
        out_shape=[jax.ShapeDtypeStruct((n, 3072), F32), jax.ShapeDtypeStruct((n, 1024), F32),
                   jax.ShapeDtypeStruct((n, 128), F32)],
        compiler_params=_cparams(("arbitrary",)),
        name="odd_proj",
    )(x, nw, sc, sh, w_packed)


def _gdn_body(qkv_ref, z_ref, ba_ref, cs_ref, s0_ref, cw_ref, hp_ref, nw_ref,
              o_ref, sfin_ref, buf_ref, s_ref, *, c, valid, n_chunks):
    ci = pl.program_id(1)
    halo = 8

    @pl.when(ci == 0)
    def _():
        buf_ref[0:halo, :] = cs_ref[0]
        s_ref[...] = s0_ref[0]

    x = qkv_ref[...]
    buf_ref[halo:halo + c, :] = x
    cw = cw_ref[...]
    conv = buf_ref[halo - 3:halo - 3 + c, :] * cw[0:1, :]
    for j in range(1, CONV_W):
        conv = conv + buf_ref[halo - 3 + j:halo - 3 + j + c, :] * cw[j:j + 1, :]
    buf_ref[0:halo, :] = x[c - halo:, :]
    conv = _silu(conv)

    hp = hp_ref[...]
    ba = ba_ref[...]
    beta_f = _sigmoid(ba)
    g_f = hp[0:1, :] * _softplus(ba + hp[1:2, :])
    if valid < c:
        live = lax.broadcasted_iota(jnp.int32, (c, 1), 0) < valid
        beta_f = jnp.where(live, beta_f, 0.0)
        g_f = jnp.where(live, g_f, 0.0)
        conv = jnp.where(live, conv, 0.0)

    ri = lax.broadcasted_iota(jnp.int32, (c, c), 0)
    cj = lax.broadcasted_iota(jnp.int32, (c, c), 1)
    tril = ri >= cj
    strict = ri > cj
    gc = _mm01(tril, g_f)
    gct = gc.T
    z = z_ref[...]
    nw = nw_ref[...]

    heads = range(GDN_HEADS)
    sls = [slice(h * GDN_HD, (h + 1) * GDN_HD) for h in heads]
    q_h = [conv[:, sl] for sl in sls]
    k_h = [conv[:, GDN_W + h * GDN_HD:GDN_W + (h + 1) * GDN_HD] for h in heads]
    v_h = [conv[:, 2 * GDN_W + h * GDN_HD:2 * GDN_W + (h + 1) * GDN_HD] for h in heads]
    q_h = [q * lax.rsqrt(jnp.sum(q * q, axis=-1, keepdims=True) + EPS) * (GDN_HD ** -0.5) for q in q_h]
    k_h = [k * lax.rsqrt(jnp.sum(k * k, axis=-1, keepdims=True) + EPS) for k in k_h]
    g_col = [gc[:, 8 + h:9 + h] for h in heads]
    eg = [jnp.exp(g) for g in g_col]
    b_col = [beta_f[:, h:h + 1] for h in heads]
    decay = [jnp.where(tril, jnp.exp(jnp.where(tril, g_col[h] - gct[8 + h:9 + h, :], 0.0)), 0.0) for h in heads]
    kb = [k_h[h] * b_col[h] for h in heads]
    vb = [v_h[h] * b_col[h] for h in heads]
    m_h = [jnp.where(strict, _mm3(kb[h], k_h[h], NT) * decay[h], 0.0) for h in heads]
    qk = [jnp.where(tril, _mm(q_h[h], k_h[h], NT) * decay[h], 0.0) for h in heads]
    t_inv = _tri_inverse(m_h, c)
    u_h = [_mm(t_inv[h], vb[h]) for h in heads]
    w_h = [_mm(t_inv[h], kb[h] * eg[h]) for h in heads]
    s_h = [s_ref[h] for h in heads]
    v_new = [u_h[h] - _mm(w_h[h], s_h[h]) for h in heads]
    o1 = [_mm(q_h[h] * eg[h], s_h[h]) for h in heads]
    o_h = [o1[h] + _mm(qk[h], v_new[h]) for h in heads]
    g_last = [g[c - 1:c, :] for g in g_col]
    ds = [_mm(k_h[h] * jnp.exp(g_last[h] - g_col[h]), v_new[h], TN) for h in heads]
    for h in heads:
        s_ref[h] = s_h[h] * jnp.exp(g_last[h]) + ds[h]
        o_n = o_h[h] * lax.rsqrt(jnp.mean(o_h[h] * o_h[h], axis=-1, keepdims=True) + EPS) * nw
        o_ref[:, sls[h]] = o_n * _silu(z[:, sls[h]])

    @pl.when(ci == n_chunks - 1)
    def _():
        sfin_ref[0] = s_ref[...]


def gdn_mix(qkv, z, ba, cs, s0, conv_w8, hp, nw, *, c, valid):
    b = s0.shape[0]
    rows = qkv.shape[0]
    n_chunks = rows // (b * c)
    const = lambda s: pl.BlockSpec(s, lambda bi, ci: tuple(0 for _ in s))
    row = lambda w: pl.BlockSpec((c, w), lambda bi, ci: (bi * n_chunks + ci, 0))
    return pl.pallas_call(
        functools.partial(_gdn_body, c=c, valid=valid, n_chunks=n_chunks),
        grid=(b, n_chunks),
        in_specs=[row(3072), row(1024), row(128),
                  pl.BlockSpec((1, 8, 3072), lambda bi, ci: (bi, 0, 0)),
                  pl.BlockSpec((1, GDN_HEADS, 128, 128), lambda bi, ci: (bi, 0, 0, 0)),
                  const((8, 3072)), const((8, 128)), const((1, 128))],
        out_specs=[row(1024), pl.BlockSpec((1, GDN_HEADS, 128, 128), lambda bi, ci: (bi, 0, 0, 0))],
        out_shape=[jax.ShapeDtypeStruct((rows, 1024), F32),
                   jax.ShapeDtypeStruct((b, GDN_HEADS, 128, 128), F32)],
        scratch_shapes=[pltpu.VMEM((8 + c, 3072), F32), pltpu.VMEM((GDN_HEADS, 128, 128), F32)],
        compiler_params=_cparams(("arbitrary", "arbitrary")),
        name="gdn_mix",
    )(qkv, z, ba, cs, s0, conv_w8, hp, nw)


def _router_body(x_ref, nw_ref, sc_ref, sh_ref, wr_ref, br_ref, h_ref, gate_ref):
    h = _norm_mod(x_ref[...], nw_ref[...], sc_ref[...], sh_ref[...])
    h_ref[...] = h.astype(BF16)
    logits = _mmh(h, wr_ref[...]) + br_ref[...]
    tm = logits.shape[0]
    lane = lax.broadcasted_iota(jnp.int32, (tm, LANE), 1)
    is_grp = (lane >= N_EXPERTS) & (lane < N_EXPERTS + N_GROUPS)
    gl = jnp.where(is_grp, logits, NEG)
    gmax = jnp.max(gl, axis=-1, keepdims=True)
    g_idx = jnp.min(jnp.where(gl == gmax, lane, 4 * LANE), axis=-1, keepdims=True) - N_EXPERTS
    g_w = 1.0 / jnp.sum(jnp.where(is_grp, jnp.exp(gl - gmax), 0.0), axis=-1, keepdims=True)
    in_grp = (lane < N_EXPERTS) & (lane // EXP_PER_GROUP == g_idx)
    el = jnp.where(in_grp, logits, NEG)
    emax = jnp.max(el, axis=-1, keepdims=True)
    e = jnp.where(in_grp, jnp.exp(el - emax), 0.0)
    p = e / jnp.sum(e, axis=-1, keepdims=True)
    p1 = jnp.where(in_grp, p, -1.0)
    m1 = jnp.max(p1, axis=-1, keepdims=True)
    i1 = jnp.min(jnp.where(p1 == m1, lane, 4 * LANE), axis=-1, keepdims=True)
    p2 = jnp.where(lane == i1, -1.0, p1)
    m2 = jnp.max(p2, axis=-1, keepdims=True)
    i2 = jnp.min(jnp.where(p2 == m2, lane, 4 * LANE), axis=-1, keepdims=True)
    tot = m1 + m2
    gate = jnp.where(lane == i1, m1 / tot * g_w, jnp.where(lane == i2, m2 / tot * g_w, 0.0))
    gate_ref[...] = jnp.where(lane == GRP_LANE, g_idx.astype(F32), gate)


def moe_router(x, nw, sc, sh, w_r, b_r, tm):
    n, d = x.shape
    rows_mod = sc.shape[0]
    return pl.pallas_call(
        _router_body,
        grid=(n // tm,),
        in_specs=[pl.BlockSpec((tm, d), lambda i: (i, 0)), pl.BlockSpec((1, d), lambda i: (0, 0)),
                  _mod_spec(rows_mod, tm, d), _mod_spec(rows_mod, tm, d),
                  pl.BlockSpec((d, LANE), lambda i: (0, 0)), pl.BlockSpec((1, LANE), lambda i: (0, 0))],
        out_specs=[pl.BlockSpec((tm, d), lambda i: (i, 0)), pl.BlockSpec((tm, LANE), lambda i: (i, 0))],
        out_shape=[jax.ShapeDtypeStruct((n, d), BF16), jax.ShapeDtypeStruct((n, LANE), F32)],
        compiler_params=_cparams(("arbitrary",)),
        name="moe_router",
    )(x, nw, sc, sh, w_r, b_r)


def _moe_body(h_ref, gate_ref, w1_ref, w3_ref, w2_ref, x_ref, g2_ref, o_ref, acc_ref):
    e = pl.program_id(1)

    @pl.when(e == 0)
    def _():
        acc_ref[...] = jnp.zeros_like(acc_ref)

    hb = h_ref[...]
    he = _silu(_mm(hb, w1_ref[0])) * _mm(hb, w3_ref[0])
    y = _mm(he, w2_ref[0])
    gate = gate_ref[...]
    lane = lax.broadcasted_iota(jnp.int32, gate.shape, 1)
    ge = jnp.sum(jnp.where(lane == e, gate, 0.0), axis=-1, keepdims=True)
    acc_ref[...] += ge * y

    @pl.when(e == pl.num_programs(1) - 1)
    def _():
        o_ref[...] = x_ref[...] + g2_ref[...] * acc_ref[...]


def moe_ffn(h, gate, w1, w3, w2, e0, x, g2, tm):
    n, d = x.shape
    de = w1.shape[2]
    return pl.pallas_call(
        _moe_body,
        grid=(n // tm, N_EXPERTS),
        in_specs=[pl.BlockSpec((tm, d), lambda i, e: (i, 0)), pl.BlockSpec((tm, LANE), lambda i, e: (i, 0)),
                  pl.BlockSpec((1, d, de), lambda i, e: (e0 + e, 0, 0)),
                  pl.BlockSpec((1, d, de), lambda i, e: (e0 + e, 0, 0)),
                  pl.BlockSpec((1, de, d), lambda i, e: (e0 + e, 0, 0)),
                  pl.BlockSpec((tm, d), lambda i, e: (i, 0)),
                  pl.BlockSpec((1, d), lambda i, e: (0, 0)) if g2.shape[0] == 1
                  else pl.BlockSpec((tm, d), lambda i, e: (i, 0))],
        out_specs=pl.BlockSpec((tm, d), lambda i, e: (i, 0)),
        out_shape=jax.ShapeDtypeStruct((n, d), F32),
        scratch_shapes=[pltpu.VMEM((tm, d), F32)],
        compiler_params=_cparams(("arbitrary", "arbitrary")),
        name="moe_ffn",
    )(h, gate, w1, w3, w2, x, g2)


def _moe_plan(grp, tm, tb, cap, max_entries):
    nt = grp.shape[0] // tm
    cnt = jax.nn.one_hot(grp, N_GROUPS, dtype=jnp.int32).reshape(nt, tm, N_GROUPS).sum(axis=1)
    pc = (cnt + ROW_ALIGN - 1) // ROW_ALIGN * ROW_ALIGN
    segb = jnp.cumsum(pc, axis=1) - pc
    off = jnp.cumsum(pc, axis=0) - pc
    tot = pc.sum(axis=0)
    n_real = (tot + tb - 1) // tb
    n_all = jnp.minimum((tot + tm + tb - 1) // tb, cap // tb)
    ends = jnp.cumsum(n_all)
    s = jnp.arange(max_entries)
    g_of = jnp.sum(s[:, None] >= ends[None, :], axis=1)
    active = g_of < N_GROUPS
    g_c = jnp.minimum(g_of, N_GROUPS - 1)
    rt = s - (ends - n_all)[g_c]
    live = tot[g_c] - rt * tb
    real = jnp.where(live <= tb // 4, 3, jnp.where(live <= tb // 2, 5, 1))
    kind = jnp.where(active, jnp.where(rt < n_real[g_c], real, 2), 0)
    last = ends[-1] - 1
    e_grp = jnp.where(active, g_c, g_c[last])
    e_rt = jnp.where(active, rt, rt[last])
    i32 = lambda a: a.reshape(-1).astype(jnp.int32)
    return i32(segb), i32(off // ROW_ALIGN), i32(e_grp), i32(e_rt), i32(kind)


def _group_perm(gate, segb_ref, base, tm, rows):
    gt = gate.T
    grp = gt[GRP_LANE:GRP_LANE + 1, :]
    gi = lax.broadcasted_iota(jnp.int32, (8, tm), 0).astype(F32)
    oh = jnp.where(gi == grp, 1.0, 0.0)
    r_i = lax.broadcasted_iota(jnp.int32, (tm, tm), 0)
    c_i = lax.broadcasted_iota(jnp.int32, (tm, tm), 1)
    before = jnp.where(r_i < c_i, 1.0, 0.0).astype(BF16)
    rank = lax.dot_general(oh.astype(BF16), before, NN, preferred_element_type=F32)
    dest = jnp.zeros((1, tm), F32)
    for g in range(N_GROUPS):
        dest = dest + oh[g:g + 1] * (segb_ref[base + g].astype(F32) + rank[g:g + 1])
    rows_i = lax.broadcasted_iota(jnp.int32, (rows, tm), 0).astype(F32)
    return jnp.where(rows_i == dest, 1.0, 0.0).astype(BF16)


def _moe_dispatch_body(segb_ref, off_ref, h_ref, gate_ref, xg_in, gg_in, xg_ref, gg_ref, xs_ref, gs_ref, *, tm, rows):
    del off_ref, xg_in, gg_in
    i = pl.program_id(0)
    g = pl.program_id(1)

    @pl.when((i == 0) & (g == 0))
    def _():
        xs_ref[...] = jnp.zeros_like(xs_ref)
        gs_ref[...] = jnp.zeros_like(gs_ref)

    @pl.when(g == 0)
    def _():
        gate = gate_ref[...]
        p = _group_perm(gate, segb_ref, i * N_GROUPS, tm, rows)
        xs_ref[0:rows, :] = lax.dot_general(p, h_ref[...], NN, preferred_element_type=F32).astype(BF16)
        gs_ref[0:rows, :] = _mm01(p, gate)

    start = pl.multiple_of(segb_ref[i * N_GROUPS + g], ROW_ALIGN)
    xg_ref[...] = xs_ref[pl.ds(start, tm), :]
    gg_ref[...] = gs_ref[pl.ds(start, tm), :]


def moe_dispatch(h, gate, segb, off, tm, cap):
    n, d = h.shape
    rows = tm + N_GROUPS * ROW_ALIGN
    win = lambda w: pl.BlockSpec((pl.Element(tm), pl.Element(w)),
                                 lambda i, g, segb, off: ((g * (cap // ROW_ALIGN) + off[i * N_GROUPS + g]) * ROW_ALIGN, 0))
    grid_spec = pltpu.PrefetchScalarGridSpec(
        num_scalar_prefetch=2,
        grid=(n // tm, N_GROUPS),
        in_specs=[pl.BlockSpec((tm, d), lambda i, g, *_: (i, 0)), pl.BlockSpec((tm, LANE), lambda i, g, *_: (i, 0)),
                  pl.BlockSpec(memory_space=pl.ANY), pl.BlockSpec(memory_space=pl.ANY)],
        out_specs=[win(d), win(LANE)],
        scratch_shapes=[pltpu.VMEM((2 * tm + N_GROUPS * ROW_ALIGN, d), BF16),
                        pltpu.VMEM((2 * tm + N_GROUPS * ROW_ALIGN, LANE), F32)],
    )
    return pl.pallas_call(
        functools.partial(_moe_dispatch_body, tm=tm, rows=rows),
        grid_spec=grid_spec,
        out_shape=[jax.ShapeDtypeStruct((N_GROUPS * cap, d), BF16), jax.ShapeDtypeStruct((N_GROUPS * cap, LANE), F32)],
        input_output_aliases={4: 0, 5: 1},
        compiler_params=_cparams(("arbitrary", "arbitrary")),
        name="moe_dispatch",
    )(segb, off, h, gate, jnp.zeros((N_GROUPS * cap, d), BF16), jnp.zeros((N_GROUPS * cap, LANE), F32))


def _moe_group_body(grp_ref, rt_ref, kind_ref, xg_ref, gg_ref, w1_ref, w3_ref, w2_ref, yg_ref, acc_ref):
    del rt_ref
    s = pl.program_id(0)
    e = pl.program_id(1)
    kind = kind_ref[s]
    last = e == pl.num_programs(1) - 1

    def run(rows):
        @pl.when(e == 0)
        def _():
            acc_ref[...] = jnp.zeros_like(acc_ref)

        xb = xg_ref[0:rows, :]
        he = _silu(_mm(xb, w1_ref[0])) * _mm(xb, w3_ref[0])
        y = _mm(he, w2_ref[0])
        gate = gg_ref[0:rows, :]
        lane = lax.broadcasted_iota(jnp.int32, gate.shape, 1)
        ge = jnp.sum(jnp.where(lane == grp_ref[s] * EXP_PER_GROUP + e, gate, 0.0), axis=-1, keepdims=True)
        acc_ref[0:rows, :] += ge * y

        @pl.when(last)
        def _():
            yg_ref[...] = acc_ref[...]

    tb = xg_ref.shape[0]
    for code, rows in ((1, tb), (5, tb // 2), (3, tb // 4)):
        pl.when(kind == code)(functools.partial(run, rows))

    @pl.when((kind == 2) & last)
    def _():
        yg_ref[...] = jnp.zeros_like(yg_ref)


def moe_group_ffn(e_grp, e_rt, e_kind, xg, gg, w1, w3, w2, e0, tb, cap):
    d = xg.shape[1]
    de = w1.shape[2]
    row = lambda s, e, grp, rt, kind: (grp[s] * (cap // tb) + rt[s], 0)
    wsel = lambda s, e, grp, rt, kind: (e0 + grp[s] * EXP_PER_GROUP + jnp.where(kind[s] % 2 == 1, e, EXP_PER_GROUP - 1),
                                        0, 0)
    grid_spec = pltpu.PrefetchScalarGridSpec(
        num_scalar_prefetch=3,
        grid=(e_grp.shape[0], EXP_PER_GROUP),
        in_specs=[pl.BlockSpec((tb, d), row), pl.BlockSpec((tb, LANE), row),
                  pl.BlockSpec((1, d, de), wsel), pl.BlockSpec((1, d, de), wsel), pl.BlockSpec((1, de, d), wsel)],
        out_specs=pl.BlockSpec((tb, d), row),
        scratch_shapes=[pltpu.VMEM((tb, d), F32)],
    )
    return pl.pallas_call(
        _moe_group_body,
        grid_spec=grid_spec,
        out_shape=jax.ShapeDtypeStruct((N_GROUPS * cap, d), F32),
        compiler_params=_cparams(("arbitrary", "arbitrary")),
        name="moe_group_ffn",
    )(e_grp, e_rt, e_kind, xg, gg, w1, w3, w2)


def _moe_combine_body(segb_ref, off_ref, yg_ref, gate_ref, x_ref, g2_ref, o_ref, ys_ref, *, tm, rows):
    del off_ref
    i = pl.program_id(0)
    g = pl.program_id(1)

    @pl.when((i == 0) & (g == 0))
    def _():
        ys_ref[...] = jnp.zeros_like(ys_ref)

    start = pl.multiple_of(segb_ref[i * N_GROUPS + g], ROW_ALIGN)
    ys_ref[pl.ds(start, tm), :] = yg_ref[...]

    @pl.when(g == N_GROUPS - 1)
    def _():
        p = _group_perm(gate_ref[...], segb_ref, i * N_GROUPS, tm, rows)
        yh, yl = _split(ys_ref[0:rows, :])
        y = (lax.dot_general(p, yh, TN, preferred_element_type=F32)
             + lax.dot_general(p, yl, TN, preferred_element_type=F32))
        o_ref[...] = x_ref[...] + g2_ref[...] * y


def moe_combine(yg, gate, x, g2, segb, off, tm, cap):
    n, d = x.shape
    rows = tm + N_GROUPS * ROW_ALIGN
    grid_spec = pltpu.PrefetchScalarGridSpec(
        num_scalar_prefetch=2,
        grid=(n // tm, N_GROUPS),
        in_specs=[pl.BlockSpec((pl.Element(tm), pl.Element(d)),
                               lambda i, g, segb, off: ((g * (cap // ROW_ALIGN) + off[i * N_GROUPS + g]) * ROW_ALIGN, 0)),
                  pl.BlockSpec((tm, LANE), lambda i, g, *_: (i, 0)),
                  pl.BlockSpec((tm, d), lambda i, g, *_: (i, 0)),
                  pl.BlockSpec((1, d), lambda i, g, *_: (0, 0))],
        out_specs=pl.BlockSpec((tm, d), lambda i, g, *_: (i, 0)),
        scratch_shapes=[pltpu.VMEM((2 * tm + N_GROUPS * ROW_ALIGN, d), F32)],
    )
    return pl.pallas_call(
        functools.partial(_moe_combine_body, tm=tm, rows=rows),
        grid_spec=grid_spec,
        out_shape=jax.ShapeDtypeStruct((n, d), F32),
        compiler_params=_cparams(("arbitrary", "arbitrary")),
        name="moe_combine",
    )(segb, off, yg, gate, x, g2)


def moe_grouped(h, gate, w1, w3, w2, e0, x, g2, tm):
    n = h.shape[0]
    tb = tm
    cap = n + 2 * tm
    max_entries = (n + (n // tm) * N_GROUPS * (ROW_ALIGN - 1) + N_GROUPS * tm) // tb + N_GROUPS + 1
    grp = gate[:, GRP_LANE].astype(jnp.int32)
    segb, off, e_grp, e_rt, e_kind = _moe_plan(grp, tm, tb, cap, max_entries)
    xg, gg = moe_dispatch(h, gate, segb, off, tm, cap)
    yg = moe_group_ffn(e_grp, e_rt, e_kind, xg, gg, w1, w3, w2, e0, tb, cap)
    return moe_combine(yg, gate, x, g2, segb, off, tm, cap)


def _final_norm_body(x_ref, w_ref, o_ref):
    x = x_ref[...]
    o_ref[...] = x * lax.rsqrt(jnp.mean(x * x, axis=-1, keepdims=True) + EPS) * w_ref[...]


def final_norm(x, w, tm):
    n, d = x.shape
    return pl.pallas_call(
        _final_norm_body,
        grid=(n // tm,),
        in_specs=[pl.BlockSpec((tm, d), lambda i: (i, 0)), pl.BlockSpec((1, d), lambda i: (0, 0))],
        out_specs=pl.BlockSpec((tm, d), lambda i: (i, 0)),
        out_shape=jax.ShapeDtypeStruct((n, d), F32),
        compiler_params=_cparams(("arbitrary",)),
        name="final_norm",
    )(x, w)


def _row_tile(n, pref):
    t = min(pref, n)
    while n % t:
        t //= 2
    return t


def kernel(x_prompt, x_sample, c_prompt, c_sample, cache_nsa_cmp, cache_nsa_sel, page_table, state_nsa_win, state_rwkv, state_rwkv_shift, state_gdn, state_gdn_conv, norm_mix, norm_ffn, norm_final, w_ada, b_ada, even_w_in, even_w_out, nsa_cmp_pos, nsa_cmp_w, rwkv_mu, rwkv_w0, rwkv_w2, rwkv_a0, rwkv_a2, rwkv_g2, rwkv_kk, rwkv_ka, rwkv_rk, rwkv_ln_w, rwkv_ln_b, odd_w_in, odd_w_out, gdn_conv_w, gdn_a_log, gdn_dt_bias, gdn_norm_w, moe_w_grp, moe_b_grp, moe_w_exp, moe_b_exp, moe_w1, moe_w3, moe_w2):
    bp, t, d = x_prompt.shape
    bs, ts, _ = x_sample.shape
    assert bp == 1 and ts <= SPAD and ts < CMP_BLK
    depth = norm_mix.shape[0]
    n_pages, page = page_table.shape[1], cache_nsa_cmp.shape[2]
    past = n_pages * page
    wb = state_nsa_win.shape[2]
    ns = bs * SPAD
    tq, tq_s, tk = 128, 32, 512
    tm_p = _row_tile(t, 512)
    tm_s = ns

    rows_c = -(-(1 + bs) // 8) * 8
    c_all = jnp.concatenate([c_prompt, c_sample, jnp.zeros((rows_c - 1 - bs, d), F32)], axis=0)
    ada = adaln(c_all, w_ada, b_ada)

    def mods(i):
        mp = [ada[i, 0:1, j * d:(j + 1) * d] for j in range(6)]
        ms = [jnp.repeat(ada[i, 1:1 + bs, j * d:(j + 1) * d], SPAD, axis=0) for j in range(6)]
        return mp, ms

    xp = x_prompt[0]
    xs = jnp.pad(x_sample, ((0, 0), (0, SPAD - ts), (0, 0))).reshape(ns, d)

    def unpad(a):
        return a.reshape(bs, SPAD, -1)[:, :ts]

    w1_all, w3_all, w2_all = (w.reshape((-1,) + w.shape[2:]) for w in (moe_w1, moe_w3, moe_w2))
    outs = {k: [] for k in ("cmp_p", "cmp_s", "sel_p", "sel_s", "win_p", "win_s", "rw_p", "rw_s", "sh_p", "sh_s",
                            "gd_p", "gd_s", "cv_p", "cv_s")}

    for i in range(depth):
        (sh1p, sc1p, gt1p, sh2p, sc2p, gt2p), (sh1s, sc1s, gt1s, sh2s, sc2s, gt2s) = mods(i)
        j = i // 2
        nw = norm_mix[i][None, :]
        if i % 2 == 0:
            w_packed = _pack_even_w(even_w_in[j])
            mu = _pack_rw_vec(rwkv_mu[j])
            wts, wc = _cmp_weights(nsa_cmp_pos[j], nsa_cmp_w[j])
            vec = jnp.stack([rwkv_w0[j], rwkv_a0[j], rwkv_kk[j], rwkv_ka[j], rwkv_ln_w[j], rwkv_ln_b[j],
                             jnp.zeros_like(rwkv_w0[j]), jnp.zeros_like(rwkv_w0[j])])
            pad_lora = lambda w: jnp.concatenate([w, jnp.zeros((128 - w.shape[0], w.shape[1]), w.dtype)], axis=0)
            w2p, a2p, g2p = pad_lora(rwkv_w2[j]), pad_lora(rwkv_a2[j]), rwkv_g2[j]
            hid = jnp.arange(RWKV_W) // RWKV_HD
            seg = (hid[:, None] == hid[None, :]).astype(F32)[:RWKV_W // 2, :RWKV_W // 2]
            rk = rwkv_rk[j].reshape(1, RWKV_W)
            wo_nsa, wo_rw = even_w_out[j][:512].astype(BF16), even_w_out[j][512:].astype(BF16)

            kv, qt, gt, ks, vst, kw, vwt, rw, hl = even_proj(xp, nw, sc1p, sh1p, w_packed, tm_p, 8)
            kvc = compress_prompt(kv, wts, wc, _row_tile(t, 512))
            gates = gt[:24].reshape(NSA_KV_HEADS, 12, t)
            gates = jnp.pad(gates, ((0, 0), (0, 4), (0, 0)))[None]
            o_nsa = nsa_attention(
                qt[None], gates, kvc[None], kvc.T[None], ks[None], vst[None], kw[None], vwt[None],
                tq=tq, tk=tk, wk=WINDOW + tq,
                pos0_fn=lambda qi: qi * tq,
                wstart_fn=lambda qi: jnp.maximum(qi * tq - WINDOW, 0),
                wpos0_fn=lambda qi: jnp.maximum(qi * tq - WINDOW, 0))[0]
            o_rw, s_rw = rwkv_mix(rw, jnp.zeros((1, 8, RW_COLS), F32), jnp.zeros((1, RWKV_HEADS, 64, 64), F32),
                                  mu, vec, w2p, a2p, g2p, seg, rk, c=64, valid=64)
            xp = out_proj([o_nsa, o_rw], [wo_nsa, wo_rw], xp, gt1p, tm_p)
            outs["cmp_p"].append(kv[:, 0:256].reshape(1, t, 2, 2, 64))
            outs["sel_p"].append(kv[:, 256:512].reshape(1, t, 2, 2, 64))
            kvw_rows = kv[:, 512:768].reshape(1, t, 2, 2, 64)
            outs["win_p"].append(kvw_rows[:, -min(WINDOW, t):])
            outs["rw_p"].append(s_rw)
            outs["sh_p"].append(hl[-1:])

            kv, qt, gt, _, _, _, _, rw, hl = even_proj(xs, nw, sc1s, sh1s, w_packed, tm_s, ns)
            kv_new = unpad(kv)
            rw0 = small_matmul(jnp.pad(state_rwkv_shift[j], ((0, -bs % 8), (0, 0))), w_packed[:, E_RW:])[:bs]
            rw0 = jnp.pad(rw0[:, None, :], ((0, 0), (7, 0), (0, 0)))
            pool_cmp = cache_nsa_cmp[j].transpose(0, 2, 3, 4, 1).reshape(-1, 256, page)
            pool_sel = cache_nsa_sel[j].transpose(0, 2, 3, 4, 1).reshape(-1, 256, page)
            kvc_s = compress_paged(pool_cmp, page_table, nsa_cmp_pos[j], wc, math.gcd(n_pages, 32))
            tail = jnp.pad(kv_new[:, :, 256:512], ((0, 0), (0, tk - ts), (0, 0)))
            wbuf = state_nsa_win[j].reshape(bs, wb, 256)
            kvw_all = jnp.concatenate([wbuf, kv_new[:, :, 512:768]], axis=1)
            wk_s = -(-(wb + ts) // 128) * 128
            kvw_pad = jnp.pad(kvw_all, ((0, 0), (0, wk_s - wb - ts), (0, 0)))
            kw_s = kvw_pad[:, :, :128].astype(BF16)
            vwt_s = jnp.swapaxes(kvw_pad[:, :, 128:], 1, 2).astype(BF16)
            qt_s = jnp.pad(qt.reshape(512, bs, SPAD).transpose(1, 0, 2), ((0, 0), (0, 0), (0, tq_s - SPAD)))
            g_s = gt[:24].reshape(NSA_KV_HEADS, 12, bs, SPAD).transpose(2, 0, 1, 3)
            g_s = jnp.pad(g_s, ((0, 0), (0, 0), (0, 4), (0, tq_s - SPAD)))
            o_nsa = nsa_attention_paged(
                qt_s, g_s, kvc_s, jnp.swapaxes(kvc_s, 1, 2), pool_sel, page_table, tail, kw_s, vwt_s,
                tq=tq_s, tk=tk, wk=wk_s,
                pos0_fn=lambda qi: past,
                wstart_fn=lambda qi: 0,
                wpos0_fn=lambda qi: past - wb)
            o_nsa = o_nsa[:, :SPAD].reshape(ns, 512)
            o_rw, s_rw = rwkv_mix(rw, rw0, state_rwkv[j], mu, vec, w2p, a2p, g2p, seg, rk, c=SPAD, valid=ts)
            xs = out_proj([o_nsa, o_rw], [wo_nsa, wo_rw], xs, gt1s, tm_s)
            outs["cmp_s"].append(kv_new[:, :, 0:256].reshape(bs, ts, 2, 2, 64))
            outs["sel_s"].append(kv_new[:, :, 256:512].reshape(bs, ts, 2, 2, 64))
            outs["win_s"].append(kvw_all[:, -wb:].reshape(bs, wb, 2, 2, 64))
            outs["rw_s"].append(s_rw)
            outs["sh_s"].append(hl.reshape(bs, SPAD, d)[:, ts - 1])
        else:
            w_in = odd_w_in[j]
            w_packed = jnp.concatenate([w_in, jnp.zeros((d, O_COLS - w_in.shape[1]), F32)], axis=1).astype(BF16)
            conv_w8 = jnp.pad(gdn_conv_w[j], ((0, 8 - CONV_W), (0, 0)))
            hp = jnp.zeros((8, 128), F32)
            hp = hp.at[0, 8:16].set(-jnp.exp(gdn_a_log[j])).at[1, 8:16].set(gdn_dt_bias[j])
            gnw = gdn_norm_w[j][None, :]
            wo = odd_w_out[j].astype(BF16)

            qkv, z, ba = odd_proj(xp, nw, sc1p, sh1p, w_packed, tm_p)
            o_g, s_g = gdn_mix(qkv, z, ba, jnp.zeros((1, 8, 3 * GDN_W), F32),
                               jnp.zeros((1, GDN_HEADS, GDN_HD, GDN_HD), F32), conv_w8, hp, gnw, c=64, valid=64)
            xp = out_proj([o_g], [wo], xp, gt1p, tm_p)
            outs["gd_p"].append(s_g)
            outs["cv_p"].append(qkv[None, -(CONV_W - 1):])

            qkv, z, ba = odd_proj(xs, nw, sc1s, sh1s, w_packed, tm_s)
            cs = jnp.pad(state_gdn_conv[j], ((0, 0), (8 - (CONV_W - 1), 0), (0, 0)))
            o_g, s_g = gdn_mix(qkv, z, ba, cs, state_gdn[j], conv_w8, hp, gnw, c=SPAD, valid=ts)
            xs = out_proj([o_g], [wo], xs, gt1s, tm_s)
            xpad = jnp.concatenate([state_gdn_conv[j], unpad(qkv)], axis=1)
            outs["gd_s"].append(s_g)
            outs["cv_s"].append(xpad[:, -(CONV_W - 1):])

        nwf = norm_ffn[i][None, :]
        w_r = jnp.concatenate([moe_w_exp[i], moe_w_grp[i], jnp.zeros((d, LANE - N_EXPERTS - N_GROUPS), F32)], axis=1)
        b_r = jnp.concatenate([moe_b_exp[i], moe_b_grp[i], jnp.zeros((LANE - N_EXPERTS - N_GROUPS,), F32)])[None, :]
        h2, gate = moe_router(xp, nwf, sc2p, sh2p, w_r, b_r, tm_p)
        xp = moe_grouped(h2, gate, w1_all, w3_all, w2_all, i * N_EXPERTS, xp, gt2p, _row_tile(t, 1024))
        h2, gate = moe_router(xs, nwf, sc2s, sh2s, w_r, b_r, tm_s)
        xs = moe_ffn(h2, gate, w1_all, w3_all, w2_all, i * N_EXPERTS, xs, gt2s, tm_s)

    nf = norm_final[None, :]
    y_prompt = final_norm(xp, nf, tm_p)[None]
    y_sample = unpad(final_norm(xs, nf, tm_s))
    st = lambda key: jnp.stack(outs[key])
    return (y_prompt, y_sample, st("cmp_p"), st("cmp_s"), st("sel_p"), st("sel_s"), st("win_p"), st("win_s"),
            st("rw_p"), st("rw_s"), st("sh_p"), st("sh_s"), st("gd_p"), st("gd_s"), st("cv_p"), st("cv_s"))
```

```python
import functools
import math

import jax
import jax.numpy as jnp
from jax import lax
from jax.experimental import pallas as pl
from jax.experimental.pallas import tpu as pltpu

F32 = jnp.float32
BF16 = jnp.bfloat16
HIGHEST = lax.Precision.HIGHEST

NSA_HEADS = 8
NSA_KV_HEADS = 2
NSA_GROUP = 4
NSA_HD = 64
CMP_BLK = 64
SEL_BLK = 64
TOPK_BLK = 16
WINDOW = 512
FORCE_BONUS = 2.0 * NSA_GROUP
RWKV_HEADS = 8
RWKV_HD = 64
RWKV_W = 512
RWKV_GN_EPS = 64e-5
GDN_HEADS = 8
GDN_HD = 128
GDN_W = 1024
CONV_W = 4
N_GROUPS = 4
EXP_PER_GROUP = 8
N_EXPERTS = 32
EPS = 1e-6
NEG = -1e30

LANE = 128
GRP_LANE = 64
ROW_ALIGN = 16
SAMPLE_TILE_SLOTS = 8
SPAD = 8
VMEM_LIMIT = 56 * 1024 * 1024

ROW_TILE = 512
MOE_ROW_TILE = 1024
NSA_TQ = 128
NSA_TQ_SAMPLE = 32
NSA_TK = 512
CHUNK = 64
CMP_PAGES_PER_STEP = 32

NN = (((1,), (0,)), ((), ()))
NT = (((1,), (1,)), ((), ()))
TN = (((0,), (0,)), ((), ()))

E_Q, E_KV, E_G, E_RW = 0, 512, 1280, 1408
E_COLS = 1408 + 1920
RW_COLS = 1920
O_COLS = 3072 + 1024 + 128


def _mm(a, b, dims=NN):
    return lax.dot_general(a.astype(BF16), b.astype(BF16), dims, preferred_element_type=F32)


def _mmh(a, b, dims=NN):
    return lax.dot_general(a.astype(F32), b.astype(F32), dims, precision=HIGHEST, preferred_element_type=F32)


def _split(a):
    hi = a.astype(BF16)
    return hi, (a - hi.astype(F32)).astype(BF16)


def _mm3(a, b, dims=NN):
    ah, al = _split(a)
    bh, bl = _split(b)
    d = lambda x, y: lax.dot_general(x, y, dims, preferred_element_type=F32)
    return d(ah, bh) + (d(ah, bl) + d(al, bh))


def _split3(x):
    h1 = x.astype(BF16)
    r1 = x - h1.astype(F32)
    h2 = r1.astype(BF16)
    return h1, h2, (r1 - h2.astype(F32)).astype(BF16)


def _mm01(m01, x):
    m = m01.astype(BF16)
    parts = _split3(x)
    d = lambda y: lax.dot_general(m, y, NN, preferred_element_type=F32)
    return d(parts[0]) + (d(parts[1]) + d(parts[2]))


def _cumsum_rows(x):
    n = x.shape[0]
    row = lax.broadcasted_iota(jnp.int32, x.shape, 0)
    shift = 1
    while shift < n:
        x = x + jnp.where(row >= shift, pltpu.roll(x, shift, 0), 0.0)
        shift *= 2
    return x


def _head_sums(xs, seg_half):
    r = xs[0].shape[0]
    half = seg_half.shape[0]
    pieces = [p[:, h0:h0 + half] for x in xs for p in _split3(x) for h0 in (0, half)]
    out = lax.dot_general(jnp.concatenate(pieces, axis=0), seg_half.astype(BF16), NN, preferred_element_type=F32)
    res = []
    for i in range(len(xs)):
        o = [out[(6 * i + u) * r:(6 * i + u + 1) * r] for u in range(6)]
        res.append(jnp.concatenate([o[0] + (o[2] + o[4]), o[1] + (o[3] + o[5])], axis=1))
    return res


def _sigmoid(x):
    return 1.0 / (1.0 + jnp.exp(-x))


def _silu(x):
    return x * _sigmoid(x)


def _softplus(x):
    return jnp.maximum(x, 0.0) + jnp.log(1.0 + jnp.exp(-jnp.abs(x)))


def _cparams(sem):
    return pltpu.CompilerParams(dimension_semantics=sem, vmem_limit_bytes=VMEM_LIMIT)


def _norm_mod(x, nw, sc, sh):
    y = x * lax.rsqrt(jnp.mean(x * x, axis=-1, keepdims=True) + EPS)
    return (y * nw) * (1.0 + sc) + sh


def _mod_spec(rows_mod, tm, d):
    if rows_mod == 1:
        return pl.BlockSpec((1, d), lambda i: (0, 0))
    return pl.BlockSpec((tm, d), lambda i: (i, 0))


def _adaln_body(c_ref, w_ref, b_ref, o_ref):
    o_ref[0] = _mmh(_silu(c_ref[...]), w_ref[0]) + b_ref[0]


def adaln(c_all, w_ada, b_ada):
    depth, d, n6 = w_ada.shape
    rows = c_all.shape[0]
    tn = 768
    return pl.pallas_call(
        _adaln_body,
        grid=(depth, n6 // tn),
        in_specs=[pl.BlockSpec((rows, d), lambda l, j: (0, 0)),
                  pl.BlockSpec((1, d, tn), lambda l, j: (l, 0, j)),
                  pl.BlockSpec((1, 1, tn), lambda l, j: (l, 0, j))],
        out_specs=pl.BlockSpec((1, rows, tn), lambda l, j: (l, 0, j)),
        out_shape=jax.ShapeDtypeStruct((depth, rows, n6), F32),
        compiler_params=_cparams(("arbitrary", "arbitrary")),
        name="adaln",
    )(c_all, w_ada, b_ada.reshape(depth, 1, n6))


def _even_proj_body(x_ref, nw_ref, sc_ref, sh_ref, w_ref,
                    kv_ref, qt_ref, gt_ref, ks_ref, vst_ref, kw_ref, vwt_ref, rw_ref, hl_ref):
    h = _norm_mod(x_ref[...], nw_ref[...], sc_ref[...], sh_ref[...])
    hl = hl_ref.shape[0]
    hl_ref[...] = h[h.shape[0] - hl:, :]
    hb = h.astype(BF16)
    q = _mm(hb, w_ref[:, E_Q:E_Q + 512]) * (NSA_HD ** -0.5)
    qt_ref[...] = q.T.astype(BF16)
    kv = _mm(hb, w_ref[:, E_KV:E_KV + 768])
    kv_ref[...] = kv
    ks_ref[...] = kv[:, 256:384].astype(BF16)
    vst_ref[...] = kv[:, 384:512].T.astype(BF16)
    kw_ref[...] = kv[:, 512:640].astype(BF16)
    vwt_ref[...] = kv[:, 640:768].T.astype(BF16)
    g = _sigmoid(_mm(hb, w_ref[:, E_G:E_G + 128]))
    gt_ref[...] = g.T
    rw_ref[...] = _mm(hb, w_ref[:, E_RW:E_RW + RW_COLS])


def even_proj(x, nw, sc, sh, w_packed, tm, hl_rows):
    n, d = x.shape
    rows_mod = sc.shape[0]
    row = lambda c: pl.BlockSpec((tm, c), lambda i: (i, 0))
    col = lambda r: pl.BlockSpec((r, tm), lambda i: (0, i))
    return pl.pallas_call(
        _even_proj_body,
        grid=(n // tm,),
        in_specs=[row(d), pl.BlockSpec((1, d), lambda i: (0, 0)),
                  _mod_spec(rows_mod, tm, d), _mod_spec(rows_mod, tm, d),
                  pl.BlockSpec((d, E_COLS), lambda i: (0, 0))],
        out_specs=[row(768), col(512), col(128), row(128), col(128), row(128), col(128), row(RW_COLS),
                   pl.BlockSpec((hl_rows, d), lambda i: (0, 0))],
        out_shape=[jax.ShapeDtypeStruct((n, 768), F32),
                   jax.ShapeDtypeStruct((512, n), BF16),
                   jax.ShapeDtypeStruct((128, n), F32),
                   jax.ShapeDtypeStruct((n, 128), BF16),
                   jax.ShapeDtypeStruct((128, n), BF16),
                   jax.ShapeDtypeStruct((n, 128), BF16),
                   jax.ShapeDtypeStruct((128, n), BF16),
                   jax.ShapeDtypeStruct((n, RW_COLS), F32),
                   jax.ShapeDtypeStruct((hl_rows, d), F32)],
        compiler_params=_cparams(("arbitrary",)),
        name="even_proj",
    )(x, nw, sc, sh, w_packed)


def _pack_even_w(w_in):
    d = w_in.shape[0]
    z = lambda c: jnp.zeros((d, c), w_in.dtype)
    nsa = 1304
    rw = w_in[:, nsa:]
    parts = [w_in[:, :1280], w_in[:, 1280:1304], z(104),
             rw[:, :1536], rw[:, 1536:1600], z(64), rw[:, 1600:1664], z(64), rw[:, 1664:1792]]
    return jnp.concatenate(parts, axis=1).astype(BF16)


def _pack_rw_vec(v):
    z = jnp.zeros((64,), v.dtype)
    return jnp.concatenate([v[:1536], v[1536:1600], z, v[1600:1664], z, v[1664:1792]])[None, :]


def _mm_body(x_ref, w_ref, o_ref):
    o_ref[...] = _mm(x_ref[...], w_ref[...])


def small_matmul(x, w):
    return pl.pallas_call(
        _mm_body,
        out_shape=jax.ShapeDtypeStruct((x.shape[0], w.shape[1]), F32),
        compiler_params=pltpu.CompilerParams(vmem_limit_bytes=VMEM_LIMIT),
        name="small_matmul",
    )(x, w)


def _compress_body(x_ref, wts_ref, wc_ref, o_ref):
    x = x_ref[...]
    nb = x.shape[0] // CMP_BLK
    pooled = jnp.sum(x.reshape(nb, CMP_BLK, x.shape[-1]) * wts_ref[...][None], axis=1)
    o_ref[...] = _mm(pooled, wc_ref[...])


def _compress_paged_body(pt_ref, *refs, pps):
    page_refs = refs[:pps]
    wp_ref, wc_ref, o_ref = refs[pps:]
    x = jnp.concatenate([r[0] for r in page_refs], axis=1)
    pooled_t = jnp.concatenate([_mm(x[0:128], wp_ref[0]), _mm(x[128:256], wp_ref[1])], axis=0)
    nb = o_ref.shape[1]
    o_ref[0] = _mm(pooled_t.T[:nb], wc_ref[...])


def _cmp_weights(pos_wts, w_c):
    wts = jnp.repeat(pos_wts.T, 128, axis=1)
    eye2 = jnp.eye(2, dtype=w_c.dtype)
    blocks = [jnp.kron(eye2, w_c[c]) for c in range(2)]
    z = jnp.zeros((128, 128), w_c.dtype)
    wc = jnp.concatenate([jnp.concatenate([blocks[0], z], axis=1),
                          jnp.concatenate([z, blocks[1]], axis=1)], axis=0)
    return wts, wc


def compress_prompt(kv, wts, wc, tr):
    t = kv.shape[0]
    nb = tr // CMP_BLK
    return pl.pallas_call(
        _compress_body,
        grid=(t // tr,),
        in_specs=[pl.BlockSpec((tr, 256), lambda i: (i, 0)),
                  pl.BlockSpec((CMP_BLK, 256), lambda i: (0, 0)),
                  pl.BlockSpec((256, 256), lambda i: (0, 0))],
        out_specs=pl.BlockSpec((nb, 256), lambda i: (i, 0)),
        out_shape=jax.ShapeDtypeStruct((t // CMP_BLK, 256), F32),
        compiler_params=_cparams(("arbitrary",)),
        name="compress_prompt",
    )(kv, wts, wc)


def compress_paged(pool_t, page_table, pos_wts, wc, pages_per_step):
    b, n_pages = page_table.shape
    page = pool_t.shape[2]
    pps = pages_per_step
    nb = pps * page // CMP_BLK
    p_idx = jnp.arange(pps * page)
    wp = jax.nn.one_hot(p_idx // CMP_BLK, LANE, dtype=F32)[None] * pos_wts[:, p_idx % CMP_BLK][:, :, None]

    def page_spec(u):
        return pl.BlockSpec((1, 256, page), lambda bi, g, pt: (pt[bi, g * pps + u], 0, 0))

    grid_spec = pltpu.PrefetchScalarGridSpec(
        num_scalar_prefetch=1,
        grid=(b, n_pages // pps),
        in_specs=[page_spec(u) for u in range(pps)] + [
            pl.BlockSpec((2, pps * page, LANE), lambda bi, g, pt: (0, 0, 0)),
            pl.BlockSpec((256, 256), lambda bi, g, pt: (0, 0))],
        out_specs=pl.BlockSpec((1, nb, 256), lambda bi, g, pt: (bi, g, 0)),
    )
    return pl.pallas_call(
        functools.partial(_compress_paged_body, pps=pps),
        grid_spec=grid_spec,
        out_shape=jax.ShapeDtypeStruct((b, n_pages * page // CMP_BLK, 256), F32),
        compiler_params=_cparams(("arbitrary", "arbitrary")),
        name="compress_paged",
    )(page_table, *([pool_t] * pps), wp, wc)


def _gather_sel_body(pt_ref, tiles_ref, cnt_ref, *refs, pps, n_page_steps, nt):
    del pt_ref
    page_refs = refs[:pps]
    tail_ref, ks_ref, vst_ref = refs[pps:]
    bi = pl.program_id(0)
    a = pl.program_id(1)
    j = tiles_ref[bi * nt + jnp.minimum(a, cnt_ref[bi] - 1)]
    live = a < cnt_ref[bi]

    @pl.when(live & (j < n_page_steps))
    def _():
        ks_ref[0] = jnp.concatenate([r[0][0:128].T for r in page_refs], axis=0).astype(BF16)
        vst_ref[0] = jnp.concatenate([r[0][128:256] for r in page_refs], axis=1).astype(BF16)

    @pl.when(live & (j >= n_page_steps))
    def _():
        x = tail_ref[0]
        ks_ref[0] = x[:, :128].astype(BF16)
        vst_ref[0] = x[:, 128:].T.astype(BF16)


def gather_sel(pool_t, page_table, tail, tk, tiles, cnt, n_slots):
    b, n_pages = page_table.shape
    page = pool_t.shape[2]
    pps = tk // page
    n_page_steps = n_pages // pps
    nt = n_page_steps + 1
    nk = n_slots * tk

    def slot(bi, a, pt, tiles, cnt):
        return jnp.minimum(a, cnt[bi] - 1)

    def page_spec(u):
        def index(bi, a, pt, tiles, cnt):
            j = tiles[bi * nt + slot(bi, a, pt, tiles, cnt)]
            return (pt[bi, jnp.minimum(j * pps + u, n_pages - 1)], 0, 0)
        return pl.BlockSpec((1, 256, page), index)

    grid_spec = pltpu.PrefetchScalarGridSpec(
        num_scalar_prefetch=3,
        grid=(b, jnp.max(cnt)),
        in_specs=[page_spec(u) for u in range(pps)] + [pl.BlockSpec((1, tk, 256), lambda bi, a, *_: (bi, 0, 0))],
        out_specs=[pl.BlockSpec((1, tk, 128), lambda bi, a, *s: (bi, slot(bi, a, *s), 0)),
                   pl.BlockSpec((1, 128, tk), lambda bi, a, *s: (bi, 0, slot(bi, a, *s)))],
    )
    return pl.pallas_call(
        functools.partial(_gather_sel_body, pps=pps, n_page_steps=n_page_steps, nt=nt),
        grid_spec=grid_spec,
        out_shape=[jax.ShapeDtypeStruct((b, nk, 128), BF16), jax.ShapeDtypeStruct((b, 128, nk), BF16)],
        compiler_params=_cparams(("arbitrary", "arbitrary")),
        name="gather_sel",
    )(page_table, tiles, cnt, *([pool_t] * pps), tail)


MASKED = -1e30
M_INIT = -1e29


def _nsa_query(qt_ref, k, tq):
    w4 = NSA_GROUP * tq
    qb = qt_ref[0].astype(F32)
    qcat = jnp.concatenate([qb[g * 64:(g + 1) * 64] for g in range(NSA_GROUP)], axis=1)
    q2 = jnp.concatenate([qcat, qcat], axis=0)
    row = lax.broadcasted_iota(jnp.int32, (128, w4), 0)
    qe = jnp.where(row // 64 == k, q2, 0.0)
    gidx = lax.broadcasted_iota(jnp.int32, (128, w4), 1) // tq
    base = jnp.where(k == 0, 0.5, 0.5 / 16.0)
    slope = base * jnp.where(gidx == 0, 1.0, jnp.where(gidx == 1, 0.5, jnp.where(gidx == 2, 0.25, 0.125)))
    mult = jnp.where(row == 0, 16.0, jnp.where(row == 1, 1.0, jnp.where(row == 2, 128.0,
                                                                         jnp.where(row == 3, 64.0, 0.0))))
    return jnp.concatenate([qe, slope * mult], axis=0).astype(BF16)


def _pos_features(rows, tile_rel):
    r = lax.broadcasted_iota(jnp.int32, (rows, LANE), 0)
    lane = lax.broadcasted_iota(jnp.int32, (rows, LANE), 1)
    ab = jnp.where(lane == 0, r // 16, jnp.where(lane == 1, r % 16, 0)).astype(F32)
    return jnp.where(lane == 2, tile_rel, ab).astype(BF16)


def _gate_rows(gb, j, tq):
    return jnp.concatenate([gb[g * 3 + j:g * 3 + j + 1, :] for g in range(NSA_GROUP)], axis=1)


def _nsa_select_body(qt_ref, g_ref, kvc_ref, kvct_ref, kw_ref, vwt_ref, part_ref, sel_ref, flag_ref, *,
                     tq, tk, wk, nbc, nb, pos0_fn, wstart_fn, wpos0_fn):
    i = pl.program_id(1)
    k = pl.program_id(2)
    w4 = NSA_GROUP * tq
    pos0 = pos0_fn(i)
    qa = _nsa_query(qt_ref, k, tq)
    pos_q = pos0 + lax.broadcasted_iota(jnp.int32, (1, w4), 1) % tq

    def softmax_cols(s, bad):
        s = jnp.where(bad, MASKED, s)
        m = jnp.maximum(jnp.max(s, axis=0, keepdims=True), M_INIT)
        e = jnp.exp(s - m)
        return e / jnp.maximum(jnp.sum(e, axis=0, keepdims=True), 1e-30)

    n_i = lax.broadcasted_iota(jnp.int32, (nbc, LANE), 0)
    lane_c = lax.broadcasted_iota(jnp.int32, (nbc, LANE), 1)
    feat_c = jnp.where(lane_c == 3, n_i - pos0 // CMP_BLK, 0).astype(F32).astype(BF16)
    kc = jnp.concatenate([kvc_ref[0][:, :128].astype(BF16), feat_c], axis=1)
    c_end = lax.broadcasted_iota(jnp.int32, (nbc, 1), 0) * CMP_BLK + (CMP_BLK - 1)
    p_c = softmax_cols(lax.dot_general(kc, qa, NN, preferred_element_type=F32), c_end > pos_q)
    vct = kvct_ref[0, pl.ds(pl.multiple_of(128 + k * 64, 64), 64), :]
    o_c = _mm(vct, p_c)

    imp = p_c[:, 0:tq]
    for g in range(1, NSA_GROUP):
        imp = imp + p_c[:, g * tq:(g + 1) * tq]
    if nb > nbc:
        imp = jnp.concatenate([imp, jnp.zeros((nb - nbc, tq), F32)], axis=0)
    blk = lax.broadcasted_iota(jnp.int32, (nb, tq), 0)
    cur = (pos0 + lax.broadcasted_iota(jnp.int32, (1, tq), 1)) // SEL_BLK
    forced = (blk == cur) | (blk == cur - 1) | (blk == 0)
    score = jnp.where(blk <= cur, imp + jnp.where(forced, FORCE_BONUS, 0.0), -1.0)
    for _ in range(min(TOPK_BLK, nb)):
        m = jnp.max(score, axis=0, keepdims=True)
        first = jnp.min(jnp.where(score == m, blk, nb), axis=0, keepdims=True)
        score = jnp.where(blk == first, -2.0, score)
    sel = jnp.where(score == -2.0, 1.0, 0.0)
    sel_ref[0, 0] = sel
    bpt = tk // SEL_BLK
    any_row = jnp.max(sel, axis=1, keepdims=True)
    flag_ref[0, 0] = jnp.max(any_row.reshape(nb // bpt, bpt, 1), axis=1)

    wstart = wstart_fn(i)
    if not isinstance(wstart, int):
        wstart = pl.multiple_of(wstart, 128)
    wpos0 = wpos0_fn(i)
    tile_rel = jnp.asarray((wpos0 - pos0) // 128, F32)
    kw = jnp.concatenate([kw_ref[0, pl.ds(wstart, wk), :], _pos_features(wk, tile_rel)], axis=1)
    dist_w = pos_q - (wpos0 + lax.broadcasted_iota(jnp.int32, (wk, 1), 0))
    p_w = softmax_cols(lax.dot_general(kw, qa, NN, preferred_element_type=F32), (dist_w < 0) | (dist_w >= WINDOW))
    vwin = vwt_ref[0, pl.ds(pl.multiple_of(k * 64, 64), 64), pl.ds(wstart, wk)]
    o_w = _mm(vwin, p_w)

    gb = g_ref[0, 0]
    part_ref[0, 0] = _gate_rows(gb, 0, tq) * o_c + _gate_rows(gb, 2, tq) * o_w


def nsa_select(qt, gates, kvc, kvct, kw, vwt, *, nb, tq, tk, wk, pos0_fn, wstart_fn, wpos0_fn):
    b, _, nq = qt.shape
    nbc = kvc.shape[1]
    nw = kw.shape[1]
    nqt = nq // tq
    nt = nb * SEL_BLK // tk
    w4 = NSA_GROUP * tq
    assert nbc <= 256 and tk <= 512 and wk <= 1024
    body = functools.partial(_nsa_select_body, tq=tq, tk=tk, wk=wk, nbc=nbc, nb=nb, pos0_fn=pos0_fn,
                             wstart_fn=wstart_fn, wpos0_fn=wpos0_fn)
    full = lambda s1, s2: pl.BlockSpec((1, s1, s2), lambda bi, i, k: (bi, 0, 0))
    step = lambda s1, s2: pl.BlockSpec((1, 1, s1, s2), lambda bi, i, k: (bi, i * NSA_KV_HEADS + k, 0, 0))
    return pl.pallas_call(
        body,
        grid=(b, nqt, NSA_KV_HEADS),
        in_specs=[pl.BlockSpec((1, 256, tq), lambda bi, i, k: (bi, k, i)),
                  pl.BlockSpec((1, 1, 16, tq), lambda bi, i, k: (bi, k, 0, i)),
                  full(nbc, 256), full(256, nbc), full(nw, 128), full(128, nw)],
        out_specs=[step(64, w4), step(nb, tq), step(nt, 1)],
        out_shape=[jax.ShapeDtypeStruct((b, nqt * 2, 64, w4), F32),
                   jax.ShapeDtypeStruct((b, nqt * 2, nb, tq), F32),
                   jax.ShapeDtypeStruct((b, nqt * 2, nt, 1), F32)],
        compiler_params=_cparams(("arbitrary", "arbitrary", "arbitrary")),
        name="nsa_select",
    )(qt, gates, kvc, kvct, kw, vwt)


def _nsa_selected_body(list_ref, slot_ref, cnt_ref, qt_ref, g_ref, sel_ref, ks_ref, vst_ref, part_ref, o_ref, *,
                       tq, tk, nt, pos0_fn):
    bi = pl.program_id(0)
    i = pl.program_id(1)
    k = pl.program_id(2)
    step = (bi * pl.num_programs(1) + i) * NSA_KV_HEADS + k
    w4 = NSA_GROUP * tq
    pos0 = pos0_fn(i)
    qa = _nsa_query(qt_ref, k, tq)
    pos_q = pos0 + lax.broadcasted_iota(jnp.int32, (1, w4), 1) % tq
    bpt = tk // SEL_BLK
    row_k = lax.broadcasted_iota(jnp.int32, (tk, 1), 0)
    r = lax.broadcasted_iota(jnp.int32, (tk, LANE), 0)
    lane = lax.broadcasted_iota(jnp.int32, (tk, LANE), 1)
    feat_ab = jnp.where(lane == 0, r // 16, jnp.where(lane == 1, r % 16, 0)).astype(F32)

    n_act = cnt_ref[step]

    def tile_scores(jj, live):
        j = list_ref[step * nt + jj]
        off = pl.multiple_of(j * tk, tk)
        buf = pl.multiple_of(slot_ref[step * nt + jj] * tk, tk)
        tile_rel = ((off - pos0) // 128).astype(F32)
        feat = jnp.where(lane == 2, tile_rel, feat_ab).astype(BF16)
        kj = jnp.concatenate([ks_ref[0, pl.ds(buf, tk), :], feat], axis=1)
        s = lax.dot_general(kj, qa, NN, preferred_element_type=F32)
        selb = (sel_ref[0, 0, pl.ds(pl.multiple_of(j * bpt, bpt), bpt), :] - 1.0) * (-MASKED)
        selb = jnp.concatenate([selb] * NSA_GROUP, axis=1) + jnp.where(live, 0.0, MASKED)
        s = s + jnp.broadcast_to(selb[:, None, :], (bpt, SEL_BLK, w4)).reshape(tk, w4)
        s = jnp.where(row_k > pos_q - off, MASKED, s)
        return s, vst_ref[0, pl.ds(pl.multiple_of(k * 64, 64), 64), pl.ds(buf, tk)]

    def kv_pair(pp, carry):
        m_i, l_i, acc = carry
        second = 2 * pp + 1
        s_a, v_a = tile_scores(2 * pp, True)
        s_b, v_b = tile_scores(jnp.minimum(second, n_act - 1), second < n_act)
        m_new = jnp.maximum(m_i, jnp.maximum(jnp.max(s_a, axis=0, keepdims=True), jnp.max(s_b, axis=0, keepdims=True)))
        p_a = jnp.exp(s_a - m_new)
        p_b = jnp.exp(s_b - m_new)
        alpha = jnp.exp(m_i - m_new)
        l_new = l_i * alpha + (jnp.sum(p_a, axis=0, keepdims=True) + jnp.sum(p_b, axis=0, keepdims=True))
        return m_new, l_new, acc * alpha + (_mm(v_a, p_a) + _mm(v_b, p_b))

    init = (jnp.full((1, w4), M_INIT, F32), jnp.zeros((1, w4), F32), jnp.zeros((64, w4), F32))
    _, l_s, acc_s = lax.fori_loop(0, (n_act + 1) // 2, kv_pair, init)
    o_s = acc_s / jnp.maximum(l_s, 1e-30)
    o_t = part_ref[0, 0] + _gate_rows(g_ref[0, 0], 1, tq) * o_s
    o_ref[0] = jnp.concatenate([o_t[:, g * tq:(g + 1) * tq].T for g in range(NSA_GROUP)], axis=1)


def nsa_selected(tile_list, slot_list, tile_cnt, qt, gates, sel, ks, vst, part, *, tq, tk, pos0_fn):
    b, _, nq = qt.shape
    nk = ks.shape[1]
    nb = sel.shape[2]
    nt = nb * SEL_BLK // tk
    w4 = NSA_GROUP * tq
    full = lambda s1, s2: pl.BlockSpec((1, s1, s2), lambda bi, i, k, *_: (bi, 0, 0))
    step = lambda s1, s2: pl.BlockSpec((1, 1, s1, s2), lambda bi, i, k, *_: (bi, i * NSA_KV_HEADS + k, 0, 0))
    grid_spec = pltpu.PrefetchScalarGridSpec(
        num_scalar_prefetch=3,
        grid=(b, nq // tq, NSA_KV_HEADS),
        in_specs=[pl.BlockSpec((1, 256, tq), lambda bi, i, k, *_: (bi, k, i)),
                  pl.BlockSpec((1, 1, 16, tq), lambda bi, i, k, *_: (bi, k, 0, i)),
                  step(nb, tq), full(nk, 128), full(128, nk), step(64, w4)],
        out_specs=pl.BlockSpec((1, tq, 256), lambda bi, i, k, *_: (bi, i, k)),
    )
    return pl.pallas_call(
        functools.partial(_nsa_selected_body, tq=tq, tk=tk, nt=nt, pos0_fn=pos0_fn),
        grid_spec=grid_spec,
        out_shape=jax.ShapeDtypeStruct((b, nq, 512), F32),
        compiler_params=_cparams(("arbitrary", "arbitrary", "arbitrary")),
        name="nsa_selected",
    )(tile_list, slot_list, tile_cnt, qt, gates, sel, ks, vst, part)


def _active_first(active):
    order = jnp.argsort(jnp.where(active, 0, 1), axis=-1, stable=True).astype(jnp.int32)
    return order, jnp.sum(active, axis=-1).astype(jnp.int32)


def nsa_attention(qt, gates, kvc, kvct, ks, vst, kw, vwt, *, tq, tk, wk, pos0_fn, wstart_fn, wpos0_fn):
    nb = ks.shape[1] // SEL_BLK
    part, sel, flags = nsa_select(qt, gates, kvc, kvct, kw, vwt, nb=nb, tq=tq, tk=tk, wk=wk, pos0_fn=pos0_fn,
                                  wstart_fn=wstart_fn, wpos0_fn=wpos0_fn)
    order, cnt = _active_first(flags[..., 0] > 0.5)
    return nsa_selected(order.reshape(-1), order.reshape(-1), cnt.reshape(-1), qt, gates, sel, ks, vst, part,
                        tq=tq, tk=tk, pos0_fn=pos0_fn)


def nsa_attention_paged(qt, gates, kvc, kvct, pool_t, page_table, tail, kw, vwt, *, tq, tk, wk, pos0_fn, wstart_fn,
                        wpos0_fn):
    nb = (page_table.shape[1] * pool_t.shape[2] + tk) // SEL_BLK
    part, sel, flags = nsa_select(qt, gates, kvc, kvct, kw, vwt, nb=nb, tq=tq, tk=tk, wk=wk, pos0_fn=pos0_fn,
                                  wstart_fn=wstart_fn, wpos0_fn=wpos0_fn)
    active = flags[..., 0] > 0.5
    tiles_b, cnt_b = _active_first(jnp.any(active, axis=1))
    slot_of_tile = jnp.argsort(tiles_b, axis=-1).astype(jnp.int32)
    order, cnt = _active_first(active)
    slots = jnp.take_along_axis(jnp.broadcast_to(slot_of_tile[:, None, :], order.shape), order, axis=-1)

    def run(n_slots):
        ks, vst = gather_sel(pool_t, page_table, tail, tk, tiles_b.reshape(-1), cnt_b, n_slots)
        return nsa_selected(order.reshape(-1), slots.reshape(-1), cnt.reshape(-1), qt, gates, sel, ks, vst, part,
                            tq=tq, tk=tk, pos0_fn=pos0_fn)

    nt = tiles_b.shape[-1]
    few = min(SAMPLE_TILE_SLOTS, nt)
    return lax.cond(jnp.max(cnt_b) <= few, lambda: run(few), lambda: run(nt))


def _tri_inverse(ms, c):
    eye = (lax.broadcasted_iota(jnp.int32, (c, c), 0) == lax.broadcasted_iota(jnp.int32, (c, c), 1)).astype(F32)
    ps = [-m for m in ms]
    ts = [eye + p for p in ps]
    steps = max(int(math.ceil(math.log2(c))) - 1, 0)
    d = lambda x, y: lax.dot_general(x, y, NN, preferred_element_type=F32)
    for _ in range(steps):
        sp = [_split(p) for p in ps]
        ps = [d(ph, ph) + (d(ph, pl_) + d(pl_, ph)) for ph, pl_ in sp]
        sp = [_split(p) for p in ps]
        st = [_split(t) for t in ts]
        ts = [t + (d(th, ph) + (d(th, pl_) + d(tl, ph))) for t, (th, tl), (ph, pl_) in zip(ts, st, sp)]
    return ts


def _rwkv_body(rw_ref, rw0_ref, s0_ref, mu_ref, vec_ref, w2_ref, a2_ref, g2_ref, seg_ref, rk_ref,
               o_ref, sfin_ref, buf_ref, s_ref, y_ref, *, c, valid, n_chunks):
    ci = pl.program_id(1)
    halo = 8

    @pl.when(ci == 0)
    def _():
        buf_ref[0:halo, :] = rw0_ref[0]
        s_ref[...] = s0_ref[0]

    cur = rw_ref[...]
    buf_ref[halo:halo + c, :] = cur
    prev = buf_ref[halo - 1:halo - 1 + c, :]
    xr = cur + (prev - cur) * mu_ref[...]
    buf_ref[0:halo, :] = cur[c - halo:, :]

    vec = vec_ref[...]
    w0, a0, kkw, kaw, ln_w, ln_b = (vec[r:r + 1, :] for r in range(6))
    r = xr[:, 0:512]
    kx = xr[:, 512:1024]
    v = xr[:, 1024:1536]
    xw = xr[:, 1536:1664]
    xa = xr[:, 1664:1792]
    xg = xr[:, 1792:1920]
    wl = -jnp.exp(-_softplus(-(w0 + _mm(jnp.tanh(xw), w2_ref[...]))) - 0.5)
    a = _sigmoid(a0 + _mm(xa, a2_ref[...]))
    gate = _mm(_sigmoid(xg), g2_ref[...])
    seg = seg_ref[...]
    zk = kx * kkw
    k2 = kx * (1.0 + (a - 1.0) * kaw)
    zz_sum, rk_sum = _head_sums([zk * zk, r * k2 * rk_ref[...]], seg)
    kk = zk * lax.rsqrt(zz_sum + EPS)
    bonus = rk_sum * v
    if valid < c:
        live = lax.broadcasted_iota(jnp.int32, (c, 1), 0) < valid
        wl = jnp.where(live, wl, 0.0)
        kk = jnp.where(live, kk, 0.0)
        k2 = jnp.where(live, k2, 0.0)
        v = jnp.where(live, v, 0.0)
        r = jnp.where(live, r, 0.0)
    bb = kk * a

    ri = lax.broadcasted_iota(jnp.int32, (c, c), 0)
    cj = lax.broadcasted_iota(jnp.int32, (c, c), 1)
    tril = ri >= cj
    strict = ri > cj
    cw = _cumsum_rows(wl)
    ecw = jnp.exp(cw)
    einv = jnp.exp(-cw)
    p_c = ecw[c - 1:c, :]
    kt = kk * jnp.exp(cw - wl)
    bt = bb * einv
    ki = k2 * einv
    rt = r * ecw
    bd = bt * p_c
    kd = ki * p_c

    heads = range(RWKV_HEADS)
    sls = [slice(h * RWKV_HD, (h + 1) * RWKV_HD) for h in heads]
    kt_h = [kt[:, sl] for sl in sls]
    bt_h = [bt[:, sl] for sl in sls]
    ki_h = [ki[:, sl] for sl in sls]
    rt_h = [rt[:, sl] for sl in sls]
    v_h = [v[:, sl] for sl in sls]
    l_m = [jnp.where(strict, _mm3(kt_h[h], bt_h[h], NT), 0.0) for h in heads]
    m_kk = [jnp.where(strict, _mm(kt_h[h], ki_h[h], NT), 0.0) for h in heads]
    a_rb = [jnp.where(tril, _mm(rt_h[h], bt_h[h], NT), 0.0) for h in heads]
    a_rk = [jnp.where(tril, _mm(rt_h[h], ki_h[h], NT), 0.0) for h in heads]
    mv = [_mm(m_kk[h], v_h[h]) for h in heads]
    y0 = [_mm(a_rk[h], v_h[h]) for h in heads]
    t_inv = _tri_inverse(l_m, c)
    w_h = [_mm3(t_inv[h], kt_h[h]) for h in heads]
    u_h = [-_mm3(t_inv[h], mv[h]) for h in heads]
    s_h = [s_ref[h] for h in heads]
    e_h = [u_h[h] - _mm(w_h[h], s_h[h], NT) for h in heads]
    y1 = [_mm(rt_h[h], s_h[h], NT) + y0[h] for h in heads]
    y_h = [y1[h] + _mm(a_rb[h], e_h[h]) for h in heads]
    ds = [_mm(e_h[h], bd[:, sls[h]], TN) + _mm(v_h[h], kd[:, sls[h]], TN) for h in heads]
    for h in heads:
        s_ref[h] = s_h[h] * p_c[:, sls[h]] + ds[h]
        mu_h = jnp.mean(y_h[h], axis=-1, keepdims=True)
        d_h = y_h[h] - mu_h
        var_h = jnp.mean(d_h * d_h, axis=-1, keepdims=True)
        y_ref[:, sls[h]] = d_h * lax.rsqrt(var_h + RWKV_GN_EPS)

    o_ref[...] = (y_ref[...] * ln_w + ln_b + bonus) * gate

    @pl.when(ci == n_chunks - 1)
    def _():
        sfin_ref[0] = s_ref[...]


def rwkv_mix(rw, rw0, s0, mu, vec, w2, a2, g2, seg, rk, *, c, valid):
    b = s0.shape[0]
    rows = rw.shape[0]
    n_chunks = rows // (b * c)
    const = lambda s: pl.BlockSpec(s, lambda bi, ci: tuple(0 for _ in s))
    return pl.pallas_call(
        functools.partial(_rwkv_body, c=c, valid=valid, n_chunks=n_chunks),
        grid=(b, n_chunks),
        in_specs=[pl.BlockSpec((c, RW_COLS), lambda bi, ci: (bi * n_chunks + ci, 0)),
                  pl.BlockSpec((1, 8, RW_COLS), lambda bi, ci: (bi, 0, 0)),
                  pl.BlockSpec((1, RWKV_HEADS, 64, 64), lambda bi, ci: (bi, 0, 0, 0)),
                  const((1, RW_COLS)), const((8, 512)), const((128, 512)), const((128, 512)), const((128, 512)),
                  const((RWKV_W // 2, RWKV_W // 2)), const((1, 512))],
        out_specs=[pl.BlockSpec((c, 512), lambda bi, ci: (bi * n_chunks + ci, 0)),
                   pl.BlockSpec((1, RWKV_HEADS, 64, 64), lambda bi, ci: (bi, 0, 0, 0))],
        out_shape=[jax.ShapeDtypeStruct((rows, 512), F32),
                   jax.ShapeDtypeStruct((b, RWKV_HEADS, 64, 64), F32)],
        scratch_shapes=[pltpu.VMEM((8 + c, RW_COLS), F32), pltpu.VMEM((RWKV_HEADS, 64, 64), F32),
                        pltpu.VMEM((c, 512), F32)],
        compiler_params=_cparams(("arbitrary", "arbitrary")),
        name="rwkv_mix",
    )(rw, rw0, s0, mu, vec, w2, a2, g2, seg, rk)


def _out_proj_body(*refs, n_in):
    a_refs = refs[:n_in]
    w_refs = refs[n_in:2 * n_in]
    x_ref, g_ref, o_ref = refs[2 * n_in:]
    y = _mm(a_refs[0][...], w_refs[0][...])
    for a_ref, w_ref in zip(a_refs[1:], w_refs[1:]):
        y = y + _mm(a_ref[...], w_ref[...])
    o_ref[...] = x_ref[...] + g_ref[...] * y


def out_proj(acts, weights, x, gate, tm):
    n, d = x.shape
    n_in = len(acts)
    return pl.pallas_call(
        functools.partial(_out_proj_body, n_in=n_in),
        grid=(n // tm,),
        in_specs=[pl.BlockSpec((tm, a.shape[1]), lambda i: (i, 0)) for a in acts]
        + [pl.BlockSpec(w.shape, lambda i: (0, 0)) for w in weights]
        + [pl.BlockSpec((tm, d), lambda i: (i, 0)), _mod_spec(gate.shape[0], tm, d)],
        out_specs=pl.BlockSpec((tm, d), lambda i: (i, 0)),
        out_shape=jax.ShapeDtypeStruct((n, d), F32),
        compiler_params=_cparams(("arbitrary",)),
        name="out_proj",
    )(*acts, *weights, x, gate)


def _odd_proj_body(x_ref, nw_ref, sc_ref, sh_ref, w_ref, qkv_ref, z_ref, ba_ref):
    hb = _norm_mod(x_ref[...], nw_ref[...], sc_ref[...], sh_ref[...]).astype(BF16)
    qkv_ref[...] = _mm(hb, w_ref[:, 0:3072])
    z_ref[...] = _mm(hb, w_ref[:, 3072:4096])
    ba_ref[...] = _mm(hb, w_ref[:, 4096:O_COLS])


def odd_proj(x, nw, sc, sh, w_packed, tm):
    n, d = x.shape
    rows_mod = sc.shape[0]
    row = lambda c: pl.BlockSpec((tm, c), lambda i: (i, 0))
    return pl.pallas_call(
        _odd_proj_body,
        grid=(n // tm,),
        in_specs=[row(d), pl.BlockSpec((1, d), lambda i: (0, 0)),
                  _mod_spec(rows_mod, tm, d), _mod_spec(rows_mod, tm, d),
                  pl.BlockSpec((d, O_COLS), lambda i: (0, 0))],
        out_specs=[row(3072), row(1024), row(128)],
        out_shape=[jax.ShapeDtypeStruct((n, 3072), F32), jax.ShapeDtypeStruct((n, 1024), F32),
                   jax.ShapeDtypeStruct((n, 128), F32)],
        compiler_params=_cparams(("arbitrary",)),
        name="odd_proj",
    )(x, nw, sc, sh, w_packed)


def _gdn_body(qkv_ref, z_ref, ba_ref, cs_ref, s0_ref, cw_ref, hp_ref, nw_ref,
              o_ref, sfin_ref, buf_ref, s_ref, *, c, valid, n_chunks):
    ci = pl.program_id(1)
    halo = 8

    @pl.when(ci == 0)
    def _():
        buf_ref[0:halo, :] = cs_ref[0]
        s_ref[...] = s0_ref[0]

    x = qkv_ref[...]
    buf_ref[halo:halo + c, :] = x
    cw = cw_ref[...]
    conv = buf_ref[halo - 3:halo - 3 + c, :] * cw[0:1, :]
    for j in range(1, CONV_W):
        conv = conv + buf_ref[halo - 3 + j:halo - 3 + j + c, :] * cw[j:j + 1, :]
    buf_ref[0:halo, :] = x[c - halo:, :]
    conv = _silu(conv)

    hp = hp_ref[...]
    ba = ba_ref[...]
    beta_f = _sigmoid(ba)
    g_f = hp[0:1, :] * _softplus(ba + hp[1:2, :])
    if valid < c:
        live = lax.broadcasted_iota(jnp.int32, (c, 1), 0) < valid
        beta_f = jnp.where(live, beta_f, 0.0)
        g_f = jnp.where(live, g_f, 0.0)
        conv = jnp.where(live, conv, 0.0)

    ri = lax.broadcasted_iota(jnp.int32, (c, c), 0)
    cj = lax.broadcasted_iota(jnp.int32, (c, c), 1)
    tril = ri >= cj
    strict = ri > cj
    gc = _cumsum_rows(g_f)
    gct = gc.T
    z = z_ref[...]
    nw = nw_ref[...]

    heads = range(GDN_HEADS)
    sls = [slice(h * GDN_HD, (h + 1) * GDN_HD) for h in heads]
    q_h = [conv[:, sl] for sl in sls]
    k_h = [conv[:, GDN_W + h * GDN_HD:GDN_W + (h + 1) * GDN_HD] for h in heads]
    v_h = [conv[:, 2 * GDN_W + h * GDN_HD:2 * GDN_W + (h + 1) * GDN_HD] for h in heads]
    q_h = [q * lax.rsqrt(jnp.sum(q * q, axis=-1, keepdims=True) + EPS) * (GDN_HD ** -0.5) for q in q_h]
    k_h = [k * lax.rsqrt(jnp.sum(k * k, axis=-1, keepdims=True) + EPS) for k in k_h]
    g_col = [gc[:, 8 + h:9 + h] for h in heads]
    eg = [jnp.exp(g) for g in g_col]
    b_col = [beta_f[:, h:h + 1] for h in heads]
    decay = [jnp.where(tril, jnp.exp(jnp.where(tril, g_col[h] - gct[8 + h:9 + h, :], 0.0)), 0.0) for h in heads]
    kb = [k_h[h] * b_col[h] for h in heads]
    vb = [v_h[h] * b_col[h] for h in heads]
    m_h = [jnp.where(strict, _mm3(kb[h], k_h[h], NT) * decay[h], 0.0) for h in heads]
    qk = [jnp.where(tril, _mm(q_h[h], k_h[h], NT) * decay[h], 0.0) for h in heads]
    t_inv = _tri_inverse(m_h, c)
    u_h = [_mm(t_inv[h], vb[h]) for h in heads]
    w_h = [_mm(t_inv[h], kb[h] * eg[h]) for h in heads]
    s_h = [s_ref[h] for h in heads]
    v_new = [u_h[h] - _mm(w_h[h], s_h[h]) for h in heads]
    o1 = [_mm(q_h[h] * eg[h], s_h[h]) for h in heads]
    o_h = [o1[h] + _mm(qk[h], v_new[h]) for h in heads]
    g_last = [g[c - 1:c, :] for g in g_col]
    ds = [_mm(k_h[h] * jnp.exp(g_last[h] - g_col[h]), v_new[h], TN) for h in heads]
    for h in heads:
        s_ref[h] = s_h[h] * jnp.exp(g_last[h]) + ds[h]
        o_n = o_h[h] * lax.rsqrt(jnp.mean(o_h[h] * o_h[h], axis=-1, keepdims=True) + EPS) * nw
        o_ref[:, sls[h]] = o_n * _silu(z[:, sls[h]])

    @pl.when(ci == n_chunks - 1)
    def _():
        sfin_ref[0] = s_ref[...]


def gdn_mix(qkv, z, ba, cs, s0, conv_w8, hp, nw, *, c, valid):
    b = s0.shape[0]
    rows = qkv.shape[0]
    n_chunks = rows // (b * c)
    const = lambda s: pl.BlockSpec(s, lambda bi, ci: tuple(0 for _ in s))
    row = lambda w: pl.BlockSpec((c, w), lambda bi, ci: (bi * n_chunks + ci, 0))
    return pl.pallas_call(
        functools.partial(_gdn_body, c=c, valid=valid, n_chunks=n_chunks),
        grid=(b, n_chunks),
        in_specs=[row(3072), row(1024), row(128),
                  pl.BlockSpec((1, 8, 3072), lambda bi, ci: (bi, 0, 0)),
                  pl.BlockSpec((1, GDN_HEADS, 128, 128), lambda bi, ci: (bi, 0, 0, 0)),
                  const((8, 3072)), const((8, 128)), const((1, 128))],
        out_specs=[row(1024), pl.BlockSpec((1, GDN_HEADS, 128, 128), lambda bi, ci: (bi, 0, 0, 0))],
        out_shape=[jax.ShapeDtypeStruct((rows, 1024), F32),
                   jax.ShapeDtypeStruct((b, GDN_HEADS, 128, 128), F32)],
        scratch_shapes=[pltpu.VMEM((8 + c, 3072), F32), pltpu.VMEM((GDN_HEADS, 128, 128), F32)],
        compiler_params=_cparams(("arbitrary", "arbitrary")),
        name="gdn_mix",
    )(qkv, z, ba, cs, s0, conv_w8, hp, nw)


def _router_body(x_ref, nw_ref, sc_ref, sh_ref, wr_ref, br_ref, h_ref, gate_ref):
    h = _norm_mod(x_ref[...], nw_ref[...], sc_ref[...], sh_ref[...])
    h_ref[...] = h.astype(BF16)
    logits = _mmh(h, wr_ref[...]) + br_ref[...]
    tm = logits.shape[0]
    lane = lax.broadcasted_iota(jnp.int32, (tm, LANE), 1)
    is_grp = (lane >= N_EXPERTS) & (lane < N_EXPERTS + N_GROUPS)
    gl = jnp.where(is_grp, logits, NEG)
    gmax = jnp.max(gl, axis=-1, keepdims=True)
    g_idx = jnp.min(jnp.where(gl == gmax, lane, 4 * LANE), axis=-1, keepdims=True) - N_EXPERTS
    g_w = 1.0 / jnp.sum(jnp.where(is_grp, jnp.exp(gl - gmax), 0.0), axis=-1, keepdims=True)
    in_grp = (lane < N_EXPERTS) & (lane // EXP_PER_GROUP == g_idx)
    el = jnp.where(in_grp, logits, NEG)
    emax = jnp.max(el, axis=-1, keepdims=True)
    e = jnp.where(in_grp, jnp.exp(el - emax), 0.0)
    p = e / jnp.sum(e, axis=-1, keepdims=True)
    p1 = jnp.where(in_grp, p, -1.0)
    m1 = jnp.max(p1, axis=-1, keepdims=True)
    i1 = jnp.min(jnp.where(p1 == m1, lane, 4 * LANE), axis=-1, keepdims=True)
    p2 = jnp.where(lane == i1, -1.0, p1)
    m2 = jnp.max(p2, axis=-1, keepdims=True)
    i2 = jnp.min(jnp.where(p2 == m2, lane, 4 * LANE), axis=-1, keepdims=True)
    tot = m1 + m2
    gate = jnp.where(lane == i1, m1 / tot * g_w, jnp.where(lane == i2, m2 / tot * g_w, 0.0))
    gate_ref[...] = jnp.where(lane == GRP_LANE, g_idx.astype(F32), gate)


def moe_router(x, nw, sc, sh, w_r, b_r, tm):
    n, d = x.shape
    rows_mod = sc.shape[0]
    return pl.pallas_call(
        _router_body,
        grid=(n // tm,),
        in_specs=[pl.BlockSpec((tm, d), lambda i: (i, 0)), pl.BlockSpec((1, d), lambda i: (0, 0)),
                  _mod_spec(rows_mod, tm, d), _mod_spec(rows_mod, tm, d),
                  pl.BlockSpec((d, LANE), lambda i: (0, 0)), pl.BlockSpec((1, LANE), lambda i: (0, 0))],
        out_specs=[pl.BlockSpec((tm, d), lambda i: (i, 0)), pl.BlockSpec((tm, LANE), lambda i: (i, 0))],
        out_shape=[jax.ShapeDtypeStruct((n, d), BF16), jax.ShapeDtypeStruct((n, LANE), F32)],
        compiler_params=_cparams(("arbitrary",)),
        name="moe_router",
    )(x, nw, sc, sh, w_r, b_r)


def _moe_body(h_ref, gate_ref, w1_ref, w3_ref, w2_ref, x_ref, g2_ref, o_ref, acc_ref):
    e = pl.program_id(1)

    @pl.when(e == 0)
    def _():
        acc_ref[...] = jnp.zeros_like(acc_ref)

    hb = h_ref[...]
    he = _silu(_mm(hb, w1_ref[0])) * _mm(hb, w3_ref[0])
    y = _mm(he, w2_ref[0])
    gate = gate_ref[...]
    lane = lax.broadcasted_iota(jnp.int32, gate.shape, 1)
    ge = jnp.sum(jnp.where(lane == e, gate, 0.0), axis=-1, keepdims=True)
    acc_ref[...] += ge * y

    @pl.when(e == pl.num_programs(1) - 1)
    def _():
        o_ref[...] = x_ref[...] + g2_ref[...] * acc_ref[...]


def moe_ffn(h, gate, w1, w3, w2, e0, x, g2, tm):
    n, d = x.shape
    de = w1.shape[2]
    return pl.pallas_call(
        _moe_body,
        grid=(n // tm, N_EXPERTS),
        in_specs=[pl.BlockSpec((tm, d), lambda i, e: (i, 0)), pl.BlockSpec((tm, LANE), lambda i, e: (i, 0)),
                  pl.BlockSpec((1, d, de), lambda i, e: (e0 + e, 0, 0)),
                  pl.BlockSpec((1, d, de), lambda i, e: (e0 + e, 0, 0)),
                  pl.BlockSpec((1, de, d), lambda i, e: (e0 + e, 0, 0)),
                  pl.BlockSpec((tm, d), lambda i, e: (i, 0)),
                  pl.BlockSpec((1, d), lambda i, e: (0, 0)) if g2.shape[0] == 1
                  else pl.BlockSpec((tm, d), lambda i, e: (i, 0))],
        out_specs=pl.BlockSpec((tm, d), lambda i, e: (i, 0)),
        out_shape=jax.ShapeDtypeStruct((n, d), F32),
        scratch_shapes=[pltpu.VMEM((tm, d), F32)],
        compiler_params=_cparams(("arbitrary", "arbitrary")),
        name="moe_ffn",
    )(h, gate, w1, w3, w2, x, g2)


def _moe_plan(grp, tm, tw, tb, cap, max_entries):
    nt = grp.shape[0] // tm
    cnt = jax.nn.one_hot(grp, N_GROUPS, dtype=jnp.int32).reshape(nt, tm, N_GROUPS).sum(axis=1)
    pc = (cnt + ROW_ALIGN - 1) // ROW_ALIGN * ROW_ALIGN
    segb = jnp.cumsum(pc, axis=1) - pc
    off = jnp.cumsum(pc, axis=0) - pc
    tot = pc.sum(axis=0)
    n_real = (tot + tb - 1) // tb
    n_all = jnp.minimum((tot + tw + tb - 1) // tb, cap // tb)
    ends = jnp.cumsum(n_all)
    s = jnp.arange(max_entries)
    g_of = jnp.sum(s[:, None] >= ends[None, :], axis=1)
    active = g_of < N_GROUPS
    g_c = jnp.minimum(g_of, N_GROUPS - 1)
    rt = s - (ends - n_all)[g_c]
    live = tot[g_c] - rt * tb
    real = jnp.where(live <= tb // 4, 3, jnp.where(live <= tb // 2, 5, 1))
    kind = jnp.where(active, jnp.where(rt < n_real[g_c], real, 2), 0)
    last = ends[-1] - 1
    e_grp = jnp.where(active, g_c, g_c[last])
    e_rt = jnp.where(active, rt, rt[last])
    i32 = lambda a: a.reshape(-1).astype(jnp.int32)
    return i32(segb), i32(off // ROW_ALIGN), i32(e_grp), i32(e_rt), i32(kind)


def _group_perm(gate, segb_ref, base, tm, rows):
    gt = gate.T
    grp = gt[GRP_LANE:GRP_LANE + 1, :]
    gi = lax.broadcasted_iota(jnp.int32, (8, tm), 0).astype(F32)
    oh = jnp.where(gi == grp, 1.0, 0.0)
    r_i = lax.broadcasted_iota(jnp.int32, (tm, tm), 0)
    c_i = lax.broadcasted_iota(jnp.int32, (tm, tm), 1)
    before = jnp.where(r_i < c_i, 1.0, 0.0).astype(BF16)
    rank = lax.dot_general(oh.astype(BF16), before, NN, preferred_element_type=F32)
    dest = jnp.zeros((1, tm), F32)
    for g in range(N_GROUPS):
        dest = dest + oh[g:g + 1] * (segb_ref[base + g].astype(F32) + rank[g:g + 1])
    rows_i = lax.broadcasted_iota(jnp.int32, (rows, tm), 0).astype(F32)
    return jnp.where(rows_i == dest, 1.0, 0.0).astype(BF16)


def _moe_dispatch_body(segb_ref, off_ref, h_ref, gate_ref, xg_in, gg_in, xg_ref, gg_ref, xs_ref, gs_ref, *, tm, tw,
                       rows):
    del off_ref, xg_in, gg_in
    i = pl.program_id(0)
    g = pl.program_id(1)

    @pl.when((i == 0) & (g == 0))
    def _():
        xs_ref[...] = jnp.zeros_like(xs_ref)
        gs_ref[...] = jnp.zeros_like(gs_ref)

    @pl.when(g == 0)
    def _():
        gate = gate_ref[...]
        p = _group_perm(gate, segb_ref, i * N_GROUPS, tm, rows)
        xs_ref[0:rows, :] = lax.dot_general(p, h_ref[...], NN, preferred_element_type=F32).astype(BF16)
        gs_ref[0:rows, :] = _mm01(p, gate)

    start = pl.multiple_of(segb_ref[i * N_GROUPS + g], ROW_ALIGN)
    xg_ref[...] = xs_ref[pl.ds(start, tw), :]
    gg_ref[...] = gs_ref[pl.ds(start, tw), :]


def moe_dispatch(h, gate, segb, off, tm, tw, cap):
    n, d = h.shape
    rows = tm + N_GROUPS * ROW_ALIGN
    win = lambda w: pl.BlockSpec((pl.Element(tw), pl.Element(w)),
                                 lambda i, g, segb, off: ((g * (cap // ROW_ALIGN) + off[i * N_GROUPS + g]) * ROW_ALIGN, 0))
    grid_spec = pltpu.PrefetchScalarGridSpec(
        num_scalar_prefetch=2,
        grid=(n // tm, N_GROUPS),
        in_specs=[pl.BlockSpec((tm, d), lambda i, g, *_: (i, 0)), pl.BlockSpec((tm, LANE), lambda i, g, *_: (i, 0)),
                  pl.BlockSpec(memory_space=pl.ANY), pl.BlockSpec(memory_space=pl.ANY)],
        out_specs=[win(d), win(LANE)],
        scratch_shapes=[pltpu.VMEM((rows + tw, d), BF16), pltpu.VMEM((rows + tw, LANE), F32)],
    )
    return pl.pallas_call(
        functools.partial(_moe_dispatch_body, tm=tm, tw=tw, rows=rows),
        grid_spec=grid_spec,
        out_shape=[jax.ShapeDtypeStruct((N_GROUPS * cap, d), BF16), jax.ShapeDtypeStruct((N_GROUPS * cap, LANE), F32)],
        input_output_aliases={4: 0, 5: 1},
        compiler_params=_cparams(("arbitrary", "arbitrary")),
        name="moe_dispatch",
    )(segb, off, h, gate, jnp.zeros((N_GROUPS * cap, d), BF16), jnp.zeros((N_GROUPS * cap, LANE), F32))


def _moe_group_body(grp_ref, rt_ref, kind_ref, xg_ref, gg_ref, w1_ref, w3_ref, w2_ref, yg_ref, acc_ref):
    del rt_ref
    s = pl.program_id(0)
    e = pl.program_id(1)
    kind = kind_ref[s]
    last = e == pl.num_programs(1) - 1

    def run(rows):
        @pl.when(e == 0)
        def _():
            acc_ref[...] = jnp.zeros_like(acc_ref)

        xb = xg_ref[0:rows, :]
        he = _silu(_mm(xb, w1_ref[0])) * _mm(xb, w3_ref[0])
        y = _mm(he, w2_ref[0])
        gate = gg_ref[0:rows, :]
        lane = lax.broadcasted_iota(jnp.int32, gate.shape, 1)
        ge = jnp.sum(jnp.where(lane == grp_ref[s] * EXP_PER_GROUP + e, gate, 0.0), axis=-1, keepdims=True)
        acc_ref[0:rows, :] += ge * y

        @pl.when(last)
        def _():
            yg_ref[...] = acc_ref[...]

    tb = xg_ref.shape[0]
    for code, rows in ((1, tb), (5, tb // 2), (3, tb // 4)):
        pl.when(kind == code)(functools.partial(run, rows))

    @pl.when((kind == 2) & last)
    def _():
        yg_ref[...] = jnp.zeros_like(yg_ref)


def moe_group_ffn(e_grp, e_rt, e_kind, xg, gg, w1, w3, w2, e0, tb, cap):
    d = xg.shape[1]
    de = w1.shape[2]
    row = lambda s, e, grp, rt, kind: (grp[s] * (cap // tb) + rt[s], 0)
    wsel = lambda s, e, grp, rt, kind: (e0 + grp[s] * EXP_PER_GROUP + jnp.where(kind[s] % 2 == 1, e, EXP_PER_GROUP - 1),
                                        0, 0)
    grid_spec = pltpu.PrefetchScalarGridSpec(
        num_scalar_prefetch=3,
        grid=(e_grp.shape[0], EXP_PER_GROUP),
        in_specs=[pl.BlockSpec((tb, d), row), pl.BlockSpec((tb, LANE), row),
                  pl.BlockSpec((1, d, de), wsel), pl.BlockSpec((1, d, de), wsel), pl.BlockSpec((1, de, d), wsel)],
        out_specs=pl.BlockSpec((tb, d), row),
        scratch_shapes=[pltpu.VMEM((tb, d), F32)],
    )
    return pl.pallas_call(
        _moe_group_body,
        grid_spec=grid_spec,
        out_shape=jax.ShapeDtypeStruct((N_GROUPS * cap, d), F32),
        compiler_params=_cparams(("arbitrary", "arbitrary")),
        name="moe_group_ffn",
    )(e_grp, e_rt, e_kind, xg, gg, w1, w3, w2)


def _moe_combine_body(segb_ref, off_ref, yg_ref, gate_ref, x_ref, g2_ref, o_ref, ys_ref, *, tm, tw, rows):
    del off_ref
    i = pl.program_id(0)
    g = pl.program_id(1)

    @pl.when((i == 0) & (g == 0))
    def _():
        ys_ref[...] = jnp.zeros_like(ys_ref)

    start = pl.multiple_of(segb_ref[i * N_GROUPS + g], ROW_ALIGN)
    ys_ref[pl.ds(start, tw), :] = yg_ref[...]

    @pl.when(g == N_GROUPS - 1)
    def _():
        p = _group_perm(gate_ref[...], segb_ref, i * N_GROUPS, tm, rows)
        yh, yl = _split(ys_ref[0:rows, :])
        y = (lax.dot_general(p, yh, TN, preferred_element_type=F32)
             + lax.dot_general(p, yl, TN, preferred_element_type=F32))
        o_ref[...] = x_ref[...] + g2_ref[...] * y


def moe_combine(yg, gate, x, g2, segb, off, tm, tw, cap):
    n, d = x.shape
    rows = tm + N_GROUPS * ROW_ALIGN
    grid_spec = pltpu.PrefetchScalarGridSpec(
        num_scalar_prefetch=2,
        grid=(n // tm, N_GROUPS),
        in_specs=[pl.BlockSpec((pl.Element(tw), pl.Element(d)),
                               lambda i, g, segb, off: ((g * (cap // ROW_ALIGN) + off[i * N_GROUPS + g]) * ROW_ALIGN, 0)),
                  pl.BlockSpec((tm, LANE), lambda i, g, *_: (i, 0)),
                  pl.BlockSpec((tm, d), lambda i, g, *_: (i, 0)),
                  pl.BlockSpec((1, d), lambda i, g, *_: (0, 0))],
        out_specs=pl.BlockSpec((tm, d), lambda i, g, *_: (i, 0)),
        scratch_shapes=[pltpu.VMEM((rows + tw, d), F32)],
    )
    return pl.pallas_call(
        functools.partial(_moe_combine_body, tm=tm, tw=tw, rows=rows),
        grid_spec=grid_spec,
        out_shape=jax.ShapeDtypeStruct((n, d), F32),
        compiler_params=_cparams(("arbitrary", "arbitrary")),
        name="moe_combine",
    )(segb, off, yg, gate, x, g2)


def moe_grouped(h, gate, w1, w3, w2, e0, x, g2, tm):
    n = h.shape[0]
    tb = tm
    cap = n + 2 * tm
    max_entries = (n + (n // tm) * N_GROUPS * (ROW_ALIGN - 1) + N_GROUPS * tm) // tb + N_GROUPS + 1
    grp = gate[:, GRP_LANE].astype(jnp.int32)

    def run(tw):
        segb, off, e_grp, e_rt, e_kind = _moe_plan(grp, tm, tw, tb, cap, max_entries)
        xg, gg = moe_dispatch(h, gate, segb, off, tm, tw, cap)
        yg = moe_group_ffn(e_grp, e_rt, e_kind, xg, gg, w1, w3, w2, e0, tb, cap)
        return moe_combine(yg, gate, x, g2, segb, off, tm, tw, cap)

    seg_max = jnp.max(jax.nn.one_hot(grp, N_GROUPS, dtype=jnp.int32).reshape(n // tm, tm, N_GROUPS).sum(axis=1))
    return lax.cond(seg_max <= tm // 2, lambda: run(tm // 2), lambda: run(tm))


def _final_norm_body(x_ref, w_ref, o_ref):
    x = x_ref[...]
    o_ref[...] = x * lax.rsqrt(jnp.mean(x * x, axis=-1, keepdims=True) + EPS) * w_ref[...]


def final_norm(x, w, tm):
    n, d = x.shape
    return pl.pallas_call(
        _final_norm_body,
        grid=(n // tm,),
        in_specs=[pl.BlockSpec((tm, d), lambda i: (i, 0)), pl.BlockSpec((1, d), lambda i: (0, 0))],
        out_specs=pl.BlockSpec((tm, d), lambda i: (i, 0)),
        out_shape=jax.ShapeDtypeStruct((n, d), F32),
        compiler_params=_cparams(("arbitrary",)),
        name="final_norm",
    )(x, w)


def _row_tile(n, pref):
    t = min(pref, n)
    while n % t:
        t //= 2
    return t


def kernel(x_prompt, x_sample, c_prompt, c_sample, cache_nsa_cmp, cache_nsa_sel, page_table, state_nsa_win, state_rwkv, state_rwkv_shift, state_gdn, state_gdn_conv, norm_mix, norm_ffn, norm_final, w_ada, b_ada, even_w_in, even_w_out, nsa_cmp_pos, nsa_cmp_w, rwkv_mu, rwkv_w0, rwkv_w2, rwkv_a0, rwkv_a2, rwkv_g2, rwkv_kk, rwkv_ka, rwkv_rk, rwkv_ln_w, rwkv_ln_b, odd_w_in, odd_w_out, gdn_conv_w, gdn_a_log, gdn_dt_bias, gdn_norm_w, moe_w_grp, moe_b_grp, moe_w_exp, moe_b_exp, moe_w1, moe_w3, moe_w2):
    bp, t, d = x_prompt.shape
    bs, ts, _ = x_sample.shape
    assert bp == 1 and ts <= SPAD and ts < CMP_BLK
    depth = norm_mix.shape[0]
    n_pages, page = page_table.shape[1], cache_nsa_cmp.shape[2]
    past = n_pages * page
    wb = state_nsa_win.shape[2]
    ns = bs * SPAD
    tq, tq_s, tk = NSA_TQ, NSA_TQ_SAMPLE, NSA_TK
    tm_p = _row_tile(t, ROW_TILE)
    tm_s = ns

    rows_c = -(-(1 + bs) // 8) * 8
    c_all = jnp.concatenate([c_prompt, c_sample, jnp.zeros((rows_c - 1 - bs, d), F32)], axis=0)
    ada = adaln(c_all, w_ada, b_ada)

    def mods(i):
        mp = [ada[i, 0:1, j * d:(j + 1) * d] for j in range(6)]
        ms = [jnp.repeat(ada[i, 1:1 + bs, j * d:(j + 1) * d], SPAD, axis=0) for j in range(6)]
        return mp, ms

    xp = x_prompt[0]
    xs = jnp.pad(x_sample, ((0, 0), (0, SPAD - ts), (0, 0))).reshape(ns, d)

    def unpad(a):
        return a.reshape(bs, SPAD, -1)[:, :ts]

    w1_all, w3_all, w2_all = (w.reshape((-1,) + w.shape[2:]) for w in (moe_w1, moe_w3, moe_w2))
    outs = {k: [] for k in ("cmp_p", "cmp_s", "sel_p", "sel_s", "win_p", "win_s", "rw_p", "rw_s", "sh_p", "sh_s",
                            "gd_p", "gd_s", "cv_p", "cv_s")}

    for i in range(depth):
        (sh1p, sc1p, gt1p, sh2p, sc2p, gt2p), (sh1s, sc1s, gt1s, sh2s, sc2s, gt2s) = mods(i)
        j = i // 2
        nw = norm_mix[i][None, :]
        if i % 2 == 0:
            w_packed = _pack_even_w(even_w_in[j])
            mu = _pack_rw_vec(rwkv_mu[j])
            wts, wc = _cmp_weights(nsa_cmp_pos[j], nsa_cmp_w[j])
            vec = jnp.stack([rwkv_w0[j], rwkv_a0[j], rwkv_kk[j], rwkv_ka[j], rwkv_ln_w[j], rwkv_ln_b[j],
                             jnp.zeros_like(rwkv_w0[j]), jnp.zeros_like(rwkv_w0[j])])
            pad_lora = lambda w: jnp.concatenate([w, jnp.zeros((128 - w.shape[0], w.shape[1]), w.dtype)], axis=0)
            w2p, a2p, g2p = pad_lora(rwkv_w2[j]), pad_lora(rwkv_a2[j]), rwkv_g2[j]
            hid = jnp.arange(RWKV_W) // RWKV_HD
            seg = (hid[:, None] == hid[None, :]).astype(F32)[:RWKV_W // 2, :RWKV_W // 2]
            rk = rwkv_rk[j].reshape(1, RWKV_W)
            wo_nsa, wo_rw = even_w_out[j][:512].astype(BF16), even_w_out[j][512:].astype(BF16)

            kv, qt, gt, ks, vst, kw, vwt, rw, hl = even_proj(xp, nw, sc1p, sh1p, w_packed, tm_p, 8)
            kvc = compress_prompt(kv, wts, wc, tm_p)
            gates = gt[:24].reshape(NSA_KV_HEADS, 12, t)
            gates = jnp.pad(gates, ((0, 0), (0, 4), (0, 0)))[None]
            o_nsa = nsa_attention(
                qt[None], gates, kvc[None], kvc.T[None], ks[None], vst[None], kw[None], vwt[None],
                tq=tq, tk=tk, wk=WINDOW + tq,
                pos0_fn=lambda qi: qi * tq,
                wstart_fn=lambda qi: jnp.maximum(qi * tq - WINDOW, 0),
                wpos0_fn=lambda qi: jnp.maximum(qi * tq - WINDOW, 0))[0]
            o_rw, s_rw = rwkv_mix(rw, jnp.zeros((1, 8, RW_COLS), F32), jnp.zeros((1, RWKV_HEADS, 64, 64), F32),
                                  mu, vec, w2p, a2p, g2p, seg, rk, c=CHUNK, valid=CHUNK)
            xp = out_proj([o_nsa, o_rw], [wo_nsa, wo_rw], xp, gt1p, tm_p)
            outs["cmp_p"].append(kv[:, 0:256].reshape(1, t, 2, 2, 64))
            outs["sel_p"].append(kv[:, 256:512].reshape(1, t, 2, 2, 64))
            kvw_rows = kv[:, 512:768].reshape(1, t, 2, 2, 64)
            outs["win_p"].append(kvw_rows[:, -min(WINDOW, t):])
            outs["rw_p"].append(s_rw)
            outs["sh_p"].append(hl[-1:])

            kv, qt, gt, _, _, _, _, rw, hl = even_proj(xs, nw, sc1s, sh1s, w_packed, tm_s, ns)
            kv_new = unpad(kv)
            rw0 = small_matmul(jnp.pad(state_rwkv_shift[j], ((0, -bs % 8), (0, 0))), w_packed[:, E_RW:])[:bs]
            rw0 = jnp.pad(rw0[:, None, :], ((0, 0), (7, 0), (0, 0)))
            pool_cmp = cache_nsa_cmp[j].transpose(0, 2, 3, 4, 1).reshape(-1, 256, page)
            pool_sel = cache_nsa_sel[j].transpose(0, 2, 3, 4, 1).reshape(-1, 256, page)
            kvc_s = compress_paged(pool_cmp, page_table, nsa_cmp_pos[j], wc, math.gcd(n_pages, CMP_PAGES_PER_STEP))
            tail = jnp.pad(kv_new[:, :, 256:512], ((0, 0), (0, tk - ts), (0, 0)))
            wbuf = state_nsa_win[j].reshape(bs, wb, 256)
            kvw_all = jnp.concatenate([wbuf, kv_new[:, :, 512:768]], axis=1)
            wk_s = -(-(wb + ts) // 128) * 128
            kvw_pad = jnp.pad(kvw_all, ((0, 0), (0, wk_s - wb - ts), (0, 0)))
            kw_s = kvw_pad[:, :, :128].astype(BF16)
            vwt_s = jnp.swapaxes(kvw_pad[:, :, 128:], 1, 2).astype(BF16)
            qt_s = jnp.pad(qt.reshape(512, bs, SPAD).transpose(1, 0, 2), ((0, 0), (0, 0), (0, tq_s - SPAD)))
            g_s = gt[:24].reshape(NSA_KV_HEADS, 12, bs, SPAD).transpose(2, 0, 1, 3)
            g_s = jnp.pad(g_s, ((0, 0), (0, 0), (0, 4), (0, tq_s - SPAD)))
            o_nsa = nsa_attention_paged(
                qt_s, g_s, kvc_s, jnp.swapaxes(kvc_s, 1, 2), pool_sel, page_table, tail, kw_s, vwt_s,
                tq=tq_s, tk=tk, wk=wk_s,
                pos0_fn=lambda qi: past,
                wstart_fn=lambda qi: 0,
                wpos0_fn=lambda qi: past - wb)
            o_nsa = o_nsa[:, :SPAD].reshape(ns, 512)
            o_rw, s_rw = rwkv_mix(rw, rw0, state_rwkv[j], mu, vec, w2p, a2p, g2p, seg, rk, c=SPAD, valid=ts)
            xs = out_proj([o_nsa, o_rw], [wo_nsa, wo_rw], xs, gt1s, tm_s)
            outs["cmp_s"].append(kv_new[:, :, 0:256].reshape(bs, ts, 2, 2, 64))
            outs["sel_s"].append(kv_new[:, :, 256:512].reshape(bs, ts, 2, 2, 64))
            outs["win_s"].append(kvw_all[:, -wb:].reshape(bs, wb, 2, 2, 64))
            outs["rw_s"].append(s_rw)
            outs["sh_s"].append(hl.reshape(bs, SPAD, d)[:, ts - 1])
        else:
            w_in = odd_w_in[j]
            w_packed = jnp.concatenate([w_in, jnp.zeros((d, O_COLS - w_in.shape[1]), F32)], axis=1).astype(BF16)
            conv_w8 = jnp.pad(gdn_conv_w[j], ((0, 8 - CONV_W), (0, 0)))
            hp = jnp.zeros((8, 128), F32)
            hp = hp.at[0, 8:16].set(-jnp.exp(gdn_a_log[j])).at[1, 8:16].set(gdn_dt_bias[j])
            gnw = gdn_norm_w[j][None, :]
            wo = odd_w_out[j].astype(BF16)

            qkv, z, ba = odd_proj(xp, nw, sc1p, sh1p, w_packed, tm_p)
            o_g, s_g = gdn_mix(qkv, z, ba, jnp.zeros((1, 8, 3 * GDN_W), F32),
                               jnp.zeros((1, GDN_HEADS, GDN_HD, GDN_HD), F32), conv_w8, hp, gnw, c=CHUNK, valid=CHUNK)
            xp = out_proj([o_g], [wo], xp, gt1p, tm_p)
            outs["gd_p"].append(s_g)
            outs["cv_p"].append(qkv[None, -(CONV_W - 1):])

            qkv, z, ba = odd_proj(xs, nw, sc1s, sh1s, w_packed, tm_s)
            cs = jnp.pad(state_gdn_conv[j], ((0, 0), (8 - (CONV_W - 1), 0), (0, 0)))
            o_g, s_g = gdn_mix(qkv, z, ba, cs, state_gdn[j], conv_w8, hp, gnw, c=SPAD, valid=ts)
            xs = out_proj([o_g], [wo], xs, gt1s, tm_s)
            xpad = jnp.concatenate([state_gdn_conv[j], unpad(qkv)], axis=1)
            outs["gd_s"].append(s_g)
            outs["cv_s"].append(xpad[:, -(CONV_W - 1):])

        nwf = norm_ffn[i][None, :]
        w_r = jnp.concatenate([moe_w_exp[i], moe_w_grp[i], jnp.zeros((d, LANE - N_EXPERTS - N_GROUPS), F32)], axis=1)
        b_r = jnp.concatenate([moe_b_exp[i], moe_b_grp[i], jnp.zeros((LANE - N_EXPERTS - N_GROUPS,), F32)])[None, :]
        h2, gate = moe_router(xp, nwf, sc2p, sh2p, w_r, b_r, tm_p)
        xp = moe_grouped(h2, gate, w1_all, w3_all, w2_all, i * N_EXPERTS, xp, gt2p, _row_tile(t, MOE_ROW_TILE))
        h2, gate = moe_router(xs, nwf, sc2s, sh2s, w_r, b_r, tm_s)
        xs = moe_ffn(h2, gate, w1_all, w3_all, w2_all, i * N_EXPERTS, xs, gt2s, tm_s)

    nf = norm_final[None, :]
    y_prompt = final_norm(xp, nf, tm_p)[None]
    y_sample = unpad(final_norm(xs, nf, tm_s))
    st = lambda key: jnp.stack(outs[key])
    return (y_prompt, y_sample, st("cmp_p"), st("cmp_s"), st("sel_p"), st("sel_s"), st("win_p"), st("win_s"),
            st("rw_p"), st("rw_s"), st("sh_p"), st("sh_s"), st("gd_p"), st("gd_s"), st("cv_p"), st("cv_s"))
```

```python
import functools
import math

import jax
import jax.numpy as jnp
from jax import lax
from jax.experimental import pallas as pl
from jax.experimental.pallas import tpu as pltpu

F32 = jnp.float32
BF16 = jnp.bfloat16
HIGHEST = lax.Precision.HIGHEST

NSA_HEADS = 8
NSA_KV_HEADS = 2
NSA_GROUP = 4
NSA_HD = 64
CMP_BLK = 64
SEL_BLK = 64
TOPK_BLK = 16
WINDOW = 512
FORCE_BONUS = 2.0 * NSA_GROUP
RWKV_HEADS = 8
RWKV_HD = 64
RWKV_W = 512
RWKV_GN_EPS = 64e-5
GDN_HEADS = 8
GDN_HD = 128
GDN_W = 1024
CONV_W = 4
N_GROUPS = 4
EXP_PER_GROUP = 8
N_EXPERTS = 32
EPS = 1e-6
NEG = -1e30

LANE = 128
GRP_LANE = 64
ROW_ALIGN = 16
SAMPLE_TILE_SLOTS = 8
SPAD = 8
VMEM_LIMIT = 56 * 1024 * 1024

ROW_TILE = 512
MOE_ROW_TILE = 1024
NSA_TQ = 128
NSA_TQ_SAMPLE = 32
NSA_TK = 512
CHUNK = 64
CMP_PAGES_PER_STEP = 32

NN = (((1,), (0,)), ((), ()))
NT = (((1,), (1,)), ((), ()))
TN = (((0,), (0,)), ((), ()))

E_Q, E_KV, E_G, E_RW = 0, 512, 1280, 1408
E_COLS = 1408 + 1920
RW_COLS = 1920
O_COLS = 3072 + 1024 + 128


def _mm(a, b, dims=NN):
    return lax.dot_general(a.astype(BF16), b.astype(BF16), dims, preferred_element_type=F32)


def _mmh(a, b, dims=NN):
    return lax.dot_general(a.astype(F32), b.astype(F32), dims, precision=HIGHEST, preferred_element_type=F32)


def _split(a):
    hi = a.astype(BF16)
    return hi, (a - hi.astype(F32)).astype(BF16)


def _mm3(a, b, dims=NN):
    ah, al = _split(a)
    bh, bl = _split(b)
    d = lambda x, y: lax.dot_general(x, y, dims, preferred_element_type=F32)
    return d(ah, bh) + (d(ah, bl) + d(al, bh))


def _split3(x):
    h1 = x.astype(BF16)
    r1 = x - h1.astype(F32)
    h2 = r1.astype(BF16)
    return h1, h2, (r1 - h2.astype(F32)).astype(BF16)


def _mm01(m01, x):
    m = m01.astype(BF16)
    parts = _split3(x)
    d = lambda y: lax.dot_general(m, y, NN, preferred_element_type=F32)
    return d(parts[0]) + (d(parts[1]) + d(parts[2]))


def _cumsum_rows(x):
    n = x.shape[0]
    row = lax.broadcasted_iota(jnp.int32, x.shape, 0)
    shift = 1
    while shift < n:
        x = x + jnp.where(row >= shift, pltpu.roll(x, shift, 0), 0.0)
        shift *= 2
    return x


def _shift_rows(x, prev8, sh):
    rolled = pltpu.roll(x, sh, 0)
    row8 = lax.broadcasted_iota(jnp.int32, (8, 1), 0)
    head = jnp.where(row8 < sh, pltpu.roll(prev8, sh, 0), rolled[0:8])
    return head if x.shape[0] == 8 else jnp.concatenate([head, rolled[8:]], axis=0)


def _head_sums(xs, seg_half):
    r = xs[0].shape[0]
    half = seg_half.shape[0]
    pieces = [p[:, h0:h0 + half] for x in xs for p in _split3(x) for h0 in (0, half)]
    out = lax.dot_general(jnp.concatenate(pieces, axis=0), seg_half.astype(BF16), NN, preferred_element_type=F32)
    res = []
    for i in range(len(xs)):
        o = [out[(6 * i + u) * r:(6 * i + u + 1) * r] for u in range(6)]
        res.append(jnp.concatenate([o[0] + (o[2] + o[4]), o[1] + (o[3] + o[5])], axis=1))
    return res


def _sigmoid(x):
    return 1.0 / (1.0 + jnp.exp(-x))


def _silu(x):
    return x * _sigmoid(x)


def _softplus(x):
    return jnp.maximum(x, 0.0) + jnp.log(1.0 + jnp.exp(-jnp.abs(x)))


def _cparams(sem):
    return pltpu.CompilerParams(dimension_semantics=sem, vmem_limit_bytes=VMEM_LIMIT)


def _norm_mod(x, nw, sc, sh):
    y = x * lax.rsqrt(jnp.mean(x * x, axis=-1, keepdims=True) + EPS)
    return (y * nw) * (1.0 + sc) + sh


def _mod_spec(rows_mod, tm, d):
    if rows_mod == 1:
        return pl.BlockSpec((1, d), lambda i: (0, 0))
    return pl.BlockSpec((tm, d), lambda i: (i, 0))


def _adaln_body(c_ref, w_ref, b_ref, o_ref):
    o_ref[0] = _mmh(_silu(c_ref[...]), w_ref[0]) + b_ref[0]


def adaln(c_all, w_ada, b_ada):
    depth, d, n6 = w_ada.shape
    rows = c_all.shape[0]
    tn = 768
    return pl.pallas_call(
        _adaln_body,
        grid=(depth, n6 // tn),
        in_specs=[pl.BlockSpec((rows, d), lambda l, j: (0, 0)),
                  pl.BlockSpec((1, d, tn), lambda l, j: (l, 0, j)),
                  pl.BlockSpec((1, 1, tn), lambda l, j: (l, 0, j))],
        out_specs=pl.BlockSpec((1, rows, tn), lambda l, j: (l, 0, j)),
        out_shape=jax.ShapeDtypeStruct((depth, rows, n6), F32),
        compiler_params=_cparams(("arbitrary", "arbitrary")),
        name="adaln",
    )(c_all, w_ada, b_ada.reshape(depth, 1, n6))


def _even_proj_body(x_ref, nw_ref, sc_ref, sh_ref, w_ref,
                    kv_ref, qt_ref, gt_ref, ks_ref, vst_ref, kw_ref, vwt_ref, rw_ref, hl_ref):
    h = _norm_mod(x_ref[...], nw_ref[...], sc_ref[...], sh_ref[...])
    hl = hl_ref.shape[0]
    hl_ref[...] = h[h.shape[0] - hl:, :]
    hb = h.astype(BF16)
    q = _mm(hb, w_ref[:, E_Q:E_Q + 512]) * (NSA_HD ** -0.5)
    qt_ref[...] = q.T.astype(BF16)
    kv = _mm(hb, w_ref[:, E_KV:E_KV + 768])
    kv_ref[...] = kv
    ks_ref[...] = kv[:, 256:384].astype(BF16)
    vst_ref[...] = kv[:, 384:512].T.astype(BF16)
    kw_ref[...] = kv[:, 512:640].astype(BF16)
    vwt_ref[...] = kv[:, 640:768].T.astype(BF16)
    g = _sigmoid(_mm(hb, w_ref[:, E_G:E_G + 128]))
    gt_ref[...] = g.T
    rw_ref[...] = _mm(hb, w_ref[:, E_RW:E_RW + RW_COLS])


def even_proj(x, nw, sc, sh, w_packed, tm, hl_rows):
    n, d = x.shape
    rows_mod = sc.shape[0]
    row = lambda c: pl.BlockSpec((tm, c), lambda i: (i, 0))
    col = lambda r: pl.BlockSpec((r, tm), lambda i: (0, i))
    return pl.pallas_call(
        _even_proj_body,
        grid=(n // tm,),
        in_specs=[row(d), pl.BlockSpec((1, d), lambda i: (0, 0)),
                  _mod_spec(rows_mod, tm, d), _mod_spec(rows_mod, tm, d),
                  pl.BlockSpec((d, E_COLS), lambda i: (0, 0))],
        out_specs=[row(768), col(512), col(128), row(128), col(128), row(128), col(128), row(RW_COLS),
                   pl.BlockSpec((hl_rows, d), lambda i: (0, 0))],
        out_shape=[jax.ShapeDtypeStruct((n, 768), F32),
                   jax.ShapeDtypeStruct((512, n), BF16),
                   jax.ShapeDtypeStruct((128, n), F32),
                   jax.ShapeDtypeStruct((n, 128), BF16),
                   jax.ShapeDtypeStruct((128, n), BF16),
                   jax.ShapeDtypeStruct((n, 128), BF16),
                   jax.ShapeDtypeStruct((128, n), BF16),
                   jax.ShapeDtypeStruct((n, RW_COLS), F32),
                   jax.ShapeDtypeStruct((hl_rows, d), F32)],
        compiler_params=_cparams(("arbitrary",)),
        name="even_proj",
    )(x, nw, sc, sh, w_packed)


def _pack_even_w(w_in):
    d = w_in.shape[0]
    z = lambda c: jnp.zeros((d, c), w_in.dtype)
    nsa = 1304
    rw = w_in[:, nsa:]
    parts = [w_in[:, :1280], w_in[:, 1280:1304], z(104),
             rw[:, :1536], rw[:, 1536:1600], z(64), rw[:, 1600:1664], z(64), rw[:, 1664:1792]]
    return jnp.concatenate(parts, axis=1).astype(BF16)


def _pack_rw_vec(v):
    z = jnp.zeros((64,), v.dtype)
    return jnp.concatenate([v[:1536], v[1536:1600], z, v[1600:1664], z, v[1664:1792]])[None, :]


def _mm_body(x_ref, w_ref, o_ref):
    o_ref[...] = _mm(x_ref[...], w_ref[...])


def small_matmul(x, w):
    return pl.pallas_call(
        _mm_body,
        out_shape=jax.ShapeDtypeStruct((x.shape[0], w.shape[1]), F32),
        compiler_params=pltpu.CompilerParams(vmem_limit_bytes=VMEM_LIMIT),
        name="small_matmul",
    )(x, w)


def _compress_body(x_ref, wts_ref, wc_ref, o_ref):
    x = x_ref[...]
    nb = x.shape[0] // CMP_BLK
    pooled = jnp.sum(x.reshape(nb, CMP_BLK, x.shape[-1]) * wts_ref[...][None], axis=1)
    o_ref[...] = _mm(pooled, wc_ref[...])


def _compress_paged_body(pt_ref, *refs, pps):
    page_refs = refs[:pps]
    wp_ref, wc_ref, o_ref = refs[pps:]
    x = jnp.concatenate([r[0] for r in page_refs], axis=1)
    pooled_t = jnp.concatenate([_mm(x[0:128], wp_ref[0]), _mm(x[128:256], wp_ref[1])], axis=0)
    nb = o_ref.shape[1]
    o_ref[0] = _mm(pooled_t.T[:nb], wc_ref[...])


def _cmp_weights(pos_wts, w_c):
    wts = jnp.repeat(pos_wts.T, 128, axis=1)
    eye2 = jnp.eye(2, dtype=w_c.dtype)
    blocks = [jnp.kron(eye2, w_c[c]) for c in range(2)]
    z = jnp.zeros((128, 128), w_c.dtype)
    wc = jnp.concatenate([jnp.concatenate([blocks[0], z], axis=1),
                          jnp.concatenate([z, blocks[1]], axis=1)], axis=0)
    return wts, wc


def compress_prompt(kv, wts, wc, tr):
    t = kv.shape[0]
    nb = tr // CMP_BLK
    return pl.pallas_call(
        _compress_body,
        grid=(t // tr,),
        in_specs=[pl.BlockSpec((tr, 256), lambda i: (i, 0)),
                  pl.BlockSpec((CMP_BLK, 256), lambda i: (0, 0)),
                  pl.BlockSpec((256, 256), lambda i: (0, 0))],
        out_specs=pl.BlockSpec((nb, 256), lambda i: (i, 0)),
        out_shape=jax.ShapeDtypeStruct((t // CMP_BLK, 256), F32),
        compiler_params=_cparams(("arbitrary",)),
        name="compress_prompt",
    )(kv, wts, wc)


def compress_paged(pool_t, page_table, pos_wts, wc, pages_per_step):
    b, n_pages = page_table.shape
    page = pool_t.shape[2]
    pps = pages_per_step
    nb = pps * page // CMP_BLK
    p_idx = jnp.arange(pps * page)
    wp = jax.nn.one_hot(p_idx // CMP_BLK, LANE, dtype=F32)[None] * pos_wts[:, p_idx % CMP_BLK][:, :, None]

    def page_spec(u):
        return pl.BlockSpec((1, 256, page), lambda bi, g, pt: (pt[bi, g * pps + u], 0, 0))

    grid_spec = pltpu.PrefetchScalarGridSpec(
        num_scalar_prefetch=1,
        grid=(b, n_pages // pps),
        in_specs=[page_spec(u) for u in range(pps)] + [
            pl.BlockSpec((2, pps * page, LANE), lambda bi, g, pt: (0, 0, 0)),
            pl.BlockSpec((256, 256), lambda bi, g, pt: (0, 0))],
        out_specs=pl.BlockSpec((1, nb, 256), lambda bi, g, pt: (bi, g, 0)),
    )
    return pl.pallas_call(
        functools.partial(_compress_paged_body, pps=pps),
        grid_spec=grid_spec,
        out_shape=jax.ShapeDtypeStruct((b, n_pages * page // CMP_BLK, 256), F32),
        compiler_params=_cparams(("arbitrary", "arbitrary")),
        name="compress_paged",
    )(page_table, *([pool_t] * pps), wp, wc)


def _gather_sel_body(pt_ref, tiles_ref, cnt_ref, *refs, pps, n_page_steps, nt):
    del pt_ref
    page_refs = refs[:pps]
    tail_ref, ks_ref, vst_ref = refs[pps:]
    bi = pl.program_id(0)
    a = pl.program_id(1)
    j = tiles_ref[bi * nt + jnp.minimum(a, cnt_ref[bi] - 1)]
    live = a < cnt_ref[bi]

    @pl.when(live & (j < n_page_steps))
    def _():
        ks_ref[0] = jnp.concatenate([r[0][0:128].T for r in page_refs], axis=0).astype(BF16)
        vst_ref[0] = jnp.concatenate([r[0][128:256] for r in page_refs], axis=1).astype(BF16)

    @pl.when(live & (j >= n_page_steps))
    def _():
        x = tail_ref[0]
        ks_ref[0] = x[:, :128].astype(BF16)
        vst_ref[0] = x[:, 128:].T.astype(BF16)


def gather_sel(pool_t, page_table, tail, tk, tiles, cnt, n_slots):
    b, n_pages = page_table.shape
    page = pool_t.shape[2]
    pps = tk // page
    n_page_steps = n_pages // pps
    nt = n_page_steps + 1
    nk = n_slots * tk

    def slot(bi, a, pt, tiles, cnt):
        return jnp.minimum(a, cnt[bi] - 1)

    def page_spec(u):
        def index(bi, a, pt, tiles, cnt):
            j = tiles[bi * nt + slot(bi, a, pt, tiles, cnt)]
            return (pt[bi, jnp.minimum(j * pps + u, n_pages - 1)], 0, 0)
        return pl.BlockSpec((1, 256, page), index)

    grid_spec = pltpu.PrefetchScalarGridSpec(
        num_scalar_prefetch=3,
        grid=(b, jnp.max(cnt)),
        in_specs=[page_spec(u) for u in range(pps)] + [pl.BlockSpec((1, tk, 256), lambda bi, a, *_: (bi, 0, 0))],
        out_specs=[pl.BlockSpec((1, tk, 128), lambda bi, a, *s: (bi, slot(bi, a, *s), 0)),
                   pl.BlockSpec((1, 128, tk), lambda bi, a, *s: (bi, 0, slot(bi, a, *s)))],
    )
    return pl.pallas_call(
        functools.partial(_gather_sel_body, pps=pps, n_page_steps=n_page_steps, nt=nt),
        grid_spec=grid_spec,
        out_shape=[jax.ShapeDtypeStruct((b, nk, 128), BF16), jax.ShapeDtypeStruct((b, 128, nk), BF16)],
        compiler_params=_cparams(("arbitrary", "arbitrary")),
        name="gather_sel",
    )(page_table, tiles, cnt, *([pool_t] * pps), tail)


MASKED = -1e30
M_INIT = -1e29


def _nsa_query(qt_ref, k, tq):
    w4 = NSA_GROUP * tq
    qb = qt_ref[0].astype(F32)
    qcat = jnp.concatenate([qb[g * 64:(g + 1) * 64] for g in range(NSA_GROUP)], axis=1)
    q2 = jnp.concatenate([qcat, qcat], axis=0)
    row = lax.broadcasted_iota(jnp.int32, (128, w4), 0)
    qe = jnp.where(row // 64 == k, q2, 0.0)
    gidx = lax.broadcasted_iota(jnp.int32, (128, w4), 1) // tq
    base = jnp.where(k == 0, 0.5, 0.5 / 16.0)
    slope = base * jnp.where(gidx == 0, 1.0, jnp.where(gidx == 1, 0.5, jnp.where(gidx == 2, 0.25, 0.125)))
    mult = jnp.where(row == 0, 16.0, jnp.where(row == 1, 1.0, jnp.where(row == 2, 128.0,
                                                                         jnp.where(row == 3, 64.0, 0.0))))
    return jnp.concatenate([qe, slope * mult], axis=0).astype(BF16)


def _pos_features(rows, tile_rel):
    r = lax.broadcasted_iota(jnp.int32, (rows, LANE), 0)
    lane = lax.broadcasted_iota(jnp.int32, (rows, LANE), 1)
    ab = jnp.where(lane == 0, r // 16, jnp.where(lane == 1, r % 16, 0)).astype(F32)
    return jnp.where(lane == 2, tile_rel, ab).astype(BF16)


def _gate_rows(gb, j, tq):
    return jnp.concatenate([gb[g * 3 + j:g * 3 + j + 1, :] for g in range(NSA_GROUP)], axis=1)


def _nsa_select_body(qt_ref, g_ref, kvc_ref, kvct_ref, kw_ref, vwt_ref, part_ref, sel_ref, flag_ref, *,
                     tq, tk, wk, nbc, nb, pos0_fn, wstart_fn, wpos0_fn):
    i = pl.program_id(1)
    k = pl.program_id(2)
    w4 = NSA_GROUP * tq
    pos0 = pos0_fn(i)
    qa = _nsa_query(qt_ref, k, tq)
    pos_q = pos0 + lax.broadcasted_iota(jnp.int32, (1, w4), 1) % tq

    def softmax_cols(s, bad):
        s = jnp.where(bad, MASKED, s)
        m = jnp.maximum(jnp.max(s, axis=0, keepdims=True), M_INIT)
        e = jnp.exp(s - m)
        return e / jnp.maximum(jnp.sum(e, axis=0, keepdims=True), 1e-30)

    n_i = lax.broadcasted_iota(jnp.int32, (nbc, LANE), 0)
    lane_c = lax.broadcasted_iota(jnp.int32, (nbc, LANE), 1)
    feat_c = jnp.where(lane_c == 3, n_i - pos0 // CMP_BLK, 0).astype(F32).astype(BF16)
    kc = jnp.concatenate([kvc_ref[0][:, :128].astype(BF16), feat_c], axis=1)
    c_end = lax.broadcasted_iota(jnp.int32, (nbc, 1), 0) * CMP_BLK + (CMP_BLK - 1)
    p_c = softmax_cols(lax.dot_general(kc, qa, NN, preferred_element_type=F32), c_end > pos_q)
    vct = kvct_ref[0, pl.ds(pl.multiple_of(128 + k * 64, 64), 64), :]
    o_c = _mm(vct, p_c)

    imp = p_c[:, 0:tq]
    for g in range(1, NSA_GROUP):
        imp = imp + p_c[:, g * tq:(g + 1) * tq]
    if nb > nbc:
        imp = jnp.concatenate([imp, jnp.zeros((nb - nbc, tq), F32)], axis=0)
    blk = lax.broadcasted_iota(jnp.int32, (nb, tq), 0)
    cur = (pos0 + lax.broadcasted_iota(jnp.int32, (1, tq), 1)) // SEL_BLK
    forced = (blk == cur) | (blk == cur - 1) | (blk == 0)
    score = jnp.where(blk <= cur, imp + jnp.where(forced, FORCE_BONUS, 0.0), -1.0)
    for _ in range(min(TOPK_BLK, nb)):
        m = jnp.max(score, axis=0, keepdims=True)
        first = jnp.min(jnp.where(score == m, blk, nb), axis=0, keepdims=True)
        score = jnp.where(blk == first, -2.0, score)
    sel = jnp.where(score == -2.0, 1.0, 0.0)
    sel_ref[0, 0] = sel
    bpt = tk // SEL_BLK
    any_row = jnp.max(sel, axis=1, keepdims=True)
    flag_ref[0, 0] = jnp.max(any_row.reshape(nb // bpt, bpt, 1), axis=1)

    wstart = wstart_fn(i)
    if not isinstance(wstart, int):
        wstart = pl.multiple_of(wstart, 128)
    wpos0 = wpos0_fn(i)
    tile_rel = jnp.asarray((wpos0 - pos0) // 128, F32)
    kw = jnp.concatenate([kw_ref[0, pl.ds(wstart, wk), :], _pos_features(wk, tile_rel)], axis=1)
    dist_w = pos_q - (wpos0 + lax.broadcasted_iota(jnp.int32, (wk, 1), 0))
    p_w = softmax_cols(lax.dot_general(kw, qa, NN, preferred_element_type=F32), (dist_w < 0) | (dist_w >= WINDOW))
    vwin = vwt_ref[0, pl.ds(pl.multiple_of(k * 64, 64), 64), pl.ds(wstart, wk)]
    o_w = _mm(vwin, p_w)

    gb = g_ref[0, 0]
    part_ref[0, 0] = _gate_rows(gb, 0, tq) * o_c + _gate_rows(gb, 2, tq) * o_w


def nsa_select(qt, gates, kvc, kvct, kw, vwt, *, nb, tq, tk, wk, pos0_fn, wstart_fn, wpos0_fn):
    b, _, nq = qt.shape
    nbc = kvc.shape[1]
    nw = kw.shape[1]
    nqt = nq // tq
    nt = nb * SEL_BLK // tk
    w4 = NSA_GROUP * tq
    assert nbc <= 256 and tk <= 512 and wk <= 1024
    body = functools.partial(_nsa_select_body, tq=tq, tk=tk, wk=wk, nbc=nbc, nb=nb, pos0_fn=pos0_fn,
                             wstart_fn=wstart_fn, wpos0_fn=wpos0_fn)
    full = lambda s1, s2: pl.BlockSpec((1, s1, s2), lambda bi, i, k: (bi, 0, 0))
    step = lambda s1, s2: pl.BlockSpec((1, 1, s1, s2), lambda bi, i, k: (bi, i * NSA_KV_HEADS + k, 0, 0))
    return pl.pallas_call(
        body,
        grid=(b, nqt, NSA_KV_HEADS),
        in_specs=[pl.BlockSpec((1, 256, tq), lambda bi, i, k: (bi, k, i)),
                  pl.BlockSpec((1, 1, 16, tq), lambda bi, i, k: (bi, k, 0, i)),
                  full(nbc, 256), full(256, nbc), full(nw, 128), full(128, nw)],
        out_specs=[step(64, w4), step(nb, tq), step(nt, 1)],
        out_shape=[jax.ShapeDtypeStruct((b, nqt * 2, 64, w4), F32),
                   jax.ShapeDtypeStruct((b, nqt * 2, nb, tq), F32),
                   jax.ShapeDtypeStruct((b, nqt * 2, nt, 1), F32)],
        compiler_params=_cparams(("arbitrary", "arbitrary", "arbitrary")),
        name="nsa_select",
    )(qt, gates, kvc, kvct, kw, vwt)


def _nsa_selected_body(list_ref, slot_ref, cnt_ref, qt_ref, g_ref, sel_ref, ks_ref, vst_ref, part_ref, o_ref, *,
                       tq, tk, nt, pos0_fn):
    bi = pl.program_id(0)
    i = pl.program_id(1)
    k = pl.program_id(2)
    step = (bi * pl.num_programs(1) + i) * NSA_KV_HEADS + k
    w4 = NSA_GROUP * tq
    pos0 = pos0_fn(i)
    qa = _nsa_query(qt_ref, k, tq)
    pos_q = pos0 + lax.broadcasted_iota(jnp.int32, (1, w4), 1) % tq
    bpt = tk // SEL_BLK
    row_k = lax.broadcasted_iota(jnp.int32, (tk, 1), 0)
    r = lax.broadcasted_iota(jnp.int32, (tk, LANE), 0)
    lane = lax.broadcasted_iota(jnp.int32, (tk, LANE), 1)
    feat_ab = jnp.where(lane == 0, r // 16, jnp.where(lane == 1, r % 16, 0)).astype(F32)

    n_act = cnt_ref[step]

    def tile_scores(jj, live):
        j = list_ref[step * nt + jj]
        off = pl.multiple_of(j * tk, tk)
        buf = pl.multiple_of(slot_ref[step * nt + jj] * tk, tk)
        tile_rel = ((off - pos0) // 128).astype(F32)
        feat = jnp.where(lane == 2, tile_rel, feat_ab).astype(BF16)
        kj = jnp.concatenate([ks_ref[0, pl.ds(buf, tk), :], feat], axis=1)
        s = lax.dot_general(kj, qa, NN, preferred_element_type=F32)
        selb = (sel_ref[0, 0, pl.ds(pl.multiple_of(j * bpt, bpt), bpt), :] - 1.0) * (-MASKED)
        selb = jnp.concatenate([selb] * NSA_GROUP, axis=1) + jnp.where(live, 0.0, MASKED)
        s = s + jnp.broadcast_to(selb[:, None, :], (bpt, SEL_BLK, w4)).reshape(tk, w4)
        s = jnp.where(row_k > pos_q - off, MASKED, s)
        return s, vst_ref[0, pl.ds(pl.multiple_of(k * 64, 64), 64), pl.ds(buf, tk)]

    def kv_pair(pp, carry):
        m_i, l_i, acc = carry
        second = 2 * pp + 1
        s_a, v_a = tile_scores(2 * pp, True)
        s_b, v_b = tile_scores(jnp.minimum(second, n_act - 1), second < n_act)
        m_new = jnp.maximum(m_i, jnp.maximum(jnp.max(s_a, axis=0, keepdims=True), jnp.max(s_b, axis=0, keepdims=True)))
        p_a = jnp.exp(s_a - m_new)
        p_b = jnp.exp(s_b - m_new)
        alpha = jnp.exp(m_i - m_new)
        l_new = l_i * alpha + (jnp.sum(p_a, axis=0, keepdims=True) + jnp.sum(p_b, axis=0, keepdims=True))
        return m_new, l_new, acc * alpha + (_mm(v_a, p_a) + _mm(v_b, p_b))

    init = (jnp.full((1, w4), M_INIT, F32), jnp.zeros((1, w4), F32), jnp.zeros((64, w4), F32))
    _, l_s, acc_s = lax.fori_loop(0, (n_act + 1) // 2, kv_pair, init)
    o_s = acc_s / jnp.maximum(l_s, 1e-30)
    o_t = part_ref[0, 0] + _gate_rows(g_ref[0, 0], 1, tq) * o_s
    o_ref[0] = jnp.concatenate([o_t[:, g * tq:(g + 1) * tq].T for g in range(NSA_GROUP)], axis=1)


def nsa_selected(tile_list, slot_list, tile_cnt, qt, gates, sel, ks, vst, part, *, tq, tk, pos0_fn):
    b, _, nq = qt.shape
    nk = ks.shape[1]
    nb = sel.shape[2]
    nt = nb * SEL_BLK // tk
    w4 = NSA_GROUP * tq
    full = lambda s1, s2: pl.BlockSpec((1, s1, s2), lambda bi, i, k, *_: (bi, 0, 0))
    step = lambda s1, s2: pl.BlockSpec((1, 1, s1, s2), lambda bi, i, k, *_: (bi, i * NSA_KV_HEADS + k, 0, 0))
    grid_spec = pltpu.PrefetchScalarGridSpec(
        num_scalar_prefetch=3,
        grid=(b, nq // tq, NSA_KV_HEADS),
        in_specs=[pl.BlockSpec((1, 256, tq), lambda bi, i, k, *_: (bi, k, i)),
                  pl.BlockSpec((1, 1, 16, tq), lambda bi, i, k, *_: (bi, k, 0, i)),
                  step(nb, tq), full(nk, 128), full(128, nk), step(64, w4)],
        out_specs=pl.BlockSpec((1, tq, 256), lambda bi, i, k, *_: (bi, i, k)),
    )
    return pl.pallas_call(
        functools.partial(_nsa_selected_body, tq=tq, tk=tk, nt=nt, pos0_fn=pos0_fn),
        grid_spec=grid_spec,
        out_shape=jax.ShapeDtypeStruct((b, nq, 512), F32),
        compiler_params=_cparams(("arbitrary", "arbitrary", "arbitrary")),
        name="nsa_selected",
    )(tile_list, slot_list, tile_cnt, qt, gates, sel, ks, vst, part)


def _active_first(active):
    order = jnp.argsort(jnp.where(active, 0, 1), axis=-1, stable=True).astype(jnp.int32)
    return order, jnp.sum(active, axis=-1).astype(jnp.int32)


def nsa_attention(qt, gates, kvc, kvct, ks, vst, kw, vwt, *, tq, tk, wk, pos0_fn, wstart_fn, wpos0_fn):
    nb = ks.shape[1] // SEL_BLK
    part, sel, flags = nsa_select(qt, gates, kvc, kvct, kw, vwt, nb=nb, tq=tq, tk=tk, wk=wk, pos0_fn=pos0_fn,
                                  wstart_fn=wstart_fn, wpos0_fn=wpos0_fn)
    order, cnt = _active_first(flags[..., 0] > 0.5)
    return nsa_selected(order.reshape(-1), order.reshape(-1), cnt.reshape(-1), qt, gates, sel, ks, vst, part,
                        tq=tq, tk=tk, pos0_fn=pos0_fn)


def nsa_attention_paged(qt, gates, kvc, kvct, pool_t, page_table, tail, kw, vwt, *, tq, tk, wk, pos0_fn, wstart_fn,
                        wpos0_fn):
    nb = (page_table.shape[1] * pool_t.shape[2] + tk) // SEL_BLK
    part, sel, flags = nsa_select(qt, gates, kvc, kvct, kw, vwt, nb=nb, tq=tq, tk=tk, wk=wk, pos0_fn=pos0_fn,
                                  wstart_fn=wstart_fn, wpos0_fn=wpos0_fn)
    active = flags[..., 0] > 0.5
    tiles_b, cnt_b = _active_first(jnp.any(active, axis=1))
    slot_of_tile = jnp.argsort(tiles_b, axis=-1).astype(jnp.int32)
    order, cnt = _active_first(active)
    slots = jnp.take_along_axis(jnp.broadcast_to(slot_of_tile[:, None, :], order.shape), order, axis=-1)

    def run(n_slots):
        ks, vst = gather_sel(pool_t, page_table, tail, tk, tiles_b.reshape(-1), cnt_b, n_slots)
        return nsa_selected(order.reshape(-1), slots.reshape(-1), cnt.reshape(-1), qt, gates, sel, ks, vst, part,
                            tq=tq, tk=tk, pos0_fn=pos0_fn)

    nt = tiles_b.shape[-1]
    few = min(SAMPLE_TILE_SLOTS, nt)
    return lax.cond(jnp.max(cnt_b) <= few, lambda: run(few), lambda: run(nt))


def _tri_inverse(ms, c):
    eye = (lax.broadcasted_iota(jnp.int32, (c, c), 0) == lax.broadcasted_iota(jnp.int32, (c, c), 1)).astype(F32)
    ps = [-m for m in ms]
    ts = [eye + p for p in ps]
    steps = max(int(math.ceil(math.log2(c))) - 1, 0)
    d = lambda x, y: lax.dot_general(x, y, NN, preferred_element_type=F32)
    for _ in range(steps):
        sp = [_split(p) for p in ps]
        ps = [d(ph, ph) + (d(ph, pl_) + d(pl_, ph)) for ph, pl_ in sp]
        sp = [_split(p) for p in ps]
        st = [_split(t) for t in ts]
        ts = [t + (d(th, ph) + (d(th, pl_) + d(tl, ph))) for t, (th, tl), (ph, pl_) in zip(ts, st, sp)]
    return ts


def _rwkv_body(rw_ref, rw0_ref, s0_ref, mu_ref, vec_ref, w2_ref, a2_ref, g2_ref, seg_ref, rk_ref,
               o_ref, sfin_ref, buf_ref, s_ref, y_ref, *, c, valid, n_chunks):
    ci = pl.program_id(1)
    halo = 8

    @pl.when(ci == 0)
    def _():
        buf_ref[...] = rw0_ref[0]
        s_ref[...] = s0_ref[0]

    cur = rw_ref[...]
    prev = _shift_rows(cur, buf_ref[...], 1)
    xr = cur + (prev - cur) * mu_ref[...]
    buf_ref[...] = cur[c - halo:, :]

    vec = vec_ref[...]
    w0, a0, kkw, kaw, ln_w, ln_b = (vec[r:r + 1, :] for r in range(6))
    r = xr[:, 0:512]
    kx = xr[:, 512:1024]
    v = xr[:, 1024:1536]
    xw = xr[:, 1536:1664]
    xa = xr[:, 1664:1792]
    xg = xr[:, 1792:1920]
    wl = -jnp.exp(-_softplus(-(w0 + _mm(jnp.tanh(xw), w2_ref[...]))) - 0.5)
    a = _sigmoid(a0 + _mm(xa, a2_ref[...]))
    gate = _mm(_sigmoid(xg), g2_ref[...])
    seg = seg_ref[...]
    zk = kx * kkw
    k2 = kx * (1.0 + (a - 1.0) * kaw)
    zz_sum, rk_sum = _head_sums([zk * zk, r * k2 * rk_ref[...]], seg)
    kk = zk * lax.rsqrt(zz_sum + EPS)
    bonus = rk_sum * v
    if valid < c:
        live = lax.broadcasted_iota(jnp.int32, (c, 1), 0) < valid
        wl = jnp.where(live, wl, 0.0)
        kk = jnp.where(live, kk, 0.0)
        k2 = jnp.where(live, k2, 0.0)
        v = jnp.where(live, v, 0.0)
        r = jnp.where(live, r, 0.0)
    bb = kk * a

    ri = lax.broadcasted_iota(jnp.int32, (c, c), 0)
    cj = lax.broadcasted_iota(jnp.int32, (c, c), 1)
    tril = ri >= cj
    strict = ri > cj
    cw = _cumsum_rows(wl)
    ecw = jnp.exp(cw)
    einv = jnp.exp(-cw)
    p_c = ecw[c - 1:c, :]
    kt = kk * jnp.exp(cw - wl)
    bt = bb * einv
    ki = k2 * einv
    rt = r * ecw
    bd = bt * p_c
    kd = ki * p_c

    heads = range(RWKV_HEADS)
    sls = [slice(h * RWKV_HD, (h + 1) * RWKV_HD) for h in heads]
    kt_h = [kt[:, sl] for sl in sls]
    bt_h = [bt[:, sl] for sl in sls]
    ki_h = [ki[:, sl] for sl in sls]
    rt_h = [rt[:, sl] for sl in sls]
    v_h = [v[:, sl] for sl in sls]
    l_m = [jnp.where(strict, _mm3(kt_h[h], bt_h[h], NT), 0.0) for h in heads]
    m_kk = [jnp.where(strict, _mm(kt_h[h], ki_h[h], NT), 0.0) for h in heads]
    a_rb = [jnp.where(tril, _mm(rt_h[h], bt_h[h], NT), 0.0) for h in heads]
    a_rk = [jnp.where(tril, _mm(rt_h[h], ki_h[h], NT), 0.0) for h in heads]
    mv = [_mm(m_kk[h], v_h[h]) for h in heads]
    y0 = [_mm(a_rk[h], v_h[h]) for h in heads]
    t_inv = _tri_inverse(l_m, c)
    w_h = [_mm3(t_inv[h], kt_h[h]) for h in heads]
    u_h = [-_mm3(t_inv[h], mv[h]) for h in heads]
    s_h = [s_ref[h] for h in heads]
    e_h = [u_h[h] - _mm(w_h[h], s_h[h], NT) for h in heads]
    y1 = [_mm(rt_h[h], s_h[h], NT) + y0[h] for h in heads]
    y_h = [y1[h] + _mm(a_rb[h], e_h[h]) for h in heads]
    ds = [_mm(e_h[h], bd[:, sls[h]], TN) + _mm(v_h[h], kd[:, sls[h]], TN) for h in heads]
    for h in heads:
        s_ref[h] = s_h[h] * p_c[:, sls[h]] + ds[h]
        mu_h = jnp.mean(y_h[h], axis=-1, keepdims=True)
        d_h = y_h[h] - mu_h
        var_h = jnp.mean(d_h * d_h, axis=-1, keepdims=True)
        y_ref[:, sls[h]] = d_h * lax.rsqrt(var_h + RWKV_GN_EPS)

    o_ref[...] = (y_ref[...] * ln_w + ln_b + bonus) * gate

    @pl.when(ci == n_chunks - 1)
    def _():
        sfin_ref[0] = s_ref[...]


def rwkv_mix(rw, rw0, s0, mu, vec, w2, a2, g2, seg, rk, *, c, valid):
    b = s0.shape[0]
    rows = rw.shape[0]
    n_chunks = rows // (b * c)
    const = lambda s: pl.BlockSpec(s, lambda bi, ci: tuple(0 for _ in s))
    return pl.pallas_call(
        functools.partial(_rwkv_body, c=c, valid=valid, n_chunks=n_chunks),
        grid=(b, n_chunks),
        in_specs=[pl.BlockSpec((c, RW_COLS), lambda bi, ci: (bi * n_chunks + ci, 0)),
                  pl.BlockSpec((1, 8, RW_COLS), lambda bi, ci: (bi, 0, 0)),
                  pl.BlockSpec((1, RWKV_HEADS, 64, 64), lambda bi, ci: (bi, 0, 0, 0)),
                  const((1, RW_COLS)), const((8, 512)), const((128, 512)), const((128, 512)), const((128, 512)),
                  const((RWKV_W // 2, RWKV_W // 2)), const((1, 512))],
        out_specs=[pl.BlockSpec((c, 512), lambda bi, ci: (bi * n_chunks + ci, 0)),
                   pl.BlockSpec((1, RWKV_HEADS, 64, 64), lambda bi, ci: (bi, 0, 0, 0))],
        out_shape=[jax.ShapeDtypeStruct((rows, 512), F32),
                   jax.ShapeDtypeStruct((b, RWKV_HEADS, 64, 64), F32)],
        scratch_shapes=[pltpu.VMEM((8, RW_COLS), F32), pltpu.VMEM((RWKV_HEADS, 64, 64), F32),
                        pltpu.VMEM((c, 512), F32)],
        compiler_params=_cparams(("arbitrary", "arbitrary")),
        name="rwkv_mix",
    )(rw, rw0, s0, mu, vec, w2, a2, g2, seg, rk)


def _out_proj_body(*refs, n_in):
    a_refs = refs[:n_in]
    w_refs = refs[n_in:2 * n_in]
    x_ref, g_ref, o_ref = refs[2 * n_in:]
    y = _mm(a_refs[0][...], w_refs[0][...])
    for a_ref, w_ref in zip(a_refs[1:], w_refs[1:]):
        y = y + _mm(a_ref[...], w_ref[...])
    o_ref[...] = x_ref[...] + g_ref[...] * y


def out_proj(acts, weights, x, gate, tm):
    n, d = x.shape
    n_in = len(acts)
    return pl.pallas_call(
        functools.partial(_out_proj_body, n_in=n_in),
        grid=(n // tm,),
        in_specs=[pl.BlockSpec((tm, a.shape[1]), lambda i: (i, 0)) for a in acts]
        + [pl.BlockSpec(w.shape, lambda i: (0, 0)) for w in weights]
        + [pl.BlockSpec((tm, d), lambda i: (i, 0)), _mod_spec(gate.shape[0], tm, d)],
        out_specs=pl.BlockSpec((tm, d), lambda i: (i, 0)),
        out_shape=jax.ShapeDtypeStruct((n, d), F32),
        compiler_params=_cparams(("arbitrary",)),
        name="out_proj",
    )(*acts, *weights, x, gate)


def _odd_proj_body(x_ref, nw_ref, sc_ref, sh_ref, w_ref, qkv_ref, z_ref, ba_ref):
    hb = _norm_mod(x_ref[...], nw_ref[...], sc_ref[...], sh_ref[...]).astype(BF16)
    qkv_ref[...] = _mm(hb, w_ref[:, 0:3072])
    z_ref[...] = _mm(hb, w_ref[:, 3072:4096])
    ba_ref[...] = _mm(hb, w_ref[:, 4096:O_COLS])


def odd_proj(x, nw, sc, sh, w_packed, tm):
    n, d = x.shape
    rows_mod = sc.shape[0]
    row = lambda c: pl.BlockSpec((tm, c), lambda i: (i, 0))
    return pl.pallas_call(
        _odd_proj_body,
        grid=(n // tm,),
        in_specs=[row(d), pl.BlockSpec((1, d), lambda i: (0, 0)),
                  _mod_spec(rows_mod, tm, d), _mod_spec(rows_mod, tm, d),
                  pl.BlockSpec((d, O_COLS), lambda i: (0, 0))],
        out_specs=[row(3072), row(1024), row(128)],
        out_shape=[jax.ShapeDtypeStruct((n, 3072), F32), jax.ShapeDtypeStruct((n, 1024), F32),
                   jax.ShapeDtypeStruct((n, 128), F32)],
        compiler_params=_cparams(("arbitrary",)),
        name="odd_proj",
    )(x, nw, sc, sh, w_packed)


def _gdn_body(qkv_ref, z_ref, ba_ref, cs_ref, s0_ref, cw_ref, hp_ref, nw_ref,
              o_ref, sfin_ref, buf_ref, s_ref, *, c, valid, n_chunks):
    ci = pl.program_id(1)
    halo = 8

    @pl.when(ci == 0)
    def _():
        buf_ref[...] = cs_ref[0]
        s_ref[...] = s0_ref[0]

    x = qkv_ref[...]
    prev8 = buf_ref[...]
    cw = cw_ref[...]
    conv = x * cw[CONV_W - 1:CONV_W, :]
    for j in range(CONV_W - 1):
        conv = conv + _shift_rows(x, prev8, CONV_W - 1 - j) * cw[j:j + 1, :]
    buf_ref[...] = x[c - halo:, :]
    conv = _silu(conv)

    hp = hp_ref[...]
    ba = ba_ref[...]
    beta_f = _sigmoid(ba)
    g_f = hp[0:1, :] * _softplus(ba + hp[1:2, :])
    if valid < c:
        live = lax.broadcasted_iota(jnp.int32, (c, 1), 0) < valid
        beta_f = jnp.where(live, beta_f, 0.0)
        g_f = jnp.where(live, g_f, 0.0)
        conv = jnp.where(live, conv, 0.0)

    ri = lax.broadcasted_iota(jnp.int32, (c, c), 0)
    cj = lax.broadcasted_iota(jnp.int32, (c, c), 1)
    tril = ri >= cj
    strict = ri > cj
    gc = _cumsum_rows(g_f)
    gct = gc.T
    z = z_ref[...]
    nw = nw_ref[...]

    heads = range(GDN_HEADS)
    sls = [slice(h * GDN_HD, (h + 1) * GDN_HD) for h in heads]
    q_h = [conv[:, sl] for sl in sls]
    k_h = [conv[:, GDN_W + h * GDN_HD:GDN_W + (h + 1) * GDN_HD] for h in heads]
    v_h = [conv[:, 2 * GDN_W + h * GDN_HD:2 * GDN_W + (h + 1) * GDN_HD] for h in heads]
    q_h = [q * lax.rsqrt(jnp.sum(q * q, axis=-1, keepdims=True) + EPS) * (GDN_HD ** -0.5) for q in q_h]
    k_h = [k * lax.rsqrt(jnp.sum(k * k, axis=-1, keepdims=True) + EPS) for k in k_h]
    g_col = [gc[:, 8 + h:9 + h] for h in heads]
    eg = [jnp.exp(g) for g in g_col]
    b_col = [beta_f[:, h:h + 1] for h in heads]
    decay = [jnp.where(tril, jnp.exp(jnp.where(tril, g_col[h] - gct[8 + h:9 + h, :], 0.0)), 0.0) for h in heads]
    kb = [k_h[h] * b_col[h] for h in heads]
    vb = [v_h[h] * b_col[h] for h in heads]
    m_h = [jnp.where(strict, _mm3(kb[h], k_h[h], NT) * decay[h], 0.0) for h in heads]
    qk = [jnp.where(tril, _mm(q_h[h], k_h[h], NT) * decay[h], 0.0) for h in heads]
    t_inv = _tri_inverse(m_h, c)
    u_h = [_mm(t_inv[h], vb[h]) for h in heads]
    w_h = [_mm(t_inv[h], kb[h] * eg[h]) for h in heads]
    s_h = [s_ref[h] for h in heads]
    v_new = [u_h[h] - _mm(w_h[h], s_h[h]) for h in heads]
    o1 = [_mm(q_h[h] * eg[h], s_h[h]) for h in heads]
    o_h = [o1[h] + _mm(qk[h], v_new[h]) for h in heads]
    g_last = [g[c - 1:c, :] for g in g_col]
    ds = [_mm(k_h[h] * jnp.exp(g_last[h] - g_col[h]), v_new[h], TN) for h in heads]
    for h in heads:
        s_ref[h] = s_h[h] * jnp.exp(g_last[h]) + ds[h]
        o_n = o_h[h] * lax.rsqrt(jnp.mean(o_h[h] * o_h[h], axis=-1, keepdims=True) + EPS) * nw
        o_ref[:, sls[h]] = o_n * _silu(z[:, sls[h]])

    @pl.when(ci == n_chunks - 1)
    def _():
        sfin_ref[0] = s_ref[...]


def gdn_mix(qkv, z, ba, cs, s0, conv_w8, hp, nw, *, c, valid):
    b = s0.shape[0]
    rows = qkv.shape[0]
    n_chunks = rows // (b * c)
    const = lambda s: pl.BlockSpec(s, lambda bi, ci: tuple(0 for _ in s))
    row = lambda w: pl.BlockSpec((c, w), lambda bi, ci: (bi * n_chunks + ci, 0))
    return pl.pallas_call(
        functools.partial(_gdn_body, c=c, valid=valid, n_chunks=n_chunks),
        grid=(b, n_chunks),
        in_specs=[row(3072), row(1024), row(128),
                  pl.BlockSpec((1, 8, 3072), lambda bi, ci: (bi, 0, 0)),
                  pl.BlockSpec((1, GDN_HEADS, 128, 128), lambda bi, ci: (bi, 0, 0, 0)),
                  const((8, 3072)), const((8, 128)), const((1, 128))],
        out_specs=[row(1024), pl.BlockSpec((1, GDN_HEADS, 128, 128), lambda bi, ci: (bi, 0, 0, 0))],
        out_shape=[jax.ShapeDtypeStruct((rows, 1024), F32),
                   jax.ShapeDtypeStruct((b, GDN_HEADS, 128, 128), F32)],
        scratch_shapes=[pltpu.VMEM((8, 3072), F32), pltpu.VMEM((GDN_HEADS, 128, 128), F32)],
        compiler_params=_cparams(("arbitrary", "arbitrary")),
        name="gdn_mix",
    )(qkv, z, ba, cs, s0, conv_w8, hp, nw)


def _router_body(x_ref, nw_ref, sc_ref, sh_ref, wr_ref, br_ref, h_ref, gate_ref):
    h = _norm_mod(x_ref[...], nw_ref[...], sc_ref[...], sh_ref[...])
    h_ref[...] = h.astype(BF16)
    logits = _mmh(h, wr_ref[...]) + br_ref[...]
    tm = logits.shape[0]
    lane = lax.broadcasted_iota(jnp.int32, (tm, LANE), 1)
    is_grp = (lane >= N_EXPERTS) & (lane < N_EXPERTS + N_GROUPS)
    gl = jnp.where(is_grp, logits, NEG)
    gmax = jnp.max(gl, axis=-1, keepdims=True)
    g_idx = jnp.min(jnp.where(gl == gmax, lane, 4 * LANE), axis=-1, keepdims=True) - N_EXPERTS
    g_w = 1.0 / jnp.sum(jnp.where(is_grp, jnp.exp(gl - gmax), 0.0), axis=-1, keepdims=True)
    in_grp = (lane < N_EXPERTS) & (lane // EXP_PER_GROUP == g_idx)
    el = jnp.where(in_grp, logits, NEG)
    emax = jnp.max(el, axis=-1, keepdims=True)
    e = jnp.where(in_grp, jnp.exp(el - emax), 0.0)
    p = e / jnp.sum(e, axis=-1, keepdims=True)
    p1 = jnp.where(in_grp, p, -1.0)
    m1 = jnp.max(p1, axis=-1, keepdims=True)
    i1 = jnp.min(jnp.where(p1 == m1, lane, 4 * LANE), axis=-1, keepdims=True)
    p2 = jnp.where(lane == i1, -1.0, p1)
    m2 = jnp.max(p2, axis=-1, keepdims=True)
    i2 = jnp.min(jnp.where(p2 == m2, lane, 4 * LANE), axis=-1, keepdims=True)
    tot = m1 + m2
    gate = jnp.where(lane == i1, m1 / tot * g_w, jnp.where(lane == i2, m2 / tot * g_w, 0.0))
    gate_ref[...] = jnp.where(lane == GRP_LANE, g_idx.astype(F32), gate)


def moe_router(x, nw, sc, sh, w_r, b_r, tm):
    n, d = x.shape
    rows_mod = sc.shape[0]
    return pl.pallas_call(
        _router_body,
        grid=(n // tm,),
        in_specs=[pl.BlockSpec((tm, d), lambda i: (i, 0)), pl.BlockSpec((1, d), lambda i: (0, 0)),
                  _mod_spec(rows_mod, tm, d), _mod_spec(rows_mod, tm, d),
                  pl.BlockSpec((d, LANE), lambda i: (0, 0)), pl.BlockSpec((1, LANE), lambda i: (0, 0))],
        out_specs=[pl.BlockSpec((tm, d), lambda i: (i, 0)), pl.BlockSpec((tm, LANE), lambda i: (i, 0))],
        out_shape=[jax.ShapeDtypeStruct((n, d), BF16), jax.ShapeDtypeStruct((n, LANE), F32)],
        compiler_params=_cparams(("arbitrary",)),
        name="moe_router",
    )(x, nw, sc, sh, w_r, b_r)


def _moe_body(h_ref, gate_ref, w1_ref, w3_ref, w2_ref, x_ref, g2_ref, o_ref, acc_ref):
    e = pl.program_id(1)

    @pl.when(e == 0)
    def _():
        acc_ref[...] = jnp.zeros_like(acc_ref)

    hb = h_ref[...]
    he = _silu(_mm(hb, w1_ref[0])) * _mm(hb, w3_ref[0])
    y = _mm(he, w2_ref[0])
    gate = gate_ref[...]
    lane = lax.broadcasted_iota(jnp.int32, gate.shape, 1)
    ge = jnp.sum(jnp.where(lane == e, gate, 0.0), axis=-1, keepdims=True)
    acc_ref[...] += ge * y

    @pl.when(e == pl.num_programs(1) - 1)
    def _():
        o_ref[...] = x_ref[...] + g2_ref[...] * acc_ref[...]


def moe_ffn(h, gate, w1, w3, w2, e0, x, g2, tm):
    n, d = x.shape
    de = w1.shape[2]
    return pl.pallas_call(
        _moe_body,
        grid=(n // tm, N_EXPERTS),
        in_specs=[pl.BlockSpec((tm, d), lambda i, e: (i, 0)), pl.BlockSpec((tm, LANE), lambda i, e: (i, 0)),
                  pl.BlockSpec((1, d, de), lambda i, e: (e0 + e, 0, 0)),
                  pl.BlockSpec((1, d, de), lambda i, e: (e0 + e, 0, 0)),
                  pl.BlockSpec((1, de, d), lambda i, e: (e0 + e, 0, 0)),
                  pl.BlockSpec((tm, d), lambda i, e: (i, 0)),
                  pl.BlockSpec((1, d), lambda i, e: (0, 0)) if g2.shape[0] == 1
                  else pl.BlockSpec((tm, d), lambda i, e: (i, 0))],
        out_specs=pl.BlockSpec((tm, d), lambda i, e: (i, 0)),
        out_shape=jax.ShapeDtypeStruct((n, d), F32),
        scratch_shapes=[pltpu.VMEM((tm, d), F32)],
        compiler_params=_cparams(("arbitrary", "arbitrary")),
        name="moe_ffn",
    )(h, gate, w1, w3, w2, x, g2)


def _moe_plan(grp, tm, tw, tb, cap, max_entries):
    nt = grp.shape[0] // tm
    cnt = jax.nn.one_hot(grp, N_GROUPS, dtype=jnp.int32).reshape(nt, tm, N_GROUPS).sum(axis=1)
    pc = (cnt + ROW_ALIGN - 1) // ROW_ALIGN * ROW_ALIGN
    segb = jnp.cumsum(pc, axis=1) - pc
    off = jnp.cumsum(pc, axis=0) - pc
    tot = pc.sum(axis=0)
    n_real = (tot + tb - 1) // tb
    n_all = jnp.minimum((tot + tw + tb - 1) // tb, cap // tb)
    ends = jnp.cumsum(n_all)
    s = jnp.arange(max_entries)
    g_of = jnp.sum(s[:, None] >= ends[None, :], axis=1)
    active = g_of < N_GROUPS
    g_c = jnp.minimum(g_of, N_GROUPS - 1)
    rt = s - (ends - n_all)[g_c]
    live = tot[g_c] - rt * tb
    real = jnp.where(live <= tb // 4, 3, jnp.where(live <= tb // 2, 5, 1))
    kind = jnp.where(active, jnp.where(rt < n_real[g_c], real, 2), 0)
    last = ends[-1] - 1
    e_grp = jnp.where(active, g_c, g_c[last])
    e_rt = jnp.where(active, rt, rt[last])
    i32 = lambda a: a.reshape(-1).astype(jnp.int32)
    return i32(segb), i32(off // ROW_ALIGN), i32(e_grp), i32(e_rt), i32(kind)


def _group_perm(gate, segb_ref, base, tm, rows):
    gt = gate.T
    grp = gt[GRP_LANE:GRP_LANE + 1, :]
    gi = lax.broadcasted_iota(jnp.int32, (8, tm), 0).astype(F32)
    oh = jnp.where(gi == grp, 1.0, 0.0)
    r_i = lax.broadcasted_iota(jnp.int32, (tm, tm), 0)
    c_i = lax.broadcasted_iota(jnp.int32, (tm, tm), 1)
    before = jnp.where(r_i < c_i, 1.0, 0.0).astype(BF16)
    rank = lax.dot_general(oh.astype(BF16), before, NN, preferred_element_type=F32)
    dest = jnp.zeros((1, tm), F32)
    for g in range(N_GROUPS):
        dest = dest + oh[g:g + 1] * (segb_ref[base + g].astype(F32) + rank[g:g + 1])
    rows_i = lax.broadcasted_iota(jnp.int32, (rows, tm), 0).astype(F32)
    return jnp.where(rows_i == dest, 1.0, 0.0).astype(BF16)


def _moe_dispatch_body(segb_ref, off_ref, h_ref, gate_ref, xg_in, gg_in, xg_ref, gg_ref, xs_ref, gs_ref, *, tm, tw,
                       rows):
    del off_ref, xg_in, gg_in
    i = pl.program_id(0)
    g = pl.program_id(1)

    @pl.when((i == 0) & (g == 0))
    def _():
        xs_ref[...] = jnp.zeros_like(xs_ref)
        gs_ref[...] = jnp.zeros_like(gs_ref)

    @pl.when(g == 0)
    def _():
        gate = gate_ref[...]
        p = _group_perm(gate, segb_ref, i * N_GROUPS, tm, rows)
        xs_ref[0:rows, :] = lax.dot_general(p, h_ref[...], NN, preferred_element_type=F32).astype(BF16)
        gs_ref[0:rows, :] = _mm01(p, gate)

    start = pl.multiple_of(segb_ref[i * N_GROUPS + g], ROW_ALIGN)
    xg_ref[...] = xs_ref[pl.ds(start, tw), :]
    gg_ref[...] = gs_ref[pl.ds(start, tw), :]


def moe_dispatch(h, gate, segb, off, tm, tw, cap):
    n, d = h.shape
    rows = tm + N_GROUPS * ROW_ALIGN
    win = lambda w: pl.BlockSpec((pl.Element(tw), pl.Element(w)),
                                 lambda i, g, segb, off: ((g * (cap // ROW_ALIGN) + off[i * N_GROUPS + g]) * ROW_ALIGN, 0))
    grid_spec = pltpu.PrefetchScalarGridSpec(
        num_scalar_prefetch=2,
        grid=(n // tm, N_GROUPS),
        in_specs=[pl.BlockSpec((tm, d), lambda i, g, *_: (i, 0)), pl.BlockSpec((tm, LANE), lambda i, g, *_: (i, 0)),
                  pl.BlockSpec(memory_space=pl.ANY), pl.BlockSpec(memory_space=pl.ANY)],
        out_specs=[win(d), win(LANE)],
        scratch_shapes=[pltpu.VMEM((rows + tw, d), BF16), pltpu.VMEM((rows + tw, LANE), F32)],
    )
    return pl.pallas_call(
        functools.partial(_moe_dispatch_body, tm=tm, tw=tw, rows=rows),
        grid_spec=grid_spec,
        out_shape=[jax.ShapeDtypeStruct((N_GROUPS * cap, d), BF16), jax.ShapeDtypeStruct((N_GROUPS * cap, LANE), F32)],
        input_output_aliases={4: 0, 5: 1},
        compiler_params=_cparams(("arbitrary", "arbitrary")),
        name="moe_dispatch",
    )(segb, off, h, gate, jnp.zeros((N_GROUPS * cap, d), BF16), jnp.zeros((N_GROUPS * cap, LANE), F32))


def _moe_group_body(grp_ref, rt_ref, kind_ref, xg_ref, gg_ref, w1_ref, w3_ref, w2_ref, yg_ref, acc_ref):
    del rt_ref
    s = pl.program_id(0)
    e = pl.program_id(1)
    kind = kind_ref[s]
    last = e == pl.num_programs(1) - 1

    def run(rows):
        @pl.when(e == 0)
        def _():
            acc_ref[...] = jnp.zeros_like(acc_ref)

        xb = xg_ref[0:rows, :]
        he = _silu(_mm(xb, w1_ref[0])) * _mm(xb, w3_ref[0])
        y = _mm(he, w2_ref[0])
        gate = gg_ref[0:rows, :]
        lane = lax.broadcasted_iota(jnp.int32, gate.shape, 1)
        ge = jnp.sum(jnp.where(lane == grp_ref[s] * EXP_PER_GROUP + e, gate, 0.0), axis=-1, keepdims=True)
        acc_ref[0:rows, :] += ge * y

        @pl.when(last)
        def _():
            yg_ref[...] = acc_ref[...]

    tb = xg_ref.shape[0]
    for code, rows in ((1, tb), (5, tb // 2), (3, tb // 4)):
        pl.when(kind == code)(functools.partial(run, rows))

    @pl.when((kind == 2) & last)
    def _():
        yg_ref[...] = jnp.zeros_like(yg_ref)


def moe_group_ffn(e_grp, e_rt, e_kind, xg, gg, w1, w3, w2, e0, tb, cap):
    d = xg.shape[1]
    de = w1.shape[2]
    row = lambda s, e, grp, rt, kind: (grp[s] * (cap // tb) + rt[s], 0)
    wsel = lambda s, e, grp, rt, kind: (e0 + grp[s] * EXP_PER_GROUP + jnp.where(kind[s] % 2 == 1, e, EXP_PER_GROUP - 1),
                                        0, 0)
    grid_spec = pltpu.PrefetchScalarGridSpec(
        num_scalar_prefetch=3,
        grid=(e_grp.shape[0], EXP_PER_GROUP),
        in_specs=[pl.BlockSpec((tb, d), row), pl.BlockSpec((tb, LANE), row),
                  pl.BlockSpec((1, d, de), wsel), pl.BlockSpec((1, d, de), wsel), pl.BlockSpec((1, de, d), wsel)],
        out_specs=pl.BlockSpec((tb, d), row),
        scratch_shapes=[pltpu.VMEM((tb, d), F32)],
    )
    return pl.pallas_call(
        _moe_group_body,
        grid_spec=grid_spec,
        out_shape=jax.ShapeDtypeStruct((N_GROUPS * cap, d), F32),
        compiler_params=_cparams(("arbitrary", "arbitrary")),
        name="moe_group_ffn",
    )(e_grp, e_rt, e_kind, xg, gg, w1, w3, w2)


def _moe_combine_body(segb_ref, off_ref, yg_ref, gate_ref, x_ref, g2_ref, o_ref, ys_ref, *, tm, tw, rows):
    del off_ref
    i = pl.program_id(0)
    g = pl.program_id(1)

    @pl.when((i == 0) & (g == 0))
    def _():
        ys_ref[...] = jnp.zeros_like(ys_ref)

    start = pl.multiple_of(segb_ref[i * N_GROUPS + g], ROW_ALIGN)
    ys_ref[pl.ds(start, tw), :] = yg_ref[...]

    @pl.when(g == N_GROUPS - 1)
    def _():
        p = _group_perm(gate_ref[...], segb_ref, i * N_GROUPS, tm, rows)
        yh, yl = _split(ys_ref[0:rows, :])
        y = (lax.dot_general(p, yh, TN, preferred_element_type=F32)
             + lax.dot_general(p, yl, TN, preferred_element_type=F32))
        o_ref[...] = x_ref[...] + g2_ref[...] * y


def moe_combine(yg, gate, x, g2, segb, off, tm, tw, cap):
    n, d = x.shape
    rows = tm + N_GROUPS * ROW_ALIGN
    grid_spec = pltpu.PrefetchScalarGridSpec(
        num_scalar_prefetch=2,
        grid=(n // tm, N_GROUPS),
        in_specs=[pl.BlockSpec((pl.Element(tw), pl.Element(d)),
                               lambda i, g, segb, off: ((g * (cap // ROW_ALIGN) + off[i * N_GROUPS + g]) * ROW_ALIGN, 0)),
                  pl.BlockSpec((tm, LANE), lambda i, g, *_: (i, 0)),
                  pl.BlockSpec((tm, d), lambda i, g, *_: (i, 0)),
                  pl.BlockSpec((1, d), lambda i, g, *_: (0, 0))],
        out_specs=pl.BlockSpec((tm, d), lambda i, g, *_: (i, 0)),
        scratch_shapes=[pltpu.VMEM((rows + tw, d), F32)],
    )
    return pl.pallas_call(
        functools.partial(_moe_combine_body, tm=tm, tw=tw, rows=rows),
        grid_spec=grid_spec,
        out_shape=jax.ShapeDtypeStruct((n, d), F32),
        compiler_params=_cparams(("arbitrary", "arbitrary")),
        name="moe_combine",
    )(segb, off, yg, gate, x, g2)


def moe_grouped(h, gate, w1, w3, w2, e0, x, g2, tm):
    n = h.shape[0]
    tb = tm
    cap = n + 2 * tm
    max_entries = (n + (n // tm) * N_GROUPS * (ROW_ALIGN - 1) + N_GROUPS * tm) // tb + N_GROUPS + 1
    grp = gate[:, GRP_LANE].astype(jnp.int32)

    def run(tw):
        segb, off, e_grp, e_rt, e_kind = _moe_plan(grp, tm, tw, tb, cap, max_entries)
        xg, gg = moe_dispatch(h, gate, segb, off, tm, tw, cap)
        yg = moe_group_ffn(e_grp, e_rt, e_kind, xg, gg, w1, w3, w2, e0, tb, cap)
        return moe_combine(yg, gate, x, g2, segb, off, tm, tw, cap)

    seg_max = jnp.max(jax.nn.one_hot(grp, N_GROUPS, dtype=jnp.int32).reshape(n // tm, tm, N_GROUPS).sum(axis=1))
    return lax.cond(seg_max <= tm // 2, lambda: run(tm // 2), lambda: run(tm))


def _final_norm_body(x_ref, w_ref, o_ref):
    x = x_ref[...]
    o_ref[...] = x * lax.rsqrt(jnp.mean(x * x, axis=-1, keepdims=True) + EPS) * w_ref[...]


def final_norm(x, w, tm):
    n, d = x.shape
    return pl.pallas_call(
        _final_norm_body,
        grid=(n // tm,),
        in_specs=[pl.BlockSpec((tm, d), lambda i: (i, 0)), pl.BlockSpec((1, d), lambda i: (0, 0))],
        out_specs=pl.BlockSpec((tm, d), lambda i: (i, 0)),
        out_shape=jax.ShapeDtypeStruct((n, d), F32),
        compiler_params=_cparams(("arbitrary",)),
        name="final_norm",
    )(x, w)


def _row_tile(n, pref):
    t = min(pref, n)
    while n % t:
        t //= 2
    return t


def kernel(x_prompt, x_sample, c_prompt, c_sample, cache_nsa_cmp, cache_nsa_sel, page_table, state_nsa_win, state_rwkv, state_rwkv_shift, state_gdn, state_gdn_conv, norm_mix, norm_ffn, norm_final, w_ada, b_ada, even_w_in, even_w_out, nsa_cmp_pos, nsa_cmp_w, rwkv_mu, rwkv_w0, rwkv_w2, rwkv_a0, rwkv_a2, rwkv_g2, rwkv_kk, rwkv_ka, rwkv_rk, rwkv_ln_w, rwkv_ln_b, odd_w_in, odd_w_out, gdn_conv_w, gdn_a_log, gdn_dt_bias, gdn_norm_w, moe_w_grp, moe_b_grp, moe_w_exp, moe_b_exp, moe_w1, moe_w3, moe_w2):
    bp, t, d = x_prompt.shape
    bs, ts, _ = x_sample.shape
    assert bp == 1 and ts <= SPAD and ts < CMP_BLK
    depth = norm_mix.shape[0]
    n_pages, page = page_table.shape[1], cache_nsa_cmp.shape[2]
    past = n_pages * page
    wb = state_nsa_win.shape[2]
    ns = bs * SPAD
    tq, tq_s, tk = NSA_TQ, NSA_TQ_SAMPLE, NSA_TK
    tm_p = _row_tile(t, ROW_TILE)
    tm_s = ns

    rows_c = -(-(1 + bs) // 8) * 8
    c_all = jnp.concatenate([c_prompt, c_sample, jnp.zeros((rows_c - 1 - bs, d), F32)], axis=0)
    ada = adaln(c_all, w_ada, b_ada)

    def mods(i):
        mp = [ada[i, 0:1, j * d:(j + 1) * d] for j in range(6)]
        ms = [jnp.repeat(ada[i, 1:1 + bs, j * d:(j + 1) * d], SPAD, axis=0) for j in range(6)]
        return mp, ms

    xp = x_prompt[0]
    xs = jnp.pad(x_sample, ((0, 0), (0, SPAD - ts), (0, 0))).reshape(ns, d)

    def unpad(a):
        return a.reshape(bs, SPAD, -1)[:, :ts]

    w1_all, w3_all, w2_all = (w.reshape((-1,) + w.shape[2:]) for w in (moe_w1, moe_w3, moe_w2))
    outs = {k: [] for k in ("cmp_p", "cmp_s", "sel_p", "sel_s", "win_p", "win_s", "rw_p", "rw_s", "sh_p", "sh_s",
                            "gd_p", "gd_s", "cv_p", "cv_s")}

    for i in range(depth):
        (sh1p, sc1p, gt1p, sh2p, sc2p, gt2p), (sh1s, sc1s, gt1s, sh2s, sc2s, gt2s) = mods(i)
        j = i // 2
        nw = norm_mix[i][None, :]
        if i % 2 == 0:
            w_packed = _pack_even_w(even_w_in[j])
            mu = _pack_rw_vec(rwkv_mu[j])
            wts, wc = _cmp_weights(nsa_cmp_pos[j], nsa_cmp_w[j])
            vec = jnp.stack([rwkv_w0[j], rwkv_a0[j], rwkv_kk[j], rwkv_ka[j], rwkv_ln_w[j], rwkv_ln_b[j],
                             jnp.zeros_like(rwkv_w0[j]), jnp.zeros_like(rwkv_w0[j])])
            pad_lora = lambda w: jnp.concatenate([w, jnp.zeros((128 - w.shape[0], w.shape[1]), w.dtype)], axis=0)
            w2p, a2p, g2p = pad_lora(rwkv_w2[j]), pad_lora(rwkv_a2[j]), rwkv_g2[j]
            hid = jnp.arange(RWKV_W) // RWKV_HD
            seg = (hid[:, None] == hid[None, :]).astype(F32)[:RWKV_W // 2, :RWKV_W // 2]
            rk = rwkv_rk[j].reshape(1, RWKV_W)
            wo_nsa, wo_rw = even_w_out[j][:512].astype(BF16), even_w_out[j][512:].astype(BF16)

            kv, qt, gt, ks, vst, kw, vwt, rw, hl = even_proj(xp, nw, sc1p, sh1p, w_packed, tm_p, 8)
            kvc = compress_prompt(kv, wts, wc, tm_p)
            gates = gt[:24].reshape(NSA_KV_HEADS, 12, t)
            gates = jnp.pad(gates, ((0, 0), (0, 4), (0, 0)))[None]
            o_nsa = nsa_attention(
                qt[None], gates, kvc[None], kvc.T[None], ks[None], vst[None], kw[None], vwt[None],
                tq=tq, tk=tk, wk=WINDOW + tq,
                pos0_fn=lambda qi: qi * tq,
                wstart_fn=lambda qi: jnp.maximum(qi * tq - WINDOW, 0),
                wpos0_fn=lambda qi: jnp.maximum(qi * tq - WINDOW, 0))[0]
            o_rw, s_rw = rwkv_mix(rw, jnp.zeros((1, 8, RW_COLS), F32), jnp.zeros((1, RWKV_HEADS, 64, 64), F32),
                                  mu, vec, w2p, a2p, g2p, seg, rk, c=CHUNK, valid=CHUNK)
            xp = out_proj([o_nsa, o_rw], [wo_nsa, wo_rw], xp, gt1p, tm_p)
            outs["cmp_p"].append(kv[:, 0:256].reshape(1, t, 2, 2, 64))
            outs["sel_p"].append(kv[:, 256:512].reshape(1, t, 2, 2, 64))
            kvw_rows = kv[:, 512:768].reshape(1, t, 2, 2, 64)
            outs["win_p"].append(kvw_rows[:, -min(WINDOW, t):])
            outs["rw_p"].append(s_rw)
            outs["sh_p"].append(hl[-1:])

            kv, qt, gt, _, _, _, _, rw, hl = even_proj(xs, nw, sc1s, sh1s, w_packed, tm_s, ns)
            kv_new = unpad(kv)
            rw0 = small_matmul(jnp.pad(state_rwkv_shift[j], ((0, -bs % 8), (0, 0))), w_packed[:, E_RW:])[:bs]
            rw0 = jnp.pad(rw0[:, None, :], ((0, 0), (7, 0), (0, 0)))
            pool_cmp = cache_nsa_cmp[j].transpose(0, 2, 3, 4, 1).reshape(-1, 256, page)
            pool_sel = cache_nsa_sel[j].transpose(0, 2, 3, 4, 1).reshape(-1, 256, page)
            kvc_s = compress_paged(pool_cmp, page_table, nsa_cmp_pos[j], wc, math.gcd(n_pages, CMP_PAGES_PER_STEP))
            tail = jnp.pad(kv_new[:, :, 256:512], ((0, 0), (0, tk - ts), (0, 0)))
            wbuf = state_nsa_win[j].reshape(bs, wb, 256)
            kvw_all = jnp.concatenate([wbuf, kv_new[:, :, 512:768]], axis=1)
            wk_s = -(-(wb + ts) // 128) * 128
            kvw_pad = jnp.pad(kvw_all, ((0, 0), (0, wk_s - wb - ts), (0, 0)))
            kw_s = kvw_pad[:, :, :128].astype(BF16)
            vwt_s = jnp.swapaxes(kvw_pad[:, :, 128:], 1, 2).astype(BF16)
            qt_s = jnp.pad(qt.reshape(512, bs, SPAD).transpose(1, 0, 2), ((0, 0), (0, 0), (0, tq_s - SPAD)))
            g_s = gt[:24].reshape(NSA_KV_HEADS, 12, bs, SPAD).transpose(2, 0, 1, 3)
            g_s = jnp.pad(g_s, ((0, 0), (0, 0), (0, 4), (0, tq_s - SPAD)))
            o_nsa = nsa_attention_paged(
                qt_s, g_s, kvc_s, jnp.swapaxes(kvc_s, 1, 2), pool_sel, page_table, tail, kw_s, vwt_s,
                tq=tq_s, tk=tk, wk=wk_s,
                pos0_fn=lambda qi: past,
                wstart_fn=lambda qi: 0,
                wpos0_fn=lambda qi: past - wb)
            o_nsa = o_nsa[:, :SPAD].reshape(ns, 512)
            o_rw, s_rw = rwkv_mix(rw, rw0, state_rwkv[j], mu, vec, w2p, a2p, g2p, seg, rk, c=SPAD, valid=ts)
            xs = out_proj([o_nsa, o_rw], [wo_nsa, wo_rw], xs, gt1s, tm_s)
            outs["cmp_s"].append(kv_new[:, :, 0:256].reshape(bs, ts, 2, 2, 64))
            outs["sel_s"].append(kv_new[:, :, 256:512].reshape(bs, ts, 2, 2, 64))
            outs["win_s"].append(kvw_all[:, -wb:].reshape(bs, wb, 2, 2, 64))
            outs["rw_s"].append(s_rw)
            outs["sh_s"].append(hl.reshape(bs, SPAD, d)[:, ts - 1])
        else:
            w_in = odd_w_in[j]
            w_packed = jnp.concatenate([w_in, jnp.zeros((d, O_COLS - w_in.shape[1]), F32)], axis=1).astype(BF16)
            conv_w8 = jnp.pad(gdn_conv_w[j], ((0, 8 - CONV_W), (0, 0)))
            hp = jnp.zeros((8, 128), F32)
            hp = hp.at[0, 8:16].set(-jnp.exp(gdn_a_log[j])).at[1, 8:16].set(gdn_dt_bias[j])
            gnw = gdn_norm_w[j][None, :]
            wo = odd_w_out[j].astype(BF16)

            qkv, z, ba = odd_proj(xp, nw, sc1p, sh1p, w_packed, tm_p)
            o_g, s_g = gdn_mix(qkv, z, ba, jnp.zeros((1, 8, 3 * GDN_W), F32),
                               jnp.zeros((1, GDN_HEADS, GDN_HD, GDN_HD), F32), conv_w8, hp, gnw, c=CHUNK, valid=CHUNK)
            xp = out_proj([o_g], [wo], xp, gt1p, tm_p)
            outs["gd_p"].append(s_g)
            outs["cv_p"].append(qkv[None, -(CONV_W - 1):])

            qkv, z, ba = odd_proj(xs, nw, sc1s, sh1s, w_packed, tm_s)
            cs = jnp.pad(state_gdn_conv[j], ((0, 0), (8 - (CONV_W - 1), 0), (0, 0)))
            o_g, s_g = gdn_mix(qkv, z, ba, cs, state_gdn[j], conv_w8, hp, gnw, c=SPAD, valid=ts)
            xs = out_proj([o_g], [wo], xs, gt1s, tm_s)
            xpad = jnp.concatenate([state_gdn_conv[j], unpad(qkv)], axis=1)
            outs["gd_s"].append(s_g)
            outs["cv_s"].append(xpad[:, -(CONV_W - 1):])

        nwf = norm_ffn[i][None, :]
        w_r = jnp.concatenate([moe_w_exp[i], moe_w_grp[i], jnp.zeros((d, LANE - N_EXPERTS - N_GROUPS), F32)], axis=1)
        b_r = jnp.concatenate([moe_b_exp[i], moe_b_grp[i], jnp.zeros((LANE - N_EXPERTS - N_GROUPS,), F32)])[None, :]
        h2, gate = moe_router(xp, nwf, sc2p, sh2p, w_r, b_r, tm_p)
        xp = moe_grouped(h2, gate, w1_all, w3_all, w2_all, i * N_EXPERTS, xp, gt2p, _row_tile(t, MOE_ROW_TILE))
        h2, gate = moe_router(xs, nwf, sc2s, sh2s, w_r, b_r, tm_s)
        xs = moe_ffn(h2, gate, w1_all, w3_all, w2_all, i * N_EXPERTS, xs, gt2s, tm_s)

    nf = norm_final[None, :]
    y_prompt = final_norm(xp, nf, tm_p)[None]
    y_sample = unpad(final_norm(xs, nf, tm_s))
    st = lambda key: jnp.stack(outs[key])
    return (y_prompt, y_sample, st("cmp_p"), st("cmp_s"), st("sel_p"), st("sel_s"), st("win_p"), st("win_s"),
            st("rw_p"), st("rw_s"), st("sh_p"), st("sh_s"), st("gd_p"), st("gd_s"), st("cv_p"), st("cv_s"))
```

```python
import functools
import math

import jax
import jax.numpy as jnp
from jax import lax
from jax.experimental import pallas as pl
from jax.experimental.pallas import tpu as pltpu

F32 = jnp.float32
BF16 = jnp.bfloat16
HIGHEST = lax.Precision.HIGHEST

NSA_HEADS = 8
NSA_KV_HEADS = 2
NSA_GROUP = 4
NSA_HD = 64
CMP_BLK = 64
SEL_BLK = 64
TOPK_BLK = 16
WINDOW = 512
FORCE_BONUS = 2.0 * NSA_GROUP
RWKV_HEADS = 8
RWKV_HD = 64
RWKV_W = 512
RWKV_GN_EPS = 64e-5
GDN_HEADS = 8
GDN_HD = 128
GDN_W = 1024
CONV_W = 4
N_GROUPS = 4
EXP_PER_GROUP = 8
N_EXPERTS = 32
EPS = 1e-6
NEG = -1e30

LANE = 128
GRP_LANE = 64
ROW_ALIGN = 16
SAMPLE_TILE_SLOTS = 8
SPAD = 8
VMEM_LIMIT = 56 * 1024 * 1024

ROW_TILE = 512
MOE_ROW_TILE = 1024
NSA_TQ = 128
NSA_TQ_SAMPLE = 32
NSA_TK = 512
CHUNK = 64
CMP_PAGES_PER_STEP = 32

NN = (((1,), (0,)), ((), ()))
NT = (((1,), (1,)), ((), ()))
TN = (((0,), (0,)), ((), ()))

E_Q, E_KV, E_G, E_RW = 0, 512, 1280, 1408
E_COLS = 1408 + 1920
RW_COLS = 1920
O_COLS = 3072 + 1024 + 128


def _mm(a, b, dims=NN):
    return lax.dot_general(a.astype(BF16), b.astype(BF16), dims, preferred_element_type=F32)


def _mmh(a, b, dims=NN):
    return lax.dot_general(a.astype(F32), b.astype(F32), dims, precision=HIGHEST, preferred_element_type=F32)


def _split(a):
    hi = a.astype(BF16)
    return hi, (a - hi.astype(F32)).astype(BF16)


def _mm3(a, b, dims=NN):
    ah, al = _split(a)
    bh, bl = _split(b)
    d = lambda x, y: lax.dot_general(x, y, dims, preferred_element_type=F32)
    return d(ah, bh) + (d(ah, bl) + d(al, bh))


def _split3(x):
    h1 = x.astype(BF16)
    r1 = x - h1.astype(F32)
    h2 = r1.astype(BF16)
    return h1, h2, (r1 - h2.astype(F32)).astype(BF16)


def _mm01(m01, x):
    m = m01.astype(BF16)
    parts = _split3(x)
    d = lambda y: lax.dot_general(m, y, NN, preferred_element_type=F32)
    return d(parts[0]) + (d(parts[1]) + d(parts[2]))


def _cumsum_rows(x):
    n = x.shape[0]
    row = lax.broadcasted_iota(jnp.int32, x.shape, 0)
    shift = 1
    while shift < n:
        x = x + jnp.where(row >= shift, pltpu.roll(x, shift, 0), 0.0)
        shift *= 2
    return x


def _shift_rows(x, prev8, sh):
    rolled = pltpu.roll(x, sh, 0)
    row8 = lax.broadcasted_iota(jnp.int32, (8, 1), 0)
    head = jnp.where(row8 < sh, pltpu.roll(prev8, sh, 0), rolled[0:8])
    return head if x.shape[0] == 8 else jnp.concatenate([head, rolled[8:]], axis=0)


def _head_sums(xs, seg_half):
    r = xs[0].shape[0]
    half = seg_half.shape[0]
    pieces = [p[:, h0:h0 + half] for x in xs for p in _split3(x) for h0 in (0, half)]
    out = lax.dot_general(jnp.concatenate(pieces, axis=0), seg_half.astype(BF16), NN, preferred_element_type=F32)
    res = []
    for i in range(len(xs)):
        o = [out[(6 * i + u) * r:(6 * i + u + 1) * r] for u in range(6)]
        res.append(jnp.concatenate([o[0] + (o[2] + o[4]), o[1] + (o[3] + o[5])], axis=1))
    return res


def _sigmoid(x):
    return 1.0 / (1.0 + jnp.exp(-x))


def _silu(x):
    return x * _sigmoid(x)


def _softplus(x):
    return jnp.maximum(x, 0.0) + jnp.log(1.0 + jnp.exp(-jnp.abs(x)))


def _cparams(sem):
    return pltpu.CompilerParams(dimension_semantics=sem, vmem_limit_bytes=VMEM_LIMIT)


def _norm_mod(x, nw, sc, sh):
    y = x * lax.rsqrt(jnp.mean(x * x, axis=-1, keepdims=True) + EPS)
    return (y * nw) * (1.0 + sc) + sh


def _mod_spec(rows_mod, tm, d):
    if rows_mod == 1:
        return pl.BlockSpec((1, d), lambda i: (0, 0))
    return pl.BlockSpec((tm, d), lambda i: (i, 0))


def _adaln_body(c_ref, w_ref, b_ref, o_ref):
    o_ref[0] = _mmh(_silu(c_ref[...]), w_ref[0]) + b_ref[0]


def adaln(c_all, w_ada, b_ada):
    depth, d, n6 = w_ada.shape
    rows = c_all.shape[0]
    tn = 768
    return pl.pallas_call(
        _adaln_body,
        grid=(depth, n6 // tn),
        in_specs=[pl.BlockSpec((rows, d), lambda l, j: (0, 0)),
                  pl.BlockSpec((1, d, tn), lambda l, j: (l, 0, j)),
                  pl.BlockSpec((1, 1, tn), lambda l, j: (l, 0, j))],
        out_specs=pl.BlockSpec((1, rows, tn), lambda l, j: (l, 0, j)),
        out_shape=jax.ShapeDtypeStruct((depth, rows, n6), F32),
        compiler_params=_cparams(("arbitrary", "arbitrary")),
        name="adaln",
    )(c_all, w_ada, b_ada.reshape(depth, 1, n6))


def _even_proj_body(x_ref, nw_ref, sc_ref, sh_ref, w_ref,
                    kv_ref, qt_ref, gt_ref, ks_ref, vst_ref, kw_ref, vwt_ref, rw_ref, hl_ref):
    h = _norm_mod(x_ref[...], nw_ref[...], sc_ref[...], sh_ref[...])
    hl = hl_ref.shape[0]
    hl_ref[...] = h[h.shape[0] - hl:, :]
    hb = h.astype(BF16)
    q = _mm(hb, w_ref[:, E_Q:E_Q + 512]) * (NSA_HD ** -0.5)
    qt_ref[...] = q.T.astype(BF16)
    kv = _mm(hb, w_ref[:, E_KV:E_KV + 768])
    kv_ref[...] = kv
    ks_ref[...] = kv[:, 256:384].astype(BF16)
    vst_ref[...] = kv[:, 384:512].T.astype(BF16)
    kw_ref[...] = kv[:, 512:640].astype(BF16)
    vwt_ref[...] = kv[:, 640:768].T.astype(BF16)
    g = _sigmoid(_mm(hb, w_ref[:, E_G:E_G + 128]))
    gt_ref[...] = g.T
    rw_ref[...] = _mm(hb, w_ref[:, E_RW:E_RW + RW_COLS])


def even_proj(x, nw, sc, sh, w_packed, tm, hl_rows):
    n, d = x.shape
    rows_mod = sc.shape[0]
    row = lambda c: pl.BlockSpec((tm, c), lambda i: (i, 0))
    col = lambda r: pl.BlockSpec((r, tm), lambda i: (0, i))
    return pl.pallas_call(
        _even_proj_body,
        grid=(n // tm,),
        in_specs=[row(d), pl.BlockSpec((1, d), lambda i: (0, 0)),
                  _mod_spec(rows_mod, tm, d), _mod_spec(rows_mod, tm, d),
                  pl.BlockSpec((d, E_COLS), lambda i: (0, 0))],
        out_specs=[row(768), col(512), col(128), row(128), col(128), row(128), col(128), row(RW_COLS),
                   pl.BlockSpec((hl_rows, d), lambda i: (0, 0))],
        out_shape=[jax.ShapeDtypeStruct((n, 768), F32),
                   jax.ShapeDtypeStruct((512, n), BF16),
                   jax.ShapeDtypeStruct((128, n), F32),
                   jax.ShapeDtypeStruct((n, 128), BF16),
                   jax.ShapeDtypeStruct((128, n), BF16),
                   jax.ShapeDtypeStruct((n, 128), BF16),
                   jax.ShapeDtypeStruct((128, n), BF16),
                   jax.ShapeDtypeStruct((n, RW_COLS), F32),
                   jax.ShapeDtypeStruct((hl_rows, d), F32)],
        compiler_params=_cparams(("arbitrary",)),
        name="even_proj",
    )(x, nw, sc, sh, w_packed)


def _pack_even_w(w_in):
    d = w_in.shape[0]
    z = lambda c: jnp.zeros((d, c), w_in.dtype)
    nsa = 1304
    rw = w_in[:, nsa:]
    parts = [w_in[:, :1280], w_in[:, 1280:1304], z(104),
             rw[:, :1536], rw[:, 1536:1600], z(64), rw[:, 1600:1664], z(64), rw[:, 1664:1792]]
    return jnp.concatenate(parts, axis=1).astype(BF16)


def _pack_rw_vec(v):
    z = jnp.zeros((64,), v.dtype)
    return jnp.concatenate([v[:1536], v[1536:1600], z, v[1600:1664], z, v[1664:1792]])[None, :]


def _mm_body(x_ref, w_ref, o_ref):
    o_ref[...] = _mm(x_ref[...], w_ref[...])


def small_matmul(x, w):
    return pl.pallas_call(
        _mm_body,
        out_shape=jax.ShapeDtypeStruct((x.shape[0], w.shape[1]), F32),
        compiler_params=pltpu.CompilerParams(vmem_limit_bytes=VMEM_LIMIT),
        name="small_matmul",
    )(x, w)


def _compress_body(x_ref, wts_ref, wc_ref, o_ref):
    x = x_ref[...]
    nb = x.shape[0] // CMP_BLK
    pooled = jnp.sum(x.reshape(nb, CMP_BLK, x.shape[-1]) * wts_ref[...][None], axis=1)
    o_ref[...] = _mm(pooled, wc_ref[...])


def _compress_paged_body(pt_ref, *refs, pps):
    page_refs = refs[:pps]
    wp_ref, wc_ref, o_ref = refs[pps:]
    x = jnp.concatenate([r[0] for r in page_refs], axis=1)
    pooled_t = jnp.concatenate([_mm(x[0:128], wp_ref[0]), _mm(x[128:256], wp_ref[1])], axis=0)
    nb = o_ref.shape[1]
    o_ref[0] = _mm(pooled_t.T[:nb], wc_ref[...])


def _cmp_weights(pos_wts, w_c):
    wts = jnp.repeat(pos_wts.T, 128, axis=1)
    eye2 = jnp.eye(2, dtype=w_c.dtype)
    blocks = [jnp.kron(eye2, w_c[c]) for c in range(2)]
    z = jnp.zeros((128, 128), w_c.dtype)
    wc = jnp.concatenate([jnp.concatenate([blocks[0], z], axis=1),
                          jnp.concatenate([z, blocks[1]], axis=1)], axis=0)
    return wts, wc


def compress_prompt(kv, wts, wc, tr):
    t = kv.shape[0]
    nb = tr // CMP_BLK
    return pl.pallas_call(
        _compress_body,
        grid=(t // tr,),
        in_specs=[pl.BlockSpec((tr, 256), lambda i: (i, 0)),
                  pl.BlockSpec((CMP_BLK, 256), lambda i: (0, 0)),
                  pl.BlockSpec((256, 256), lambda i: (0, 0))],
        out_specs=pl.BlockSpec((nb, 256), lambda i: (i, 0)),
        out_shape=jax.ShapeDtypeStruct((t // CMP_BLK, 256), F32),
        compiler_params=_cparams(("arbitrary",)),
        name="compress_prompt",
    )(kv, wts, wc)


def compress_paged(pool_t, page_table, pos_wts, wc, pages_per_step):
    b, n_pages = page_table.shape
    page = pool_t.shape[2]
    pps = pages_per_step
    nb = pps * page // CMP_BLK
    p_idx = jnp.arange(pps * page)
    wp = jax.nn.one_hot(p_idx // CMP_BLK, LANE, dtype=F32)[None] * pos_wts[:, p_idx % CMP_BLK][:, :, None]

    def page_spec(u):
        return pl.BlockSpec((1, 256, page), lambda bi, g, pt: (pt[bi, g * pps + u], 0, 0))

    grid_spec = pltpu.PrefetchScalarGridSpec(
        num_scalar_prefetch=1,
        grid=(b, n_pages // pps),
        in_specs=[page_spec(u) for u in range(pps)] + [
            pl.BlockSpec((2, pps * page, LANE), lambda bi, g, pt: (0, 0, 0)),
            pl.BlockSpec((256, 256), lambda bi, g, pt: (0, 0))],
        out_specs=pl.BlockSpec((1, nb, 256), lambda bi, g, pt: (bi, g, 0)),
    )
    return pl.pallas_call(
        functools.partial(_compress_paged_body, pps=pps),
        grid_spec=grid_spec,
        out_shape=jax.ShapeDtypeStruct((b, n_pages * page // CMP_BLK, 256), F32),
        compiler_params=_cparams(("arbitrary", "arbitrary")),
        name="compress_paged",
    )(page_table, *([pool_t] * pps), wp, wc)


def _gather_sel_body(pt_ref, tiles_ref, cnt_ref, *refs, pps, n_page_steps, nt):
    del pt_ref
    page_refs = refs[:pps]
    tail_ref, ks_ref, vst_ref = refs[pps:]
    bi = pl.program_id(0)
    a = pl.program_id(1)
    j = tiles_ref[bi * nt + jnp.minimum(a, cnt_ref[bi] - 1)]
    live = a < cnt_ref[bi]

    @pl.when(live & (j < n_page_steps))
    def _():
        ks_ref[0] = jnp.concatenate([r[0][0:128].T for r in page_refs], axis=0).astype(BF16)
        vst_ref[0] = jnp.concatenate([r[0][128:256] for r in page_refs], axis=1).astype(BF16)

    @pl.when(live & (j >= n_page_steps))
    def _():
        x = tail_ref[0]
        ks_ref[0] = x[:, :128].astype(BF16)
        vst_ref[0] = x[:, 128:].T.astype(BF16)


def gather_sel(pool_t, page_table, tail, tk, tiles, cnt, n_slots):
    b, n_pages = page_table.shape
    page = pool_t.shape[2]
    pps = tk // page
    n_page_steps = n_pages // pps
    nt = n_page_steps + 1
    nk = n_slots * tk

    def slot(bi, a, pt, tiles, cnt):
        return jnp.minimum(a, cnt[bi] - 1)

    def page_spec(u):
        def index(bi, a, pt, tiles, cnt):
            j = tiles[bi * nt + slot(bi, a, pt, tiles, cnt)]
            return (pt[bi, jnp.minimum(j * pps + u, n_pages - 1)], 0, 0)
        return pl.BlockSpec((1, 256, page), index)

    grid_spec = pltpu.PrefetchScalarGridSpec(
        num_scalar_prefetch=3,
        grid=(b, jnp.max(cnt)),
        in_specs=[page_spec(u) for u in range(pps)] + [pl.BlockSpec((1, tk, 256), lambda bi, a, *_: (bi, 0, 0))],
        out_specs=[pl.BlockSpec((1, tk, 128), lambda bi, a, *s: (bi, slot(bi, a, *s), 0)),
                   pl.BlockSpec((1, 128, tk), lambda bi, a, *s: (bi, 0, slot(bi, a, *s)))],
    )
    return pl.pallas_call(
        functools.partial(_gather_sel_body, pps=pps, n_page_steps=n_page_steps, nt=nt),
        grid_spec=grid_spec,
        out_shape=[jax.ShapeDtypeStruct((b, nk, 128), BF16), jax.ShapeDtypeStruct((b, 128, nk), BF16)],
        compiler_params=_cparams(("arbitrary", "arbitrary")),
        name="gather_sel",
    )(page_table, tiles, cnt, *([pool_t] * pps), tail)


MASKED = -1e30
M_INIT = -1e29


def _nsa_query(qt_ref, k, tq):
    w4 = NSA_GROUP * tq
    qb = qt_ref[0].astype(F32)
    qcat = jnp.concatenate([qb[g * 64:(g + 1) * 64] for g in range(NSA_GROUP)], axis=1)
    q2 = jnp.concatenate([qcat, qcat], axis=0)
    row = lax.broadcasted_iota(jnp.int32, (128, w4), 0)
    qe = jnp.where(row // 64 == k, q2, 0.0)
    gidx = lax.broadcasted_iota(jnp.int32, (128, w4), 1) // tq
    base = jnp.where(k == 0, 0.5, 0.5 / 16.0)
    slope = base * jnp.where(gidx == 0, 1.0, jnp.where(gidx == 1, 0.5, jnp.where(gidx == 2, 0.25, 0.125)))
    mult = jnp.where(row == 0, 16.0, jnp.where(row == 1, 1.0, jnp.where(row == 2, 128.0,
                                                                         jnp.where(row == 3, 64.0, 0.0))))
    return jnp.concatenate([qe, slope * mult], axis=0).astype(BF16)


def _pos_features(rows, tile_rel):
    r = lax.broadcasted_iota(jnp.int32, (rows, LANE), 0)
    lane = lax.broadcasted_iota(jnp.int32, (rows, LANE), 1)
    ab = jnp.where(lane == 0, r // 16, jnp.where(lane == 1, r % 16, 0)).astype(F32)
    return jnp.where(lane == 2, tile_rel, ab).astype(BF16)


def _gate_rows(gb, j, tq):
    return jnp.concatenate([gb[g * 3 + j:g * 3 + j + 1, :] for g in range(NSA_GROUP)], axis=1)


def _nsa_select_body(qt_ref, g_ref, kvc_ref, kvct_ref, kw_ref, vwt_ref, part_ref, sel_ref, flag_ref, *,
                     tq, tk, wk, nbc, nb, pos0_fn, wstart_fn, wpos0_fn):
    i = pl.program_id(1)
    k = pl.program_id(2)
    w4 = NSA_GROUP * tq
    pos0 = pos0_fn(i)
    qa = _nsa_query(qt_ref, k, tq)
    pos_q = pos0 + lax.broadcasted_iota(jnp.int32, (1, w4), 1) % tq

    def softmax_cols(s, bad):
        s = jnp.where(bad, MASKED, s)
        m = jnp.maximum(jnp.max(s, axis=0, keepdims=True), M_INIT)
        e = jnp.exp(s - m)
        return e / jnp.maximum(jnp.sum(e, axis=0, keepdims=True), 1e-30)

    n_i = lax.broadcasted_iota(jnp.int32, (nbc, LANE), 0)
    lane_c = lax.broadcasted_iota(jnp.int32, (nbc, LANE), 1)
    feat_c = jnp.where(lane_c == 3, n_i - pos0 // CMP_BLK, 0).astype(F32).astype(BF16)
    kc = jnp.concatenate([kvc_ref[0][:, :128].astype(BF16), feat_c], axis=1)
    c_end = lax.broadcasted_iota(jnp.int32, (nbc, 1), 0) * CMP_BLK + (CMP_BLK - 1)
    p_c = softmax_cols(lax.dot_general(kc, qa, NN, preferred_element_type=F32), c_end > pos_q)
    vct = kvct_ref[0, pl.ds(pl.multiple_of(128 + k * 64, 64), 64), :]
    o_c = _mm(vct, p_c)

    imp = p_c[:, 0:tq]
    for g in range(1, NSA_GROUP):
        imp = imp + p_c[:, g * tq:(g + 1) * tq]
    if nb > nbc:
        imp = jnp.concatenate([imp, jnp.zeros((nb - nbc, tq), F32)], axis=0)
    blk = lax.broadcasted_iota(jnp.int32, (nb, tq), 0)
    cur = (pos0 + lax.broadcasted_iota(jnp.int32, (1, tq), 1)) // SEL_BLK
    forced = (blk == cur) | (blk == cur - 1) | (blk == 0)
    score = jnp.where(blk <= cur, imp + jnp.where(forced, FORCE_BONUS, 0.0), -1.0)
    for _ in range(min(TOPK_BLK, nb)):
        m = jnp.max(score, axis=0, keepdims=True)
        first = jnp.min(jnp.where(score == m, blk, nb), axis=0, keepdims=True)
        score = jnp.where(blk == first, -2.0, score)
    sel = jnp.where(score == -2.0, 1.0, 0.0)
    sel_ref[0, 0] = sel
    bpt = tk // SEL_BLK
    any_row = jnp.max(sel, axis=1, keepdims=True)
    flag_ref[0, 0] = jnp.max(any_row.reshape(nb // bpt, bpt, 1), axis=1)

    wstart = wstart_fn(i)
    if not isinstance(wstart, int):
        wstart = pl.multiple_of(wstart, 128)
    wpos0 = wpos0_fn(i)
    tile_rel = jnp.asarray((wpos0 - pos0) // 128, F32)
    kw = jnp.concatenate([kw_ref[0, pl.ds(wstart, wk), :], _pos_features(wk, tile_rel)], axis=1)
    dist_w = pos_q - (wpos0 + lax.broadcasted_iota(jnp.int32, (wk, 1), 0))
    p_w = softmax_cols(lax.dot_general(kw, qa, NN, preferred_element_type=F32), (dist_w < 0) | (dist_w >= WINDOW))
    vwin = vwt_ref[0, pl.ds(pl.multiple_of(k * 64, 64), 64), pl.ds(wstart, wk)]
    o_w = _mm(vwin, p_w)

    gb = g_ref[0, 0]
    part_ref[0, 0] = _gate_rows(gb, 0, tq) * o_c + _gate_rows(gb, 2, tq) * o_w


def nsa_select(qt, gates, kvc, kvct, kw, vwt, *, nb, tq, tk, wk, pos0_fn, wstart_fn, wpos0_fn):
    b, _, nq = qt.shape
    nbc = kvc.shape[1]
    nw = kw.shape[1]
    nqt = nq // tq
    nt = nb * SEL_BLK // tk
    w4 = NSA_GROUP * tq
    assert nbc <= 256 and tk <= 512 and wk <= 1024
    body = functools.partial(_nsa_select_body, tq=tq, tk=tk, wk=wk, nbc=nbc, nb=nb, pos0_fn=pos0_fn,
                             wstart_fn=wstart_fn, wpos0_fn=wpos0_fn)
    full = lambda s1, s2: pl.BlockSpec((1, s1, s2), lambda bi, i, k: (bi, 0, 0))
    step = lambda s1, s2: pl.BlockSpec((1, 1, s1, s2), lambda bi, i, k: (bi, i * NSA_KV_HEADS + k, 0, 0))
    return pl.pallas_call(
        body,
        grid=(b, nqt, NSA_KV_HEADS),
        in_specs=[pl.BlockSpec((1, 256, tq), lambda bi, i, k: (bi, k, i)),
                  pl.BlockSpec((1, 1, 16, tq), lambda bi, i, k: (bi, k, 0, i)),
                  full(nbc, 256), full(256, nbc), full(nw, 128), full(128, nw)],
        out_specs=[step(64, w4), step(nb, tq), step(nt, 1)],
        out_shape=[jax.ShapeDtypeStruct((b, nqt * 2, 64, w4), F32),
                   jax.ShapeDtypeStruct((b, nqt * 2, nb, tq), F32),
                   jax.ShapeDtypeStruct((b, nqt * 2, nt, 1), F32)],
        compiler_params=_cparams(("arbitrary", "arbitrary", "arbitrary")),
        name="nsa_select",
    )(qt, gates, kvc, kvct, kw, vwt)


def _nsa_selected_body(list_ref, slot_ref, cnt_ref, qt_ref, g_ref, sel_ref, ks_ref, vst_ref, part_ref, o_ref, *,
                       tq, tk, nt, pos0_fn):
    bi = pl.program_id(0)
    i = pl.program_id(1)
    k = pl.program_id(2)
    step = (bi * pl.num_programs(1) + i) * NSA_KV_HEADS + k
    w4 = NSA_GROUP * tq
    pos0 = pos0_fn(i)
    qa = _nsa_query(qt_ref, k, tq)
    pos_q = pos0 + lax.broadcasted_iota(jnp.int32, (1, w4), 1) % tq
    bpt = tk // SEL_BLK
    row_k = lax.broadcasted_iota(jnp.int32, (tk, 1), 0)
    r = lax.broadcasted_iota(jnp.int32, (tk, LANE), 0)
    lane = lax.broadcasted_iota(jnp.int32, (tk, LANE), 1)
    feat_ab = jnp.where(lane == 0, r // 16, jnp.where(lane == 1, r % 16, 0)).astype(F32)

    n_act = cnt_ref[step]

    def tile_scores(jj, live):
        j = list_ref[step * nt + jj]
        off = pl.multiple_of(j * tk, tk)
        buf = pl.multiple_of(slot_ref[step * nt + jj] * tk, tk)
        tile_rel = ((off - pos0) // 128).astype(F32)
        feat = jnp.where(lane == 2, tile_rel, feat_ab).astype(BF16)
        kj = jnp.concatenate([ks_ref[0, pl.ds(buf, tk), :], feat], axis=1)
        s = lax.dot_general(kj, qa, NN, preferred_element_type=F32)
        selb = (sel_ref[0, 0, pl.ds(pl.multiple_of(j * bpt, bpt), bpt), :] - 1.0) * (-MASKED)
        selb = jnp.concatenate([selb] * NSA_GROUP, axis=1) + jnp.where(live, 0.0, MASKED)
        s = s + jnp.broadcast_to(selb[:, None, :], (bpt, SEL_BLK, w4)).reshape(tk, w4)
        s = jnp.where(row_k > pos_q - off, MASKED, s)
        return s, vst_ref[0, pl.ds(pl.multiple_of(k * 64, 64), 64), pl.ds(buf, tk)]

    def kv_pair(pp, carry):
        m_i, l_i, acc = carry
        second = 2 * pp + 1
        s_a, v_a = tile_scores(2 * pp, True)
        s_b, v_b = tile_scores(jnp.minimum(second, n_act - 1), second < n_act)
        m_new = jnp.maximum(m_i, jnp.maximum(jnp.max(s_a, axis=0, keepdims=True), jnp.max(s_b, axis=0, keepdims=True)))
        p_a = jnp.exp(s_a - m_new)
        p_b = jnp.exp(s_b - m_new)
        alpha = jnp.exp(m_i - m_new)
        l_new = l_i * alpha + (jnp.sum(p_a, axis=0, keepdims=True) + jnp.sum(p_b, axis=0, keepdims=True))
        return m_new, l_new, acc * alpha + (_mm(v_a, p_a) + _mm(v_b, p_b))

    init = (jnp.full((1, w4), M_INIT, F32), jnp.zeros((1, w4), F32), jnp.zeros((64, w4), F32))
    _, l_s, acc_s = lax.fori_loop(0, (n_act + 1) // 2, kv_pair, init)
    o_s = acc_s / jnp.maximum(l_s, 1e-30)
    o_t = part_ref[0, 0] + _gate_rows(g_ref[0, 0], 1, tq) * o_s
    o_ref[0] = jnp.concatenate([o_t[:, g * tq:(g + 1) * tq].T for g in range(NSA_GROUP)], axis=1)


def nsa_selected(tile_list, slot_list, tile_cnt, qt, gates, sel, ks, vst, part, *, tq, tk, pos0_fn):
    b, _, nq = qt.shape
    nk = ks.shape[1]
    nb = sel.shape[2]
    nt = nb * SEL_BLK // tk
    w4 = NSA_GROUP * tq
    full = lambda s1, s2: pl.BlockSpec((1, s1, s2), lambda bi, i, k, *_: (bi, 0, 0))
    step = lambda s1, s2: pl.BlockSpec((1, 1, s1, s2), lambda bi, i, k, *_: (bi, i * NSA_KV_HEADS + k, 0, 0))
    grid_spec = pltpu.PrefetchScalarGridSpec(
        num_scalar_prefetch=3,
        grid=(b, nq // tq, NSA_KV_HEADS),
        in_specs=[pl.BlockSpec((1, 256, tq), lambda bi, i, k, *_: (bi, k, i)),
                  pl.BlockSpec((1, 1, 16, tq), lambda bi, i, k, *_: (bi, k, 0, i)),
                  step(nb, tq), full(nk, 128), full(128, nk), step(64, w4)],
        out_specs=pl.BlockSpec((1, tq, 256), lambda bi, i, k, *_: (bi, i, k)),
    )
    return pl.pallas_call(
        functools.partial(_nsa_selected_body, tq=tq, tk=tk, nt=nt, pos0_fn=pos0_fn),
        grid_spec=grid_spec,
        out_shape=jax.ShapeDtypeStruct((b, nq, 512), F32),
        compiler_params=_cparams(("arbitrary", "arbitrary", "arbitrary")),
        name="nsa_selected",
    )(tile_list, slot_list, tile_cnt, qt, gates, sel, ks, vst, part)


def _active_first(active):
    order = jnp.argsort(jnp.where(active, 0, 1), axis=-1, stable=True).astype(jnp.int32)
    return order, jnp.sum(active, axis=-1).astype(jnp.int32)


def nsa_attention(qt, gates, kvc, kvct, ks, vst, kw, vwt, *, tq, tk, wk, pos0_fn, wstart_fn, wpos0_fn):
    nb = ks.shape[1] // SEL_BLK
    part, sel, flags = nsa_select(qt, gates, kvc, kvct, kw, vwt, nb=nb, tq=tq, tk=tk, wk=wk, pos0_fn=pos0_fn,
                                  wstart_fn=wstart_fn, wpos0_fn=wpos0_fn)
    order, cnt = _active_first(flags[..., 0] > 0.5)
    return nsa_selected(order.reshape(-1), order.reshape(-1), cnt.reshape(-1), qt, gates, sel, ks, vst, part,
                        tq=tq, tk=tk, pos0_fn=pos0_fn)


def nsa_attention_paged(qt, gates, kvc, kvct, pool_t, page_table, tail, kw, vwt, *, tq, tk, wk, pos0_fn, wstart_fn,
                        wpos0_fn):
    nb = (page_table.shape[1] * pool_t.shape[2] + tk) // SEL_BLK
    part, sel, flags = nsa_select(qt, gates, kvc, kvct, kw, vwt, nb=nb, tq=tq, tk=tk, wk=wk, pos0_fn=pos0_fn,
                                  wstart_fn=wstart_fn, wpos0_fn=wpos0_fn)
    active = flags[..., 0] > 0.5
    tiles_b, cnt_b = _active_first(jnp.any(active, axis=1))
    slot_of_tile = jnp.argsort(tiles_b, axis=-1).astype(jnp.int32)
    order, cnt = _active_first(active)
    slots = jnp.take_along_axis(jnp.broadcast_to(slot_of_tile[:, None, :], order.shape), order, axis=-1)

    def run(n_slots):
        ks, vst = gather_sel(pool_t, page_table, tail, tk, tiles_b.reshape(-1), cnt_b, n_slots)
        return nsa_selected(order.reshape(-1), slots.reshape(-1), cnt.reshape(-1), qt, gates, sel, ks, vst, part,
                            tq=tq, tk=tk, pos0_fn=pos0_fn)

    nt = tiles_b.shape[-1]
    few = min(SAMPLE_TILE_SLOTS, nt)
    return lax.cond(jnp.max(cnt_b) <= few, lambda: run(few), lambda: run(nt))


def _tri_inverse(ms, c):
    eye = (lax.broadcasted_iota(jnp.int32, (c, c), 0) == lax.broadcasted_iota(jnp.int32, (c, c), 1)).astype(F32)
    ps = [-m for m in ms]
    ts = [eye + p for p in ps]
    steps = max(int(math.ceil(math.log2(c))) - 1, 0)
    d = lambda x, y: lax.dot_general(x, y, NN, preferred_element_type=F32)
    for _ in range(steps):
        sp = [_split(p) for p in ps]
        ps = [d(ph, ph) + (d(ph, pl_) + d(pl_, ph)) for ph, pl_ in sp]
        sp = [_split(p) for p in ps]
        st = [_split(t) for t in ts]
        ts = [t + (d(th, ph) + (d(th, pl_) + d(tl, ph))) for t, (th, tl), (ph, pl_) in zip(ts, st, sp)]
    return ts


def _rwkv_body(rw_ref, rw0_ref, s0_ref, mu_ref, vec_ref, w2_ref, a2_ref, g2_ref, seg_ref, rk_ref,
               o_ref, sfin_ref, buf_ref, s_ref, y_ref, *, c, valid, n_chunks):
    ci = pl.program_id(1)
    halo = 8

    @pl.when(ci == 0)
    def _():
        buf_ref[...] = rw0_ref[0]
        s_ref[...] = s0_ref[0]

    cur = rw_ref[...]
    prev = _shift_rows(cur, buf_ref[...], 1)
    xr = cur + (prev - cur) * mu_ref[...]
    buf_ref[...] = cur[c - halo:, :]

    vec = vec_ref[...]
    w0, a0, kkw, kaw, ln_w, ln_b = (vec[r:r + 1, :] for r in range(6))
    r = xr[:, 0:512]
    kx = xr[:, 512:1024]
    v = xr[:, 1024:1536]
    xw = xr[:, 1536:1664]
    xa = xr[:, 1664:1792]
    xg = xr[:, 1792:1920]
    wl = -jnp.exp(-_softplus(-(w0 + _mm(jnp.tanh(xw), w2_ref[...]))) - 0.5)
    a = _sigmoid(a0 + _mm(xa, a2_ref[...]))
    gate = _mm(_sigmoid(xg), g2_ref[...])
    seg = seg_ref[...]
    zk = kx * kkw
    k2 = kx * (1.0 + (a - 1.0) * kaw)
    zz_sum, rk_sum = _head_sums([zk * zk, r * k2 * rk_ref[...]], seg)
    kk = zk * lax.rsqrt(zz_sum + EPS)
    bonus = rk_sum * v
    if valid < c:
        live = lax.broadcasted_iota(jnp.int32, (c, 1), 0) < valid
        wl = jnp.where(live, wl, 0.0)
        kk = jnp.where(live, kk, 0.0)
        k2 = jnp.where(live, k2, 0.0)
        v = jnp.where(live, v, 0.0)
        r = jnp.where(live, r, 0.0)
    bb = kk * a

    ri = lax.broadcasted_iota(jnp.int32, (c, c), 0)
    cj = lax.broadcasted_iota(jnp.int32, (c, c), 1)
    tril = ri >= cj
    strict = ri > cj
    cw = _cumsum_rows(wl)
    ecw = jnp.exp(cw)
    einv = jnp.exp(-cw)
    p_c = ecw[c - 1:c, :]
    kt = kk * jnp.exp(cw - wl)
    bt = bb * einv
    ki = k2 * einv
    rt = r * ecw
    bd = bt * p_c
    kd = ki * p_c

    heads = range(RWKV_HEADS)
    sls = [slice(h * RWKV_HD, (h + 1) * RWKV_HD) for h in heads]
    kt_h = [kt[:, sl] for sl in sls]
    bt_h = [bt[:, sl] for sl in sls]
    ki_h = [ki[:, sl] for sl in sls]
    rt_h = [rt[:, sl] for sl in sls]
    v_h = [v[:, sl] for sl in sls]
    l_m = [jnp.where(strict, _mm3(kt_h[h], bt_h[h], NT), 0.0) for h in heads]
    m_kk = [jnp.where(strict, _mm(kt_h[h], ki_h[h], NT), 0.0) for h in heads]
    a_rb = [jnp.where(tril, _mm(rt_h[h], bt_h[h], NT), 0.0) for h in heads]
    a_rk = [jnp.where(tril, _mm(rt_h[h], ki_h[h], NT), 0.0) for h in heads]
    mv = [_mm(m_kk[h], v_h[h]) for h in heads]
    y0 = [_mm(a_rk[h], v_h[h]) for h in heads]
    t_inv = _tri_inverse(l_m, c)
    w_h = [_mm3(t_inv[h], kt_h[h]) for h in heads]
    u_h = [-_mm3(t_inv[h], mv[h]) for h in heads]
    s_h = [s_ref[h] for h in heads]
    e_h = [u_h[h] - _mm(w_h[h], s_h[h], NT) for h in heads]
    y1 = [_mm(rt_h[h], s_h[h], NT) + y0[h] for h in heads]
    y_h = [y1[h] + _mm(a_rb[h], e_h[h]) for h in heads]
    ds = [_mm(e_h[h], bd[:, sls[h]], TN) + _mm(v_h[h], kd[:, sls[h]], TN) for h in heads]
    for h in heads:
        s_ref[h] = s_h[h] * p_c[:, sls[h]] + ds[h]
        mu_h = jnp.mean(y_h[h], axis=-1, keepdims=True)
        d_h = y_h[h] - mu_h
        var_h = jnp.mean(d_h * d_h, axis=-1, keepdims=True)
        y_ref[:, sls[h]] = d_h * lax.rsqrt(var_h + RWKV_GN_EPS)

    o_ref[...] = (y_ref[...] * ln_w + ln_b + bonus) * gate

    @pl.when(ci == n_chunks - 1)
    def _():
        sfin_ref[0] = s_ref[...]


def rwkv_mix(rw, rw0, s0, mu, vec, w2, a2, g2, seg, rk, *, c, valid):
    b = s0.shape[0]
    rows = rw.shape[0]
    n_chunks = rows // (b * c)
    const = lambda s: pl.BlockSpec(s, lambda bi, ci: tuple(0 for _ in s))
    return pl.pallas_call(
        functools.partial(_rwkv_body, c=c, valid=valid, n_chunks=n_chunks),
        grid=(b, n_chunks),
        in_specs=[pl.BlockSpec((c, RW_COLS), lambda bi, ci: (bi * n_chunks + ci, 0)),
                  pl.BlockSpec((1, 8, RW_COLS), lambda bi, ci: (bi, 0, 0)),
                  pl.BlockSpec((1, RWKV_HEADS, 64, 64), lambda bi, ci: (bi, 0, 0, 0)),
                  const((1, RW_COLS)), const((8, 512)), const((128, 512)), const((128, 512)), const((128, 512)),
                  const((RWKV_W // 2, RWKV_W // 2)), const((1, 512))],
        out_specs=[pl.BlockSpec((c, 512), lambda bi, ci: (bi * n_chunks + ci, 0)),
                   pl.BlockSpec((1, RWKV_HEADS, 64, 64), lambda bi, ci: (bi, 0, 0, 0))],
        out_shape=[jax.ShapeDtypeStruct((rows, 512), F32),
                   jax.ShapeDtypeStruct((b, RWKV_HEADS, 64, 64), F32)],
        scratch_shapes=[pltpu.VMEM((8, RW_COLS), F32), pltpu.VMEM((RWKV_HEADS, 64, 64), F32),
                        pltpu.VMEM((c, 512), F32)],
        compiler_params=_cparams(("arbitrary", "arbitrary")),
        name="rwkv_mix",
    )(rw, rw0, s0, mu, vec, w2, a2, g2, seg, rk)


def _out_proj_body(*refs, n_in):
    a_refs = refs[:n_in]
    w_refs = refs[n_in:2 * n_in]
    x_ref, g_ref, o_ref = refs[2 * n_in:]
    y = _mm(a_refs[0][...], w_refs[0][...])
    for a_ref, w_ref in zip(a_refs[1:], w_refs[1:]):
        y = y + _mm(a_ref[...], w_ref[...])
    o_ref[...] = x_ref[...] + g_ref[...] * y


def out_proj(acts, weights, x, gate, tm):
    n, d = x.shape
    n_in = len(acts)
    return pl.pallas_call(
        functools.partial(_out_proj_body, n_in=n_in),
        grid=(n // tm,),
        in_specs=[pl.BlockSpec((tm, a.shape[1]), lambda i: (i, 0)) for a in acts]
        + [pl.BlockSpec(w.shape, lambda i: (0, 0)) for w in weights]
        + [pl.BlockSpec((tm, d), lambda i: (i, 0)), _mod_spec(gate.shape[0], tm, d)],
        out_specs=pl.BlockSpec((tm, d), lambda i: (i, 0)),
        out_shape=jax.ShapeDtypeStruct((n, d), F32),
        compiler_params=_cparams(("arbitrary",)),
        name="out_proj",
    )(*acts, *weights, x, gate)


def _odd_proj_body(x_ref, nw_ref, sc_ref, sh_ref, w_ref, qkv_ref, z_ref, ba_ref):
    hb = _norm_mod(x_ref[...], nw_ref[...], sc_ref[...], sh_ref[...]).astype(BF16)
    qkv_ref[...] = _mm(hb, w_ref[:, 0:3072])
    z_ref[...] = _mm(hb, w_ref[:, 3072:4096])
    ba_ref[...] = _mm(hb, w_ref[:, 4096:O_COLS])


def odd_proj(x, nw, sc, sh, w_packed, tm):
    n, d = x.shape
    rows_mod = sc.shape[0]
    row = lambda c: pl.BlockSpec((tm, c), lambda i: (i, 0))
    return pl.pallas_call(
        _odd_proj_body,
        grid=(n // tm,),
        in_specs=[row(d), pl.BlockSpec((1, d), lambda i: (0, 0)),
                  _mod_spec(rows_mod, tm, d), _mod_spec(rows_mod, tm, d),
                  pl.BlockSpec((d, O_COLS), lambda i: (0, 0))],
        out_specs=[row(3072), row(1024), row(128)],
        out_shape=[jax.ShapeDtypeStruct((n, 3072), F32), jax.ShapeDtypeStruct((n, 1024), F32),
                   jax.ShapeDtypeStruct((n, 128), F32)],
        compiler_params=_cparams(("arbitrary",)),
        name="odd_proj",
    )(x, nw, sc, sh, w_packed)


def _gdn_body(qkv_ref, z_ref, ba_ref, cs_ref, s0_ref, cw_ref, hp_ref, nw_ref,
              o_ref, sfin_ref, buf_ref, s_ref, *, c, valid, n_chunks):
    ci = pl.program_id(1)
    halo = 8

    @pl.when(ci == 0)
    def _():
        buf_ref[...] = cs_ref[0]
        s_ref[...] = s0_ref[0]

    x = qkv_ref[...]
    prev8 = buf_ref[...]
    cw = cw_ref[...]
    conv = x * cw[CONV_W - 1:CONV_W, :]
    for j in range(CONV_W - 1):
        conv = conv + _shift_rows(x, prev8, CONV_W - 1 - j) * cw[j:j + 1, :]
    buf_ref[...] = x[c - halo:, :]
    conv = _silu(conv)

    hp = hp_ref[...]
    ba = ba_ref[...]
    beta_f = _sigmoid(ba)
    g_f = hp[0:1, :] * _softplus(ba + hp[1:2, :])
    if valid < c:
        live = lax.broadcasted_iota(jnp.int32, (c, 1), 0) < valid
        beta_f = jnp.where(live, beta_f, 0.0)
        g_f = jnp.where(live, g_f, 0.0)
        conv = jnp.where(live, conv, 0.0)

    ri = lax.broadcasted_iota(jnp.int32, (c, c), 0)
    cj = lax.broadcasted_iota(jnp.int32, (c, c), 1)
    tril = ri >= cj
    strict = ri > cj
    gc = _cumsum_rows(g_f)
    gct = gc.T
    z = z_ref[...]
    nw = nw_ref[...]

    heads = range(GDN_HEADS)
    sls = [slice(h * GDN_HD, (h + 1) * GDN_HD) for h in heads]
    q_h = [conv[:, sl] for sl in sls]
    k_h = [conv[:, GDN_W + h * GDN_HD:GDN_W + (h + 1) * GDN_HD] for h in heads]
    v_h = [conv[:, 2 * GDN_W + h * GDN_HD:2 * GDN_W + (h + 1) * GDN_HD] for h in heads]
    q_h = [q * lax.rsqrt(jnp.sum(q * q, axis=-1, keepdims=True) + EPS) * (GDN_HD ** -0.5) for q in q_h]
    k_h = [k * lax.rsqrt(jnp.sum(k * k, axis=-1, keepdims=True) + EPS) for k in k_h]
    g_col = [gc[:, 8 + h:9 + h] for h in heads]
    eg = [jnp.exp(g) for g in g_col]
    b_col = [beta_f[:, h:h + 1] for h in heads]
    decay = [jnp.where(tril, jnp.exp(jnp.where(tril, g_col[h] - gct[8 + h:9 + h, :], 0.0)), 0.0) for h in heads]
    kb = [k_h[h] * b_col[h] for h in heads]
    vb = [v_h[h] * b_col[h] for h in heads]
    m_h = [jnp.where(strict, _mm3(kb[h], k_h[h], NT) * decay[h], 0.0) for h in heads]
    qk = [jnp.where(tril, _mm(q_h[h], k_h[h], NT) * decay[h], 0.0) for h in heads]
    t_inv = _tri_inverse(m_h, c)
    u_h = [_mm(t_inv[h], vb[h]) for h in heads]
    w_h = [_mm(t_inv[h], kb[h] * eg[h]) for h in heads]
    s_h = [s_ref[h] for h in heads]
    v_new = [u_h[h] - _mm(w_h[h], s_h[h]) for h in heads]
    o1 = [_mm(q_h[h] * eg[h], s_h[h]) for h in heads]
    o_h = [o1[h] + _mm(qk[h], v_new[h]) for h in heads]
    g_last = [g[c - 1:c, :] for g in g_col]
    ds = [_mm(k_h[h] * jnp.exp(g_last[h] - g_col[h]), v_new[h], TN) for h in heads]
    for h in heads:
        s_ref[h] = s_h[h] * jnp.exp(g_last[h]) + ds[h]
        o_n = o_h[h] * lax.rsqrt(jnp.mean(o_h[h] * o_h[h], axis=-1, keepdims=True) + EPS) * nw
        o_ref[:, sls[h]] = o_n * _silu(z[:, sls[h]])

    @pl.when(ci == n_chunks - 1)
    def _():
        sfin_ref[0] = s_ref[...]


def gdn_mix(qkv, z, ba, cs, s0, conv_w8, hp, nw, *, c, valid):
    b = s0.shape[0]
    rows = qkv.shape[0]
    n_chunks = rows // (b * c)
    const = lambda s: pl.BlockSpec(s, lambda bi, ci: tuple(0 for _ in s))
    row = lambda w: pl.BlockSpec((c, w), lambda bi, ci: (bi * n_chunks + ci, 0))
    return pl.pallas_call(
        functools.partial(_gdn_body, c=c, valid=valid, n_chunks=n_chunks),
        grid=(b, n_chunks),
        in_specs=[row(3072), row(1024), row(128),
                  pl.BlockSpec((1, 8, 3072), lambda bi, ci: (bi, 0, 0)),
                  pl.BlockSpec((1, GDN_HEADS, 128, 128), lambda bi, ci: (bi, 0, 0, 0)),
                  const((8, 3072)), const((8, 128)), const((1, 128))],
        out_specs=[row(1024), pl.BlockSpec((1, GDN_HEADS, 128, 128), lambda bi, ci: (bi, 0, 0, 0))],
        out_shape=[jax.ShapeDtypeStruct((rows, 1024), F32),
                   jax.ShapeDtypeStruct((b, GDN_HEADS, 128, 128), F32)],
        scratch_shapes=[pltpu.VMEM((8, 3072), F32), pltpu.VMEM((GDN_HEADS, 128, 128), F32)],
        compiler_params=_cparams(("arbitrary", "arbitrary")),
        name="gdn_mix",
    )(qkv, z, ba, cs, s0, conv_w8, hp, nw)


def _router_body(x_ref, nw_ref, sc_ref, sh_ref, wr_ref, br_ref, h_ref, gate_ref):
    h = _norm_mod(x_ref[...], nw_ref[...], sc_ref[...], sh_ref[...])
    h_ref[...] = h.astype(BF16)
    logits = _mm3(h, wr_ref[...]) + br_ref[...]
    tm = logits.shape[0]
    lane = lax.broadcasted_iota(jnp.int32, (tm, LANE), 1)
    is_grp = (lane >= N_EXPERTS) & (lane < N_EXPERTS + N_GROUPS)
    gl = jnp.where(is_grp, logits, NEG)
    gmax = jnp.max(gl, axis=-1, keepdims=True)
    g_idx = jnp.min(jnp.where(gl == gmax, lane, 4 * LANE), axis=-1, keepdims=True) - N_EXPERTS
    g_w = 1.0 / jnp.sum(jnp.where(is_grp, jnp.exp(gl - gmax), 0.0), axis=-1, keepdims=True)
    in_grp = (lane < N_EXPERTS) & (lane // EXP_PER_GROUP == g_idx)
    el = jnp.where(in_grp, logits, NEG)
    emax = jnp.max(el, axis=-1, keepdims=True)
    e = jnp.where(in_grp, jnp.exp(el - emax), 0.0)
    p = e / jnp.sum(e, axis=-1, keepdims=True)
    p1 = jnp.where(in_grp, p, -1.0)
    m1 = jnp.max(p1, axis=-1, keepdims=True)
    i1 = jnp.min(jnp.where(p1 == m1, lane, 4 * LANE), axis=-1, keepdims=True)
    p2 = jnp.where(lane == i1, -1.0, p1)
    m2 = jnp.max(p2, axis=-1, keepdims=True)
    i2 = jnp.min(jnp.where(p2 == m2, lane, 4 * LANE), axis=-1, keepdims=True)
    tot = m1 + m2
    gate = jnp.where(lane == i1, m1 / tot * g_w, jnp.where(lane == i2, m2 / tot * g_w, 0.0))
    gate_ref[...] = jnp.where(lane == GRP_LANE, g_idx.astype(F32), gate)


def moe_router(x, nw, sc, sh, w_r, b_r, tm):
    n, d = x.shape
    rows_mod = sc.shape[0]
    return pl.pallas_call(
        _router_body,
        grid=(n // tm,),
        in_specs=[pl.BlockSpec((tm, d), lambda i: (i, 0)), pl.BlockSpec((1, d), lambda i: (0, 0)),
                  _mod_spec(rows_mod, tm, d), _mod_spec(rows_mod, tm, d),
                  pl.BlockSpec((d, LANE), lambda i: (0, 0)), pl.BlockSpec((1, LANE), lambda i: (0, 0))],
        out_specs=[pl.BlockSpec((tm, d), lambda i: (i, 0)), pl.BlockSpec((tm, LANE), lambda i: (i, 0))],
        out_shape=[jax.ShapeDtypeStruct((n, d), BF16), jax.ShapeDtypeStruct((n, LANE), F32)],
        compiler_params=_cparams(("arbitrary",)),
        name="moe_router",
    )(x, nw, sc, sh, w_r, b_r)


def _moe_body(h_ref, gate_ref, w1_ref, w3_ref, w2_ref, x_ref, g2_ref, o_ref, acc_ref):
    e = pl.program_id(1)

    @pl.when(e == 0)
    def _():
        acc_ref[...] = jnp.zeros_like(acc_ref)

    hb = h_ref[...]
    he = _silu(_mm(hb, w1_ref[0])) * _mm(hb, w3_ref[0])
    y = _mm(he, w2_ref[0])
    gate = gate_ref[...]
    lane = lax.broadcasted_iota(jnp.int32, gate.shape, 1)
    ge = jnp.sum(jnp.where(lane == e, gate, 0.0), axis=-1, keepdims=True)
    acc_ref[...] += ge * y

    @pl.when(e == pl.num_programs(1) - 1)
    def _():
        o_ref[...] = x_ref[...] + g2_ref[...] * acc_ref[...]


def moe_ffn(h, gate, w1, w3, w2, e0, x, g2, tm):
    n, d = x.shape
    de = w1.shape[2]
    return pl.pallas_call(
        _moe_body,
        grid=(n // tm, N_EXPERTS),
        in_specs=[pl.BlockSpec((tm, d), lambda i, e: (i, 0)), pl.BlockSpec((tm, LANE), lambda i, e: (i, 0)),
                  pl.BlockSpec((1, d, de), lambda i, e: (e0 + e, 0, 0)),
                  pl.BlockSpec((1, d, de), lambda i, e: (e0 + e, 0, 0)),
                  pl.BlockSpec((1, de, d), lambda i, e: (e0 + e, 0, 0)),
                  pl.BlockSpec((tm, d), lambda i, e: (i, 0)),
                  pl.BlockSpec((1, d), lambda i, e: (0, 0)) if g2.shape[0] == 1
                  else pl.BlockSpec((tm, d), lambda i, e: (i, 0))],
        out_specs=pl.BlockSpec((tm, d), lambda i, e: (i, 0)),
        out_shape=jax.ShapeDtypeStruct((n, d), F32),
        scratch_shapes=[pltpu.VMEM((tm, d), F32)],
        compiler_params=_cparams(("arbitrary", "arbitrary")),
        name="moe_ffn",
    )(h, gate, w1, w3, w2, x, g2)


def _moe_plan(grp, tm, tw, tb, cap, max_entries):
    nt = grp.shape[0] // tm
    cnt = jax.nn.one_hot(grp, N_GROUPS, dtype=jnp.int32).reshape(nt, tm, N_GROUPS).sum(axis=1)
    pc = (cnt + ROW_ALIGN - 1) // ROW_ALIGN * ROW_ALIGN
    segb = jnp.cumsum(pc, axis=1) - pc
    off = jnp.cumsum(pc, axis=0) - pc
    tot = pc.sum(axis=0)
    n_real = (tot + tb - 1) // tb
    n_all = jnp.minimum((tot + tw + tb - 1) // tb, cap // tb)
    ends = jnp.cumsum(n_all)
    s = jnp.arange(max_entries)
    g_of = jnp.sum(s[:, None] >= ends[None, :], axis=1)
    active = g_of < N_GROUPS
    g_c = jnp.minimum(g_of, N_GROUPS - 1)
    rt = s - (ends - n_all)[g_c]
    live = tot[g_c] - rt * tb
    real = jnp.where(live <= tb // 4, 3, jnp.where(live <= tb // 2, 5, 1))
    kind = jnp.where(active, jnp.where(rt < n_real[g_c], real, 2), 0)
    last = ends[-1] - 1
    e_grp = jnp.where(active, g_c, g_c[last])
    e_rt = jnp.where(active, rt, rt[last])
    i32 = lambda a: a.reshape(-1).astype(jnp.int32)
    return i32(segb), i32(off // ROW_ALIGN), i32(e_grp), i32(e_rt), i32(kind)


def _group_perm(gate, segb_ref, base, tm, rows):
    gt = gate.T
    grp = gt[GRP_LANE:GRP_LANE + 1, :]
    gi = lax.broadcasted_iota(jnp.int32, (8, tm), 0).astype(F32)
    oh = jnp.where(gi == grp, 1.0, 0.0)
    r_i = lax.broadcasted_iota(jnp.int32, (tm, tm), 0)
    c_i = lax.broadcasted_iota(jnp.int32, (tm, tm), 1)
    before = jnp.where(r_i < c_i, 1.0, 0.0).astype(BF16)
    rank = lax.dot_general(oh.astype(BF16), before, NN, preferred_element_type=F32)
    dest = jnp.zeros((1, tm), F32)
    for g in range(N_GROUPS):
        dest = dest + oh[g:g + 1] * (segb_ref[base + g].astype(F32) + rank[g:g + 1])
    rows_i = lax.broadcasted_iota(jnp.int32, (rows, tm), 0).astype(F32)
    return jnp.where(rows_i == dest, 1.0, 0.0).astype(BF16)


def _moe_dispatch_body(segb_ref, off_ref, h_ref, gate_ref, xg_in, gg_in, xg_ref, gg_ref, xs_ref, gs_ref, *, tm, tw,
                       rows):
    del off_ref, xg_in, gg_in
    i = pl.program_id(0)
    g = pl.program_id(1)

    @pl.when((i == 0) & (g == 0))
    def _():
        xs_ref[...] = jnp.zeros_like(xs_ref)
        gs_ref[...] = jnp.zeros_like(gs_ref)

    @pl.when(g == 0)
    def _():
        gate = gate_ref[...]
        p = _group_perm(gate, segb_ref, i * N_GROUPS, tm, rows)
        xs_ref[0:rows, :] = lax.dot_general(p, h_ref[...], NN, preferred_element_type=F32).astype(BF16)
        gs_ref[0:rows, :] = _mm01(p, gate)

    start = pl.multiple_of(segb_ref[i * N_GROUPS + g], ROW_ALIGN)
    xg_ref[...] = xs_ref[pl.ds(start, tw), :]
    gg_ref[...] = gs_ref[pl.ds(start, tw), :]


def moe_dispatch(h, gate, segb, off, tm, tw, cap):
    n, d = h.shape
    rows = tm + N_GROUPS * ROW_ALIGN
    win = lambda w: pl.BlockSpec((pl.Element(tw), pl.Element(w)),
                                 lambda i, g, segb, off: ((g * (cap // ROW_ALIGN) + off[i * N_GROUPS + g]) * ROW_ALIGN, 0))
    grid_spec = pltpu.PrefetchScalarGridSpec(
        num_scalar_prefetch=2,
        grid=(n // tm, N_GROUPS),
        in_specs=[pl.BlockSpec((tm, d), lambda i, g, *_: (i, 0)), pl.BlockSpec((tm, LANE), lambda i, g, *_: (i, 0)),
                  pl.BlockSpec(memory_space=pl.ANY), pl.BlockSpec(memory_space=pl.ANY)],
        out_specs=[win(d), win(LANE)],
        scratch_shapes=[pltpu.VMEM((rows + tw, d), BF16), pltpu.VMEM((rows + tw, LANE), F32)],
    )
    return pl.pallas_call(
        functools.partial(_moe_dispatch_body, tm=tm, tw=tw, rows=rows),
        grid_spec=grid_spec,
        out_shape=[jax.ShapeDtypeStruct((N_GROUPS * cap, d), BF16), jax.ShapeDtypeStruct((N_GROUPS * cap, LANE), F32)],
        input_output_aliases={4: 0, 5: 1},
        compiler_params=_cparams(("arbitrary", "arbitrary")),
        name="moe_dispatch",
    )(segb, off, h, gate, jnp.zeros((N_GROUPS * cap, d), BF16), jnp.zeros((N_GROUPS * cap, LANE), F32))


def _moe_group_body(grp_ref, rt_ref, kind_ref, xg_ref, gg_ref, w1_ref, w3_ref, w2_ref, yg_ref, acc_ref):
    del rt_ref
    s = pl.program_id(0)
    e = pl.program_id(1)
    kind = kind_ref[s]
    last = e == pl.num_programs(1) - 1

    def run(rows):
        @pl.when(e == 0)
        def _():
            acc_ref[...] = jnp.zeros_like(acc_ref)

        xb = xg_ref[0:rows, :]
        he = _silu(_mm(xb, w1_ref[0])) * _mm(xb, w3_ref[0])
        y = _mm(he, w2_ref[0])
        gate = gg_ref[0:rows, :]
        lane = lax.broadcasted_iota(jnp.int32, gate.shape, 1)
        ge = jnp.sum(jnp.where(lane == grp_ref[s] * EXP_PER_GROUP + e, gate, 0.0), axis=-1, keepdims=True)
        acc_ref[0:rows, :] += ge * y

        @pl.when(last)
        def _():
            yg_ref[...] = acc_ref[...]

    tb = xg_ref.shape[0]
    for code, rows in ((1, tb), (5, tb // 2), (3, tb // 4)):
        pl.when(kind == code)(functools.partial(run, rows))

    @pl.when((kind == 2) & last)
    def _():
        yg_ref[...] = jnp.zeros_like(yg_ref)


def moe_group_ffn(e_grp, e_rt, e_kind, xg, gg, w1, w3, w2, e0, tb, cap):
    d = xg.shape[1]
    de = w1.shape[2]
    row = lambda s, e, grp, rt, kind: (grp[s] * (cap // tb) + rt[s], 0)
    wsel = lambda s, e, grp, rt, kind: (e0 + grp[s] * EXP_PER_GROUP + jnp.where(kind[s] % 2 == 1, e, EXP_PER_GROUP - 1),
                                        0, 0)
    grid_spec = pltpu.PrefetchScalarGridSpec(
        num_scalar_prefetch=3,
        grid=(e_grp.shape[0], EXP_PER_GROUP),
        in_specs=[pl.BlockSpec((tb, d), row), pl.BlockSpec((tb, LANE), row),
                  pl.BlockSpec((1, d, de), wsel), pl.BlockSpec((1, d, de), wsel), pl.BlockSpec((1, de, d), wsel)],
        out_specs=pl.BlockSpec((tb, d), row),
        scratch_shapes=[pltpu.VMEM((tb, d), F32)],
    )
    return pl.pallas_call(
        _moe_group_body,
        grid_spec=grid_spec,
        out_shape=jax.ShapeDtypeStruct((N_GROUPS * cap, d), F32),
        compiler_params=_cparams(("arbitrary", "arbitrary")),
        name="moe_group_ffn",
    )(e_grp, e_rt, e_kind, xg, gg, w1, w3, w2)


def _moe_combine_body(segb_ref, off_ref, yg_ref, gate_ref, x_ref, g2_ref, o_ref, ys_ref, *, tm, tw, rows):
    del off_ref
    i = pl.program_id(0)
    g = pl.program_id(1)

    @pl.when((i == 0) & (g == 0))
    def _():
        ys_ref[...] = jnp.zeros_like(ys_ref)

    start = pl.multiple_of(segb_ref[i * N_GROUPS + g], ROW_ALIGN)
    ys_ref[pl.ds(start, tw), :] = yg_ref[...]

    @pl.when(g == N_GROUPS - 1)
    def _():
        p = _group_perm(gate_ref[...], segb_ref, i * N_GROUPS, tm, rows)
        yh, yl = _split(ys_ref[0:rows, :])
        y = (lax.dot_general(p, yh, TN, preferred_element_type=F32)
             + lax.dot_general(p, yl, TN, preferred_element_type=F32))
        o_ref[...] = x_ref[...] + g2_ref[...] * y


def moe_combine(yg, gate, x, g2, segb, off, tm, tw, cap):
    n, d = x.shape
    rows = tm + N_GROUPS * ROW_ALIGN
    grid_spec = pltpu.PrefetchScalarGridSpec(
        num_scalar_prefetch=2,
        grid=(n // tm, N_GROUPS),
        in_specs=[pl.BlockSpec((pl.Element(tw), pl.Element(d)),
                               lambda i, g, segb, off: ((g * (cap // ROW_ALIGN) + off[i * N_GROUPS + g]) * ROW_ALIGN, 0)),
                  pl.BlockSpec((tm, LANE), lambda i, g, *_: (i, 0)),
                  pl.BlockSpec((tm, d), lambda i, g, *_: (i, 0)),
                  pl.BlockSpec((1, d), lambda i, g, *_: (0, 0))],
        out_specs=pl.BlockSpec((tm, d), lambda i, g, *_: (i, 0)),
        scratch_shapes=[pltpu.VMEM((rows + tw, d), F32)],
    )
    return pl.pallas_call(
        functools.partial(_moe_combine_body, tm=tm, tw=tw, rows=rows),
        grid_spec=grid_spec,
        out_shape=jax.ShapeDtypeStruct((n, d), F32),
        compiler_params=_cparams(("arbitrary", "arbitrary")),
        name="moe_combine",
    )(segb, off, yg, gate, x, g2)


def moe_grouped(h, gate, w1, w3, w2, e0, x, g2, tm):
    n = h.shape[0]
    tb = tm
    cap = n + 2 * tm
    max_entries = (n + (n // tm) * N_GROUPS * (ROW_ALIGN - 1) + N_GROUPS * tm) // tb + N_GROUPS + 1
    grp = gate[:, GRP_LANE].astype(jnp.int32)

    def run(tw):
        segb, off, e_grp, e_rt, e_kind = _moe_plan(grp, tm, tw, tb, cap, max_entries)
        xg, gg = moe_dispatch(h, gate, segb, off, tm, tw, cap)
        yg = moe_group_ffn(e_grp, e_rt, e_kind, xg, gg, w1, w3, w2, e0, tb, cap)
        return moe_combine(yg, gate, x, g2, segb, off, tm, tw, cap)

    seg_max = jnp.max(jax.nn.one_hot(grp, N_GROUPS, dtype=jnp.int32).reshape(n // tm, tm, N_GROUPS).sum(axis=1))
    return lax.cond(seg_max <= tm // 2, lambda: run(tm // 2), lambda: run(tm))


def _final_norm_body(x_ref, w_ref, o_ref):
    x = x_ref[...]
    o_ref[...] = x * lax.rsqrt(jnp.mean(x * x, axis=-1, keepdims=True) + EPS) * w_ref[...]


def final_norm(x, w, tm):
    n, d = x.shape
    return pl.pallas_call(
        _final_norm_body,
        grid=(n // tm,),
        in_specs=[pl.BlockSpec((tm, d), lambda i: (i, 0)), pl.BlockSpec((1, d), lambda i: (0, 0))],
        out_specs=pl.BlockSpec((tm, d), lambda i: (i, 0)),
        out_shape=jax.ShapeDtypeStruct((n, d), F32),
        compiler_params=_cparams(("arbitrary",)),
        name="final_norm",
    )(x, w)


def _row_tile(n, pref):
    t = min(pref, n)
    while n % t:
        t //= 2
    return t


def kernel(x_prompt, x_sample, c_prompt, c_sample, cache_nsa_cmp, cache_nsa_sel, page_table, state_nsa_win, state_rwkv, state_rwkv_shift, state_gdn, state_gdn_conv, norm_mix, norm_ffn, norm_final, w_ada, b_ada, even_w_in, even_w_out, nsa_cmp_pos, nsa_cmp_w, rwkv_mu, rwkv_w0, rwkv_w2, rwkv_a0, rwkv_a2, rwkv_g2, rwkv_kk, rwkv_ka, rwkv_rk, rwkv_ln_w, rwkv_ln_b, odd_w_in, odd_w_out, gdn_conv_w, gdn_a_log, gdn_dt_bias, gdn_norm_w, moe_w_grp, moe_b_grp, moe_w_exp, moe_b_exp, moe_w1, moe_w3, moe_w2):
    bp, t, d = x_prompt.shape
    bs, ts, _ = x_sample.shape
    assert bp == 1 and ts <= SPAD and ts < CMP_BLK
    depth = norm_mix.shape[0]
    n_pages, page = page_table.shape[1], cache_nsa_cmp.shape[2]
    past = n_pages * page
    wb = state_nsa_win.shape[2]
    ns = bs * SPAD
    tq, tq_s, tk = NSA_TQ, NSA_TQ_SAMPLE, NSA_TK
    tm_p = _row_tile(t, ROW_TILE)
    tm_s = ns

    rows_c = -(-(1 + bs) // 8) * 8
    c_all = jnp.concatenate([c_prompt, c_sample, jnp.zeros((rows_c - 1 - bs, d), F32)], axis=0)
    ada = adaln(c_all, w_ada, b_ada)

    def mods(i):
        mp = [ada[i, 0:1, j * d:(j + 1) * d] for j in range(6)]
        ms = [jnp.repeat(ada[i, 1:1 + bs, j * d:(j + 1) * d], SPAD, axis=0) for j in range(6)]
        return mp, ms

    xp = x_prompt[0]
    xs = jnp.pad(x_sample, ((0, 0), (0, SPAD - ts), (0, 0))).reshape(ns, d)

    def unpad(a):
        return a.reshape(bs, SPAD, -1)[:, :ts]

    w1_all, w3_all, w2_all = (w.reshape((-1,) + w.shape[2:]) for w in (moe_w1, moe_w3, moe_w2))
    outs = {k: [] for k in ("cmp_p", "cmp_s", "sel_p", "sel_s", "win_p", "win_s", "rw_p", "rw_s", "sh_p", "sh_s",
                            "gd_p", "gd_s", "cv_p", "cv_s")}

    for i in range(depth):
        (sh1p, sc1p, gt1p, sh2p, sc2p, gt2p), (sh1s, sc1s, gt1s, sh2s, sc2s, gt2s) = mods(i)
        j = i // 2
        nw = norm_mix[i][None, :]
        if i % 2 == 0:
            w_packed = _pack_even_w(even_w_in[j])
            mu = _pack_rw_vec(rwkv_mu[j])
            wts, wc = _cmp_weights(nsa_cmp_pos[j], nsa_cmp_w[j])
            vec = jnp.stack([rwkv_w0[j], rwkv_a0[j], rwkv_kk[j], rwkv_ka[j], rwkv_ln_w[j], rwkv_ln_b[j],
                             jnp.zeros_like(rwkv_w0[j]), jnp.zeros_like(rwkv_w0[j])])
            pad_lora = lambda w: jnp.concatenate([w, jnp.zeros((128 - w.shape[0], w.shape[1]), w.dtype)], axis=0)
            w2p, a2p, g2p = pad_lora(rwkv_w2[j]), pad_lora(rwkv_a2[j]), rwkv_g2[j]
            hid = jnp.arange(RWKV_W) // RWKV_HD
            seg = (hid[:, None] == hid[None, :]).astype(F32)[:RWKV_W // 2, :RWKV_W // 2]
            rk = rwkv_rk[j].reshape(1, RWKV_W)
            wo_nsa, wo_rw = even_w_out[j][:512].astype(BF16), even_w_out[j][512:].astype(BF16)

            kv, qt, gt, ks, vst, kw, vwt, rw, hl = even_proj(xp, nw, sc1p, sh1p, w_packed, tm_p, 8)
            kvc = compress_prompt(kv, wts, wc, tm_p)
            gates = gt[:24].reshape(NSA_KV_HEADS, 12, t)
            gates = jnp.pad(gates, ((0, 0), (0, 4), (0, 0)))[None]
            o_nsa = nsa_attention(
                qt[None], gates, kvc[None], kvc.T[None], ks[None], vst[None], kw[None], vwt[None],
                tq=tq, tk=tk, wk=WINDOW + tq,
                pos0_fn=lambda qi: qi * tq,
                wstart_fn=lambda qi: jnp.maximum(qi * tq - WINDOW, 0),
                wpos0_fn=lambda qi: jnp.maximum(qi * tq - WINDOW, 0))[0]
            o_rw, s_rw = rwkv_mix(rw, jnp.zeros((1, 8, RW_COLS), F32), jnp.zeros((1, RWKV_HEADS, 64, 64), F32),
                                  mu, vec, w2p, a2p, g2p, seg, rk, c=CHUNK, valid=CHUNK)
            xp = out_proj([o_nsa, o_rw], [wo_nsa, wo_rw], xp, gt1p, tm_p)
            outs["cmp_p"].append(kv[:, 0:256].reshape(1, t, 2, 2, 64))
            outs["sel_p"].append(kv[:, 256:512].reshape(1, t, 2, 2, 64))
            kvw_rows = kv[:, 512:768].reshape(1, t, 2, 2, 64)
            outs["win_p"].append(kvw_rows[:, -min(WINDOW, t):])
            outs["rw_p"].append(s_rw)
            outs["sh_p"].append(hl[-1:])

            kv, qt, gt, _, _, _, _, rw, hl = even_proj(xs, nw, sc1s, sh1s, w_packed, tm_s, ns)
            kv_new = unpad(kv)
            rw0 = small_matmul(jnp.pad(state_rwkv_shift[j], ((0, -bs % 8), (0, 0))), w_packed[:, E_RW:])[:bs]
            rw0 = jnp.pad(rw0[:, None, :], ((0, 0), (7, 0), (0, 0)))
            pool_cmp = cache_nsa_cmp[j].transpose(0, 2, 3, 4, 1).reshape(-1, 256, page)
            pool_sel = cache_nsa_sel[j].transpose(0, 2, 3, 4, 1).reshape(-1, 256, page)
            kvc_s = compress_paged(pool_cmp, page_table, nsa_cmp_pos[j], wc, math.gcd(n_pages, CMP_PAGES_PER_STEP))
            tail = jnp.pad(kv_new[:, :, 256:512], ((0, 0), (0, tk - ts), (0, 0)))
            wbuf = state_nsa_win[j].reshape(bs, wb, 256)
            kvw_all = jnp.concatenate([wbuf, kv_new[:, :, 512:768]], axis=1)
            wk_s = -(-(wb + ts) // 128) * 128
            kvw_pad = jnp.pad(kvw_all, ((0, 0), (0, wk_s - wb - ts), (0, 0)))
            kw_s = kvw_pad[:, :, :128].astype(BF16)
            vwt_s = jnp.swapaxes(kvw_pad[:, :, 128:], 1, 2).astype(BF16)
            qt_s = jnp.pad(qt.reshape(512, bs, SPAD).transpose(1, 0, 2), ((0, 0), (0, 0), (0, tq_s - SPAD)))
            g_s = gt[:24].reshape(NSA_KV_HEADS, 12, bs, SPAD).transpose(2, 0, 1, 3)
            g_s = jnp.pad(g_s, ((0, 0), (0, 0), (0, 4), (0, tq_s - SPAD)))
            o_nsa = nsa_attention_paged(
                qt_s, g_s, kvc_s, jnp.swapaxes(kvc_s, 1, 2), pool_sel, page_table, tail, kw_s, vwt_s,
                tq=tq_s, tk=tk, wk=wk_s,
                pos0_fn=lambda qi: past,
                wstart_fn=lambda qi: 0,
                wpos0_fn=lambda qi: past - wb)
            o_nsa = o_nsa[:, :SPAD].reshape(ns, 512)
            o_rw, s_rw = rwkv_mix(rw, rw0, state_rwkv[j], mu, vec, w2p, a2p, g2p, seg, rk, c=SPAD, valid=ts)
            xs = out_proj([o_nsa, o_rw], [wo_nsa, wo_rw], xs, gt1s, tm_s)
            outs["cmp_s"].append(kv_new[:, :, 0:256].reshape(bs, ts, 2, 2, 64))
            outs["sel_s"].append(kv_new[:, :, 256:512].reshape(bs, ts, 2, 2, 64))
            outs["win_s"].append(kvw_all[:, -wb:].reshape(bs, wb, 2, 2, 64))
            outs["rw_s"].append(s_rw)
            outs["sh_s"].append(hl.reshape(bs, SPAD, d)[:, ts - 1])
        else:
            w_in = odd_w_in[j]
            w_packed = jnp.concatenate([w_in, jnp.zeros((d, O_COLS - w_in.shape[1]), F32)], axis=1).astype(BF16)
            conv_w8 = jnp.pad(gdn_conv_w[j], ((0, 8 - CONV_W), (0, 0)))
            hp = jnp.zeros((8, 128), F32)
            hp = hp.at[0, 8:16].set(-jnp.exp(gdn_a_log[j])).at[1, 8:16].set(gdn_dt_bias[j])
            gnw = gdn_norm_w[j][None, :]
            wo = odd_w_out[j].astype(BF16)

            qkv, z, ba = odd_proj(xp, nw, sc1p, sh1p, w_packed, tm_p)
            o_g, s_g = gdn_mix(qkv, z, ba, jnp.zeros((1, 8, 3 * GDN_W), F32),
                               jnp.zeros((1, GDN_HEADS, GDN_HD, GDN_HD), F32), conv_w8, hp, gnw, c=CHUNK, valid=CHUNK)
            xp = out_proj([o_g], [wo], xp, gt1p, tm_p)
            outs["gd_p"].append(s_g)
            outs["cv_p"].append(qkv[None, -(CONV_W - 1):])

            qkv, z, ba = odd_proj(xs, nw, sc1s, sh1s, w_packed, tm_s)
            cs = jnp.pad(state_gdn_conv[j], ((0, 0), (8 - (CONV_W - 1), 0), (0, 0)))
            o_g, s_g = gdn_mix(qkv, z, ba, cs, state_gdn[j], conv_w8, hp, gnw, c=SPAD, valid=ts)
            xs = out_proj([o_g], [wo], xs, gt1s, tm_s)
            xpad = jnp.concatenate([state_gdn_conv[j], unpad(qkv)], axis=1)
            outs["gd_s"].append(s_g)
            outs["cv_s"].append(xpad[:, -(CONV_W - 1):])

        nwf = norm_ffn[i][None, :]
        w_r = jnp.concatenate([moe_w_exp[i], moe_w_grp[i], jnp.zeros((d, LANE - N_EXPERTS - N_GROUPS), F32)], axis=1)
        b_r = jnp.concatenate([moe_b_exp[i], moe_b_grp[i], jnp.zeros((LANE - N_EXPERTS - N_GROUPS,), F32)])[None, :]
        h2, gate = moe_router(xp, nwf, sc2p, sh2p, w_r, b_r, tm_p)
        xp = moe_grouped(h2, gate, w1_all, w3_all, w2_all, i * N_EXPERTS, xp, gt2p, _row_tile(t, MOE_ROW_TILE))
        h2, gate = moe_router(xs, nwf, sc2s, sh2s, w_r, b_r, tm_s)
        xs = moe_ffn(h2, gate, w1_all, w3_all, w2_all, i * N_EXPERTS, xs, gt2s, tm_s)

    nf = norm_final[None, :]
    y_prompt = final_norm(xp, nf, tm_p)[None]
    y_sample = unpad(final_norm(xs, nf, tm_s))
    st = lambda key: jnp.stack(outs[key])
    return (y_prompt, y_sample, st("cmp_p"), st("cmp_s"), st("sel_p"), st("sel_s"), st("win_p"), st("win_s"),
            st("rw_p"), st("rw_s"), st("sh_p"), st("sh_s"), st("gd_p"), st("gd_s"), st("cv_p"), st("cv_s"))
```

```python
import functools
import math

import jax
import jax.numpy as jnp
from jax import lax
from jax.experimental import pallas as pl
from jax.experimental.pallas import tpu as pltpu

F32 = jnp.float32
BF16 = jnp.bfloat16
HIGHEST = lax.Precision.HIGHEST

NSA_HEADS = 8
NSA_KV_HEADS = 2
NSA_GROUP = 4
NSA_HD = 64
CMP_BLK = 64
SEL_BLK = 64
TOPK_BLK = 16
WINDOW = 512
FORCE_BONUS = 2.0 * NSA_GROUP
RWKV_HEADS = 8
RWKV_HD = 64
RWKV_W = 512
RWKV_GN_EPS = 64e-5
GDN_HEADS = 8
GDN_HD = 128
GDN_W = 1024
CONV_W = 4
N_GROUPS = 4
EXP_PER_GROUP = 8
N_EXPERTS = 32
EPS = 1e-6
NEG = -1e30

LANE = 128
GRP_LANE = 64
ROW_ALIGN = 16
SAMPLE_TILE_SLOTS = 8
SPAD = 8
VMEM_LIMIT = 56 * 1024 * 1024

ROW_TILE = 512
MOE_ROW_TILE = 1024
NSA_TQ = 128
NSA_TQ_SAMPLE = 32
NSA_TK = 512
CHUNK = 64
CMP_PAGES_PER_STEP = 64

NN = (((1,), (0,)), ((), ()))
NT = (((1,), (1,)), ((), ()))
TN = (((0,), (0,)), ((), ()))

E_Q, E_KV, E_G, E_RW = 0, 512, 1280, 1408
E_COLS = 1408 + 1920
RW_COLS = 1920
O_COLS = 3072 + 1024 + 128


def _mm(a, b, dims=NN):
    return lax.dot_general(a.astype(BF16), b.astype(BF16), dims, preferred_element_type=F32)


def _mmh(a, b, dims=NN):
    return lax.dot_general(a.astype(F32), b.astype(F32), dims, precision=HIGHEST, preferred_element_type=F32)


def _split(a):
    hi = a.astype(BF16)
    return hi, (a - hi.astype(F32)).astype(BF16)


def _mm3(a, b, dims=NN):
    ah, al = _split(a)
    bh, bl = _split(b)
    d = lambda x, y: lax.dot_general(x, y, dims, preferred_element_type=F32)
    return d(ah, bh) + (d(ah, bl) + d(al, bh))


def _split3(x):
    h1 = x.astype(BF16)
    r1 = x - h1.astype(F32)
    h2 = r1.astype(BF16)
    return h1, h2, (r1 - h2.astype(F32)).astype(BF16)


def _mm01(m01, x):
    m = m01.astype(BF16)
    parts = _split3(x)
    d = lambda y: lax.dot_general(m, y, NN, preferred_element_type=F32)
    return d(parts[0]) + (d(parts[1]) + d(parts[2]))


def _cumsum_rows(x):
    n = x.shape[0]
    row = lax.broadcasted_iota(jnp.int32, x.shape, 0)
    shift = 1
    while shift < n:
        x = x + jnp.where(row >= shift, pltpu.roll(x, shift, 0), 0.0)
        shift *= 2
    return x


def _shift_rows(x, prev8, sh):
    rolled = pltpu.roll(x, sh, 0)
    row8 = lax.broadcasted_iota(jnp.int32, (8, 1), 0)
    head = jnp.where(row8 < sh, pltpu.roll(prev8, sh, 0), rolled[0:8])
    return head if x.shape[0] == 8 else jnp.concatenate([head, rolled[8:]], axis=0)


def _head_sums(xs, seg_half):
    r = xs[0].shape[0]
    half = seg_half.shape[0]
    pieces = [p[:, h0:h0 + half] for x in xs for p in _split3(x) for h0 in (0, half)]
    out = lax.dot_general(jnp.concatenate(pieces, axis=0), seg_half.astype(BF16), NN, preferred_element_type=F32)
    res = []
    for i in range(len(xs)):
        o = [out[(6 * i + u) * r:(6 * i + u + 1) * r] for u in range(6)]
        res.append(jnp.concatenate([o[0] + (o[2] + o[4]), o[1] + (o[3] + o[5])], axis=1))
    return res


def _sigmoid(x):
    return 1.0 / (1.0 + jnp.exp(-x))


def _silu(x):
    return x * _sigmoid(x)


def _softplus(x):
    return jnp.maximum(x, 0.0) + jnp.log(1.0 + jnp.exp(-jnp.abs(x)))


def _cparams(sem):
    return pltpu.CompilerParams(dimension_semantics=sem, vmem_limit_bytes=VMEM_LIMIT)


def _norm_mod(x, nw, sc, sh):
    y = x * lax.rsqrt(jnp.mean(x * x, axis=-1, keepdims=True) + EPS)
    return (y * nw) * (1.0 + sc) + sh


def _mod_spec(rows_mod, tm, d):
    if rows_mod == 1:
        return pl.BlockSpec((1, d), lambda i: (0, 0))
    return pl.BlockSpec((tm, d), lambda i: (i, 0))


def _adaln_body(c_ref, w_ref, b_ref, o_ref):
    o_ref[0] = _mm3(_silu(c_ref[...]), w_ref[0]) + b_ref[0]


def adaln(c_all, w_ada, b_ada):
    depth, d, n6 = w_ada.shape
    rows = c_all.shape[0]
    tn = 768
    return pl.pallas_call(
        _adaln_body,
        grid=(depth, n6 // tn),
        in_specs=[pl.BlockSpec((rows, d), lambda l, j: (0, 0)),
                  pl.BlockSpec((1, d, tn), lambda l, j: (l, 0, j)),
                  pl.BlockSpec((1, 1, tn), lambda l, j: (l, 0, j))],
        out_specs=pl.BlockSpec((1, rows, tn), lambda l, j: (l, 0, j)),
        out_shape=jax.ShapeDtypeStruct((depth, rows, n6), F32),
        compiler_params=_cparams(("arbitrary", "arbitrary")),
        name="adaln",
    )(c_all, w_ada, b_ada.reshape(depth, 1, n6))


def _even_proj_body(x_ref, nw_ref, sc_ref, sh_ref, w_ref,
                    kv_ref, qt_ref, gt_ref, ks_ref, vst_ref, kw_ref, vwt_ref, rw_ref, hl_ref):
    h = _norm_mod(x_ref[...], nw_ref[...], sc_ref[...], sh_ref[...])
    hl = hl_ref.shape[0]
    hl_ref[...] = h[h.shape[0] - hl:, :]
    hb = h.astype(BF16)
    q = _mm(hb, w_ref[:, E_Q:E_Q + 512]) * (NSA_HD ** -0.5)
    qt_ref[...] = q.T.astype(BF16)
    kv = _mm(hb, w_ref[:, E_KV:E_KV + 768])
    kv_ref[...] = kv
    ks_ref[...] = kv[:, 256:384].astype(BF16)
    vst_ref[...] = kv[:, 384:512].T.astype(BF16)
    kw_ref[...] = kv[:, 512:640].astype(BF16)
    vwt_ref[...] = kv[:, 640:768].T.astype(BF16)
    g = _sigmoid(_mm(hb, w_ref[:, E_G:E_G + 128]))
    gt_ref[...] = g.T
    rw_ref[...] = _mm(hb, w_ref[:, E_RW:E_RW + RW_COLS])


def even_proj(x, nw, sc, sh, w_packed, tm, hl_rows):
    n, d = x.shape
    rows_mod = sc.shape[0]
    row = lambda c: pl.BlockSpec((tm, c), lambda i: (i, 0))
    col = lambda r: pl.BlockSpec((r, tm), lambda i: (0, i))
    return pl.pallas_call(
        _even_proj_body,
        grid=(n // tm,),
        in_specs=[row(d), pl.BlockSpec((1, d), lambda i: (0, 0)),
                  _mod_spec(rows_mod, tm, d), _mod_spec(rows_mod, tm, d),
                  pl.BlockSpec((d, E_COLS), lambda i: (0, 0))],
        out_specs=[row(768), col(512), col(128), row(128), col(128), row(128), col(128), row(RW_COLS),
                   pl.BlockSpec((hl_rows, d), lambda i: (0, 0))],
        out_shape=[jax.ShapeDtypeStruct((n, 768), F32),
                   jax.ShapeDtypeStruct((512, n), BF16),
                   jax.ShapeDtypeStruct((128, n), F32),
                   jax.ShapeDtypeStruct((n, 128), BF16),
                   jax.ShapeDtypeStruct((128, n), BF16),
                   jax.ShapeDtypeStruct((n, 128), BF16),
                   jax.ShapeDtypeStruct((128, n), BF16),
                   jax.ShapeDtypeStruct((n, RW_COLS), F32),
                   jax.ShapeDtypeStruct((hl_rows, d), F32)],
        compiler_params=_cparams(("arbitrary",)),
        name="even_proj",
    )(x, nw, sc, sh, w_packed)


def _pack_even_w(w_in):
    d = w_in.shape[0]
    z = lambda c: jnp.zeros((d, c), w_in.dtype)
    nsa = 1304
    rw = w_in[:, nsa:]
    parts = [w_in[:, :1280], w_in[:, 1280:1304], z(104),
             rw[:, :1536], rw[:, 1536:1600], z(64), rw[:, 1600:1664], z(64), rw[:, 1664:1792]]
    return jnp.concatenate(parts, axis=1).astype(BF16)


def _pack_rw_vec(v):
    z = jnp.zeros((64,), v.dtype)
    return jnp.concatenate([v[:1536], v[1536:1600], z, v[1600:1664], z, v[1664:1792]])[None, :]


def _mm_body(x_ref, w_ref, o_ref):
    o_ref[...] = _mm(x_ref[...], w_ref[...])


def small_matmul(x, w):
    return pl.pallas_call(
        _mm_body,
        out_shape=jax.ShapeDtypeStruct((x.shape[0], w.shape[1]), F32),
        compiler_params=pltpu.CompilerParams(vmem_limit_bytes=VMEM_LIMIT),
        name="small_matmul",
    )(x, w)


def _compress_body(x_ref, wts_ref, wc_ref, o_ref):
    x = x_ref[...]
    nb = x.shape[0] // CMP_BLK
    pooled = jnp.sum(x.reshape(nb, CMP_BLK, x.shape[-1]) * wts_ref[...][None], axis=1)
    o_ref[...] = _mm(pooled, wc_ref[...])


def _compress_paged_body(pt_ref, *refs, pps):
    page_refs = refs[:pps]
    wp_ref, wc_ref, o_ref = refs[pps:]
    x = jnp.concatenate([r[0] for r in page_refs], axis=1)
    pooled_t = jnp.concatenate([_mm(x[0:128], wp_ref[0]), _mm(x[128:256], wp_ref[1])], axis=0)
    nb = o_ref.shape[1]
    o_ref[0] = _mm(pooled_t.T[:nb], wc_ref[...])


def _cmp_weights(pos_wts, w_c):
    wts = jnp.repeat(pos_wts.T, 128, axis=1)
    eye2 = jnp.eye(2, dtype=w_c.dtype)
    blocks = [jnp.kron(eye2, w_c[c]) for c in range(2)]
    z = jnp.zeros((128, 128), w_c.dtype)
    wc = jnp.concatenate([jnp.concatenate([blocks[0], z], axis=1),
                          jnp.concatenate([z, blocks[1]], axis=1)], axis=0)
    return wts, wc


def compress_prompt(kv, wts, wc, tr):
    t = kv.shape[0]
    nb = tr // CMP_BLK
    return pl.pallas_call(
        _compress_body,
        grid=(t // tr,),
        in_specs=[pl.BlockSpec((tr, 256), lambda i: (i, 0)),
                  pl.BlockSpec((CMP_BLK, 256), lambda i: (0, 0)),
                  pl.BlockSpec((256, 256), lambda i: (0, 0))],
        out_specs=pl.BlockSpec((nb, 256), lambda i: (i, 0)),
        out_shape=jax.ShapeDtypeStruct((t // CMP_BLK, 256), F32),
        compiler_params=_cparams(("arbitrary",)),
        name="compress_prompt",
    )(kv, wts, wc)


def compress_paged(pool_t, page_table, pos_wts, wc, pages_per_step):
    b, n_pages = page_table.shape
    page = pool_t.shape[2]
    pps = pages_per_step
    nb = pps * page // CMP_BLK
    p_idx = jnp.arange(pps * page)
    wp = jax.nn.one_hot(p_idx // CMP_BLK, LANE, dtype=F32)[None] * pos_wts[:, p_idx % CMP_BLK][:, :, None]

    def page_spec(u):
        return pl.BlockSpec((1, 256, page), lambda bi, g, pt: (pt[bi, g * pps + u], 0, 0))

    grid_spec = pltpu.PrefetchScalarGridSpec(
        num_scalar_prefetch=1,
        grid=(b, n_pages // pps),
        in_specs=[page_spec(u) for u in range(pps)] + [
            pl.BlockSpec((2, pps * page, LANE), lambda bi, g, pt: (0, 0, 0)),
            pl.BlockSpec((256, 256), lambda bi, g, pt: (0, 0))],
        out_specs=pl.BlockSpec((1, nb, 256), lambda bi, g, pt: (bi, g, 0)),
    )
    return pl.pallas_call(
        functools.partial(_compress_paged_body, pps=pps),
        grid_spec=grid_spec,
        out_shape=jax.ShapeDtypeStruct((b, n_pages * page // CMP_BLK, 256), F32),
        compiler_params=_cparams(("arbitrary", "arbitrary")),
        name="compress_paged",
    )(page_table, *([pool_t] * pps), wp, wc)


def _gather_sel_body(pt_ref, tiles_ref, cnt_ref, *refs, pps, n_page_steps, nt):
    del pt_ref
    page_refs = refs[:pps]
    tail_ref, ks_ref, vst_ref = refs[pps:]
    bi = pl.program_id(0)
    a = pl.program_id(1)
    j = tiles_ref[bi * nt + jnp.minimum(a, cnt_ref[bi] - 1)]
    live = a < cnt_ref[bi]

    @pl.when(live & (j < n_page_steps))
    def _():
        ks_ref[0] = jnp.concatenate([r[0][0:128].T for r in page_refs], axis=0).astype(BF16)
        vst_ref[0] = jnp.concatenate([r[0][128:256] for r in page_refs], axis=1).astype(BF16)

    @pl.when(live & (j >= n_page_steps))
    def _():
        x = tail_ref[0]
        ks_ref[0] = x[:, :128].astype(BF16)
        vst_ref[0] = x[:, 128:].T.astype(BF16)


def gather_sel(pool_t, page_table, tail, tk, tiles, cnt, n_slots):
    b, n_pages = page_table.shape
    page = pool_t.shape[2]
    pps = tk // page
    n_page_steps = n_pages // pps
    nt = n_page_steps + 1
    nk = n_slots * tk

    def slot(bi, a, pt, tiles, cnt):
        return jnp.minimum(a, cnt[bi] - 1)

    def page_spec(u):
        def index(bi, a, pt, tiles, cnt):
            j = tiles[bi * nt + slot(bi, a, pt, tiles, cnt)]
            return (pt[bi, jnp.minimum(j * pps + u, n_pages - 1)], 0, 0)
        return pl.BlockSpec((1, 256, page), index)

    grid_spec = pltpu.PrefetchScalarGridSpec(
        num_scalar_prefetch=3,
        grid=(b, jnp.max(cnt)),
        in_specs=[page_spec(u) for u in range(pps)] + [pl.BlockSpec((1, tk, 256), lambda bi, a, *_: (bi, 0, 0))],
        out_specs=[pl.BlockSpec((1, tk, 128), lambda bi, a, *s: (bi, slot(bi, a, *s), 0)),
                   pl.BlockSpec((1, 128, tk), lambda bi, a, *s: (bi, 0, slot(bi, a, *s)))],
    )
    return pl.pallas_call(
        functools.partial(_gather_sel_body, pps=pps, n_page_steps=n_page_steps, nt=nt),
        grid_spec=grid_spec,
        out_shape=[jax.ShapeDtypeStruct((b, nk, 128), BF16), jax.ShapeDtypeStruct((b, 128, nk), BF16)],
        compiler_params=_cparams(("arbitrary", "arbitrary")),
        name="gather_sel",
    )(page_table, tiles, cnt, *([pool_t] * pps), tail)


MASKED = -1e30
M_INIT = -1e29


def _nsa_query(qt_ref, k, tq):
    w4 = NSA_GROUP * tq
    qb = qt_ref[0].astype(F32)
    qcat = jnp.concatenate([qb[g * 64:(g + 1) * 64] for g in range(NSA_GROUP)], axis=1)
    q2 = jnp.concatenate([qcat, qcat], axis=0)
    row = lax.broadcasted_iota(jnp.int32, (128, w4), 0)
    qe = jnp.where(row // 64 == k, q2, 0.0)
    gidx = lax.broadcasted_iota(jnp.int32, (128, w4), 1) // tq
    base = jnp.where(k == 0, 0.5, 0.5 / 16.0)
    slope = base * jnp.where(gidx == 0, 1.0, jnp.where(gidx == 1, 0.5, jnp.where(gidx == 2, 0.25, 0.125)))
    mult = jnp.where(row == 0, 16.0, jnp.where(row == 1, 1.0, jnp.where(row == 2, 128.0,
                                                                         jnp.where(row == 3, 64.0, 0.0))))
    return jnp.concatenate([qe, slope * mult], axis=0).astype(BF16)


def _pos_features(rows, tile_rel):
    r = lax.broadcasted_iota(jnp.int32, (rows, LANE), 0)
    lane = lax.broadcasted_iota(jnp.int32, (rows, LANE), 1)
    ab = jnp.where(lane == 0, r // 16, jnp.where(lane == 1, r % 16, 0)).astype(F32)
    return jnp.where(lane == 2, tile_rel, ab).astype(BF16)


def _gate_rows(gb, j, tq):
    return jnp.concatenate([gb[g * 3 + j:g * 3 + j + 1, :] for g in range(NSA_GROUP)], axis=1)


def _nsa_select_body(qt_ref, g_ref, kvc_ref, kvct_ref, kw_ref, vwt_ref, part_ref, sel_ref, flag_ref, *,
                     tq, tk, wk, nbc, nb, pos0_fn, wstart_fn, wpos0_fn):
    i = pl.program_id(1)
    k = pl.program_id(2)
    w4 = NSA_GROUP * tq
    pos0 = pos0_fn(i)
    qa = _nsa_query(qt_ref, k, tq)
    pos_q = pos0 + lax.broadcasted_iota(jnp.int32, (1, w4), 1) % tq

    def softmax_cols(s, bad):
        s = jnp.where(bad, MASKED, s)
        m = jnp.maximum(jnp.max(s, axis=0, keepdims=True), M_INIT)
        e = jnp.exp(s - m)
        return e / jnp.maximum(jnp.sum(e, axis=0, keepdims=True), 1e-30)

    n_i = lax.broadcasted_iota(jnp.int32, (nbc, LANE), 0)
    lane_c = lax.broadcasted_iota(jnp.int32, (nbc, LANE), 1)
    feat_c = jnp.where(lane_c == 3, n_i - pos0 // CMP_BLK, 0).astype(F32).astype(BF16)
    kc = jnp.concatenate([kvc_ref[0][:, :128].astype(BF16), feat_c], axis=1)
    c_end = lax.broadcasted_iota(jnp.int32, (nbc, 1), 0) * CMP_BLK + (CMP_BLK - 1)
    p_c = softmax_cols(lax.dot_general(kc, qa, NN, preferred_element_type=F32), c_end > pos_q)
    vct = kvct_ref[0, pl.ds(pl.multiple_of(128 + k * 64, 64), 64), :]
    o_c = _mm(vct, p_c)

    imp = p_c[:, 0:tq]
    for g in range(1, NSA_GROUP):
        imp = imp + p_c[:, g * tq:(g + 1) * tq]
    if nb > nbc:
        imp = jnp.concatenate([imp, jnp.zeros((nb - nbc, tq), F32)], axis=0)
    blk = lax.broadcasted_iota(jnp.int32, (nb, tq), 0)
    cur = (pos0 + lax.broadcasted_iota(jnp.int32, (1, tq), 1)) // SEL_BLK
    forced = (blk == cur) | (blk == cur - 1) | (blk == 0)
    score = jnp.where(blk <= cur, imp + jnp.where(forced, FORCE_BONUS, 0.0), -1.0)
    for _ in range(min(TOPK_BLK, nb)):
        m = jnp.max(score, axis=0, keepdims=True)
        first = jnp.min(jnp.where(score == m, blk, nb), axis=0, keepdims=True)
        score = jnp.where(blk == first, -2.0, score)
    sel = jnp.where(score == -2.0, 1.0, 0.0)
    sel_ref[0, 0] = sel
    bpt = tk // SEL_BLK
    any_row = jnp.max(sel, axis=1, keepdims=True)
    flag_ref[0, 0] = jnp.max(any_row.reshape(nb // bpt, bpt, 1), axis=1)

    wstart = wstart_fn(i)
    if not isinstance(wstart, int):
        wstart = pl.multiple_of(wstart, 128)
    wpos0 = wpos0_fn(i)
    tile_rel = jnp.asarray((wpos0 - pos0) // 128, F32)
    kw = jnp.concatenate([kw_ref[0, pl.ds(wstart, wk), :], _pos_features(wk, tile_rel)], axis=1)
    dist_w = pos_q - (wpos0 + lax.broadcasted_iota(jnp.int32, (wk, 1), 0))
    p_w = softmax_cols(lax.dot_general(kw, qa, NN, preferred_element_type=F32), (dist_w < 0) | (dist_w >= WINDOW))
    vwin = vwt_ref[0, pl.ds(pl.multiple_of(k * 64, 64), 64), pl.ds(wstart, wk)]
    o_w = _mm(vwin, p_w)

    gb = g_ref[0, 0]
    part_ref[0, 0] = _gate_rows(gb, 0, tq) * o_c + _gate_rows(gb, 2, tq) * o_w


def nsa_select(qt, gates, kvc, kvct, kw, vwt, *, nb, tq, tk, wk, pos0_fn, wstart_fn, wpos0_fn):
    b, _, nq = qt.shape
    nbc = kvc.shape[1]
    nw = kw.shape[1]
    nqt = nq // tq
    nt = nb * SEL_BLK // tk
    w4 = NSA_GROUP * tq
    assert nbc <= 256 and tk <= 512 and wk <= 1024
    body = functools.partial(_nsa_select_body, tq=tq, tk=tk, wk=wk, nbc=nbc, nb=nb, pos0_fn=pos0_fn,
                             wstart_fn=wstart_fn, wpos0_fn=wpos0_fn)
    full = lambda s1, s2: pl.BlockSpec((1, s1, s2), lambda bi, i, k: (bi, 0, 0))
    step = lambda s1, s2: pl.BlockSpec((1, 1, s1, s2), lambda bi, i, k: (bi, i * NSA_KV_HEADS + k, 0, 0))
    return pl.pallas_call(
        body,
        grid=(b, nqt, NSA_KV_HEADS),
        in_specs=[pl.BlockSpec((1, 256, tq), lambda bi, i, k: (bi, k, i)),
                  pl.BlockSpec((1, 1, 16, tq), lambda bi, i, k: (bi, k, 0, i)),
                  full(nbc, 256), full(256, nbc), full(nw, 128), full(128, nw)],
        out_specs=[step(64, w4), step(nb, tq), step(nt, 1)],
        out_shape=[jax.ShapeDtypeStruct((b, nqt * 2, 64, w4), F32),
                   jax.ShapeDtypeStruct((b, nqt * 2, nb, tq), F32),
                   jax.ShapeDtypeStruct((b, nqt * 2, nt, 1), F32)],
        compiler_params=_cparams(("arbitrary", "arbitrary", "arbitrary")),
        name="nsa_select",
    )(qt, gates, kvc, kvct, kw, vwt)


def _nsa_selected_body(list_ref, slot_ref, cnt_ref, qt_ref, g_ref, sel_ref, ks_ref, vst_ref, part_ref, o_ref, *,
                       tq, tk, nt, pos0_fn):
    bi = pl.program_id(0)
    i = pl.program_id(1)
    k = pl.program_id(2)
    step = (bi * pl.num_programs(1) + i) * NSA_KV_HEADS + k
    w4 = NSA_GROUP * tq
    pos0 = pos0_fn(i)
    qa = _nsa_query(qt_ref, k, tq)
    pos_q = pos0 + lax.broadcasted_iota(jnp.int32, (1, w4), 1) % tq
    bpt = tk // SEL_BLK
    row_k = lax.broadcasted_iota(jnp.int32, (tk, 1), 0)
    r = lax.broadcasted_iota(jnp.int32, (tk, LANE), 0)
    lane = lax.broadcasted_iota(jnp.int32, (tk, LANE), 1)
    feat_ab = jnp.where(lane == 0, r // 16, jnp.where(lane == 1, r % 16, 0)).astype(F32)

    n_act = cnt_ref[step]

    def tile_scores(jj, live):
        j = list_ref[step * nt + jj]
        off = pl.multiple_of(j * tk, tk)
        buf = pl.multiple_of(slot_ref[step * nt + jj] * tk, tk)
        tile_rel = ((off - pos0) // 128).astype(F32)
        feat = jnp.where(lane == 2, tile_rel, feat_ab).astype(BF16)
        kj = jnp.concatenate([ks_ref[0, pl.ds(buf, tk), :], feat], axis=1)
        s = lax.dot_general(kj, qa, NN, preferred_element_type=F32)
        selb = (sel_ref[0, 0, pl.ds(pl.multiple_of(j * bpt, bpt), bpt), :] - 1.0) * (-MASKED)
        selb = jnp.concatenate([selb] * NSA_GROUP, axis=1) + jnp.where(live, 0.0, MASKED)
        s = s + jnp.broadcast_to(selb[:, None, :], (bpt, SEL_BLK, w4)).reshape(tk, w4)
        s = jnp.where(row_k > pos_q - off, MASKED, s)
        return s, vst_ref[0, pl.ds(pl.multiple_of(k * 64, 64), 64), pl.ds(buf, tk)]

    def kv_pair(pp, carry):
        m_i, l_i, acc = carry
        second = 2 * pp + 1
        s_a, v_a = tile_scores(2 * pp, True)
        s_b, v_b = tile_scores(jnp.minimum(second, n_act - 1), second < n_act)
        m_new = jnp.maximum(m_i, jnp.maximum(jnp.max(s_a, axis=0, keepdims=True), jnp.max(s_b, axis=0, keepdims=True)))
        p_a = jnp.exp(s_a - m_new)
        p_b = jnp.exp(s_b - m_new)
        alpha = jnp.exp(m_i - m_new)
        l_new = l_i * alpha + (jnp.sum(p_a, axis=0, keepdims=True) + jnp.sum(p_b, axis=0, keepdims=True))
        return m_new, l_new, acc * alpha + (_mm(v_a, p_a) + _mm(v_b, p_b))

    init = (jnp.full((1, w4), M_INIT, F32), jnp.zeros((1, w4), F32), jnp.zeros((64, w4), F32))
    _, l_s, acc_s = lax.fori_loop(0, (n_act + 1) // 2, kv_pair, init)
    o_s = acc_s / jnp.maximum(l_s, 1e-30)
    o_t = part_ref[0, 0] + _gate_rows(g_ref[0, 0], 1, tq) * o_s
    o_ref[0] = jnp.concatenate([o_t[:, g * tq:(g + 1) * tq].T for g in range(NSA_GROUP)], axis=1)


def nsa_selected(tile_list, slot_list, tile_cnt, qt, gates, sel, ks, vst, part, *, tq, tk, pos0_fn):
    b, _, nq = qt.shape
    nk = ks.shape[1]
    nb = sel.shape[2]
    nt = nb * SEL_BLK // tk
    w4 = NSA_GROUP * tq
    full = lambda s1, s2: pl.BlockSpec((1, s1, s2), lambda bi, i, k, *_: (bi, 0, 0))
    step = lambda s1, s2: pl.BlockSpec((1, 1, s1, s2), lambda bi, i, k, *_: (bi, i * NSA_KV_HEADS + k, 0, 0))
    grid_spec = pltpu.PrefetchScalarGridSpec(
        num_scalar_prefetch=3,
        grid=(b, nq // tq, NSA_KV_HEADS),
        in_specs=[pl.BlockSpec((1, 256, tq), lambda bi, i, k, *_: (bi, k, i)),
                  pl.BlockSpec((1, 1, 16, tq), lambda bi, i, k, *_: (bi, k, 0, i)),
                  step(nb, tq), full(nk, 128), full(128, nk), step(64, w4)],
        out_specs=pl.BlockSpec((1, tq, 256), lambda bi, i, k, *_: (bi, i, k)),
    )
    return pl.pallas_call(
        functools.partial(_nsa_selected_body, tq=tq, tk=tk, nt=nt, pos0_fn=pos0_fn),
        grid_spec=grid_spec,
        out_shape=jax.ShapeDtypeStruct((b, nq, 512), F32),
        compiler_params=_cparams(("arbitrary", "arbitrary", "arbitrary")),
        name="nsa_selected",
    )(tile_list, slot_list, tile_cnt, qt, gates, sel, ks, vst, part)


def _active_first(active):
    order = jnp.argsort(jnp.where(active, 0, 1), axis=-1, stable=True).astype(jnp.int32)
    return order, jnp.sum(active, axis=-1).astype(jnp.int32)


def nsa_attention(qt, gates, kvc, kvct, ks, vst, kw, vwt, *, tq, tk, wk, pos0_fn, wstart_fn, wpos0_fn):
    nb = ks.shape[1] // SEL_BLK
    part, sel, flags = nsa_select(qt, gates, kvc, kvct, kw, vwt, nb=nb, tq=tq, tk=tk, wk=wk, pos0_fn=pos0_fn,
                                  wstart_fn=wstart_fn, wpos0_fn=wpos0_fn)
    order, cnt = _active_first(flags[..., 0] > 0.5)
    return nsa_selected(order.reshape(-1), order.reshape(-1), cnt.reshape(-1), qt, gates, sel, ks, vst, part,
                        tq=tq, tk=tk, pos0_fn=pos0_fn)


def nsa_attention_paged(qt, gates, kvc, kvct, pool_t, page_table, tail, kw, vwt, *, tq, tk, wk, pos0_fn, wstart_fn,
                        wpos0_fn):
    nb = (page_table.shape[1] * pool_t.shape[2] + tk) // SEL_BLK
    part, sel, flags = nsa_select(qt, gates, kvc, kvct, kw, vwt, nb=nb, tq=tq, tk=tk, wk=wk, pos0_fn=pos0_fn,
                                  wstart_fn=wstart_fn, wpos0_fn=wpos0_fn)
    active = flags[..., 0] > 0.5
    tiles_b, cnt_b = _active_first(jnp.any(active, axis=1))
    slot_of_tile = jnp.argsort(tiles_b, axis=-1).astype(jnp.int32)
    order, cnt = _active_first(active)
    slots = jnp.take_along_axis(jnp.broadcast_to(slot_of_tile[:, None, :], order.shape), order, axis=-1)

    def run(n_slots):
        ks, vst = gather_sel(pool_t, page_table, tail, tk, tiles_b.reshape(-1), cnt_b, n_slots)
        return nsa_selected(order.reshape(-1), slots.reshape(-1), cnt.reshape(-1), qt, gates, sel, ks, vst, part,
                            tq=tq, tk=tk, pos0_fn=pos0_fn)

    nt = tiles_b.shape[-1]
    few = min(SAMPLE_TILE_SLOTS, nt)
    return lax.cond(jnp.max(cnt_b) <= few, lambda: run(few), lambda: run(nt))


def _tri_inverse(ms, c):
    eye = (lax.broadcasted_iota(jnp.int32, (c, c), 0) == lax.broadcasted_iota(jnp.int32, (c, c), 1)).astype(F32)
    ps = [-m for m in ms]
    ts = [eye + p for p in ps]
    steps = max(int(math.ceil(math.log2(c))) - 1, 0)
    d = lambda x, y: lax.dot_general(x, y, NN, preferred_element_type=F32)
    for _ in range(steps):
        sp = [_split(p) for p in ps]
        ps = [d(ph, ph) + (d(ph, pl_) + d(pl_, ph)) for ph, pl_ in sp]
        sp = [_split(p) for p in ps]
        st = [_split(t) for t in ts]
        ts = [t + (d(th, ph) + (d(th, pl_) + d(tl, ph))) for t, (th, tl), (ph, pl_) in zip(ts, st, sp)]
    return ts


def _rwkv_body(rw_ref, rw0_ref, s0_ref, mu_ref, vec_ref, w2_ref, a2_ref, g2_ref, seg_ref, rk_ref,
               o_ref, sfin_ref, buf_ref, s_ref, y_ref, *, c, valid, n_chunks):
    ci = pl.program_id(1)
    halo = 8

    @pl.when(ci == 0)
    def _():
        buf_ref[...] = rw0_ref[0]
        s_ref[...] = s0_ref[0]

    cur = rw_ref[...]
    prev = _shift_rows(cur, buf_ref[...], 1)
    xr = cur + (prev - cur) * mu_ref[...]
    buf_ref[...] = cur[c - halo:, :]

    vec = vec_ref[...]
    w0, a0, kkw, kaw, ln_w, ln_b = (vec[r:r + 1, :] for r in range(6))
    r = xr[:, 0:512]
    kx = xr[:, 512:1024]
    v = xr[:, 1024:1536]
    xw = xr[:, 1536:1664]
    xa = xr[:, 1664:1792]
    xg = xr[:, 1792:1920]
    wl = -jnp.exp(-_softplus(-(w0 + _mm(jnp.tanh(xw), w2_ref[...]))) - 0.5)
    a = _sigmoid(a0 + _mm(xa, a2_ref[...]))
    gate = _mm(_sigmoid(xg), g2_ref[...])
    seg = seg_ref[...]
    zk = kx * kkw
    k2 = kx * (1.0 + (a - 1.0) * kaw)
    zz_sum, rk_sum = _head_sums([zk * zk, r * k2 * rk_ref[...]], seg)
    kk = zk * lax.rsqrt(zz_sum + EPS)
    bonus = rk_sum * v
    if valid < c:
        live = lax.broadcasted_iota(jnp.int32, (c, 1), 0) < valid
        wl = jnp.where(live, wl, 0.0)
        kk = jnp.where(live, kk, 0.0)
        k2 = jnp.where(live, k2, 0.0)
        v = jnp.where(live, v, 0.0)
        r = jnp.where(live, r, 0.0)
    bb = kk * a

    ri = lax.broadcasted_iota(jnp.int32, (c, c), 0)
    cj = lax.broadcasted_iota(jnp.int32, (c, c), 1)
    tril = ri >= cj
    strict = ri > cj
    cw = _cumsum_rows(wl)
    ecw = jnp.exp(cw)
    einv = jnp.exp(-cw)
    p_c = ecw[c - 1:c, :]
    kt = kk * jnp.exp(cw - wl)
    bt = bb * einv
    ki = k2 * einv
    rt = r * ecw
    bd = bt * p_c
    kd = ki * p_c

    heads = range(RWKV_HEADS)
    sls = [slice(h * RWKV_HD, (h + 1) * RWKV_HD) for h in heads]
    kt_h = [kt[:, sl] for sl in sls]
    bt_h = [bt[:, sl] for sl in sls]
    ki_h = [ki[:, sl] for sl in sls]
    rt_h = [rt[:, sl] for sl in sls]
    v_h = [v[:, sl] for sl in sls]
    l_m = [jnp.where(strict, _mm3(kt_h[h], bt_h[h], NT), 0.0) for h in heads]
    m_kk = [jnp.where(strict, _mm(kt_h[h], ki_h[h], NT), 0.0) for h in heads]
    a_rb = [jnp.where(tril, _mm(rt_h[h], bt_h[h], NT), 0.0) for h in heads]
    a_rk = [jnp.where(tril, _mm(rt_h[h], ki_h[h], NT), 0.0) for h in heads]
    mv = [_mm(m_kk[h], v_h[h]) for h in heads]
    y0 = [_mm(a_rk[h], v_h[h]) for h in heads]
    t_inv = _tri_inverse(l_m, c)
    w_h = [_mm3(t_inv[h], kt_h[h]) for h in heads]
    u_h = [-_mm3(t_inv[h], mv[h]) for h in heads]
    s_h = [s_ref[h] for h in heads]
    e_h = [u_h[h] - _mm(w_h[h], s_h[h], NT) for h in heads]
    y1 = [_mm(rt_h[h], s_h[h], NT) + y0[h] for h in heads]
    y_h = [y1[h] + _mm(a_rb[h], e_h[h]) for h in heads]
    ds = [_mm(e_h[h], bd[:, sls[h]], TN) + _mm(v_h[h], kd[:, sls[h]], TN) for h in heads]
    for h in heads:
        s_ref[h] = s_h[h] * p_c[:, sls[h]] + ds[h]
        mu_h = jnp.mean(y_h[h], axis=-1, keepdims=True)
        d_h = y_h[h] - mu_h
        var_h = jnp.mean(d_h * d_h, axis=-1, keepdims=True)
        y_ref[:, sls[h]] = d_h * lax.rsqrt(var_h + RWKV_GN_EPS)

    o_ref[...] = (y_ref[...] * ln_w + ln_b + bonus) * gate

    @pl.when(ci == n_chunks - 1)
    def _():
        sfin_ref[0] = s_ref[...]


def rwkv_mix(rw, rw0, s0, mu, vec, w2, a2, g2, seg, rk, *, c, valid):
    b = s0.shape[0]
    rows = rw.shape[0]
    n_chunks = rows // (b * c)
    const = lambda s: pl.BlockSpec(s, lambda bi, ci: tuple(0 for _ in s))
    return pl.pallas_call(
        functools.partial(_rwkv_body, c=c, valid=valid, n_chunks=n_chunks),
        grid=(b, n_chunks),
        in_specs=[pl.BlockSpec((c, RW_COLS), lambda bi, ci: (bi * n_chunks + ci, 0)),
                  pl.BlockSpec((1, 8, RW_COLS), lambda bi, ci: (bi, 0, 0)),
                  pl.BlockSpec((1, RWKV_HEADS, 64, 64), lambda bi, ci: (bi, 0, 0, 0)),
                  const((1, RW_COLS)), const((8, 512)), const((128, 512)), const((128, 512)), const((128, 512)),
                  const((RWKV_W // 2, RWKV_W // 2)), const((1, 512))],
        out_specs=[pl.BlockSpec((c, 512), lambda bi, ci: (bi * n_chunks + ci, 0)),
                   pl.BlockSpec((1, RWKV_HEADS, 64, 64), lambda bi, ci: (bi, 0, 0, 0))],
        out_shape=[jax.ShapeDtypeStruct((rows, 512), F32),
                   jax.ShapeDtypeStruct((b, RWKV_HEADS, 64, 64), F32)],
        scratch_shapes=[pltpu.VMEM((8, RW_COLS), F32), pltpu.VMEM((RWKV_HEADS, 64, 64), F32),
                        pltpu.VMEM((c, 512), F32)],
        compiler_params=_cparams(("arbitrary", "arbitrary")),
        name="rwkv_mix",
    )(rw, rw0, s0, mu, vec, w2, a2, g2, seg, rk)


def _out_proj_body(*refs, n_in):
    a_refs = refs[:n_in]
    w_refs = refs[n_in:2 * n_in]
    x_ref, g_ref, o_ref = refs[2 * n_in:]
    y = _mm(a_refs[0][...], w_refs[0][...])
    for a_ref, w_ref in zip(a_refs[1:], w_refs[1:]):
        y = y + _mm(a_ref[...], w_ref[...])
    o_ref[...] = x_ref[...] + g_ref[...] * y


def out_proj(acts, weights, x, gate, tm):
    n, d = x.shape
    n_in = len(acts)
    return pl.pallas_call(
        functools.partial(_out_proj_body, n_in=n_in),
        grid=(n // tm,),
        in_specs=[pl.BlockSpec((tm, a.shape[1]), lambda i: (i, 0)) for a in acts]
        + [pl.BlockSpec(w.shape, lambda i: (0, 0)) for w in weights]
        + [pl.BlockSpec((tm, d), lambda i: (i, 0)), _mod_spec(gate.shape[0], tm, d)],
        out_specs=pl.BlockSpec((tm, d), lambda i: (i, 0)),
        out_shape=jax.ShapeDtypeStruct((n, d), F32),
        compiler_params=_cparams(("arbitrary",)),
        name="out_proj",
    )(*acts, *weights, x, gate)


def _odd_proj_body(x_ref, nw_ref, sc_ref, sh_ref, w_ref, qkv_ref, z_ref, ba_ref):
    hb = _norm_mod(x_ref[...], nw_ref[...], sc_ref[...], sh_ref[...]).astype(BF16)
    qkv_ref[...] = _mm(hb, w_ref[:, 0:3072])
    z_ref[...] = _mm(hb, w_ref[:, 3072:4096])
    ba_ref[...] = _mm(hb, w_ref[:, 4096:O_COLS])


def odd_proj(x, nw, sc, sh, w_packed, tm):
    n, d = x.shape
    rows_mod = sc.shape[0]
    row = lambda c: pl.BlockSpec((tm, c), lambda i: (i, 0))
    return pl.pallas_call(
        _odd_proj_body,
        grid=(n // tm,),
        in_specs=[row(d), pl.BlockSpec((1, d), lambda i: (0, 0)),
                  _mod_spec(rows_mod, tm, d), _mod_spec(rows_mod, tm, d),
                  pl.BlockSpec((d, O_COLS), lambda i: (0, 0))],
        out_specs=[row(3072), row(1024), row(128)],
        out_shape=[jax.ShapeDtypeStruct((n, 3072), F32), jax.ShapeDtypeStruct((n, 1024), F32),
                   jax.ShapeDtypeStruct((n, 128), F32)],
        compiler_params=_cparams(("arbitrary",)),
        name="odd_proj",
    )(x, nw, sc, sh, w_packed)


def _gdn_body(qkv_ref, z_ref, ba_ref, cs_ref, s0_ref, cw_ref, hp_ref, nw_ref,
              o_ref, sfin_ref, buf_ref, s_ref, *, c, valid, n_chunks):
    ci = pl.program_id(1)
    halo = 8

    @pl.when(ci == 0)
    def _():
        buf_ref[...] = cs_ref[0]
        s_ref[...] = s0_ref[0]

    x = qkv_ref[...]
    prev8 = buf_ref[...]
    cw = cw_ref[...]
    conv = x * cw[CONV_W - 1:CONV_W, :]
    for j in range(CONV_W - 1):
        conv = conv + _shift_rows(x, prev8, CONV_W - 1 - j) * cw[j:j + 1, :]
    buf_ref[...] = x[c - halo:, :]
    conv = _silu(conv)

    hp = hp_ref[...]
    ba = ba_ref[...]
    beta_f = _sigmoid(ba)
    g_f = hp[0:1, :] * _softplus(ba + hp[1:2, :])
    if valid < c:
        live = lax.broadcasted_iota(jnp.int32, (c, 1), 0) < valid
        beta_f = jnp.where(live, beta_f, 0.0)
        g_f = jnp.where(live, g_f, 0.0)
        conv = jnp.where(live, conv, 0.0)

    ri = lax.broadcasted_iota(jnp.int32, (c, c), 0)
    cj = lax.broadcasted_iota(jnp.int32, (c, c), 1)
    tril = ri >= cj
    strict = ri > cj
    gc = _cumsum_rows(g_f)
    gct = gc.T
    z = z_ref[...]
    nw = nw_ref[...]

    heads = range(GDN_HEADS)
    sls = [slice(h * GDN_HD, (h + 1) * GDN_HD) for h in heads]
    q_h = [conv[:, sl] for sl in sls]
    k_h = [conv[:, GDN_W + h * GDN_HD:GDN_W + (h + 1) * GDN_HD] for h in heads]
    v_h = [conv[:, 2 * GDN_W + h * GDN_HD:2 * GDN_W + (h + 1) * GDN_HD] for h in heads]
    q_h = [q * lax.rsqrt(jnp.sum(q * q, axis=-1, keepdims=True) + EPS) * (GDN_HD ** -0.5) for q in q_h]
    k_h = [k * lax.rsqrt(jnp.sum(k * k, axis=-1, keepdims=True) + EPS) for k in k_h]
    g_col = [gc[:, 8 + h:9 + h] for h in heads]
    eg = [jnp.exp(g) for g in g_col]
    b_col = [beta_f[:, h:h + 1] for h in heads]
    decay = [jnp.where(tril, jnp.exp(jnp.where(tril, g_col[h] - gct[8 + h:9 + h, :], 0.0)), 0.0) for h in heads]
    kb = [k_h[h] * b_col[h] for h in heads]
    vb = [v_h[h] * b_col[h] for h in heads]
    m_h = [jnp.where(strict, _mm3(kb[h], k_h[h], NT) * decay[h], 0.0) for h in heads]
    qk = [jnp.where(tril, _mm(q_h[h], k_h[h], NT) * decay[h], 0.0) for h in heads]
    t_inv = _tri_inverse(m_h, c)
    u_h = [_mm(t_inv[h], vb[h]) for h in heads]
    w_h = [_mm(t_inv[h], kb[h] * eg[h]) for h in heads]
    s_h = [s_ref[h] for h in heads]
    v_new = [u_h[h] - _mm(w_h[h], s_h[h]) for h in heads]
    o1 = [_mm(q_h[h] * eg[h], s_h[h]) for h in heads]
    o_h = [o1[h] + _mm(qk[h], v_new[h]) for h in heads]
    g_last = [g[c - 1:c, :] for g in g_col]
    ds = [_mm(k_h[h] * jnp.exp(g_last[h] - g_col[h]), v_new[h], TN) for h in heads]
    for h in heads:
        s_ref[h] = s_h[h] * jnp.exp(g_last[h]) + ds[h]
        o_n = o_h[h] * lax.rsqrt(jnp.mean(o_h[h] * o_h[h], axis=-1, keepdims=True) + EPS) * nw
        o_ref[:, sls[h]] = o_n * _silu(z[:, sls[h]])

    @pl.when(ci == n_chunks - 1)
    def _():
        sfin_ref[0] = s_ref[...]


def gdn_mix(qkv, z, ba, cs, s0, conv_w8, hp, nw, *, c, valid):
    b = s0.shape[0]
    rows = qkv.shape[0]
    n_chunks = rows // (b * c)
    const = lambda s: pl.BlockSpec(s, lambda bi, ci: tuple(0 for _ in s))
    row = lambda w: pl.BlockSpec((c, w), lambda bi, ci: (bi * n_chunks + ci, 0))
    return pl.pallas_call(
        functools.partial(_gdn_body, c=c, valid=valid, n_chunks=n_chunks),
        grid=(b, n_chunks),
        in_specs=[row(3072), row(1024), row(128),
                  pl.BlockSpec((1, 8, 3072), lambda bi, ci: (bi, 0, 0)),
                  pl.BlockSpec((1, GDN_HEADS, 128, 128), lambda bi, ci: (bi, 0, 0, 0)),
                  const((8, 3072)), const((8, 128)), const((1, 128))],
        out_specs=[row(1024), pl.BlockSpec((1, GDN_HEADS, 128, 128), lambda bi, ci: (bi, 0, 0, 0))],
        out_shape=[jax.ShapeDtypeStruct((rows, 1024), F32),
                   jax.ShapeDtypeStruct((b, GDN_HEADS, 128, 128), F32)],
        scratch_shapes=[pltpu.VMEM((8, 3072), F32), pltpu.VMEM((GDN_HEADS, 128, 128), F32)],
        compiler_params=_cparams(("arbitrary", "arbitrary")),
        name="gdn_mix",
    )(qkv, z, ba, cs, s0, conv_w8, hp, nw)


def _router_body(x_ref, nw_ref, sc_ref, sh_ref, wr_ref, br_ref, h_ref, gate_ref):
    h = _norm_mod(x_ref[...], nw_ref[...], sc_ref[...], sh_ref[...])
    h_ref[...] = h.astype(BF16)
    logits = _mm3(h, wr_ref[...]) + br_ref[...]
    tm = logits.shape[0]
    lane = lax.broadcasted_iota(jnp.int32, (tm, LANE), 1)
    is_grp = (lane >= N_EXPERTS) & (lane < N_EXPERTS + N_GROUPS)
    gl = jnp.where(is_grp, logits, NEG)
    gmax = jnp.max(gl, axis=-1, keepdims=True)
    g_idx = jnp.min(jnp.where(gl == gmax, lane, 4 * LANE), axis=-1, keepdims=True) - N_EXPERTS
    g_w = 1.0 / jnp.sum(jnp.where(is_grp, jnp.exp(gl - gmax), 0.0), axis=-1, keepdims=True)
    in_grp = (lane < N_EXPERTS) & (lane // EXP_PER_GROUP == g_idx)
    el = jnp.where(in_grp, logits, NEG)
    emax = jnp.max(el, axis=-1, keepdims=True)
    e = jnp.where(in_grp, jnp.exp(el - emax), 0.0)
    p = e / jnp.sum(e, axis=-1, keepdims=True)
    p1 = jnp.where(in_grp, p, -1.0)
    m1 = jnp.max(p1, axis=-1, keepdims=True)
    i1 = jnp.min(jnp.where(p1 == m1, lane, 4 * LANE), axis=-1, keepdims=True)
    p2 = jnp.where(lane == i1, -1.0, p1)
    m2 = jnp.max(p2, axis=-1, keepdims=True)
    i2 = jnp.min(jnp.where(p2 == m2, lane, 4 * LANE), axis=-1, keepdims=True)
    tot = m1 + m2
    gate = jnp.where(lane == i1, m1 / tot * g_w, jnp.where(lane == i2, m2 / tot * g_w, 0.0))
    gate_ref[...] = jnp.where(lane == GRP_LANE, g_idx.astype(F32), gate)


def moe_router(x, nw, sc, sh, w_r, b_r, tm):
    n, d = x.shape
    rows_mod = sc.shape[0]
    return pl.pallas_call(
        _router_body,
        grid=(n // tm,),
        in_specs=[pl.BlockSpec((tm, d), lambda i: (i, 0)), pl.BlockSpec((1, d), lambda i: (0, 0)),
                  _mod_spec(rows_mod, tm, d), _mod_spec(rows_mod, tm, d),
                  pl.BlockSpec((d, LANE), lambda i: (0, 0)), pl.BlockSpec((1, LANE), lambda i: (0, 0))],
        out_specs=[pl.BlockSpec((tm, d), lambda i: (i, 0)), pl.BlockSpec((tm, LANE), lambda i: (i, 0))],
        out_shape=[jax.ShapeDtypeStruct((n, d), BF16), jax.ShapeDtypeStruct((n, LANE), F32)],
        compiler_params=_cparams(("arbitrary",)),
        name="moe_router",
    )(x, nw, sc, sh, w_r, b_r)


def _moe_body(h_ref, gate_ref, w1_ref, w3_ref, w2_ref, x_ref, g2_ref, o_ref, acc_ref):
    e = pl.program_id(1)

    @pl.when(e == 0)
    def _():
        acc_ref[...] = jnp.zeros_like(acc_ref)

    hb = h_ref[...]
    he = _silu(_mm(hb, w1_ref[0])) * _mm(hb, w3_ref[0])
    y = _mm(he, w2_ref[0])
    gate = gate_ref[...]
    lane = lax.broadcasted_iota(jnp.int32, gate.shape, 1)
    ge = jnp.sum(jnp.where(lane == e, gate, 0.0), axis=-1, keepdims=True)
    acc_ref[...] += ge * y

    @pl.when(e == pl.num_programs(1) - 1)
    def _():
        o_ref[...] = x_ref[...] + g2_ref[...] * acc_ref[...]


def moe_ffn(h, gate, w1, w3, w2, e0, x, g2, tm):
    n, d = x.shape
    de = w1.shape[2]
    return pl.pallas_call(
        _moe_body,
        grid=(n // tm, N_EXPERTS),
        in_specs=[pl.BlockSpec((tm, d), lambda i, e: (i, 0)), pl.BlockSpec((tm, LANE), lambda i, e: (i, 0)),
                  pl.BlockSpec((1, d, de), lambda i, e: (e0 + e, 0, 0)),
                  pl.BlockSpec((1, d, de), lambda i, e: (e0 + e, 0, 0)),
                  pl.BlockSpec((1, de, d), lambda i, e: (e0 + e, 0, 0)),
                  pl.BlockSpec((tm, d), lambda i, e: (i, 0)),
                  pl.BlockSpec((1, d), lambda i, e: (0, 0)) if g2.shape[0] == 1
                  else pl.BlockSpec((tm, d), lambda i, e: (i, 0))],
        out_specs=pl.BlockSpec((tm, d), lambda i, e: (i, 0)),
        out_shape=jax.ShapeDtypeStruct((n, d), F32),
        scratch_shapes=[pltpu.VMEM((tm, d), F32)],
        compiler_params=_cparams(("arbitrary", "arbitrary")),
        name="moe_ffn",
    )(h, gate, w1, w3, w2, x, g2)


def _moe_plan(grp, tm, tw, tb, cap, max_entries):
    nt = grp.shape[0] // tm
    cnt = jax.nn.one_hot(grp, N_GROUPS, dtype=jnp.int32).reshape(nt, tm, N_GROUPS).sum(axis=1)
    pc = (cnt + ROW_ALIGN - 1) // ROW_ALIGN * ROW_ALIGN
    segb = jnp.cumsum(pc, axis=1) - pc
    off = jnp.cumsum(pc, axis=0) - pc
    tot = pc.sum(axis=0)
    n_real = (tot + tb - 1) // tb
    n_all = jnp.minimum((tot + tw + tb - 1) // tb, cap // tb)
    ends = jnp.cumsum(n_all)
    s = jnp.arange(max_entries)
    g_of = jnp.sum(s[:, None] >= ends[None, :], axis=1)
    active = g_of < N_GROUPS
    g_c = jnp.minimum(g_of, N_GROUPS - 1)
    rt = s - (ends - n_all)[g_c]
    live = tot[g_c] - rt * tb
    real = jnp.where(live <= tb // 4, 3, jnp.where(live <= tb // 2, 5, 1))
    kind = jnp.where(active, jnp.where(rt < n_real[g_c], real, 2), 0)
    last = ends[-1] - 1
    e_grp = jnp.where(active, g_c, g_c[last])
    e_rt = jnp.where(active, rt, rt[last])
    i32 = lambda a: a.reshape(-1).astype(jnp.int32)
    return i32(segb), i32(off // ROW_ALIGN), i32(e_grp), i32(e_rt), i32(kind)


def _group_perm(gate, segb_ref, base, tm, rows):
    gt = gate.T
    grp = gt[GRP_LANE:GRP_LANE + 1, :]
    gi = lax.broadcasted_iota(jnp.int32, (8, tm), 0).astype(F32)
    oh = jnp.where(gi == grp, 1.0, 0.0)
    r_i = lax.broadcasted_iota(jnp.int32, (tm, tm), 0)
    c_i = lax.broadcasted_iota(jnp.int32, (tm, tm), 1)
    before = jnp.where(r_i < c_i, 1.0, 0.0).astype(BF16)
    rank = lax.dot_general(oh.astype(BF16), before, NN, preferred_element_type=F32)
    dest = jnp.zeros((1, tm), F32)
    for g in range(N_GROUPS):
        dest = dest + oh[g:g + 1] * (segb_ref[base + g].astype(F32) + rank[g:g + 1])
    rows_i = lax.broadcasted_iota(jnp.int32, (rows, tm), 0).astype(F32)
    return jnp.where(rows_i == dest, 1.0, 0.0).astype(BF16)


def _moe_dispatch_body(segb_ref, off_ref, h_ref, gate_ref, xg_in, gg_in, xg_ref, gg_ref, xs_ref, gs_ref, *, tm, tw,
                       rows):
    del off_ref, xg_in, gg_in
    i = pl.program_id(0)
    g = pl.program_id(1)

    @pl.when((i == 0) & (g == 0))
    def _():
        xs_ref[...] = jnp.zeros_like(xs_ref)
        gs_ref[...] = jnp.zeros_like(gs_ref)

    @pl.when(g == 0)
    def _():
        gate = gate_ref[...]
        p = _group_perm(gate, segb_ref, i * N_GROUPS, tm, rows)
        xs_ref[0:rows, :] = lax.dot_general(p, h_ref[...], NN, preferred_element_type=F32).astype(BF16)
        gs_ref[0:rows, :] = _mm01(p, gate)

    start = pl.multiple_of(segb_ref[i * N_GROUPS + g], ROW_ALIGN)
    xg_ref[...] = xs_ref[pl.ds(start, tw), :]
    gg_ref[...] = gs_ref[pl.ds(start, tw), :]


def moe_dispatch(h, gate, segb, off, tm, tw, cap):
    n, d = h.shape
    rows = tm + N_GROUPS * ROW_ALIGN
    win = lambda w: pl.BlockSpec((pl.Element(tw), pl.Element(w)),
                                 lambda i, g, segb, off: ((g * (cap // ROW_ALIGN) + off[i * N_GROUPS + g]) * ROW_ALIGN, 0))
    grid_spec = pltpu.PrefetchScalarGridSpec(
        num_scalar_prefetch=2,
        grid=(n // tm, N_GROUPS),
        in_specs=[pl.BlockSpec((tm, d), lambda i, g, *_: (i, 0)), pl.BlockSpec((tm, LANE), lambda i, g, *_: (i, 0)),
                  pl.BlockSpec(memory_space=pl.ANY), pl.BlockSpec(memory_space=pl.ANY)],
        out_specs=[win(d), win(LANE)],
        scratch_shapes=[pltpu.VMEM((rows + tw, d), BF16), pltpu.VMEM((rows + tw, LANE), F32)],
    )
    return pl.pallas_call(
        functools.partial(_moe_dispatch_body, tm=tm, tw=tw, rows=rows),
        grid_spec=grid_spec,
        out_shape=[jax.ShapeDtypeStruct((N_GROUPS * cap, d), BF16), jax.ShapeDtypeStruct((N_GROUPS * cap, LANE), F32)],
        input_output_aliases={4: 0, 5: 1},
        compiler_params=_cparams(("arbitrary", "arbitrary")),
        name="moe_dispatch",
    )(segb, off, h, gate, jnp.zeros((N_GROUPS * cap, d), BF16), jnp.zeros((N_GROUPS * cap, LANE), F32))


def _moe_group_body(grp_ref, rt_ref, kind_ref, xg_ref, gg_ref, w1_ref, w3_ref, w2_ref, yg_ref, acc_ref):
    del rt_ref
    s = pl.program_id(0)
    e = pl.program_id(1)
    kind = kind_ref[s]
    last = e == pl.num_programs(1) - 1

    def run(rows):
        @pl.when(e == 0)
        def _():
            acc_ref[...] = jnp.zeros_like(acc_ref)

        xb = xg_ref[0:rows, :]
        he = _silu(_mm(xb, w1_ref[0])) * _mm(xb, w3_ref[0])
        y = _mm(he, w2_ref[0])
        gate = gg_ref[0:rows, :]
        lane = lax.broadcasted_iota(jnp.int32, gate.shape, 1)
        ge = jnp.sum(jnp.where(lane == grp_ref[s] * EXP_PER_GROUP + e, gate, 0.0), axis=-1, keepdims=True)
        acc_ref[0:rows, :] += ge * y

        @pl.when(last)
        def _():
            yg_ref[...] = acc_ref[...]

    tb = xg_ref.shape[0]
    for code, rows in ((1, tb), (5, tb // 2), (3, tb // 4)):
        pl.when(kind == code)(functools.partial(run, rows))

    @pl.when((kind == 2) & last)
    def _():
        yg_ref[...] = jnp.zeros_like(yg_ref)


def moe_group_ffn(e_grp, e_rt, e_kind, xg, gg, w1, w3, w2, e0, tb, cap):
    d = xg.shape[1]
    de = w1.shape[2]
    row = lambda s, e, grp, rt, kind: (grp[s] * (cap // tb) + rt[s], 0)
    wsel = lambda s, e, grp, rt, kind: (e0 + grp[s] * EXP_PER_GROUP + jnp.where(kind[s] % 2 == 1, e, EXP_PER_GROUP - 1),
                                        0, 0)
    grid_spec = pltpu.PrefetchScalarGridSpec(
        num_scalar_prefetch=3,
        grid=(e_grp.shape[0], EXP_PER_GROUP),
        in_specs=[pl.BlockSpec((tb, d), row), pl.BlockSpec((tb, LANE), row),
                  pl.BlockSpec((1, d, de), wsel), pl.BlockSpec((1, d, de), wsel), pl.BlockSpec((1, de, d), wsel)],
        out_specs=pl.BlockSpec((tb, d), row),
        scratch_shapes=[pltpu.VMEM((tb, d), F32)],
    )
    return pl.pallas_call(
        _moe_group_body,
        grid_spec=grid_spec,
        out_shape=jax.ShapeDtypeStruct((N_GROUPS * cap, d), F32),
        compiler_params=_cparams(("arbitrary", "arbitrary")),
        name="moe_group_ffn",
    )(e_grp, e_rt, e_kind, xg, gg, w1, w3, w2)


def _moe_combine_body(segb_ref, off_ref, yg_ref, gate_ref, x_ref, g2_ref, o_ref, ys_ref, *, tm, tw, rows):
    del off_ref
    i = pl.program_id(0)
    g = pl.program_id(1)

    @pl.when((i == 0) & (g == 0))
    def _():
        ys_ref[...] = jnp.zeros_like(ys_ref)

    start = pl.multiple_of(segb_ref[i * N_GROUPS + g], ROW_ALIGN)
    ys_ref[pl.ds(start, tw), :] = yg_ref[...]

    @pl.when(g == N_GROUPS - 1)
    def _():
        p = _group_perm(gate_ref[...], segb_ref, i * N_GROUPS, tm, rows)
        yh, yl = _split(ys_ref[0:rows, :])
        y = (lax.dot_general(p, yh, TN, preferred_element_type=F32)
             + lax.dot_general(p, yl, TN, preferred_element_type=F32))
        o_ref[...] = x_ref[...] + g2_ref[...] * y


def moe_combine(yg, gate, x, g2, segb, off, tm, tw, cap):
    n, d = x.shape
    rows = tm + N_GROUPS * ROW_ALIGN
    grid_spec = pltpu.PrefetchScalarGridSpec(
        num_scalar_prefetch=2,
        grid=(n // tm, N_GROUPS),
        in_specs=[pl.BlockSpec((pl.Element(tw), pl.Element(d)),
                               lambda i, g, segb, off: ((g * (cap // ROW_ALIGN) + off[i * N_GROUPS + g]) * ROW_ALIGN, 0)),
                  pl.BlockSpec((tm, LANE), lambda i, g, *_: (i, 0)),
                  pl.BlockSpec((tm, d), lambda i, g, *_: (i, 0)),
                  pl.BlockSpec((1, d), lambda i, g, *_: (0, 0))],
        out_specs=pl.BlockSpec((tm, d), lambda i, g, *_: (i, 0)),
        scratch_shapes=[pltpu.VMEM((rows + tw, d), F32)],
    )
    return pl.pallas_call(
        functools.partial(_moe_combine_body, tm=tm, tw=tw, rows=rows),
        grid_spec=grid_spec,
        out_shape=jax.ShapeDtypeStruct((n, d), F32),
        compiler_params=_cparams(("arbitrary", "arbitrary")),
        name="moe_combine",
    )(segb, off, yg, gate, x, g2)


def moe_grouped(h, gate, w1, w3, w2, e0, x, g2, tm):
    n = h.shape[0]
    tb = tm
    cap = n + 2 * tm
    max_entries = (n + (n // tm) * N_GROUPS * (ROW_ALIGN - 1) + N_GROUPS * tm) // tb + N_GROUPS + 1
    grp = gate[:, GRP_LANE].astype(jnp.int32)

    def run(tw):
        segb, off, e_grp, e_rt, e_kind = _moe_plan(grp, tm, tw, tb, cap, max_entries)
        xg, gg = moe_dispatch(h, gate, segb, off, tm, tw, cap)
        yg = moe_group_ffn(e_grp, e_rt, e_kind, xg, gg, w1, w3, w2, e0, tb, cap)
        return moe_combine(yg, gate, x, g2, segb, off, tm, tw, cap)

    seg_max = jnp.max(jax.nn.one_hot(grp, N_GROUPS, dtype=jnp.int32).reshape(n // tm, tm, N_GROUPS).sum(axis=1))
    return lax.cond(seg_max <= tm // 2, lambda: run(tm // 2), lambda: run(tm))


def _final_norm_body(x_ref, w_ref, o_ref):
    x = x_ref[...]
    o_ref[...] = x * lax.rsqrt(jnp.mean(x * x, axis=-1, keepdims=True) + EPS) * w_ref[...]


def final_norm(x, w, tm):
    n, d = x.shape
    return pl.pallas_call(
        _final_norm_body,
        grid=(n // tm,),
        in_specs=[pl.BlockSpec((tm, d), lambda i: (i, 0)), pl.BlockSpec((1, d), lambda i: (0, 0))],
        out_specs=pl.BlockSpec((tm, d), lambda i: (i, 0)),
        out_shape=jax.ShapeDtypeStruct((n, d), F32),
        compiler_params=_cparams(("arbitrary",)),
        name="final_norm",
    )(x, w)


def _row_tile(n, pref):
    t = min(pref, n)
    while n % t:
        t //= 2
    return t


def kernel(x_prompt, x_sample, c_prompt, c_sample, cache_nsa_cmp, cache_nsa_sel, page_table, state_nsa_win, state_rwkv, state_rwkv_shift, state_gdn, state_gdn_conv, norm_mix, norm_ffn, norm_final, w_ada, b_ada, even_w_in, even_w_out, nsa_cmp_pos, nsa_cmp_w, rwkv_mu, rwkv_w0, rwkv_w2, rwkv_a0, rwkv_a2, rwkv_g2, rwkv_kk, rwkv_ka, rwkv_rk, rwkv_ln_w, rwkv_ln_b, odd_w_in, odd_w_out, gdn_conv_w, gdn_a_log, gdn_dt_bias, gdn_norm_w, moe_w_grp, moe_b_grp, moe_w_exp, moe_b_exp, moe_w1, moe_w3, moe_w2):
    bp, t, d = x_prompt.shape
    bs, ts, _ = x_sample.shape
    assert bp == 1 and ts <= SPAD and ts < CMP_BLK
    depth = norm_mix.shape[0]
    n_pages, page = page_table.shape[1], cache_nsa_cmp.shape[2]
    past = n_pages * page
    wb = state_nsa_win.shape[2]
    ns = bs * SPAD
    tq, tq_s, tk = NSA_TQ, NSA_TQ_SAMPLE, NSA_TK
    tm_p = _row_tile(t, ROW_TILE)
    tm_s = ns

    rows_c = -(-(1 + bs) // 8) * 8
    c_all = jnp.concatenate([c_prompt, c_sample, jnp.zeros((rows_c - 1 - bs, d), F32)], axis=0)
    ada = adaln(c_all, w_ada, b_ada)

    def mods(i):
        mp = [ada[i, 0:1, j * d:(j + 1) * d] for j in range(6)]
        ms = [jnp.repeat(ada[i, 1:1 + bs, j * d:(j + 1) * d], SPAD, axis=0) for j in range(6)]
        return mp, ms

    xp = x_prompt[0]
    xs = jnp.pad(x_sample, ((0, 0), (0, SPAD - ts), (0, 0))).reshape(ns, d)

    def unpad(a):
        return a.reshape(bs, SPAD, -1)[:, :ts]

    w1_all, w3_all, w2_all = (w.reshape((-1,) + w.shape[2:]) for w in (moe_w1, moe_w3, moe_w2))
    outs = {k: [] for k in ("cmp_p", "cmp_s", "sel_p", "sel_s", "win_p", "win_s", "rw_p", "rw_s", "sh_p", "sh_s",
                            "gd_p", "gd_s", "cv_p", "cv_s")}

    for i in range(depth):
        (sh1p, sc1p, gt1p, sh2p, sc2p, gt2p), (sh1s, sc1s, gt1s, sh2s, sc2s, gt2s) = mods(i)
        j = i // 2
        nw = norm_mix[i][None, :]
        if i % 2 == 0:
            w_packed = _pack_even_w(even_w_in[j])
            mu = _pack_rw_vec(rwkv_mu[j])
            wts, wc = _cmp_weights(nsa_cmp_pos[j], nsa_cmp_w[j])
            vec = jnp.stack([rwkv_w0[j], rwkv_a0[j], rwkv_kk[j], rwkv_ka[j], rwkv_ln_w[j], rwkv_ln_b[j],
                             jnp.zeros_like(rwkv_w0[j]), jnp.zeros_like(rwkv_w0[j])])
            pad_lora = lambda w: jnp.concatenate([w, jnp.zeros((128 - w.shape[0], w.shape[1]), w.dtype)], axis=0)
            w2p, a2p, g2p = pad_lora(rwkv_w2[j]), pad_lora(rwkv_a2[j]), rwkv_g2[j]
            hid = jnp.arange(RWKV_W) // RWKV_HD
            seg = (hid[:, None] == hid[None, :]).astype(F32)[:RWKV_W // 2, :RWKV_W // 2]
            rk = rwkv_rk[j].reshape(1, RWKV_W)
            wo_nsa, wo_rw = even_w_out[j][:512].astype(BF16), even_w_out[j][512:].astype(BF16)

            kv, qt, gt, ks, vst, kw, vwt, rw, hl = even_proj(xp, nw, sc1p, sh1p, w_packed, tm_p, 8)
            kvc = compress_prompt(kv, wts, wc, tm_p)
            gates = gt[:24].reshape(NSA_KV_HEADS, 12, t)
            gates = jnp.pad(gates, ((0, 0), (0, 4), (0, 0)))[None]
            o_nsa = nsa_attention(
                qt[None], gates, kvc[None], kvc.T[None], ks[None], vst[None], kw[None], vwt[None],
                tq=tq, tk=tk, wk=WINDOW + tq,
                pos0_fn=lambda qi: qi * tq,
                wstart_fn=lambda qi: jnp.maximum(qi * tq - WINDOW, 0),
                wpos0_fn=lambda qi: jnp.maximum(qi * tq - WINDOW, 0))[0]
            o_rw, s_rw = rwkv_mix(rw, jnp.zeros((1, 8, RW_COLS), F32), jnp.zeros((1, RWKV_HEADS, 64, 64), F32),
                                  mu, vec, w2p, a2p, g2p, seg, rk, c=CHUNK, valid=CHUNK)
            xp = out_proj([o_nsa, o_rw], [wo_nsa, wo_rw], xp, gt1p, tm_p)
            outs["cmp_p"].append(kv[:, 0:256].reshape(1, t, 2, 2, 64))
            outs["sel_p"].append(kv[:, 256:512].reshape(1, t, 2, 2, 64))
            kvw_rows = kv[:, 512:768].reshape(1, t, 2, 2, 64)
            outs["win_p"].append(kvw_rows[:, -min(WINDOW, t):])
            outs["rw_p"].append(s_rw)
            outs["sh_p"].append(hl[-1:])

            kv, qt, gt, _, _, _, _, rw, hl = even_proj(xs, nw, sc1s, sh1s, w_packed, tm_s, ns)
            kv_new = unpad(kv)
            rw0 = small_matmul(jnp.pad(state_rwkv_shift[j], ((0, -bs % 8), (0, 0))), w_packed[:, E_RW:])[:bs]
            rw0 = jnp.pad(rw0[:, None, :], ((0, 0), (7, 0), (0, 0)))
            pool_cmp = cache_nsa_cmp[j].transpose(0, 2, 3, 4, 1).reshape(-1, 256, page)
            pool_sel = cache_nsa_sel[j].transpose(0, 2, 3, 4, 1).reshape(-1, 256, page)
            kvc_s = compress_paged(pool_cmp, page_table, nsa_cmp_pos[j], wc, math.gcd(n_pages, CMP_PAGES_PER_STEP))
            tail = jnp.pad(kv_new[:, :, 256:512], ((0, 0), (0, tk - ts), (0, 0)))
            wbuf = state_nsa_win[j].reshape(bs, wb, 256)
            kvw_all = jnp.concatenate([wbuf, kv_new[:, :, 512:768]], axis=1)
            wk_s = -(-(wb + ts) // 128) * 128
            kvw_pad = jnp.pad(kvw_all, ((0, 0), (0, wk_s - wb - ts), (0, 0)))
            kw_s = kvw_pad[:, :, :128].astype(BF16)
            vwt_s = jnp.swapaxes(kvw_pad[:, :, 128:], 1, 2).astype(BF16)
            qt_s = jnp.pad(qt.reshape(512, bs, SPAD).transpose(1, 0, 2), ((0, 0), (0, 0), (0, tq_s - SPAD)))
            g_s = gt[:24].reshape(NSA_KV_HEADS, 12, bs, SPAD).transpose(2, 0, 1, 3)
            g_s = jnp.pad(g_s, ((0, 0), (0, 0), (0, 4), (0, tq_s - SPAD)))
            o_nsa = nsa_attention_paged(
                qt_s, g_s, kvc_s, jnp.swapaxes(kvc_s, 1, 2), pool_sel, page_table, tail, kw_s, vwt_s,
                tq=tq_s, tk=tk, wk=wk_s,
                pos0_fn=lambda qi: past,
                wstart_fn=lambda qi: 0,
                wpos0_fn=lambda qi: past - wb)
            o_nsa = o_nsa[:, :SPAD].reshape(ns, 512)
            o_rw, s_rw = rwkv_mix(rw, rw0, state_rwkv[j], mu, vec, w2p, a2p, g2p, seg, rk, c=SPAD, valid=ts)
            xs = out_proj([o_nsa, o_rw], [wo_nsa, wo_rw], xs, gt1s, tm_s)
            outs["cmp_s"].append(kv_new[:, :, 0:256].reshape(bs, ts, 2, 2, 64))
            outs["sel_s"].append(kv_new[:, :, 256:512].reshape(bs, ts, 2, 2, 64))
            outs["win_s"].append(kvw_all[:, -wb:].reshape(bs, wb, 2, 2, 64))
            outs["rw_s"].append(s_rw)
            outs["sh_s"].append(hl.reshape(bs, SPAD, d)[:, ts - 1])
        else:
            w_in = odd_w_in[j]
            w_packed = jnp.concatenate([w_in, jnp.zeros((d, O_COLS - w_in.shape[1]), F32)], axis=1).astype(BF16)
            conv_w8 = jnp.pad(gdn_conv_w[j], ((0, 8 - CONV_W), (0, 0)))
            hp = jnp.zeros((8, 128), F32)
            hp = hp.at[0, 8:16].set(-jnp.exp(gdn_a_log[j])).at[1, 8:16].set(gdn_dt_bias[j])
            gnw = gdn_norm_w[j][None, :]
            wo = odd_w_out[j].astype(BF16)

            qkv, z, ba = odd_proj(xp, nw, sc1p, sh1p, w_packed, tm_p)
            o_g, s_g = gdn_mix(qkv, z, ba, jnp.zeros((1, 8, 3 * GDN_W), F32),
                               jnp.zeros((1, GDN_HEADS, GDN_HD, GDN_HD), F32), conv_w8, hp, gnw, c=CHUNK, valid=CHUNK)
            xp = out_proj([o_g], [wo], xp, gt1p, tm_p)
            outs["gd_p"].append(s_g)
            outs["cv_p"].append(qkv[None, -(CONV_W - 1):])

            qkv, z, ba = odd_proj(xs, nw, sc1s, sh1s, w_packed, tm_s)
            cs = jnp.pad(state_gdn_conv[j], ((0, 0), (8 - (CONV_W - 1), 0), (0, 0)))
            o_g, s_g = gdn_mix(qkv, z, ba, cs, state_gdn[j], conv_w8, hp, gnw, c=SPAD, valid=ts)
            xs = out_proj([o_g], [wo], xs, gt1s, tm_s)
            xpad = jnp.concatenate([state_gdn_conv[j], unpad(qkv)], axis=1)
            outs["gd_s"].append(s_g)
            outs["cv_s"].append(xpad[:, -(CONV_W - 1):])

        nwf = norm_ffn[i][None, :]
        w_r = jnp.concatenate([moe_w_exp[i], moe_w_grp[i], jnp.zeros((d, LANE - N_EXPERTS - N_GROUPS), F32)], axis=1)
        b_r = jnp.concatenate([moe_b_exp[i], moe_b_grp[i], jnp.zeros((LANE - N_EXPERTS - N_GROUPS,), F32)])[None, :]
        h2, gate = moe_router(xp, nwf, sc2p, sh2p, w_r, b_r, tm_p)
        xp = moe_grouped(h2, gate, w1_all, w3_all, w2_all, i * N_EXPERTS, xp, gt2p, _row_tile(t, MOE_ROW_TILE))
        h2, gate = moe_router(xs, nwf, sc2s, sh2s, w_r, b_r, tm_s)
        xs = moe_ffn(h2, gate, w1_all, w3_all, w2_all, i * N_EXPERTS, xs, gt2s, tm_s)

    nf = norm_final[None, :]
    y_prompt = final_norm(xp, nf, tm_p)[None]
    y_sample = unpad(final_norm(xs, nf, tm_s))
    st = lambda key: jnp.stack(outs[key])
    return (y_prompt, y_sample, st("cmp_p"), st("cmp_s"), st("sel_p"), st("sel_s"), st("win_p"), st("win_s"),
            st("rw_p"), st("rw_s"), st("sh_p"), st("sh_s"), st("gd_p"), st("gd_s"), st("cv_p"), st("cv_s"))
```

```python
import functools
import math

import jax
import jax.numpy as jnp
from jax import lax
from jax.experimental import pallas as pl
from jax.experimental.pallas import tpu as pltpu

F32 = jnp.float32
BF16 = jnp.bfloat16

NSA_HEADS = 8
NSA_KV_HEADS = 2
NSA_GROUP = 4
NSA_HD = 64
CMP_BLK = 64
SEL_BLK = 64
TOPK_BLK = 16
WINDOW = 512
FORCE_BONUS = 2.0 * NSA_GROUP
RWKV_HEADS = 8
RWKV_HD = 64
RWKV_W = 512
RWKV_GN_EPS = 64e-5
GDN_HEADS = 8
GDN_HD = 128
GDN_W = 1024
CONV_W = 4
N_GROUPS = 4
EXP_PER_GROUP = 8
N_EXPERTS = 32
EPS = 1e-6
NEG = -1e30

LANE = 128
GRP_LANE = 64
ROW_ALIGN = 16
SAMPLE_TILE_SLOTS = 8
SPAD = 8
VMEM_LIMIT = 56 * 1024 * 1024

ROW_TILE = 512
MOE_ROW_TILE = 1024
NSA_TQ = 128
NSA_TQ_SAMPLE = 32
NSA_TK = 512
CHUNK = 64
CMP_PAGES_PER_STEP = 64

NN = (((1,), (0,)), ((), ()))
NT = (((1,), (1,)), ((), ()))
TN = (((0,), (0,)), ((), ()))

E_Q, E_KV, E_G, E_RW = 0, 512, 1280, 1408
E_COLS = 1408 + 1920
RW_COLS = 1920
O_COLS = 3072 + 1024 + 128


def _mm(a, b, dims=NN):
    return lax.dot_general(a.astype(BF16), b.astype(BF16), dims, preferred_element_type=F32)


def _split(a):
    hi = a.astype(BF16)
    return hi, (a - hi.astype(F32)).astype(BF16)


def _mm3(a, b, dims=NN):
    ah, al = _split(a)
    bh, bl = _split(b)
    d = lambda x, y: lax.dot_general(x, y, dims, preferred_element_type=F32)
    return d(ah, bh) + (d(ah, bl) + d(al, bh))


def _split3(x):
    h1 = x.astype(BF16)
    r1 = x - h1.astype(F32)
    h2 = r1.astype(BF16)
    return h1, h2, (r1 - h2.astype(F32)).astype(BF16)


def _mm01(m01, x):
    m = m01.astype(BF16)
    parts = _split3(x)
    d = lambda y: lax.dot_general(m, y, NN, preferred_element_type=F32)
    return d(parts[0]) + (d(parts[1]) + d(parts[2]))


def _cumsum_rows(x):
    n = x.shape[0]
    row = lax.broadcasted_iota(jnp.int32, x.shape, 0)
    shift = 1
    while shift < n:
        x = x + jnp.where(row >= shift, pltpu.roll(x, shift, 0), 0.0)
        shift *= 2
    return x


def _shift_rows(x, prev8, sh):
    rolled = pltpu.roll(x, sh, 0)
    row8 = lax.broadcasted_iota(jnp.int32, (8, 1), 0)
    head = jnp.where(row8 < sh, pltpu.roll(prev8, sh, 0), rolled[0:8])
    return head if x.shape[0] == 8 else jnp.concatenate([head, rolled[8:]], axis=0)


def _head_sums(xs, seg_half):
    r = xs[0].shape[0]
    half = seg_half.shape[0]
    pieces = [p[:, h0:h0 + half] for x in xs for p in _split3(x) for h0 in (0, half)]
    out = lax.dot_general(jnp.concatenate(pieces, axis=0), seg_half.astype(BF16), NN, preferred_element_type=F32)
    res = []
    for i in range(len(xs)):
        o = [out[(6 * i + u) * r:(6 * i + u + 1) * r] for u in range(6)]
        res.append(jnp.concatenate([o[0] + (o[2] + o[4]), o[1] + (o[3] + o[5])], axis=1))
    return res


def _sigmoid(x):
    return 1.0 / (1.0 + jnp.exp(-x))


def _silu(x):
    return x * _sigmoid(x)


def _softplus(x):
    return jnp.maximum(x, 0.0) + jnp.log(1.0 + jnp.exp(-jnp.abs(x)))


def _cparams(sem):
    return pltpu.CompilerParams(dimension_semantics=sem, vmem_limit_bytes=VMEM_LIMIT)


def _norm_mod(x, nw, sc, sh):
    y = x * lax.rsqrt(jnp.mean(x * x, axis=-1, keepdims=True) + EPS)
    return (y * nw) * (1.0 + sc) + sh


def _mod_spec(rows_mod, tm, d):
    if rows_mod == 1:
        return pl.BlockSpec((1, d), lambda i: (0, 0))
    return pl.BlockSpec((tm, d), lambda i: (i, 0))


def _adaln_body(c_ref, w_ref, b_ref, o_ref):
    o_ref[0] = _mm3(_silu(c_ref[...]), w_ref[0]) + b_ref[0]


def adaln(c_all, w_ada, b_ada):
    depth, d, n6 = w_ada.shape
    rows = c_all.shape[0]
    tn = 768
    return pl.pallas_call(
        _adaln_body,
        grid=(depth, n6 // tn),
        in_specs=[pl.BlockSpec((rows, d), lambda l, j: (0, 0)),
                  pl.BlockSpec((1, d, tn), lambda l, j: (l, 0, j)),
                  pl.BlockSpec((1, 1, tn), lambda l, j: (l, 0, j))],
        out_specs=pl.BlockSpec((1, rows, tn), lambda l, j: (l, 0, j)),
        out_shape=jax.ShapeDtypeStruct((depth, rows, n6), F32),
        compiler_params=_cparams(("arbitrary", "arbitrary")),
        name="adaln",
    )(c_all, w_ada, b_ada.reshape(depth, 1, n6))


def _even_proj_body(x_ref, nw_ref, sc_ref, sh_ref, w_ref,
                    kv_ref, qt_ref, gt_ref, ks_ref, vst_ref, kw_ref, vwt_ref, rw_ref, hl_ref):
    h = _norm_mod(x_ref[...], nw_ref[...], sc_ref[...], sh_ref[...])
    hl = hl_ref.shape[0]
    hl_ref[...] = h[h.shape[0] - hl:, :]
    hb = h.astype(BF16)
    q = _mm(hb, w_ref[:, E_Q:E_Q + 512]) * (NSA_HD ** -0.5)
    qt_ref[...] = q.T.astype(BF16)
    kv = _mm(hb, w_ref[:, E_KV:E_KV + 768])
    kv_ref[...] = kv
    ks_ref[...] = kv[:, 256:384].astype(BF16)
    vst_ref[...] = kv[:, 384:512].T.astype(BF16)
    kw_ref[...] = kv[:, 512:640].astype(BF16)
    vwt_ref[...] = kv[:, 640:768].T.astype(BF16)
    g = _sigmoid(_mm(hb, w_ref[:, E_G:E_G + 128]))
    gt_ref[...] = g.T
    rw_ref[...] = _mm(hb, w_ref[:, E_RW:E_RW + RW_COLS])


def even_proj(x, nw, sc, sh, w_packed, tm, hl_rows):
    n, d = x.shape
    rows_mod = sc.shape[0]
    row = lambda c: pl.BlockSpec((tm, c), lambda i: (i, 0))
    col = lambda r: pl.BlockSpec((r, tm), lambda i: (0, i))
    return pl.pallas_call(
        _even_proj_body,
        grid=(n // tm,),
        in_specs=[row(d), pl.BlockSpec((1, d), lambda i: (0, 0)),
                  _mod_spec(rows_mod, tm, d), _mod_spec(rows_mod, tm, d),
                  pl.BlockSpec((d, E_COLS), lambda i: (0, 0))],
        out_specs=[row(768), col(512), col(128), row(128), col(128), row(128), col(128), row(RW_COLS),
                   pl.BlockSpec((hl_rows, d), lambda i: (0, 0))],
        out_shape=[jax.ShapeDtypeStruct((n, 768), F32),
                   jax.ShapeDtypeStruct((512, n), BF16),
                   jax.ShapeDtypeStruct((128, n), F32),
                   jax.ShapeDtypeStruct((n, 128), BF16),
                   jax.ShapeDtypeStruct((128, n), BF16),
                   jax.ShapeDtypeStruct((n, 128), BF16),
                   jax.ShapeDtypeStruct((128, n), BF16),
                   jax.ShapeDtypeStruct((n, RW_COLS), F32),
                   jax.ShapeDtypeStruct((hl_rows, d), F32)],
        compiler_params=_cparams(("arbitrary",)),
        name="even_proj",
    )(x, nw, sc, sh, w_packed)


def _pack_even_w(w_in):
    d = w_in.shape[0]
    z = lambda c: jnp.zeros((d, c), w_in.dtype)
    nsa = 1304
    rw = w_in[:, nsa:]
    parts = [w_in[:, :1280], w_in[:, 1280:1304], z(104),
             rw[:, :1536], rw[:, 1536:1600], z(64), rw[:, 1600:1664], z(64), rw[:, 1664:1792]]
    return jnp.concatenate(parts, axis=1).astype(BF16)


def _pack_rw_vec(v):
    z = jnp.zeros((64,), v.dtype)
    return jnp.concatenate([v[:1536], v[1536:1600], z, v[1600:1664], z, v[1664:1792]])[None, :]


def _mm_body(x_ref, w_ref, o_ref):
    o_ref[...] = _mm(x_ref[...], w_ref[...])


def small_matmul(x, w):
    return pl.pallas_call(
        _mm_body,
        out_shape=jax.ShapeDtypeStruct((x.shape[0], w.shape[1]), F32),
        compiler_params=pltpu.CompilerParams(vmem_limit_bytes=VMEM_LIMIT),
        name="small_matmul",
    )(x, w)


def _compress_body(x_ref, wts_ref, wc_ref, o_ref):
    x = x_ref[...]
    nb = x.shape[0] // CMP_BLK
    pooled = jnp.sum(x.reshape(nb, CMP_BLK, x.shape[-1]) * wts_ref[...][None], axis=1)
    o_ref[...] = _mm(pooled, wc_ref[...])


def _compress_paged_body(pt_ref, *refs, pps):
    page_refs = refs[:pps]
    wp_ref, wc_ref, o_ref = refs[pps:]
    x = jnp.concatenate([r[0] for r in page_refs], axis=1)
    pooled_t = jnp.concatenate([_mm(x[0:128], wp_ref[0]), _mm(x[128:256], wp_ref[1])], axis=0)
    nb = o_ref.shape[1]
    o_ref[0] = _mm(pooled_t.T[:nb], wc_ref[...])


def _cmp_weights(pos_wts, w_c):
    wts = jnp.repeat(pos_wts.T, 128, axis=1)
    eye2 = jnp.eye(2, dtype=w_c.dtype)
    blocks = [jnp.kron(eye2, w_c[c]) for c in range(2)]
    z = jnp.zeros((128, 128), w_c.dtype)
    wc = jnp.concatenate([jnp.concatenate([blocks[0], z], axis=1),
                          jnp.concatenate([z, blocks[1]], axis=1)], axis=0)
    return wts, wc


def compress_prompt(kv, wts, wc, tr):
    t = kv.shape[0]
    nb = tr // CMP_BLK
    return pl.pallas_call(
        _compress_body,
        grid=(t // tr,),
        in_specs=[pl.BlockSpec((tr, 256), lambda i: (i, 0)),
                  pl.BlockSpec((CMP_BLK, 256), lambda i: (0, 0)),
                  pl.BlockSpec((256, 256), lambda i: (0, 0))],
        out_specs=pl.BlockSpec((nb, 256), lambda i: (i, 0)),
        out_shape=jax.ShapeDtypeStruct((t // CMP_BLK, 256), F32),
        compiler_params=_cparams(("arbitrary",)),
        name="compress_prompt",
    )(kv, wts, wc)


def compress_paged(pool_t, page_table, pos_wts, wc, pages_per_step):
    b, n_pages = page_table.shape
    page = pool_t.shape[2]
    pps = pages_per_step
    nb = pps * page // CMP_BLK
    p_idx = jnp.arange(pps * page)
    wp = jax.nn.one_hot(p_idx // CMP_BLK, LANE, dtype=F32)[None] * pos_wts[:, p_idx % CMP_BLK][:, :, None]

    def page_spec(u):
        return pl.BlockSpec((1, 256, page), lambda bi, g, pt: (pt[bi, g * pps + u], 0, 0))

    grid_spec = pltpu.PrefetchScalarGridSpec(
        num_scalar_prefetch=1,
        grid=(b, n_pages // pps),
        in_specs=[page_spec(u) for u in range(pps)] + [
            pl.BlockSpec((2, pps * page, LANE), lambda bi, g, pt: (0, 0, 0)),
            pl.BlockSpec((256, 256), lambda bi, g, pt: (0, 0))],
        out_specs=pl.BlockSpec((1, nb, 256), lambda bi, g, pt: (bi, g, 0)),
    )
    return pl.pallas_call(
        functools.partial(_compress_paged_body, pps=pps),
        grid_spec=grid_spec,
        out_shape=jax.ShapeDtypeStruct((b, n_pages * page // CMP_BLK, 256), F32),
        compiler_params=_cparams(("arbitrary", "arbitrary")),
        name="compress_paged",
    )(page_table, *([pool_t] * pps), wp, wc)


def _gather_sel_body(pt_ref, tiles_ref, cnt_ref, *refs, pps, n_page_steps, nt):
    del pt_ref
    page_refs = refs[:pps]
    tail_ref, ks_ref, vst_ref = refs[pps:]
    bi = pl.program_id(0)
    a = pl.program_id(1)
    j = tiles_ref[bi * nt + jnp.minimum(a, cnt_ref[bi] - 1)]
    live = a < cnt_ref[bi]

    @pl.when(live & (j < n_page_steps))
    def _():
        ks_ref[0] = jnp.concatenate([r[0][0:128].T for r in page_refs], axis=0).astype(BF16)
        vst_ref[0] = jnp.concatenate([r[0][128:256] for r in page_refs], axis=1).astype(BF16)

    @pl.when(live & (j >= n_page_steps))
    def _():
        x = tail_ref[0]
        ks_ref[0] = x[:, :128].astype(BF16)
        vst_ref[0] = x[:, 128:].T.astype(BF16)


def gather_sel(pool_t, page_table, tail, tk, tiles, cnt, n_slots):
    b, n_pages = page_table.shape
    page = pool_t.shape[2]
    pps = tk // page
    n_page_steps = n_pages // pps
    nt = n_page_steps + 1
    nk = n_slots * tk

    def slot(bi, a, pt, tiles, cnt):
        return jnp.minimum(a, cnt[bi] - 1)

    def page_spec(u):
        def index(bi, a, pt, tiles, cnt):
            j = tiles[bi * nt + slot(bi, a, pt, tiles, cnt)]
            return (pt[bi, jnp.minimum(j * pps + u, n_pages - 1)], 0, 0)
        return pl.BlockSpec((1, 256, page), index)

    grid_spec = pltpu.PrefetchScalarGridSpec(
        num_scalar_prefetch=3,
        grid=(b, jnp.max(cnt)),
        in_specs=[page_spec(u) for u in range(pps)] + [pl.BlockSpec((1, tk, 256), lambda bi, a, *_: (bi, 0, 0))],
        out_specs=[pl.BlockSpec((1, tk, 128), lambda bi, a, *s: (bi, slot(bi, a, *s), 0)),
                   pl.BlockSpec((1, 128, tk), lambda bi, a, *s: (bi, 0, slot(bi, a, *s)))],
    )
    return pl.pallas_call(
        functools.partial(_gather_sel_body, pps=pps, n_page_steps=n_page_steps, nt=nt),
        grid_spec=grid_spec,
        out_shape=[jax.ShapeDtypeStruct((b, nk, 128), BF16), jax.ShapeDtypeStruct((b, 128, nk), BF16)],
        compiler_params=_cparams(("arbitrary", "arbitrary")),
        name="gather_sel",
    )(page_table, tiles, cnt, *([pool_t] * pps), tail)


MASKED = -1e30
M_INIT = -1e29


def _nsa_query(qt_ref, k, tq):
    w4 = NSA_GROUP * tq
    qb = qt_ref[0].astype(F32)
    qcat = jnp.concatenate([qb[g * 64:(g + 1) * 64] for g in range(NSA_GROUP)], axis=1)
    q2 = jnp.concatenate([qcat, qcat], axis=0)
    row = lax.broadcasted_iota(jnp.int32, (128, w4), 0)
    qe = jnp.where(row // 64 == k, q2, 0.0)
    gidx = lax.broadcasted_iota(jnp.int32, (128, w4), 1) // tq
    base = jnp.where(k == 0, 0.5, 0.5 / 16.0)
    slope = base * jnp.where(gidx == 0, 1.0, jnp.where(gidx == 1, 0.5, jnp.where(gidx == 2, 0.25, 0.125)))
    mult = jnp.where(row == 0, 16.0, jnp.where(row == 1, 1.0, jnp.where(row == 2, 128.0,
                                                                         jnp.where(row == 3, 64.0, 0.0))))
    return jnp.concatenate([qe, slope * mult], axis=0).astype(BF16)


def _pos_features(rows, tile_rel):
    r = lax.broadcasted_iota(jnp.int32, (rows, LANE), 0)
    lane = lax.broadcasted_iota(jnp.int32, (rows, LANE), 1)
    ab = jnp.where(lane == 0, r // 16, jnp.where(lane == 1, r % 16, 0)).astype(F32)
    return jnp.where(lane == 2, tile_rel, ab).astype(BF16)


def _gate_rows(gb, j, tq):
    return jnp.concatenate([gb[g * 3 + j:g * 3 + j + 1, :] for g in range(NSA_GROUP)], axis=1)


def _nsa_select_body(qt_ref, g_ref, kvc_ref, kvct_ref, kw_ref, vwt_ref, part_ref, sel_ref, flag_ref, *,
                     tq, tk, wk, nbc, nb, pos0_fn, wstart_fn, wpos0_fn):
    i = pl.program_id(1)
    k = pl.program_id(2)
    w4 = NSA_GROUP * tq
    pos0 = pos0_fn(i)
    qa = _nsa_query(qt_ref, k, tq)
    pos_q = pos0 + lax.broadcasted_iota(jnp.int32, (1, w4), 1) % tq

    def softmax_cols(s, bad):
        s = jnp.where(bad, MASKED, s)
        m = jnp.maximum(jnp.max(s, axis=0, keepdims=True), M_INIT)
        e = jnp.exp(s - m)
        return e / jnp.maximum(jnp.sum(e, axis=0, keepdims=True), 1e-30)

    n_i = lax.broadcasted_iota(jnp.int32, (nbc, LANE), 0)
    lane_c = lax.broadcasted_iota(jnp.int32, (nbc, LANE), 1)
    feat_c = jnp.where(lane_c == 3, n_i - pos0 // CMP_BLK, 0).astype(F32).astype(BF16)
    kc = jnp.concatenate([kvc_ref[0][:, :128].astype(BF16), feat_c], axis=1)
    c_end = lax.broadcasted_iota(jnp.int32, (nbc, 1), 0) * CMP_BLK + (CMP_BLK - 1)
    p_c = softmax_cols(lax.dot_general(kc, qa, NN, preferred_element_type=F32), c_end > pos_q)
    vct = kvct_ref[0, pl.ds(pl.multiple_of(128 + k * 64, 64), 64), :]
    o_c = _mm(vct, p_c)

    imp = p_c[:, 0:tq]
    for g in range(1, NSA_GROUP):
        imp = imp + p_c[:, g * tq:(g + 1) * tq]
    if nb > nbc:
        imp = jnp.concatenate([imp, jnp.zeros((nb - nbc, tq), F32)], axis=0)
    blk = lax.broadcasted_iota(jnp.int32, (nb, tq), 0)
    cur = (pos0 + lax.broadcasted_iota(jnp.int32, (1, tq), 1)) // SEL_BLK
    forced = (blk == cur) | (blk == cur - 1) | (blk == 0)
    score = jnp.where(blk <= cur, imp + jnp.where(forced, FORCE_BONUS, 0.0), -1.0)
    for _ in range(min(TOPK_BLK, nb)):
        m = jnp.max(score, axis=0, keepdims=True)
        first = jnp.min(jnp.where(score == m, blk, nb), axis=0, keepdims=True)
        score = jnp.where(blk == first, -2.0, score)
    sel = jnp.where(score == -2.0, 1.0, 0.0)
    sel_ref[0, 0] = sel
    bpt = tk // SEL_BLK
    any_row = jnp.max(sel, axis=1, keepdims=True)
    flag_ref[0, 0] = jnp.max(any_row.reshape(nb // bpt, bpt, 1), axis=1)

    wstart = wstart_fn(i)
    if not isinstance(wstart, int):
        wstart = pl.multiple_of(wstart, 128)
    wpos0 = wpos0_fn(i)
    tile_rel = jnp.asarray((wpos0 - pos0) // 128, F32)
    kw = jnp.concatenate([kw_ref[0, pl.ds(wstart, wk), :], _pos_features(wk, tile_rel)], axis=1)
    dist_w = pos_q - (wpos0 + lax.broadcasted_iota(jnp.int32, (wk, 1), 0))
    p_w = softmax_cols(lax.dot_general(kw, qa, NN, preferred_element_type=F32), (dist_w < 0) | (dist_w >= WINDOW))
    vwin = vwt_ref[0, pl.ds(pl.multiple_of(k * 64, 64), 64), pl.ds(wstart, wk)]
    o_w = _mm(vwin, p_w)

    gb = g_ref[0, 0]
    part_ref[0, 0] = _gate_rows(gb, 0, tq) * o_c + _gate_rows(gb, 2, tq) * o_w


def nsa_select(qt, gates, kvc, kvct, kw, vwt, *, nb, tq, tk, wk, pos0_fn, wstart_fn, wpos0_fn):
    b, _, nq = qt.shape
    nbc = kvc.shape[1]
    nw = kw.shape[1]
    nqt = nq // tq
    nt = nb * SEL_BLK // tk
    w4 = NSA_GROUP * tq
    assert nbc <= 256 and tk <= 512 and wk <= 1024
    body = functools.partial(_nsa_select_body, tq=tq, tk=tk, wk=wk, nbc=nbc, nb=nb, pos0_fn=pos0_fn,
                             wstart_fn=wstart_fn, wpos0_fn=wpos0_fn)
    full = lambda s1, s2: pl.BlockSpec((1, s1, s2), lambda bi, i, k: (bi, 0, 0))
    step = lambda s1, s2: pl.BlockSpec((1, 1, s1, s2), lambda bi, i, k: (bi, i * NSA_KV_HEADS + k, 0, 0))
    return pl.pallas_call(
        body,
        grid=(b, nqt, NSA_KV_HEADS),
        in_specs=[pl.BlockSpec((1, 256, tq), lambda bi, i, k: (bi, k, i)),
                  pl.BlockSpec((1, 1, 16, tq), lambda bi, i, k: (bi, k, 0, i)),
                  full(nbc, 256), full(256, nbc), full(nw, 128), full(128, nw)],
        out_specs=[step(64, w4), step(nb, tq), step(nt, 1)],
        out_shape=[jax.ShapeDtypeStruct((b, nqt * 2, 64, w4), F32),
                   jax.ShapeDtypeStruct((b, nqt * 2, nb, tq), F32),
                   jax.ShapeDtypeStruct((b, nqt * 2, nt, 1), F32)],
        compiler_params=_cparams(("arbitrary", "arbitrary", "arbitrary")),
        name="nsa_select",
    )(qt, gates, kvc, kvct, kw, vwt)


def _nsa_selected_body(list_ref, slot_ref, cnt_ref, qt_ref, g_ref, sel_ref, ks_ref, vst_ref, part_ref, o_ref, *,
                       tq, tk, nt, pos0_fn):
    bi = pl.program_id(0)
    i = pl.program_id(1)
    k = pl.program_id(2)
    step = (bi * pl.num_programs(1) + i) * NSA_KV_HEADS + k
    w4 = NSA_GROUP * tq
    pos0 = pos0_fn(i)
    qa = _nsa_query(qt_ref, k, tq)
    pos_q = pos0 + lax.broadcasted_iota(jnp.int32, (1, w4), 1) % tq
    bpt = tk // SEL_BLK
    row_k = lax.broadcasted_iota(jnp.int32, (tk, 1), 0)
    r = lax.broadcasted_iota(jnp.int32, (tk, LANE), 0)
    lane = lax.broadcasted_iota(jnp.int32, (tk, LANE), 1)
    feat_ab = jnp.where(lane == 0, r // 16, jnp.where(lane == 1, r % 16, 0)).astype(F32)

    n_act = cnt_ref[step]

    def tile_scores(jj, live):
        j = list_ref[step * nt + jj]
        off = pl.multiple_of(j * tk, tk)
        buf = pl.multiple_of(slot_ref[step * nt + jj] * tk, tk)
        tile_rel = ((off - pos0) // 128).astype(F32)
        feat = jnp.where(lane == 2, tile_rel, feat_ab).astype(BF16)
        kj = jnp.concatenate([ks_ref[0, pl.ds(buf, tk), :], feat], axis=1)
        s = lax.dot_general(kj, qa, NN, preferred_element_type=F32)
        selb = (sel_ref[0, 0, pl.ds(pl.multiple_of(j * bpt, bpt), bpt), :] - 1.0) * (-MASKED)
        selb = jnp.concatenate([selb] * NSA_GROUP, axis=1) + jnp.where(live, 0.0, MASKED)
        s = s + jnp.broadcast_to(selb[:, None, :], (bpt, SEL_BLK, w4)).reshape(tk, w4)
        s = jnp.where(row_k > pos_q - off, MASKED, s)
        return s, vst_ref[0, pl.ds(pl.multiple_of(k * 64, 64), 64), pl.ds(buf, tk)]

    def kv_pair(pp, carry):
        m_i, l_i, acc = carry
        second = 2 * pp + 1
        s_a, v_a = tile_scores(2 * pp, True)
        s_b, v_b = tile_scores(jnp.minimum(second, n_act - 1), second < n_act)
        m_new = jnp.maximum(m_i, jnp.maximum(jnp.max(s_a, axis=0, keepdims=True), jnp.max(s_b, axis=0, keepdims=True)))
        p_a = jnp.exp(s_a - m_new)
        p_b = jnp.exp(s_b - m_new)
        alpha = jnp.exp(m_i - m_new)
        l_new = l_i * alpha + (jnp.sum(p_a, axis=0, keepdims=True) + jnp.sum(p_b, axis=0, keepdims=True))
        return m_new, l_new, acc * alpha + (_mm(v_a, p_a) + _mm(v_b, p_b))

    init = (jnp.full((1, w4), M_INIT, F32), jnp.zeros((1, w4), F32), jnp.zeros((64, w4), F32))
    _, l_s, acc_s = lax.fori_loop(0, (n_act + 1) // 2, kv_pair, init)
    o_s = acc_s / jnp.maximum(l_s, 1e-30)
    o_t = part_ref[0, 0] + _gate_rows(g_ref[0, 0], 1, tq) * o_s
    o_ref[0] = jnp.concatenate([o_t[:, g * tq:(g + 1) * tq].T for g in range(NSA_GROUP)], axis=1)


def nsa_selected(tile_list, slot_list, tile_cnt, qt, gates, sel, ks, vst, part, *, tq, tk, pos0_fn):
    b, _, nq = qt.shape
    nk = ks.shape[1]
    nb = sel.shape[2]
    nt = nb * SEL_BLK // tk
    w4 = NSA_GROUP * tq
    full = lambda s1, s2: pl.BlockSpec((1, s1, s2), lambda bi, i, k, *_: (bi, 0, 0))
    step = lambda s1, s2: pl.BlockSpec((1, 1, s1, s2), lambda bi, i, k, *_: (bi, i * NSA_KV_HEADS + k, 0, 0))
    grid_spec = pltpu.PrefetchScalarGridSpec(
        num_scalar_prefetch=3,
        grid=(b, nq // tq, NSA_KV_HEADS),
        in_specs=[pl.BlockSpec((1, 256, tq), lambda bi, i, k, *_: (bi, k, i)),
                  pl.BlockSpec((1, 1, 16, tq), lambda bi, i, k, *_: (bi, k, 0, i)),
                  step(nb, tq), full(nk, 128), full(128, nk), step(64, w4)],
        out_specs=pl.BlockSpec((1, tq, 256), lambda bi, i, k, *_: (bi, i, k)),
    )
    return pl.pallas_call(
        functools.partial(_nsa_selected_body, tq=tq, tk=tk, nt=nt, pos0_fn=pos0_fn),
        grid_spec=grid_spec,
        out_shape=jax.ShapeDtypeStruct((b, nq, 512), F32),
        compiler_params=_cparams(("arbitrary", "arbitrary", "arbitrary")),
        name="nsa_selected",
    )(tile_list, slot_list, tile_cnt, qt, gates, sel, ks, vst, part)


def _active_first(active):
    order = jnp.argsort(jnp.where(active, 0, 1), axis=-1, stable=True).astype(jnp.int32)
    return order, jnp.sum(active, axis=-1).astype(jnp.int32)


def nsa_attention(qt, gates, kvc, kvct, ks, vst, kw, vwt, *, tq, tk, wk, pos0_fn, wstart_fn, wpos0_fn):
    nb = ks.shape[1] // SEL_BLK
    part, sel, flags = nsa_select(qt, gates, kvc, kvct, kw, vwt, nb=nb, tq=tq, tk=tk, wk=wk, pos0_fn=pos0_fn,
                                  wstart_fn=wstart_fn, wpos0_fn=wpos0_fn)
    order, cnt = _active_first(flags[..., 0] > 0.5)
    return nsa_selected(order.reshape(-1), order.reshape(-1), cnt.reshape(-1), qt, gates, sel, ks, vst, part,
                        tq=tq, tk=tk, pos0_fn=pos0_fn)


def nsa_attention_paged(qt, gates, kvc, kvct, pool_t, page_table, tail, kw, vwt, *, tq, tk, wk, pos0_fn, wstart_fn,
                        wpos0_fn):
    nb = (page_table.shape[1] * pool_t.shape[2] + tk) // SEL_BLK
    part, sel, flags = nsa_select(qt, gates, kvc, kvct, kw, vwt, nb=nb, tq=tq, tk=tk, wk=wk, pos0_fn=pos0_fn,
                                  wstart_fn=wstart_fn, wpos0_fn=wpos0_fn)
    active = flags[..., 0] > 0.5
    tiles_b, cnt_b = _active_first(jnp.any(active, axis=1))
    slot_of_tile = jnp.argsort(tiles_b, axis=-1).astype(jnp.int32)
    order, cnt = _active_first(active)
    slots = jnp.take_along_axis(jnp.broadcast_to(slot_of_tile[:, None, :], order.shape), order, axis=-1)

    def run(n_slots):
        ks, vst = gather_sel(pool_t, page_table, tail, tk, tiles_b.reshape(-1), cnt_b, n_slots)
        return nsa_selected(order.reshape(-1), slots.reshape(-1), cnt.reshape(-1), qt, gates, sel, ks, vst, part,
                            tq=tq, tk=tk, pos0_fn=pos0_fn)

    nt = tiles_b.shape[-1]
    few = min(SAMPLE_TILE_SLOTS, nt)
    return lax.cond(jnp.max(cnt_b) <= few, lambda: run(few), lambda: run(nt))


def _tri_inverse(ms, c):
    eye = (lax.broadcasted_iota(jnp.int32, (c, c), 0) == lax.broadcasted_iota(jnp.int32, (c, c), 1)).astype(F32)
    ps = [-m for m in ms]
    ts = [eye + p for p in ps]
    steps = max(int(math.ceil(math.log2(c))) - 1, 0)
    d = lambda x, y: lax.dot_general(x, y, NN, preferred_element_type=F32)
    for _ in range(steps):
        sp = [_split(p) for p in ps]
        ps = [d(ph, ph) + (d(ph, pl_) + d(pl_, ph)) for ph, pl_ in sp]
        sp = [_split(p) for p in ps]
        st = [_split(t) for t in ts]
        ts = [t + (d(th, ph) + (d(th, pl_) + d(tl, ph))) for t, (th, tl), (ph, pl_) in zip(ts, st, sp)]
    return ts


def _rwkv_body(rw_ref, rw0_ref, s0_ref, mu_ref, vec_ref, w2_ref, a2_ref, g2_ref, seg_ref, rk_ref,
               o_ref, sfin_ref, buf_ref, s_ref, y_ref, *, c, valid, n_chunks):
    ci = pl.program_id(1)
    halo = 8

    @pl.when(ci == 0)
    def _():
        buf_ref[...] = rw0_ref[0]
        s_ref[...] = s0_ref[0]

    cur = rw_ref[...]
    prev = _shift_rows(cur, buf_ref[...], 1)
    xr = cur + (prev - cur) * mu_ref[...]
    buf_ref[...] = cur[c - halo:, :]

    vec = vec_ref[...]
    w0, a0, kkw, kaw, ln_w, ln_b = (vec[r:r + 1, :] for r in range(6))
    r = xr[:, 0:512]
    kx = xr[:, 512:1024]
    v = xr[:, 1024:1536]
    xw = xr[:, 1536:1664]
    xa = xr[:, 1664:1792]
    xg = xr[:, 1792:1920]
    wl = -jnp.exp(-_softplus(-(w0 + _mm(jnp.tanh(xw), w2_ref[...]))) - 0.5)
    a = _sigmoid(a0 + _mm(xa, a2_ref[...]))
    gate = _mm(_sigmoid(xg), g2_ref[...])
    seg = seg_ref[...]
    zk = kx * kkw
    k2 = kx * (1.0 + (a - 1.0) * kaw)
    zz_sum, rk_sum = _head_sums([zk * zk, r * k2 * rk_ref[...]], seg)
    kk = zk * lax.rsqrt(zz_sum + EPS)
    bonus = rk_sum * v
    if valid < c:
        live = lax.broadcasted_iota(jnp.int32, (c, 1), 0) < valid
        wl = jnp.where(live, wl, 0.0)
        kk = jnp.where(live, kk, 0.0)
        k2 = jnp.where(live, k2, 0.0)
        v = jnp.where(live, v, 0.0)
        r = jnp.where(live, r, 0.0)
    bb = kk * a

    ri = lax.broadcasted_iota(jnp.int32, (c, c), 0)
    cj = lax.broadcasted_iota(jnp.int32, (c, c), 1)
    tril = ri >= cj
    strict = ri > cj
    cw = _cumsum_rows(wl)
    ecw = jnp.exp(cw)
    einv = jnp.exp(-cw)
    p_c = ecw[c - 1:c, :]
    kt = kk * jnp.exp(cw - wl)
    bt = bb * einv
    ki = k2 * einv
    rt = r * ecw
    bd = bt * p_c
    kd = ki * p_c

    heads = range(RWKV_HEADS)
    sls = [slice(h * RWKV_HD, (h + 1) * RWKV_HD) for h in heads]
    kt_h = [kt[:, sl] for sl in sls]
    bt_h = [bt[:, sl] for sl in sls]
    ki_h = [ki[:, sl] for sl in sls]
    rt_h = [rt[:, sl] for sl in sls]
    v_h = [v[:, sl] for sl in sls]
    l_m = [jnp.where(strict, _mm3(kt_h[h], bt_h[h], NT), 0.0) for h in heads]
    m_kk = [jnp.where(strict, _mm(kt_h[h], ki_h[h], NT), 0.0) for h in heads]
    a_rb = [jnp.where(tril, _mm(rt_h[h], bt_h[h], NT), 0.0) for h in heads]
    a_rk = [jnp.where(tril, _mm(rt_h[h], ki_h[h], NT), 0.0) for h in heads]
    mv = [_mm(m_kk[h], v_h[h]) for h in heads]
    y0 = [_mm(a_rk[h], v_h[h]) for h in heads]
    t_inv = _tri_inverse(l_m, c)
    w_h = [_mm3(t_inv[h], kt_h[h]) for h in heads]
    u_h = [-_mm3(t_inv[h], mv[h]) for h in heads]
    s_h = [s_ref[h] for h in heads]
    e_h = [u_h[h] - _mm(w_h[h], s_h[h], NT) for h in heads]
    y1 = [_mm(rt_h[h], s_h[h], NT) + y0[h] for h in heads]
    y_h = [y1[h] + _mm(a_rb[h], e_h[h]) for h in heads]
    ds = [_mm(e_h[h], bd[:, sls[h]], TN) + _mm(v_h[h], kd[:, sls[h]], TN) for h in heads]
    for h in heads:
        s_ref[h] = s_h[h] * p_c[:, sls[h]] + ds[h]
        mu_h = jnp.mean(y_h[h], axis=-1, keepdims=True)
        d_h = y_h[h] - mu_h
        var_h = jnp.mean(d_h * d_h, axis=-1, keepdims=True)
        y_ref[:, sls[h]] = d_h * lax.rsqrt(var_h + RWKV_GN_EPS)

    o_ref[...] = (y_ref[...] * ln_w + ln_b + bonus) * gate

    @pl.when(ci == n_chunks - 1)
    def _():
        sfin_ref[0] = s_ref[...]


def rwkv_mix(rw, rw0, s0, mu, vec, w2, a2, g2, seg, rk, *, c, valid):
    b = s0.shape[0]
    rows = rw.shape[0]
    n_chunks = rows // (b * c)
    const = lambda s: pl.BlockSpec(s, lambda bi, ci: tuple(0 for _ in s))
    return pl.pallas_call(
        functools.partial(_rwkv_body, c=c, valid=valid, n_chunks=n_chunks),
        grid=(b, n_chunks),
        in_specs=[pl.BlockSpec((c, RW_COLS), lambda bi, ci: (bi * n_chunks + ci, 0)),
                  pl.BlockSpec((1, 8, RW_COLS), lambda bi, ci: (bi, 0, 0)),
                  pl.BlockSpec((1, RWKV_HEADS, 64, 64), lambda bi, ci: (bi, 0, 0, 0)),
                  const((1, RW_COLS)), const((8, 512)), const((128, 512)), const((128, 512)), const((128, 512)),
                  const((RWKV_W // 2, RWKV_W // 2)), const((1, 512))],
        out_specs=[pl.BlockSpec((c, 512), lambda bi, ci: (bi * n_chunks + ci, 0)),
                   pl.BlockSpec((1, RWKV_HEADS, 64, 64), lambda bi, ci: (bi, 0, 0, 0))],
        out_shape=[jax.ShapeDtypeStruct((rows, 512), F32),
                   jax.ShapeDtypeStruct((b, RWKV_HEADS, 64, 64), F32)],
        scratch_shapes=[pltpu.VMEM((8, RW_COLS), F32), pltpu.VMEM((RWKV_HEADS, 64, 64), F32),
                        pltpu.VMEM((c, 512), F32)],
        compiler_params=_cparams(("arbitrary", "arbitrary")),
        name="rwkv_mix",
    )(rw, rw0, s0, mu, vec, w2, a2, g2, seg, rk)


def _out_proj_body(*refs, n_in):
    a_refs = refs[:n_in]
    w_refs = refs[n_in:2 * n_in]
    x_ref, g_ref, o_ref = refs[2 * n_in:]
    y = _mm(a_refs[0][...], w_refs[0][...])
    for a_ref, w_ref in zip(a_refs[1:], w_refs[1:]):
        y = y + _mm(a_ref[...], w_ref[...])
    o_ref[...] = x_ref[...] + g_ref[...] * y


def out_proj(acts, weights, x, gate, tm):
    n, d = x.shape
    n_in = len(acts)
    return pl.pallas_call(
        functools.partial(_out_proj_body, n_in=n_in),
        grid=(n // tm,),
        in_specs=[pl.BlockSpec((tm, a.shape[1]), lambda i: (i, 0)) for a in acts]
        + [pl.BlockSpec(w.shape, lambda i: (0, 0)) for w in weights]
        + [pl.BlockSpec((tm, d), lambda i: (i, 0)), _mod_spec(gate.shape[0], tm, d)],
        out_specs=pl.BlockSpec((tm, d), lambda i: (i, 0)),
        out_shape=jax.ShapeDtypeStruct((n, d), F32),
        compiler_params=_cparams(("arbitrary",)),
        name="out_proj",
    )(*acts, *weights, x, gate)


def _odd_proj_body(x_ref, nw_ref, sc_ref, sh_ref, w_ref, qkv_ref, z_ref, ba_ref):
    hb = _norm_mod(x_ref[...], nw_ref[...], sc_ref[...], sh_ref[...]).astype(BF16)
    qkv_ref[...] = _mm(hb, w_ref[:, 0:3072])
    z_ref[...] = _mm(hb, w_ref[:, 3072:4096])
    ba_ref[...] = _mm(hb, w_ref[:, 4096:O_COLS])


def odd_proj(x, nw, sc, sh, w_packed, tm):
    n, d = x.shape
    rows_mod = sc.shape[0]
    row = lambda c: pl.BlockSpec((tm, c), lambda i: (i, 0))
    return pl.pallas_call(
        _odd_proj_body,
        grid=(n // tm,),
        in_specs=[row(d), pl.BlockSpec((1, d), lambda i: (0, 0)),
                  _mod_spec(rows_mod, tm, d), _mod_spec(rows_mod, tm, d),
                  pl.BlockSpec((d, O_COLS), lambda i: (0, 0))],
        out_specs=[row(3072), row(1024), row(128)],
        out_shape=[jax.ShapeDtypeStruct((n, 3072), F32), jax.ShapeDtypeStruct((n, 1024), F32),
                   jax.ShapeDtypeStruct((n, 128), F32)],
        compiler_params=_cparams(("arbitrary",)),
        name="odd_proj",
    )(x, nw, sc, sh, w_packed)


def _gdn_body(qkv_ref, z_ref, ba_ref, cs_ref, s0_ref, cw_ref, hp_ref, nw_ref,
              o_ref, sfin_ref, buf_ref, s_ref, *, c, valid, n_chunks):
    ci = pl.program_id(1)
    halo = 8

    @pl.when(ci == 0)
    def _():
        buf_ref[...] = cs_ref[0]
        s_ref[...] = s0_ref[0]

    x = qkv_ref[...]
    prev8 = buf_ref[...]
    cw = cw_ref[...]
    conv = x * cw[CONV_W - 1:CONV_W, :]
    for j in range(CONV_W - 1):
        conv = conv + _shift_rows(x, prev8, CONV_W - 1 - j) * cw[j:j + 1, :]
    buf_ref[...] = x[c - halo:, :]
    conv = _silu(conv)

    hp = hp_ref[...]
    ba = ba_ref[...]
    beta_f = _sigmoid(ba)
    g_f = hp[0:1, :] * _softplus(ba + hp[1:2, :])
    if valid < c:
        live = lax.broadcasted_iota(jnp.int32, (c, 1), 0) < valid
        beta_f = jnp.where(live, beta_f, 0.0)
        g_f = jnp.where(live, g_f, 0.0)
        conv = jnp.where(live, conv, 0.0)

    ri = lax.broadcasted_iota(jnp.int32, (c, c), 0)
    cj = lax.broadcasted_iota(jnp.int32, (c, c), 1)
    tril = ri >= cj
    strict = ri > cj
    gc = _cumsum_rows(g_f)
    gct = gc.T
    z = z_ref[...]
    nw = nw_ref[...]

    heads = range(GDN_HEADS)
    sls = [slice(h * GDN_HD, (h + 1) * GDN_HD) for h in heads]
    q_h = [conv[:, sl] for sl in sls]
    k_h = [conv[:, GDN_W + h * GDN_HD:GDN_W + (h + 1) * GDN_HD] for h in heads]
    v_h = [conv[:, 2 * GDN_W + h * GDN_HD:2 * GDN_W + (h + 1) * GDN_HD] for h in heads]
    q_h = [q * lax.rsqrt(jnp.sum(q * q, axis=-1, keepdims=True) + EPS) * (GDN_HD ** -0.5) for q in q_h]
    k_h = [k * lax.rsqrt(jnp.sum(k * k, axis=-1, keepdims=True) + EPS) for k in k_h]
    g_col = [gc[:, 8 + h:9 + h] for h in heads]
    eg = [jnp.exp(g) for g in g_col]
    b_col = [beta_f[:, h:h + 1] for h in heads]
    decay = [jnp.where(tril, jnp.exp(jnp.where(tril, g_col[h] - gct[8 + h:9 + h, :], 0.0)), 0.0) for h in heads]
    kb = [k_h[h] * b_col[h] for h in heads]
    vb = [v_h[h] * b_col[h] for h in heads]
    m_h = [jnp.where(strict, _mm3(kb[h], k_h[h], NT) * decay[h], 0.0) for h in heads]
    qk = [jnp.where(tril, _mm(q_h[h], k_h[h], NT) * decay[h], 0.0) for h in heads]
    t_inv = _tri_inverse(m_h, c)
    u_h = [_mm(t_inv[h], vb[h]) for h in heads]
    w_h = [_mm(t_inv[h], kb[h] * eg[h]) for h in heads]
    s_h = [s_ref[h] for h in heads]
    v_new = [u_h[h] - _mm(w_h[h], s_h[h]) for h in heads]
    o1 = [_mm(q_h[h] * eg[h], s_h[h]) for h in heads]
    o_h = [o1[h] + _mm(qk[h], v_new[h]) for h in heads]
    g_last = [g[c - 1:c, :] for g in g_col]
    ds = [_mm(k_h[h] * jnp.exp(g_last[h] - g_col[h]), v_new[h], TN) for h in heads]
    for h in heads:
        s_ref[h] = s_h[h] * jnp.exp(g_last[h]) + ds[h]
        o_n = o_h[h] * lax.rsqrt(jnp.mean(o_h[h] * o_h[h], axis=-1, keepdims=True) + EPS) * nw
        o_ref[:, sls[h]] = o_n * _silu(z[:, sls[h]])

    @pl.when(ci == n_chunks - 1)
    def _():
        sfin_ref[0] = s_ref[...]


def gdn_mix(qkv, z, ba, cs, s0, conv_w8, hp, nw, *, c, valid):
    b = s0.shape[0]
    rows = qkv.shape[0]
    n_chunks = rows // (b * c)
    const = lambda s: pl.BlockSpec(s, lambda bi, ci: tuple(0 for _ in s))
    row = lambda w: pl.BlockSpec((c, w), lambda bi, ci: (bi * n_chunks + ci, 0))
    return pl.pallas_call(
        functools.partial(_gdn_body, c=c, valid=valid, n_chunks=n_chunks),
        grid=(b, n_chunks),
        in_specs=[row(3072), row(1024), row(128),
                  pl.BlockSpec((1, 8, 3072), lambda bi, ci: (bi, 0, 0)),
                  pl.BlockSpec((1, GDN_HEADS, 128, 128), lambda bi, ci: (bi, 0, 0, 0)),
                  const((8, 3072)), const((8, 128)), const((1, 128))],
        out_specs=[row(1024), pl.BlockSpec((1, GDN_HEADS, 128, 128), lambda bi, ci: (bi, 0, 0, 0))],
        out_shape=[jax.ShapeDtypeStruct((rows, 1024), F32),
                   jax.ShapeDtypeStruct((b, GDN_HEADS, 128, 128), F32)],
        scratch_shapes=[pltpu.VMEM((8, 3072), F32), pltpu.VMEM((GDN_HEADS, 128, 128), F32)],
        compiler_params=_cparams(("arbitrary", "arbitrary")),
        name="gdn_mix",
    )(qkv, z, ba, cs, s0, conv_w8, hp, nw)


def _router_body(x_ref, nw_ref, sc_ref, sh_ref, wr_ref, br_ref, h_ref, gate_ref):
    h = _norm_mod(x_ref[...], nw_ref[...], sc_ref[...], sh_ref[...])
    h_ref[...] = h.astype(BF16)
    logits = _mm3(h, wr_ref[...]) + br_ref[...]
    tm = logits.shape[0]
    lane = lax.broadcasted_iota(jnp.int32, (tm, LANE), 1)
    is_grp = (lane >= N_EXPERTS) & (lane < N_EXPERTS + N_GROUPS)
    gl = jnp.where(is_grp, logits, NEG)
    gmax = jnp.max(gl, axis=-1, keepdims=True)
    g_idx = jnp.min(jnp.where(gl == gmax, lane, 4 * LANE), axis=-1, keepdims=True) - N_EXPERTS
    g_w = 1.0 / jnp.sum(jnp.where(is_grp, jnp.exp(gl - gmax), 0.0), axis=-1, keepdims=True)
    in_grp = (lane < N_EXPERTS) & (lane // EXP_PER_GROUP == g_idx)
    el = jnp.where(in_grp, logits, NEG)
    emax = jnp.max(el, axis=-1, keepdims=True)
    e = jnp.where(in_grp, jnp.exp(el - emax), 0.0)
    p = e / jnp.sum(e, axis=-1, keepdims=True)
    p1 = jnp.where(in_grp, p, -1.0)
    m1 = jnp.max(p1, axis=-1, keepdims=True)
    i1 = jnp.min(jnp.where(p1 == m1, lane, 4 * LANE), axis=-1, keepdims=True)
    p2 = jnp.where(lane == i1, -1.0, p1)
    m2 = jnp.max(p2, axis=-1, keepdims=True)
    i2 = jnp.min(jnp.where(p2 == m2, lane, 4 * LANE), axis=-1, keepdims=True)
    tot = m1 + m2
    gate = jnp.where(lane == i1, m1 / tot * g_w, jnp.where(lane == i2, m2 / tot * g_w, 0.0))
    gate_ref[...] = jnp.where(lane == GRP_LANE, g_idx.astype(F32), gate)


def moe_router(x, nw, sc, sh, w_r, b_r, tm):
    n, d = x.shape
    rows_mod = sc.shape[0]
    return pl.pallas_call(
        _router_body,
        grid=(n // tm,),
        in_specs=[pl.BlockSpec((tm, d), lambda i: (i, 0)), pl.BlockSpec((1, d), lambda i: (0, 0)),
                  _mod_spec(rows_mod, tm, d), _mod_spec(rows_mod, tm, d),
                  pl.BlockSpec((d, LANE), lambda i: (0, 0)), pl.BlockSpec((1, LANE), lambda i: (0, 0))],
        out_specs=[pl.BlockSpec((tm, d), lambda i: (i, 0)), pl.BlockSpec((tm, LANE), lambda i: (i, 0))],
        out_shape=[jax.ShapeDtypeStruct((n, d), BF16), jax.ShapeDtypeStruct((n, LANE), F32)],
        compiler_params=_cparams(("arbitrary",)),
        name="moe_router",
    )(x, nw, sc, sh, w_r, b_r)


def _moe_body(h_ref, gate_ref, w1_ref, w3_ref, w2_ref, x_ref, g2_ref, o_ref, acc_ref):
    e = pl.program_id(1)

    @pl.when(e == 0)
    def _():
        acc_ref[...] = jnp.zeros_like(acc_ref)

    hb = h_ref[...]
    he = _silu(_mm(hb, w1_ref[0])) * _mm(hb, w3_ref[0])
    y = _mm(he, w2_ref[0])
    gate = gate_ref[...]
    lane = lax.broadcasted_iota(jnp.int32, gate.shape, 1)
    ge = jnp.sum(jnp.where(lane == e, gate, 0.0), axis=-1, keepdims=True)
    acc_ref[...] += ge * y

    @pl.when(e == pl.num_programs(1) - 1)
    def _():
        o_ref[...] = x_ref[...] + g2_ref[...] * acc_ref[...]


def moe_ffn(h, gate, w1, w3, w2, e0, x, g2, tm):
    n, d = x.shape
    de = w1.shape[2]
    return pl.pallas_call(
        _moe_body,
        grid=(n // tm, N_EXPERTS),
        in_specs=[pl.BlockSpec((tm, d), lambda i, e: (i, 0)), pl.BlockSpec((tm, LANE), lambda i, e: (i, 0)),
                  pl.BlockSpec((1, d, de), lambda i, e: (e0 + e, 0, 0)),
                  pl.BlockSpec((1, d, de), lambda i, e: (e0 + e, 0, 0)),
                  pl.BlockSpec((1, de, d), lambda i, e: (e0 + e, 0, 0)),
                  pl.BlockSpec((tm, d), lambda i, e: (i, 0)),
                  pl.BlockSpec((1, d), lambda i, e: (0, 0)) if g2.shape[0] == 1
                  else pl.BlockSpec((tm, d), lambda i, e: (i, 0))],
        out_specs=pl.BlockSpec((tm, d), lambda i, e: (i, 0)),
        out_shape=jax.ShapeDtypeStruct((n, d), F32),
        scratch_shapes=[pltpu.VMEM((tm, d), F32)],
        compiler_params=_cparams(("arbitrary", "arbitrary")),
        name="moe_ffn",
    )(h, gate, w1, w3, w2, x, g2)


def _moe_plan(grp, tm, tw, tb, cap, max_entries):
    nt = grp.shape[0] // tm
    cnt = jax.nn.one_hot(grp, N_GROUPS, dtype=jnp.int32).reshape(nt, tm, N_GROUPS).sum(axis=1)
    pc = (cnt + ROW_ALIGN - 1) // ROW_ALIGN * ROW_ALIGN
    segb = jnp.cumsum(pc, axis=1) - pc
    off = jnp.cumsum(pc, axis=0) - pc
    tot = pc.sum(axis=0)
    n_real = (tot + tb - 1) // tb
    n_all = jnp.minimum((tot + tw + tb - 1) // tb, cap // tb)
    ends = jnp.cumsum(n_all)
    s = jnp.arange(max_entries)
    g_of = jnp.sum(s[:, None] >= ends[None, :], axis=1)
    active = g_of < N_GROUPS
    g_c = jnp.minimum(g_of, N_GROUPS - 1)
    rt = s - (ends - n_all)[g_c]
    live = tot[g_c] - rt * tb
    real = jnp.where(live <= tb // 4, 3, jnp.where(live <= tb // 2, 5, 1))
    kind = jnp.where(active, jnp.where(rt < n_real[g_c], real, 2), 0)
    last = ends[-1] - 1
    e_grp = jnp.where(active, g_c, g_c[last])
    e_rt = jnp.where(active, rt, rt[last])
    i32 = lambda a: a.reshape(-1).astype(jnp.int32)
    return i32(segb), i32(off // ROW_ALIGN), i32(e_grp), i32(e_rt), i32(kind)


def _group_perm(gate, segb_ref, base, tm, rows):
    gt = gate.T
    grp = gt[GRP_LANE:GRP_LANE + 1, :]
    gi = lax.broadcasted_iota(jnp.int32, (8, tm), 0).astype(F32)
    oh = jnp.where(gi == grp, 1.0, 0.0)
    r_i = lax.broadcasted_iota(jnp.int32, (tm, tm), 0)
    c_i = lax.broadcasted_iota(jnp.int32, (tm, tm), 1)
    before = jnp.where(r_i < c_i, 1.0, 0.0).astype(BF16)
    rank = lax.dot_general(oh.astype(BF16), before, NN, preferred_element_type=F32)
    dest = jnp.zeros((1, tm), F32)
    for g in range(N_GROUPS):
        dest = dest + oh[g:g + 1] * (segb_ref[base + g].astype(F32) + rank[g:g + 1])
    rows_i = lax.broadcasted_iota(jnp.int32, (rows, tm), 0).astype(F32)
    return jnp.where(rows_i == dest, 1.0, 0.0).astype(BF16)


def _moe_dispatch_body(segb_ref, off_ref, h_ref, gate_ref, xg_in, gg_in, xg_ref, gg_ref, xs_ref, gs_ref, *, tm, tw,
                       rows):
    del off_ref, xg_in, gg_in
    i = pl.program_id(0)
    g = pl.program_id(1)

    @pl.when((i == 0) & (g == 0))
    def _():
        xs_ref[...] = jnp.zeros_like(xs_ref)
        gs_ref[...] = jnp.zeros_like(gs_ref)

    @pl.when(g == 0)
    def _():
        gate = gate_ref[...]
        p = _group_perm(gate, segb_ref, i * N_GROUPS, tm, rows)
        xs_ref[0:rows, :] = lax.dot_general(p, h_ref[...], NN, preferred_element_type=F32).astype(BF16)
        gs_ref[0:rows, :] = _mm01(p, gate)

    start = pl.multiple_of(segb_ref[i * N_GROUPS + g], ROW_ALIGN)
    xg_ref[...] = xs_ref[pl.ds(start, tw), :]
    gg_ref[...] = gs_ref[pl.ds(start, tw), :]


def moe_dispatch(h, gate, segb, off, tm, tw, cap):
    n, d = h.shape
    rows = tm + N_GROUPS * ROW_ALIGN
    win = lambda w: pl.BlockSpec((pl.Element(tw), pl.Element(w)),
                                 lambda i, g, segb, off: ((g * (cap // ROW_ALIGN) + off[i * N_GROUPS + g]) * ROW_ALIGN, 0))
    grid_spec = pltpu.PrefetchScalarGridSpec(
        num_scalar_prefetch=2,
        grid=(n // tm, N_GROUPS),
        in_specs=[pl.BlockSpec((tm, d), lambda i, g, *_: (i, 0)), pl.BlockSpec((tm, LANE), lambda i, g, *_: (i, 0)),
                  pl.BlockSpec(memory_space=pl.ANY), pl.BlockSpec(memory_space=pl.ANY)],
        out_specs=[win(d), win(LANE)],
        scratch_shapes=[pltpu.VMEM((rows + tw, d), BF16), pltpu.VMEM((rows + tw, LANE), F32)],
    )
    return pl.pallas_call(
        functools.partial(_moe_dispatch_body, tm=tm, tw=tw, rows=rows),
        grid_spec=grid_spec,
        out_shape=[jax.ShapeDtypeStruct((N_GROUPS * cap, d), BF16), jax.ShapeDtypeStruct((N_GROUPS * cap, LANE), F32)],
        input_output_aliases={4: 0, 5: 1},
        compiler_params=_cparams(("arbitrary", "arbitrary")),
        name="moe_dispatch",
    )(segb, off, h, gate, jnp.zeros((N_GROUPS * cap, d), BF16), jnp.zeros((N_GROUPS * cap, LANE), F32))


def _moe_group_body(grp_ref, rt_ref, kind_ref, xg_ref, gg_ref, w1_ref, w3_ref, w2_ref, yg_ref, acc_ref):
    del rt_ref
    s = pl.program_id(0)
    e = pl.program_id(1)
    kind = kind_ref[s]
    last = e == pl.num_programs(1) - 1

    def run(rows):
        @pl.when(e == 0)
        def _():
            acc_ref[...] = jnp.zeros_like(acc_ref)

        xb = xg_ref[0:rows, :]
        he = _silu(_mm(xb, w1_ref[0])) * _mm(xb, w3_ref[0])
        y = _mm(he, w2_ref[0])
        gate = gg_ref[0:rows, :]
        lane = lax.broadcasted_iota(jnp.int32, gate.shape, 1)
        ge = jnp.sum(jnp.where(lane == grp_ref[s] * EXP_PER_GROUP + e, gate, 0.0), axis=-1, keepdims=True)
        acc_ref[0:rows, :] += ge * y

        @pl.when(last)
        def _():
            yg_ref[...] = acc_ref[...]

    tb = xg_ref.shape[0]
    for code, rows in ((1, tb), (5, tb // 2), (3, tb // 4)):
        pl.when(kind == code)(functools.partial(run, rows))

    @pl.when((kind == 2) & last)
    def _():
        yg_ref[...] = jnp.zeros_like(yg_ref)


def moe_group_ffn(e_grp, e_rt, e_kind, xg, gg, w1, w3, w2, e0, tb, cap):
    d = xg.shape[1]
    de = w1.shape[2]
    row = lambda s, e, grp, rt, kind: (grp[s] * (cap // tb) + rt[s], 0)
    wsel = lambda s, e, grp, rt, kind: (e0 + grp[s] * EXP_PER_GROUP + jnp.where(kind[s] % 2 == 1, e, EXP_PER_GROUP - 1),
                                        0, 0)
    grid_spec = pltpu.PrefetchScalarGridSpec(
        num_scalar_prefetch=3,
        grid=(e_grp.shape[0], EXP_PER_GROUP),
        in_specs=[pl.BlockSpec((tb, d), row), pl.BlockSpec((tb, LANE), row),
                  pl.BlockSpec((1, d, de), wsel), pl.BlockSpec((1, d, de), wsel), pl.BlockSpec((1, de, d), wsel)],
        out_specs=pl.BlockSpec((tb, d), row),
        scratch_shapes=[pltpu.VMEM((tb, d), F32)],
    )
    return pl.pallas_call(
        _moe_group_body,
        grid_spec=grid_spec,
        out_shape=jax.ShapeDtypeStruct((N_GROUPS * cap, d), F32),
        compiler_params=_cparams(("arbitrary", "arbitrary")),
        name="moe_group_ffn",
    )(e_grp, e_rt, e_kind, xg, gg, w1, w3, w2)


def _moe_combine_body(segb_ref, off_ref, yg_ref, gate_ref, x_ref, g2_ref, o_ref, ys_ref, *, tm, tw, rows):
    del off_ref
    i = pl.program_id(0)
    g = pl.program_id(1)

    @pl.when((i == 0) & (g == 0))
    def _():
        ys_ref[...] = jnp.zeros_like(ys_ref)

    start = pl.multiple_of(segb_ref[i * N_GROUPS + g], ROW_ALIGN)
    ys_ref[pl.ds(start, tw), :] = yg_ref[...]

    @pl.when(g == N_GROUPS - 1)
    def _():
        p = _group_perm(gate_ref[...], segb_ref, i * N_GROUPS, tm, rows)
        yh, yl = _split(ys_ref[0:rows, :])
        y = (lax.dot_general(p, yh, TN, preferred_element_type=F32)
             + lax.dot_general(p, yl, TN, preferred_element_type=F32))
        o_ref[...] = x_ref[...] + g2_ref[...] * y


def moe_combine(yg, gate, x, g2, segb, off, tm, tw, cap):
    n, d = x.shape
    rows = tm + N_GROUPS * ROW_ALIGN
    grid_spec = pltpu.PrefetchScalarGridSpec(
        num_scalar_prefetch=2,
        grid=(n // tm, N_GROUPS),
        in_specs=[pl.BlockSpec((pl.Element(tw), pl.Element(d)),
                               lambda i, g, segb, off: ((g * (cap // ROW_ALIGN) + off[i * N_GROUPS + g]) * ROW_ALIGN, 0)),
                  pl.BlockSpec((tm, LANE), lambda i, g, *_: (i, 0)),
                  pl.BlockSpec((tm, d), lambda i, g, *_: (i, 0)),
                  pl.BlockSpec((1, d), lambda i, g, *_: (0, 0))],
        out_specs=pl.BlockSpec((tm, d), lambda i, g, *_: (i, 0)),
        scratch_shapes=[pltpu.VMEM((rows + tw, d), F32)],
    )
    return pl.pallas_call(
        functools.partial(_moe_combine_body, tm=tm, tw=tw, rows=rows),
        grid_spec=grid_spec,
        out_shape=jax.ShapeDtypeStruct((n, d), F32),
        compiler_params=_cparams(("arbitrary", "arbitrary")),
        name="moe_combine",
    )(segb, off, yg, gate, x, g2)


def moe_grouped(h, gate, w1, w3, w2, e0, x, g2, tm):
    n = h.shape[0]
    tb = tm
    cap = n + 2 * tm
    max_entries = (n + (n // tm) * N_GROUPS * (ROW_ALIGN - 1) + N_GROUPS * tm) // tb + N_GROUPS + 1
    grp = gate[:, GRP_LANE].astype(jnp.int32)

    def run(tw):
        segb, off, e_grp, e_rt, e_kind = _moe_plan(grp, tm, tw, tb, cap, max_entries)
        xg, gg = moe_dispatch(h, gate, segb, off, tm, tw, cap)
        yg = moe_group_ffn(e_grp, e_rt, e_kind, xg, gg, w1, w3, w2, e0, tb, cap)
        return moe_combine(yg, gate, x, g2, segb, off, tm, tw, cap)

    seg_max = jnp.max(jax.nn.one_hot(grp, N_GROUPS, dtype=jnp.int32).reshape(n // tm, tm, N_GROUPS).sum(axis=1))
    return lax.cond(seg_max <= tm // 2, lambda: run(tm // 2), lambda: run(tm))


def _final_norm_body(x_ref, w_ref, o_ref):
    x = x_ref[...]
    o_ref[...] = x * lax.rsqrt(jnp.mean(x * x, axis=-1, keepdims=True) + EPS) * w_ref[...]


def final_norm(x, w, tm):
    n, d = x.shape
    return pl.pallas_call(
        _final_norm_body,
        grid=(n // tm,),
        in_specs=[pl.BlockSpec((tm, d), lambda i: (i, 0)), pl.BlockSpec((1, d), lambda i: (0, 0))],
        out_specs=pl.BlockSpec((tm, d), lambda i: (i, 0)),
        out_shape=jax.ShapeDtypeStruct((n, d), F32),
        compiler_params=_cparams(("arbitrary",)),
        name="final_norm",
    )(x, w)


def _row_tile(n, pref):
    t = min(pref, n)
    while n % t:
        t //= 2
    return t


def kernel(x_prompt, x_sample, c_prompt, c_sample, cache_nsa_cmp, cache_nsa_sel, page_table, state_nsa_win, state_rwkv, state_rwkv_shift, state_gdn, state_gdn_conv, norm_mix, norm_ffn, norm_final, w_ada, b_ada, even_w_in, even_w_out, nsa_cmp_pos, nsa_cmp_w, rwkv_mu, rwkv_w0, rwkv_w2, rwkv_a0, rwkv_a2, rwkv_g2, rwkv_kk, rwkv_ka, rwkv_rk, rwkv_ln_w, rwkv_ln_b, odd_w_in, odd_w_out, gdn_conv_w, gdn_a_log, gdn_dt_bias, gdn_norm_w, moe_w_grp, moe_b_grp, moe_w_exp, moe_b_exp, moe_w1, moe_w3, moe_w2):
    bp, t, d = x_prompt.shape
    bs, ts, _ = x_sample.shape
    assert bp == 1 and ts <= SPAD and ts < CMP_BLK
    depth = norm_mix.shape[0]
    n_pages, page = page_table.shape[1], cache_nsa_cmp.shape[2]
    past = n_pages * page
    wb = state_nsa_win.shape[2]
    ns = bs * SPAD
    tq, tq_s, tk = NSA_TQ, NSA_TQ_SAMPLE, NSA_TK
    tm_p = _row_tile(t, ROW_TILE)
    tm_s = ns

    rows_c = -(-(1 + bs) // 8) * 8
    c_all = jnp.concatenate([c_prompt, c_sample, jnp.zeros((rows_c - 1 - bs, d), F32)], axis=0)
    ada = adaln(c_all, w_ada, b_ada)

    def mods(i):
        mp = [ada[i, 0:1, j * d:(j + 1) * d] for j in range(6)]
        ms = [jnp.repeat(ada[i, 1:1 + bs, j * d:(j + 1) * d], SPAD, axis=0) for j in range(6)]
        return mp, ms

    xp = x_prompt[0]
    xs = jnp.pad(x_sample, ((0, 0), (0, SPAD - ts), (0, 0))).reshape(ns, d)

    def unpad(a):
        return a.reshape(bs, SPAD, -1)[:, :ts]

    w1_all, w3_all, w2_all = (w.reshape((-1,) + w.shape[2:]) for w in (moe_w1, moe_w3, moe_w2))
    outs = {k: [] for k in ("cmp_p", "cmp_s", "sel_p", "sel_s", "win_p", "win_s", "rw_p", "rw_s", "sh_p", "sh_s",
                            "gd_p", "gd_s", "cv_p", "cv_s")}

    for i in range(depth):
        (sh1p, sc1p, gt1p, sh2p, sc2p, gt2p), (sh1s, sc1s, gt1s, sh2s, sc2s, gt2s) = mods(i)
        j = i // 2
        nw = norm_mix[i][None, :]
        if i % 2 == 0:
            w_packed = _pack_even_w(even_w_in[j])
            mu = _pack_rw_vec(rwkv_mu[j])
            wts, wc = _cmp_weights(nsa_cmp_pos[j], nsa_cmp_w[j])
            vec = jnp.stack([rwkv_w0[j], rwkv_a0[j], rwkv_kk[j], rwkv_ka[j], rwkv_ln_w[j], rwkv_ln_b[j],
                             jnp.zeros_like(rwkv_w0[j]), jnp.zeros_like(rwkv_w0[j])])
            pad_lora = lambda w: jnp.concatenate([w, jnp.zeros((128 - w.shape[0], w.shape[1]), w.dtype)], axis=0)
            w2p, a2p, g2p = pad_lora(rwkv_w2[j]), pad_lora(rwkv_a2[j]), rwkv_g2[j]
            hid = jnp.arange(RWKV_W) // RWKV_HD
            seg = (hid[:, None] == hid[None, :]).astype(F32)[:RWKV_W // 2, :RWKV_W // 2]
            rk = rwkv_rk[j].reshape(1, RWKV_W)
            wo_nsa, wo_rw = even_w_out[j][:512].astype(BF16), even_w_out[j][512:].astype(BF16)

            kv, qt, gt, ks, vst, kw, vwt, rw, hl = even_proj(xp, nw, sc1p, sh1p, w_packed, tm_p, 8)
            kvc = compress_prompt(kv, wts, wc, tm_p)
            gates = gt[:24].reshape(NSA_KV_HEADS, 12, t)
            gates = jnp.pad(gates, ((0, 0), (0, 4), (0, 0)))[None]
            o_nsa = nsa_attention(
                qt[None], gates, kvc[None], kvc.T[None], ks[None], vst[None], kw[None], vwt[None],
                tq=tq, tk=tk, wk=WINDOW + tq,
                pos0_fn=lambda qi: qi * tq,
                wstart_fn=lambda qi: jnp.maximum(qi * tq - WINDOW, 0),
                wpos0_fn=lambda qi: jnp.maximum(qi * tq - WINDOW, 0))[0]
            o_rw, s_rw = rwkv_mix(rw, jnp.zeros((1, 8, RW_COLS), F32), jnp.zeros((1, RWKV_HEADS, 64, 64), F32),
                                  mu, vec, w2p, a2p, g2p, seg, rk, c=CHUNK, valid=CHUNK)
            xp = out_proj([o_nsa, o_rw], [wo_nsa, wo_rw], xp, gt1p, tm_p)
            outs["cmp_p"].append(kv[:, 0:256].reshape(1, t, 2, 2, 64))
            outs["sel_p"].append(kv[:, 256:512].reshape(1, t, 2, 2, 64))
            kvw_rows = kv[:, 512:768].reshape(1, t, 2, 2, 64)
            outs["win_p"].append(kvw_rows[:, -min(WINDOW, t):])
            outs["rw_p"].append(s_rw)
            outs["sh_p"].append(hl[-1:])

            kv, qt, gt, _, _, _, _, rw, hl = even_proj(xs, nw, sc1s, sh1s, w_packed, tm_s, ns)
            kv_new = unpad(kv)
            rw0 = small_matmul(jnp.pad(state_rwkv_shift[j], ((0, -bs % 8), (0, 0))), w_packed[:, E_RW:])[:bs]
            rw0 = jnp.pad(rw0[:, None, :], ((0, 0), (7, 0), (0, 0)))
            pool_cmp = cache_nsa_cmp[j].transpose(0, 2, 3, 4, 1).reshape(-1, 256, page)
            pool_sel = cache_nsa_sel[j].transpose(0, 2, 3, 4, 1).reshape(-1, 256, page)
            kvc_s = compress_paged(pool_cmp, page_table, nsa_cmp_pos[j], wc, math.gcd(n_pages, CMP_PAGES_PER_STEP))
            tail = jnp.pad(kv_new[:, :, 256:512], ((0, 0), (0, tk - ts), (0, 0)))
            wbuf = state_nsa_win[j].reshape(bs, wb, 256)
            kvw_all = jnp.concatenate([wbuf, kv_new[:, :, 512:768]], axis=1)
            wk_s = -(-(wb + ts) // 128) * 128
            kvw_pad = jnp.pad(kvw_all, ((0, 0), (0, wk_s - wb - ts), (0, 0)))
            kw_s = kvw_pad[:, :, :128].astype(BF16)
            vwt_s = jnp.swapaxes(kvw_pad[:, :, 128:], 1, 2).astype(BF16)
            qt_s = jnp.pad(qt.reshape(512, bs, SPAD).transpose(1, 0, 2), ((0, 0), (0, 0), (0, tq_s - SPAD)))
            g_s = gt[:24].reshape(NSA_KV_HEADS, 12, bs, SPAD).transpose(2, 0, 1, 3)
            g_s = jnp.pad(g_s, ((0, 0), (0, 0), (0, 4), (0, tq_s - SPAD)))
            o_nsa = nsa_attention_paged(
                qt_s, g_s, kvc_s, jnp.swapaxes(kvc_s, 1, 2), pool_sel, page_table, tail, kw_s, vwt_s,
                tq=tq_s, tk=tk, wk=wk_s,
                pos0_fn=lambda qi: past,
                wstart_fn=lambda qi: 0,
                wpos0_fn=lambda qi: past - wb)
            o_nsa = o_nsa[:, :SPAD].reshape(ns, 512)
            o_rw, s_rw = rwkv_mix(rw, rw0, state_rwkv[j], mu, vec, w2p, a2p, g2p, seg, rk, c=SPAD, valid=ts)
            xs = out_proj([o_nsa, o_rw], [wo_nsa, wo_rw], xs, gt1s, tm_s)
            outs["cmp_s"].append(kv_new[:, :, 0:256].reshape(bs, ts, 2, 2, 64))
            outs["sel_s"].append(kv_new[:, :, 256:512].reshape(bs, ts, 2, 2, 64))
            outs["win_s"].append(kvw_all[:, -wb:].reshape(bs, wb, 2, 2, 64))
            outs["rw_s"].append(s_rw)
            outs["sh_s"].append(hl.reshape(bs, SPAD, d)[:, ts - 1])
        else:
            w_in = odd_w_in[j]
            w_packed = jnp.concatenate([w_in, jnp.zeros((d, O_COLS - w_in.shape[1]), F32)], axis=1).astype(BF16)
            conv_w8 = jnp.pad(gdn_conv_w[j], ((0, 8 - CONV_W), (0, 0)))
            hp = jnp.zeros((8, 128), F32)
            hp = hp.at[0, 8:16].set(-jnp.exp(gdn_a_log[j])).at[1, 8:16].set(gdn_dt_bias[j])
            gnw = gdn_norm_w[j][None, :]
            wo = odd_w_out[j].astype(BF16)

            qkv, z, ba = odd_proj(xp, nw, sc1p, sh1p, w_packed, tm_p)
            o_g, s_g = gdn_mix(qkv, z, ba, jnp.zeros((1, 8, 3 * GDN_W), F32),
                               jnp.zeros((1, GDN_HEADS, GDN_HD, GDN_HD), F32), conv_w8, hp, gnw, c=CHUNK, valid=CHUNK)
            xp = out_proj([o_g], [wo], xp, gt1p, tm_p)
            outs["gd_p"].append(s_g)
            outs["cv_p"].append(qkv[None, -(CONV_W - 1):])

            qkv, z, ba = odd_proj(xs, nw, sc1s, sh1s, w_packed, tm_s)
            cs = jnp.pad(state_gdn_conv[j], ((0, 0), (8 - (CONV_W - 1), 0), (0, 0)))
            o_g, s_g = gdn_mix(qkv, z, ba, cs, state_gdn[j], conv_w8, hp, gnw, c=SPAD, valid=ts)
            xs = out_proj([o_g], [wo], xs, gt1s, tm_s)
            xpad = jnp.concatenate([state_gdn_conv[j], unpad(qkv)], axis=1)
            outs["gd_s"].append(s_g)
            outs["cv_s"].append(xpad[:, -(CONV_W - 1):])

        nwf = norm_ffn[i][None, :]
        w_r = jnp.concatenate([moe_w_exp[i], moe_w_grp[i], jnp.zeros((d, LANE - N_EXPERTS - N_GROUPS), F32)], axis=1)
        b_r = jnp.concatenate([moe_b_exp[i], moe_b_grp[i], jnp.zeros((LANE - N_EXPERTS - N_GROUPS,), F32)])[None, :]
        h2, gate = moe_router(xp, nwf, sc2p, sh2p, w_r, b_r, tm_p)
        xp = moe_grouped(h2, gate, w1_all, w3_all, w2_all, i * N_EXPERTS, xp, gt2p, _row_tile(t, MOE_ROW_TILE))
        h2, gate = moe_router(xs, nwf, sc2s, sh2s, w_r, b_r, tm_s)
        xs = moe_ffn(h2, gate, w1_all, w3_all, w2_all, i * N_EXPERTS, xs, gt2s, tm_s)

    nf = norm_final[None, :]
    y_prompt = final_norm(xp, nf, tm_p)[None]
    y_sample = unpad(final_norm(xs, nf, tm_s))
    st = lambda key: jnp.stack(outs[key])
    return (y_prompt, y_sample, st("cmp_p"), st("cmp_s"), st("sel_p"), st("sel_s"), st("win_p"), st("win_s"),
            st("rw_p"), st("rw_s"), st("sh_p"), st("sh_s"), st("gd_p"), st("gd_s"), st("cv_p"), st("cv_s"))
```

```python
import functools
import math

import jax
import jax.numpy as jnp
from jax import lax
from jax.experimental import pallas as pl
from jax.experimental.pallas import tpu as pltpu

F32 = jnp.float32
BF16 = jnp.bfloat16

NSA_HEADS = 8
NSA_KV_HEADS = 2
NSA_GROUP = 4
NSA_HD = 64
CMP_BLK = 64
SEL_BLK = 64
TOPK_BLK = 16
WINDOW = 512
FORCE_BONUS = 2.0 * NSA_GROUP
RWKV_HEADS = 8
RWKV_HD = 64
RWKV_W = 512
RWKV_GN_EPS = 64e-5
GDN_HEADS = 8
GDN_HD = 128
GDN_W = 1024
CONV_W = 4
N_GROUPS = 4
EXP_PER_GROUP = 8
N_EXPERTS = 32
EPS = 1e-6
NEG = -1e30

LANE = 128
GRP_LANE = 64
ROW_ALIGN = 16
SAMPLE_TILE_SLOTS = 8
SPAD = 8
VMEM_LIMIT = 56 * 1024 * 1024

ROW_TILE = 512
MOE_ROW_TILE = 1024
NSA_TQ = 128
NSA_TQ_SAMPLE = 32
NSA_TK = 512
CHUNK = 64
CMP_PAGES_PER_STEP = 64

NN = (((1,), (0,)), ((), ()))
NT = (((1,), (1,)), ((), ()))
TN = (((0,), (0,)), ((), ()))

E_Q, E_KV, E_G, E_RW = 0, 512, 1280, 1408
E_COLS = 1408 + 1920
RW_COLS = 1920
O_COLS = 3072 + 1024 + 128


def _mm(a, b, dims=NN):
    return lax.dot_general(a.astype(BF16), b.astype(BF16), dims, preferred_element_type=F32)


def _split(a):
    hi = a.astype(BF16)
    return hi, (a - hi.astype(F32)).astype(BF16)


def _mm3(a, b, dims=NN):
    ah, al = _split(a)
    bh, bl = _split(b)
    d = lambda x, y: lax.dot_general(x, y, dims, preferred_element_type=F32)
    return d(ah, bh) + (d(ah, bl) + d(al, bh))


def _split3(x):
    h1 = x.astype(BF16)
    r1 = x - h1.astype(F32)
    h2 = r1.astype(BF16)
    return h1, h2, (r1 - h2.astype(F32)).astype(BF16)


def _mm01(m01, x):
    m = m01.astype(BF16)
    parts = _split3(x)
    d = lambda y: lax.dot_general(m, y, NN, preferred_element_type=F32)
    return d(parts[0]) + (d(parts[1]) + d(parts[2]))


def _cumsum_rows(x):
    n = x.shape[0]
    row = lax.broadcasted_iota(jnp.int32, x.shape, 0)
    shift = 1
    while shift < n:
        x = x + jnp.where(row >= shift, pltpu.roll(x, shift, 0), 0.0)
        shift *= 2
    return x


def _shift_rows(x, prev8, sh):
    rolled = pltpu.roll(x, sh, 0)
    row8 = lax.broadcasted_iota(jnp.int32, (8, 1), 0)
    head = jnp.where(row8 < sh, pltpu.roll(prev8, sh, 0), rolled[0:8])
    return head if x.shape[0] == 8 else jnp.concatenate([head, rolled[8:]], axis=0)


def _head_sums(xs, seg_half):
    r = xs[0].shape[0]
    half = seg_half.shape[0]
    pieces = [p[:, h0:h0 + half] for x in xs for p in _split3(x) for h0 in (0, half)]
    out = lax.dot_general(jnp.concatenate(pieces, axis=0), seg_half.astype(BF16), NN, preferred_element_type=F32)
    res = []
    for i in range(len(xs)):
        o = [out[(6 * i + u) * r:(6 * i + u + 1) * r] for u in range(6)]
        res.append(jnp.concatenate([o[0] + (o[2] + o[4]), o[1] + (o[3] + o[5])], axis=1))
    return res


def _sigmoid(x):
    return 1.0 / (1.0 + jnp.exp(-x))


def _silu(x):
    return x * _sigmoid(x)


def _softplus(x):
    return jnp.maximum(x, 0.0) + jnp.log(1.0 + jnp.exp(-jnp.abs(x)))


def _cparams(sem):
    return pltpu.CompilerParams(dimension_semantics=sem, vmem_limit_bytes=VMEM_LIMIT)


def _norm_mod(x, nw, sc, sh):
    y = x * lax.rsqrt(jnp.mean(x * x, axis=-1, keepdims=True) + EPS)
    return (y * nw) * (1.0 + sc) + sh


def _mod_spec(rows_mod, tm, d):
    if rows_mod == 1:
        return pl.BlockSpec((1, d), lambda i: (0, 0))
    return pl.BlockSpec((tm, d), lambda i: (i, 0))


def _adaln_body(c_ref, w_ref, b_ref, o_ref):
    o_ref[0] = _mm3(_silu(c_ref[...]), w_ref[0]) + b_ref[0]


def adaln(c_all, w_ada, b_ada):
    depth, d, n6 = w_ada.shape
    rows = c_all.shape[0]
    tn = 768
    return pl.pallas_call(
        _adaln_body,
        grid=(depth, n6 // tn),
        in_specs=[pl.BlockSpec((rows, d), lambda l, j: (0, 0)),
                  pl.BlockSpec((1, d, tn), lambda l, j: (l, 0, j)),
                  pl.BlockSpec((1, 1, tn), lambda l, j: (l, 0, j))],
        out_specs=pl.BlockSpec((1, rows, tn), lambda l, j: (l, 0, j)),
        out_shape=jax.ShapeDtypeStruct((depth, rows, n6), F32),
        compiler_params=_cparams(("arbitrary", "arbitrary")),
        name="adaln",
    )(c_all, w_ada, b_ada.reshape(depth, 1, n6))


def _even_proj_body(x_ref, nw_ref, sc_ref, sh_ref, w_ref,
                    kv_ref, qt_ref, gt_ref, ks_ref, vst_ref, kw_ref, vwt_ref, rw_ref, hl_ref):
    h = _norm_mod(x_ref[...], nw_ref[...], sc_ref[...], sh_ref[...])
    hl = hl_ref.shape[0]
    hl_ref[...] = h[h.shape[0] - hl:, :]
    hb = h.astype(BF16)
    q = _mm(hb, w_ref[:, E_Q:E_Q + 512]) * (NSA_HD ** -0.5)
    qt_ref[...] = q.T.astype(BF16)
    kv = _mm(hb, w_ref[:, E_KV:E_KV + 768])
    kv_ref[...] = kv
    ks_ref[...] = kv[:, 256:384].astype(BF16)
    vst_ref[...] = kv[:, 384:512].T.astype(BF16)
    kw_ref[...] = kv[:, 512:640].astype(BF16)
    vwt_ref[...] = kv[:, 640:768].T.astype(BF16)
    g = _sigmoid(_mm(hb, w_ref[:, E_G:E_G + 128]))
    gt_ref[...] = g.T
    rw_ref[...] = _mm(hb, w_ref[:, E_RW:E_RW + RW_COLS])


def even_proj(x, nw, sc, sh, w_packed, tm, hl_rows):
    n, d = x.shape
    rows_mod = sc.shape[0]
    row = lambda c: pl.BlockSpec((tm, c), lambda i: (i, 0))
    col = lambda r: pl.BlockSpec((r, tm), lambda i: (0, i))
    return pl.pallas_call(
        _even_proj_body,
        grid=(n // tm,),
        in_specs=[row(d), pl.BlockSpec((1, d), lambda i: (0, 0)),
                  _mod_spec(rows_mod, tm, d), _mod_spec(rows_mod, tm, d),
                  pl.BlockSpec((d, E_COLS), lambda i: (0, 0))],
        out_specs=[row(768), col(512), col(128), row(128), col(128), row(128), col(128), row(RW_COLS),
                   pl.BlockSpec((hl_rows, d), lambda i: (0, 0))],
        out_shape=[jax.ShapeDtypeStruct((n, 768), F32),
                   jax.ShapeDtypeStruct((512, n), BF16),
                   jax.ShapeDtypeStruct((128, n), F32),
                   jax.ShapeDtypeStruct((n, 128), BF16),
                   jax.ShapeDtypeStruct((128, n), BF16),
                   jax.ShapeDtypeStruct((n, 128), BF16),
                   jax.ShapeDtypeStruct((128, n), BF16),
                   jax.ShapeDtypeStruct((n, RW_COLS), F32),
                   jax.ShapeDtypeStruct((hl_rows, d), F32)],
        compiler_params=_cparams(("arbitrary",)),
        name="even_proj",
    )(x, nw, sc, sh, w_packed)


def _pack_even_w(w_in):
    d = w_in.shape[0]
    z = lambda c: jnp.zeros((d, c), w_in.dtype)
    nsa = 1304
    rw = w_in[:, nsa:]
    parts = [w_in[:, :1280], w_in[:, 1280:1304], z(104),
             rw[:, :1536], rw[:, 1536:1600], z(64), rw[:, 1600:1664], z(64), rw[:, 1664:1792]]
    return jnp.concatenate(parts, axis=1).astype(BF16)


def _pack_rw_vec(v):
    z = jnp.zeros((64,), v.dtype)
    return jnp.concatenate([v[:1536], v[1536:1600], z, v[1600:1664], z, v[1664:1792]])[None, :]


def _mm_body(x_ref, w_ref, o_ref):
    o_ref[...] = _mm(x_ref[...], w_ref[...])


def small_matmul(x, w):
    return pl.pallas_call(
        _mm_body,
        out_shape=jax.ShapeDtypeStruct((x.shape[0], w.shape[1]), F32),
        compiler_params=pltpu.CompilerParams(vmem_limit_bytes=VMEM_LIMIT),
        name="small_matmul",
    )(x, w)


def _compress_body(x_ref, wts_ref, wc_ref, o_ref):
    x = x_ref[...]
    nb = x.shape[0] // CMP_BLK
    pooled = jnp.sum(x.reshape(nb, CMP_BLK, x.shape[-1]) * wts_ref[...][None], axis=1)
    o_ref[...] = _mm(pooled, wc_ref[...])


def _compress_paged_body(pt_ref, *refs, pps):
    page_refs = refs[:pps]
    wp_ref, wc_ref, o_ref = refs[pps:]
    x = jnp.concatenate([r[0] for r in page_refs], axis=1)
    pooled_t = jnp.concatenate([_mm(x[0:128], wp_ref[0]), _mm(x[128:256], wp_ref[1])], axis=0)
    nb = o_ref.shape[1]
    o_ref[0] = _mm(pooled_t.T[:nb], wc_ref[...])


def _cmp_weights(pos_wts, w_c):
    wts = jnp.repeat(pos_wts.T, 128, axis=1)
    eye2 = jnp.eye(2, dtype=w_c.dtype)
    blocks = [jnp.kron(eye2, w_c[c]) for c in range(2)]
    z = jnp.zeros((128, 128), w_c.dtype)
    wc = jnp.concatenate([jnp.concatenate([blocks[0], z], axis=1),
                          jnp.concatenate([z, blocks[1]], axis=1)], axis=0)
    return wts, wc


def compress_prompt(kv, wts, wc, tr):
    t = kv.shape[0]
    nb = tr // CMP_BLK
    return pl.pallas_call(
        _compress_body,
        grid=(t // tr,),
        in_specs=[pl.BlockSpec((tr, 256), lambda i: (i, 0)),
                  pl.BlockSpec((CMP_BLK, 256), lambda i: (0, 0)),
                  pl.BlockSpec((256, 256), lambda i: (0, 0))],
        out_specs=pl.BlockSpec((nb, 256), lambda i: (i, 0)),
        out_shape=jax.ShapeDtypeStruct((t // CMP_BLK, 256), F32),
        compiler_params=_cparams(("arbitrary",)),
        name="compress_prompt",
    )(kv, wts, wc)


def compress_paged(pool_t, page_table, pos_wts, wc, pages_per_step):
    b, n_pages = page_table.shape
    page = pool_t.shape[2]
    pps = pages_per_step
    nb = pps * page // CMP_BLK
    p_idx = jnp.arange(pps * page)
    wp = jax.nn.one_hot(p_idx // CMP_BLK, LANE, dtype=F32)[None] * pos_wts[:, p_idx % CMP_BLK][:, :, None]

    def page_spec(u):
        return pl.BlockSpec((1, 256, page), lambda bi, g, pt: (pt[bi, g * pps + u], 0, 0))

    grid_spec = pltpu.PrefetchScalarGridSpec(
        num_scalar_prefetch=1,
        grid=(b, n_pages // pps),
        in_specs=[page_spec(u) for u in range(pps)] + [
            pl.BlockSpec((2, pps * page, LANE), lambda bi, g, pt: (0, 0, 0)),
            pl.BlockSpec((256, 256), lambda bi, g, pt: (0, 0))],
        out_specs=pl.BlockSpec((1, nb, 256), lambda bi, g, pt: (bi, g, 0)),
    )
    return pl.pallas_call(
        functools.partial(_compress_paged_body, pps=pps),
        grid_spec=grid_spec,
        out_shape=jax.ShapeDtypeStruct((b, n_pages * page // CMP_BLK, 256), F32),
        compiler_params=_cparams(("arbitrary", "arbitrary")),
        name="compress_paged",
    )(page_table, *([pool_t] * pps), wp, wc)


def _gather_sel_body(pt_ref, tiles_ref, cnt_ref, *refs, pps, n_page_steps, nt):
    del pt_ref
    page_refs = refs[:pps]
    tail_ref, ks_ref, vst_ref = refs[pps:]
    bi = pl.program_id(0)
    a = pl.program_id(1)
    j = tiles_ref[bi * nt + jnp.minimum(a, cnt_ref[bi] - 1)]
    live = a < cnt_ref[bi]

    @pl.when(live & (j < n_page_steps))
    def _():
        ks_ref[0] = jnp.concatenate([r[0][0:128].T for r in page_refs], axis=0).astype(BF16)
        vst_ref[0] = jnp.concatenate([r[0][128:256] for r in page_refs], axis=1).astype(BF16)

    @pl.when(live & (j >= n_page_steps))
    def _():
        x = tail_ref[0]
        ks_ref[0] = x[:, :128].astype(BF16)
        vst_ref[0] = x[:, 128:].T.astype(BF16)


def gather_sel(pool_t, page_table, tail, tk, tiles, cnt, n_slots):
    b, n_pages = page_table.shape
    page = pool_t.shape[2]
    pps = tk // page
    n_page_steps = n_pages // pps
    nt = n_page_steps + 1
    nk = n_slots * tk

    def slot(bi, a, pt, tiles, cnt):
        return jnp.minimum(a, cnt[bi] - 1)

    def page_spec(u):
        def index(bi, a, pt, tiles, cnt):
            j = tiles[bi * nt + slot(bi, a, pt, tiles, cnt)]
            return (pt[bi, jnp.minimum(j * pps + u, n_pages - 1)], 0, 0)
        return pl.BlockSpec((1, 256, page), index)

    grid_spec = pltpu.PrefetchScalarGridSpec(
        num_scalar_prefetch=3,
        grid=(b, jnp.max(cnt)),
        in_specs=[page_spec(u) for u in range(pps)] + [pl.BlockSpec((1, tk, 256), lambda bi, a, *_: (bi, 0, 0))],
        out_specs=[pl.BlockSpec((1, tk, 128), lambda bi, a, *s: (bi, slot(bi, a, *s), 0)),
                   pl.BlockSpec((1, 128, tk), lambda bi, a, *s: (bi, 0, slot(bi, a, *s)))],
    )
    return pl.pallas_call(
        functools.partial(_gather_sel_body, pps=pps, n_page_steps=n_page_steps, nt=nt),
        grid_spec=grid_spec,
        out_shape=[jax.ShapeDtypeStruct((b, nk, 128), BF16), jax.ShapeDtypeStruct((b, 128, nk), BF16)],
        compiler_params=_cparams(("arbitrary", "arbitrary")),
        name="gather_sel",
    )(page_table, tiles, cnt, *([pool_t] * pps), tail)


MASKED = -1e30
M_INIT = -1e29


def _nsa_query(qt_ref, k, tq):
    w4 = NSA_GROUP * tq
    qb = qt_ref[0].astype(F32)
    qcat = jnp.concatenate([qb[g * 64:(g + 1) * 64] for g in range(NSA_GROUP)], axis=1)
    q2 = jnp.concatenate([qcat, qcat], axis=0)
    row = lax.broadcasted_iota(jnp.int32, (128, w4), 0)
    qe = jnp.where(row // 64 == k, q2, 0.0)
    gidx = lax.broadcasted_iota(jnp.int32, (128, w4), 1) // tq
    base = jnp.where(k == 0, 0.5, 0.5 / 16.0)
    slope = base * jnp.where(gidx == 0, 1.0, jnp.where(gidx == 1, 0.5, jnp.where(gidx == 2, 0.25, 0.125)))
    mult = jnp.where(row == 0, 16.0, jnp.where(row == 1, 1.0, jnp.where(row == 2, 128.0,
                                                                         jnp.where(row == 3, 64.0, 0.0))))
    return jnp.concatenate([qe, slope * mult], axis=0).astype(BF16)


def _pos_features(rows, tile_rel):
    r = lax.broadcasted_iota(jnp.int32, (rows, LANE), 0)
    lane = lax.broadcasted_iota(jnp.int32, (rows, LANE), 1)
    ab = jnp.where(lane == 0, r // 16, jnp.where(lane == 1, r % 16, 0)).astype(F32)
    return jnp.where(lane == 2, tile_rel, ab).astype(BF16)


def _gate_rows(gb, j, tq):
    return jnp.concatenate([gb[g * 3 + j:g * 3 + j + 1, :] for g in range(NSA_GROUP)], axis=1)


def _nsa_select_body(qt_ref, g_ref, kvc_ref, kvct_ref, kw_ref, vwt_ref, part_ref, sel_ref, flag_ref, *,
                     tq, tk, wk, nbc, nb, pos0_fn, wstart_fn, wpos0_fn):
    i = pl.program_id(1)
    k = pl.program_id(2)
    w4 = NSA_GROUP * tq
    pos0 = pos0_fn(i)
    qa = _nsa_query(qt_ref, k, tq)
    pos_q = pos0 + lax.broadcasted_iota(jnp.int32, (1, w4), 1) % tq

    def softmax_cols(s, bad):
        s = jnp.where(bad, MASKED, s)
        m = jnp.maximum(jnp.max(s, axis=0, keepdims=True), M_INIT)
        e = jnp.exp(s - m)
        return e / jnp.maximum(jnp.sum(e, axis=0, keepdims=True), 1e-30)

    n_i = lax.broadcasted_iota(jnp.int32, (nbc, LANE), 0)
    lane_c = lax.broadcasted_iota(jnp.int32, (nbc, LANE), 1)
    feat_c = jnp.where(lane_c == 3, n_i - pos0 // CMP_BLK, 0).astype(F32).astype(BF16)
    kc = jnp.concatenate([kvc_ref[0][:, :128].astype(BF16), feat_c], axis=1)
    c_end = lax.broadcasted_iota(jnp.int32, (nbc, 1), 0) * CMP_BLK + (CMP_BLK - 1)
    p_c = softmax_cols(lax.dot_general(kc, qa, NN, preferred_element_type=F32), c_end > pos_q)
    vct = kvct_ref[0, pl.ds(pl.multiple_of(128 + k * 64, 64), 64), :]
    o_c = _mm(vct, p_c)

    imp = p_c[:, 0:tq]
    for g in range(1, NSA_GROUP):
        imp = imp + p_c[:, g * tq:(g + 1) * tq]
    if nb > nbc:
        imp = jnp.concatenate([imp, jnp.zeros((nb - nbc, tq), F32)], axis=0)
    blk = lax.broadcasted_iota(jnp.int32, (nb, tq), 0)
    cur = (pos0 + lax.broadcasted_iota(jnp.int32, (1, tq), 1)) // SEL_BLK
    forced = (blk == cur) | (blk == cur - 1) | (blk == 0)
    score = jnp.where(blk <= cur, imp + jnp.where(forced, FORCE_BONUS, 0.0), -1.0)
    for _ in range(min(TOPK_BLK, nb)):
        m = jnp.max(score, axis=0, keepdims=True)
        first = jnp.min(jnp.where(score == m, blk, nb), axis=0, keepdims=True)
        score = jnp.where(blk == first, -2.0, score)
    sel = jnp.where(score == -2.0, 1.0, 0.0)
    sel_ref[0, 0] = sel
    bpt = tk // SEL_BLK
    any_row = jnp.max(sel, axis=1, keepdims=True)
    flag_ref[0, 0] = jnp.max(any_row.reshape(nb // bpt, bpt, 1), axis=1)

    wstart = wstart_fn(i)
    if not isinstance(wstart, int):
        wstart = pl.multiple_of(wstart, 128)
    wpos0 = wpos0_fn(i)
    tile_rel = jnp.asarray((wpos0 - pos0) // 128, F32)
    kw = jnp.concatenate([kw_ref[0, pl.ds(wstart, wk), :], _pos_features(wk, tile_rel)], axis=1)
    dist_w = pos_q - (wpos0 + lax.broadcasted_iota(jnp.int32, (wk, 1), 0))
    p_w = softmax_cols(lax.dot_general(kw, qa, NN, preferred_element_type=F32), (dist_w < 0) | (dist_w >= WINDOW))
    vwin = vwt_ref[0, pl.ds(pl.multiple_of(k * 64, 64), 64), pl.ds(wstart, wk)]
    o_w = _mm(vwin, p_w)

    gb = g_ref[0, 0]
    part_ref[0, 0] = _gate_rows(gb, 0, tq) * o_c + _gate_rows(gb, 2, tq) * o_w


def nsa_select(qt, gates, kvc, kvct, kw, vwt, *, nb, tq, tk, wk, pos0_fn, wstart_fn, wpos0_fn):
    b, _, nq = qt.shape
    nbc = kvc.shape[1]
    nw = kw.shape[1]
    nqt = nq // tq
    nt = nb * SEL_BLK // tk
    w4 = NSA_GROUP * tq
    assert nbc <= 256 and tk <= 512 and wk <= 1024
    body = functools.partial(_nsa_select_body, tq=tq, tk=tk, wk=wk, nbc=nbc, nb=nb, pos0_fn=pos0_fn,
                             wstart_fn=wstart_fn, wpos0_fn=wpos0_fn)
    full = lambda s1, s2: pl.BlockSpec((1, s1, s2), lambda bi, i, k: (bi, 0, 0))
    step = lambda s1, s2: pl.BlockSpec((1, 1, s1, s2), lambda bi, i, k: (bi, i * NSA_KV_HEADS + k, 0, 0))
    return pl.pallas_call(
        body,
        grid=(b, nqt, NSA_KV_HEADS),
        in_specs=[pl.BlockSpec((1, 256, tq), lambda bi, i, k: (bi, k, i)),
                  pl.BlockSpec((1, 1, 16, tq), lambda bi, i, k: (bi, k, 0, i)),
                  full(nbc, 256), full(256, nbc), full(nw, 128), full(128, nw)],
        out_specs=[step(64, w4), step(nb, tq), step(nt, 1)],
        out_shape=[jax.ShapeDtypeStruct((b, nqt * 2, 64, w4), F32),
                   jax.ShapeDtypeStruct((b, nqt * 2, nb, tq), F32),
                   jax.ShapeDtypeStruct((b, nqt * 2, nt, 1), F32)],
        compiler_params=_cparams(("arbitrary", "arbitrary", "arbitrary")),
        name="nsa_select",
    )(qt, gates, kvc, kvct, kw, vwt)


def _nsa_selected_body(list_ref, slot_ref, cnt_ref, qt_ref, g_ref, sel_ref, ks_ref, vst_ref, part_ref, o_ref, *,
                       tq, tk, nt, pos0_fn):
    bi = pl.program_id(0)
    i = pl.program_id(1)
    k = pl.program_id(2)
    step = (bi * pl.num_programs(1) + i) * NSA_KV_HEADS + k
    w4 = NSA_GROUP * tq
    pos0 = pos0_fn(i)
    qa = _nsa_query(qt_ref, k, tq)
    pos_q = pos0 + lax.broadcasted_iota(jnp.int32, (1, w4), 1) % tq
    bpt = tk // SEL_BLK
    row_k = lax.broadcasted_iota(jnp.int32, (tk, 1), 0)
    r = lax.broadcasted_iota(jnp.int32, (tk, LANE), 0)
    lane = lax.broadcasted_iota(jnp.int32, (tk, LANE), 1)
    feat_ab = jnp.where(lane == 0, r // 16, jnp.where(lane == 1, r % 16, 0)).astype(F32)

    n_act = cnt_ref[step]

    def tile_scores(jj, live):
        j = list_ref[step * nt + jj]
        off = pl.multiple_of(j * tk, tk)
        buf = pl.multiple_of(slot_ref[step * nt + jj] * tk, tk)
        tile_rel = ((off - pos0) // 128).astype(F32)
        feat = jnp.where(lane == 2, tile_rel, feat_ab).astype(BF16)
        kj = jnp.concatenate([ks_ref[0, pl.ds(buf, tk), :], feat], axis=1)
        s = lax.dot_general(kj, qa, NN, preferred_element_type=F32)
        selb = (sel_ref[0, 0, pl.ds(pl.multiple_of(j * bpt, bpt), bpt), :] - 1.0) * (-MASKED)
        selb = jnp.concatenate([selb] * NSA_GROUP, axis=1) + jnp.where(live, 0.0, MASKED)
        s = s + jnp.broadcast_to(selb[:, None, :], (bpt, SEL_BLK, w4)).reshape(tk, w4)
        s = jnp.where(row_k > pos_q - off, MASKED, s)
        return s, vst_ref[0, pl.ds(pl.multiple_of(k * 64, 64), 64), pl.ds(buf, tk)]

    def kv_pair(pp, carry):
        m_i, l_i, acc = carry
        second = 2 * pp + 1
        s_a, v_a = tile_scores(2 * pp, True)
        s_b, v_b = tile_scores(jnp.minimum(second, n_act - 1), second < n_act)
        m_new = jnp.maximum(m_i, jnp.maximum(jnp.max(s_a, axis=0, keepdims=True), jnp.max(s_b, axis=0, keepdims=True)))
        p_a = jnp.exp(s_a - m_new)
        p_b = jnp.exp(s_b - m_new)
        alpha = jnp.exp(m_i - m_new)
        l_new = l_i * alpha + (jnp.sum(p_a, axis=0, keepdims=True) + jnp.sum(p_b, axis=0, keepdims=True))
        return m_new, l_new, acc * alpha + (_mm(v_a, p_a) + _mm(v_b, p_b))

    init = (jnp.full((1, w4), M_INIT, F32), jnp.zeros((1, w4), F32), jnp.zeros((64, w4), F32))
    _, l_s, acc_s = lax.fori_loop(0, (n_act + 1) // 2, kv_pair, init)
    o_s = acc_s / jnp.maximum(l_s, 1e-30)
    o_t = part_ref[0, 0] + _gate_rows(g_ref[0, 0], 1, tq) * o_s
    o_ref[0] = jnp.concatenate([o_t[:, g * tq:(g + 1) * tq].T for g in range(NSA_GROUP)], axis=1)


def nsa_selected(tile_list, slot_list, tile_cnt, qt, gates, sel, ks, vst, part, *, tq, tk, pos0_fn):
    b, _, nq = qt.shape
    nk = ks.shape[1]
    nb = sel.shape[2]
    nt = nb * SEL_BLK // tk
    w4 = NSA_GROUP * tq
    full = lambda s1, s2: pl.BlockSpec((1, s1, s2), lambda bi, i, k, *_: (bi, 0, 0))
    step = lambda s1, s2: pl.BlockSpec((1, 1, s1, s2), lambda bi, i, k, *_: (bi, i * NSA_KV_HEADS + k, 0, 0))
    grid_spec = pltpu.PrefetchScalarGridSpec(
        num_scalar_prefetch=3,
        grid=(b, nq // tq, NSA_KV_HEADS),
        in_specs=[pl.BlockSpec((1, 256, tq), lambda bi, i, k, *_: (bi, k, i)),
                  pl.BlockSpec((1, 1, 16, tq), lambda bi, i, k, *_: (bi, k, 0, i)),
                  step(nb, tq), full(nk, 128), full(128, nk), step(64, w4)],
        out_specs=pl.BlockSpec((1, tq, 256), lambda bi, i, k, *_: (bi, i, k)),
    )
    return pl.pallas_call(
        functools.partial(_nsa_selected_body, tq=tq, tk=tk, nt=nt, pos0_fn=pos0_fn),
        grid_spec=grid_spec,
        out_shape=jax.ShapeDtypeStruct((b, nq, 512), F32),
        compiler_params=_cparams(("arbitrary", "arbitrary", "arbitrary")),
        name="nsa_selected",
    )(tile_list, slot_list, tile_cnt, qt, gates, sel, ks, vst, part)


def _active_first(active):
    order = jnp.argsort(jnp.where(active, 0, 1), axis=-1, stable=True).astype(jnp.int32)
    return order, jnp.sum(active, axis=-1).astype(jnp.int32)


def nsa_attention(qt, gates, kvc, kvct, ks, vst, kw, vwt, *, tq, tk, wk, pos0_fn, wstart_fn, wpos0_fn):
    nb = ks.shape[1] // SEL_BLK
    part, sel, flags = nsa_select(qt, gates, kvc, kvct, kw, vwt, nb=nb, tq=tq, tk=tk, wk=wk, pos0_fn=pos0_fn,
                                  wstart_fn=wstart_fn, wpos0_fn=wpos0_fn)
    order, cnt = _active_first(flags[..., 0] > 0.5)
    return nsa_selected(order.reshape(-1), order.reshape(-1), cnt.reshape(-1), qt, gates, sel, ks, vst, part,
                        tq=tq, tk=tk, pos0_fn=pos0_fn)


def nsa_attention_paged(qt, gates, kvc, kvct, pool_t, page_table, tail, kw, vwt, *, tq, tk, wk, pos0_fn, wstart_fn,
                        wpos0_fn):
    nb = (page_table.shape[1] * pool_t.shape[2] + tk) // SEL_BLK
    part, sel, flags = nsa_select(qt, gates, kvc, kvct, kw, vwt, nb=nb, tq=tq, tk=tk, wk=wk, pos0_fn=pos0_fn,
                                  wstart_fn=wstart_fn, wpos0_fn=wpos0_fn)
    active = flags[..., 0] > 0.5
    tiles_b, cnt_b = _active_first(jnp.any(active, axis=1))
    slot_of_tile = jnp.argsort(tiles_b, axis=-1).astype(jnp.int32)
    order, cnt = _active_first(active)
    slots = jnp.take_along_axis(jnp.broadcast_to(slot_of_tile[:, None, :], order.shape), order, axis=-1)

    def run(n_slots):
        ks, vst = gather_sel(pool_t, page_table, tail, tk, tiles_b.reshape(-1), cnt_b, n_slots)
        return nsa_selected(order.reshape(-1), slots.reshape(-1), cnt.reshape(-1), qt, gates, sel, ks, vst, part,
                            tq=tq, tk=tk, pos0_fn=pos0_fn)

    nt = tiles_b.shape[-1]
    few = min(SAMPLE_TILE_SLOTS, nt)
    return lax.cond(jnp.max(cnt_b) <= few, lambda: run(few), lambda: run(nt))


def _tri_inverse(ms, c):
    eye = (lax.broadcasted_iota(jnp.int32, (c, c), 0) == lax.broadcasted_iota(jnp.int32, (c, c), 1)).astype(F32)
    ps = [-m for m in ms]
    ts = [eye + p for p in ps]
    steps = max(int(math.ceil(math.log2(c))) - 1, 0)
    d = lambda x, y: lax.dot_general(x, y, NN, preferred_element_type=F32)
    for _ in range(steps):
        sp = [_split(p) for p in ps]
        ps = [d(ph, ph) + (d(ph, pl_) + d(pl_, ph)) for ph, pl_ in sp]
        sp = [_split(p) for p in ps]
        st = [_split(t) for t in ts]
        ts = [t + (d(th, ph) + (d(th, pl_) + d(tl, ph))) for t, (th, tl), (ph, pl_) in zip(ts, st, sp)]
    return ts


def _rwkv_body(rw_ref, rw0_ref, s0_ref, mu_ref, vec_ref, w2_ref, a2_ref, g2_ref, seg_ref, rk_ref,
               o_ref, sfin_ref, buf_ref, s_ref, y_ref, *, c, valid, n_chunks):
    ci = pl.program_id(1)
    halo = 8

    @pl.when(ci == 0)
    def _():
        buf_ref[...] = rw0_ref[0]
        s_ref[...] = s0_ref[0]

    cur = rw_ref[...]
    prev = _shift_rows(cur, buf_ref[...], 1)
    xr = cur + (prev - cur) * mu_ref[...]
    buf_ref[...] = cur[c - halo:, :]

    vec = vec_ref[...]
    w0, a0, kkw, kaw, ln_w, ln_b = (vec[r:r + 1, :] for r in range(6))
    r = xr[:, 0:512]
    kx = xr[:, 512:1024]
    v = xr[:, 1024:1536]
    xw = xr[:, 1536:1664]
    xa = xr[:, 1664:1792]
    xg = xr[:, 1792:1920]
    wl = -jnp.exp(-_softplus(-(w0 + _mm(jnp.tanh(xw), w2_ref[...]))) - 0.5)
    a = _sigmoid(a0 + _mm(xa, a2_ref[...]))
    gate = _mm(_sigmoid(xg), g2_ref[...])
    seg = seg_ref[...]
    zk = kx * kkw
    k2 = kx * (1.0 + (a - 1.0) * kaw)
    zz_sum, rk_sum = _head_sums([zk * zk, r * k2 * rk_ref[...]], seg)
    kk = zk * lax.rsqrt(zz_sum + EPS)
    bonus = rk_sum * v
    if valid < c:
        live = lax.broadcasted_iota(jnp.int32, (c, 1), 0) < valid
        wl = jnp.where(live, wl, 0.0)
        kk = jnp.where(live, kk, 0.0)
        k2 = jnp.where(live, k2, 0.0)
        v = jnp.where(live, v, 0.0)
        r = jnp.where(live, r, 0.0)
    bb = kk * a

    ri = lax.broadcasted_iota(jnp.int32, (c, c), 0)
    cj = lax.broadcasted_iota(jnp.int32, (c, c), 1)
    tril = ri >= cj
    strict = ri > cj
    cw = _cumsum_rows(wl)
    ecw = jnp.exp(cw)
    einv = jnp.exp(-cw)
    p_c = ecw[c - 1:c, :]
    kt = kk * jnp.exp(cw - wl)
    bt = bb * einv
    ki = k2 * einv
    rt = r * ecw
    bd = bt * p_c
    kd = ki * p_c

    heads = range(RWKV_HEADS)
    sls = [slice(h * RWKV_HD, (h + 1) * RWKV_HD) for h in heads]
    kt_h = [kt[:, sl] for sl in sls]
    bt_h = [bt[:, sl] for sl in sls]
    ki_h = [ki[:, sl] for sl in sls]
    rt_h = [rt[:, sl] for sl in sls]
    v_h = [v[:, sl] for sl in sls]
    l_m = [jnp.where(strict, _mm3(kt_h[h], bt_h[h], NT), 0.0) for h in heads]
    m_kk = [jnp.where(strict, _mm(kt_h[h], ki_h[h], NT), 0.0) for h in heads]
    a_rb = [jnp.where(tril, _mm(rt_h[h], bt_h[h], NT), 0.0) for h in heads]
    a_rk = [jnp.where(tril, _mm(rt_h[h], ki_h[h], NT), 0.0) for h in heads]
    mv = [_mm(m_kk[h], v_h[h]) for h in heads]
    y0 = [_mm(a_rk[h], v_h[h]) for h in heads]
    t_inv = _tri_inverse(l_m, c)
    w_h = [_mm3(t_inv[h], kt_h[h]) for h in heads]
    u_h = [-_mm3(t_inv[h], mv[h]) for h in heads]
    s_h = [s_ref[h] for h in heads]
    e_h = [u_h[h] - _mm(w_h[h], s_h[h], NT) for h in heads]
    y1 = [_mm(rt_h[h], s_h[h], NT) + y0[h] for h in heads]
    y_h = [y1[h] + _mm(a_rb[h], e_h[h]) for h in heads]
    ds = [_mm(e_h[h], bd[:, sls[h]], TN) + _mm(v_h[h], kd[:, sls[h]], TN) for h in heads]
    for h in heads:
        s_ref[h] = s_h[h] * p_c[:, sls[h]] + ds[h]
        mu_h = jnp.mean(y_h[h], axis=-1, keepdims=True)
        d_h = y_h[h] - mu_h
        var_h = jnp.mean(d_h * d_h, axis=-1, keepdims=True)
        y_ref[:, sls[h]] = d_h * lax.rsqrt(var_h + RWKV_GN_EPS)

    o_ref[...] = (y_ref[...] * ln_w + ln_b + bonus) * gate

    @pl.when(ci == n_chunks - 1)
    def _():
        sfin_ref[0] = s_ref[...]


def rwkv_mix(rw, rw0, s0, mu, vec, w2, a2, g2, seg, rk, *, c, valid):
    b = s0.shape[0]
    rows = rw.shape[0]
    n_chunks = rows // (b * c)
    const = lambda s: pl.BlockSpec(s, lambda bi, ci: tuple(0 for _ in s))
    return pl.pallas_call(
        functools.partial(_rwkv_body, c=c, valid=valid, n_chunks=n_chunks),
        grid=(b, n_chunks),
        in_specs=[pl.BlockSpec((c, RW_COLS), lambda bi, ci: (bi * n_chunks + ci, 0)),
                  pl.BlockSpec((1, 8, RW_COLS), lambda bi, ci: (bi, 0, 0)),
                  pl.BlockSpec((1, RWKV_HEADS, 64, 64), lambda bi, ci: (bi, 0, 0, 0)),
                  const((1, RW_COLS)), const((8, 512)), const((128, 512)), const((128, 512)), const((128, 512)),
                  const((RWKV_W // 2, RWKV_W // 2)), const((1, 512))],
        out_specs=[pl.BlockSpec((c, 512), lambda bi, ci: (bi * n_chunks + ci, 0)),
                   pl.BlockSpec((1, RWKV_HEADS, 64, 64), lambda bi, ci: (bi, 0, 0, 0))],
        out_shape=[jax.ShapeDtypeStruct((rows, 512), F32),
                   jax.ShapeDtypeStruct((b, RWKV_HEADS, 64, 64), F32)],
        scratch_shapes=[pltpu.VMEM((8, RW_COLS), F32), pltpu.VMEM((RWKV_HEADS, 64, 64), F32),
                        pltpu.VMEM((c, 512), F32)],
        compiler_params=_cparams(("arbitrary", "arbitrary")),
        name="rwkv_mix",
    )(rw, rw0, s0, mu, vec, w2, a2, g2, seg, rk)


def _out_proj_body(*refs, n_in):
    a_refs = refs[:n_in]
    w_refs = refs[n_in:2 * n_in]
    x_ref, g_ref, o_ref = refs[2 * n_in:]
    y = _mm(a_refs[0][...], w_refs[0][...])
    for a_ref, w_ref in zip(a_refs[1:], w_refs[1:]):
        y = y + _mm(a_ref[...], w_ref[...])
    o_ref[...] = x_ref[...] + g_ref[...] * y


def out_proj(acts, weights, x, gate, tm):
    n, d = x.shape
    n_in = len(acts)
    return pl.pallas_call(
        functools.partial(_out_proj_body, n_in=n_in),
        grid=(n // tm,),
        in_specs=[pl.BlockSpec((tm, a.shape[1]), lambda i: (i, 0)) for a in acts]
        + [pl.BlockSpec(w.shape, lambda i: (0, 0)) for w in weights]
        + [pl.BlockSpec((tm, d), lambda i: (i, 0)), _mod_spec(gate.shape[0], tm, d)],
        out_specs=pl.BlockSpec((tm, d), lambda i: (i, 0)),
        out_shape=jax.ShapeDtypeStruct((n, d), F32),
        compiler_params=_cparams(("arbitrary",)),
        name="out_proj",
    )(*acts, *weights, x, gate)


def _odd_proj_body(x_ref, nw_ref, sc_ref, sh_ref, w_ref, qkv_ref, z_ref, ba_ref):
    hb = _norm_mod(x_ref[...], nw_ref[...], sc_ref[...], sh_ref[...]).astype(BF16)
    qkv_ref[...] = _mm(hb, w_ref[:, 0:3072])
    z_ref[...] = _mm(hb, w_ref[:, 3072:4096])
    ba_ref[...] = _mm(hb, w_ref[:, 4096:O_COLS])


def odd_proj(x, nw, sc, sh, w_packed, tm):
    n, d = x.shape
    rows_mod = sc.shape[0]
    row = lambda c: pl.BlockSpec((tm, c), lambda i: (i, 0))
    return pl.pallas_call(
        _odd_proj_body,
        grid=(n // tm,),
        in_specs=[row(d), pl.BlockSpec((1, d), lambda i: (0, 0)),
                  _mod_spec(rows_mod, tm, d), _mod_spec(rows_mod, tm, d),
                  pl.BlockSpec((d, O_COLS), lambda i: (0, 0))],
        out_specs=[row(3072), row(1024), row(128)],
        out_shape=[jax.ShapeDtypeStruct((n, 3072), F32), jax.ShapeDtypeStruct((n, 1024), F32),
                   jax.ShapeDtypeStruct((n, 128), F32)],
        compiler_params=_cparams(("arbitrary",)),
        name="odd_proj",
    )(x, nw, sc, sh, w_packed)


def _gdn_body(qkv_ref, z_ref, ba_ref, cs_ref, s0_ref, cw_ref, hp_ref, nw_ref,
              o_ref, sfin_ref, buf_ref, s_ref, *, c, valid, n_chunks):
    ci = pl.program_id(1)
    halo = 8

    @pl.when(ci == 0)
    def _():
        buf_ref[...] = cs_ref[0]
        s_ref[...] = s0_ref[0]

    x = qkv_ref[...]
    prev8 = buf_ref[...]
    cw = cw_ref[...]
    conv = x * cw[CONV_W - 1:CONV_W, :]
    for j in range(CONV_W - 1):
        conv = conv + _shift_rows(x, prev8, CONV_W - 1 - j) * cw[j:j + 1, :]
    buf_ref[...] = x[c - halo:, :]
    conv = _silu(conv)

    hp = hp_ref[...]
    ba = ba_ref[...]
    beta_f = _sigmoid(ba)
    g_f = hp[0:1, :] * _softplus(ba + hp[1:2, :])
    if valid < c:
        live = lax.broadcasted_iota(jnp.int32, (c, 1), 0) < valid
        beta_f = jnp.where(live, beta_f, 0.0)
        g_f = jnp.where(live, g_f, 0.0)
        conv = jnp.where(live, conv, 0.0)

    ri = lax.broadcasted_iota(jnp.int32, (c, c), 0)
    cj = lax.broadcasted_iota(jnp.int32, (c, c), 1)
    tril = ri >= cj
    strict = ri > cj
    gc = _cumsum_rows(g_f)
    gct = gc.T
    z = z_ref[...]
    nw = nw_ref[...]

    heads = range(GDN_HEADS)
    sls = [slice(h * GDN_HD, (h + 1) * GDN_HD) for h in heads]
    q_h = [conv[:, sl] for sl in sls]
    k_h = [conv[:, GDN_W + h * GDN_HD:GDN_W + (h + 1) * GDN_HD] for h in heads]
    v_h = [conv[:, 2 * GDN_W + h * GDN_HD:2 * GDN_W + (h + 1) * GDN_HD] for h in heads]
    q_h = [q * lax.rsqrt(jnp.sum(q * q, axis=-1, keepdims=True) + EPS) * (GDN_HD ** -0.5) for q in q_h]
    k_h = [k * lax.rsqrt(jnp.sum(k * k, axis=-1, keepdims=True) + EPS) for k in k_h]
    g_col = [gc[:, 8 + h:9 + h] for h in heads]
    eg = [jnp.exp(g) for g in g_col]
    b_col = [beta_f[:, h:h + 1] for h in heads]
    decay = [jnp.where(tril, jnp.exp(jnp.where(tril, g_col[h] - gct[8 + h:9 + h, :], 0.0)), 0.0) for h in heads]
    kb = [k_h[h] * b_col[h] for h in heads]
    vb = [v_h[h] * b_col[h] for h in heads]
    m_h = [jnp.where(strict, _mm3(kb[h], k_h[h], NT) * decay[h], 0.0) for h in heads]
    qk = [jnp.where(tril, _mm(q_h[h], k_h[h], NT) * decay[h], 0.0) for h in heads]
    t_inv = _tri_inverse(m_h, c)
    u_h = [_mm(t_inv[h], vb[h]) for h in heads]
    w_h = [_mm(t_inv[h], kb[h] * eg[h]) for h in heads]
    s_h = [s_ref[h] for h in heads]
    v_new = [u_h[h] - _mm(w_h[h], s_h[h]) for h in heads]
    o1 = [_mm(q_h[h] * eg[h], s_h[h]) for h in heads]
    o_h = [o1[h] + _mm(qk[h], v_new[h]) for h in heads]
    g_last = [g[c - 1:c, :] for g in g_col]
    ds = [_mm(k_h[h] * jnp.exp(g_last[h] - g_col[h]), v_new[h], TN) for h in heads]
    for h in heads:
        s_ref[h] = s_h[h] * jnp.exp(g_last[h]) + ds[h]
        o_n = o_h[h] * lax.rsqrt(jnp.mean(o_h[h] * o_h[h], axis=-1, keepdims=True) + EPS) * nw
        o_ref[:, sls[h]] = o_n * _silu(z[:, sls[h]])

    @pl.when(ci == n_chunks - 1)
    def _():
        sfin_ref[0] = s_ref[...]


def gdn_mix(qkv, z, ba, cs, s0, conv_w8, hp, nw, *, c, valid):
    b = s0.shape[0]
    rows = qkv.shape[0]
    n_chunks = rows // (b * c)
    const = lambda s: pl.BlockSpec(s, lambda bi, ci: tuple(0 for _ in s))
    row = lambda w: pl.BlockSpec((c, w), lambda bi, ci: (bi * n_chunks + ci, 0))
    return pl.pallas_call(
        functools.partial(_gdn_body, c=c, valid=valid, n_chunks=n_chunks),
        grid=(b, n_chunks),
        in_specs=[row(3072), row(1024), row(128),
                  pl.BlockSpec((1, 8, 3072), lambda bi, ci: (bi, 0, 0)),
                  pl.BlockSpec((1, GDN_HEADS, 128, 128), lambda bi, ci: (bi, 0, 0, 0)),
                  const((8, 3072)), const((8, 128)), const((1, 128))],
        out_specs=[row(1024), pl.BlockSpec((1, GDN_HEADS, 128, 128), lambda bi, ci: (bi, 0, 0, 0))],
        out_shape=[jax.ShapeDtypeStruct((rows, 1024), F32),
                   jax.ShapeDtypeStruct((b, GDN_HEADS, 128, 128), F32)],
        scratch_shapes=[pltpu.VMEM((8, 3072), F32), pltpu.VMEM((GDN_HEADS, 128, 128), F32)],
        compiler_params=_cparams(("arbitrary", "arbitrary")),
        name="gdn_mix",
    )(qkv, z, ba, cs, s0, conv_w8, hp, nw)


def _router_body(x_ref, nw_ref, sc_ref, sh_ref, wr_ref, br_ref, h_ref, gate_ref):
    h = _norm_mod(x_ref[...], nw_ref[...], sc_ref[...], sh_ref[...])
    h_ref[...] = h.astype(BF16)
    logits = _mm3(h, wr_ref[...]) + br_ref[...]
    tm = logits.shape[0]
    lane = lax.broadcasted_iota(jnp.int32, (tm, LANE), 1)
    is_grp = (lane >= N_EXPERTS) & (lane < N_EXPERTS + N_GROUPS)
    gl = jnp.where(is_grp, logits, NEG)
    gmax = jnp.max(gl, axis=-1, keepdims=True)
    g_idx = jnp.min(jnp.where(gl == gmax, lane, 4 * LANE), axis=-1, keepdims=True) - N_EXPERTS
    g_w = 1.0 / jnp.sum(jnp.where(is_grp, jnp.exp(gl - gmax), 0.0), axis=-1, keepdims=True)
    in_grp = (lane < N_EXPERTS) & (lane // EXP_PER_GROUP == g_idx)
    el = jnp.where(in_grp, logits, NEG)
    emax = jnp.max(el, axis=-1, keepdims=True)
    e = jnp.where(in_grp, jnp.exp(el - emax), 0.0)
    p = e / jnp.sum(e, axis=-1, keepdims=True)
    p1 = jnp.where(in_grp, p, -1.0)
    m1 = jnp.max(p1, axis=-1, keepdims=True)
    i1 = jnp.min(jnp.where(p1 == m1, lane, 4 * LANE), axis=-1, keepdims=True)
    p2 = jnp.where(lane == i1, -1.0, p1)
    m2 = jnp.max(p2, axis=-1, keepdims=True)
    i2 = jnp.min(jnp.where(p2 == m2, lane, 4 * LANE), axis=-1, keepdims=True)
    tot = m1 + m2
    gate = jnp.where(lane == i1, m1 / tot * g_w, jnp.where(lane == i2, m2 / tot * g_w, 0.0))
    gate_ref[...] = jnp.where(lane == GRP_LANE, g_idx.astype(F32), gate)


def moe_router(x, nw, sc, sh, w_r, b_r, tm):
    n, d = x.shape
    rows_mod = sc.shape[0]
    return pl.pallas_call(
        _router_body,
        grid=(n // tm,),
        in_specs=[pl.BlockSpec((tm, d), lambda i: (i, 0)), pl.BlockSpec((1, d), lambda i: (0, 0)),
                  _mod_spec(rows_mod, tm, d), _mod_spec(rows_mod, tm, d),
                  pl.BlockSpec((d, LANE), lambda i: (0, 0)), pl.BlockSpec((1, LANE), lambda i: (0, 0))],
        out_specs=[pl.BlockSpec((tm, d), lambda i: (i, 0)), pl.BlockSpec((tm, LANE), lambda i: (i, 0))],
        out_shape=[jax.ShapeDtypeStruct((n, d), BF16), jax.ShapeDtypeStruct((n, LANE), F32)],
        compiler_params=_cparams(("arbitrary",)),
        name="moe_router",
    )(x, nw, sc, sh, w_r, b_r)


def _moe_body(h_ref, gate_ref, w1_ref, w3_ref, w2_ref, x_ref, g2_ref, o_ref, acc_ref):
    e = pl.program_id(1)

    @pl.when(e == 0)
    def _():
        acc_ref[...] = jnp.zeros_like(acc_ref)

    hb = h_ref[...]
    he = _silu(_mm(hb, w1_ref[0])) * _mm(hb, w3_ref[0])
    y = _mm(he, w2_ref[0])
    gate = gate_ref[...]
    lane = lax.broadcasted_iota(jnp.int32, gate.shape, 1)
    ge = jnp.sum(jnp.where(lane == e, gate, 0.0), axis=-1, keepdims=True)
    acc_ref[...] += ge * y

    @pl.when(e == pl.num_programs(1) - 1)
    def _():
        o_ref[...] = x_ref[...] + g2_ref[...] * acc_ref[...]


def moe_ffn(h, gate, w1, w3, w2, e0, x, g2, tm):
    n, d = x.shape
    de = w1.shape[2]
    return pl.pallas_call(
        _moe_body,
        grid=(n // tm, N_EXPERTS),
        in_specs=[pl.BlockSpec((tm, d), lambda i, e: (i, 0)), pl.BlockSpec((tm, LANE), lambda i, e: (i, 0)),
                  pl.BlockSpec((1, d, de), lambda i, e: (e0 + e, 0, 0)),
                  pl.BlockSpec((1, d, de), lambda i, e: (e0 + e, 0, 0)),
                  pl.BlockSpec((1, de, d), lambda i, e: (e0 + e, 0, 0)),
                  pl.BlockSpec((tm, d), lambda i, e: (i, 0)),
                  pl.BlockSpec((1, d), lambda i, e: (0, 0)) if g2.shape[0] == 1
                  else pl.BlockSpec((tm, d), lambda i, e: (i, 0))],
        out_specs=pl.BlockSpec((tm, d), lambda i, e: (i, 0)),
        out_shape=jax.ShapeDtypeStruct((n, d), F32),
        scratch_shapes=[pltpu.VMEM((tm, d), F32)],
        compiler_params=_cparams(("arbitrary", "arbitrary")),
        name="moe_ffn",
    )(h, gate, w1, w3, w2, x, g2)


def _moe_plan(grp, tm, tw, tb, cap, max_entries):
    nt = grp.shape[0] // tm
    cnt = jax.nn.one_hot(grp, N_GROUPS, dtype=jnp.int32).reshape(nt, tm, N_GROUPS).sum(axis=1)
    pc = (cnt + ROW_ALIGN - 1) // ROW_ALIGN * ROW_ALIGN
    segb = jnp.cumsum(pc, axis=1) - pc
    off = jnp.cumsum(pc, axis=0) - pc
    tot = pc.sum(axis=0)
    n_real = (tot + tb - 1) // tb
    n_all = jnp.minimum((tot + tw + tb - 1) // tb, cap // tb)
    ends = jnp.cumsum(n_all)
    s = jnp.arange(max_entries)
    g_of = jnp.sum(s[:, None] >= ends[None, :], axis=1)
    active = g_of < N_GROUPS
    g_c = jnp.minimum(g_of, N_GROUPS - 1)
    rt = s - (ends - n_all)[g_c]
    live = tot[g_c] - rt * tb
    real = jnp.where(live <= tb // 4, 3, jnp.where(live <= tb // 2, 5, 1))
    kind = jnp.where(active, jnp.where(rt < n_real[g_c], real, 2), 0)
    last = ends[-1] - 1
    e_grp = jnp.where(active, g_c, g_c[last])
    e_rt = jnp.where(active, rt, rt[last])
    i32 = lambda a: a.reshape(-1).astype(jnp.int32)
    return i32(segb), i32(off // ROW_ALIGN), i32(e_grp), i32(e_rt), i32(kind)


def _group_perm(gate, segb_ref, base, tm, rows):
    gt = gate.T
    grp = gt[GRP_LANE:GRP_LANE + 1, :]
    gi = lax.broadcasted_iota(jnp.int32, (8, tm), 0).astype(F32)
    oh = jnp.where(gi == grp, 1.0, 0.0)
    r_i = lax.broadcasted_iota(jnp.int32, (tm, tm), 0)
    c_i = lax.broadcasted_iota(jnp.int32, (tm, tm), 1)
    before = jnp.where(r_i < c_i, 1.0, 0.0).astype(BF16)
    rank = lax.dot_general(oh.astype(BF16), before, NN, preferred_element_type=F32)
    dest = jnp.zeros((1, tm), F32)
    for g in range(N_GROUPS):
        dest = dest + oh[g:g + 1] * (segb_ref[base + g].astype(F32) + rank[g:g + 1])
    rows_i = lax.broadcasted_iota(jnp.int32, (rows, tm), 0).astype(F32)
    return jnp.where(rows_i == dest, 1.0, 0.0).astype(BF16)


def _moe_dispatch_body(segb_ref, off_ref, h_ref, gate_ref, xg_in, gg_in, xg_ref, gg_ref, xs_ref, gs_ref, *, tm, tw,
                       rows):
    del off_ref, xg_in, gg_in
    i = pl.program_id(0)
    g = pl.program_id(1)

    @pl.when((i == 0) & (g == 0))
    def _():
        xs_ref[...] = jnp.zeros_like(xs_ref)
        gs_ref[...] = jnp.zeros_like(gs_ref)

    @pl.when(g == 0)
    def _():
        gate = gate_ref[...]
        p = _group_perm(gate, segb_ref, i * N_GROUPS, tm, rows)
        xs_ref[0:rows, :] = lax.dot_general(p, h_ref[...], NN, preferred_element_type=F32).astype(BF16)
        gs_ref[0:rows, :] = _mm01(p, gate)

    start = pl.multiple_of(segb_ref[i * N_GROUPS + g], ROW_ALIGN)
    xg_ref[...] = xs_ref[pl.ds(start, tw), :]
    gg_ref[...] = gs_ref[pl.ds(start, tw), :]


def moe_dispatch(h, gate, segb, off, tm, tw, cap):
    n, d = h.shape
    rows = tm + N_GROUPS * ROW_ALIGN
    win = lambda w: pl.BlockSpec((pl.Element(tw), pl.Element(w)),
                                 lambda i, g, segb, off: ((g * (cap // ROW_ALIGN) + off[i * N_GROUPS + g]) * ROW_ALIGN, 0))
    grid_spec = pltpu.PrefetchScalarGridSpec(
        num_scalar_prefetch=2,
        grid=(n // tm, N_GROUPS),
        in_specs=[pl.BlockSpec((tm, d), lambda i, g, *_: (i, 0)), pl.BlockSpec((tm, LANE), lambda i, g, *_: (i, 0)),
                  pl.BlockSpec(memory_space=pl.ANY), pl.BlockSpec(memory_space=pl.ANY)],
        out_specs=[win(d), win(LANE)],
        scratch_shapes=[pltpu.VMEM((rows + tw, d), BF16), pltpu.VMEM((rows + tw, LANE), F32)],
    )
    return pl.pallas_call(
        functools.partial(_moe_dispatch_body, tm=tm, tw=tw, rows=rows),
        grid_spec=grid_spec,
        out_shape=[jax.ShapeDtypeStruct((N_GROUPS * cap, d), BF16), jax.ShapeDtypeStruct((N_GROUPS * cap, LANE), F32)],
        input_output_aliases={4: 0, 5: 1},
        compiler_params=_cparams(("arbitrary", "arbitrary")),
        name="moe_dispatch",
    )(segb, off, h, gate, jnp.zeros((N_GROUPS * cap, d), BF16), jnp.zeros((N_GROUPS * cap, LANE), F32))


def _moe_group_body(grp_ref, rt_ref, kind_ref, xg_ref, gg_ref, w1_ref, w3_ref, w2_ref, yg_ref, acc_ref):
    del rt_ref
    s = pl.program_id(0)
    e = pl.program_id(1)
    kind = kind_ref[s]
    last = e == pl.num_programs(1) - 1

    def run(rows):
        @pl.when(e == 0)
        def _():
            acc_ref[...] = jnp.zeros_like(acc_ref)

        xb = xg_ref[0:rows, :]
        he = _silu(_mm(xb, w1_ref[0])) * _mm(xb, w3_ref[0])
        y = _mm(he, w2_ref[0])
        gate = gg_ref[0:rows, :]
        lane = lax.broadcasted_iota(jnp.int32, gate.shape, 1)
        ge = jnp.sum(jnp.where(lane == grp_ref[s] * EXP_PER_GROUP + e, gate, 0.0), axis=-1, keepdims=True)
        acc_ref[0:rows, :] += ge * y

        @pl.when(last)
        def _():
            yg_ref[...] = acc_ref[...]

    tb = xg_ref.shape[0]
    for code, rows in ((1, tb), (5, tb // 2), (3, tb // 4)):
        pl.when(kind == code)(functools.partial(run, rows))

    @pl.when((kind == 2) & last)
    def _():
        yg_ref[...] = jnp.zeros_like(yg_ref)


def moe_group_ffn(e_grp, e_rt, e_kind, xg, gg, w1, w3, w2, e0, tb, cap):
    d = xg.shape[1]
    de = w1.shape[2]
    row = lambda s, e, grp, rt, kind: (grp[s] * (cap // tb) + rt[s], 0)
    wsel = lambda s, e, grp, rt, kind: (e0 + grp[s] * EXP_PER_GROUP + jnp.where(kind[s] % 2 == 1, e, EXP_PER_GROUP - 1),
                                        0, 0)
    grid_spec = pltpu.PrefetchScalarGridSpec(
        num_scalar_prefetch=3,
        grid=(e_grp.shape[0], EXP_PER_GROUP),
        in_specs=[pl.BlockSpec((tb, d), row), pl.BlockSpec((tb, LANE), row),
                  pl.BlockSpec((1, d, de), wsel), pl.BlockSpec((1, d, de), wsel), pl.BlockSpec((1, de, d), wsel)],
        out_specs=pl.BlockSpec((tb, d), row),
        scratch_shapes=[pltpu.VMEM((tb, d), F32)],
    )
    return pl.pallas_call(
        _moe_group_body,
        grid_spec=grid_spec,
        out_shape=jax.ShapeDtypeStruct((N_GROUPS * cap, d), F32),
        compiler_params=_cparams(("arbitrary", "arbitrary")),
        name="moe_group_ffn",
    )(e_grp, e_rt, e_kind, xg, gg, w1, w3, w2)


def _moe_combine_body(segb_ref, off_ref, yg_ref, gate_ref, x_ref, g2_ref, fw_ref, o_ref, ys_ref, *, tm, tw, rows,
                      final):
    del off_ref
    i = pl.program_id(0)
    g = pl.program_id(1)

    @pl.when((i == 0) & (g == 0))
    def _():
        ys_ref[...] = jnp.zeros_like(ys_ref)

    start = pl.multiple_of(segb_ref[i * N_GROUPS + g], ROW_ALIGN)
    ys_ref[pl.ds(start, tw), :] = yg_ref[...]

    @pl.when(g == N_GROUPS - 1)
    def _():
        p = _group_perm(gate_ref[...], segb_ref, i * N_GROUPS, tm, rows)
        yh, yl = _split(ys_ref[0:rows, :])
        y = (lax.dot_general(p, yh, TN, preferred_element_type=F32)
             + lax.dot_general(p, yl, TN, preferred_element_type=F32))
        r = x_ref[...] + g2_ref[...] * y
        if final:
            r = r * lax.rsqrt(jnp.mean(r * r, axis=-1, keepdims=True) + EPS) * fw_ref[...]
        o_ref[...] = r


def moe_combine(yg, gate, x, g2, fw, segb, off, tm, tw, cap, final):
    n, d = x.shape
    rows = tm + N_GROUPS * ROW_ALIGN
    grid_spec = pltpu.PrefetchScalarGridSpec(
        num_scalar_prefetch=2,
        grid=(n // tm, N_GROUPS),
        in_specs=[pl.BlockSpec((pl.Element(tw), pl.Element(d)),
                               lambda i, g, segb, off: ((g * (cap // ROW_ALIGN) + off[i * N_GROUPS + g]) * ROW_ALIGN, 0)),
                  pl.BlockSpec((tm, LANE), lambda i, g, *_: (i, 0)),
                  pl.BlockSpec((tm, d), lambda i, g, *_: (i, 0)),
                  pl.BlockSpec((1, d), lambda i, g, *_: (0, 0)),
                  pl.BlockSpec((1, d), lambda i, g, *_: (0, 0))],
        out_specs=pl.BlockSpec((tm, d), lambda i, g, *_: (i, 0)),
        scratch_shapes=[pltpu.VMEM((rows + tw, d), F32)],
    )
    return pl.pallas_call(
        functools.partial(_moe_combine_body, tm=tm, tw=tw, rows=rows, final=final),
        grid_spec=grid_spec,
        out_shape=jax.ShapeDtypeStruct((n, d), F32),
        compiler_params=_cparams(("arbitrary", "arbitrary")),
        name="moe_combine",
    )(segb, off, yg, gate, x, g2, fw)


def moe_grouped(h, gate, w1, w3, w2, e0, x, g2, tm, fw, final):
    n = h.shape[0]
    tb = tm
    cap = n + 2 * tm
    max_entries = (n + (n // tm) * N_GROUPS * (ROW_ALIGN - 1) + N_GROUPS * tm) // tb + N_GROUPS + 1
    grp = gate[:, GRP_LANE].astype(jnp.int32)

    def run(tw):
        segb, off, e_grp, e_rt, e_kind = _moe_plan(grp, tm, tw, tb, cap, max_entries)
        xg, gg = moe_dispatch(h, gate, segb, off, tm, tw, cap)
        yg = moe_group_ffn(e_grp, e_rt, e_kind, xg, gg, w1, w3, w2, e0, tb, cap)
        return moe_combine(yg, gate, x, g2, fw, segb, off, tm, tw, cap, final)

    seg_max = jnp.max(jax.nn.one_hot(grp, N_GROUPS, dtype=jnp.int32).reshape(n // tm, tm, N_GROUPS).sum(axis=1))
    return lax.cond(seg_max <= tm // 2, lambda: run(tm // 2), lambda: run(tm))


def _final_norm_body(x_ref, w_ref, o_ref):
    x = x_ref[...]
    o_ref[...] = x * lax.rsqrt(jnp.mean(x * x, axis=-1, keepdims=True) + EPS) * w_ref[...]


def final_norm(x, w, tm):
    n, d = x.shape
    return pl.pallas_call(
        _final_norm_body,
        grid=(n // tm,),
        in_specs=[pl.BlockSpec((tm, d), lambda i: (i, 0)), pl.BlockSpec((1, d), lambda i: (0, 0))],
        out_specs=pl.BlockSpec((tm, d), lambda i: (i, 0)),
        out_shape=jax.ShapeDtypeStruct((n, d), F32),
        compiler_params=_cparams(("arbitrary",)),
        name="final_norm",
    )(x, w)


def _row_tile(n, pref):
    t = min(pref, n)
    while n % t:
        t //= 2
    return t


def kernel(x_prompt, x_sample, c_prompt, c_sample, cache_nsa_cmp, cache_nsa_sel, page_table, state_nsa_win, state_rwkv, state_rwkv_shift, state_gdn, state_gdn_conv, norm_mix, norm_ffn, norm_final, w_ada, b_ada, even_w_in, even_w_out, nsa_cmp_pos, nsa_cmp_w, rwkv_mu, rwkv_w0, rwkv_w2, rwkv_a0, rwkv_a2, rwkv_g2, rwkv_kk, rwkv_ka, rwkv_rk, rwkv_ln_w, rwkv_ln_b, odd_w_in, odd_w_out, gdn_conv_w, gdn_a_log, gdn_dt_bias, gdn_norm_w, moe_w_grp, moe_b_grp, moe_w_exp, moe_b_exp, moe_w1, moe_w3, moe_w2):
    bp, t, d = x_prompt.shape
    bs, ts, _ = x_sample.shape
    assert bp == 1 and ts <= SPAD and ts < CMP_BLK
    depth = norm_mix.shape[0]
    n_pages, page = page_table.shape[1], cache_nsa_cmp.shape[2]
    past = n_pages * page
    wb = state_nsa_win.shape[2]
    ns = bs * SPAD
    tq, tq_s, tk = NSA_TQ, NSA_TQ_SAMPLE, NSA_TK
    tm_p = _row_tile(t, ROW_TILE)
    tm_s = ns

    rows_c = -(-(1 + bs) // 8) * 8
    c_all = jnp.concatenate([c_prompt, c_sample, jnp.zeros((rows_c - 1 - bs, d), F32)], axis=0)
    ada = adaln(c_all, w_ada, b_ada)

    def mods(i):
        mp = [ada[i, 0:1, j * d:(j + 1) * d] for j in range(6)]
        ms = [jnp.repeat(ada[i, 1:1 + bs, j * d:(j + 1) * d], SPAD, axis=0) for j in range(6)]
        return mp, ms

    xp = x_prompt[0]
    xs = jnp.pad(x_sample, ((0, 0), (0, SPAD - ts), (0, 0))).reshape(ns, d)

    def unpad(a):
        return a.reshape(bs, SPAD, -1)[:, :ts]

    w1_all, w3_all, w2_all = (w.reshape((-1,) + w.shape[2:]) for w in (moe_w1, moe_w3, moe_w2))
    outs = {k: [] for k in ("cmp_p", "cmp_s", "sel_p", "sel_s", "win_p", "win_s", "rw_p", "rw_s", "sh_p", "sh_s",
                            "gd_p", "gd_s", "cv_p", "cv_s")}

    for i in range(depth):
        (sh1p, sc1p, gt1p, sh2p, sc2p, gt2p), (sh1s, sc1s, gt1s, sh2s, sc2s, gt2s) = mods(i)
        j = i // 2
        nw = norm_mix[i][None, :]
        if i % 2 == 0:
            w_packed = _pack_even_w(even_w_in[j])
            mu = _pack_rw_vec(rwkv_mu[j])
            wts, wc = _cmp_weights(nsa_cmp_pos[j], nsa_cmp_w[j])
            vec = jnp.stack([rwkv_w0[j], rwkv_a0[j], rwkv_kk[j], rwkv_ka[j], rwkv_ln_w[j], rwkv_ln_b[j],
                             jnp.zeros_like(rwkv_w0[j]), jnp.zeros_like(rwkv_w0[j])])
            pad_lora = lambda w: jnp.concatenate([w, jnp.zeros((128 - w.shape[0], w.shape[1]), w.dtype)], axis=0)
            w2p, a2p, g2p = pad_lora(rwkv_w2[j]), pad_lora(rwkv_a2[j]), rwkv_g2[j]
            hid = jnp.arange(RWKV_W) // RWKV_HD
            seg = (hid[:, None] == hid[None, :]).astype(F32)[:RWKV_W // 2, :RWKV_W // 2]
            rk = rwkv_rk[j].reshape(1, RWKV_W)
            wo_nsa, wo_rw = even_w_out[j][:512].astype(BF16), even_w_out[j][512:].astype(BF16)

            kv, qt, gt, ks, vst, kw, vwt, rw, hl = even_proj(xp, nw, sc1p, sh1p, w_packed, tm_p, 8)
            kvc = compress_prompt(kv, wts, wc, tm_p)
            gates = gt[:24].reshape(NSA_KV_HEADS, 12, t)
            gates = jnp.pad(gates, ((0, 0), (0, 4), (0, 0)))[None]
            o_nsa = nsa_attention(
                qt[None], gates, kvc[None], kvc.T[None], ks[None], vst[None], kw[None], vwt[None],
                tq=tq, tk=tk, wk=WINDOW + tq,
                pos0_fn=lambda qi: qi * tq,
                wstart_fn=lambda qi: jnp.maximum(qi * tq - WINDOW, 0),
                wpos0_fn=lambda qi: jnp.maximum(qi * tq - WINDOW, 0))[0]
            o_rw, s_rw = rwkv_mix(rw, jnp.zeros((1, 8, RW_COLS), F32), jnp.zeros((1, RWKV_HEADS, 64, 64), F32),
                                  mu, vec, w2p, a2p, g2p, seg, rk, c=CHUNK, valid=CHUNK)
            xp = out_proj([o_nsa, o_rw], [wo_nsa, wo_rw], xp, gt1p, tm_p)
            outs["cmp_p"].append(kv[:, 0:256].reshape(1, t, 2, 2, 64))
            outs["sel_p"].append(kv[:, 256:512].reshape(1, t, 2, 2, 64))
            kvw_rows = kv[:, 512:768].reshape(1, t, 2, 2, 64)
            outs["win_p"].append(kvw_rows[:, -min(WINDOW, t):])
            outs["rw_p"].append(s_rw)
            outs["sh_p"].append(hl[-1:])

            kv, qt, gt, _, _, _, _, rw, hl = even_proj(xs, nw, sc1s, sh1s, w_packed, tm_s, ns)
            kv_new = unpad(kv)
            rw0 = small_matmul(jnp.pad(state_rwkv_shift[j], ((0, -bs % 8), (0, 0))), w_packed[:, E_RW:])[:bs]
            rw0 = jnp.pad(rw0[:, None, :], ((0, 0), (7, 0), (0, 0)))
            pool_cmp = cache_nsa_cmp[j].transpose(0, 2, 3, 4, 1).reshape(-1, 256, page)
            pool_sel = cache_nsa_sel[j].transpose(0, 2, 3, 4, 1).reshape(-1, 256, page)
            kvc_s = compress_paged(pool_cmp, page_table, nsa_cmp_pos[j], wc, math.gcd(n_pages, CMP_PAGES_PER_STEP))
            tail = jnp.pad(kv_new[:, :, 256:512], ((0, 0), (0, tk - ts), (0, 0)))
            wbuf = state_nsa_win[j].reshape(bs, wb, 256)
            kvw_all = jnp.concatenate([wbuf, kv_new[:, :, 512:768]], axis=1)
            wk_s = -(-(wb + ts) // 128) * 128
            kvw_pad = jnp.pad(kvw_all, ((0, 0), (0, wk_s - wb - ts), (0, 0)))
            kw_s = kvw_pad[:, :, :128].astype(BF16)
            vwt_s = jnp.swapaxes(kvw_pad[:, :, 128:], 1, 2).astype(BF16)
            qt_s = jnp.pad(qt.reshape(512, bs, SPAD).transpose(1, 0, 2), ((0, 0), (0, 0), (0, tq_s - SPAD)))
            g_s = gt[:24].reshape(NSA_KV_HEADS, 12, bs, SPAD).transpose(2, 0, 1, 3)
            g_s = jnp.pad(g_s, ((0, 0), (0, 0), (0, 4), (0, tq_s - SPAD)))
            o_nsa = nsa_attention_paged(
                qt_s, g_s, kvc_s, jnp.swapaxes(kvc_s, 1, 2), pool_sel, page_table, tail, kw_s, vwt_s,
                tq=tq_s, tk=tk, wk=wk_s,
                pos0_fn=lambda qi: past,
                wstart_fn=lambda qi: 0,
                wpos0_fn=lambda qi: past - wb)
            o_nsa = o_nsa[:, :SPAD].reshape(ns, 512)
            o_rw, s_rw = rwkv_mix(rw, rw0, state_rwkv[j], mu, vec, w2p, a2p, g2p, seg, rk, c=SPAD, valid=ts)
            xs = out_proj([o_nsa, o_rw], [wo_nsa, wo_rw], xs, gt1s, tm_s)
            outs["cmp_s"].append(kv_new[:, :, 0:256].reshape(bs, ts, 2, 2, 64))
            outs["sel_s"].append(kv_new[:, :, 256:512].reshape(bs, ts, 2, 2, 64))
            outs["win_s"].append(kvw_all[:, -wb:].reshape(bs, wb, 2, 2, 64))
            outs["rw_s"].append(s_rw)
            outs["sh_s"].append(hl.reshape(bs, SPAD, d)[:, ts - 1])
        else:
            w_in = odd_w_in[j]
            w_packed = jnp.concatenate([w_in, jnp.zeros((d, O_COLS - w_in.shape[1]), F32)], axis=1).astype(BF16)
            conv_w8 = jnp.pad(gdn_conv_w[j], ((0, 8 - CONV_W), (0, 0)))
            hp = jnp.zeros((8, 128), F32)
            hp = hp.at[0, 8:16].set(-jnp.exp(gdn_a_log[j])).at[1, 8:16].set(gdn_dt_bias[j])
            gnw = gdn_norm_w[j][None, :]
            wo = odd_w_out[j].astype(BF16)

            qkv, z, ba = odd_proj(xp, nw, sc1p, sh1p, w_packed, tm_p)
            o_g, s_g = gdn_mix(qkv, z, ba, jnp.zeros((1, 8, 3 * GDN_W), F32),
                               jnp.zeros((1, GDN_HEADS, GDN_HD, GDN_HD), F32), conv_w8, hp, gnw, c=CHUNK, valid=CHUNK)
            xp = out_proj([o_g], [wo], xp, gt1p, tm_p)
            outs["gd_p"].append(s_g)
            outs["cv_p"].append(qkv[None, -(CONV_W - 1):])

            qkv, z, ba = odd_proj(xs, nw, sc1s, sh1s, w_packed, tm_s)
            cs = jnp.pad(state_gdn_conv[j], ((0, 0), (8 - (CONV_W - 1), 0), (0, 0)))
            o_g, s_g = gdn_mix(qkv, z, ba, cs, state_gdn[j], conv_w8, hp, gnw, c=SPAD, valid=ts)
            xs = out_proj([o_g], [wo], xs, gt1s, tm_s)
            xpad = jnp.concatenate([state_gdn_conv[j], unpad(qkv)], axis=1)
            outs["gd_s"].append(s_g)
            outs["cv_s"].append(xpad[:, -(CONV_W - 1):])

        nwf = norm_ffn[i][None, :]
        w_r = jnp.concatenate([moe_w_exp[i], moe_w_grp[i], jnp.zeros((d, LANE - N_EXPERTS - N_GROUPS), F32)], axis=1)
        b_r = jnp.concatenate([moe_b_exp[i], moe_b_grp[i], jnp.zeros((LANE - N_EXPERTS - N_GROUPS,), F32)])[None, :]
        h2, gate = moe_router(xp, nwf, sc2p, sh2p, w_r, b_r, tm_p)
        xp = moe_grouped(h2, gate, w1_all, w3_all, w2_all, i * N_EXPERTS, xp, gt2p, _row_tile(t, MOE_ROW_TILE),
                         norm_final[None, :], i == depth - 1)
        h2, gate = moe_router(xs, nwf, sc2s, sh2s, w_r, b_r, tm_s)
        xs = moe_ffn(h2, gate, w1_all, w3_all, w2_all, i * N_EXPERTS, xs, gt2s, tm_s)

    nf = norm_final[None, :]
    y_prompt = xp[None]
    y_sample = unpad(final_norm(xs, nf, tm_s))
    st = lambda key: jnp.stack(outs[key])
    return (y_prompt, y_sample, st("cmp_p"), st("cmp_s"), st("sel_p"), st("sel_s"), st("win_p"), st("win_s"),
            st("rw_p"), st("rw_s"), st("sh_p"), st("sh_s"), st("gd_p"), st("gd_s"), st("cv_p"), st("cv_s"))
```
